```python
import math
import jax
import jax.numpy as jnp
from jax import lax
import numpy as np

D_MODEL = 2048
BATCH = 16
SEQ = 256
DEPTH = 2
DEC_BATCH = 8
DEC_SEQ = 1024
PAST_LEN = 512

GRID_W = 64
HEAD_DIM = 128
N_MIXERS = 4
GROUP_WIDTH = D_MODEL // N_MIXERS
RET_HEADS = GROUP_WIDTH // HEAD_DIM
GDN_HEADS = GROUP_WIDTH // HEAD_DIM
HG_HEADS = GROUP_WIDTH // HEAD_DIM
S5_CH = 16
S5_GROUPS = GROUP_WIDTH // S5_CH
S5_N = 64
GDN_CONV = 3
CHUNK = 64
HG_CHUNK = 16
ROPE_BASE = 10000.0
FFN_HIDDEN = -(-8 * D_MODEL // (3 * 256)) * 256
N_MOD = 6
EPS = 1e-6
LB_FLOOR = 1e-30
PROJ_WIDTH = 14 * GROUP_WIDTH + 4 * GDN_HEADS

kernel_name = 'hybrid_bidir_flow_trunk_step'


def _rmsnorm(x, w):
    xf = x.astype(jnp.float32)
    y = xf * lax.rsqrt(jnp.mean(xf * xf, axis=-1, keepdims=True) + EPS)
    return (y * w.astype(jnp.float32)).astype(x.dtype)


def _head_layernorm(x):
    mu = jnp.mean(x, axis=-1, keepdims=True)
    xc = x - mu
    return xc * lax.rsqrt(jnp.mean(xc * xc, axis=-1, keepdims=True) + EPS)


def _l2norm(x):
    return x * lax.rsqrt(jnp.sum(x * x, axis=-1, keepdims=True) + EPS)


def _heads(t, n_heads):
    b, l, _ = t.shape
    return t.reshape(b, l, n_heads, -1).transpose(0, 2, 1, 3)


def _merge(t):
    b, h, l, d = t.shape
    return t.transpose(0, 2, 1, 3).reshape(b, l, h * d)


def _chunks(t, size):
    return t.reshape(t.shape[:2] + (t.shape[2] // size, size) + t.shape[3:])


def _rev(t):
    return jnp.flip(t, axis=2)


def _axial_rope(l):
    n_rows = l // GRID_W
    t_row = jnp.repeat(jnp.arange(n_rows, dtype=jnp.float32), GRID_W)
    t_col = jnp.tile(jnp.arange(GRID_W, dtype=jnp.float32), n_rows)
    n_freq = HEAD_DIM // 4
    inv = ROPE_BASE ** (-jnp.arange(n_freq, dtype=jnp.float32) / n_freq)
    ang = jnp.concatenate([t_row[:, None] * inv, t_col[:, None] * inv], axis=-1)
    return jnp.cos(ang), jnp.sin(ang)


def _apply_rope(x, cos, sin):
    x1, x2 = jnp.split(x, 2, axis=-1)
    return jnp.concatenate([x1 * cos - x2 * sin, x1 * sin + x2 * cos], axis=-1)


def _dwconv(x, w):
    ch = x.shape[-1]
    pad = (GDN_CONV - 1) // 2
    return lax.conv_general_dilated(x, w[:, None, :], window_strides=(1,), padding=[(pad, pad)],
                                    dimension_numbers=('NWC', 'WIO', 'NWC'), feature_group_count=ch)


def _retention(q, k, v, log_gamma, s0):
    b, h, l, d = q.shape
    qc, kc, vc = _chunks(q, CHUNK), _chunks(k, CHUNK), _chunks(v, CHUNK)
    pos = jnp.arange(CHUNK, dtype=jnp.float32)
    rel = pos[:, None] - pos[None, :]
    intra = jnp.where(rel >= 0, jnp.exp(jnp.maximum(rel, 0.0) * log_gamma[:, None, None]), 0.0)
    q_dec = jnp.exp((pos + 1.0) * log_gamma[:, None])
    k_dec = jnp.exp((CHUNK - 1.0 - pos) * log_gamma[:, None])
    c_dec = jnp.exp(CHUNK * log_gamma)[None, :, None, None]
    scores = jnp.einsum('bhnid,bhnjd->bhnij', qc, kc) * intra[None, :, None]
    o_intra = jnp.einsum('bhnij,bhnje->bhnie', scores, vc)
    kv = jnp.einsum('bhnjd,hj,bhnje->bhnde', kc, k_dec, vc)

    def step(s, kv_n):
        return c_dec * s + kv_n, s

    s_fin, s_prev = lax.scan(step, s0, jnp.moveaxis(kv, 2, 0))
    o_inter = jnp.einsum('bhnid,hi,bhnde->bhnie', qc, q_dec, jnp.moveaxis(s_prev, 0, 2))
    return (o_intra + o_inter).reshape(b, h, l, d), s_fin


def _gated_delta(q, k, v, log_alpha, beta, s0):
    b, h, l, _ = q.shape
    qc, kc, vc = _chunks(q, CHUNK), _chunks(k, CHUNK), _chunks(v, CHUNK)
    g = jnp.cumsum(_chunks(log_alpha, CHUNK), axis=-1)
    bc = _chunks(beta, CHUNK)
    idx = jnp.arange(CHUNK)
    incl = idx[:, None] >= idx[None, :]
    strict = idx[:, None] > idx[None, :]
    decay = jnp.exp(jnp.where(incl, g[..., :, None] - g[..., None, :], -jnp.inf))
    kb = kc * bc[..., None]
    a_mat = jnp.where(strict, jnp.einsum('bhnid,bhnjd->bhnij', kb, kc) * decay, 0.0)
    eye = jnp.eye(CHUNK, dtype=jnp.float32)
    t_mat = lax.linalg.triangular_solve(a_mat + eye, jnp.broadcast_to(eye, a_mat.shape),
                                        left_side=True, lower=True, unit_diagonal=True)
    u = jnp.einsum('bhnij,bhnje->bhnie', t_mat, vc * bc[..., None])
    w = jnp.einsum('bhnij,bhnjd->bhnid', t_mat, kb * jnp.exp(g)[..., None])
    attn = jnp.einsum('bhnid,bhnjd->bhnij', qc, kc) * decay
    qg = qc * jnp.exp(g)[..., None]
    kt = kc * jnp.exp(g[..., -1:] - g)[..., None]
    cd = jnp.exp(g[..., -1])

    def step(s, xs):
        u_n, w_n, qg_n, at_n, kt_n, cd_n = xs
        v_new = u_n - jnp.einsum('bhcd,bhde->bhce', w_n, s)
        o = jnp.einsum('bhcd,bhde->bhce', qg_n, s) + jnp.einsum('bhcs,bhse->bhce', at_n, v_new)
        s = s * cd_n[..., None, None] + jnp.einsum('bhcd,bhce->bhde', kt_n, v_new)
        return s, o

    xs = (jnp.moveaxis(u, 2, 0), jnp.moveaxis(w, 2, 0), jnp.moveaxis(qg, 2, 0),
          jnp.moveaxis(attn, 2, 0), jnp.moveaxis(kt, 2, 0), jnp.moveaxis(cd, 2, 0))
    s_fin, o = lax.scan(step, s0, xs)
    return jnp.moveaxis(o, 0, 2).reshape(b, h, l, -1), s_fin


def _gla(q, k, v, log_f, s0):
    b, h, l, _ = q.shape
    qc, kc, vc, fc = (_chunks(q, HG_CHUNK), _chunks(k, HG_CHUNK), _chunks(v, HG_CHUNK), _chunks(log_f, HG_CHUNK))
    cum = jnp.cumsum(fc, axis=-2)
    idx = jnp.arange(HG_CHUNK)
    incl = (idx[:, None] >= idx[None, :])[:, :, None]
    dec = jnp.exp(jnp.where(incl, cum[..., :, None, :] - cum[..., None, :, :], -jnp.inf))
    attn = jnp.einsum('bhnid,bhnjd,bhnijd->bhnij', qc, kc, dec)
    o_intra = jnp.einsum('bhnij,bhnje->bhnie', attn, vc)
    q_in = qc * jnp.exp(cum)
    k_tail = kc * jnp.exp(cum[..., -1:, :] - cum)
    cd = jnp.exp(cum[..., -1, :])
    kv = jnp.einsum('bhnjd,bhnje->bhnde', k_tail, vc)

    def step(s, xs):
        kv_n, cd_n = xs
        return cd_n[..., None] * s + kv_n, s

    s_fin, s_prev = lax.scan(step, s0, (jnp.moveaxis(kv, 2, 0), jnp.moveaxis(cd, 2, 0)))
    o_inter = jnp.einsum('bhnid,bhnde->bhnie', q_in, jnp.moveaxis(s_prev, 0, 2))
    return (o_intra + o_inter).reshape(b, h, l, -1), s_fin


def _cplx_combine(e1, e2):
    a1r, a1i, b1r, b1i = e1
    a2r, a2i, b2r, b2i = e2
    return (a1r * a2r - a1i * a2i, a1r * a2i + a1i * a2r,
            a2r * b1r - a2i * b1i + b2r, a2r * b1i + a2i * b1r + b2i)


def _s5(u, a_re, a_im, log_step, b_re, b_im, c_re, c_im, x0_re, x0_im):
    dt = jnp.exp(log_step)[:, None]
    mag = jnp.exp(a_re * dt)
    ab_re = mag * jnp.cos(a_im * dt)
    ab_im = mag * jnp.sin(a_im * dt)
    den = a_re * a_re + a_im * a_im
    nr = ab_re - 1.0
    f_re = (nr * a_re + ab_im * a_im) / den
    f_im = (ab_im * a_re - nr * a_im) / den
    bb_re = f_re[..., None] * b_re - f_im[..., None] * b_im
    bb_im = f_re[..., None] * b_im + f_im[..., None] * b_re
    bu_re = jnp.einsum('blgc,gnc->blgn', u, bb_re)
    bu_im = jnp.einsum('blgc,gnc->blgn', u, bb_im)
    bu_re = bu_re.at[:, 0].add(ab_re * x0_re - ab_im * x0_im)
    bu_im = bu_im.at[:, 0].add(ab_re * x0_im + ab_im * x0_re)
    a_full_re = jnp.broadcast_to(ab_re, bu_re.shape)
    a_full_im = jnp.broadcast_to(ab_im, bu_im.shape)
    _, _, x_re, x_im = lax.associative_scan(_cplx_combine, (a_full_re, a_full_im, bu_re, bu_im), axis=1)
    y = jnp.einsum('blgn,gcn->blgc', x_re, c_re) - jnp.einsum('blgn,gcn->blgc', x_im, c_im)
    return y, x_re[:, -1], x_im[:, -1]


def _mixer(h, lp, lower_bound, s_ret, s_gdn, s_hg, s_s5_re, s_s5_im, rope):
    f32 = jnp.float32
    bsz, l, _ = h.shape
    w = GROUP_WIDTH
    widths = (w, w, w, w, w, w, w, w, 2 * GDN_HEADS, 2 * GDN_HEADS, w, 2 * w, w, w)
    offsets, acc = [], 0
    for wd in widths:
        acc += wd
        offsets.append(acc)
    proj = (h @ lp['in_proj']).astype(f32)
    (r_q, r_k, r_v, r_g, d_q, d_k, d_v, d_g, d_a, d_b,
     h_q, h_f, h_i, h_g, s_u) = jnp.split(proj, offsets, axis=-1)

    q = _heads(r_q, RET_HEADS)
    k = _heads(r_k, RET_HEADS)
    v = _heads(r_v, RET_HEADS)
    if rope is not None:
        q = _apply_rope(q, *rope)
        k = _apply_rope(k, *rope)
    k = k * HEAD_DIM ** -0.5
    log_gamma = jax.nn.log_sigmoid(lp['ret_decay_logit'].astype(f32))
    o_f, rs_f = _retention(q, k, v, log_gamma[0], s_ret[:, 0])
    o_b, rs_b = _retention(_rev(q), _rev(k), _rev(v), log_gamma[1], s_ret[:, 1])
    ret_out = _merge(_head_layernorm(o_f + _rev(o_b))) * jax.nn.silu(r_g)

    qkv = jax.nn.silu(_dwconv(jnp.concatenate([d_q, d_k, d_v], axis=-1), lp['gdn_conv'].astype(f32)))
    gq, gk, gv = jnp.split(qkv, 3, axis=-1)
    q = _l2norm(_heads(gq, GDN_HEADS)) * HEAD_DIM ** -0.5
    k = _l2norm(_heads(gk, GDN_HEADS))
    v = _heads(gv, GDN_HEADS)
    a = d_a.reshape(bsz, l, 2, GDN_HEADS)
    bt = d_b.reshape(bsz, l, 2, GDN_HEADS)
    log_alpha = (-jnp.exp(lp['gdn_a_log'].astype(f32))
                 * jax.nn.softplus(a + lp['gdn_dt_bias'].astype(f32))).transpose(2, 0, 3, 1)
    beta = jax.nn.sigmoid(bt).transpose(2, 0, 3, 1)
    o_f, gs_f = _gated_delta(q, k, v, log_alpha[0], beta[0], s_gdn[:, 0])
    o_b, gs_b = _gated_delta(_rev(q), _rev(k), _rev(v), _rev(log_alpha[1]), _rev(beta[1]), s_gdn[:, 1])
    gdn_out = _merge(_rmsnorm(o_f + _rev(o_b), lp['gdn_norm_w'])) * jax.nn.silu(d_g)

    q = _heads(jax.nn.silu(h_q), HG_HEADS) * HEAD_DIM ** -0.5
    v = _heads(h_i, HG_HEADS)
    fx = h_f.reshape(bsz, l, 2, w)
    log_f = jnp.logaddexp(jnp.log(jnp.maximum(lower_bound, LB_FLOOR)),
                          jnp.log1p(-lower_bound) + jax.nn.log_sigmoid(fx))
    key_in = (1.0 - lower_bound) * jax.nn.sigmoid(-fx)
    o_f, hs_f = _gla(q, _heads(key_in[:, :, 0], HG_HEADS), v, _heads(log_f[:, :, 0], HG_HEADS), s_hg[:, 0])
    o_b, hs_b = _gla(_rev(q), _rev(_heads(key_in[:, :, 1], HG_HEADS)), _rev(v),
                     _rev(_heads(log_f[:, :, 1], HG_HEADS)), s_hg[:, 1])
    hg_out = _merge(_rmsnorm(o_f + _rev(o_b), lp['hg_norm_w'])) * jax.nn.silu(h_g)

    u = s_u.reshape(bsz, l, S5_GROUPS, S5_CH)
    p = [lp[nm].astype(f32) for nm in ('s5_a_re', 's5_a_im', 's5_log_step', 's5_b_re', 's5_b_im', 's5_c_re', 's5_c_im')]
    y_f, xr_f, xi_f = _s5(u, p[0][0], p[1][0], p[2][0], p[3][0], p[4][0], p[5][0], p[6][0], s_s5_re[:, 0], s_s5_im[:, 0])
    y_b, xr_b, xi_b = _s5(jnp.flip(u, 1), p[0][1], p[1][1], p[2][1], p[3][1], p[4][1], p[5][1], p[6][1],
                          s_s5_re[:, 1], s_s5_im[:, 1])
    y = (y_f + jnp.flip(y_b, 1)).reshape(bsz, l, w) + lp['s5_d'].astype(f32) * s_u
    z = jax.nn.gelu(y)
    s5_out = z * jax.nn.sigmoid(z @ lp['s5_glu_w'].astype(f32) + lp['s5_glu_b'].astype(f32))

    mixed = jnp.concatenate([ret_out, gdn_out, hg_out, s5_out], axis=-1).astype(h.dtype) @ lp['out_proj']
    new_states = (jnp.stack([rs_f, rs_b], axis=1), jnp.stack([gs_f, gs_b], axis=1),
                  jnp.stack([hs_f, hs_b], axis=1), jnp.stack([xr_f, xr_b], axis=1),
                  jnp.stack([xi_f, xi_b], axis=1))
    return mixed, new_states


def _layer(x, mod, lp, lower_bound, states, rope):
    shift1, scale1, gate1, shift2, scale2, gate2 = jnp.split(mod[:, None, :], N_MOD, axis=-1)
    h = _rmsnorm(x, lp['norm1_w']) * (1.0 + scale1) + shift1
    mixed, new_states = _mixer(h, lp, lower_bound, states[0], states[1], states[2], states[3], states[4], rope)
    x = x + gate1 * mixed
    h = _rmsnorm(x, lp['norm2_w']) * (1.0 + scale2) + shift2
    ffn = (jax.nn.silu(h @ lp['ffn_w1']) * (h @ lp['ffn_w3'])) @ lp['ffn_w2']
    x = x + gate2 * ffn
    return x, new_states


def setup_inputs(seed: int = 0) -> dict:
    key = jax.random.key(seed)
    ks = iter(jax.random.split(key, 48))
    f32 = jnp.float32
    w = GROUP_WIDTH

    def nrm(shape, scale):
        return jax.random.normal(next(ks), shape, f32) * scale

    def unif(shape, lo, hi):
        return jax.random.uniform(next(ks), shape, f32, lo, hi)

    st_shape = (DEC_BATCH, DEPTH, 2, RET_HEADS, HEAD_DIM, HEAD_DIM)
    s5_st_shape = (DEC_BATCH, DEPTH, 2, S5_GROUPS, S5_N)
    gamma = 1.0 - 2.0 ** (-5.0 - jnp.arange(RET_HEADS, dtype=f32))
    dt = jnp.exp(unif((DEPTH, 2, GDN_HEADS), math.log(1e-3), math.log(0.1)))
    n_idx = jnp.arange(S5_N, dtype=f32)
    return {
        'x_prompt': nrm((BATCH, SEQ, D_MODEL), 1.0),
        'x_sample': nrm((DEC_BATCH, DEC_SEQ, D_MODEL), 1.0),
        'state_ret': nrm(st_shape, 0.5),
        'state_gdn': nrm(st_shape, 0.5),
        'state_hgrn': nrm(st_shape, 0.5),
        'state_s5_re': nrm(s5_st_shape, 0.1),
        'state_s5_im': nrm(s5_st_shape, 0.1),
        'c': nrm((DEC_BATCH, D_MODEL), 1.0),
        'c_ctx': nrm((D_MODEL,), 1.0),
        'norm1_w': 1.0 + nrm((DEPTH, D_MODEL), 0.01),
        'norm2_w': 1.0 + nrm((DEPTH, D_MODEL), 0.01),
        'final_norm_w': 1.0 + nrm((D_MODEL,), 0.01),
        'ada_w': nrm((DEPTH, D_MODEL, N_MOD * D_MODEL), 0.5 * D_MODEL ** -0.5),
        'ada_b': nrm((DEPTH, N_MOD * D_MODEL), 0.01),
        'in_proj': nrm((DEPTH, D_MODEL, PROJ_WIDTH), D_MODEL ** -0.5),
        'out_proj': nrm((DEPTH, D_MODEL, D_MODEL), D_MODEL ** -0.5),
        'ret_decay_logit': jnp.log(gamma / (1.0 - gamma)) + nrm((DEPTH, 2, RET_HEADS), 0.1),
        'gdn_conv': nrm((DEPTH, GDN_CONV, 3 * w), GDN_CONV ** -0.5),
        'gdn_a_log': jnp.log(unif((DEPTH, 2, GDN_HEADS), 1.0, 16.0)),
        'gdn_dt_bias': dt + jnp.log(-jnp.expm1(-dt)),
        'gdn_norm_w': 1.0 + nrm((DEPTH, HEAD_DIM), 0.01),
        'hg_lb_param': nrm((DEPTH, 2, w), 0.1),
        'hg_norm_w': 1.0 + nrm((DEPTH, HEAD_DIM), 0.01),
        's5_a_re': -0.5 + nrm((DEPTH, 2, S5_GROUPS, S5_N), 0.01),
        's5_a_im': jnp.pi * n_idx + nrm((DEPTH, 2, S5_GROUPS, S5_N), 0.01),
        's5_b_re': nrm((DEPTH, 2, S5_GROUPS, S5_N, S5_CH), (2 * S5_CH) ** -0.5),
        's5_b_im': nrm((DEPTH, 2, S5_GROUPS, S5_N, S5_CH), (2 * S5_CH) ** -0.5),
        's5_c_re': nrm((DEPTH, 2, S5_GROUPS, S5_CH, S5_N), S5_N ** -0.5),
        's5_c_im': nrm((DEPTH, 2, S5_GROUPS, S5_CH, S5_N), S5_N ** -0.5),
        's5_log_step': unif((DEPTH, 2, S5_GROUPS), math.log(1e-3), math.log(0.1)),
        's5_d': nrm((DEPTH, w), 1.0),
        's5_glu_w': nrm((DEPTH, w, w), w ** -0.5),
        's5_glu_b': nrm((DEPTH, w), 0.01),
        'ffn_w1': nrm((DEPTH, D_MODEL, FFN_HIDDEN), D_MODEL ** -0.5),
        'ffn_w3': nrm((DEPTH, D_MODEL, FFN_HIDDEN), D_MODEL ** -0.5),
        'ffn_w2': nrm((DEPTH, FFN_HIDDEN, D_MODEL), FFN_HIDDEN ** -0.5),
    }


def reference(x_prompt, x_sample, state_ret, state_gdn, state_hgrn, state_s5_re, state_s5_im, c, c_ctx,
              norm1_w, norm2_w, final_norm_w, ada_w, ada_b, in_proj, out_proj, ret_decay_logit,
              gdn_conv, gdn_a_log, gdn_dt_bias, gdn_norm_w, hg_lb_param, hg_norm_w,
              s5_a_re, s5_a_im, s5_b_re, s5_b_im, s5_c_re, s5_c_im, s5_log_step, s5_d,
              s5_glu_w, s5_glu_b, ffn_w1, ffn_w3, ffn_w2):
    f32 = jnp.float32
    lb_soft = jax.nn.softmax(hg_lb_param.astype(f32), axis=0)
    lower_bounds = jnp.cumsum(lb_soft, axis=0) - lb_soft[0]
    rope = _axial_rope(x_sample.shape[1])
    bp = x_prompt.shape[0]
    zero_mat = jnp.zeros((bp, 2, RET_HEADS, HEAD_DIM, HEAD_DIM), f32)
    zero_s5 = jnp.zeros((bp, 2, S5_GROUPS, S5_N), f32)
    ctx_init = (zero_mat, zero_mat, zero_mat, zero_s5, zero_s5)
    hp, hs = x_prompt, x_sample
    ctx_states = []
    for i in range(DEPTH):
        lp = {
            'norm1_w': norm1_w[i], 'norm2_w': norm2_w[i], 'in_proj': in_proj[i], 'out_proj': out_proj[i],
            'ret_decay_logit': ret_decay_logit[i], 'gdn_conv': gdn_conv[i], 'gdn_a_log': gdn_a_log[i],
            'gdn_dt_bias': gdn_dt_bias[i], 'gdn_norm_w': gdn_norm_w[i], 'hg_norm_w': hg_norm_w[i],
            's5_a_re': s5_a_re[i], 's5_a_im': s5_a_im[i], 's5_log_step': s5_log_step[i],
            's5_b_re': s5_b_re[i], 's5_b_im': s5_b_im[i], 's5_c_re': s5_c_re[i], 's5_c_im': s5_c_im[i],
            's5_d': s5_d[i], 's5_glu_w': s5_glu_w[i], 's5_glu_b': s5_glu_b[i],
            'ffn_w1': ffn_w1[i], 'ffn_w3': ffn_w3[i], 'ffn_w2': ffn_w2[i],
        }
        mod_ctx = jax.nn.silu(c_ctx)[None, :] @ ada_w[i] + ada_b[i]
        mod_lat = jax.nn.silu(c) @ ada_w[i] + ada_b[i]
        hp, st = _layer(hp, mod_ctx, lp, lower_bounds[i], ctx_init, None)
        ctx_states.append(st)
        lat_init = (state_ret[:, i].astype(f32), state_gdn[:, i].astype(f32), state_hgrn[:, i].astype(f32),
                    state_s5_re[:, i].astype(f32), state_s5_im[:, i].astype(f32))
        hs, _ = _layer(hs, mod_lat, lp, lower_bounds[i], lat_init, rope)
    y_prompt = _rmsnorm(hp, final_norm_w)
    y_sample = _rmsnorm(hs, final_norm_w)
    sdt = x_prompt.dtype
    new_ret = jnp.stack([s[0] for s in ctx_states], axis=1).astype(sdt)
    new_gdn = jnp.stack([s[1] for s in ctx_states], axis=1).astype(sdt)
    new_hgrn = jnp.stack([s[2] for s in ctx_states], axis=1).astype(sdt)
    new_s5_re = jnp.stack([s[3] for s in ctx_states], axis=1).astype(sdt)
    new_s5_im = jnp.stack([s[4] for s in ctx_states], axis=1).astype(sdt)
    return (y_prompt, y_sample, new_ret, new_gdn, new_hgrn, new_s5_re, new_s5_im)
```

```python
import functools
import math

import jax
import jax.numpy as jnp
from jax import lax
from jax.experimental import pallas as pl
from jax.experimental.pallas import tpu as pltpu

F32 = jnp.float32
BF16 = jnp.bfloat16

D_MODEL = 2048
BATCH = 16
SEQ = 256
DEPTH = 2
DEC_BATCH = 8
DEC_SEQ = 1024
GRID_W = 64
HEAD_DIM = 128
GROUP_WIDTH = 512
N_HEADS = 4
S5_CH = 16
S5_GROUPS = 32
S5_N = 64
GDN_CONV = 3
CHUNK = 64
HG_CHUNK = 16
ROPE_BASE = 10000.0
FFN_HIDDEN = 5632
N_MOD = 6
EPS = 1e-6
LB_FLOOR = 1e-30

N_CTX_TOK = BATCH * SEQ
N_LAT_TOK = DEC_BATCH * DEC_SEQ
N_TOK = N_CTX_TOK + N_LAT_TOK
N_SEQ_ROWS = 16
PROJ_MAIN = 14 * GROUP_WIDTH
PROJ_SMALL = 128
VMEM_LIMIT = 56 * 1024 * 1024


def _seq_row(tile, tm):
    n_ctx = N_CTX_TOK // tm
    per_lat = DEC_SEQ // tm
    return jnp.where(tile < n_ctx, 0, 1 + (tile - n_ctx) // per_lat)


def _ada_body(c_ref, w_ref, b_ref, o_ref):
    cv = c_ref[...]
    s = cv * jax.nn.sigmoid(cv)
    o_ref[0] = jnp.dot(s.astype(BF16), w_ref[0].astype(BF16), preferred_element_type=F32) + b_ref[0]


def _ada(cvec, ada_w, ada_b):
    tn = 1024
    n = N_MOD * D_MODEL
    return pl.pallas_call(
        _ada_body,
        out_shape=jax.ShapeDtypeStruct((DEPTH, N_SEQ_ROWS, n), F32),
        grid=(DEPTH, n // tn),
        in_specs=[
            pl.BlockSpec((N_SEQ_ROWS, D_MODEL), lambda l, j: (0, 0)),
            pl.BlockSpec((1, D_MODEL, tn), lambda l, j: (l, 0, j)),
            pl.BlockSpec((1, 1, tn), lambda l, j: (l, 0, j)),
        ],
        out_specs=pl.BlockSpec((1, N_SEQ_ROWS, tn), lambda l, j: (l, 0, j)),
        compiler_params=pltpu.CompilerParams(
            dimension_semantics=("parallel", "parallel"), vmem_limit_bytes=VMEM_LIMIT),
        name="ada_mod",
    )(cvec, ada_w, ada_b.reshape(DEPTH, 1, n))


def _norm_mod(x, nw, sc, sh):
    ms = jnp.mean(x * x, axis=-1, keepdims=True)
    y = x * lax.rsqrt(ms + EPS) * nw
    return y * (1.0 + sc) + sh


def _inproj_body(x_ref, nw_ref, sc_ref, sh_ref, w_ref, ws_ref, o_ref, os_ref, h_ref):
    @pl.when(pl.program_id(1) == 0)
    def _():
        hb = _norm_mod(x_ref[...], nw_ref[...], sc_ref[...], sh_ref[...]).astype(BF16)
        h_ref[...] = hb
        os_ref[...] = jnp.dot(hb, ws_ref[...], preferred_element_type=F32)

    o_ref[...] = jnp.dot(h_ref[...], w_ref[...], preferred_element_type=F32)


def _inproj(x, nw, mod, w_main, w_small):
    tm, tn = 1024, 1024
    return pl.pallas_call(
        _inproj_body,
        out_shape=(jax.ShapeDtypeStruct((N_TOK, PROJ_MAIN), F32),
                   jax.ShapeDtypeStruct((N_TOK, PROJ_SMALL), F32)),
        grid=(N_TOK // tm, PROJ_MAIN // tn),
        in_specs=[
            pl.BlockSpec((tm, D_MODEL), lambda i, j: (i, 0)),
            pl.BlockSpec((1, D_MODEL), lambda i, j: (0, 0)),
            pl.BlockSpec((None, None, 1, D_MODEL), lambda i, j: (_seq_row(i, tm), 1, 0, 0)),
            pl.BlockSpec((None, None, 1, D_MODEL), lambda i, j: (_seq_row(i, tm), 0, 0, 0)),
            pl.BlockSpec((D_MODEL, tn), lambda i, j: (0, j)),
            pl.BlockSpec((D_MODEL, PROJ_SMALL), lambda i, j: (0, 0)),
        ],
        out_specs=(pl.BlockSpec((tm, tn), lambda i, j: (i, j)),
                   pl.BlockSpec((tm, PROJ_SMALL), lambda i, j: (i, 0))),
        scratch_shapes=[pltpu.VMEM((tm, D_MODEL), BF16)],
        compiler_params=pltpu.CompilerParams(
            dimension_semantics=("parallel", "arbitrary"), vmem_limit_bytes=VMEM_LIMIT),
        name="in_proj",
    )(x, nw, mod, mod, w_main, w_small)


def _outproj_body(m_ref, w_ref, x_ref, g_ref, o_ref):
    o_ref[...] = x_ref[...] + g_ref[...] * jnp.dot(m_ref[...], w_ref[...], preferred_element_type=F32)


def _outproj(mixed, w, x, mod):
    tm, tn = 1024, 1024
    return pl.pallas_call(
        _outproj_body,
        out_shape=jax.ShapeDtypeStruct((N_TOK, D_MODEL), F32),
        grid=(N_TOK // tm, D_MODEL // tn),
        in_specs=[
            pl.BlockSpec((tm, D_MODEL), lambda i, j: (i, 0)),
            pl.BlockSpec((D_MODEL, tn), lambda i, j: (0, j)),
            pl.BlockSpec((tm, tn), lambda i, j: (i, j)),
            pl.BlockSpec((None, None, 1, tn), lambda i, j: (_seq_row(i, tm), 2, 0, j)),
        ],
        out_specs=pl.BlockSpec((tm, tn), lambda i, j: (i, j)),
        compiler_params=pltpu.CompilerParams(
            dimension_semantics=("parallel", "arbitrary"), vmem_limit_bytes=VMEM_LIMIT),
        name="out_proj",
    )(mixed, w, x, mod)


def _ffn_body(x_ref, nw_ref, sc_ref, sh_ref, g_ref, w1_ref, w3_ref, w2_ref, fw_ref, o_ref, h_ref, acc_ref,
              *, final_norm):
    k = pl.program_id(1)

    @pl.when(k == 0)
    def _():
        h_ref[...] = _norm_mod(x_ref[...], nw_ref[...], sc_ref[...], sh_ref[...]).astype(BF16)
        acc_ref[...] = jnp.zeros_like(acc_ref)

    h = h_ref[...]
    a = jnp.dot(h, w1_ref[...], preferred_element_type=F32)
    b = jnp.dot(h, w3_ref[...], preferred_element_type=F32)
    g = (a * jax.nn.sigmoid(a) * b).astype(BF16)
    acc_ref[...] += jnp.dot(g, w2_ref[...], preferred_element_type=F32)

    @pl.when(k == pl.num_programs(1) - 1)
    def _():
        y = x_ref[...] + g_ref[...] * acc_ref[...]
        if final_norm:
            ms = jnp.mean(y * y, axis=-1, keepdims=True)
            y = y * lax.rsqrt(ms + EPS) * fw_ref[...]
        o_ref[...] = y


def _ffn(x, nw, mod, w1, w3, w2, fw, final_norm):
    tm, th = 512, 512
    return pl.pallas_call(
        functools.partial(_ffn_body, final_norm=final_norm),
        out_shape=jax.ShapeDtypeStruct((N_TOK, D_MODEL), F32),
        grid=(N_TOK // tm, FFN_HIDDEN // th),
        in_specs=[
            pl.BlockSpec((tm, D_MODEL), lambda i, k: (i, 0)),
            pl.BlockSpec((1, D_MODEL), lambda i, k: (0, 0)),
            pl.BlockSpec((None, None, 1, D_MODEL), lambda i, k: (_seq_row(i, tm), 4, 0, 0)),
            pl.BlockSpec((None, None, 1, D_MODEL), lambda i, k: (_seq_row(i, tm), 3, 0, 0)),
            pl.BlockSpec((None, None, 1, D_MODEL), lambda i, k: (_seq_row(i, tm), 5, 0, 0)),
            pl.BlockSpec((D_MODEL, th), lambda i, k: (0, k)),
            pl.BlockSpec((D_MODEL, th), lambda i, k: (0, k)),
            pl.BlockSpec((th, D_MODEL), lambda i, k: (k, 0)),
            pl.BlockSpec((1, D_MODEL), lambda i, k: (0, 0)),
        ],
        out_specs=pl.BlockSpec((tm, D_MODEL), lambda i, k: (i, 0)),
        scratch_shapes=[pltpu.VMEM((tm, D_MODEL), BF16), pltpu.VMEM((tm, D_MODEL), F32)],
        compiler_params=pltpu.CompilerParams(
            dimension_semantics=("parallel", "arbitrary"), vmem_limit_bytes=VMEM_LIMIT),
        name="ffn",
    )(x, nw, mod, mod, mod, w1, w3, w2, fw)


def _head_layernorm(x):
    mu = jnp.mean(x, axis=-1, keepdims=True)
    xc = x - mu
    return xc * lax.rsqrt(jnp.mean(xc * xc, axis=-1, keepdims=True) + EPS)


def _rmsnorm(x, w):
    y = x * lax.rsqrt(jnp.mean(x * x, axis=-1, keepdims=True) + EPS)
    return y * w


def _l2norm(x):
    return x * lax.rsqrt(jnp.sum(x * x, axis=-1, keepdims=True) + EPS)


def _heads(t, n_heads):
    b, l, _ = t.shape
    return t.reshape(b, l, n_heads, -1).transpose(0, 2, 1, 3)


def _merge(t):
    b, h, l, d = t.shape
    return t.transpose(0, 2, 1, 3).reshape(b, l, h * d)


def _chunks(t, size):
    return t.reshape(t.shape[:2] + (t.shape[2] // size, size) + t.shape[3:])


def _rev(t):
    return jnp.flip(t, axis=2)


def _axial_rope(l):
    n_rows = l // GRID_W
    t_row = jnp.repeat(jnp.arange(n_rows, dtype=F32), GRID_W)
    t_col = jnp.tile(jnp.arange(GRID_W, dtype=F32), n_rows)
    n_freq = HEAD_DIM // 4
    inv = ROPE_BASE ** (-jnp.arange(n_freq, dtype=F32) / n_freq)
    ang = jnp.concatenate([t_row[:, None] * inv, t_col[:, None] * inv], axis=-1)
    return jnp.cos(ang), jnp.sin(ang)


def _apply_rope(x, cos, sin):
    x1, x2 = jnp.split(x, 2, axis=-1)
    return jnp.concatenate([x1 * cos - x2 * sin, x1 * sin + x2 * cos], axis=-1)


def _dwconv(x, w):
    ch = x.shape[-1]
    pad = (GDN_CONV - 1) // 2
    return lax.conv_general_dilated(x, w[:, None, :], window_strides=(1,), padding=[(pad, pad)],
                                    dimension_numbers=('NWC', 'WIO', 'NWC'), feature_group_count=ch)


def _retention(q, k, v, log_gamma, s0):
    b, h, l, d = q.shape
    qc, kc, vc = _chunks(q, CHUNK), _chunks(k, CHUNK), _chunks(v, CHUNK)
    pos = jnp.arange(CHUNK, dtype=F32)
    rel = pos[:, None] - pos[None, :]
    intra = jnp.where(rel >= 0, jnp.exp(jnp.maximum(rel, 0.0) * log_gamma[:, None, None]), 0.0)
    q_dec = jnp.exp((pos + 1.0) * log_gamma[:, None])
    k_dec = jnp.exp((CHUNK - 1.0 - pos) * log_gamma[:, None])
    c_dec = jnp.exp(CHUNK * log_gamma)[None, :, None, None]
    scores = jnp.einsum('bhnid,bhnjd->bhnij', qc, kc) * intra[None, :, None]
    o_intra = jnp.einsum('bhnij,bhnje->bhnie', scores, vc)
    kv = jnp.einsum('bhnjd,hj,bhnje->bhnde', kc, k_dec, vc)

    def step(s, kv_n):
        return c_dec * s + kv_n, s

    s_fin, s_prev = lax.scan(step, s0, jnp.moveaxis(kv, 2, 0))
    o_inter = jnp.einsum('bhnid,hi,bhnde->bhnie', qc, q_dec, jnp.moveaxis(s_prev, 0, 2))
    return (o_intra + o_inter).reshape(b, h, l, d), s_fin


def _gated_delta(q, k, v, log_alpha, beta, s0):
    b, h, l, _ = q.shape
    qc, kc, vc = _chunks(q, CHUNK), _chunks(k, CHUNK), _chunks(v, CHUNK)
    g = jnp.cumsum(_chunks(log_alpha, CHUNK), axis=-1)
    bc = _chunks(beta, CHUNK)
    idx = jnp.arange(CHUNK)
    incl = idx[:, None] >= idx[None, :]
    strict = idx[:, None] > idx[None, :]
    decay = jnp.exp(jnp.where(incl, g[..., :, None] - g[..., None, :], -jnp.inf))
    kb = kc * bc[..., None]
    a_mat = jnp.where(strict, jnp.einsum('bhnid,bhnjd->bhnij', kb, kc) * decay, 0.0)
    eye = jnp.eye(CHUNK, dtype=F32)
    t_mat = lax.linalg.triangular_solve(a_mat + eye, jnp.broadcast_to(eye, a_mat.shape),
                                        left_side=True, lower=True, unit_diagonal=True)
    u = jnp.einsum('bhnij,bhnje->bhnie', t_mat, vc * bc[..., None])
    w = jnp.einsum('bhnij,bhnjd->bhnid', t_mat, kb * jnp.exp(g)[..., None])
    attn = jnp.einsum('bhnid,bhnjd->bhnij', qc, kc) * decay
    qg = qc * jnp.exp(g)[..., None]
    kt = kc * jnp.exp(g[..., -1:] - g)[..., None]
    cd = jnp.exp(g[..., -1])

    def step(s, xs):
        u_n, w_n, qg_n, at_n, kt_n, cd_n = xs
        v_new = u_n - jnp.einsum('bhcd,bhde->bhce', w_n, s)
        o = jnp.einsum('bhcd,bhde->bhce', qg_n, s) + jnp.einsum('bhcs,bhse->bhce', at_n, v_new)
        s = s * cd_n[..., None, None] + jnp.einsum('bhcd,bhce->bhde', kt_n, v_new)
        return s, o

    xs = (jnp.moveaxis(u, 2, 0), jnp.moveaxis(w, 2, 0), jnp.moveaxis(qg, 2, 0),
          jnp.moveaxis(attn, 2, 0), jnp.moveaxis(kt, 2, 0), jnp.moveaxis(cd, 2, 0))
    s_fin, o = lax.scan(step, s0, xs)
    return jnp.moveaxis(o, 0, 2).reshape(b, h, l, -1), s_fin


def _gla(q, k, v, log_f, s0):
    b, h, l, _ = q.shape
    qc, kc, vc, fc = (_chunks(q, HG_CHUNK), _chunks(k, HG_CHUNK), _chunks(v, HG_CHUNK), _chunks(log_f, HG_CHUNK))
    cum = jnp.cumsum(fc, axis=-2)
    idx = jnp.arange(HG_CHUNK)
    incl = (idx[:, None] >= idx[None, :])[:, :, None]
    dec = jnp.exp(jnp.where(incl, cum[..., :, None, :] - cum[..., None, :, :], -jnp.inf))
    attn = jnp.einsum('bhnid,bhnjd,bhnijd->bhnij', qc, kc, dec)
    o_intra = jnp.einsum('bhnij,bhnje->bhnie', attn, vc)
    q_in = qc * jnp.exp(cum)
    k_tail = kc * jnp.exp(cum[..., -1:, :] - cum)
    cd = jnp.exp(cum[..., -1, :])
    kv = jnp.einsum('bhnjd,bhnje->bhnde', k_tail, vc)

    def step(s, xs):
        kv_n, cd_n = xs
        return cd_n[..., None] * s + kv_n, s

    s_fin, s_prev = lax.scan(step, s0, (jnp.moveaxis(kv, 2, 0), jnp.moveaxis(cd, 2, 0)))
    o_inter = jnp.einsum('bhnid,bhnde->bhnie', q_in, jnp.moveaxis(s_prev, 0, 2))
    return (o_intra + o_inter).reshape(b, h, l, -1), s_fin


def _cplx_combine(e1, e2):
    a1r, a1i, b1r, b1i = e1
    a2r, a2i, b2r, b2i = e2
    return (a1r * a2r - a1i * a2i, a1r * a2i + a1i * a2r,
            a2r * b1r - a2i * b1i + b2r, a2r * b1i + a2i * b1r + b2i)


def _s5(u, a_re, a_im, log_step, b_re, b_im, c_re, c_im, x0_re, x0_im):
    dt = jnp.exp(log_step)[:, None]
    mag = jnp.exp(a_re * dt)
    ab_re = mag * jnp.cos(a_im * dt)
    ab_im = mag * jnp.sin(a_im * dt)
    den = a_re * a_re + a_im * a_im
    nr = ab_re - 1.0
    f_re = (nr * a_re + ab_im * a_im) / den
    f_im = (ab_im * a_re - nr * a_im) / den
    bb_re = f_re[..., None] * b_re - f_im[..., None] * b_im
    bb_im = f_re[..., None] * b_im + f_im[..., None] * b_re
    bu_re = jnp.einsum('blgc,gnc->blgn', u, bb_re)
    bu_im = jnp.einsum('blgc,gnc->blgn', u, bb_im)
    bu_re = bu_re.at[:, 0].add(ab_re * x0_re - ab_im * x0_im)
    bu_im = bu_im.at[:, 0].add(ab_re * x0_im + ab_im * x0_re)
    a_full_re = jnp.broadcast_to(ab_re, bu_re.shape)
    a_full_im = jnp.broadcast_to(ab_im, bu_im.shape)
    _, _, x_re, x_im = lax.associative_scan(_cplx_combine, (a_full_re, a_full_im, bu_re, bu_im), axis=1)
    y = jnp.einsum('blgn,gcn->blgc', x_re, c_re) - jnp.einsum('blgn,gcn->blgc', x_im, c_im)
    return y, x_re[:, -1], x_im[:, -1]


def _mixer_jax(pm, ps, lp, lower_bound, s_ret, s_gdn, s_hg, s_s5_re, s_s5_im, rope):
    bsz, l, _ = pm.shape
    w = GROUP_WIDTH
    sl = lambda i, n=1: pm[..., i * w:(i + n) * w]
    r_q, r_k, r_v, r_g, d_q, d_k, d_v, d_g = (sl(i) for i in range(8))
    h_q, h_f, h_i, h_g, s_u = sl(8), sl(9, 2), sl(11), sl(12), sl(13)
    d_a, d_b = ps[..., :8], ps[..., 8:16]

    q = _heads(r_q, N_HEADS)
    k = _heads(r_k, N_HEADS)
    v = _heads(r_v, N_HEADS)
    if rope is not None:
        q = _apply_rope(q, *rope)
        k = _apply_rope(k, *rope)
    k = k * HEAD_DIM ** -0.5
    log_gamma = jax.nn.log_sigmoid(lp['ret_decay_logit'])
    o_f, rs_f = _retention(q, k, v, log_gamma[0], s_ret[:, 0])
    o_b, rs_b = _retention(_rev(q), _rev(k), _rev(v), log_gamma[1], s_ret[:, 1])
    ret_out = _merge(_head_layernorm(o_f + _rev(o_b))) * jax.nn.silu(r_g)

    qkv = jax.nn.silu(_dwconv(jnp.concatenate([d_q, d_k, d_v], axis=-1), lp['gdn_conv']))
    gq, gk, gv = jnp.split(qkv, 3, axis=-1)
    q = _l2norm(_heads(gq, N_HEADS)) * HEAD_DIM ** -0.5
    k = _l2norm(_heads(gk, N_HEADS))
    v = _heads(gv, N_HEADS)
    a = d_a.reshape(bsz, l, 2, N_HEADS)
    bt = d_b.reshape(bsz, l, 2, N_HEADS)
    log_alpha = (-jnp.exp(lp['gdn_a_log']) * jax.nn.softplus(a + lp['gdn_dt_bias'])).transpose(2, 0, 3, 1)
    beta = jax.nn.sigmoid(bt).transpose(2, 0, 3, 1)
    o_f, gs_f = _gated_delta(q, k, v, log_alpha[0], beta[0], s_gdn[:, 0])
    o_b, gs_b = _gated_delta(_rev(q), _rev(k), _rev(v), _rev(log_alpha[1]), _rev(beta[1]), s_gdn[:, 1])
    gdn_out = _merge(_rmsnorm(o_f + _rev(o_b), lp['gdn_norm_w'])) * jax.nn.silu(d_g)

    q = _heads(jax.nn.silu(h_q), N_HEADS) * HEAD_DIM ** -0.5
    v = _heads(h_i, N_HEADS)
    fx = h_f.reshape(bsz, l, 2, w)
    log_f = jnp.logaddexp(jnp.log(jnp.maximum(lower_bound, LB_FLOOR)),
                          jnp.log1p(-lower_bound) + jax.nn.log_sigmoid(fx))
    key_in = (1.0 - lower_bound) * jax.nn.sigmoid(-fx)
    o_f, hs_f = _gla(q, _heads(key_in[:, :, 0], N_HEADS), v, _heads(log_f[:, :, 0], N_HEADS), s_hg[:, 0])
    o_b, hs_b = _gla(_rev(q), _rev(_heads(key_in[:, :, 1], N_HEADS)), _rev(v),
                     _rev(_heads(log_f[:, :, 1], N_HEADS)), s_hg[:, 1])
    hg_out = _merge(_rmsnorm(o_f + _rev(o_b), lp['hg_norm_w'])) * jax.nn.silu(h_g)

    u = s_u.reshape(bsz, l, S5_GROUPS, S5_CH)
    p = [lp[nm] for nm in ('s5_a_re', 's5_a_im', 's5_log_step', 's5_b_re', 's5_b_im', 's5_c_re', 's5_c_im')]
    y_f, xr_f, xi_f = _s5(u, p[0][0], p[1][0], p[2][0], p[3][0], p[4][0], p[5][0], p[6][0],
                          s_s5_re[:, 0], s_s5_im[:, 0])
    y_b, xr_b, xi_b = _s5(jnp.flip(u, 1), p[0][1], p[1][1], p[2][1], p[3][1], p[4][1], p[5][1], p[6][1],
                          s_s5_re[:, 1], s_s5_im[:, 1])
    y = (y_f + jnp.flip(y_b, 1)).reshape(bsz, l, w) + lp['s5_d'] * s_u
    z = jax.nn.gelu(y)
    s5_out = z * jax.nn.sigmoid(z @ lp['s5_glu_w'] + lp['s5_glu_b'])

    mixed = jnp.concatenate([ret_out, gdn_out, hg_out, s5_out], axis=-1).astype(BF16)
    new_states = (jnp.stack([rs_f, rs_b], axis=1), jnp.stack([gs_f, gs_b], axis=1),
                  jnp.stack([hs_f, hs_b], axis=1), jnp.stack([xr_f, xr_b], axis=1),
                  jnp.stack([xi_f, xi_b], axis=1))
    return mixed, new_states


def kernel(x_prompt, x_sample, state_ret, state_gdn, state_hgrn, state_s5_re, state_s5_im, c, c_ctx, norm1_w, norm2_w, final_norm_w, ada_w, ada_b, in_proj, out_proj, ret_decay_logit, gdn_conv, gdn_a_log, gdn_dt_bias, gdn_norm_w, hg_lb_param, hg_norm_w, s5_a_re, s5_a_im, s5_b_re, s5_b_im, s5_c_re, s5_c_im, s5_log_step, s5_d, s5_glu_w, s5_glu_b, ffn_w1, ffn_w3, ffn_w2):
    lb_soft = jax.nn.softmax(hg_lb_param, axis=0)
    lower_bounds = jnp.cumsum(lb_soft, axis=0) - lb_soft[0]
    rope = _axial_rope(DEC_SEQ)

    cvec = jnp.zeros((N_SEQ_ROWS, D_MODEL), F32).at[0].set(c_ctx).at[1:1 + DEC_BATCH].set(c)
    mod_all = _ada(cvec, ada_w, ada_b).reshape(DEPTH, N_SEQ_ROWS, N_MOD, 1, D_MODEL)

    w_main = jnp.concatenate([in_proj[:, :, :8 * GROUP_WIDTH], in_proj[:, :, 8 * GROUP_WIDTH + 16:]],
                             axis=-1).astype(BF16)
    w_small = jnp.pad(in_proj[:, :, 8 * GROUP_WIDTH:8 * GROUP_WIDTH + 16],
                      ((0, 0), (0, 0), (0, PROJ_SMALL - 16))).astype(BF16)
    w_out = out_proj.astype(BF16)
    w1, w3, w2 = ffn_w1.astype(BF16), ffn_w3.astype(BF16), ffn_w2.astype(BF16)

    x = jnp.concatenate([x_prompt.reshape(N_CTX_TOK, D_MODEL), x_sample.reshape(N_LAT_TOK, D_MODEL)], axis=0)
    zero_mat = jnp.zeros((BATCH, 2, N_HEADS, HEAD_DIM, HEAD_DIM), F32)
    zero_s5 = jnp.zeros((BATCH, 2, S5_GROUPS, S5_N), F32)
    ctx_states = []
    for i in range(DEPTH):
        lp = {
            'ret_decay_logit': ret_decay_logit[i], 'gdn_conv': gdn_conv[i], 'gdn_a_log': gdn_a_log[i],
            'gdn_dt_bias': gdn_dt_bias[i], 'gdn_norm_w': gdn_norm_w[i], 'hg_norm_w': hg_norm_w[i],
            's5_a_re': s5_a_re[i], 's5_a_im': s5_a_im[i], 's5_log_step': s5_log_step[i],
            's5_b_re': s5_b_re[i], 's5_b_im': s5_b_im[i], 's5_c_re': s5_c_re[i], 's5_c_im': s5_c_im[i],
            's5_d': s5_d[i], 's5_glu_w': s5_glu_w[i], 's5_glu_b': s5_glu_b[i],
        }
        mod = mod_all[i]
        pm, ps = _inproj(x, norm1_w[i][None], mod, w_main[i], w_small[i])
        mixed_c, st = _mixer_jax(pm[:N_CTX_TOK].reshape(BATCH, SEQ, PROJ_MAIN),
                                 ps[:N_CTX_TOK].reshape(BATCH, SEQ, PROJ_SMALL), lp, lower_bounds[i],
                                 zero_mat, zero_mat, zero_mat, zero_s5, zero_s5, None)
        ctx_states.append(st)
        mixed_l, _ = _mixer_jax(pm[N_CTX_TOK:].reshape(DEC_BATCH, DEC_SEQ, PROJ_MAIN),
                                ps[N_CTX_TOK:].reshape(DEC_BATCH, DEC_SEQ, PROJ_SMALL), lp, lower_bounds[i],
                                state_ret[:, i], state_gdn[:, i], state_hgrn[:, i],
                                state_s5_re[:, i], state_s5_im[:, i], rope)
        mixed = jnp.concatenate([mixed_c.reshape(N_CTX_TOK, D_MODEL), mixed_l.reshape(N_LAT_TOK, D_MODEL)], axis=0)
        x = _outproj(mixed, w_out[i], x, mod)
        x = _ffn(x, norm2_w[i][None], mod, w1[i], w3[i], w2[i], final_norm_w[None], i == DEPTH - 1)

    y_prompt = x[:N_CTX_TOK].reshape(BATCH, SEQ, D_MODEL)
    y_sample = x[N_CTX_TOK:].reshape(DEC_BATCH, DEC_SEQ, D_MODEL)
    new_ret = jnp.stack([s[0] for s in ctx_states], axis=1)
    new_gdn = jnp.stack([s[1] for s in ctx_states], axis=1)
    new_hgrn = jnp.stack([s[2] for s in ctx_states], axis=1)
    new_s5_re = jnp.stack([s[3] for s in ctx_states], axis=1)
    new_s5_im = jnp.stack([s[4] for s in ctx_states], axis=1)
    return (y_prompt, y_sample, new_ret, new_gdn, new_hgrn, new_s5_re, new_s5_im)
```

```python
import functools
import math

import jax
import jax.numpy as jnp
from jax import lax
from jax.experimental import pallas as pl
from jax.experimental.pallas import tpu as pltpu

F32 = jnp.float32
BF16 = jnp.bfloat16

D_MODEL = 2048
BATCH = 16
SEQ = 256
DEPTH = 2
DEC_BATCH = 8
DEC_SEQ = 1024
GRID_W = 64
HEAD_DIM = 128
GROUP_WIDTH = 512
N_HEADS = 4
S5_CH = 16
S5_GROUPS = 32
S5_N = 64
GDN_CONV = 3
CHUNK = 64
HG_CHUNK = 16
ROPE_BASE = 10000.0
FFN_HIDDEN = 5632
N_MOD = 6
EPS = 1e-6
LB_FLOOR = 1e-30

N_CTX_TOK = BATCH * SEQ
N_LAT_TOK = DEC_BATCH * DEC_SEQ
N_TOK = N_CTX_TOK + N_LAT_TOK
N_SEQ_ROWS = 16
PROJ_MAIN = 14 * GROUP_WIDTH
PROJ_SMALL = 128
VMEM_LIMIT = 56 * 1024 * 1024


def _seq_row(tile, tm):
    n_ctx = N_CTX_TOK // tm
    per_lat = DEC_SEQ // tm
    return jnp.where(tile < n_ctx, 0, 1 + (tile - n_ctx) // per_lat)


def _ada_body(c_ref, w_ref, b_ref, o_ref):
    cv = c_ref[...]
    s = cv * jax.nn.sigmoid(cv)
    o_ref[0] = jnp.dot(s.astype(BF16), w_ref[0].astype(BF16), preferred_element_type=F32) + b_ref[0]


def _ada(cvec, ada_w, ada_b):
    tn = 1024
    n = N_MOD * D_MODEL
    return pl.pallas_call(
        _ada_body,
        out_shape=jax.ShapeDtypeStruct((DEPTH, N_SEQ_ROWS, n), F32),
        grid=(DEPTH, n // tn),
        in_specs=[
            pl.BlockSpec((N_SEQ_ROWS, D_MODEL), lambda l, j: (0, 0)),
            pl.BlockSpec((1, D_MODEL, tn), lambda l, j: (l, 0, j)),
            pl.BlockSpec((1, 1, tn), lambda l, j: (l, 0, j)),
        ],
        out_specs=pl.BlockSpec((1, N_SEQ_ROWS, tn), lambda l, j: (l, 0, j)),
        compiler_params=pltpu.CompilerParams(
            dimension_semantics=("parallel", "parallel"), vmem_limit_bytes=VMEM_LIMIT),
        name="ada_mod",
    )(cvec, ada_w, ada_b.reshape(DEPTH, 1, n))


def _norm_mod(x, nw, sc, sh):
    ms = jnp.mean(x * x, axis=-1, keepdims=True)
    y = x * lax.rsqrt(ms + EPS) * nw
    return y * (1.0 + sc) + sh


def _inproj_body(x_ref, nw_ref, sc_ref, sh_ref, w_ref, ws_ref, o_ref, os_ref, h_ref):
    @pl.when(pl.program_id(1) == 0)
    def _():
        hb = _norm_mod(x_ref[...], nw_ref[...], sc_ref[...], sh_ref[...]).astype(BF16)
        h_ref[...] = hb
        os_ref[...] = jnp.dot(hb, ws_ref[...], preferred_element_type=F32)

    o_ref[...] = jnp.dot(h_ref[...], w_ref[...], preferred_element_type=F32)


def _inproj(x, nw, mod, w_main, w_small):
    tm, tn = 1024, 1024
    return pl.pallas_call(
        _inproj_body,
        out_shape=(jax.ShapeDtypeStruct((N_TOK, PROJ_MAIN), F32),
                   jax.ShapeDtypeStruct((N_TOK, PROJ_SMALL), F32)),
        grid=(N_TOK // tm, PROJ_MAIN // tn),
        in_specs=[
            pl.BlockSpec((tm, D_MODEL), lambda i, j: (i, 0)),
            pl.BlockSpec((1, D_MODEL), lambda i, j: (0, 0)),
            pl.BlockSpec((None, None, 1, D_MODEL), lambda i, j: (_seq_row(i, tm), 1, 0, 0)),
            pl.BlockSpec((None, None, 1, D_MODEL), lambda i, j: (_seq_row(i, tm), 0, 0, 0)),
            pl.BlockSpec((D_MODEL, tn), lambda i, j: (0, j)),
            pl.BlockSpec((D_MODEL, PROJ_SMALL), lambda i, j: (0, 0)),
        ],
        out_specs=(pl.BlockSpec((tm, tn), lambda i, j: (i, j)),
                   pl.BlockSpec((tm, PROJ_SMALL), lambda i, j: (i, 0))),
        scratch_shapes=[pltpu.VMEM((tm, D_MODEL), BF16)],
        compiler_params=pltpu.CompilerParams(
            dimension_semantics=("parallel", "arbitrary"), vmem_limit_bytes=VMEM_LIMIT),
        name="in_proj",
    )(x, nw, mod, mod, w_main, w_small)


def _outproj_body(m0_ref, m1_ref, m2_ref, m3_ref, w_ref, x_ref, g_ref, o_ref):
    acc = None
    for p, m_ref in enumerate((m0_ref, m1_ref, m2_ref, m3_ref)):
        part = jnp.dot(m_ref[...], w_ref[p * GROUP_WIDTH:(p + 1) * GROUP_WIDTH, :], preferred_element_type=F32)
        acc = part if acc is None else acc + part
    o_ref[...] = x_ref[...] + g_ref[...] * acc


def _outproj(parts, w, x, mod):
    tm, tn = 1024, 1024
    return pl.pallas_call(
        _outproj_body,
        out_shape=jax.ShapeDtypeStruct((N_TOK, D_MODEL), F32),
        grid=(N_TOK // tm, D_MODEL // tn),
        in_specs=[pl.BlockSpec((tm, GROUP_WIDTH), lambda i, j: (i, 0))] * 4 + [
            pl.BlockSpec((D_MODEL, tn), lambda i, j: (0, j)),
            pl.BlockSpec((tm, tn), lambda i, j: (i, j)),
            pl.BlockSpec((None, None, 1, tn), lambda i, j: (_seq_row(i, tm), 2, 0, j)),
        ],
        out_specs=pl.BlockSpec((tm, tn), lambda i, j: (i, j)),
        compiler_params=pltpu.CompilerParams(
            dimension_semantics=("parallel", "arbitrary"), vmem_limit_bytes=VMEM_LIMIT),
        name="out_proj",
    )(*parts, w, x, mod)


def _ffn_body(x_ref, nw_ref, sc_ref, sh_ref, g_ref, w1_ref, w3_ref, w2_ref, fw_ref, o_ref, h_ref, acc_ref,
              *, final_norm):
    k = pl.program_id(1)

    @pl.when(k == 0)
    def _():
        h_ref[...] = _norm_mod(x_ref[...], nw_ref[...], sc_ref[...], sh_ref[...]).astype(BF16)
        acc_ref[...] = jnp.zeros_like(acc_ref)

    h = h_ref[...]
    a = jnp.dot(h, w1_ref[...], preferred_element_type=F32)
    b = jnp.dot(h, w3_ref[...], preferred_element_type=F32)
    g = (a * jax.nn.sigmoid(a) * b).astype(BF16)
    acc_ref[...] += jnp.dot(g, w2_ref[...], preferred_element_type=F32)

    @pl.when(k == pl.num_programs(1) - 1)
    def _():
        y = x_ref[...] + g_ref[...] * acc_ref[...]
        if final_norm:
            ms = jnp.mean(y * y, axis=-1, keepdims=True)
            y = y * lax.rsqrt(ms + EPS) * fw_ref[...]
        o_ref[...] = y


def _ffn(x, nw, mod, w1, w3, w2, fw, final_norm):
    tm, th = 512, 512
    return pl.pallas_call(
        functools.partial(_ffn_body, final_norm=final_norm),
        out_shape=jax.ShapeDtypeStruct((N_TOK, D_MODEL), F32),
        grid=(N_TOK // tm, FFN_HIDDEN // th),
        in_specs=[
            pl.BlockSpec((tm, D_MODEL), lambda i, k: (i, 0)),
            pl.BlockSpec((1, D_MODEL), lambda i, k: (0, 0)),
            pl.BlockSpec((None, None, 1, D_MODEL), lambda i, k: (_seq_row(i, tm), 4, 0, 0)),
            pl.BlockSpec((None, None, 1, D_MODEL), lambda i, k: (_seq_row(i, tm), 3, 0, 0)),
            pl.BlockSpec((None, None, 1, D_MODEL), lambda i, k: (_seq_row(i, tm), 5, 0, 0)),
            pl.BlockSpec((D_MODEL, th), lambda i, k: (0, k)),
            pl.BlockSpec((D_MODEL, th), lambda i, k: (0, k)),
            pl.BlockSpec((th, D_MODEL), lambda i, k: (k, 0)),
            pl.BlockSpec((1, D_MODEL), lambda i, k: (0, 0)),
        ],
        out_specs=pl.BlockSpec((tm, D_MODEL), lambda i, k: (i, 0)),
        scratch_shapes=[pltpu.VMEM((tm, D_MODEL), BF16), pltpu.VMEM((tm, D_MODEL), F32)],
        compiler_params=pltpu.CompilerParams(
            dimension_semantics=("parallel", "arbitrary"), vmem_limit_bytes=VMEM_LIMIT),
        name="ffn",
    )(x, nw, mod, mod, mod, w1, w3, w2, fw)


def _head_layernorm(x):
    mu = jnp.mean(x, axis=-1, keepdims=True)
    xc = x - mu
    return xc * lax.rsqrt(jnp.mean(xc * xc, axis=-1, keepdims=True) + EPS)


def _rmsnorm(x, w):
    y = x * lax.rsqrt(jnp.mean(x * x, axis=-1, keepdims=True) + EPS)
    return y * w


def _l2norm(x):
    return x * lax.rsqrt(jnp.sum(x * x, axis=-1, keepdims=True) + EPS)


def _heads(t, n_heads):
    b, l, _ = t.shape
    return t.reshape(b, l, n_heads, -1).transpose(0, 2, 1, 3)


def _merge(t):
    b, h, l, d = t.shape
    return t.transpose(0, 2, 1, 3).reshape(b, l, h * d)


def _chunks(t, size):
    return t.reshape(t.shape[:2] + (t.shape[2] // size, size) + t.shape[3:])


def _rev(t):
    return jnp.flip(t, axis=2)


def _axial_rope(l):
    n_rows = l // GRID_W
    t_row = jnp.repeat(jnp.arange(n_rows, dtype=F32), GRID_W)
    t_col = jnp.tile(jnp.arange(GRID_W, dtype=F32), n_rows)
    n_freq = HEAD_DIM // 4
    inv = ROPE_BASE ** (-jnp.arange(n_freq, dtype=F32) / n_freq)
    ang = jnp.concatenate([t_row[:, None] * inv, t_col[:, None] * inv], axis=-1)
    return jnp.cos(ang), jnp.sin(ang)


def _apply_rope(x, cos, sin):
    x1, x2 = jnp.split(x, 2, axis=-1)
    return jnp.concatenate([x1 * cos - x2 * sin, x1 * sin + x2 * cos], axis=-1)


def _dwconv(x, w):
    ch = x.shape[-1]
    pad = (GDN_CONV - 1) // 2
    return lax.conv_general_dilated(x, w[:, None, :], window_strides=(1,), padding=[(pad, pad)],
                                    dimension_numbers=('NWC', 'WIO', 'NWC'), feature_group_count=ch)


def _retention(q, k, v, log_gamma, s0):
    b, h, l, d = q.shape
    qc, kc, vc = _chunks(q, CHUNK), _chunks(k, CHUNK), _chunks(v, CHUNK)
    pos = jnp.arange(CHUNK, dtype=F32)
    rel = pos[:, None] - pos[None, :]
    intra = jnp.where(rel >= 0, jnp.exp(jnp.maximum(rel, 0.0) * log_gamma[:, None, None]), 0.0)
    q_dec = jnp.exp((pos + 1.0) * log_gamma[:, None])
    k_dec = jnp.exp((CHUNK - 1.0 - pos) * log_gamma[:, None])
    c_dec = jnp.exp(CHUNK * log_gamma)[None, :, None, None]
    scores = jnp.einsum('bhnid,bhnjd->bhnij', qc, kc) * intra[None, :, None]
    o_intra = jnp.einsum('bhnij,bhnje->bhnie', scores, vc)
    kv = jnp.einsum('bhnjd,hj,bhnje->bhnde', kc, k_dec, vc)

    def step(s, kv_n):
        return c_dec * s + kv_n, s

    s_fin, s_prev = lax.scan(step, s0, jnp.moveaxis(kv, 2, 0))
    o_inter = jnp.einsum('bhnid,hi,bhnde->bhnie', qc, q_dec, jnp.moveaxis(s_prev, 0, 2))
    return (o_intra + o_inter).reshape(b, h, l, d), s_fin


def _gated_delta(q, k, v, log_alpha, beta, s0):
    b, h, l, _ = q.shape
    qc, kc, vc = _chunks(q, CHUNK), _chunks(k, CHUNK), _chunks(v, CHUNK)
    g = jnp.cumsum(_chunks(log_alpha, CHUNK), axis=-1)
    bc = _chunks(beta, CHUNK)
    idx = jnp.arange(CHUNK)
    incl = idx[:, None] >= idx[None, :]
    strict = idx[:, None] > idx[None, :]
    decay = jnp.exp(jnp.where(incl, g[..., :, None] - g[..., None, :], -jnp.inf))
    kb = kc * bc[..., None]
    a_mat = jnp.where(strict, jnp.einsum('bhnid,bhnjd->bhnij', kb, kc) * decay, 0.0)
    eye = jnp.eye(CHUNK, dtype=F32)
    t_mat = lax.linalg.triangular_solve(a_mat + eye, jnp.broadcast_to(eye, a_mat.shape),
                                        left_side=True, lower=True, unit_diagonal=True)
    u = jnp.einsum('bhnij,bhnje->bhnie', t_mat, vc * bc[..., None])
    w = jnp.einsum('bhnij,bhnjd->bhnid', t_mat, kb * jnp.exp(g)[..., None])
    attn = jnp.einsum('bhnid,bhnjd->bhnij', qc, kc) * decay
    qg = qc * jnp.exp(g)[..., None]
    kt = kc * jnp.exp(g[..., -1:] - g)[..., None]
    cd = jnp.exp(g[..., -1])

    def step(s, xs):
        u_n, w_n, qg_n, at_n, kt_n, cd_n = xs
        v_new = u_n - jnp.einsum('bhcd,bhde->bhce', w_n, s)
        o = jnp.einsum('bhcd,bhde->bhce', qg_n, s) + jnp.einsum('bhcs,bhse->bhce', at_n, v_new)
        s = s * cd_n[..., None, None] + jnp.einsum('bhcd,bhce->bhde', kt_n, v_new)
        return s, o

    xs = (jnp.moveaxis(u, 2, 0), jnp.moveaxis(w, 2, 0), jnp.moveaxis(qg, 2, 0),
          jnp.moveaxis(attn, 2, 0), jnp.moveaxis(kt, 2, 0), jnp.moveaxis(cd, 2, 0))
    s_fin, o = lax.scan(step, s0, xs)
    return jnp.moveaxis(o, 0, 2).reshape(b, h, l, -1), s_fin


def _gla(q, k, v, log_f, s0):
    b, h, l, _ = q.shape
    qc, kc, vc, fc = (_chunks(q, HG_CHUNK), _chunks(k, HG_CHUNK), _chunks(v, HG_CHUNK), _chunks(log_f, HG_CHUNK))
    cum = jnp.cumsum(fc, axis=-2)
    idx = jnp.arange(HG_CHUNK)
    incl = (idx[:, None] >= idx[None, :])[:, :, None]
    dec = jnp.exp(jnp.where(incl, cum[..., :, None, :] - cum[..., None, :, :], -jnp.inf))
    attn = jnp.einsum('bhnid,bhnjd,bhnijd->bhnij', qc, kc, dec)
    o_intra = jnp.einsum('bhnij,bhnje->bhnie', attn, vc)
    q_in = qc * jnp.exp(cum)
    k_tail = kc * jnp.exp(cum[..., -1:, :] - cum)
    cd = jnp.exp(cum[..., -1, :])
    kv = jnp.einsum('bhnjd,bhnje->bhnde', k_tail, vc)

    def step(s, xs):
        kv_n, cd_n = xs
        return cd_n[..., None] * s + kv_n, s

    s_fin, s_prev = lax.scan(step, s0, (jnp.moveaxis(kv, 2, 0), jnp.moveaxis(cd, 2, 0)))
    o_inter = jnp.einsum('bhnid,bhnde->bhnie', q_in, jnp.moveaxis(s_prev, 0, 2))
    return (o_intra + o_inter).reshape(b, h, l, -1), s_fin


def _cplx_combine(e1, e2):
    a1r, a1i, b1r, b1i = e1
    a2r, a2i, b2r, b2i = e2
    return (a1r * a2r - a1i * a2i, a1r * a2i + a1i * a2r,
            a2r * b1r - a2i * b1i + b2r, a2r * b1i + a2i * b1r + b2i)


def _s5(u, a_re, a_im, log_step, b_re, b_im, c_re, c_im, x0_re, x0_im):
    dt = jnp.exp(log_step)[:, None]
    mag = jnp.exp(a_re * dt)
    ab_re = mag * jnp.cos(a_im * dt)
    ab_im = mag * jnp.sin(a_im * dt)
    den = a_re * a_re + a_im * a_im
    nr = ab_re - 1.0
    f_re = (nr * a_re + ab_im * a_im) / den
    f_im = (ab_im * a_re - nr * a_im) / den
    bb_re = f_re[..., None] * b_re - f_im[..., None] * b_im
    bb_im = f_re[..., None] * b_im + f_im[..., None] * b_re
    bu_re = jnp.einsum('blgc,gnc->blgn', u, bb_re)
    bu_im = jnp.einsum('blgc,gnc->blgn', u, bb_im)
    bu_re = bu_re.at[:, 0].add(ab_re * x0_re - ab_im * x0_im)
    bu_im = bu_im.at[:, 0].add(ab_re * x0_im + ab_im * x0_re)
    a_full_re = jnp.broadcast_to(ab_re, bu_re.shape)
    a_full_im = jnp.broadcast_to(ab_im, bu_im.shape)
    _, _, x_re, x_im = lax.associative_scan(_cplx_combine, (a_full_re, a_full_im, bu_re, bu_im), axis=1)
    y = jnp.einsum('blgn,gcn->blgc', x_re, c_re) - jnp.einsum('blgn,gcn->blgc', x_im, c_im)
    return y, x_re[:, -1], x_im[:, -1]


def _mixer_jax(pm, ps, lp, lower_bound, s_ret, s_gdn, s_hg, s_s5_re, s_s5_im, rope):
    bsz, l, _ = pm.shape
    w = GROUP_WIDTH
    sl = lambda i, n=1: pm[..., i * w:(i + n) * w]
    r_q, r_k, r_v, r_g, d_q, d_k, d_v, d_g = (sl(i) for i in range(8))
    h_q, h_f, h_i, h_g, s_u = sl(8), sl(9, 2), sl(11), sl(12), sl(13)
    d_a, d_b = ps[..., :8], ps[..., 8:16]

    q = _heads(r_q, N_HEADS)
    k = _heads(r_k, N_HEADS)
    v = _heads(r_v, N_HEADS)
    if rope is not None:
        q = _apply_rope(q, *rope)
        k = _apply_rope(k, *rope)
    k = k * HEAD_DIM ** -0.5
    log_gamma = jax.nn.log_sigmoid(lp['ret_decay_logit'])
    o_f, rs_f = _retention(q, k, v, log_gamma[0], s_ret[:, 0])
    o_b, rs_b = _retention(_rev(q), _rev(k), _rev(v), log_gamma[1], s_ret[:, 1])
    ret_out = _merge(_head_layernorm(o_f + _rev(o_b))) * jax.nn.silu(r_g)

    qkv = jax.nn.silu(_dwconv(jnp.concatenate([d_q, d_k, d_v], axis=-1), lp['gdn_conv']))
    gq, gk, gv = jnp.split(qkv, 3, axis=-1)
    q = _l2norm(_heads(gq, N_HEADS)) * HEAD_DIM ** -0.5
    k = _l2norm(_heads(gk, N_HEADS))
    v = _heads(gv, N_HEADS)
    a = d_a.reshape(bsz, l, 2, N_HEADS)
    bt = d_b.reshape(bsz, l, 2, N_HEADS)
    log_alpha = (-jnp.exp(lp['gdn_a_log']) * jax.nn.softplus(a + lp['gdn_dt_bias'])).transpose(2, 0, 3, 1)
    beta = jax.nn.sigmoid(bt).transpose(2, 0, 3, 1)
    o_f, gs_f = _gated_delta(q, k, v, log_alpha[0], beta[0], s_gdn[:, 0])
    o_b, gs_b = _gated_delta(_rev(q), _rev(k), _rev(v), _rev(log_alpha[1]), _rev(beta[1]), s_gdn[:, 1])
    gdn_out = _merge(_rmsnorm(o_f + _rev(o_b), lp['gdn_norm_w'])) * jax.nn.silu(d_g)

    q = _heads(jax.nn.silu(h_q), N_HEADS) * HEAD_DIM ** -0.5
    v = _heads(h_i, N_HEADS)
    fx = h_f.reshape(bsz, l, 2, w)
    log_f = jnp.logaddexp(jnp.log(jnp.maximum(lower_bound, LB_FLOOR)),
                          jnp.log1p(-lower_bound) + jax.nn.log_sigmoid(fx))
    key_in = (1.0 - lower_bound) * jax.nn.sigmoid(-fx)
    o_f, hs_f = _gla(q, _heads(key_in[:, :, 0], N_HEADS), v, _heads(log_f[:, :, 0], N_HEADS), s_hg[:, 0])
    o_b, hs_b = _gla(_rev(q), _rev(_heads(key_in[:, :, 1], N_HEADS)), _rev(v),
                     _rev(_heads(log_f[:, :, 1], N_HEADS)), s_hg[:, 1])
    hg_out = _merge(_rmsnorm(o_f + _rev(o_b), lp['hg_norm_w'])) * jax.nn.silu(h_g)

    u = s_u.reshape(bsz, l, S5_GROUPS, S5_CH)
    p = [lp[nm] for nm in ('s5_a_re', 's5_a_im', 's5_log_step', 's5_b_re', 's5_b_im', 's5_c_re', 's5_c_im')]
    y_f, xr_f, xi_f = _s5(u, p[0][0], p[1][0], p[2][0], p[3][0], p[4][0], p[5][0], p[6][0],
                          s_s5_re[:, 0], s_s5_im[:, 0])
    y_b, xr_b, xi_b = _s5(jnp.flip(u, 1), p[0][1], p[1][1], p[2][1], p[3][1], p[4][1], p[5][1], p[6][1],
                          s_s5_re[:, 1], s_s5_im[:, 1])
    y = (y_f + jnp.flip(y_b, 1)).reshape(bsz, l, w) + lp['s5_d'] * s_u
    z = jax.nn.gelu(y)
    s5_out = z * jax.nn.sigmoid(z @ lp['s5_glu_w'] + lp['s5_glu_b'])

    mixed = jnp.concatenate([ret_out, gdn_out, hg_out, s5_out], axis=-1).astype(BF16)
    new_states = (jnp.stack([rs_f, rs_b], axis=1), jnp.stack([gs_f, gs_b], axis=1),
                  jnp.stack([hs_f, hs_b], axis=1), jnp.stack([xr_f, xr_b], axis=1),
                  jnp.stack([xi_f, xi_b], axis=1))
    return mixed, new_states


MIX_CHUNK = 128
DIRS = 2
NT_DIMS = (((1,), (1,)), ((), ()))
TN_DIMS = (((0,), (0,)), ((), ()))


def _dot(a, b):
    return jnp.dot(a.astype(BF16), b.astype(BF16), preferred_element_type=F32)


def _dot_nt(a, b):
    return lax.dot_general(a.astype(BF16), b.astype(BF16), NT_DIMS, preferred_element_type=F32)


def _dot_tn(a, b):
    return lax.dot_general(a.astype(BF16), b.astype(BF16), TN_DIMS, preferred_element_type=F32)


def _silu(x):
    return x * jax.nn.sigmoid(x)


def _seq_spec(seq_len, width, row_off, col_blk):
    return pl.BlockSpec((seq_len, width), lambda b: (row_off + b, col_blk))


def _mix_params(n_par):
    return pltpu.CompilerParams(dimension_semantics=("parallel",) * n_par, vmem_limit_bytes=VMEM_LIMIT)


def _ret_body(*refs, n_chunks, use_rope, has_s0):
    it = iter(refs)
    lg_ref, q_ref, k_ref, v_ref, g_ref = (next(it) for _ in range(5))
    cos_ref, sin_ref = (next(it), next(it)) if use_rope else (None, None)
    s0_ref = next(it) if has_s0 else None
    o_ref, sf_ref, acc_ref, s_ref, intra_ref, qd_ref, kd_ref, cd_ref = (next(it) for _ in range(8))
    C, H = MIX_CHUNK, N_HEADS

    row = lax.broadcasted_iota(jnp.int32, (C, C), 0)
    col = lax.broadcasted_iota(jnp.int32, (C, C), 1)
    rel = (row - col).astype(F32)
    pos = lax.broadcasted_iota(jnp.int32, (C, HEAD_DIM), 0).astype(F32)
    for d in range(DIRS):
        for h in range(H):
            lg = lg_ref[d, h]
            if d == 0:
                intra_ref[d, h] = jnp.where(rel >= 0, jnp.exp(jnp.maximum(rel, 0.0) * lg), 0.0)
                qd_ref[d, h] = jnp.exp((pos + 1.0) * lg)
                kd_ref[d, h] = jnp.exp((C - 1.0 - pos) * lg)
            else:
                intra_ref[d, h] = jnp.where(rel <= 0, jnp.exp(jnp.maximum(-rel, 0.0) * lg), 0.0)
                qd_ref[d, h] = jnp.exp((C - pos) * lg)
                kd_ref[d, h] = jnp.exp(pos * lg)
            cd_ref[d, h] = jnp.exp(jnp.full((C, HEAD_DIM), C, F32) * lg)
    if has_s0:
        s_ref[...] = s0_ref[...]
    else:
        s_ref[...] = jnp.zeros_like(s_ref)
    acc_ref[...] = jnp.zeros_like(acc_ref)

    def chunk_step(n, carry):
        for d in range(DIRS):
            c = n if d == 0 else n_chunks - 1 - n
            rows = pl.ds(pl.multiple_of(c * C, C), C)
            for h in range(H):
                cols = slice(h * HEAD_DIM, (h + 1) * HEAD_DIM)
                q = q_ref[rows, cols]
                k = k_ref[rows, cols]
                v = v_ref[rows, cols].astype(BF16)
                if use_rope:
                    cs, sn = cos_ref[rows, :], sin_ref[rows, :]
                    q = q * cs + pltpu.roll(q, HEAD_DIM // 2, 1) * sn
                    k = k * cs + pltpu.roll(k, HEAD_DIM // 2, 1) * sn
                k = k * HEAD_DIM ** -0.5
                s = s_ref[d, h]
                scores = _dot_nt(q, k) * intra_ref[d, h]
                o = _dot(scores, v) + _dot(q * qd_ref[d, h], s)
                acc_ref[rows, cols] += o
                s_ref[d, h] = cd_ref[d, h] * s + _dot_tn(k * kd_ref[d, h], v)
        return carry

    lax.fori_loop(0, n_chunks, chunk_step, 0)
    sf_ref[...] = s_ref[...]

    def finish(n, carry):
        rows = pl.ds(pl.multiple_of(n * C, C), C)
        for h in range(H):
            cols = slice(h * HEAD_DIM, (h + 1) * HEAD_DIM)
            o = acc_ref[rows, cols]
            mu = jnp.mean(o, axis=-1, keepdims=True)
            oc = o - mu
            y = oc * lax.rsqrt(jnp.mean(oc * oc, axis=-1, keepdims=True) + EPS)
            o_ref[rows, cols] = (y * _silu(g_ref[rows, cols])).astype(BF16)
        return carry

    lax.fori_loop(0, n_chunks, finish, 0)


def _retention_pallas(pm, log_gamma, n_seq, seq_len, row_off, rope2, s0, layer):
    use_rope, has_s0 = rope2 is not None, s0 is not None
    st_shape = (DIRS, N_HEADS, HEAD_DIM, HEAD_DIM)
    in_specs = [pl.BlockSpec(memory_space=pltpu.SMEM)]
    in_specs += [_seq_spec(seq_len, GROUP_WIDTH, row_off, cb) for cb in range(4)]
    args = [log_gamma, pm, pm, pm, pm]
    if use_rope:
        in_specs += [pl.BlockSpec((seq_len, HEAD_DIM), lambda b: (0, 0))] * 2
        args += list(rope2)
    if has_s0:
        in_specs.append(pl.BlockSpec((None, None) + st_shape, lambda b: (b, layer, 0, 0, 0, 0)))
        args.append(s0)
    return pl.pallas_call(
        functools.partial(_ret_body, n_chunks=seq_len // MIX_CHUNK, use_rope=use_rope, has_s0=has_s0),
        out_shape=(jax.ShapeDtypeStruct((n_seq * seq_len, GROUP_WIDTH), BF16),
                   jax.ShapeDtypeStruct((n_seq,) + st_shape, F32)),
        grid=(n_seq,),
        in_specs=in_specs,
        out_specs=(pl.BlockSpec((seq_len, GROUP_WIDTH), lambda b: (b, 0)),
                   pl.BlockSpec((None,) + st_shape, lambda b: (b, 0, 0, 0, 0))),
        scratch_shapes=[pltpu.VMEM((seq_len, GROUP_WIDTH), F32), pltpu.VMEM(st_shape, F32),
                        pltpu.VMEM((DIRS, N_HEADS, MIX_CHUNK, MIX_CHUNK), F32),
                        pltpu.VMEM((DIRS, N_HEADS, MIX_CHUNK, HEAD_DIM), F32),
                        pltpu.VMEM((DIRS, N_HEADS, MIX_CHUNK, HEAD_DIM), F32),
                        pltpu.VMEM((DIRS, N_HEADS, MIX_CHUNK, HEAD_DIM), F32)],
        compiler_params=_mix_params(1),
        name="retention",
    )(*args)


def _rope_tables(l):
    cos, sin = _axial_rope(l)
    return jnp.concatenate([cos, cos], axis=-1), jnp.concatenate([-sin, sin], axis=-1)


S5_HALF_G = S5_GROUPS // 2
S5_HALF_U = S5_HALF_G * S5_CH
S5_HALF_X = S5_HALF_G * S5_N
S5_X = S5_GROUPS * S5_N
S5_TC = 256
S5_STEPS = 8


def _s5_tables(a_re, a_im, log_step, b_re, b_im, c_re, c_im):
    dt = jnp.exp(log_step)[..., None]
    mag = jnp.exp(a_re * dt)
    ab_re = mag * jnp.cos(a_im * dt)
    ab_im = mag * jnp.sin(a_im * dt)
    den = a_re * a_re + a_im * a_im
    nr = ab_re - 1.0
    f_re = (nr * a_re + ab_im * a_im) / den
    f_im = (ab_im * a_re - nr * a_im) / den
    bb_re = f_re[..., None] * b_re - f_im[..., None] * b_im
    bb_im = f_re[..., None] * b_im + f_im[..., None] * b_re
    eye = jnp.eye(S5_HALF_G, dtype=F32)

    def in_mat(bb):
        bb = bb.reshape(DIRS, 2, S5_HALF_G, S5_N, S5_CH)
        return jnp.einsum('dhgnc,gk->dhgckn', bb, eye).reshape(DIRS, 2, S5_HALF_U, S5_HALF_X)

    def out_mat(cc):
        cc = cc.reshape(DIRS, 2, S5_HALF_G, S5_CH, S5_N)
        return jnp.einsum('dhgcn,gk->dhgnkc', cc, eye).reshape(DIRS, 2, S5_HALF_X, S5_HALF_U)

    bm = jnp.concatenate([in_mat(bb_re), in_mat(bb_im)], axis=-1).astype(BF16)
    cm = jnp.concatenate([out_mat(c_re), -out_mat(c_im)], axis=-2).astype(BF16)
    steps = (2.0 ** jnp.arange(S5_STEPS, dtype=F32))[None, :, None, None]
    pmag = jnp.exp(steps * (a_re * dt)[:, None])
    pang = steps * (a_im * dt)[:, None]
    pw_re = (pmag * jnp.cos(pang)).reshape(DIRS, S5_STEPS, 1, S5_X)
    pw_im = (pmag * jnp.sin(pang)).reshape(DIRS, S5_STEPS, 1, S5_X)
    return bm, cm, pw_re, pw_im


def _gelu_tanh(x):
    return 0.5 * x * (1.0 + jnp.tanh(math.sqrt(2.0 / math.pi) * (x + 0.044715 * (x * x * x))))


def _s5_body(*refs, seq_len, has_s0):
    it = iter(refs)
    u_ref, bm_ref, cm_ref, pwr_ref, pwi_ref, d_ref, gw_ref, gb_ref = (next(it) for _ in range(8))
    x0r_ref, x0i_ref = (next(it), next(it)) if has_s0 else (None, None)
    o_ref, sfr_ref, sfi_ref, y_ref, xr_ref, xi_ref = (next(it) for _ in range(6))
    tc = min(seq_len, S5_TC)
    n_steps = tc.bit_length() - 1
    n_blocks = seq_len // tc
    rowi = lax.broadcasted_iota(jnp.int32, (tc, S5_HALF_X), 0)

    y_ref[...] = d_ref[...] * u_ref[...]
    for d in range(DIRS):
        for hf in range(2):
            xs = slice(hf * S5_HALF_X, (hf + 1) * S5_HALF_X)
            us = slice(hf * S5_HALF_U, (hf + 1) * S5_HALF_U)
            a_re, a_im = pwr_ref[d, 0, :, xs], pwi_ref[d, 0, :, xs]
            if has_s0:
                car_re, car_im = x0r_ref[d, :, xs], x0i_ref[d, :, xs]
            else:
                car_re = car_im = jnp.zeros((1, S5_HALF_X), F32)
            for blk in range(n_blocks):
                t0 = (blk if d == 0 else n_blocks - 1 - blk) * tc
                first = 0 if d == 0 else tc - 1
                last = tc - 1 if d == 0 else 0
                bu = _dot(u_ref[t0:t0 + tc, us], bm_ref[d, hf])
                xr_ref[...] = bu[:, :S5_HALF_X]
                xi_ref[...] = bu[:, S5_HALF_X:]
                xr_ref[first:first + 1, :] += a_re * car_re - a_im * car_im
                xi_ref[first:first + 1, :] += a_re * car_im + a_im * car_re
                for k in range(n_steps):
                    s = 1 << k
                    p_re, p_im = pwr_ref[d, k, :, xs], pwi_ref[d, k, :, xs]
                    xr, xi = xr_ref[...], xi_ref[...]
                    if d == 0:
                        sr = jnp.where(rowi >= s, pltpu.roll(xr, s, 0), 0.0)
                        si = jnp.where(rowi >= s, pltpu.roll(xi, s, 0), 0.0)
                    else:
                        sr = jnp.where(rowi < tc - s, pltpu.roll(xr, tc - s, 0), 0.0)
                        si = jnp.where(rowi < tc - s, pltpu.roll(xi, tc - s, 0), 0.0)
                    xr_ref[...] = xr + p_re * sr - p_im * si
                    xi_ref[...] = xi + p_re * si + p_im * sr
                y_ref[t0:t0 + tc, us] += (_dot(xr_ref[...], cm_ref[d, hf, :S5_HALF_X, :])
                                          + _dot(xi_ref[...], cm_ref[d, hf, S5_HALF_X:, :]))
                car_re, car_im = xr_ref[last:last + 1, :], xi_ref[last:last + 1, :]
            sfr_ref[d, :, xs] = car_re
            sfi_ref[d, :, xs] = car_im

    z = _gelu_tanh(y_ref[...])
    o_ref[...] = (z * jax.nn.sigmoid(_dot(z, gw_ref[...]) + gb_ref[...])).astype(BF16)


def _s5_pallas(pm, tables, s5_d, glu_w, glu_b, n_seq, seq_len, row_off, x0, layer):
    bm, cm, pw_re, pw_im = tables
    has_s0 = x0 is not None
    full = lambda shape: pl.BlockSpec(shape, lambda b: (0,) * len(shape))
    in_specs = [_seq_spec(seq_len, GROUP_WIDTH, row_off, 13),
                full(bm.shape), full(cm.shape), full(pw_re.shape), full(pw_im.shape),
                full((1, GROUP_WIDTH)), full((GROUP_WIDTH, GROUP_WIDTH)), full((1, GROUP_WIDTH))]
    args = [pm, bm, cm, pw_re, pw_im, s5_d.reshape(1, GROUP_WIDTH), glu_w.astype(BF16),
            glu_b.reshape(1, GROUP_WIDTH)]
    if has_s0:
        in_specs += [pl.BlockSpec((None, None, DIRS, 1, S5_X), lambda b: (b, layer, 0, 0, 0))] * 2
        args += list(x0)
    st = jax.ShapeDtypeStruct((n_seq, DIRS, 1, S5_X), F32)
    st_spec = pl.BlockSpec((None, DIRS, 1, S5_X), lambda b: (b, 0, 0, 0))
    tc = min(seq_len, S5_TC)
    return pl.pallas_call(
        functools.partial(_s5_body, seq_len=seq_len, has_s0=has_s0),
        out_shape=(jax.ShapeDtypeStruct((n_seq * seq_len, GROUP_WIDTH), BF16), st, st),
        grid=(n_seq,),
        in_specs=in_specs,
        out_specs=(pl.BlockSpec((seq_len, GROUP_WIDTH), lambda b: (b, 0)), st_spec, st_spec),
        scratch_shapes=[pltpu.VMEM((seq_len, GROUP_WIDTH), F32),
                        pltpu.VMEM((tc, S5_HALF_X), F32), pltpu.VMEM((tc, S5_HALF_X), F32)],
        compiler_params=_mix_params(1),
        name="s5",
    )(*args)


GLA_LEVELS = 7


def _log_sigmoid(x):
    return jnp.minimum(x, 0.0) - jnp.log1p(jnp.exp(-jnp.abs(x)))


def _logaddexp(a, b):
    return jnp.maximum(a, b) + jnp.log1p(jnp.exp(-jnp.abs(a - b)))


def _chunk_cumsum_rows(x, rowi):
    s = 1
    while s < MIX_CHUNK:
        x = x + jnp.where(rowi >= s, pltpu.roll(x, s, 0), 0.0)
        s *= 2
    return x


def _gla_body(*refs, n_chunks, has_s0):
    it = iter(refs)
    q_ref, f0_ref, f1_ref, i_ref, g_ref, lb_ref, nw_ref = (next(it) for _ in range(7))
    s0_ref = next(it) if has_s0 else None
    o_ref, sf_ref, acc_ref, s_ref, code_ref = (next(it) for _ in range(5))
    C, H = MIX_CHUNK, N_HEADS
    f_refs = (f0_ref, f1_ref)

    rowi = lax.broadcasted_iota(jnp.int32, (C, HEAD_DIM), 0)
    ri = lax.broadcasted_iota(jnp.int32, (C, C), 0)
    ci = lax.broadcasted_iota(jnp.int32, (C, C), 1)
    top_bit = 31 - lax.clz(ri ^ ci)
    code_ref[...] = jnp.where(ri > ci, top_bit, jnp.where(ri < ci, -1 - top_bit, GLA_LEVELS))
    if has_s0:
        s_ref[...] = s0_ref[...]
    else:
        s_ref[...] = jnp.zeros_like(s_ref)
    acc_ref[...] = jnp.zeros_like(acc_ref)

    def chunk_step(n, carry):
        for d in range(DIRS):
            c = n if d == 0 else n_chunks - 1 - n
            rows = pl.ds(pl.multiple_of(c * C, C), C)
            for h in range(H):
                cols = slice(h * HEAD_DIM, (h + 1) * HEAD_DIM)
                code = code_ref[...]
                q = _silu(q_ref[rows, cols]) * HEAD_DIM ** -0.5
                v = i_ref[rows, cols].astype(BF16)
                fx = f_refs[d][rows, cols]
                lb = lb_ref[d:d + 1, cols]
                logf = _logaddexp(jnp.log(jnp.maximum(lb, LB_FLOOR)), jnp.log1p(-lb) + _log_sigmoid(fx))
                k = (1.0 - lb) * jax.nn.sigmoid(-fx)
                cum = _chunk_cumsum_rows(logf, rowi)
                own = cum
                attn = jnp.where(code == GLA_LEVELS, _dot_nt(q, k), 0.0)
                for lvl in range(GLA_LEVELS):
                    m = 1 << lvl
                    prev = pltpu.roll(own, m, 0)
                    pre = jnp.minimum(cum - prev, 0.0)
                    suf = jnp.minimum(own - cum, 0.0)
                    if d == 0:
                        sc = _dot_nt(q * jnp.exp(pre), k * jnp.exp(suf))
                        hit = code == lvl
                    else:
                        sc = _dot_nt(q * jnp.exp(jnp.minimum(suf + logf, 0.0)),
                                     k * jnp.exp(jnp.minimum(pre - logf, 0.0)))
                        hit = code == -1 - lvl
                    attn = jnp.where(hit, sc, attn)
                    own = jnp.where(((rowi >> lvl) & 1) == 0, pltpu.roll(own, C - m, 0), own)
                tot = own
                st = s_ref[d, h]
                if d == 0:
                    q_in, k_out = q * jnp.exp(cum), k * jnp.exp(tot - cum)
                else:
                    q_in, k_out = q * jnp.exp(tot - cum + logf), k * jnp.exp(cum - logf)
                acc_ref[rows, cols] += _dot(attn, v) + _dot_nt(q_in, st)
                s_ref[d, h] = jnp.exp(tot[0:1, :]) * st + _dot_tn(v, k_out)
        return carry

    lax.fori_loop(0, n_chunks, chunk_step, 0)
    sf_ref[...] = s_ref[...]

    def finish(n, carry):
        rows = pl.ds(pl.multiple_of(n * C, C), C)
        for h in range(H):
            cols = slice(h * HEAD_DIM, (h + 1) * HEAD_DIM)
            o = acc_ref[rows, cols]
            y = o * lax.rsqrt(jnp.mean(o * o, axis=-1, keepdims=True) + EPS) * nw_ref[...]
            o_ref[rows, cols] = (y * _silu(g_ref[rows, cols])).astype(BF16)
        return carry

    lax.fori_loop(0, n_chunks, finish, 0)


def _gla_pallas(pm, lower_bound, norm_w, n_seq, seq_len, row_off, s0_t, layer):
    has_s0 = s0_t is not None
    st_shape = (DIRS, N_HEADS, HEAD_DIM, HEAD_DIM)
    in_specs = [_seq_spec(seq_len, GROUP_WIDTH, row_off, cb) for cb in (8, 9, 10, 11, 12)]
    in_specs += [pl.BlockSpec((DIRS, GROUP_WIDTH), lambda b: (0, 0)), pl.BlockSpec((1, HEAD_DIM), lambda b: (0, 0))]
    args = [pm] * 5 + [lower_bound, norm_w.reshape(1, HEAD_DIM)]
    if has_s0:
        in_specs.append(pl.BlockSpec((None, None) + st_shape, lambda b: (b, layer, 0, 0, 0, 0)))
        args.append(s0_t)
    return pl.pallas_call(
        functools.partial(_gla_body, n_chunks=seq_len // MIX_CHUNK, has_s0=has_s0),
        out_shape=(jax.ShapeDtypeStruct((n_seq * seq_len, GROUP_WIDTH), BF16),
                   jax.ShapeDtypeStruct((n_seq,) + st_shape, F32)),
        grid=(n_seq,),
        in_specs=in_specs,
        out_specs=(pl.BlockSpec((seq_len, GROUP_WIDTH), lambda b: (b, 0)),
                   pl.BlockSpec((None,) + st_shape, lambda b: (b, 0, 0, 0, 0))),
        scratch_shapes=[pltpu.VMEM((seq_len, GROUP_WIDTH), F32), pltpu.VMEM(st_shape, F32),
                        pltpu.VMEM((MIX_CHUNK, MIX_CHUNK), jnp.int32)],
        compiler_params=_mix_params(1),
        name="hgrn2",
    )(*args)


GDN_BLOCK_BITS = 4
GDN_MERGES = 3
N_GATES = DIRS * N_HEADS


def _softplus(x):
    return jnp.maximum(x, 0.0) + jnp.log1p(jnp.exp(-jnp.abs(x)))


def _split_bf16(a):
    hi = a.astype(BF16)
    return hi, (a - hi.astype(F32)).astype(BF16)


def _dot3(a, b):
    a1, a2 = _split_bf16(a)
    b1, b2 = _split_bf16(b)
    dot = functools.partial(jnp.dot, preferred_element_type=F32)
    return dot(a1, b1) + (dot(a1, b2) + dot(a2, b1))


def _unit_tri_inverse(a, blev, eye):
    b = -jnp.where(blev == 0, a, 0.0)
    p = eye + b
    for _ in range(GDN_BLOCK_BITS - 1):
        b = _dot3(b, b)
        p = p + _dot3(p, b)
    for lvl in range(1, GDN_MERGES + 1):
        e = jnp.where(blev == lvl, a, 0.0)
        p = p - _dot3(p, _dot3(e, p))
    return p


def _gdn_body(*refs, seq_len, has_s0):
    it = iter(refs)
    (q_ref, k_ref, v_ref, g_ref, ps_ref, pst_ref, cw_ref, prow_ref, pcol_ref, nw_ref) = (next(it) for _ in range(10))
    s0_ref = next(it) if has_s0 else None
    (o_ref, sf_ref, acc_ref, s_ref, qn_ref, kn_ref, vn_ref, gate_ref, gatet_ref, blev_ref) = (
        next(it) for _ in range(10))
    C, H, L = MIX_CHUNK, N_HEADS, seq_len
    n_chunks = L // C
    w = GROUP_WIDTH

    rowl = lax.broadcasted_iota(jnp.int32, (L, HEAD_DIM), 0)
    for part, (src, dst) in enumerate(((q_ref, qn_ref), (k_ref, kn_ref), (v_ref, vn_ref))):
        for h in range(H):
            cols = slice(h * HEAD_DIM, (h + 1) * HEAD_DIM)
            wc = slice(part * w + h * HEAD_DIM, part * w + (h + 1) * HEAD_DIM)
            x = src[:, cols]
            x_prev = jnp.where(rowl >= 1, pltpu.roll(x, 1, 0), 0.0)
            x_next = jnp.where(rowl < L - 1, pltpu.roll(x, L - 1, 0), 0.0)
            y = _silu(cw_ref[0:1, wc] * x_prev + cw_ref[1:2, wc] * x + cw_ref[2:3, wc] * x_next)
            if part < 2:
                y = y * lax.rsqrt(jnp.sum(y * y, axis=-1, keepdims=True) + EPS)
            if part == 0:
                y = y * HEAD_DIM ** -0.5
            dst[:, cols] = y

    rowi = lax.broadcasted_iota(jnp.int32, (C, HEAD_DIM), 0)
    lane = lax.broadcasted_iota(jnp.int32, (C, HEAD_DIM), 1)
    lane_t = lax.broadcasted_iota(jnp.int32, (2 * N_GATES, C), 1)
    row_t = lax.broadcasted_iota(jnp.int32, (2 * N_GATES, C), 0)
    for c in range(n_chunks):
        rows = slice(c * C, (c + 1) * C)
        a = ps_ref[rows, :]
        la = -jnp.exp(prow_ref[0:1, :]) * _softplus(a + prow_ref[1:2, :])
        pre = _chunk_cumsum_rows(la, rowi)
        tot = jnp.broadcast_to(pre[C - 1:C, :], (C, HEAD_DIM))
        g = jnp.where(lane < N_HEADS, pre, tot - pre + la)
        gate_ref[rows, :] = jnp.where(lane < N_GATES, g,
                                      jnp.where(lane < 2 * N_GATES, jax.nn.sigmoid(a),
                                                pltpu.roll(tot, 2 * N_GATES, 1)))
        at = pst_ref[:, rows]
        lat = -jnp.exp(pcol_ref[:, 0:1]) * _softplus(at + pcol_ref[:, 1:2])
        pre_t, s = lat, 1
        while s < C:
            pre_t = pre_t + jnp.where(lane_t >= s, pltpu.roll(pre_t, s, 1), 0.0)
            s *= 2
        tot_t = jnp.broadcast_to(pre_t[:, C - 1:C], (2 * N_GATES, C))
        g_t = jnp.where(row_t < N_HEADS, pre_t, tot_t - pre_t + lat)
        gatet_ref[:, rows] = jnp.where(row_t < N_GATES, g_t, jax.nn.sigmoid(at))

    ri = lax.broadcasted_iota(jnp.int32, (C, C), 0)
    ci = lax.broadcasted_iota(jnp.int32, (C, C), 1)
    bx = (ri >> GDN_BLOCK_BITS) ^ (ci >> GDN_BLOCK_BITS)
    blev_ref[...] = jnp.where(bx == 0, 0, 32 - lax.clz(bx))
    if has_s0:
        s_ref[...] = s0_ref[...]
    else:
        s_ref[...] = jnp.zeros_like(s_ref)
    acc_ref[...] = jnp.zeros_like(acc_ref)

    def chunk_step(n, carry):
        eye = jnp.where(ri == ci, 1.0, 0.0)
        blev = blev_ref[...]
        for d in range(DIRS):
            c = n if d == 0 else n_chunks - 1 - n
            rows = pl.ds(pl.multiple_of(c * C, C), C)
            incl = (ri >= ci) if d == 0 else (ri <= ci)
            strict = (ri > ci) if d == 0 else (ri < ci)
            for h in range(H):
                cols = slice(h * HEAD_DIM, (h + 1) * HEAD_DIM)
                cg = d * N_HEADS + h
                q, k, v = qn_ref[rows, cols], kn_ref[rows, cols], vn_ref[rows, cols]
                g_i = jnp.broadcast_to(gate_ref[rows, cg:cg + 1], (C, C))
                b_i = jnp.broadcast_to(gate_ref[rows, N_GATES + cg:N_GATES + cg + 1], (C, C))
                tot = jnp.broadcast_to(gate_ref[rows, 2 * N_GATES + cg:2 * N_GATES + cg + 1], (C, C))
                g_j = gatet_ref[cg:cg + 1, rows]
                b_j = gatet_ref[N_GATES + cg:N_GATES + cg + 1, rows]
                decay = jnp.where(incl, jnp.exp(jnp.minimum(g_i - g_j, 0.0)), 0.0)
                a_mat = jnp.where(strict, _dot_nt(k, k) * b_i * decay, 0.0)
                t_mat = _unit_tri_inverse(a_mat, blev, eye)
                tb = t_mat * b_j
                u = _dot(tb, v)
                wk = _dot(tb * jnp.exp(g_j), k)
                attn = _dot_nt(q, k) * decay
                s = s_ref[d, h]
                v_new = u - _dot(wk, s)
                acc_ref[rows, cols] += _dot(q * jnp.exp(g_i), s) + _dot(attn, v_new)
                s_ref[d, h] = s * jnp.exp(tot) + _dot_tn(k * jnp.exp(tot - g_i), v_new)
        return carry

    lax.fori_loop(0, n_chunks, chunk_step, 0)
    sf_ref[...] = s_ref[...]

    def finish(n, carry):
        rows = pl.ds(pl.multiple_of(n * C, C), C)
        for h in range(H):
            cols = slice(h * HEAD_DIM, (h + 1) * HEAD_DIM)
            o = acc_ref[rows, cols]
            y = o * lax.rsqrt(jnp.mean(o * o, axis=-1, keepdims=True) + EPS) * nw_ref[...]
            o_ref[rows, cols] = (y * _silu(g_ref[rows, cols])).astype(BF16)
        return carry

    lax.fori_loop(0, n_chunks, finish, 0)


def _gdn_pallas(pm, ps, ps_t, conv_w, a_log, dt_bias, norm_w, n_seq, seq_len, row_off, s0, layer):
    has_s0 = s0 is not None
    st_shape = (DIRS, N_HEADS, HEAD_DIM, HEAD_DIM)
    par = jnp.stack([a_log.reshape(N_GATES), dt_bias.reshape(N_GATES)])
    par_row = jnp.pad(par, ((0, 0), (0, HEAD_DIM - N_GATES)))
    par_col = jnp.pad(par.T, ((0, N_GATES), (0, 0)))
    full = lambda shape: pl.BlockSpec(shape, lambda b: (0,) * len(shape))
    in_specs = [_seq_spec(seq_len, GROUP_WIDTH, row_off, cb) for cb in (4, 5, 6, 7)]
    in_specs += [_seq_spec(seq_len, PROJ_SMALL, row_off, 0),
                 pl.BlockSpec((2 * N_GATES, seq_len), lambda b: (0, row_off + b)),
                 full((GDN_CONV, 3 * GROUP_WIDTH)), full((2, HEAD_DIM)), full((2 * N_GATES, 2)), full((1, HEAD_DIM))]
    args = [pm] * 4 + [ps, ps_t, conv_w, par_row, par_col, norm_w.reshape(1, HEAD_DIM)]
    if has_s0:
        in_specs.append(pl.BlockSpec((None, None) + st_shape, lambda b: (b, layer, 0, 0, 0, 0)))
        args.append(s0)
    seq_f32 = lambda width: pltpu.VMEM((seq_len, width), F32)
    return pl.pallas_call(
        functools.partial(_gdn_body, seq_len=seq_len, has_s0=has_s0),
        out_shape=(jax.ShapeDtypeStruct((n_seq * seq_len, GROUP_WIDTH), BF16),
                   jax.ShapeDtypeStruct((n_seq,) + st_shape, F32)),
        grid=(n_seq,),
        in_specs=in_specs,
        out_specs=(pl.BlockSpec((seq_len, GROUP_WIDTH), lambda b: (b, 0)),
                   pl.BlockSpec((None,) + st_shape, lambda b: (b, 0, 0, 0, 0))),
        scratch_shapes=[seq_f32(GROUP_WIDTH), pltpu.VMEM(st_shape, F32),
                        seq_f32(GROUP_WIDTH), seq_f32(GROUP_WIDTH), seq_f32(GROUP_WIDTH),
                        seq_f32(HEAD_DIM), pltpu.VMEM((2 * N_GATES, seq_len), F32),
                        pltpu.VMEM((MIX_CHUNK, MIX_CHUNK), jnp.int32)],
        compiler_params=_mix_params(1),
        name="gated_delta",
    )(*args)


def kernel(x_prompt, x_sample, state_ret, state_gdn, state_hgrn, state_s5_re, state_s5_im, c, c_ctx, norm1_w, norm2_w, final_norm_w, ada_w, ada_b, in_proj, out_proj, ret_decay_logit, gdn_conv, gdn_a_log, gdn_dt_bias, gdn_norm_w, hg_lb_param, hg_norm_w, s5_a_re, s5_a_im, s5_b_re, s5_b_im, s5_c_re, s5_c_im, s5_log_step, s5_d, s5_glu_w, s5_glu_b, ffn_w1, ffn_w3, ffn_w2):
    lb_soft = jax.nn.softmax(hg_lb_param, axis=0)
    lower_bounds = jnp.cumsum(lb_soft, axis=0) - lb_soft[0]
    rope2 = _rope_tables(DEC_SEQ)

    cvec = jnp.zeros((N_SEQ_ROWS, D_MODEL), F32).at[0].set(c_ctx).at[1:1 + DEC_BATCH].set(c)
    mod_all = _ada(cvec, ada_w, ada_b).reshape(DEPTH, N_SEQ_ROWS, N_MOD, 1, D_MODEL)

    w_main = jnp.concatenate([in_proj[:, :, :8 * GROUP_WIDTH], in_proj[:, :, 8 * GROUP_WIDTH + 16:]],
                             axis=-1).astype(BF16)
    w_small = jnp.pad(in_proj[:, :, 8 * GROUP_WIDTH:8 * GROUP_WIDTH + 16],
                      ((0, 0), (0, 0), (0, PROJ_SMALL - 16))).astype(BF16)
    w_out = out_proj.astype(BF16)
    w1, w3, w2 = ffn_w1.astype(BF16), ffn_w3.astype(BF16), ffn_w2.astype(BF16)

    x = jnp.concatenate([x_prompt.reshape(N_CTX_TOK, D_MODEL), x_sample.reshape(N_LAT_TOK, D_MODEL)], axis=0)
    hgrn_t = jnp.swapaxes(state_hgrn, -1, -2)
    s5_x0 = (state_s5_re.reshape(DEC_BATCH, DEPTH, DIRS, 1, S5_X), state_s5_im.reshape(DEC_BATCH, DEPTH, DIRS, 1, S5_X))
    lat_off = N_CTX_TOK // DEC_SEQ
    ctx_states = []
    for i in range(DEPTH):
        mod = mod_all[i]
        pm, ps = _inproj(x, norm1_w[i][None], mod, w_main[i], w_small[i])
        ps_t = ps[:, :2 * N_GATES].T
        lg = jax.nn.log_sigmoid(ret_decay_logit[i])
        tables = _s5_tables(s5_a_re[i], s5_a_im[i], s5_log_step[i], s5_b_re[i], s5_b_im[i], s5_c_re[i], s5_c_im[i])
        ctx = (BATCH, SEQ, 0)
        lat = (DEC_BATCH, DEC_SEQ, lat_off)

        ret_c, rs = _retention_pallas(pm, lg, *ctx, None, None, i)
        ret_l, _ = _retention_pallas(pm, lg, *lat, rope2, state_ret, i)
        gdn_args = (pm, ps, ps_t, gdn_conv[i], gdn_a_log[i], gdn_dt_bias[i], gdn_norm_w[i])
        gdn_c, gs = _gdn_pallas(*gdn_args, *ctx, None, i)
        gdn_l, _ = _gdn_pallas(*gdn_args, *lat, state_gdn, i)
        hg_c, hs = _gla_pallas(pm, lower_bounds[i], hg_norm_w[i], *ctx, None, i)
        hg_l, _ = _gla_pallas(pm, lower_bounds[i], hg_norm_w[i], *lat, hgrn_t, i)
        s5_args = (pm, tables, s5_d[i], s5_glu_w[i], s5_glu_b[i])
        s5_c, xr, xi = _s5_pallas(*s5_args, *ctx, None, i)
        s5_l, _, _ = _s5_pallas(*s5_args, *lat, s5_x0, i)
        ctx_states.append((rs, gs, jnp.swapaxes(hs, -1, -2), xr.reshape(BATCH, DIRS, S5_GROUPS, S5_N),
                           xi.reshape(BATCH, DIRS, S5_GROUPS, S5_N)))

        parts = [jnp.concatenate(pair, axis=0) for pair in ((ret_c, ret_l), (gdn_c, gdn_l), (hg_c, hg_l), (s5_c, s5_l))]
        x = _outproj(parts, w_out[i], x, mod)
        x = _ffn(x, norm2_w[i][None], mod, w1[i], w3[i], w2[i], final_norm_w[None], i == DEPTH - 1)

    y_prompt = x[:N_CTX_TOK].reshape(BATCH, SEQ, D_MODEL)
    y_sample = x[N_CTX_TOK:].reshape(DEC_BATCH, DEC_SEQ, D_MODEL)
    new_states = tuple(jnp.stack([s[j] for s in ctx_states], axis=1) for j in range(5))
    return (y_prompt, y_sample) + new_states
```

```python
import functools
import math

import jax
import jax.numpy as jnp
from jax import lax
from jax.experimental import pallas as pl
from jax.experimental.pallas import tpu as pltpu

F32 = jnp.float32
BF16 = jnp.bfloat16

D_MODEL = 2048
BATCH = 16
SEQ = 256
DEPTH = 2
DEC_BATCH = 8
DEC_SEQ = 1024
GRID_W = 64
HEAD_DIM = 128
GROUP_WIDTH = 512
N_HEADS = 4
S5_CH = 16
S5_GROUPS = 32
S5_N = 64
GDN_CONV = 3
CHUNK = 64
HG_CHUNK = 16
ROPE_BASE = 10000.0
FFN_HIDDEN = 5632
N_MOD = 6
EPS = 1e-6
LB_FLOOR = 1e-30

N_CTX_TOK = BATCH * SEQ
N_LAT_TOK = DEC_BATCH * DEC_SEQ
N_TOK = N_CTX_TOK + N_LAT_TOK
N_SEQ_ROWS = 16
PROJ_MAIN = 14 * GROUP_WIDTH
PROJ_SMALL = 128
VMEM_LIMIT = 56 * 1024 * 1024


def _seq_row(tile, tm):
    n_ctx = N_CTX_TOK // tm
    per_lat = DEC_SEQ // tm
    return jnp.where(tile < n_ctx, 0, 1 + (tile - n_ctx) // per_lat)


def _ada_body(c_ref, w_ref, b_ref, o_ref):
    cv = c_ref[...]
    s = cv * jax.nn.sigmoid(cv)
    o_ref[0] = jnp.dot(s.astype(BF16), w_ref[0].astype(BF16), preferred_element_type=F32) + b_ref[0]


def _ada(cvec, ada_w, ada_b):
    tn = 1024
    n = N_MOD * D_MODEL
    return pl.pallas_call(
        _ada_body,
        out_shape=jax.ShapeDtypeStruct((DEPTH, N_SEQ_ROWS, n), F32),
        grid=(DEPTH, n // tn),
        in_specs=[
            pl.BlockSpec((N_SEQ_ROWS, D_MODEL), lambda l, j: (0, 0)),
            pl.BlockSpec((1, D_MODEL, tn), lambda l, j: (l, 0, j)),
            pl.BlockSpec((1, 1, tn), lambda l, j: (l, 0, j)),
        ],
        out_specs=pl.BlockSpec((1, N_SEQ_ROWS, tn), lambda l, j: (l, 0, j)),
        compiler_params=pltpu.CompilerParams(
            dimension_semantics=("parallel", "parallel"), vmem_limit_bytes=VMEM_LIMIT),
        name="ada_mod",
    )(cvec, ada_w, ada_b.reshape(DEPTH, 1, n))


def _norm_mod(x, nw, sc, sh):
    ms = jnp.mean(x * x, axis=-1, keepdims=True)
    y = x * lax.rsqrt(ms + EPS) * nw
    return y * (1.0 + sc) + sh


def _inproj_body(x_ref, nw_ref, sc_ref, sh_ref, w_ref, ws_ref, o_ref, os_ref, h_ref):
    @pl.when(pl.program_id(1) == 0)
    def _():
        hb = _norm_mod(x_ref[...], nw_ref[...], sc_ref[...], sh_ref[...]).astype(BF16)
        h_ref[...] = hb
        os_ref[...] = jnp.dot(hb, ws_ref[...], preferred_element_type=F32)

    o_ref[...] = jnp.dot(h_ref[...], w_ref[...], preferred_element_type=F32)


def _inproj(x, nw, mod, w_main, w_small):
    tm, tn = 1024, 1024
    return pl.pallas_call(
        _inproj_body,
        out_shape=(jax.ShapeDtypeStruct((N_TOK, PROJ_MAIN), F32),
                   jax.ShapeDtypeStruct((N_TOK, PROJ_SMALL), F32)),
        grid=(N_TOK // tm, PROJ_MAIN // tn),
        in_specs=[
            pl.BlockSpec((tm, D_MODEL), lambda i, j: (i, 0)),
            pl.BlockSpec((1, D_MODEL), lambda i, j: (0, 0)),
            pl.BlockSpec((None, None, 1, D_MODEL), lambda i, j: (_seq_row(i, tm), 1, 0, 0)),
            pl.BlockSpec((None, None, 1, D_MODEL), lambda i, j: (_seq_row(i, tm), 0, 0, 0)),
            pl.BlockSpec((D_MODEL, tn), lambda i, j: (0, j)),
            pl.BlockSpec((D_MODEL, PROJ_SMALL), lambda i, j: (0, 0)),
        ],
        out_specs=(pl.BlockSpec((tm, tn), lambda i, j: (i, j)),
                   pl.BlockSpec((tm, PROJ_SMALL), lambda i, j: (i, 0))),
        scratch_shapes=[pltpu.VMEM((tm, D_MODEL), BF16)],
        compiler_params=pltpu.CompilerParams(
            dimension_semantics=("parallel", "arbitrary"), vmem_limit_bytes=VMEM_LIMIT),
        name="in_proj",
    )(x, nw, mod, mod, w_main, w_small)


def _outproj_body(m0_ref, m1_ref, m2_ref, m3_ref, w_ref, x_ref, g_ref, o_ref):
    acc = None
    for p, m_ref in enumerate((m0_ref, m1_ref, m2_ref, m3_ref)):
        part = jnp.dot(m_ref[...], w_ref[p * GROUP_WIDTH:(p + 1) * GROUP_WIDTH, :], preferred_element_type=F32)
        acc = part if acc is None else acc + part
    o_ref[...] = x_ref[...] + g_ref[...] * acc


def _outproj(parts, w, x, mod):
    tm, tn = 1024, 1024
    return pl.pallas_call(
        _outproj_body,
        out_shape=jax.ShapeDtypeStruct((N_TOK, D_MODEL), F32),
        grid=(N_TOK // tm, D_MODEL // tn),
        in_specs=[pl.BlockSpec((tm, GROUP_WIDTH), lambda i, j: (i, 0))] * 4 + [
            pl.BlockSpec((D_MODEL, tn), lambda i, j: (0, j)),
            pl.BlockSpec((tm, tn), lambda i, j: (i, j)),
            pl.BlockSpec((None, None, 1, tn), lambda i, j: (_seq_row(i, tm), 2, 0, j)),
        ],
        out_specs=pl.BlockSpec((tm, tn), lambda i, j: (i, j)),
        compiler_params=pltpu.CompilerParams(
            dimension_semantics=("parallel", "arbitrary"), vmem_limit_bytes=VMEM_LIMIT),
        name="out_proj",
    )(*parts, w, x, mod)


def _ffn_body(x_ref, nw_ref, sc_ref, sh_ref, g_ref, w1_ref, w3_ref, w2_ref, fw_ref, o_ref, h_ref, acc_ref,
              *, final_norm):
    k = pl.program_id(1)

    @pl.when(k == 0)
    def _():
        h_ref[...] = _norm_mod(x_ref[...], nw_ref[...], sc_ref[...], sh_ref[...]).astype(BF16)
        acc_ref[...] = jnp.zeros_like(acc_ref)

    h = h_ref[...]
    a = jnp.dot(h, w1_ref[...], preferred_element_type=F32)
    b = jnp.dot(h, w3_ref[...], preferred_element_type=F32)
    g = (a * jax.nn.sigmoid(a) * b).astype(BF16)
    acc_ref[...] += jnp.dot(g, w2_ref[...], preferred_element_type=F32)

    @pl.when(k == pl.num_programs(1) - 1)
    def _():
        y = x_ref[...] + g_ref[...] * acc_ref[...]
        if final_norm:
            ms = jnp.mean(y * y, axis=-1, keepdims=True)
            y = y * lax.rsqrt(ms + EPS) * fw_ref[...]
        o_ref[...] = y


def _ffn(x, nw, mod, w1, w3, w2, fw, final_norm):
    tm, th = 512, 512
    return pl.pallas_call(
        functools.partial(_ffn_body, final_norm=final_norm),
        out_shape=jax.ShapeDtypeStruct((N_TOK, D_MODEL), F32),
        grid=(N_TOK // tm, FFN_HIDDEN // th),
        in_specs=[
            pl.BlockSpec((tm, D_MODEL), lambda i, k: (i, 0)),
            pl.BlockSpec((1, D_MODEL), lambda i, k: (0, 0)),
            pl.BlockSpec((None, None, 1, D_MODEL), lambda i, k: (_seq_row(i, tm), 4, 0, 0)),
            pl.BlockSpec((None, None, 1, D_MODEL), lambda i, k: (_seq_row(i, tm), 3, 0, 0)),
            pl.BlockSpec((None, None, 1, D_MODEL), lambda i, k: (_seq_row(i, tm), 5, 0, 0)),
            pl.BlockSpec((D_MODEL, th), lambda i, k: (0, k)),
            pl.BlockSpec((D_MODEL, th), lambda i, k: (0, k)),
            pl.BlockSpec((th, D_MODEL), lambda i, k: (k, 0)),
            pl.BlockSpec((1, D_MODEL), lambda i, k: (0, 0)),
        ],
        out_specs=pl.BlockSpec((tm, D_MODEL), lambda i, k: (i, 0)),
        scratch_shapes=[pltpu.VMEM((tm, D_MODEL), BF16), pltpu.VMEM((tm, D_MODEL), F32)],
        compiler_params=pltpu.CompilerParams(
            dimension_semantics=("parallel", "arbitrary"), vmem_limit_bytes=VMEM_LIMIT),
        name="ffn",
    )(x, nw, mod, mod, mod, w1, w3, w2, fw)


def _head_layernorm(x):
    mu = jnp.mean(x, axis=-1, keepdims=True)
    xc = x - mu
    return xc * lax.rsqrt(jnp.mean(xc * xc, axis=-1, keepdims=True) + EPS)


def _rmsnorm(x, w):
    y = x * lax.rsqrt(jnp.mean(x * x, axis=-1, keepdims=True) + EPS)
    return y * w


def _l2norm(x):
    return x * lax.rsqrt(jnp.sum(x * x, axis=-1, keepdims=True) + EPS)


def _heads(t, n_heads):
    b, l, _ = t.shape
    return t.reshape(b, l, n_heads, -1).transpose(0, 2, 1, 3)


def _merge(t):
    b, h, l, d = t.shape
    return t.transpose(0, 2, 1, 3).reshape(b, l, h * d)


def _chunks(t, size):
    return t.reshape(t.shape[:2] + (t.shape[2] // size, size) + t.shape[3:])


def _rev(t):
    return jnp.flip(t, axis=2)


def _axial_rope(l):
    n_rows = l // GRID_W
    t_row = jnp.repeat(jnp.arange(n_rows, dtype=F32), GRID_W)
    t_col = jnp.tile(jnp.arange(GRID_W, dtype=F32), n_rows)
    n_freq = HEAD_DIM // 4
    inv = ROPE_BASE ** (-jnp.arange(n_freq, dtype=F32) / n_freq)
    ang = jnp.concatenate([t_row[:, None] * inv, t_col[:, None] * inv], axis=-1)
    return jnp.cos(ang), jnp.sin(ang)


def _apply_rope(x, cos, sin):
    x1, x2 = jnp.split(x, 2, axis=-1)
    return jnp.concatenate([x1 * cos - x2 * sin, x1 * sin + x2 * cos], axis=-1)


def _dwconv(x, w):
    ch = x.shape[-1]
    pad = (GDN_CONV - 1) // 2
    return lax.conv_general_dilated(x, w[:, None, :], window_strides=(1,), padding=[(pad, pad)],
                                    dimension_numbers=('NWC', 'WIO', 'NWC'), feature_group_count=ch)


def _retention(q, k, v, log_gamma, s0):
    b, h, l, d = q.shape
    qc, kc, vc = _chunks(q, CHUNK), _chunks(k, CHUNK), _chunks(v, CHUNK)
    pos = jnp.arange(CHUNK, dtype=F32)
    rel = pos[:, None] - pos[None, :]
    intra = jnp.where(rel >= 0, jnp.exp(jnp.maximum(rel, 0.0) * log_gamma[:, None, None]), 0.0)
    q_dec = jnp.exp((pos + 1.0) * log_gamma[:, None])
    k_dec = jnp.exp((CHUNK - 1.0 - pos) * log_gamma[:, None])
    c_dec = jnp.exp(CHUNK * log_gamma)[None, :, None, None]
    scores = jnp.einsum('bhnid,bhnjd->bhnij', qc, kc) * intra[None, :, None]
    o_intra = jnp.einsum('bhnij,bhnje->bhnie', scores, vc)
    kv = jnp.einsum('bhnjd,hj,bhnje->bhnde', kc, k_dec, vc)

    def step(s, kv_n):
        return c_dec * s + kv_n, s

    s_fin, s_prev = lax.scan(step, s0, jnp.moveaxis(kv, 2, 0))
    o_inter = jnp.einsum('bhnid,hi,bhnde->bhnie', qc, q_dec, jnp.moveaxis(s_prev, 0, 2))
    return (o_intra + o_inter).reshape(b, h, l, d), s_fin


def _gated_delta(q, k, v, log_alpha, beta, s0):
    b, h, l, _ = q.shape
    qc, kc, vc = _chunks(q, CHUNK), _chunks(k, CHUNK), _chunks(v, CHUNK)
    g = jnp.cumsum(_chunks(log_alpha, CHUNK), axis=-1)
    bc = _chunks(beta, CHUNK)
    idx = jnp.arange(CHUNK)
    incl = idx[:, None] >= idx[None, :]
    strict = idx[:, None] > idx[None, :]
    decay = jnp.exp(jnp.where(incl, g[..., :, None] - g[..., None, :], -jnp.inf))
    kb = kc * bc[..., None]
    a_mat = jnp.where(strict, jnp.einsum('bhnid,bhnjd->bhnij', kb, kc) * decay, 0.0)
    eye = jnp.eye(CHUNK, dtype=F32)
    t_mat = lax.linalg.triangular_solve(a_mat + eye, jnp.broadcast_to(eye, a_mat.shape),
                                        left_side=True, lower=True, unit_diagonal=True)
    u = jnp.einsum('bhnij,bhnje->bhnie', t_mat, vc * bc[..., None])
    w = jnp.einsum('bhnij,bhnjd->bhnid', t_mat, kb * jnp.exp(g)[..., None])
    attn = jnp.einsum('bhnid,bhnjd->bhnij', qc, kc) * decay
    qg = qc * jnp.exp(g)[..., None]
    kt = kc * jnp.exp(g[..., -1:] - g)[..., None]
    cd = jnp.exp(g[..., -1])

    def step(s, xs):
        u_n, w_n, qg_n, at_n, kt_n, cd_n = xs
        v_new = u_n - jnp.einsum('bhcd,bhde->bhce', w_n, s)
        o = jnp.einsum('bhcd,bhde->bhce', qg_n, s) + jnp.einsum('bhcs,bhse->bhce', at_n, v_new)
        s = s * cd_n[..., None, None] + jnp.einsum('bhcd,bhce->bhde', kt_n, v_new)
        return s, o

    xs = (jnp.moveaxis(u, 2, 0), jnp.moveaxis(w, 2, 0), jnp.moveaxis(qg, 2, 0),
          jnp.moveaxis(attn, 2, 0), jnp.moveaxis(kt, 2, 0), jnp.moveaxis(cd, 2, 0))
    s_fin, o = lax.scan(step, s0, xs)
    return jnp.moveaxis(o, 0, 2).reshape(b, h, l, -1), s_fin


def _gla(q, k, v, log_f, s0):
    b, h, l, _ = q.shape
    qc, kc, vc, fc = (_chunks(q, HG_CHUNK), _chunks(k, HG_CHUNK), _chunks(v, HG_CHUNK), _chunks(log_f, HG_CHUNK))
    cum = jnp.cumsum(fc, axis=-2)
    idx = jnp.arange(HG_CHUNK)
    incl = (idx[:, None] >= idx[None, :])[:, :, None]
    dec = jnp.exp(jnp.where(incl, cum[..., :, None, :] - cum[..., None, :, :], -jnp.inf))
    attn = jnp.einsum('bhnid,bhnjd,bhnijd->bhnij', qc, kc, dec)
    o_intra = jnp.einsum('bhnij,bhnje->bhnie', attn, vc)
    q_in = qc * jnp.exp(cum)
    k_tail = kc * jnp.exp(cum[..., -1:, :] - cum)
    cd = jnp.exp(cum[..., -1, :])
    kv = jnp.einsum('bhnjd,bhnje->bhnde', k_tail, vc)

    def step(s, xs):
        kv_n, cd_n = xs
        return cd_n[..., None] * s + kv_n, s

    s_fin, s_prev = lax.scan(step, s0, (jnp.moveaxis(kv, 2, 0), jnp.moveaxis(cd, 2, 0)))
    o_inter = jnp.einsum('bhnid,bhnde->bhnie', q_in, jnp.moveaxis(s_prev, 0, 2))
    return (o_intra + o_inter).reshape(b, h, l, -1), s_fin


def _cplx_combine(e1, e2):
    a1r, a1i, b1r, b1i = e1
    a2r, a2i, b2r, b2i = e2
    return (a1r * a2r - a1i * a2i, a1r * a2i + a1i * a2r,
            a2r * b1r - a2i * b1i + b2r, a2r * b1i + a2i * b1r + b2i)


def _s5(u, a_re, a_im, log_step, b_re, b_im, c_re, c_im, x0_re, x0_im):
    dt = jnp.exp(log_step)[:, None]
    mag = jnp.exp(a_re * dt)
    ab_re = mag * jnp.cos(a_im * dt)
    ab_im = mag * jnp.sin(a_im * dt)
    den = a_re * a_re + a_im * a_im
    nr = ab_re - 1.0
    f_re = (nr * a_re + ab_im * a_im) / den
    f_im = (ab_im * a_re - nr * a_im) / den
    bb_re = f_re[..., None] * b_re - f_im[..., None] * b_im
    bb_im = f_re[..., None] * b_im + f_im[..., None] * b_re
    bu_re = jnp.einsum('blgc,gnc->blgn', u, bb_re)
    bu_im = jnp.einsum('blgc,gnc->blgn', u, bb_im)
    bu_re = bu_re.at[:, 0].add(ab_re * x0_re - ab_im * x0_im)
    bu_im = bu_im.at[:, 0].add(ab_re * x0_im + ab_im * x0_re)
    a_full_re = jnp.broadcast_to(ab_re, bu_re.shape)
    a_full_im = jnp.broadcast_to(ab_im, bu_im.shape)
    _, _, x_re, x_im = lax.associative_scan(_cplx_combine, (a_full_re, a_full_im, bu_re, bu_im), axis=1)
    y = jnp.einsum('blgn,gcn->blgc', x_re, c_re) - jnp.einsum('blgn,gcn->blgc', x_im, c_im)
    return y, x_re[:, -1], x_im[:, -1]


def _mixer_jax(pm, ps, lp, lower_bound, s_ret, s_gdn, s_hg, s_s5_re, s_s5_im, rope):
    bsz, l, _ = pm.shape
    w = GROUP_WIDTH
    sl = lambda i, n=1: pm[..., i * w:(i + n) * w]
    r_q, r_k, r_v, r_g, d_q, d_k, d_v, d_g = (sl(i) for i in range(8))
    h_q, h_f, h_i, h_g, s_u = sl(8), sl(9, 2), sl(11), sl(12), sl(13)
    d_a, d_b = ps[..., :8], ps[..., 8:16]

    q = _heads(r_q, N_HEADS)
    k = _heads(r_k, N_HEADS)
    v = _heads(r_v, N_HEADS)
    if rope is not None:
        q = _apply_rope(q, *rope)
        k = _apply_rope(k, *rope)
    k = k * HEAD_DIM ** -0.5
    log_gamma = jax.nn.log_sigmoid(lp['ret_decay_logit'])
    o_f, rs_f = _retention(q, k, v, log_gamma[0], s_ret[:, 0])
    o_b, rs_b = _retention(_rev(q), _rev(k), _rev(v), log_gamma[1], s_ret[:, 1])
    ret_out = _merge(_head_layernorm(o_f + _rev(o_b))) * jax.nn.silu(r_g)

    qkv = jax.nn.silu(_dwconv(jnp.concatenate([d_q, d_k, d_v], axis=-1), lp['gdn_conv']))
    gq, gk, gv = jnp.split(qkv, 3, axis=-1)
    q = _l2norm(_heads(gq, N_HEADS)) * HEAD_DIM ** -0.5
    k = _l2norm(_heads(gk, N_HEADS))
    v = _heads(gv, N_HEADS)
    a = d_a.reshape(bsz, l, 2, N_HEADS)
    bt = d_b.reshape(bsz, l, 2, N_HEADS)
    log_alpha = (-jnp.exp(lp['gdn_a_log']) * jax.nn.softplus(a + lp['gdn_dt_bias'])).transpose(2, 0, 3, 1)
    beta = jax.nn.sigmoid(bt).transpose(2, 0, 3, 1)
    o_f, gs_f = _gated_delta(q, k, v, log_alpha[0], beta[0], s_gdn[:, 0])
    o_b, gs_b = _gated_delta(_rev(q), _rev(k), _rev(v), _rev(log_alpha[1]), _rev(beta[1]), s_gdn[:, 1])
    gdn_out = _merge(_rmsnorm(o_f + _rev(o_b), lp['gdn_norm_w'])) * jax.nn.silu(d_g)

    q = _heads(jax.nn.silu(h_q), N_HEADS) * HEAD_DIM ** -0.5
    v = _heads(h_i, N_HEADS)
    fx = h_f.reshape(bsz, l, 2, w)
    log_f = jnp.logaddexp(jnp.log(jnp.maximum(lower_bound, LB_FLOOR)),
                          jnp.log1p(-lower_bound) + jax.nn.log_sigmoid(fx))
    key_in = (1.0 - lower_bound) * jax.nn.sigmoid(-fx)
    o_f, hs_f = _gla(q, _heads(key_in[:, :, 0], N_HEADS), v, _heads(log_f[:, :, 0], N_HEADS), s_hg[:, 0])
    o_b, hs_b = _gla(_rev(q), _rev(_heads(key_in[:, :, 1], N_HEADS)), _rev(v),
                     _rev(_heads(log_f[:, :, 1], N_HEADS)), s_hg[:, 1])
    hg_out = _merge(_rmsnorm(o_f + _rev(o_b), lp['hg_norm_w'])) * jax.nn.silu(h_g)

    u = s_u.reshape(bsz, l, S5_GROUPS, S5_CH)
    p = [lp[nm] for nm in ('s5_a_re', 's5_a_im', 's5_log_step', 's5_b_re', 's5_b_im', 's5_c_re', 's5_c_im')]
    y_f, xr_f, xi_f = _s5(u, p[0][0], p[1][0], p[2][0], p[3][0], p[4][0], p[5][0], p[6][0],
                          s_s5_re[:, 0], s_s5_im[:, 0])
    y_b, xr_b, xi_b = _s5(jnp.flip(u, 1), p[0][1], p[1][1], p[2][1], p[3][1], p[4][1], p[5][1], p[6][1],
                          s_s5_re[:, 1], s_s5_im[:, 1])
    y = (y_f + jnp.flip(y_b, 1)).reshape(bsz, l, w) + lp['s5_d'] * s_u
    z = jax.nn.gelu(y)
    s5_out = z * jax.nn.sigmoid(z @ lp['s5_glu_w'] + lp['s5_glu_b'])

    mixed = jnp.concatenate([ret_out, gdn_out, hg_out, s5_out], axis=-1).astype(BF16)
    new_states = (jnp.stack([rs_f, rs_b], axis=1), jnp.stack([gs_f, gs_b], axis=1),
                  jnp.stack([hs_f, hs_b], axis=1), jnp.stack([xr_f, xr_b], axis=1),
                  jnp.stack([xi_f, xi_b], axis=1))
    return mixed, new_states


MIX_CHUNK = 128
DIRS = 2
NT_DIMS = (((1,), (1,)), ((), ()))
TN_DIMS = (((0,), (0,)), ((), ()))


def _dot(a, b):
    return jnp.dot(a.astype(BF16), b.astype(BF16), preferred_element_type=F32)


def _dot_nt(a, b):
    return lax.dot_general(a.astype(BF16), b.astype(BF16), NT_DIMS, preferred_element_type=F32)


def _dot_tn(a, b):
    return lax.dot_general(a.astype(BF16), b.astype(BF16), TN_DIMS, preferred_element_type=F32)


def _silu(x):
    return x * jax.nn.sigmoid(x)


def _seq_spec(seq_len, width, row_off, col_blk):
    return pl.BlockSpec((seq_len, width), lambda b: (row_off + b, col_blk))


def _mix_params(n_par):
    return pltpu.CompilerParams(dimension_semantics=("parallel",) * n_par, vmem_limit_bytes=VMEM_LIMIT)


def _ret_body(*refs, n_chunks, use_rope, has_s0):
    it = iter(refs)
    lg_ref, q_ref, k_ref, v_ref, g_ref = (next(it) for _ in range(5))
    cos_ref, sin_ref = (next(it), next(it)) if use_rope else (None, None)
    s0_ref = next(it) if has_s0 else None
    o_ref, sf_ref, acc_ref, s_ref, intra_ref, qd_ref, kd_ref, cd_ref = (next(it) for _ in range(8))
    C, H = MIX_CHUNK, N_HEADS

    row = lax.broadcasted_iota(jnp.int32, (C, C), 0)
    col = lax.broadcasted_iota(jnp.int32, (C, C), 1)
    rel = (row - col).astype(F32)
    pos = lax.broadcasted_iota(jnp.int32, (C, HEAD_DIM), 0).astype(F32)
    for d in range(DIRS):
        for h in range(H):
            lg = lg_ref[d, h]
            if d == 0:
                intra_ref[d, h] = jnp.where(rel >= 0, jnp.exp(jnp.maximum(rel, 0.0) * lg), 0.0)
                qd_ref[d, h] = jnp.exp((pos + 1.0) * lg)
                kd_ref[d, h] = jnp.exp((C - 1.0 - pos) * lg)
            else:
                intra_ref[d, h] = jnp.where(rel <= 0, jnp.exp(jnp.maximum(-rel, 0.0) * lg), 0.0)
                qd_ref[d, h] = jnp.exp((C - pos) * lg)
                kd_ref[d, h] = jnp.exp(pos * lg)
            cd_ref[d, h] = jnp.exp(jnp.full((C, HEAD_DIM), C, F32) * lg)
    if has_s0:
        s_ref[...] = s0_ref[...]
    else:
        s_ref[...] = jnp.zeros_like(s_ref)
    acc_ref[...] = jnp.zeros_like(acc_ref)

    def chunk_step(n, carry):
        for d in range(DIRS):
            c = n if d == 0 else n_chunks - 1 - n
            rows = pl.ds(pl.multiple_of(c * C, C), C)
            for h in range(H):
                cols = slice(h * HEAD_DIM, (h + 1) * HEAD_DIM)
                q = q_ref[rows, cols]
                k = k_ref[rows, cols]
                v = v_ref[rows, cols].astype(BF16)
                if use_rope:
                    cs, sn = cos_ref[rows, :], sin_ref[rows, :]
                    q = q * cs + pltpu.roll(q, HEAD_DIM // 2, 1) * sn
                    k = k * cs + pltpu.roll(k, HEAD_DIM // 2, 1) * sn
                k = k * HEAD_DIM ** -0.5
                s = s_ref[d, h]
                scores = _dot_nt(q, k) * intra_ref[d, h]
                o = _dot(scores, v) + _dot(q * qd_ref[d, h], s)
                acc_ref[rows, cols] += o
                s_ref[d, h] = cd_ref[d, h] * s + _dot_tn(k * kd_ref[d, h], v)
        return carry

    lax.fori_loop(0, n_chunks, chunk_step, 0)
    sf_ref[...] = s_ref[...]

    def finish(n, carry):
        rows = pl.ds(pl.multiple_of(n * C, C), C)
        for h in range(H):
            cols = slice(h * HEAD_DIM, (h + 1) * HEAD_DIM)
            o = acc_ref[rows, cols]
            mu = jnp.mean(o, axis=-1, keepdims=True)
            oc = o - mu
            y = oc * lax.rsqrt(jnp.mean(oc * oc, axis=-1, keepdims=True) + EPS)
            o_ref[rows, cols] = (y * _silu(g_ref[rows, cols])).astype(BF16)
        return carry

    lax.fori_loop(0, n_chunks, finish, 0)


def _retention_pallas(pm, log_gamma, n_seq, seq_len, row_off, rope2, s0, layer):
    use_rope, has_s0 = rope2 is not None, s0 is not None
    st_shape = (DIRS, N_HEADS, HEAD_DIM, HEAD_DIM)
    in_specs = [pl.BlockSpec(memory_space=pltpu.SMEM)]
    in_specs += [_seq_spec(seq_len, GROUP_WIDTH, row_off, cb) for cb in range(4)]
    args = [log_gamma, pm, pm, pm, pm]
    if use_rope:
        in_specs += [pl.BlockSpec((seq_len, HEAD_DIM), lambda b: (0, 0))] * 2
        args += list(rope2)
    if has_s0:
        in_specs.append(pl.BlockSpec((None, None) + st_shape, lambda b: (b, layer, 0, 0, 0, 0)))
        args.append(s0)
    return pl.pallas_call(
        functools.partial(_ret_body, n_chunks=seq_len // MIX_CHUNK, use_rope=use_rope, has_s0=has_s0),
        out_shape=(jax.ShapeDtypeStruct((n_seq * seq_len, GROUP_WIDTH), BF16),
                   jax.ShapeDtypeStruct((n_seq,) + st_shape, F32)),
        grid=(n_seq,),
        in_specs=in_specs,
        out_specs=(pl.BlockSpec((seq_len, GROUP_WIDTH), lambda b: (b, 0)),
                   pl.BlockSpec((None,) + st_shape, lambda b: (b, 0, 0, 0, 0))),
        scratch_shapes=[pltpu.VMEM((seq_len, GROUP_WIDTH), F32), pltpu.VMEM(st_shape, F32),
                        pltpu.VMEM((DIRS, N_HEADS, MIX_CHUNK, MIX_CHUNK), F32),
                        pltpu.VMEM((DIRS, N_HEADS, MIX_CHUNK, HEAD_DIM), F32),
                        pltpu.VMEM((DIRS, N_HEADS, MIX_CHUNK, HEAD_DIM), F32),
                        pltpu.VMEM((DIRS, N_HEADS, MIX_CHUNK, HEAD_DIM), F32)],
        compiler_params=_mix_params(1),
        name="retention",
    )(*args)


def _rope_tables(l):
    cos, sin = _axial_rope(l)
    return jnp.concatenate([cos, cos], axis=-1), jnp.concatenate([-sin, sin], axis=-1)


S5_HALF_G = S5_GROUPS // 2
S5_HALF_U = S5_HALF_G * S5_CH
S5_HALF_X = S5_HALF_G * S5_N
S5_X = S5_GROUPS * S5_N
S5_TC = 256
S5_BLK = 8
S5_TABS = 4


def _s5_tables(a_re, a_im, log_step, b_re, b_im, c_re, c_im):
    dt = jnp.exp(log_step)[..., None]
    mag = jnp.exp(a_re * dt)
    ab_re = mag * jnp.cos(a_im * dt)
    ab_im = mag * jnp.sin(a_im * dt)
    den = a_re * a_re + a_im * a_im
    nr = ab_re - 1.0
    f_re = (nr * a_re + ab_im * a_im) / den
    f_im = (ab_im * a_re - nr * a_im) / den
    bb_re = f_re[..., None] * b_re - f_im[..., None] * b_im
    bb_im = f_re[..., None] * b_im + f_im[..., None] * b_re
    eye = jnp.eye(S5_HALF_G, dtype=F32)

    def in_mat(bb):
        bb = bb.reshape(DIRS, 2, S5_HALF_G, S5_N, S5_CH)
        return jnp.einsum('dhgnc,gk->dhgckn', bb, eye).reshape(DIRS, 2, S5_HALF_U, S5_HALF_X)

    def out_mat(cc):
        cc = cc.reshape(DIRS, 2, S5_HALF_G, S5_CH, S5_N)
        return jnp.einsum('dhgcn,gk->dhgnkc', cc, eye).reshape(DIRS, 2, S5_HALF_X, S5_HALF_U)

    bm = jnp.concatenate([in_mat(bb_re), in_mat(bb_im)], axis=-1).astype(BF16)
    cm = jnp.concatenate([out_mat(c_re), -out_mat(c_im)], axis=-2).astype(BF16)
    t = jnp.arange(S5_BLK, dtype=F32)
    order = jnp.stack([t, S5_BLK - 1.0 - t])
    shifts = 2.0 ** jnp.arange(S5_TABS - 1, dtype=F32)
    expo = jnp.concatenate([jnp.where(order[:, None, :] >= shifts[None, :, None], shifts[None, :, None], jnp.nan),
                            order[:, None, :] + 1.0], axis=1)
    live = ~jnp.isnan(expo)
    e = jnp.where(live, expo, 0.0)[..., None]
    adt_re = (a_re * dt).reshape(DIRS, 1, 1, S5_X)
    adt_im = (a_im * dt).reshape(DIRS, 1, 1, S5_X)
    pmag = jnp.where(live[..., None], jnp.exp(e * adt_re), 0.0)
    pw_re = pmag * jnp.cos(e * adt_im)
    pw_im = pmag * jnp.sin(e * adt_im)
    return bm, cm, pw_re, pw_im


def _gelu_tanh(x):
    return 0.5 * x * (1.0 + jnp.tanh(math.sqrt(2.0 / math.pi) * (x + 0.044715 * (x * x * x))))


def _s5_body(*refs, seq_len, has_s0):
    it = iter(refs)
    u_ref, bm_ref, cm_ref, pwr_ref, pwi_ref, d_ref, gw_ref, gb_ref = (next(it) for _ in range(8))
    x0r_ref, x0i_ref = (next(it), next(it)) if has_s0 else (None, None)
    o_ref, sfr_ref, sfi_ref, y_ref, xr_ref, xi_ref = (next(it) for _ in range(6))
    tc = min(seq_len, S5_TC)
    n_tiles = seq_len // tc
    n_blk = tc // S5_BLK

    y_ref[...] = d_ref[...] * u_ref[...]
    for d in range(DIRS):
        last = S5_BLK - 1 if d == 0 else 0
        for hf in range(2):
            xs = slice(hf * S5_HALF_X, (hf + 1) * S5_HALF_X)
            us = slice(hf * S5_HALF_U, (hf + 1) * S5_HALF_U)

            def scan_block(j, carry, d=d, xs=xs, last=last):
                car_re, car_im = carry
                blk = j if d == 0 else n_blk - 1 - j
                rows = pl.ds(pl.multiple_of(blk * S5_BLK, S5_BLK), S5_BLK)
                xr, xi = xr_ref[rows, :], xi_ref[rows, :]
                for k in range(S5_TABS - 1):
                    s = 1 << k
                    shift = s if d == 0 else S5_BLK - s
                    p_re, p_im = pwr_ref[d, k, :, xs], pwi_ref[d, k, :, xs]
                    sr, si = pltpu.roll(xr, shift, 0), pltpu.roll(xi, shift, 0)
                    xr, xi = xr + p_re * sr - p_im * si, xi + p_re * si + p_im * sr
                p_re, p_im = pwr_ref[d, S5_TABS - 1, :, xs], pwi_ref[d, S5_TABS - 1, :, xs]
                xr, xi = xr + p_re * car_re - p_im * car_im, xi + p_re * car_im + p_im * car_re
                xr_ref[rows, :] = xr
                xi_ref[rows, :] = xi
                return xr[last:last + 1, :], xi[last:last + 1, :]

            def scan_tile(i, carry, d=d, hf=hf, us=us, scan_block=scan_block):
                tile = i if d == 0 else n_tiles - 1 - i
                rows = pl.ds(pl.multiple_of(tile * tc, tc), tc)
                bu = _dot(u_ref[rows, us], bm_ref[d, hf])
                xr_ref[...] = bu[:, :S5_HALF_X]
                xi_ref[...] = bu[:, S5_HALF_X:]
                carry = lax.fori_loop(0, n_blk, scan_block, carry)
                y_ref[rows, us] += (_dot(xr_ref[...], cm_ref[d, hf, :S5_HALF_X, :])
                                    + _dot(xi_ref[...], cm_ref[d, hf, S5_HALF_X:, :]))
                return carry

            if has_s0:
                carry0 = (x0r_ref[d, :, xs], x0i_ref[d, :, xs])
            else:
                carry0 = (jnp.zeros((1, S5_HALF_X), F32), jnp.zeros((1, S5_HALF_X), F32))
            car_re, car_im = lax.fori_loop(0, n_tiles, scan_tile, carry0)
            sfr_ref[d, :, xs] = car_re
            sfi_ref[d, :, xs] = car_im

    z = _gelu_tanh(y_ref[...])
    o_ref[...] = (z * jax.nn.sigmoid(_dot(z, gw_ref[...]) + gb_ref[...])).astype(BF16)


def _s5_pallas(pm, tables, s5_d, glu_w, glu_b, n_seq, seq_len, row_off, x0, layer):
    bm, cm, pw_re, pw_im = tables
    has_s0 = x0 is not None
    full = lambda shape: pl.BlockSpec(shape, lambda b: (0,) * len(shape))
    in_specs = [_seq_spec(seq_len, GROUP_WIDTH, row_off, 13),
                full(bm.shape), full(cm.shape), full(pw_re.shape), full(pw_im.shape),
                full((1, GROUP_WIDTH)), full((GROUP_WIDTH, GROUP_WIDTH)), full((1, GROUP_WIDTH))]
    args = [pm, bm, cm, pw_re, pw_im, s5_d.reshape(1, GROUP_WIDTH), glu_w.astype(BF16),
            glu_b.reshape(1, GROUP_WIDTH)]
    if has_s0:
        in_specs += [pl.BlockSpec((None, None, DIRS, 1, S5_X), lambda b: (b, layer, 0, 0, 0))] * 2
        args += list(x0)
    st = jax.ShapeDtypeStruct((n_seq, DIRS, 1, S5_X), F32)
    st_spec = pl.BlockSpec((None, DIRS, 1, S5_X), lambda b: (b, 0, 0, 0))
    tc = min(seq_len, S5_TC)
    return pl.pallas_call(
        functools.partial(_s5_body, seq_len=seq_len, has_s0=has_s0),
        out_shape=(jax.ShapeDtypeStruct((n_seq * seq_len, GROUP_WIDTH), BF16), st, st),
        grid=(n_seq,),
        in_specs=in_specs,
        out_specs=(pl.BlockSpec((seq_len, GROUP_WIDTH), lambda b: (b, 0)), st_spec, st_spec),
        scratch_shapes=[pltpu.VMEM((seq_len, GROUP_WIDTH), F32),
                        pltpu.VMEM((tc, S5_HALF_X), F32), pltpu.VMEM((tc, S5_HALF_X), F32)],
        compiler_params=_mix_params(1),
        name="s5",
    )(*args)


GLA_LEVELS = 7


def _log_sigmoid(x):
    return jnp.minimum(x, 0.0) - jnp.log1p(jnp.exp(-jnp.abs(x)))


def _logaddexp(a, b):
    return jnp.maximum(a, b) + jnp.log1p(jnp.exp(-jnp.abs(a - b)))


def _chunk_cumsum_rows(x, rowi):
    s = 1
    while s < MIX_CHUNK:
        x = x + jnp.where(rowi >= s, pltpu.roll(x, s, 0), 0.0)
        s *= 2
    return x


def _gla_body(*refs, n_chunks, has_s0):
    it = iter(refs)
    q_ref, f0_ref, f1_ref, i_ref, g_ref, lb_ref, nw_ref = (next(it) for _ in range(7))
    s0_ref = next(it) if has_s0 else None
    o_ref, sf_ref, acc_ref, s_ref, code_ref = (next(it) for _ in range(5))
    C, H = MIX_CHUNK, N_HEADS
    f_refs = (f0_ref, f1_ref)

    rowi = lax.broadcasted_iota(jnp.int32, (C, HEAD_DIM), 0)
    ri = lax.broadcasted_iota(jnp.int32, (C, C), 0)
    ci = lax.broadcasted_iota(jnp.int32, (C, C), 1)
    top_bit = 31 - lax.clz(ri ^ ci)
    code_ref[...] = jnp.where(ri > ci, top_bit, jnp.where(ri < ci, -1 - top_bit, GLA_LEVELS))
    if has_s0:
        s_ref[...] = s0_ref[...]
    else:
        s_ref[...] = jnp.zeros_like(s_ref)
    acc_ref[...] = jnp.zeros_like(acc_ref)

    def chunk_step(n, carry):
        for d in range(DIRS):
            c = n if d == 0 else n_chunks - 1 - n
            rows = pl.ds(pl.multiple_of(c * C, C), C)
            for h in range(H):
                cols = slice(h * HEAD_DIM, (h + 1) * HEAD_DIM)
                code = code_ref[...]
                q = _silu(q_ref[rows, cols]) * HEAD_DIM ** -0.5
                v = i_ref[rows, cols].astype(BF16)
                fx = f_refs[d][rows, cols]
                lb = lb_ref[d:d + 1, cols]
                logf = _logaddexp(jnp.log(jnp.maximum(lb, LB_FLOOR)), jnp.log1p(-lb) + _log_sigmoid(fx))
                k = (1.0 - lb) * jax.nn.sigmoid(-fx)
                cum = _chunk_cumsum_rows(logf, rowi)
                own = cum
                attn = jnp.where(code == GLA_LEVELS, _dot_nt(q, k), 0.0)
                for lvl in range(GLA_LEVELS):
                    m = 1 << lvl
                    prev = pltpu.roll(own, m, 0)
                    pre = jnp.minimum(cum - prev, 0.0)
                    suf = jnp.minimum(own - cum, 0.0)
                    if d == 0:
                        sc = _dot_nt(q * jnp.exp(pre), k * jnp.exp(suf))
                        hit = code == lvl
                    else:
                        sc = _dot_nt(q * jnp.exp(jnp.minimum(suf + logf, 0.0)),
                                     k * jnp.exp(jnp.minimum(pre - logf, 0.0)))
                        hit = code == -1 - lvl
                    attn = jnp.where(hit, sc, attn)
                    own = jnp.where(((rowi >> lvl) & 1) == 0, pltpu.roll(own, C - m, 0), own)
                tot = own
                st = s_ref[d, h]
                if d == 0:
                    q_in, k_out = q * jnp.exp(cum), k * jnp.exp(tot - cum)
                else:
                    q_in, k_out = q * jnp.exp(tot - cum + logf), k * jnp.exp(cum - logf)
                acc_ref[rows, cols] += _dot(attn, v) + _dot_nt(q_in, st)
                s_ref[d, h] = jnp.exp(tot[0:1, :]) * st + _dot_tn(v, k_out)
        return carry

    lax.fori_loop(0, n_chunks, chunk_step, 0)
    sf_ref[...] = s_ref[...]

    def finish(n, carry):
        rows = pl.ds(pl.multiple_of(n * C, C), C)
        for h in range(H):
            cols = slice(h * HEAD_DIM, (h + 1) * HEAD_DIM)
            o = acc_ref[rows, cols]
            y = o * lax.rsqrt(jnp.mean(o * o, axis=-1, keepdims=True) + EPS) * nw_ref[...]
            o_ref[rows, cols] = (y * _silu(g_ref[rows, cols])).astype(BF16)
        return carry

    lax.fori_loop(0, n_chunks, finish, 0)


def _gla_pallas(pm, lower_bound, norm_w, n_seq, seq_len, row_off, s0_t, layer):
    has_s0 = s0_t is not None
    st_shape = (DIRS, N_HEADS, HEAD_DIM, HEAD_DIM)
    in_specs = [_seq_spec(seq_len, GROUP_WIDTH, row_off, cb) for cb in (8, 9, 10, 11, 12)]
    in_specs += [pl.BlockSpec((DIRS, GROUP_WIDTH), lambda b: (0, 0)), pl.BlockSpec((1, HEAD_DIM), lambda b: (0, 0))]
    args = [pm] * 5 + [lower_bound, norm_w.reshape(1, HEAD_DIM)]
    if has_s0:
        in_specs.append(pl.BlockSpec((None, None) + st_shape, lambda b: (b, layer, 0, 0, 0, 0)))
        args.append(s0_t)
    return pl.pallas_call(
        functools.partial(_gla_body, n_chunks=seq_len // MIX_CHUNK, has_s0=has_s0),
        out_shape=(jax.ShapeDtypeStruct((n_seq * seq_len, GROUP_WIDTH), BF16),
                   jax.ShapeDtypeStruct((n_seq,) + st_shape, F32)),
        grid=(n_seq,),
        in_specs=in_specs,
        out_specs=(pl.BlockSpec((seq_len, GROUP_WIDTH), lambda b: (b, 0)),
                   pl.BlockSpec((None,) + st_shape, lambda b: (b, 0, 0, 0, 0))),
        scratch_shapes=[pltpu.VMEM((seq_len, GROUP_WIDTH), F32), pltpu.VMEM(st_shape, F32),
                        pltpu.VMEM((MIX_CHUNK, MIX_CHUNK), jnp.int32)],
        compiler_params=_mix_params(1),
        name="hgrn2",
    )(*args)


GDN_BLOCK_BITS = 4
GDN_MERGES = 3
N_GATES = DIRS * N_HEADS


def _softplus(x):
    return jnp.maximum(x, 0.0) + jnp.log1p(jnp.exp(-jnp.abs(x)))


def _unit_tri_inverse(a, blev, eye):
    b = -jnp.where(blev == 0, a, 0.0)
    p = eye + b
    for _ in range(GDN_BLOCK_BITS - 1):
        b = _dot(b, b)
        p = p + _dot(p, b)
    for lvl in range(1, GDN_MERGES + 1):
        e = jnp.where(blev == lvl, a, 0.0)
        p = p - _dot(p, _dot(e, p))
    return p


def _gdn_body(*refs, seq_len, has_s0):
    it = iter(refs)
    (q_ref, k_ref, v_ref, g_ref, ps_ref, pst_ref, cw_ref, prow_ref, pcol_ref, nw_ref) = (next(it) for _ in range(10))
    s0_ref = next(it) if has_s0 else None
    (o_ref, sf_ref, acc_ref, s_ref, qn_ref, kn_ref, vn_ref, gate_ref, gatet_ref, blev_ref) = (
        next(it) for _ in range(10))
    C, H, L = MIX_CHUNK, N_HEADS, seq_len
    n_chunks = L // C
    w = GROUP_WIDTH

    rowl = lax.broadcasted_iota(jnp.int32, (L, HEAD_DIM), 0)
    for part, (src, dst) in enumerate(((q_ref, qn_ref), (k_ref, kn_ref), (v_ref, vn_ref))):
        for h in range(H):
            cols = slice(h * HEAD_DIM, (h + 1) * HEAD_DIM)
            wc = slice(part * w + h * HEAD_DIM, part * w + (h + 1) * HEAD_DIM)
            x = src[:, cols]
            x_prev = jnp.where(rowl >= 1, pltpu.roll(x, 1, 0), 0.0)
            x_next = jnp.where(rowl < L - 1, pltpu.roll(x, L - 1, 0), 0.0)
            y = _silu(cw_ref[0:1, wc] * x_prev + cw_ref[1:2, wc] * x + cw_ref[2:3, wc] * x_next)
            if part < 2:
                y = y * lax.rsqrt(jnp.sum(y * y, axis=-1, keepdims=True) + EPS)
            if part == 0:
                y = y * HEAD_DIM ** -0.5
            dst[:, cols] = y

    rowi = lax.broadcasted_iota(jnp.int32, (C, HEAD_DIM), 0)
    lane = lax.broadcasted_iota(jnp.int32, (C, HEAD_DIM), 1)
    lane_t = lax.broadcasted_iota(jnp.int32, (2 * N_GATES, C), 1)
    row_t = lax.broadcasted_iota(jnp.int32, (2 * N_GATES, C), 0)
    for c in range(n_chunks):
        rows = slice(c * C, (c + 1) * C)
        a = ps_ref[rows, :]
        la = -jnp.exp(prow_ref[0:1, :]) * _softplus(a + prow_ref[1:2, :])
        pre = _chunk_cumsum_rows(la, rowi)
        tot = jnp.broadcast_to(pre[C - 1:C, :], (C, HEAD_DIM))
        g = jnp.where(lane < N_HEADS, pre, tot - pre + la)
        gate_ref[rows, :] = jnp.where(lane < N_GATES, g,
                                      jnp.where(lane < 2 * N_GATES, jax.nn.sigmoid(a),
                                                pltpu.roll(tot, 2 * N_GATES, 1)))
        at = pst_ref[:, rows]
        lat = -jnp.exp(pcol_ref[:, 0:1]) * _softplus(at + pcol_ref[:, 1:2])
        pre_t, s = lat, 1
        while s < C:
            pre_t = pre_t + jnp.where(lane_t >= s, pltpu.roll(pre_t, s, 1), 0.0)
            s *= 2
        tot_t = jnp.broadcast_to(pre_t[:, C - 1:C], (2 * N_GATES, C))
        g_t = jnp.where(row_t < N_HEADS, pre_t, tot_t - pre_t + lat)
        gatet_ref[:, rows] = jnp.where(row_t < N_GATES, g_t, jax.nn.sigmoid(at))

    ri = lax.broadcasted_iota(jnp.int32, (C, C), 0)
    ci = lax.broadcasted_iota(jnp.int32, (C, C), 1)
    bx = (ri >> GDN_BLOCK_BITS) ^ (ci >> GDN_BLOCK_BITS)
    blev_ref[...] = jnp.where(bx == 0, 0, 32 - lax.clz(bx))
    if has_s0:
        s_ref[...] = s0_ref[...]
    else:
        s_ref[...] = jnp.zeros_like(s_ref)
    acc_ref[...] = jnp.zeros_like(acc_ref)

    def chunk_step(n, carry):
        eye = jnp.where(ri == ci, 1.0, 0.0)
        blev = blev_ref[...]
        for d in range(DIRS):
            c = n if d == 0 else n_chunks - 1 - n
            rows = pl.ds(pl.multiple_of(c * C, C), C)
            incl = (ri >= ci) if d == 0 else (ri <= ci)
            strict = (ri > ci) if d == 0 else (ri < ci)
            for h in range(H):
                cols = slice(h * HEAD_DIM, (h + 1) * HEAD_DIM)
                cg = d * N_HEADS + h
                q, k, v = qn_ref[rows, cols], kn_ref[rows, cols], vn_ref[rows, cols]
                g_i = jnp.broadcast_to(gate_ref[rows, cg:cg + 1], (C, C))
                b_i = jnp.broadcast_to(gate_ref[rows, N_GATES + cg:N_GATES + cg + 1], (C, C))
                tot = jnp.broadcast_to(gate_ref[rows, 2 * N_GATES + cg:2 * N_GATES + cg + 1], (C, C))
                g_j = gatet_ref[cg:cg + 1, rows]
                b_j = gatet_ref[N_GATES + cg:N_GATES + cg + 1, rows]
                decay = jnp.where(incl, jnp.exp(jnp.minimum(g_i - g_j, 0.0)), 0.0)
                a_mat = jnp.where(strict, _dot_nt(k, k) * b_i * decay, 0.0)
                t_mat = _unit_tri_inverse(a_mat, blev, eye)
                tb = t_mat * b_j
                u = _dot(tb, v)
                wk = _dot(tb * jnp.exp(g_j), k)
                attn = _dot_nt(q, k) * decay
                s = s_ref[d, h]
                v_new = u - _dot(wk, s)
                acc_ref[rows, cols] += _dot(q * jnp.exp(g_i), s) + _dot(attn, v_new)
                s_ref[d, h] = s * jnp.exp(tot) + _dot_tn(k * jnp.exp(tot - g_i), v_new)
        return carry

    lax.fori_loop(0, n_chunks, chunk_step, 0)
    sf_ref[...] = s_ref[...]

    def finish(n, carry):
        rows = pl.ds(pl.multiple_of(n * C, C), C)
        for h in range(H):
            cols = slice(h * HEAD_DIM, (h + 1) * HEAD_DIM)
            o = acc_ref[rows, cols]
            y = o * lax.rsqrt(jnp.mean(o * o, axis=-1, keepdims=True) + EPS) * nw_ref[...]
            o_ref[rows, cols] = (y * _silu(g_ref[rows, cols])).astype(BF16)
        return carry

    lax.fori_loop(0, n_chunks, finish, 0)


def _gdn_pallas(pm, ps, ps_t, conv_w, a_log, dt_bias, norm_w, n_seq, seq_len, row_off, s0, layer):
    has_s0 = s0 is not None
    st_shape = (DIRS, N_HEADS, HEAD_DIM, HEAD_DIM)
    par = jnp.stack([a_log.reshape(N_GATES), dt_bias.reshape(N_GATES)])
    par_row = jnp.pad(par, ((0, 0), (0, HEAD_DIM - N_GATES)))
    par_col = jnp.pad(par.T, ((0, N_GATES), (0, 0)))
    full = lambda shape: pl.BlockSpec(shape, lambda b: (0,) * len(shape))
    in_specs = [_seq_spec(seq_len, GROUP_WIDTH, row_off, cb) for cb in (4, 5, 6, 7)]
    in_specs += [_seq_spec(seq_len, PROJ_SMALL, row_off, 0),
                 pl.BlockSpec((2 * N_GATES, seq_len), lambda b: (0, row_off + b)),
                 full((GDN_CONV, 3 * GROUP_WIDTH)), full((2, HEAD_DIM)), full((2 * N_GATES, 2)), full((1, HEAD_DIM))]
    args = [pm] * 4 + [ps, ps_t, conv_w, par_row, par_col, norm_w.reshape(1, HEAD_DIM)]
    if has_s0:
        in_specs.append(pl.BlockSpec((None, None) + st_shape, lambda b: (b, layer, 0, 0, 0, 0)))
        args.append(s0)
    seq_f32 = lambda width: pltpu.VMEM((seq_len, width), F32)
    return pl.pallas_call(
        functools.partial(_gdn_body, seq_len=seq_len, has_s0=has_s0),
        out_shape=(jax.ShapeDtypeStruct((n_seq * seq_len, GROUP_WIDTH), BF16),
                   jax.ShapeDtypeStruct((n_seq,) + st_shape, F32)),
        grid=(n_seq,),
        in_specs=in_specs,
        out_specs=(pl.BlockSpec((seq_len, GROUP_WIDTH), lambda b: (b, 0)),
                   pl.BlockSpec((None,) + st_shape, lambda b: (b, 0, 0, 0, 0))),
        scratch_shapes=[seq_f32(GROUP_WIDTH), pltpu.VMEM(st_shape, F32),
                        seq_f32(GROUP_WIDTH), seq_f32(GROUP_WIDTH), seq_f32(GROUP_WIDTH),
                        seq_f32(HEAD_DIM), pltpu.VMEM((2 * N_GATES, seq_len), F32),
                        pltpu.VMEM((MIX_CHUNK, MIX_CHUNK), jnp.int32)],
        compiler_params=_mix_params(1),
        name="gated_delta",
    )(*args)


def kernel(x_prompt, x_sample, state_ret, state_gdn, state_hgrn, state_s5_re, state_s5_im, c, c_ctx, norm1_w, norm2_w, final_norm_w, ada_w, ada_b, in_proj, out_proj, ret_decay_logit, gdn_conv, gdn_a_log, gdn_dt_bias, gdn_norm_w, hg_lb_param, hg_norm_w, s5_a_re, s5_a_im, s5_b_re, s5_b_im, s5_c_re, s5_c_im, s5_log_step, s5_d, s5_glu_w, s5_glu_b, ffn_w1, ffn_w3, ffn_w2):
    lb_soft = jax.nn.softmax(hg_lb_param, axis=0)
    lower_bounds = jnp.cumsum(lb_soft, axis=0) - lb_soft[0]
    rope2 = _rope_tables(DEC_SEQ)

    cvec = jnp.zeros((N_SEQ_ROWS, D_MODEL), F32).at[0].set(c_ctx).at[1:1 + DEC_BATCH].set(c)
    mod_all = _ada(cvec, ada_w, ada_b).reshape(DEPTH, N_SEQ_ROWS, N_MOD, 1, D_MODEL)

    w_main = jnp.concatenate([in_proj[:, :, :8 * GROUP_WIDTH], in_proj[:, :, 8 * GROUP_WIDTH + 16:]],
                             axis=-1).astype(BF16)
    w_small = jnp.pad(in_proj[:, :, 8 * GROUP_WIDTH:8 * GROUP_WIDTH + 16],
                      ((0, 0), (0, 0), (0, PROJ_SMALL - 16))).astype(BF16)
    w_out = out_proj.astype(BF16)
    w1, w3, w2 = ffn_w1.astype(BF16), ffn_w3.astype(BF16), ffn_w2.astype(BF16)

    x = jnp.concatenate([x_prompt.reshape(N_CTX_TOK, D_MODEL), x_sample.reshape(N_LAT_TOK, D_MODEL)], axis=0)
    hgrn_t = jnp.swapaxes(state_hgrn, -1, -2)
    s5_x0 = (state_s5_re.reshape(DEC_BATCH, DEPTH, DIRS, 1, S5_X), state_s5_im.reshape(DEC_BATCH, DEPTH, DIRS, 1, S5_X))
    lat_off = N_CTX_TOK // DEC_SEQ
    ctx_states = []
    for i in range(DEPTH):
        mod = mod_all[i]
        pm, ps = _inproj(x, norm1_w[i][None], mod, w_main[i], w_small[i])
        ps_t = ps[:, :2 * N_GATES].T
        lg = jax.nn.log_sigmoid(ret_decay_logit[i])
        tables = _s5_tables(s5_a_re[i], s5_a_im[i], s5_log_step[i], s5_b_re[i], s5_b_im[i], s5_c_re[i], s5_c_im[i])
        ctx = (BATCH, SEQ, 0)
        lat = (DEC_BATCH, DEC_SEQ, lat_off)

        ret_c, rs = _retention_pallas(pm, lg, *ctx, None, None, i)
        ret_l, _ = _retention_pallas(pm, lg, *lat, rope2, state_ret, i)
        gdn_args = (pm, ps, ps_t, gdn_conv[i], gdn_a_log[i], gdn_dt_bias[i], gdn_norm_w[i])
        gdn_c, gs = _gdn_pallas(*gdn_args, *ctx, None, i)
        gdn_l, _ = _gdn_pallas(*gdn_args, *lat, state_gdn, i)
        hg_c, hs = _gla_pallas(pm, lower_bounds[i], hg_norm_w[i], *ctx, None, i)
        hg_l, _ = _gla_pallas(pm, lower_bounds[i], hg_norm_w[i], *lat, hgrn_t, i)
        s5_args = (pm, tables, s5_d[i], s5_glu_w[i], s5_glu_b[i])
        s5_c, xr, xi = _s5_pallas(*s5_args, *ctx, None, i)
        s5_l, _, _ = _s5_pallas(*s5_args, *lat, s5_x0, i)
        ctx_states.append((rs, gs, jnp.swapaxes(hs, -1, -2), xr.reshape(BATCH, DIRS, S5_GROUPS, S5_N),
                           xi.reshape(BATCH, DIRS, S5_GROUPS, S5_N)))

        parts = [jnp.concatenate(pair, axis=0) for pair in ((ret_c, ret_l), (gdn_c, gdn_l), (hg_c, hg_l), (s5_c, s5_l))]
        x = _outproj(parts, w_out[i], x, mod)
        x = _ffn(x, norm2_w[i][None], mod, w1[i], w3[i], w2[i], final_norm_w[None], i == DEPTH - 1)

    y_prompt = x[:N_CTX_TOK].reshape(BATCH, SEQ, D_MODEL)
    y_sample = x[N_CTX_TOK:].reshape(DEC_BATCH, DEC_SEQ, D_MODEL)
    new_states = tuple(jnp.stack([s[j] for s in ctx_states], axis=1) for j in range(5))
    return (y_prompt, y_sample) + new_states
```

```python
import functools
import math

import jax
import jax.numpy as jnp
from jax import lax
from jax.experimental import pallas as pl
from jax.experimental.pallas import tpu as pltpu

F32 = jnp.float32
BF16 = jnp.bfloat16

D_MODEL = 2048
BATCH = 16
SEQ = 256
DEPTH = 2
DEC_BATCH = 8
DEC_SEQ = 1024
GRID_W = 64
HEAD_DIM = 128
GROUP_WIDTH = 512
N_HEADS = 4
S5_CH = 16
S5_GROUPS = 32
S5_N = 64
GDN_CONV = 3
CHUNK = 64
HG_CHUNK = 16
ROPE_BASE = 10000.0
FFN_HIDDEN = 5632
N_MOD = 6
EPS = 1e-6
LB_FLOOR = 1e-30

N_CTX_TOK = BATCH * SEQ
N_LAT_TOK = DEC_BATCH * DEC_SEQ
N_TOK = N_CTX_TOK + N_LAT_TOK
N_SEQ_ROWS = 16
PROJ_MAIN = 14 * GROUP_WIDTH
PROJ_SMALL = 128
VMEM_LIMIT = 56 * 1024 * 1024


def _seq_row(tile, tm):
    n_ctx = N_CTX_TOK // tm
    per_lat = DEC_SEQ // tm
    return jnp.where(tile < n_ctx, 0, 1 + (tile - n_ctx) // per_lat)


def _ada_body(c_ref, w_ref, b_ref, o_ref):
    cv = c_ref[...]
    s = cv * jax.nn.sigmoid(cv)
    o_ref[0] = jnp.dot(s.astype(BF16), w_ref[0].astype(BF16), preferred_element_type=F32) + b_ref[0]


def _ada(cvec, ada_w, ada_b):
    tn = 1024
    n = N_MOD * D_MODEL
    return pl.pallas_call(
        _ada_body,
        out_shape=jax.ShapeDtypeStruct((DEPTH, N_SEQ_ROWS, n), F32),
        grid=(DEPTH, n // tn),
        in_specs=[
            pl.BlockSpec((N_SEQ_ROWS, D_MODEL), lambda l, j: (0, 0)),
            pl.BlockSpec((1, D_MODEL, tn), lambda l, j: (l, 0, j)),
            pl.BlockSpec((1, 1, tn), lambda l, j: (l, 0, j)),
        ],
        out_specs=pl.BlockSpec((1, N_SEQ_ROWS, tn), lambda l, j: (l, 0, j)),
        compiler_params=pltpu.CompilerParams(
            dimension_semantics=("parallel", "parallel"), vmem_limit_bytes=VMEM_LIMIT),
        name="ada_mod",
    )(cvec, ada_w, ada_b.reshape(DEPTH, 1, n))


def _norm_mod(x, nw, sc, sh):
    ms = jnp.mean(x * x, axis=-1, keepdims=True)
    y = x * lax.rsqrt(ms + EPS) * nw
    return y * (1.0 + sc) + sh


def _inproj_body(x_ref, nw_ref, sc_ref, sh_ref, w_ref, ws_ref, o_ref, os_ref, h_ref):
    @pl.when(pl.program_id(1) == 0)
    def _():
        hb = _norm_mod(x_ref[...], nw_ref[...], sc_ref[...], sh_ref[...]).astype(BF16)
        h_ref[...] = hb
        os_ref[...] = jnp.dot(hb, ws_ref[...], preferred_element_type=F32)

    o_ref[...] = jnp.dot(h_ref[...], w_ref[...], preferred_element_type=F32)


def _inproj(x, nw, mod, w_main, w_small):
    tm, tn = 1024, 1024
    return pl.pallas_call(
        _inproj_body,
        out_shape=(jax.ShapeDtypeStruct((N_TOK, PROJ_MAIN), F32),
                   jax.ShapeDtypeStruct((N_TOK, PROJ_SMALL), F32)),
        grid=(N_TOK // tm, PROJ_MAIN // tn),
        in_specs=[
            pl.BlockSpec((tm, D_MODEL), lambda i, j: (i, 0)),
            pl.BlockSpec((1, D_MODEL), lambda i, j: (0, 0)),
            pl.BlockSpec((None, None, 1, D_MODEL), lambda i, j: (_seq_row(i, tm), 1, 0, 0)),
            pl.BlockSpec((None, None, 1, D_MODEL), lambda i, j: (_seq_row(i, tm), 0, 0, 0)),
            pl.BlockSpec((D_MODEL, tn), lambda i, j: (0, j)),
            pl.BlockSpec((D_MODEL, PROJ_SMALL), lambda i, j: (0, 0)),
        ],
        out_specs=(pl.BlockSpec((tm, tn), lambda i, j: (i, j)),
                   pl.BlockSpec((tm, PROJ_SMALL), lambda i, j: (i, 0))),
        scratch_shapes=[pltpu.VMEM((tm, D_MODEL), BF16)],
        compiler_params=pltpu.CompilerParams(
            dimension_semantics=("parallel", "arbitrary"), vmem_limit_bytes=VMEM_LIMIT),
        name="in_proj",
    )(x, nw, mod, mod, w_main, w_small)


def _outproj_body(m0_ref, m1_ref, m2_ref, m3_ref, w_ref, x_ref, g_ref, o_ref):
    acc = None
    for p, m_ref in enumerate((m0_ref, m1_ref, m2_ref, m3_ref)):
        part = jnp.dot(m_ref[...], w_ref[p * GROUP_WIDTH:(p + 1) * GROUP_WIDTH, :], preferred_element_type=F32)
        acc = part if acc is None else acc + part
    o_ref[...] = x_ref[...] + g_ref[...] * acc


def _outproj(parts, w, x, mod):
    tm, tn = 1024, 1024
    return pl.pallas_call(
        _outproj_body,
        out_shape=jax.ShapeDtypeStruct((N_TOK, D_MODEL), F32),
        grid=(N_TOK // tm, D_MODEL // tn),
        in_specs=[pl.BlockSpec((tm, GROUP_WIDTH), lambda i, j: (i, 0))] * 4 + [
            pl.BlockSpec((D_MODEL, tn), lambda i, j: (0, j)),
            pl.BlockSpec((tm, tn), lambda i, j: (i, j)),
            pl.BlockSpec((None, None, 1, tn), lambda i, j: (_seq_row(i, tm), 2, 0, j)),
        ],
        out_specs=pl.BlockSpec((tm, tn), lambda i, j: (i, j)),
        compiler_params=pltpu.CompilerParams(
            dimension_semantics=("parallel", "arbitrary"), vmem_limit_bytes=VMEM_LIMIT),
        name="out_proj",
    )(*parts, w, x, mod)


def _ffn_body(x_ref, nw_ref, sc_ref, sh_ref, g_ref, w1_ref, w3_ref, w2_ref, fw_ref, o_ref, h_ref, acc_ref,
              *, final_norm):
    k = pl.program_id(1)

    @pl.when(k == 0)
    def _():
        h_ref[...] = _norm_mod(x_ref[...], nw_ref[...], sc_ref[...], sh_ref[...]).astype(BF16)
        acc_ref[...] = jnp.zeros_like(acc_ref)

    h = h_ref[...]
    a = jnp.dot(h, w1_ref[...], preferred_element_type=F32)
    b = jnp.dot(h, w3_ref[...], preferred_element_type=F32)
    g = (a * jax.nn.sigmoid(a) * b).astype(BF16)
    acc_ref[...] += jnp.dot(g, w2_ref[...], preferred_element_type=F32)

    @pl.when(k == pl.num_programs(1) - 1)
    def _():
        y = x_ref[...] + g_ref[...] * acc_ref[...]
        if final_norm:
            ms = jnp.mean(y * y, axis=-1, keepdims=True)
            y = y * lax.rsqrt(ms + EPS) * fw_ref[...]
        o_ref[...] = y


def _ffn(x, nw, mod, w1, w3, w2, fw, final_norm):
    tm, th = 512, 512
    return pl.pallas_call(
        functools.partial(_ffn_body, final_norm=final_norm),
        out_shape=jax.ShapeDtypeStruct((N_TOK, D_MODEL), F32),
        grid=(N_TOK // tm, FFN_HIDDEN // th),
        in_specs=[
            pl.BlockSpec((tm, D_MODEL), lambda i, k: (i, 0)),
            pl.BlockSpec((1, D_MODEL), lambda i, k: (0, 0)),
            pl.BlockSpec((None, None, 1, D_MODEL), lambda i, k: (_seq_row(i, tm), 4, 0, 0)),
            pl.BlockSpec((None, None, 1, D_MODEL), lambda i, k: (_seq_row(i, tm), 3, 0, 0)),
            pl.BlockSpec((None, None, 1, D_MODEL), lambda i, k: (_seq_row(i, tm), 5, 0, 0)),
            pl.BlockSpec((D_MODEL, th), lambda i, k: (0, k)),
            pl.BlockSpec((D_MODEL, th), lambda i, k: (0, k)),
            pl.BlockSpec((th, D_MODEL), lambda i, k: (k, 0)),
            pl.BlockSpec((1, D_MODEL), lambda i, k: (0, 0)),
        ],
        out_specs=pl.BlockSpec((tm, D_MODEL), lambda i, k: (i, 0)),
        scratch_shapes=[pltpu.VMEM((tm, D_MODEL), BF16), pltpu.VMEM((tm, D_MODEL), F32)],
        compiler_params=pltpu.CompilerParams(
            dimension_semantics=("parallel", "arbitrary"), vmem_limit_bytes=VMEM_LIMIT),
        name="ffn",
    )(x, nw, mod, mod, mod, w1, w3, w2, fw)


def _head_layernorm(x):
    mu = jnp.mean(x, axis=-1, keepdims=True)
    xc = x - mu
    return xc * lax.rsqrt(jnp.mean(xc * xc, axis=-1, keepdims=True) + EPS)


def _rmsnorm(x, w):
    y = x * lax.rsqrt(jnp.mean(x * x, axis=-1, keepdims=True) + EPS)
    return y * w


def _l2norm(x):
    return x * lax.rsqrt(jnp.sum(x * x, axis=-1, keepdims=True) + EPS)


def _heads(t, n_heads):
    b, l, _ = t.shape
    return t.reshape(b, l, n_heads, -1).transpose(0, 2, 1, 3)


def _merge(t):
    b, h, l, d = t.shape
    return t.transpose(0, 2, 1, 3).reshape(b, l, h * d)


def _chunks(t, size):
    return t.reshape(t.shape[:2] + (t.shape[2] // size, size) + t.shape[3:])


def _rev(t):
    return jnp.flip(t, axis=2)


def _axial_rope(l):
    n_rows = l // GRID_W
    t_row = jnp.repeat(jnp.arange(n_rows, dtype=F32), GRID_W)
    t_col = jnp.tile(jnp.arange(GRID_W, dtype=F32), n_rows)
    n_freq = HEAD_DIM // 4
    inv = ROPE_BASE ** (-jnp.arange(n_freq, dtype=F32) / n_freq)
    ang = jnp.concatenate([t_row[:, None] * inv, t_col[:, None] * inv], axis=-1)
    return jnp.cos(ang), jnp.sin(ang)


def _apply_rope(x, cos, sin):
    x1, x2 = jnp.split(x, 2, axis=-1)
    return jnp.concatenate([x1 * cos - x2 * sin, x1 * sin + x2 * cos], axis=-1)


def _dwconv(x, w):
    ch = x.shape[-1]
    pad = (GDN_CONV - 1) // 2
    return lax.conv_general_dilated(x, w[:, None, :], window_strides=(1,), padding=[(pad, pad)],
                                    dimension_numbers=('NWC', 'WIO', 'NWC'), feature_group_count=ch)


def _retention(q, k, v, log_gamma, s0):
    b, h, l, d = q.shape
    qc, kc, vc = _chunks(q, CHUNK), _chunks(k, CHUNK), _chunks(v, CHUNK)
    pos = jnp.arange(CHUNK, dtype=F32)
    rel = pos[:, None] - pos[None, :]
    intra = jnp.where(rel >= 0, jnp.exp(jnp.maximum(rel, 0.0) * log_gamma[:, None, None]), 0.0)
    q_dec = jnp.exp((pos + 1.0) * log_gamma[:, None])
    k_dec = jnp.exp((CHUNK - 1.0 - pos) * log_gamma[:, None])
    c_dec = jnp.exp(CHUNK * log_gamma)[None, :, None, None]
    scores = jnp.einsum('bhnid,bhnjd->bhnij', qc, kc) * intra[None, :, None]
    o_intra = jnp.einsum('bhnij,bhnje->bhnie', scores, vc)
    kv = jnp.einsum('bhnjd,hj,bhnje->bhnde', kc, k_dec, vc)

    def step(s, kv_n):
        return c_dec * s + kv_n, s

    s_fin, s_prev = lax.scan(step, s0, jnp.moveaxis(kv, 2, 0))
    o_inter = jnp.einsum('bhnid,hi,bhnde->bhnie', qc, q_dec, jnp.moveaxis(s_prev, 0, 2))
    return (o_intra + o_inter).reshape(b, h, l, d), s_fin


def _gated_delta(q, k, v, log_alpha, beta, s0):
    b, h, l, _ = q.shape
    qc, kc, vc = _chunks(q, CHUNK), _chunks(k, CHUNK), _chunks(v, CHUNK)
    g = jnp.cumsum(_chunks(log_alpha, CHUNK), axis=-1)
    bc = _chunks(beta, CHUNK)
    idx = jnp.arange(CHUNK)
    incl = idx[:, None] >= idx[None, :]
    strict = idx[:, None] > idx[None, :]
    decay = jnp.exp(jnp.where(incl, g[..., :, None] - g[..., None, :], -jnp.inf))
    kb = kc * bc[..., None]
    a_mat = jnp.where(strict, jnp.einsum('bhnid,bhnjd->bhnij', kb, kc) * decay, 0.0)
    eye = jnp.eye(CHUNK, dtype=F32)
    t_mat = lax.linalg.triangular_solve(a_mat + eye, jnp.broadcast_to(eye, a_mat.shape),
                                        left_side=True, lower=True, unit_diagonal=True)
    u = jnp.einsum('bhnij,bhnje->bhnie', t_mat, vc * bc[..., None])
    w = jnp.einsum('bhnij,bhnjd->bhnid', t_mat, kb * jnp.exp(g)[..., None])
    attn = jnp.einsum('bhnid,bhnjd->bhnij', qc, kc) * decay
    qg = qc * jnp.exp(g)[..., None]
    kt = kc * jnp.exp(g[..., -1:] - g)[..., None]
    cd = jnp.exp(g[..., -1])

    def step(s, xs):
        u_n, w_n, qg_n, at_n, kt_n, cd_n = xs
        v_new = u_n - jnp.einsum('bhcd,bhde->bhce', w_n, s)
        o = jnp.einsum('bhcd,bhde->bhce', qg_n, s) + jnp.einsum('bhcs,bhse->bhce', at_n, v_new)
        s = s * cd_n[..., None, None] + jnp.einsum('bhcd,bhce->bhde', kt_n, v_new)
        return s, o

    xs = (jnp.moveaxis(u, 2, 0), jnp.moveaxis(w, 2, 0), jnp.moveaxis(qg, 2, 0),
          jnp.moveaxis(attn, 2, 0), jnp.moveaxis(kt, 2, 0), jnp.moveaxis(cd, 2, 0))
    s_fin, o = lax.scan(step, s0, xs)
    return jnp.moveaxis(o, 0, 2).reshape(b, h, l, -1), s_fin


def _gla(q, k, v, log_f, s0):
    b, h, l, _ = q.shape
    qc, kc, vc, fc = (_chunks(q, HG_CHUNK), _chunks(k, HG_CHUNK), _chunks(v, HG_CHUNK), _chunks(log_f, HG_CHUNK))
    cum = jnp.cumsum(fc, axis=-2)
    idx = jnp.arange(HG_CHUNK)
    incl = (idx[:, None] >= idx[None, :])[:, :, None]
    dec = jnp.exp(jnp.where(incl, cum[..., :, None, :] - cum[..., None, :, :], -jnp.inf))
    attn = jnp.einsum('bhnid,bhnjd,bhnijd->bhnij', qc, kc, dec)
    o_intra = jnp.einsum('bhnij,bhnje->bhnie', attn, vc)
    q_in = qc * jnp.exp(cum)
    k_tail = kc * jnp.exp(cum[..., -1:, :] - cum)
    cd = jnp.exp(cum[..., -1, :])
    kv = jnp.einsum('bhnjd,bhnje->bhnde', k_tail, vc)

    def step(s, xs):
        kv_n, cd_n = xs
        return cd_n[..., None] * s + kv_n, s

    s_fin, s_prev = lax.scan(step, s0, (jnp.moveaxis(kv, 2, 0), jnp.moveaxis(cd, 2, 0)))
    o_inter = jnp.einsum('bhnid,bhnde->bhnie', q_in, jnp.moveaxis(s_prev, 0, 2))
    return (o_intra + o_inter).reshape(b, h, l, -1), s_fin


def _cplx_combine(e1, e2):
    a1r, a1i, b1r, b1i = e1
    a2r, a2i, b2r, b2i = e2
    return (a1r * a2r - a1i * a2i, a1r * a2i + a1i * a2r,
            a2r * b1r - a2i * b1i + b2r, a2r * b1i + a2i * b1r + b2i)


def _s5(u, a_re, a_im, log_step, b_re, b_im, c_re, c_im, x0_re, x0_im):
    dt = jnp.exp(log_step)[:, None]
    mag = jnp.exp(a_re * dt)
    ab_re = mag * jnp.cos(a_im * dt)
    ab_im = mag * jnp.sin(a_im * dt)
    den = a_re * a_re + a_im * a_im
    nr = ab_re - 1.0
    f_re = (nr * a_re + ab_im * a_im) / den
    f_im = (ab_im * a_re - nr * a_im) / den
    bb_re = f_re[..., None] * b_re - f_im[..., None] * b_im
    bb_im = f_re[..., None] * b_im + f_im[..., None] * b_re
    bu_re = jnp.einsum('blgc,gnc->blgn', u, bb_re)
    bu_im = jnp.einsum('blgc,gnc->blgn', u, bb_im)
    bu_re = bu_re.at[:, 0].add(ab_re * x0_re - ab_im * x0_im)
    bu_im = bu_im.at[:, 0].add(ab_re * x0_im + ab_im * x0_re)
    a_full_re = jnp.broadcast_to(ab_re, bu_re.shape)
    a_full_im = jnp.broadcast_to(ab_im, bu_im.shape)
    _, _, x_re, x_im = lax.associative_scan(_cplx_combine, (a_full_re, a_full_im, bu_re, bu_im), axis=1)
    y = jnp.einsum('blgn,gcn->blgc', x_re, c_re) - jnp.einsum('blgn,gcn->blgc', x_im, c_im)
    return y, x_re[:, -1], x_im[:, -1]


def _mixer_jax(pm, ps, lp, lower_bound, s_ret, s_gdn, s_hg, s_s5_re, s_s5_im, rope):
    bsz, l, _ = pm.shape
    w = GROUP_WIDTH
    sl = lambda i, n=1: pm[..., i * w:(i + n) * w]
    r_q, r_k, r_v, r_g, d_q, d_k, d_v, d_g = (sl(i) for i in range(8))
    h_q, h_f, h_i, h_g, s_u = sl(8), sl(9, 2), sl(11), sl(12), sl(13)
    d_a, d_b = ps[..., :8], ps[..., 8:16]

    q = _heads(r_q, N_HEADS)
    k = _heads(r_k, N_HEADS)
    v = _heads(r_v, N_HEADS)
    if rope is not None:
        q = _apply_rope(q, *rope)
        k = _apply_rope(k, *rope)
    k = k * HEAD_DIM ** -0.5
    log_gamma = jax.nn.log_sigmoid(lp['ret_decay_logit'])
    o_f, rs_f = _retention(q, k, v, log_gamma[0], s_ret[:, 0])
    o_b, rs_b = _retention(_rev(q), _rev(k), _rev(v), log_gamma[1], s_ret[:, 1])
    ret_out = _merge(_head_layernorm(o_f + _rev(o_b))) * jax.nn.silu(r_g)

    qkv = jax.nn.silu(_dwconv(jnp.concatenate([d_q, d_k, d_v], axis=-1), lp['gdn_conv']))
    gq, gk, gv = jnp.split(qkv, 3, axis=-1)
    q = _l2norm(_heads(gq, N_HEADS)) * HEAD_DIM ** -0.5
    k = _l2norm(_heads(gk, N_HEADS))
    v = _heads(gv, N_HEADS)
    a = d_a.reshape(bsz, l, 2, N_HEADS)
    bt = d_b.reshape(bsz, l, 2, N_HEADS)
    log_alpha = (-jnp.exp(lp['gdn_a_log']) * jax.nn.softplus(a + lp['gdn_dt_bias'])).transpose(2, 0, 3, 1)
    beta = jax.nn.sigmoid(bt).transpose(2, 0, 3, 1)
    o_f, gs_f = _gated_delta(q, k, v, log_alpha[0], beta[0], s_gdn[:, 0])
    o_b, gs_b = _gated_delta(_rev(q), _rev(k), _rev(v), _rev(log_alpha[1]), _rev(beta[1]), s_gdn[:, 1])
    gdn_out = _merge(_rmsnorm(o_f + _rev(o_b), lp['gdn_norm_w'])) * jax.nn.silu(d_g)

    q = _heads(jax.nn.silu(h_q), N_HEADS) * HEAD_DIM ** -0.5
    v = _heads(h_i, N_HEADS)
    fx = h_f.reshape(bsz, l, 2, w)
    log_f = jnp.logaddexp(jnp.log(jnp.maximum(lower_bound, LB_FLOOR)),
                          jnp.log1p(-lower_bound) + jax.nn.log_sigmoid(fx))
    key_in = (1.0 - lower_bound) * jax.nn.sigmoid(-fx)
    o_f, hs_f = _gla(q, _heads(key_in[:, :, 0], N_HEADS), v, _heads(log_f[:, :, 0], N_HEADS), s_hg[:, 0])
    o_b, hs_b = _gla(_rev(q), _rev(_heads(key_in[:, :, 1], N_HEADS)), _rev(v),
                     _rev(_heads(log_f[:, :, 1], N_HEADS)), s_hg[:, 1])
    hg_out = _merge(_rmsnorm(o_f + _rev(o_b), lp['hg_norm_w'])) * jax.nn.silu(h_g)

    u = s_u.reshape(bsz, l, S5_GROUPS, S5_CH)
    p = [lp[nm] for nm in ('s5_a_re', 's5_a_im', 's5_log_step', 's5_b_re', 's5_b_im', 's5_c_re', 's5_c_im')]
    y_f, xr_f, xi_f = _s5(u, p[0][0], p[1][0], p[2][0], p[3][0], p[4][0], p[5][0], p[6][0],
                          s_s5_re[:, 0], s_s5_im[:, 0])
    y_b, xr_b, xi_b = _s5(jnp.flip(u, 1), p[0][1], p[1][1], p[2][1], p[3][1], p[4][1], p[5][1], p[6][1],
                          s_s5_re[:, 1], s_s5_im[:, 1])
    y = (y_f + jnp.flip(y_b, 1)).reshape(bsz, l, w) + lp['s5_d'] * s_u
    z = jax.nn.gelu(y)
    s5_out = z * jax.nn.sigmoid(z @ lp['s5_glu_w'] + lp['s5_glu_b'])

    mixed = jnp.concatenate([ret_out, gdn_out, hg_out, s5_out], axis=-1).astype(BF16)
    new_states = (jnp.stack([rs_f, rs_b], axis=1), jnp.stack([gs_f, gs_b], axis=1),
                  jnp.stack([hs_f, hs_b], axis=1), jnp.stack([xr_f, xr_b], axis=1),
                  jnp.stack([xi_f, xi_b], axis=1))
    return mixed, new_states


MIX_CHUNK = 128
DIRS = 2
NT_DIMS = (((1,), (1,)), ((), ()))
TN_DIMS = (((0,), (0,)), ((), ()))


def _dot(a, b):
    return jnp.dot(a.astype(BF16), b.astype(BF16), preferred_element_type=F32)


def _dot_nt(a, b):
    return lax.dot_general(a.astype(BF16), b.astype(BF16), NT_DIMS, preferred_element_type=F32)


def _dot_tn(a, b):
    return lax.dot_general(a.astype(BF16), b.astype(BF16), TN_DIMS, preferred_element_type=F32)


def _silu(x):
    return x * jax.nn.sigmoid(x)


def _seq_spec(seq_len, width, row_off, col_blk):
    return pl.BlockSpec((seq_len, width), lambda b: (row_off + b, col_blk))


def _mix_params(n_par):
    return pltpu.CompilerParams(dimension_semantics=("parallel",) * n_par, vmem_limit_bytes=VMEM_LIMIT)


def _ret_body(*refs, n_chunks, use_rope, has_s0):
    it = iter(refs)
    lg_ref, q_ref, k_ref, v_ref, g_ref = (next(it) for _ in range(5))
    cos_ref, sin_ref = (next(it), next(it)) if use_rope else (None, None)
    s0_ref = next(it) if has_s0 else None
    o_ref, sf_ref, acc_ref, s_ref, intra_ref, qd_ref, kd_ref, cd_ref = (next(it) for _ in range(8))
    C, H = MIX_CHUNK, N_HEADS

    row = lax.broadcasted_iota(jnp.int32, (C, C), 0)
    col = lax.broadcasted_iota(jnp.int32, (C, C), 1)
    rel = (row - col).astype(F32)
    pos = lax.broadcasted_iota(jnp.int32, (C, HEAD_DIM), 0).astype(F32)
    for d in range(DIRS):
        for h in range(H):
            lg = lg_ref[d, h]
            if d == 0:
                intra_ref[d, h] = jnp.where(rel >= 0, jnp.exp(jnp.maximum(rel, 0.0) * lg), 0.0)
                qd_ref[d, h] = jnp.exp((pos + 1.0) * lg)
                kd_ref[d, h] = jnp.exp((C - 1.0 - pos) * lg)
            else:
                intra_ref[d, h] = jnp.where(rel <= 0, jnp.exp(jnp.maximum(-rel, 0.0) * lg), 0.0)
                qd_ref[d, h] = jnp.exp((C - pos) * lg)
                kd_ref[d, h] = jnp.exp(pos * lg)
            cd_ref[d, h] = jnp.exp(jnp.full((C, HEAD_DIM), C, F32) * lg)
    if has_s0:
        s_ref[...] = s0_ref[...]
    else:
        s_ref[...] = jnp.zeros_like(s_ref)
    acc_ref[...] = jnp.zeros_like(acc_ref)

    def chunk_step(n, carry):
        for d in range(DIRS):
            c = n if d == 0 else n_chunks - 1 - n
            rows = pl.ds(pl.multiple_of(c * C, C), C)
            for h in range(H):
                cols = slice(h * HEAD_DIM, (h + 1) * HEAD_DIM)
                q = q_ref[rows, cols]
                k = k_ref[rows, cols]
                v = v_ref[rows, cols].astype(BF16)
                if use_rope:
                    cs, sn = cos_ref[rows, :], sin_ref[rows, :]
                    q = q * cs + pltpu.roll(q, HEAD_DIM // 2, 1) * sn
                    k = k * cs + pltpu.roll(k, HEAD_DIM // 2, 1) * sn
                k = k * HEAD_DIM ** -0.5
                s = s_ref[d, h]
                scores = _dot_nt(q, k) * intra_ref[d, h]
                o = _dot(scores, v) + _dot(q * qd_ref[d, h], s)
                acc_ref[rows, cols] += o
                s_ref[d, h] = cd_ref[d, h] * s + _dot_tn(k * kd_ref[d, h], v)
        return carry

    lax.fori_loop(0, n_chunks, chunk_step, 0)
    sf_ref[...] = s_ref[...]

    def finish(n, carry):
        rows = pl.ds(pl.multiple_of(n * C, C), C)
        for h in range(H):
            cols = slice(h * HEAD_DIM, (h + 1) * HEAD_DIM)
            o = acc_ref[rows, cols]
            mu = jnp.mean(o, axis=-1, keepdims=True)
            oc = o - mu
            y = oc * lax.rsqrt(jnp.mean(oc * oc, axis=-1, keepdims=True) + EPS)
            o_ref[rows, cols] = (y * _silu(g_ref[rows, cols])).astype(BF16)
        return carry

    lax.fori_loop(0, n_chunks, finish, 0)


def _retention_pallas(pm, log_gamma, n_seq, seq_len, row_off, rope2, s0, layer):
    use_rope, has_s0 = rope2 is not None, s0 is not None
    st_shape = (DIRS, N_HEADS, HEAD_DIM, HEAD_DIM)
    in_specs = [pl.BlockSpec(memory_space=pltpu.SMEM)]
    in_specs += [_seq_spec(seq_len, GROUP_WIDTH, row_off, cb) for cb in range(4)]
    args = [log_gamma, pm, pm, pm, pm]
    if use_rope:
        in_specs += [pl.BlockSpec((seq_len, HEAD_DIM), lambda b: (0, 0))] * 2
        args += list(rope2)
    if has_s0:
        in_specs.append(pl.BlockSpec((None, None) + st_shape, lambda b: (b, layer, 0, 0, 0, 0)))
        args.append(s0)
    return pl.pallas_call(
        functools.partial(_ret_body, n_chunks=seq_len // MIX_CHUNK, use_rope=use_rope, has_s0=has_s0),
        out_shape=(jax.ShapeDtypeStruct((n_seq * seq_len, GROUP_WIDTH), BF16),
                   jax.ShapeDtypeStruct((n_seq,) + st_shape, F32)),
        grid=(n_seq,),
        in_specs=in_specs,
        out_specs=(pl.BlockSpec((seq_len, GROUP_WIDTH), lambda b: (b, 0)),
                   pl.BlockSpec((None,) + st_shape, lambda b: (b, 0, 0, 0, 0))),
        scratch_shapes=[pltpu.VMEM((seq_len, GROUP_WIDTH), F32), pltpu.VMEM(st_shape, F32),
                        pltpu.VMEM((DIRS, N_HEADS, MIX_CHUNK, MIX_CHUNK), F32),
                        pltpu.VMEM((DIRS, N_HEADS, MIX_CHUNK, HEAD_DIM), F32),
                        pltpu.VMEM((DIRS, N_HEADS, MIX_CHUNK, HEAD_DIM), F32),
                        pltpu.VMEM((DIRS, N_HEADS, MIX_CHUNK, HEAD_DIM), F32)],
        compiler_params=_mix_params(1),
        name="retention",
    )(*args)


def _rope_tables(l):
    cos, sin = _axial_rope(l)
    return jnp.concatenate([cos, cos], axis=-1), jnp.concatenate([-sin, sin], axis=-1)


S5_HALF_G = S5_GROUPS // 2
S5_HALF_U = S5_HALF_G * S5_CH
S5_HALF_X = S5_HALF_G * S5_N
S5_X = S5_GROUPS * S5_N
S5_TC = 256
S5_BLK = 8
S5_TABS = 4


def _s5_tables(a_re, a_im, log_step, b_re, b_im, c_re, c_im):
    dt = jnp.exp(log_step)[..., None]
    mag = jnp.exp(a_re * dt)
    ab_re = mag * jnp.cos(a_im * dt)
    ab_im = mag * jnp.sin(a_im * dt)
    den = a_re * a_re + a_im * a_im
    nr = ab_re - 1.0
    f_re = (nr * a_re + ab_im * a_im) / den
    f_im = (ab_im * a_re - nr * a_im) / den
    bb_re = f_re[..., None] * b_re - f_im[..., None] * b_im
    bb_im = f_re[..., None] * b_im + f_im[..., None] * b_re
    eye = jnp.eye(S5_HALF_G, dtype=F32)

    def in_mat(bb):
        bb = bb.reshape(DIRS, 2, S5_HALF_G, S5_N, S5_CH)
        return jnp.einsum('dhgnc,gk->dhgckn', bb, eye).reshape(DIRS, 2, S5_HALF_U, S5_HALF_X)

    def out_mat(cc):
        cc = cc.reshape(DIRS, 2, S5_HALF_G, S5_CH, S5_N)
        return jnp.einsum('dhgcn,gk->dhgnkc', cc, eye).reshape(DIRS, 2, S5_HALF_X, S5_HALF_U)

    bm = jnp.concatenate([in_mat(bb_re), in_mat(bb_im)], axis=-1).astype(BF16)
    cm = jnp.concatenate([out_mat(c_re), -out_mat(c_im)], axis=-2).astype(BF16)
    t = jnp.arange(S5_BLK, dtype=F32)
    order = jnp.stack([t, S5_BLK - 1.0 - t])
    shifts = 2.0 ** jnp.arange(S5_TABS - 1, dtype=F32)
    expo = jnp.concatenate([jnp.where(order[:, None, :] >= shifts[None, :, None], shifts[None, :, None], jnp.nan),
                            order[:, None, :] + 1.0], axis=1)
    live = ~jnp.isnan(expo)
    e = jnp.where(live, expo, 0.0)[..., None]
    adt_re = (a_re * dt).reshape(DIRS, 1, 1, S5_X)
    adt_im = (a_im * dt).reshape(DIRS, 1, 1, S5_X)
    pmag = jnp.where(live[..., None], jnp.exp(e * adt_re), 0.0)
    pw_re = pmag * jnp.cos(e * adt_im)
    pw_im = pmag * jnp.sin(e * adt_im)
    return bm, cm, pw_re, pw_im


def _gelu_tanh(x):
    return 0.5 * x * (1.0 + jnp.tanh(math.sqrt(2.0 / math.pi) * (x + 0.044715 * (x * x * x))))


def _s5_body(*refs, seq_len, has_s0):
    it = iter(refs)
    u_ref, bm_ref, cm_ref, pwr_ref, pwi_ref, d_ref, gw_ref, gb_ref = (next(it) for _ in range(8))
    x0r_ref, x0i_ref = (next(it), next(it)) if has_s0 else (None, None)
    o_ref, sfr_ref, sfi_ref, y_ref, xr_ref, xi_ref = (next(it) for _ in range(6))
    tc = min(seq_len, S5_TC)
    n_tiles = seq_len // tc
    n_blk = tc // S5_BLK

    y_ref[...] = d_ref[...] * u_ref[...]
    for d in range(DIRS):
        last = S5_BLK - 1 if d == 0 else 0
        for hf in range(2):
            xs = slice(hf * S5_HALF_X, (hf + 1) * S5_HALF_X)
            us = slice(hf * S5_HALF_U, (hf + 1) * S5_HALF_U)

            def scan_block(j, carry, d=d, xs=xs, last=last):
                car_re, car_im = carry
                blk = j if d == 0 else n_blk - 1 - j
                rows = pl.ds(pl.multiple_of(blk * S5_BLK, S5_BLK), S5_BLK)
                xr, xi = xr_ref[rows, :], xi_ref[rows, :]
                for k in range(S5_TABS - 1):
                    s = 1 << k
                    shift = s if d == 0 else S5_BLK - s
                    p_re, p_im = pwr_ref[d, k, :, xs], pwi_ref[d, k, :, xs]
                    sr, si = pltpu.roll(xr, shift, 0), pltpu.roll(xi, shift, 0)
                    xr, xi = xr + p_re * sr - p_im * si, xi + p_re * si + p_im * sr
                p_re, p_im = pwr_ref[d, S5_TABS - 1, :, xs], pwi_ref[d, S5_TABS - 1, :, xs]
                xr, xi = xr + p_re * car_re - p_im * car_im, xi + p_re * car_im + p_im * car_re
                xr_ref[rows, :] = xr
                xi_ref[rows, :] = xi
                return xr[last:last + 1, :], xi[last:last + 1, :]

            def scan_tile(i, carry, d=d, hf=hf, us=us, scan_block=scan_block):
                tile = i if d == 0 else n_tiles - 1 - i
                rows = pl.ds(pl.multiple_of(tile * tc, tc), tc)
                bu = _dot(u_ref[rows, us], bm_ref[d, hf])
                xr_ref[...] = bu[:, :S5_HALF_X]
                xi_ref[...] = bu[:, S5_HALF_X:]
                carry = lax.fori_loop(0, n_blk, scan_block, carry)
                y_ref[rows, us] += (_dot(xr_ref[...], cm_ref[d, hf, :S5_HALF_X, :])
                                    + _dot(xi_ref[...], cm_ref[d, hf, S5_HALF_X:, :]))
                return carry

            if has_s0:
                carry0 = (x0r_ref[d, :, xs], x0i_ref[d, :, xs])
            else:
                carry0 = (jnp.zeros((1, S5_HALF_X), F32), jnp.zeros((1, S5_HALF_X), F32))
            car_re, car_im = lax.fori_loop(0, n_tiles, scan_tile, carry0)
            sfr_ref[d, :, xs] = car_re
            sfi_ref[d, :, xs] = car_im

    z = _gelu_tanh(y_ref[...])
    o_ref[...] = (z * jax.nn.sigmoid(_dot(z, gw_ref[...]) + gb_ref[...])).astype(BF16)


def _s5_pallas(pm, tables, s5_d, glu_w, glu_b, n_seq, seq_len, row_off, x0, layer):
    bm, cm, pw_re, pw_im = tables
    has_s0 = x0 is not None
    full = lambda shape: pl.BlockSpec(shape, lambda b: (0,) * len(shape))
    in_specs = [_seq_spec(seq_len, GROUP_WIDTH, row_off, 13),
                full(bm.shape), full(cm.shape), full(pw_re.shape), full(pw_im.shape),
                full((1, GROUP_WIDTH)), full((GROUP_WIDTH, GROUP_WIDTH)), full((1, GROUP_WIDTH))]
    args = [pm, bm, cm, pw_re, pw_im, s5_d.reshape(1, GROUP_WIDTH), glu_w.astype(BF16),
            glu_b.reshape(1, GROUP_WIDTH)]
    if has_s0:
        in_specs += [pl.BlockSpec((None, None, DIRS, 1, S5_X), lambda b: (b, layer, 0, 0, 0))] * 2
        args += list(x0)
    st = jax.ShapeDtypeStruct((n_seq, DIRS, 1, S5_X), F32)
    st_spec = pl.BlockSpec((None, DIRS, 1, S5_X), lambda b: (b, 0, 0, 0))
    tc = min(seq_len, S5_TC)
    return pl.pallas_call(
        functools.partial(_s5_body, seq_len=seq_len, has_s0=has_s0),
        out_shape=(jax.ShapeDtypeStruct((n_seq * seq_len, GROUP_WIDTH), BF16), st, st),
        grid=(n_seq,),
        in_specs=in_specs,
        out_specs=(pl.BlockSpec((seq_len, GROUP_WIDTH), lambda b: (b, 0)), st_spec, st_spec),
        scratch_shapes=[pltpu.VMEM((seq_len, GROUP_WIDTH), F32),
                        pltpu.VMEM((tc, S5_HALF_X), F32), pltpu.VMEM((tc, S5_HALF_X), F32)],
        compiler_params=_mix_params(1),
        name="s5",
    )(*args)


GLA_LEVELS = 7


def _log_sigmoid(x):
    return jnp.minimum(x, 0.0) - jnp.log1p(jnp.exp(-jnp.abs(x)))


def _logaddexp(a, b):
    return jnp.maximum(a, b) + jnp.log1p(jnp.exp(-jnp.abs(a - b)))


def _chunk_cumsum_rows(x, rowi):
    s = 1
    while s < MIX_CHUNK:
        x = x + jnp.where(rowi >= s, pltpu.roll(x, s, 0), 0.0)
        s *= 2
    return x


def _gla_body(*refs, n_chunks, has_s0):
    it = iter(refs)
    q_ref, f0_ref, f1_ref, i_ref, g_ref, lb_ref, nw_ref = (next(it) for _ in range(7))
    s0_ref = next(it) if has_s0 else None
    o_ref, sf_ref, acc_ref, s_ref, code_ref = (next(it) for _ in range(5))
    C, H = MIX_CHUNK, N_HEADS
    f_refs = (f0_ref, f1_ref)

    rowi = lax.broadcasted_iota(jnp.int32, (C, HEAD_DIM), 0)
    ri = lax.broadcasted_iota(jnp.int32, (C, C), 0)
    ci = lax.broadcasted_iota(jnp.int32, (C, C), 1)
    top_bit = 31 - lax.clz(ri ^ ci)
    code_ref[...] = jnp.where(ri > ci, top_bit, jnp.where(ri < ci, -1 - top_bit, GLA_LEVELS))
    if has_s0:
        s_ref[...] = s0_ref[...]
    else:
        s_ref[...] = jnp.zeros_like(s_ref)
    acc_ref[...] = jnp.zeros_like(acc_ref)

    def chunk_step(n, carry):
        for d in range(DIRS):
            c = n if d == 0 else n_chunks - 1 - n
            rows = pl.ds(pl.multiple_of(c * C, C), C)
            for h in range(H):
                cols = slice(h * HEAD_DIM, (h + 1) * HEAD_DIM)
                code = code_ref[...]
                q = _silu(q_ref[rows, cols]) * HEAD_DIM ** -0.5
                v = i_ref[rows, cols].astype(BF16)
                fx = f_refs[d][rows, cols]
                lb = lb_ref[d:d + 1, cols]
                logf = _logaddexp(jnp.log(jnp.maximum(lb, LB_FLOOR)), jnp.log1p(-lb) + _log_sigmoid(fx))
                k = (1.0 - lb) * jax.nn.sigmoid(-fx)
                cum = _chunk_cumsum_rows(logf, rowi)
                own = cum
                attn = jnp.where(code == GLA_LEVELS, _dot_nt(q, k), 0.0)
                for lvl in range(GLA_LEVELS):
                    m = 1 << lvl
                    prev = pltpu.roll(own, m, 0)
                    pre = jnp.minimum(cum - prev, 0.0)
                    suf = jnp.minimum(own - cum, 0.0)
                    if d == 0:
                        sc = _dot_nt(q * jnp.exp(pre), k * jnp.exp(suf))
                        hit = code == lvl
                    else:
                        sc = _dot_nt(q * jnp.exp(jnp.minimum(suf + logf, 0.0)),
                                     k * jnp.exp(jnp.minimum(pre - logf, 0.0)))
                        hit = code == -1 - lvl
                    attn = jnp.where(hit, sc, attn)
                    own = jnp.where(((rowi >> lvl) & 1) == 0, pltpu.roll(own, C - m, 0), own)
                tot = own
                st = s_ref[d, h]
                if d == 0:
                    q_in, k_out = q * jnp.exp(cum), k * jnp.exp(tot - cum)
                else:
                    q_in, k_out = q * jnp.exp(tot - cum + logf), k * jnp.exp(cum - logf)
                acc_ref[rows, cols] += _dot(attn, v) + _dot_nt(q_in, st)
                s_ref[d, h] = jnp.exp(tot[0:1, :]) * st + _dot_tn(v, k_out)
        return carry

    lax.fori_loop(0, n_chunks, chunk_step, 0)
    sf_ref[...] = s_ref[...]

    def finish(n, carry):
        rows = pl.ds(pl.multiple_of(n * C, C), C)
        for h in range(H):
            cols = slice(h * HEAD_DIM, (h + 1) * HEAD_DIM)
            o = acc_ref[rows, cols]
            y = o * lax.rsqrt(jnp.mean(o * o, axis=-1, keepdims=True) + EPS) * nw_ref[...]
            o_ref[rows, cols] = (y * _silu(g_ref[rows, cols])).astype(BF16)
        return carry

    lax.fori_loop(0, n_chunks, finish, 0)


def _gla_pallas(pm, lower_bound, norm_w, n_seq, seq_len, row_off, s0_t, layer):
    has_s0 = s0_t is not None
    st_shape = (DIRS, N_HEADS, HEAD_DIM, HEAD_DIM)
    in_specs = [_seq_spec(seq_len, GROUP_WIDTH, row_off, cb) for cb in (8, 9, 10, 11, 12)]
    in_specs += [pl.BlockSpec((DIRS, GROUP_WIDTH), lambda b: (0, 0)), pl.BlockSpec((1, HEAD_DIM), lambda b: (0, 0))]
    args = [pm] * 5 + [lower_bound, norm_w.reshape(1, HEAD_DIM)]
    if has_s0:
        in_specs.append(pl.BlockSpec((None, None) + st_shape, lambda b: (b, layer, 0, 0, 0, 0)))
        args.append(s0_t)
    return pl.pallas_call(
        functools.partial(_gla_body, n_chunks=seq_len // MIX_CHUNK, has_s0=has_s0),
        out_shape=(jax.ShapeDtypeStruct((n_seq * seq_len, GROUP_WIDTH), BF16),
                   jax.ShapeDtypeStruct((n_seq,) + st_shape, F32)),
        grid=(n_seq,),
        in_specs=in_specs,
        out_specs=(pl.BlockSpec((seq_len, GROUP_WIDTH), lambda b: (b, 0)),
                   pl.BlockSpec((None,) + st_shape, lambda b: (b, 0, 0, 0, 0))),
        scratch_shapes=[pltpu.VMEM((seq_len, GROUP_WIDTH), F32), pltpu.VMEM(st_shape, F32),
                        pltpu.VMEM((MIX_CHUNK, MIX_CHUNK), jnp.int32)],
        compiler_params=_mix_params(1),
        name="hgrn2",
    )(*args)


GDN_BLOCK_BITS = 4
GDN_MERGES = 3
N_GATES = DIRS * N_HEADS


def _softplus(x):
    return jnp.maximum(x, 0.0) + jnp.log1p(jnp.exp(-jnp.abs(x)))


def _gdn_body(*refs, seq_len, has_s0):
    it = iter(refs)
    (q_ref, k_ref, v_ref, g_ref, ps_ref, pst_ref, cw_ref, prow_ref, pcol_ref, nw_ref) = (next(it) for _ in range(10))
    s0_ref = next(it) if has_s0 else None
    (o_ref, sf_ref, acc_ref, s_ref, qn_ref, kn_ref, vn_ref, gate_ref, gatet_ref, blev_ref) = (
        next(it) for _ in range(10))
    C, H, L = MIX_CHUNK, N_HEADS, seq_len
    n_chunks = L // C
    w = GROUP_WIDTH

    rowl = lax.broadcasted_iota(jnp.int32, (L, HEAD_DIM), 0)
    for part, (src, dst) in enumerate(((q_ref, qn_ref), (k_ref, kn_ref), (v_ref, vn_ref))):
        for h in range(H):
            cols = slice(h * HEAD_DIM, (h + 1) * HEAD_DIM)
            wc = slice(part * w + h * HEAD_DIM, part * w + (h + 1) * HEAD_DIM)
            x = src[:, cols]
            x_prev = jnp.where(rowl >= 1, pltpu.roll(x, 1, 0), 0.0)
            x_next = jnp.where(rowl < L - 1, pltpu.roll(x, L - 1, 0), 0.0)
            y = _silu(cw_ref[0:1, wc] * x_prev + cw_ref[1:2, wc] * x + cw_ref[2:3, wc] * x_next)
            if part < 2:
                y = y * lax.rsqrt(jnp.sum(y * y, axis=-1, keepdims=True) + EPS)
            if part == 0:
                y = y * HEAD_DIM ** -0.5
            dst[:, cols] = y

    rowi = lax.broadcasted_iota(jnp.int32, (C, HEAD_DIM), 0)
    lane = lax.broadcasted_iota(jnp.int32, (C, HEAD_DIM), 1)
    lane_t = lax.broadcasted_iota(jnp.int32, (2 * N_GATES, C), 1)
    row_t = lax.broadcasted_iota(jnp.int32, (2 * N_GATES, C), 0)
    for c in range(n_chunks):
        rows = slice(c * C, (c + 1) * C)
        a = ps_ref[rows, :]
        la = -jnp.exp(prow_ref[0:1, :]) * _softplus(a + prow_ref[1:2, :])
        pre = _chunk_cumsum_rows(la, rowi)
        tot = jnp.broadcast_to(pre[C - 1:C, :], (C, HEAD_DIM))
        g = jnp.where(lane < N_HEADS, pre, tot - pre + la)
        gate_ref[rows, :] = jnp.where(lane < N_GATES, g,
                                      jnp.where(lane < 2 * N_GATES, jax.nn.sigmoid(a),
                                                pltpu.roll(tot, 2 * N_GATES, 1)))
        at = pst_ref[:, rows]
        lat = -jnp.exp(pcol_ref[:, 0:1]) * _softplus(at + pcol_ref[:, 1:2])
        pre_t, s = lat, 1
        while s < C:
            pre_t = pre_t + jnp.where(lane_t >= s, pltpu.roll(pre_t, s, 1), 0.0)
            s *= 2
        tot_t = jnp.broadcast_to(pre_t[:, C - 1:C], (2 * N_GATES, C))
        g_t = jnp.where(row_t < N_HEADS, pre_t, tot_t - pre_t + lat)
        gatet_ref[:, rows] = jnp.where(row_t < N_GATES, g_t, jax.nn.sigmoid(at))

    ri = lax.broadcasted_iota(jnp.int32, (C, C), 0)
    ci = lax.broadcasted_iota(jnp.int32, (C, C), 1)
    bx = (ri >> GDN_BLOCK_BITS) ^ (ci >> GDN_BLOCK_BITS)
    blev_ref[...] = jnp.where(bx == 0, 0, 32 - lax.clz(bx))
    if has_s0:
        s_ref[...] = s0_ref[...]
    else:
        s_ref[...] = jnp.zeros_like(s_ref)
    acc_ref[...] = jnp.zeros_like(acc_ref)

    units = [(d, h) for d in range(DIRS) for h in range(H)]

    def chunk_step(n, carry):
        eye = jnp.where(ri == ci, 1.0, 0.0)
        blev = blev_ref[...]
        rows_d = [pl.ds(pl.multiple_of(c * C, C), C) for c in (n, n_chunks - 1 - n)]
        incl_d = [ri >= ci, ri <= ci]
        strict_d = [ri > ci, ri < ci]
        q_l, k_l, eg_l, et_l, ek_l, rhs_l, a_l, attn_l = ([] for _ in range(8))
        for d, h in units:
            rows, cols, cg = rows_d[d], slice(h * HEAD_DIM, (h + 1) * HEAD_DIM), d * N_HEADS + h
            q, k, v = qn_ref[rows, cols], kn_ref[rows, cols], vn_ref[rows, cols]
            g_i = jnp.broadcast_to(gate_ref[rows, cg:cg + 1], (C, C))
            b_i = jnp.broadcast_to(gate_ref[rows, N_GATES + cg:N_GATES + cg + 1], (C, C))
            tot = jnp.broadcast_to(gate_ref[rows, 2 * N_GATES + cg:2 * N_GATES + cg + 1], (C, C))
            g_j = gatet_ref[cg:cg + 1, rows]
            decay = jnp.where(incl_d[d], jnp.exp(jnp.minimum(g_i - g_j, 0.0)), 0.0)
            e_g = jnp.exp(g_i)
            kb = k.astype(BF16)
            a_l.append(jnp.where(strict_d[d], _dot_nt(kb, kb) * b_i * decay, 0.0))
            attn_l.append(_dot_nt(q, kb) * decay)
            rhs_l.append(jnp.concatenate([v * b_i, k * (b_i * e_g)], axis=1).astype(BF16))
            q_l.append(q * e_g)
            ek_l.append(k * jnp.exp(tot - g_i))
            et_l.append(jnp.exp(tot))
        b_l = [-jnp.where(blev == 0, a, 0.0) for a in a_l]
        p_l = [eye + b for b in b_l]
        for _ in range(GDN_BLOCK_BITS - 1):
            b_l = [_dot(b, b) for b in b_l]
            p_l = [p + _dot(p, b) for p, b in zip(p_l, b_l)]
        for lvl in range(1, GDN_MERGES + 1):
            ep_l = [_dot(jnp.where(blev == lvl, a, 0.0), p) for a, p in zip(a_l, p_l)]
            p_l = [p - _dot(p, ep) for p, ep in zip(p_l, ep_l)]
        uw_l = [_dot(p, rhs) for p, rhs in zip(p_l, rhs_l)]
        s_l = [s_ref[d, h] for d, h in units]
        vn_l = [uw[:, :HEAD_DIM] - _dot(uw[:, HEAD_DIM:], s) for uw, s in zip(uw_l, s_l)]
        for i, (d, h) in enumerate(units):
            cols = slice(h * HEAD_DIM, (h + 1) * HEAD_DIM)
            acc_ref[rows_d[d], cols] += _dot(jnp.concatenate([q_l[i], attn_l[i]], axis=1),
                                             jnp.concatenate([s_l[i], vn_l[i]], axis=0))
            s_ref[d, h] = s_l[i] * et_l[i] + _dot_tn(ek_l[i], vn_l[i])
        return carry

    lax.fori_loop(0, n_chunks, chunk_step, 0)
    sf_ref[...] = s_ref[...]

    def finish(n, carry):
        rows = pl.ds(pl.multiple_of(n * C, C), C)
        for h in range(H):
            cols = slice(h * HEAD_DIM, (h + 1) * HEAD_DIM)
            o = acc_ref[rows, cols]
            y = o * lax.rsqrt(jnp.mean(o * o, axis=-1, keepdims=True) + EPS) * nw_ref[...]
            o_ref[rows, cols] = (y * _silu(g_ref[rows, cols])).astype(BF16)
        return carry

    lax.fori_loop(0, n_chunks, finish, 0)


def _gdn_pallas(pm, ps, ps_t, conv_w, a_log, dt_bias, norm_w, n_seq, seq_len, row_off, s0, layer):
    has_s0 = s0 is not None
    st_shape = (DIRS, N_HEADS, HEAD_DIM, HEAD_DIM)
    par = jnp.stack([a_log.reshape(N_GATES), dt_bias.reshape(N_GATES)])
    par_row = jnp.pad(par, ((0, 0), (0, HEAD_DIM - N_GATES)))
    par_col = jnp.pad(par.T, ((0, N_GATES), (0, 0)))
    full = lambda shape: pl.BlockSpec(shape, lambda b: (0,) * len(shape))
    in_specs = [_seq_spec(seq_len, GROUP_WIDTH, row_off, cb) for cb in (4, 5, 6, 7)]
    in_specs += [_seq_spec(seq_len, PROJ_SMALL, row_off, 0),
                 pl.BlockSpec((2 * N_GATES, seq_len), lambda b: (0, row_off + b)),
                 full((GDN_CONV, 3 * GROUP_WIDTH)), full((2, HEAD_DIM)), full((2 * N_GATES, 2)), full((1, HEAD_DIM))]
    args = [pm] * 4 + [ps, ps_t, conv_w, par_row, par_col, norm_w.reshape(1, HEAD_DIM)]
    if has_s0:
        in_specs.append(pl.BlockSpec((None, None) + st_shape, lambda b: (b, layer, 0, 0, 0, 0)))
        args.append(s0)
    seq_f32 = lambda width: pltpu.VMEM((seq_len, width), F32)
    return pl.pallas_call(
        functools.partial(_gdn_body, seq_len=seq_len, has_s0=has_s0),
        out_shape=(jax.ShapeDtypeStruct((n_seq * seq_len, GROUP_WIDTH), BF16),
                   jax.ShapeDtypeStruct((n_seq,) + st_shape, F32)),
        grid=(n_seq,),
        in_specs=in_specs,
        out_specs=(pl.BlockSpec((seq_len, GROUP_WIDTH), lambda b: (b, 0)),
                   pl.BlockSpec((None,) + st_shape, lambda b: (b, 0, 0, 0, 0))),
        scratch_shapes=[seq_f32(GROUP_WIDTH), pltpu.VMEM(st_shape, F32),
                        seq_f32(GROUP_WIDTH), seq_f32(GROUP_WIDTH), seq_f32(GROUP_WIDTH),
                        seq_f32(HEAD_DIM), pltpu.VMEM((2 * N_GATES, seq_len), F32),
                        pltpu.VMEM((MIX_CHUNK, MIX_CHUNK), jnp.int32)],
        compiler_params=_mix_params(1),
        name="gated_delta",
    )(*args)


def kernel(x_prompt, x_sample, state_ret, state_gdn, state_hgrn, state_s5_re, state_s5_im, c, c_ctx, norm1_w, norm2_w, final_norm_w, ada_w, ada_b, in_proj, out_proj, ret_decay_logit, gdn_conv, gdn_a_log, gdn_dt_bias, gdn_norm_w, hg_lb_param, hg_norm_w, s5_a_re, s5_a_im, s5_b_re, s5_b_im, s5_c_re, s5_c_im, s5_log_step, s5_d, s5_glu_w, s5_glu_b, ffn_w1, ffn_w3, ffn_w2):
    lb_soft = jax.nn.softmax(hg_lb_param, axis=0)
    lower_bounds = jnp.cumsum(lb_soft, axis=0) - lb_soft[0]
    rope2 = _rope_tables(DEC_SEQ)

    cvec = jnp.zeros((N_SEQ_ROWS, D_MODEL), F32).at[0].set(c_ctx).at[1:1 + DEC_BATCH].set(c)
    mod_all = _ada(cvec, ada_w, ada_b).reshape(DEPTH, N_SEQ_ROWS, N_MOD, 1, D_MODEL)

    w_main = jnp.concatenate([in_proj[:, :, :8 * GROUP_WIDTH], in_proj[:, :, 8 * GROUP_WIDTH + 16:]],
                             axis=-1).astype(BF16)
    w_small = jnp.pad(in_proj[:, :, 8 * GROUP_WIDTH:8 * GROUP_WIDTH + 16],
                      ((0, 0), (0, 0), (0, PROJ_SMALL - 16))).astype(BF16)
    w_out = out_proj.astype(BF16)
    w1, w3, w2 = ffn_w1.astype(BF16), ffn_w3.astype(BF16), ffn_w2.astype(BF16)

    x = jnp.concatenate([x_prompt.reshape(N_CTX_TOK, D_MODEL), x_sample.reshape(N_LAT_TOK, D_MODEL)], axis=0)
    hgrn_t = jnp.swapaxes(state_hgrn, -1, -2)
    s5_x0 = (state_s5_re.reshape(DEC_BATCH, DEPTH, DIRS, 1, S5_X), state_s5_im.reshape(DEC_BATCH, DEPTH, DIRS, 1, S5_X))
    lat_off = N_CTX_TOK // DEC_SEQ
    ctx_states = []
    for i in range(DEPTH):
        mod = mod_all[i]
        pm, ps = _inproj(x, norm1_w[i][None], mod, w_main[i], w_small[i])
        ps_t = ps[:, :2 * N_GATES].T
        lg = jax.nn.log_sigmoid(ret_decay_logit[i])
        tables = _s5_tables(s5_a_re[i], s5_a_im[i], s5_log_step[i], s5_b_re[i], s5_b_im[i], s5_c_re[i], s5_c_im[i])
        ctx = (BATCH, SEQ, 0)
        lat = (DEC_BATCH, DEC_SEQ, lat_off)

        ret_c, rs = _retention_pallas(pm, lg, *ctx, None, None, i)
        ret_l, _ = _retention_pallas(pm, lg, *lat, rope2, state_ret, i)
        gdn_args = (pm, ps, ps_t, gdn_conv[i], gdn_a_log[i], gdn_dt_bias[i], gdn_norm_w[i])
        gdn_c, gs = _gdn_pallas(*gdn_args, *ctx, None, i)
        gdn_l, _ = _gdn_pallas(*gdn_args, *lat, state_gdn, i)
        hg_c, hs = _gla_pallas(pm, lower_bounds[i], hg_norm_w[i], *ctx, None, i)
        hg_l, _ = _gla_pallas(pm, lower_bounds[i], hg_norm_w[i], *lat, hgrn_t, i)
        s5_args = (pm, tables, s5_d[i], s5_glu_w[i], s5_glu_b[i])
        s5_c, xr, xi = _s5_pallas(*s5_args, *ctx, None, i)
        s5_l, _, _ = _s5_pallas(*s5_args, *lat, s5_x0, i)
        ctx_states.append((rs, gs, jnp.swapaxes(hs, -1, -2), xr.reshape(BATCH, DIRS, S5_GROUPS, S5_N),
                           xi.reshape(BATCH, DIRS, S5_GROUPS, S5_N)))

        parts = [jnp.concatenate(pair, axis=0) for pair in ((ret_c, ret_l), (gdn_c, gdn_l), (hg_c, hg_l), (s5_c, s5_l))]
        x = _outproj(parts, w_out[i], x, mod)
        x = _ffn(x, norm2_w[i][None], mod, w1[i], w3[i], w2[i], final_norm_w[None], i == DEPTH - 1)

    y_prompt = x[:N_CTX_TOK].reshape(BATCH, SEQ, D_MODEL)
    y_sample = x[N_CTX_TOK:].reshape(DEC_BATCH, DEC_SEQ, D_MODEL)
    new_states = tuple(jnp.stack([s[j] for s in ctx_states], axis=1) for j in range(5))
    return (y_prompt, y_sample) + new_states
```

```python
import functools
import math

import jax
import jax.numpy as jnp
from jax import lax
from jax.experimental import pallas as pl
from jax.experimental.pallas import tpu as pltpu

F32 = jnp.float32
BF16 = jnp.bfloat16

D_MODEL = 2048
BATCH = 16
SEQ = 256
DEPTH = 2
DEC_BATCH = 8
DEC_SEQ = 1024
GRID_W = 64
HEAD_DIM = 128
GROUP_WIDTH = 512
N_HEADS = 4
S5_CH = 16
S5_GROUPS = 32
S5_N = 64
GDN_CONV = 3
CHUNK = 64
HG_CHUNK = 16
ROPE_BASE = 10000.0
FFN_HIDDEN = 5632
N_MOD = 6
EPS = 1e-6
LB_FLOOR = 1e-30

N_CTX_TOK = BATCH * SEQ
N_LAT_TOK = DEC_BATCH * DEC_SEQ
N_TOK = N_CTX_TOK + N_LAT_TOK
N_SEQ_ROWS = 16
PROJ_MAIN = 14 * GROUP_WIDTH
PROJ_SMALL = 128
VMEM_LIMIT = 56 * 1024 * 1024


def _seq_row(tile, tm):
    n_ctx = N_CTX_TOK // tm
    per_lat = DEC_SEQ // tm
    return jnp.where(tile < n_ctx, 0, 1 + (tile - n_ctx) // per_lat)


def _stream_specs(tm, width, n_col=1):
    n_ctx = N_CTX_TOK // tm

    def ctx_map(i, j):
        return jnp.minimum(i, n_ctx - 1), (jnp.where(i < n_ctx, j, n_col - 1) if n_col > 1 else 0)

    def lat_map(i, j):
        return jnp.maximum(i - n_ctx, 0), (jnp.where(i >= n_ctx, j, 0) if n_col > 1 else 0)

    return pl.BlockSpec((tm, width), ctx_map), pl.BlockSpec((tm, width), lat_map)


def _on_stream(tile, tm, fn):
    n_ctx = N_CTX_TOK // tm
    pl.when(tile < n_ctx)(functools.partial(fn, 0))
    pl.when(tile >= n_ctx)(functools.partial(fn, 1))


def _ada_body(c_ref, w_ref, b_ref, o_ref):
    cv = c_ref[...]
    s = cv * jax.nn.sigmoid(cv)
    o_ref[0] = jnp.dot(s.astype(BF16), w_ref[0].astype(BF16), preferred_element_type=F32) + b_ref[0]


def _ada(cvec, ada_w, ada_b):
    tn = 1024
    n = N_MOD * D_MODEL
    return pl.pallas_call(
        _ada_body,
        out_shape=jax.ShapeDtypeStruct((DEPTH, N_SEQ_ROWS, n), F32),
        grid=(DEPTH, n // tn),
        in_specs=[
            pl.BlockSpec((N_SEQ_ROWS, D_MODEL), lambda l, j: (0, 0)),
            pl.BlockSpec((1, D_MODEL, tn), lambda l, j: (l, 0, j)),
            pl.BlockSpec((1, 1, tn), lambda l, j: (l, 0, j)),
        ],
        out_specs=pl.BlockSpec((1, N_SEQ_ROWS, tn), lambda l, j: (l, 0, j)),
        compiler_params=pltpu.CompilerParams(
            dimension_semantics=("parallel", "parallel"), vmem_limit_bytes=VMEM_LIMIT),
        name="ada_mod",
    )(cvec, ada_w, ada_b.reshape(DEPTH, 1, n))


def _norm_mod(x, nw, sc, sh):
    ms = jnp.mean(x * x, axis=-1, keepdims=True)
    y = x * lax.rsqrt(ms + EPS) * nw
    return y * (1.0 + sc) + sh


PROJ_TM = 1024


def _inproj_body(x_ref, nw_ref, sc_ref, sh_ref, w_ref, ws_ref, o_ref, os_ref, h_ref):
    @pl.when(pl.program_id(1) == 0)
    def _():
        hb = _norm_mod(x_ref[...], nw_ref[...], sc_ref[...], sh_ref[...]).astype(BF16)
        h_ref[...] = hb
        os_ref[...] = jnp.dot(hb, ws_ref[...], preferred_element_type=F32)

    o_ref[...] = jnp.dot(h_ref[...], w_ref[...], preferred_element_type=F32)


def _inproj(x, nw, mod, w_main, w_small):
    tm, tn = PROJ_TM, 1024
    return pl.pallas_call(
        _inproj_body,
        out_shape=(jax.ShapeDtypeStruct((N_TOK, PROJ_MAIN), F32),
                   jax.ShapeDtypeStruct((N_TOK, PROJ_SMALL), F32)),
        grid=(N_TOK // tm, PROJ_MAIN // tn),
        in_specs=[
            pl.BlockSpec((tm, D_MODEL), lambda i, j: (i, 0)),
            pl.BlockSpec((1, D_MODEL), lambda i, j: (0, 0)),
            pl.BlockSpec((None, None, 1, D_MODEL), lambda i, j: (_seq_row(i, tm), 1, 0, 0)),
            pl.BlockSpec((None, None, 1, D_MODEL), lambda i, j: (_seq_row(i, tm), 0, 0, 0)),
            pl.BlockSpec((D_MODEL, tn), lambda i, j: (0, j)),
            pl.BlockSpec((D_MODEL, PROJ_SMALL), lambda i, j: (0, 0)),
        ],
        out_specs=(pl.BlockSpec((tm, tn), lambda i, j: (i, j)),
                   pl.BlockSpec((tm, PROJ_SMALL), lambda i, j: (i, 0))),
        scratch_shapes=[pltpu.VMEM((tm, D_MODEL), BF16)],
        compiler_params=pltpu.CompilerParams(
            dimension_semantics=("parallel", "arbitrary"), vmem_limit_bytes=VMEM_LIMIT),
        name="in_proj",
    )(x, nw, mod, mod, w_main, w_small)


def _outproj_body(*refs):
    n_mix = 4
    m_refs, (w_ref, x_ref, g_ref, o_ref) = refs[:2 * n_mix], refs[2 * n_mix:]

    def compute(stream):
        acc = None
        for p in range(n_mix):
            part = jnp.dot(m_refs[2 * p + stream][...], w_ref[p * GROUP_WIDTH:(p + 1) * GROUP_WIDTH, :],
                           preferred_element_type=F32)
            acc = part if acc is None else acc + part
        o_ref[...] = x_ref[...] + g_ref[...] * acc

    _on_stream(pl.program_id(0), PROJ_TM, compute)


def _outproj(parts, w, x, mod):
    tm, tn = PROJ_TM, 1024
    in_specs = []
    for _ in parts:
        in_specs += list(_stream_specs(tm, GROUP_WIDTH))
    in_specs += [pl.BlockSpec((D_MODEL, tn), lambda i, j: (0, j)),
                 pl.BlockSpec((tm, tn), lambda i, j: (i, j)),
                 pl.BlockSpec((None, None, 1, tn), lambda i, j: (_seq_row(i, tm), 2, 0, j))]
    return pl.pallas_call(
        _outproj_body,
        out_shape=jax.ShapeDtypeStruct((N_TOK, D_MODEL), F32),
        grid=(N_TOK // tm, D_MODEL // tn),
        in_specs=in_specs,
        out_specs=pl.BlockSpec((tm, tn), lambda i, j: (i, j)),
        compiler_params=pltpu.CompilerParams(
            dimension_semantics=("parallel", "arbitrary"), vmem_limit_bytes=VMEM_LIMIT),
        name="out_proj",
    )(*(a for pair in parts for a in pair), w, x, mod)


FFN_TM = 512


def _ffn_body(x_ref, nw_ref, sc_ref, sh_ref, g_ref, w1_ref, w3_ref, w2_ref, fw_ref, *rest, final_norm):
    o_refs, (h_ref, acc_ref) = rest[:-2], rest[-2:]
    i, k = pl.program_id(0), pl.program_id(1)

    @pl.when(k == 0)
    def _():
        h_ref[...] = _norm_mod(x_ref[...], nw_ref[...], sc_ref[...], sh_ref[...]).astype(BF16)
        acc_ref[...] = jnp.zeros_like(acc_ref)

    h = h_ref[...]
    a = jnp.dot(h, w1_ref[...], preferred_element_type=F32)
    b = jnp.dot(h, w3_ref[...], preferred_element_type=F32)
    g = (a * jax.nn.sigmoid(a) * b).astype(BF16)
    acc_ref[...] += jnp.dot(g, w2_ref[...], preferred_element_type=F32)

    @pl.when(k == pl.num_programs(1) - 1)
    def _():
        y = x_ref[...] + g_ref[...] * acc_ref[...]
        if not final_norm:
            o_refs[0][...] = y
        else:
            ms = jnp.mean(y * y, axis=-1, keepdims=True)
            y = y * lax.rsqrt(ms + EPS) * fw_ref[...]

            def write(stream):
                o_refs[stream][...] = y

            _on_stream(i, FFN_TM, write)


def _ffn(x, nw, mod, w1, w3, w2, fw, final_norm):
    tm, th = FFN_TM, 512
    if final_norm:
        out_shape = (jax.ShapeDtypeStruct((N_CTX_TOK, D_MODEL), F32), jax.ShapeDtypeStruct((N_LAT_TOK, D_MODEL), F32))
        out_specs = _stream_specs(tm, D_MODEL)
    else:
        out_shape = jax.ShapeDtypeStruct((N_TOK, D_MODEL), F32)
        out_specs = pl.BlockSpec((tm, D_MODEL), lambda i, k: (i, 0))
    return pl.pallas_call(
        functools.partial(_ffn_body, final_norm=final_norm),
        out_shape=out_shape,
        grid=(N_TOK // tm, FFN_HIDDEN // th),
        in_specs=[
            pl.BlockSpec((tm, D_MODEL), lambda i, k: (i, 0)),
            pl.BlockSpec((1, D_MODEL), lambda i, k: (0, 0)),
            pl.BlockSpec((None, None, 1, D_MODEL), lambda i, k: (_seq_row(i, tm), 4, 0, 0)),
            pl.BlockSpec((None, None, 1, D_MODEL), lambda i, k: (_seq_row(i, tm), 3, 0, 0)),
            pl.BlockSpec((None, None, 1, D_MODEL), lambda i, k: (_seq_row(i, tm), 5, 0, 0)),
            pl.BlockSpec((D_MODEL, th), lambda i, k: (0, k)),
            pl.BlockSpec((D_MODEL, th), lambda i, k: (0, k)),
            pl.BlockSpec((th, D_MODEL), lambda i, k: (k, 0)),
            pl.BlockSpec((1, D_MODEL), lambda i, k: (0, 0)),
        ],
        out_specs=out_specs,
        scratch_shapes=[pltpu.VMEM((tm, D_MODEL), BF16), pltpu.VMEM((tm, D_MODEL), F32)],
        compiler_params=pltpu.CompilerParams(
            dimension_semantics=("arbitrary", "arbitrary"), vmem_limit_bytes=VMEM_LIMIT),
        name="ffn",
    )(x, nw, mod, mod, mod, w1, w3, w2, fw)


def _head_layernorm(x):
    mu = jnp.mean(x, axis=-1, keepdims=True)
    xc = x - mu
    return xc * lax.rsqrt(jnp.mean(xc * xc, axis=-1, keepdims=True) + EPS)


def _rmsnorm(x, w):
    y = x * lax.rsqrt(jnp.mean(x * x, axis=-1, keepdims=True) + EPS)
    return y * w


def _l2norm(x):
    return x * lax.rsqrt(jnp.sum(x * x, axis=-1, keepdims=True) + EPS)


def _heads(t, n_heads):
    b, l, _ = t.shape
    return t.reshape(b, l, n_heads, -1).transpose(0, 2, 1, 3)


def _merge(t):
    b, h, l, d = t.shape
    return t.transpose(0, 2, 1, 3).reshape(b, l, h * d)


def _chunks(t, size):
    return t.reshape(t.shape[:2] + (t.shape[2] // size, size) + t.shape[3:])


def _rev(t):
    return jnp.flip(t, axis=2)


def _axial_rope(l):
    n_rows = l // GRID_W
    t_row = jnp.repeat(jnp.arange(n_rows, dtype=F32), GRID_W)
    t_col = jnp.tile(jnp.arange(GRID_W, dtype=F32), n_rows)
    n_freq = HEAD_DIM // 4
    inv = ROPE_BASE ** (-jnp.arange(n_freq, dtype=F32) / n_freq)
    ang = jnp.concatenate([t_row[:, None] * inv, t_col[:, None] * inv], axis=-1)
    return jnp.cos(ang), jnp.sin(ang)


def _apply_rope(x, cos, sin):
    x1, x2 = jnp.split(x, 2, axis=-1)
    return jnp.concatenate([x1 * cos - x2 * sin, x1 * sin + x2 * cos], axis=-1)


def _dwconv(x, w):
    ch = x.shape[-1]
    pad = (GDN_CONV - 1) // 2
    return lax.conv_general_dilated(x, w[:, None, :], window_strides=(1,), padding=[(pad, pad)],
                                    dimension_numbers=('NWC', 'WIO', 'NWC'), feature_group_count=ch)


def _retention(q, k, v, log_gamma, s0):
    b, h, l, d = q.shape
    qc, kc, vc = _chunks(q, CHUNK), _chunks(k, CHUNK), _chunks(v, CHUNK)
    pos = jnp.arange(CHUNK, dtype=F32)
    rel = pos[:, None] - pos[None, :]
    intra = jnp.where(rel >= 0, jnp.exp(jnp.maximum(rel, 0.0) * log_gamma[:, None, None]), 0.0)
    q_dec = jnp.exp((pos + 1.0) * log_gamma[:, None])
    k_dec = jnp.exp((CHUNK - 1.0 - pos) * log_gamma[:, None])
    c_dec = jnp.exp(CHUNK * log_gamma)[None, :, None, None]
    scores = jnp.einsum('bhnid,bhnjd->bhnij', qc, kc) * intra[None, :, None]
    o_intra = jnp.einsum('bhnij,bhnje->bhnie', scores, vc)
    kv = jnp.einsum('bhnjd,hj,bhnje->bhnde', kc, k_dec, vc)

    def step(s, kv_n):
        return c_dec * s + kv_n, s

    s_fin, s_prev = lax.scan(step, s0, jnp.moveaxis(kv, 2, 0))
    o_inter = jnp.einsum('bhnid,hi,bhnde->bhnie', qc, q_dec, jnp.moveaxis(s_prev, 0, 2))
    return (o_intra + o_inter).reshape(b, h, l, d), s_fin


def _gated_delta(q, k, v, log_alpha, beta, s0):
    b, h, l, _ = q.shape
    qc, kc, vc = _chunks(q, CHUNK), _chunks(k, CHUNK), _chunks(v, CHUNK)
    g = jnp.cumsum(_chunks(log_alpha, CHUNK), axis=-1)
    bc = _chunks(beta, CHUNK)
    idx = jnp.arange(CHUNK)
    incl = idx[:, None] >= idx[None, :]
    strict = idx[:, None] > idx[None, :]
    decay = jnp.exp(jnp.where(incl, g[..., :, None] - g[..., None, :], -jnp.inf))
    kb = kc * bc[..., None]
    a_mat = jnp.where(strict, jnp.einsum('bhnid,bhnjd->bhnij', kb, kc) * decay, 0.0)
    eye = jnp.eye(CHUNK, dtype=F32)
    t_mat = lax.linalg.triangular_solve(a_mat + eye, jnp.broadcast_to(eye, a_mat.shape),
                                        left_side=True, lower=True, unit_diagonal=True)
    u = jnp.einsum('bhnij,bhnje->bhnie', t_mat, vc * bc[..., None])
    w = jnp.einsum('bhnij,bhnjd->bhnid', t_mat, kb * jnp.exp(g)[..., None])
    attn = jnp.einsum('bhnid,bhnjd->bhnij', qc, kc) * decay
    qg = qc * jnp.exp(g)[..., None]
    kt = kc * jnp.exp(g[..., -1:] - g)[..., None]
    cd = jnp.exp(g[..., -1])

    def step(s, xs):
        u_n, w_n, qg_n, at_n, kt_n, cd_n = xs
        v_new = u_n - jnp.einsum('bhcd,bhde->bhce', w_n, s)
        o = jnp.einsum('bhcd,bhde->bhce', qg_n, s) + jnp.einsum('bhcs,bhse->bhce', at_n, v_new)
        s = s * cd_n[..., None, None] + jnp.einsum('bhcd,bhce->bhde', kt_n, v_new)
        return s, o

    xs = (jnp.moveaxis(u, 2, 0), jnp.moveaxis(w, 2, 0), jnp.moveaxis(qg, 2, 0),
          jnp.moveaxis(attn, 2, 0), jnp.moveaxis(kt, 2, 0), jnp.moveaxis(cd, 2, 0))
    s_fin, o = lax.scan(step, s0, xs)
    return jnp.moveaxis(o, 0, 2).reshape(b, h, l, -1), s_fin


def _gla(q, k, v, log_f, s0):
    b, h, l, _ = q.shape
    qc, kc, vc, fc = (_chunks(q, HG_CHUNK), _chunks(k, HG_CHUNK), _chunks(v, HG_CHUNK), _chunks(log_f, HG_CHUNK))
    cum = jnp.cumsum(fc, axis=-2)
    idx = jnp.arange(HG_CHUNK)
    incl = (idx[:, None] >= idx[None, :])[:, :, None]
    dec = jnp.exp(jnp.where(incl, cum[..., :, None, :] - cum[..., None, :, :], -jnp.inf))
    attn = jnp.einsum('bhnid,bhnjd,bhnijd->bhnij', qc, kc, dec)
    o_intra = jnp.einsum('bhnij,bhnje->bhnie', attn, vc)
    q_in = qc * jnp.exp(cum)
    k_tail = kc * jnp.exp(cum[..., -1:, :] - cum)
    cd = jnp.exp(cum[..., -1, :])
    kv = jnp.einsum('bhnjd,bhnje->bhnde', k_tail, vc)

    def step(s, xs):
        kv_n, cd_n = xs
        return cd_n[..., None] * s + kv_n, s

    s_fin, s_prev = lax.scan(step, s0, (jnp.moveaxis(kv, 2, 0), jnp.moveaxis(cd, 2, 0)))
    o_inter = jnp.einsum('bhnid,bhnde->bhnie', q_in, jnp.moveaxis(s_prev, 0, 2))
    return (o_intra + o_inter).reshape(b, h, l, -1), s_fin


def _cplx_combine(e1, e2):
    a1r, a1i, b1r, b1i = e1
    a2r, a2i, b2r, b2i = e2
    return (a1r * a2r - a1i * a2i, a1r * a2i + a1i * a2r,
            a2r * b1r - a2i * b1i + b2r, a2r * b1i + a2i * b1r + b2i)


def _s5(u, a_re, a_im, log_step, b_re, b_im, c_re, c_im, x0_re, x0_im):
    dt = jnp.exp(log_step)[:, None]
    mag = jnp.exp(a_re * dt)
    ab_re = mag * jnp.cos(a_im * dt)
    ab_im = mag * jnp.sin(a_im * dt)
    den = a_re * a_re + a_im * a_im
    nr = ab_re - 1.0
    f_re = (nr * a_re + ab_im * a_im) / den
    f_im = (ab_im * a_re - nr * a_im) / den
    bb_re = f_re[..., None] * b_re - f_im[..., None] * b_im
    bb_im = f_re[..., None] * b_im + f_im[..., None] * b_re
    bu_re = jnp.einsum('blgc,gnc->blgn', u, bb_re)
    bu_im = jnp.einsum('blgc,gnc->blgn', u, bb_im)
    bu_re = bu_re.at[:, 0].add(ab_re * x0_re - ab_im * x0_im)
    bu_im = bu_im.at[:, 0].add(ab_re * x0_im + ab_im * x0_re)
    a_full_re = jnp.broadcast_to(ab_re, bu_re.shape)
    a_full_im = jnp.broadcast_to(ab_im, bu_im.shape)
    _, _, x_re, x_im = lax.associative_scan(_cplx_combine, (a_full_re, a_full_im, bu_re, bu_im), axis=1)
    y = jnp.einsum('blgn,gcn->blgc', x_re, c_re) - jnp.einsum('blgn,gcn->blgc', x_im, c_im)
    return y, x_re[:, -1], x_im[:, -1]


def _mixer_jax(pm, ps, lp, lower_bound, s_ret, s_gdn, s_hg, s_s5_re, s_s5_im, rope):
    bsz, l, _ = pm.shape
    w = GROUP_WIDTH
    sl = lambda i, n=1: pm[..., i * w:(i + n) * w]
    r_q, r_k, r_v, r_g, d_q, d_k, d_v, d_g = (sl(i) for i in range(8))
    h_q, h_f, h_i, h_g, s_u = sl(8), sl(9, 2), sl(11), sl(12), sl(13)
    d_a, d_b = ps[..., :8], ps[..., 8:16]

    q = _heads(r_q, N_HEADS)
    k = _heads(r_k, N_HEADS)
    v = _heads(r_v, N_HEADS)
    if rope is not None:
        q = _apply_rope(q, *rope)
        k = _apply_rope(k, *rope)
    k = k * HEAD_DIM ** -0.5
    log_gamma = jax.nn.log_sigmoid(lp['ret_decay_logit'])
    o_f, rs_f = _retention(q, k, v, log_gamma[0], s_ret[:, 0])
    o_b, rs_b = _retention(_rev(q), _rev(k), _rev(v), log_gamma[1], s_ret[:, 1])
    ret_out = _merge(_head_layernorm(o_f + _rev(o_b))) * jax.nn.silu(r_g)

    qkv = jax.nn.silu(_dwconv(jnp.concatenate([d_q, d_k, d_v], axis=-1), lp['gdn_conv']))
    gq, gk, gv = jnp.split(qkv, 3, axis=-1)
    q = _l2norm(_heads(gq, N_HEADS)) * HEAD_DIM ** -0.5
    k = _l2norm(_heads(gk, N_HEADS))
    v = _heads(gv, N_HEADS)
    a = d_a.reshape(bsz, l, 2, N_HEADS)
    bt = d_b.reshape(bsz, l, 2, N_HEADS)
    log_alpha = (-jnp.exp(lp['gdn_a_log']) * jax.nn.softplus(a + lp['gdn_dt_bias'])).transpose(2, 0, 3, 1)
    beta = jax.nn.sigmoid(bt).transpose(2, 0, 3, 1)
    o_f, gs_f = _gated_delta(q, k, v, log_alpha[0], beta[0], s_gdn[:, 0])
    o_b, gs_b = _gated_delta(_rev(q), _rev(k), _rev(v), _rev(log_alpha[1]), _rev(beta[1]), s_gdn[:, 1])
    gdn_out = _merge(_rmsnorm(o_f + _rev(o_b), lp['gdn_norm_w'])) * jax.nn.silu(d_g)

    q = _heads(jax.nn.silu(h_q), N_HEADS) * HEAD_DIM ** -0.5
    v = _heads(h_i, N_HEADS)
    fx = h_f.reshape(bsz, l, 2, w)
    log_f = jnp.logaddexp(jnp.log(jnp.maximum(lower_bound, LB_FLOOR)),
                          jnp.log1p(-lower_bound) + jax.nn.log_sigmoid(fx))
    key_in = (1.0 - lower_bound) * jax.nn.sigmoid(-fx)
    o_f, hs_f = _gla(q, _heads(key_in[:, :, 0], N_HEADS), v, _heads(log_f[:, :, 0], N_HEADS), s_hg[:, 0])
    o_b, hs_b = _gla(_rev(q), _rev(_heads(key_in[:, :, 1], N_HEADS)), _rev(v),
                     _rev(_heads(log_f[:, :, 1], N_HEADS)), s_hg[:, 1])
    hg_out = _merge(_rmsnorm(o_f + _rev(o_b), lp['hg_norm_w'])) * jax.nn.silu(h_g)

    u = s_u.reshape(bsz, l, S5_GROUPS, S5_CH)
    p = [lp[nm] for nm in ('s5_a_re', 's5_a_im', 's5_log_step', 's5_b_re', 's5_b_im', 's5_c_re', 's5_c_im')]
    y_f, xr_f, xi_f = _s5(u, p[0][0], p[1][0], p[2][0], p[3][0], p[4][0], p[5][0], p[6][0],
                          s_s5_re[:, 0], s_s5_im[:, 0])
    y_b, xr_b, xi_b = _s5(jnp.flip(u, 1), p[0][1], p[1][1], p[2][1], p[3][1], p[4][1], p[5][1], p[6][1],
                          s_s5_re[:, 1], s_s5_im[:, 1])
    y = (y_f + jnp.flip(y_b, 1)).reshape(bsz, l, w) + lp['s5_d'] * s_u
    z = jax.nn.gelu(y)
    s5_out = z * jax.nn.sigmoid(z @ lp['s5_glu_w'] + lp['s5_glu_b'])

    mixed = jnp.concatenate([ret_out, gdn_out, hg_out, s5_out], axis=-1).astype(BF16)
    new_states = (jnp.stack([rs_f, rs_b], axis=1), jnp.stack([gs_f, gs_b], axis=1),
                  jnp.stack([hs_f, hs_b], axis=1), jnp.stack([xr_f, xr_b], axis=1),
                  jnp.stack([xi_f, xi_b], axis=1))
    return mixed, new_states


MIX_CHUNK = 128
DIRS = 2
NT_DIMS = (((1,), (1,)), ((), ()))
TN_DIMS = (((0,), (0,)), ((), ()))


def _dot(a, b):
    return jnp.dot(a.astype(BF16), b.astype(BF16), preferred_element_type=F32)


def _dot_nt(a, b):
    return lax.dot_general(a.astype(BF16), b.astype(BF16), NT_DIMS, preferred_element_type=F32)


def _dot_tn(a, b):
    return lax.dot_general(a.astype(BF16), b.astype(BF16), TN_DIMS, preferred_element_type=F32)


def _silu(x):
    return x * jax.nn.sigmoid(x)


def _seq_spec(seq_len, width, row_off, col_blk):
    return pl.BlockSpec((seq_len, width), lambda b: (row_off + b, col_blk))


def _mix_params(n_par):
    return pltpu.CompilerParams(dimension_semantics=("parallel",) * n_par, vmem_limit_bytes=VMEM_LIMIT)


def _ret_body(*refs, n_chunks, use_rope, has_s0):
    it = iter(refs)
    lg_ref, q_ref, k_ref, v_ref, g_ref = (next(it) for _ in range(5))
    cos_ref, sin_ref = (next(it), next(it)) if use_rope else (None, None)
    s0_ref = next(it) if has_s0 else None
    o_ref, sf_ref, acc_ref, s_ref, intra_ref, qd_ref, kd_ref, cd_ref = (next(it) for _ in range(8))
    C, H = MIX_CHUNK, N_HEADS

    row = lax.broadcasted_iota(jnp.int32, (C, C), 0)
    col = lax.broadcasted_iota(jnp.int32, (C, C), 1)
    rel = (row - col).astype(F32)
    pos = lax.broadcasted_iota(jnp.int32, (C, HEAD_DIM), 0).astype(F32)
    for d in range(DIRS):
        for h in range(H):
            lg = lg_ref[d, h]
            if d == 0:
                intra_ref[d, h] = jnp.where(rel >= 0, jnp.exp(jnp.maximum(rel, 0.0) * lg), 0.0)
                qd_ref[d, h] = jnp.exp((pos + 1.0) * lg)
                kd_ref[d, h] = jnp.exp((C - 1.0 - pos) * lg)
            else:
                intra_ref[d, h] = jnp.where(rel <= 0, jnp.exp(jnp.maximum(-rel, 0.0) * lg), 0.0)
                qd_ref[d, h] = jnp.exp((C - pos) * lg)
                kd_ref[d, h] = jnp.exp(pos * lg)
            cd_ref[d, h] = jnp.exp(jnp.full((C, HEAD_DIM), C, F32) * lg)
    if has_s0:
        s_ref[...] = s0_ref[...]
    else:
        s_ref[...] = jnp.zeros_like(s_ref)
    acc_ref[...] = jnp.zeros_like(acc_ref)

    units = [(d, h) for d in range(DIRS) for h in range(H)]

    def chunk_step(n, carry):
        rows_d = [pl.ds(pl.multiple_of(c * C, C), C) for c in (n, n_chunks - 1 - n)]
        q_l, k_l, v_l, sc_l = [], [], [], []
        for d, h in units:
            rows, cols = rows_d[d], slice(h * HEAD_DIM, (h + 1) * HEAD_DIM)
            q, k = q_ref[rows, cols], k_ref[rows, cols]
            if use_rope:
                cs, sn = cos_ref[rows, :], sin_ref[rows, :]
                q = q * cs + pltpu.roll(q, HEAD_DIM // 2, 1) * sn
                k = k * cs + pltpu.roll(k, HEAD_DIM // 2, 1) * sn
            k = k * HEAD_DIM ** -0.5
            q_l.append(q)
            k_l.append(k)
            v_l.append(v_ref[rows, cols].astype(BF16))
            sc_l.append(_dot_nt(q, k) * intra_ref[d, h])
        for i, (d, h) in enumerate(units):
            cols = slice(h * HEAD_DIM, (h + 1) * HEAD_DIM)
            s = s_ref[d, h]
            acc_ref[rows_d[d], cols] += _dot(jnp.concatenate([sc_l[i], q_l[i] * qd_ref[d, h]], axis=1),
                                             jnp.concatenate([v_l[i], s.astype(BF16)], axis=0))
            s_ref[d, h] = cd_ref[d, h] * s + _dot_tn(k_l[i] * kd_ref[d, h], v_l[i])
        return carry

    lax.fori_loop(0, n_chunks, chunk_step, 0)
    sf_ref[...] = s_ref[...]

    def finish(n, carry):
        rows = pl.ds(pl.multiple_of(n * C, C), C)
        for h in range(H):
            cols = slice(h * HEAD_DIM, (h + 1) * HEAD_DIM)
            o = acc_ref[rows, cols]
            mu = jnp.mean(o, axis=-1, keepdims=True)
            oc = o - mu
            y = oc * lax.rsqrt(jnp.mean(oc * oc, axis=-1, keepdims=True) + EPS)
            o_ref[rows, cols] = (y * _silu(g_ref[rows, cols])).astype(BF16)
        return carry

    lax.fori_loop(0, n_chunks, finish, 0)


def _retention_pallas(pm, log_gamma, n_seq, seq_len, row_off, rope2, s0, layer):
    use_rope, has_s0 = rope2 is not None, s0 is not None
    st_shape = (DIRS, N_HEADS, HEAD_DIM, HEAD_DIM)
    in_specs = [pl.BlockSpec(memory_space=pltpu.SMEM)]
    in_specs += [_seq_spec(seq_len, GROUP_WIDTH, row_off, cb) for cb in range(4)]
    args = [log_gamma, pm, pm, pm, pm]
    if use_rope:
        in_specs += [pl.BlockSpec((seq_len, HEAD_DIM), lambda b: (0, 0))] * 2
        args += list(rope2)
    if has_s0:
        in_specs.append(pl.BlockSpec((None, None) + st_shape, lambda b: (b, layer, 0, 0, 0, 0)))
        args.append(s0)
    return pl.pallas_call(
        functools.partial(_ret_body, n_chunks=seq_len // MIX_CHUNK, use_rope=use_rope, has_s0=has_s0),
        out_shape=(jax.ShapeDtypeStruct((n_seq * seq_len, GROUP_WIDTH), BF16),
                   jax.ShapeDtypeStruct((n_seq,) + st_shape, F32)),
        grid=(n_seq,),
        in_specs=in_specs,
        out_specs=(pl.BlockSpec((seq_len, GROUP_WIDTH), lambda b: (b, 0)),
                   pl.BlockSpec((None,) + st_shape, lambda b: (b, 0, 0, 0, 0))),
        scratch_shapes=[pltpu.VMEM((seq_len, GROUP_WIDTH), F32), pltpu.VMEM(st_shape, F32),
                        pltpu.VMEM((DIRS, N_HEADS, MIX_CHUNK, MIX_CHUNK), F32),
                        pltpu.VMEM((DIRS, N_HEADS, MIX_CHUNK, HEAD_DIM), F32),
                        pltpu.VMEM((DIRS, N_HEADS, MIX_CHUNK, HEAD_DIM), F32),
                        pltpu.VMEM((DIRS, N_HEADS, MIX_CHUNK, HEAD_DIM), F32)],
        compiler_params=_mix_params(1),
        name="retention",
    )(*args)


def _rope_tables(l):
    cos, sin = _axial_rope(l)
    return jnp.concatenate([cos, cos], axis=-1), jnp.concatenate([-sin, sin], axis=-1)


S5_HALF_G = S5_GROUPS // 2
S5_HALF_U = S5_HALF_G * S5_CH
S5_HALF_X = S5_HALF_G * S5_N
S5_X = S5_GROUPS * S5_N
S5_TC = 256
S5_BLK = 8
S5_TABS = 4


def _s5_tables(a_re, a_im, log_step, b_re, b_im, c_re, c_im):
    dt = jnp.exp(log_step)[..., None]
    mag = jnp.exp(a_re * dt)
    ab_re = mag * jnp.cos(a_im * dt)
    ab_im = mag * jnp.sin(a_im * dt)
    den = a_re * a_re + a_im * a_im
    nr = ab_re - 1.0
    f_re = (nr * a_re + ab_im * a_im) / den
    f_im = (ab_im * a_re - nr * a_im) / den
    bb_re = f_re[..., None] * b_re - f_im[..., None] * b_im
    bb_im = f_re[..., None] * b_im + f_im[..., None] * b_re
    eye = jnp.eye(S5_HALF_G, dtype=F32)

    def in_mat(bb):
        bb = bb.reshape(DIRS, 2, S5_HALF_G, S5_N, S5_CH)
        return jnp.einsum('dhgnc,gk->dhgckn', bb, eye).reshape(DIRS, 2, S5_HALF_U, S5_HALF_X)

    def out_mat(cc):
        cc = cc.reshape(DIRS, 2, S5_HALF_G, S5_CH, S5_N)
        return jnp.einsum('dhgcn,gk->dhgnkc', cc, eye).reshape(DIRS, 2, S5_HALF_X, S5_HALF_U)

    bm = jnp.concatenate([in_mat(bb_re), in_mat(bb_im)], axis=-1).astype(BF16)
    cm = jnp.concatenate([out_mat(c_re), -out_mat(c_im)], axis=-2).astype(BF16)
    t = jnp.arange(S5_BLK, dtype=F32)
    order = jnp.stack([t, S5_BLK - 1.0 - t])
    shifts = 2.0 ** jnp.arange(S5_TABS - 1, dtype=F32)
    expo = jnp.concatenate([jnp.where(order[:, None, :] >= shifts[None, :, None], shifts[None, :, None], jnp.nan),
                            order[:, None, :] + 1.0], axis=1)
    live = ~jnp.isnan(expo)
    e = jnp.where(live, expo, 0.0)[..., None]
    adt_re = (a_re * dt).reshape(DIRS, 1, 1, S5_X)
    adt_im = (a_im * dt).reshape(DIRS, 1, 1, S5_X)
    pmag = jnp.where(live[..., None], jnp.exp(e * adt_re), 0.0)
    pw_re = pmag * jnp.cos(e * adt_im)
    pw_im = pmag * jnp.sin(e * adt_im)
    return bm, cm, pw_re, pw_im


def _gelu_tanh(x):
    return 0.5 * x * (1.0 + jnp.tanh(math.sqrt(2.0 / math.pi) * (x + 0.044715 * (x * x * x))))


def _s5_body(*refs, seq_len, has_s0):
    it = iter(refs)
    u_ref, bm_ref, cm_ref, pwr_ref, pwi_ref, d_ref, gw_ref, gb_ref = (next(it) for _ in range(8))
    x0r_ref, x0i_ref = (next(it), next(it)) if has_s0 else (None, None)
    o_ref, sfr_ref, sfi_ref, y_ref, xr_ref, xi_ref = (next(it) for _ in range(6))
    tc = min(seq_len, S5_TC)
    n_tiles = seq_len // tc
    n_blk = tc // S5_BLK

    y_ref[...] = d_ref[...] * u_ref[...]
    for d in range(DIRS):
        last = S5_BLK - 1 if d == 0 else 0
        for hf in range(2):
            xs = slice(hf * S5_HALF_X, (hf + 1) * S5_HALF_X)
            us = slice(hf * S5_HALF_U, (hf + 1) * S5_HALF_U)

            def scan_block(j, carry, d=d, xs=xs, last=last):
                car_re, car_im = carry
                blk = j if d == 0 else n_blk - 1 - j
                rows = pl.ds(pl.multiple_of(blk * S5_BLK, S5_BLK), S5_BLK)
                xr, xi = xr_ref[rows, :], xi_ref[rows, :]
                for k in range(S5_TABS - 1):
                    s = 1 << k
                    shift = s if d == 0 else S5_BLK - s
                    p_re, p_im = pwr_ref[d, k, :, xs], pwi_ref[d, k, :, xs]
                    sr, si = pltpu.roll(xr, shift, 0), pltpu.roll(xi, shift, 0)
                    xr, xi = xr + p_re * sr - p_im * si, xi + p_re * si + p_im * sr
                p_re, p_im = pwr_ref[d, S5_TABS - 1, :, xs], pwi_ref[d, S5_TABS - 1, :, xs]
                xr, xi = xr + p_re * car_re - p_im * car_im, xi + p_re * car_im + p_im * car_re
                xr_ref[rows, :] = xr
                xi_ref[rows, :] = xi
                return xr[last:last + 1, :], xi[last:last + 1, :]

            def scan_tile(i, carry, d=d, hf=hf, us=us, scan_block=scan_block):
                tile = i if d == 0 else n_tiles - 1 - i
                rows = pl.ds(pl.multiple_of(tile * tc, tc), tc)
                bu = _dot(u_ref[rows, us], bm_ref[d, hf])
                xr_ref[...] = bu[:, :S5_HALF_X]
                xi_ref[...] = bu[:, S5_HALF_X:]
                carry = lax.fori_loop(0, n_blk, scan_block, carry)
                y_ref[rows, us] += (_dot(xr_ref[...], cm_ref[d, hf, :S5_HALF_X, :])
                                    + _dot(xi_ref[...], cm_ref[d, hf, S5_HALF_X:, :]))
                return carry

            if has_s0:
                carry0 = (x0r_ref[d, :, xs], x0i_ref[d, :, xs])
            else:
                carry0 = (jnp.zeros((1, S5_HALF_X), F32), jnp.zeros((1, S5_HALF_X), F32))
            car_re, car_im = lax.fori_loop(0, n_tiles, scan_tile, carry0)
            sfr_ref[d, :, xs] = car_re
            sfi_ref[d, :, xs] = car_im

    z = _gelu_tanh(y_ref[...])
    o_ref[...] = (z * jax.nn.sigmoid(_dot(z, gw_ref[...]) + gb_ref[...])).astype(BF16)


def _s5_pallas(pm, tables, s5_d, glu_w, glu_b, n_seq, seq_len, row_off, x0, layer):
    bm, cm, pw_re, pw_im = tables
    has_s0 = x0 is not None
    full = lambda shape: pl.BlockSpec(shape, lambda b: (0,) * len(shape))
    in_specs = [_seq_spec(seq_len, GROUP_WIDTH, row_off, 13),
                full(bm.shape), full(cm.shape), full(pw_re.shape), full(pw_im.shape),
                full((1, GROUP_WIDTH)), full((GROUP_WIDTH, GROUP_WIDTH)), full((1, GROUP_WIDTH))]
    args = [pm, bm, cm, pw_re, pw_im, s5_d.reshape(1, GROUP_WIDTH), glu_w.astype(BF16),
            glu_b.reshape(1, GROUP_WIDTH)]
    if has_s0:
        in_specs += [pl.BlockSpec((None, None, DIRS, 1, S5_X), lambda b: (b, layer, 0, 0, 0))] * 2
        args += list(x0)
    st = jax.ShapeDtypeStruct((n_seq, DIRS, 1, S5_X), F32)
    st_spec = pl.BlockSpec((None, DIRS, 1, S5_X), lambda b: (b, 0, 0, 0))
    tc = min(seq_len, S5_TC)
    return pl.pallas_call(
        functools.partial(_s5_body, seq_len=seq_len, has_s0=has_s0),
        out_shape=(jax.ShapeDtypeStruct((n_seq * seq_len, GROUP_WIDTH), BF16), st, st),
        grid=(n_seq,),
        in_specs=in_specs,
        out_specs=(pl.BlockSpec((seq_len, GROUP_WIDTH), lambda b: (b, 0)), st_spec, st_spec),
        scratch_shapes=[pltpu.VMEM((seq_len, GROUP_WIDTH), F32),
                        pltpu.VMEM((tc, S5_HALF_X), F32), pltpu.VMEM((tc, S5_HALF_X), F32)],
        compiler_params=_mix_params(1),
        name="s5",
    )(*args)


GLA_LEVELS = 7


def _log_sigmoid(x):
    return jnp.minimum(x, 0.0) - jnp.log1p(jnp.exp(-jnp.abs(x)))


def _logaddexp(a, b):
    return jnp.maximum(a, b) + jnp.log1p(jnp.exp(-jnp.abs(a - b)))


def _chunk_cumsum_rows(x, rowi):
    s = 1
    while s < MIX_CHUNK:
        x = x + jnp.where(rowi >= s, pltpu.roll(x, s, 0), 0.0)
        s *= 2
    return x


def _gla_body(*refs, n_chunks, has_s0):
    it = iter(refs)
    q_ref, f0_ref, f1_ref, i_ref, g_ref, lb_ref, nw_ref = (next(it) for _ in range(7))
    s0_ref = next(it) if has_s0 else None
    o_ref, sf_ref, acc_ref, s_ref, code_ref = (next(it) for _ in range(5))
    C, H = MIX_CHUNK, N_HEADS
    f_refs = (f0_ref, f1_ref)

    rowi = lax.broadcasted_iota(jnp.int32, (C, HEAD_DIM), 0)
    ri = lax.broadcasted_iota(jnp.int32, (C, C), 0)
    ci = lax.broadcasted_iota(jnp.int32, (C, C), 1)
    top_bit = 31 - lax.clz(ri ^ ci)
    code_ref[...] = jnp.where(ri > ci, top_bit, jnp.where(ri < ci, -1 - top_bit, GLA_LEVELS))
    if has_s0:
        s_ref[...] = s0_ref[...]
    else:
        s_ref[...] = jnp.zeros_like(s_ref)
    acc_ref[...] = jnp.zeros_like(acc_ref)

    def chunk_step(n, carry):
        for d in range(DIRS):
            c = n if d == 0 else n_chunks - 1 - n
            rows = pl.ds(pl.multiple_of(c * C, C), C)
            for h in range(H):
                cols = slice(h * HEAD_DIM, (h + 1) * HEAD_DIM)
                code = code_ref[...]
                q = _silu(q_ref[rows, cols]) * HEAD_DIM ** -0.5
                v = i_ref[rows, cols].astype(BF16)
                fx = f_refs[d][rows, cols]
                lb = lb_ref[d:d + 1, cols]
                logf = _logaddexp(jnp.log(jnp.maximum(lb, LB_FLOOR)), jnp.log1p(-lb) + _log_sigmoid(fx))
                k = (1.0 - lb) * jax.nn.sigmoid(-fx)
                cum = _chunk_cumsum_rows(logf, rowi)
                own = cum
                attn = jnp.where(code == GLA_LEVELS, _dot_nt(q, k), 0.0)
                for lvl in range(GLA_LEVELS):
                    m = 1 << lvl
                    prev = pltpu.roll(own, m, 0)
                    pre = jnp.minimum(cum - prev, 0.0)
                    suf = jnp.minimum(own - cum, 0.0)
                    if d == 0:
                        sc = _dot_nt(q * jnp.exp(pre), k * jnp.exp(suf))
                        hit = code == lvl
                    else:
                        sc = _dot_nt(q * jnp.exp(jnp.minimum(suf + logf, 0.0)),
                                     k * jnp.exp(jnp.minimum(pre - logf, 0.0)))
                        hit = code == -1 - lvl
                    attn = jnp.where(hit, sc, attn)
                    own = jnp.where(((rowi >> lvl) & 1) == 0, pltpu.roll(own, C - m, 0), own)
                tot = own
                st = s_ref[d, h]
                if d == 0:
                    q_in, k_out = q * jnp.exp(cum), k * jnp.exp(tot - cum)
                else:
                    q_in, k_out = q * jnp.exp(tot - cum + logf), k * jnp.exp(cum - logf)
                acc_ref[rows, cols] += _dot(attn, v) + _dot_nt(q_in, st)
                s_ref[d, h] = jnp.exp(tot[0:1, :]) * st + _dot_tn(v, k_out)
        return carry

    lax.fori_loop(0, n_chunks, chunk_step, 0)
    sf_ref[...] = s_ref[...]

    def finish(n, carry):
        rows = pl.ds(pl.multiple_of(n * C, C), C)
        for h in range(H):
            cols = slice(h * HEAD_DIM, (h + 1) * HEAD_DIM)
            o = acc_ref[rows, cols]
            y = o * lax.rsqrt(jnp.mean(o * o, axis=-1, keepdims=True) + EPS) * nw_ref[...]
            o_ref[rows, cols] = (y * _silu(g_ref[rows, cols])).astype(BF16)
        return carry

    lax.fori_loop(0, n_chunks, finish, 0)


def _gla_pallas(pm, lower_bound, norm_w, n_seq, seq_len, row_off, s0_t, layer):
    has_s0 = s0_t is not None
    st_shape = (DIRS, N_HEADS, HEAD_DIM, HEAD_DIM)
    in_specs = [_seq_spec(seq_len, GROUP_WIDTH, row_off, cb) for cb in (8, 9, 10, 11, 12)]
    in_specs += [pl.BlockSpec((DIRS, GROUP_WIDTH), lambda b: (0, 0)), pl.BlockSpec((1, HEAD_DIM), lambda b: (0, 0))]
    args = [pm] * 5 + [lower_bound, norm_w.reshape(1, HEAD_DIM)]
    if has_s0:
        in_specs.append(pl.BlockSpec((None, None) + st_shape, lambda b: (b, layer, 0, 0, 0, 0)))
        args.append(s0_t)
    return pl.pallas_call(
        functools.partial(_gla_body, n_chunks=seq_len // MIX_CHUNK, has_s0=has_s0),
        out_shape=(jax.ShapeDtypeStruct((n_seq * seq_len, GROUP_WIDTH), BF16),
                   jax.ShapeDtypeStruct((n_seq,) + st_shape, F32)),
        grid=(n_seq,),
        in_specs=in_specs,
        out_specs=(pl.BlockSpec((seq_len, GROUP_WIDTH), lambda b: (b, 0)),
                   pl.BlockSpec((None,) + st_shape, lambda b: (b, 0, 0, 0, 0))),
        scratch_shapes=[pltpu.VMEM((seq_len, GROUP_WIDTH), F32), pltpu.VMEM(st_shape, F32),
                        pltpu.VMEM((MIX_CHUNK, MIX_CHUNK), jnp.int32)],
        compiler_params=_mix_params(1),
        name="hgrn2",
    )(*args)


GDN_BLOCK_BITS = 4
GDN_MERGES = 3
N_GATES = DIRS * N_HEADS


def _softplus(x):
    return jnp.maximum(x, 0.0) + jnp.log1p(jnp.exp(-jnp.abs(x)))


def _gdn_body(*refs, seq_len, has_s0):
    it = iter(refs)
    (q_ref, k_ref, v_ref, g_ref, ps_ref, pst_ref, cw_ref, prow_ref, pcol_ref, nw_ref) = (next(it) for _ in range(10))
    s0_ref = next(it) if has_s0 else None
    (o_ref, sf_ref, acc_ref, s_ref, qn_ref, kn_ref, vn_ref, gate_ref, gatet_ref, blev_ref) = (
        next(it) for _ in range(10))
    C, H, L = MIX_CHUNK, N_HEADS, seq_len
    n_chunks = L // C
    w = GROUP_WIDTH

    rowl = lax.broadcasted_iota(jnp.int32, (L, HEAD_DIM), 0)
    for part, (src, dst) in enumerate(((q_ref, qn_ref), (k_ref, kn_ref), (v_ref, vn_ref))):
        for h in range(H):
            cols = slice(h * HEAD_DIM, (h + 1) * HEAD_DIM)
            wc = slice(part * w + h * HEAD_DIM, part * w + (h + 1) * HEAD_DIM)
            x = src[:, cols]
            x_prev = jnp.where(rowl >= 1, pltpu.roll(x, 1, 0), 0.0)
            x_next = jnp.where(rowl < L - 1, pltpu.roll(x, L - 1, 0), 0.0)
            y = _silu(cw_ref[0:1, wc] * x_prev + cw_ref[1:2, wc] * x + cw_ref[2:3, wc] * x_next)
            if part < 2:
                y = y * lax.rsqrt(jnp.sum(y * y, axis=-1, keepdims=True) + EPS)
            if part == 0:
                y = y * HEAD_DIM ** -0.5
            dst[:, cols] = y

    rowi = lax.broadcasted_iota(jnp.int32, (C, HEAD_DIM), 0)
    lane = lax.broadcasted_iota(jnp.int32, (C, HEAD_DIM), 1)
    lane_t = lax.broadcasted_iota(jnp.int32, (2 * N_GATES, C), 1)
    row_t = lax.broadcasted_iota(jnp.int32, (2 * N_GATES, C), 0)
    for c in range(n_chunks):
        rows = slice(c * C, (c + 1) * C)
        a = ps_ref[rows, :]
        la = -jnp.exp(prow_ref[0:1, :]) * _softplus(a + prow_ref[1:2, :])
        pre = _chunk_cumsum_rows(la, rowi)
        tot = jnp.broadcast_to(pre[C - 1:C, :], (C, HEAD_DIM))
        g = jnp.where(lane < N_HEADS, pre, tot - pre + la)
        gate_ref[rows, :] = jnp.where(lane < N_GATES, g,
                                      jnp.where(lane < 2 * N_GATES, jax.nn.sigmoid(a),
                                                pltpu.roll(tot, 2 * N_GATES, 1)))
        at = pst_ref[:, rows]
        lat = -jnp.exp(pcol_ref[:, 0:1]) * _softplus(at + pcol_ref[:, 1:2])
        pre_t, s = lat, 1
        while s < C:
            pre_t = pre_t + jnp.where(lane_t >= s, pltpu.roll(pre_t, s, 1), 0.0)
            s *= 2
        tot_t = jnp.broadcast_to(pre_t[:, C - 1:C], (2 * N_GATES, C))
        g_t = jnp.where(row_t < N_HEADS, pre_t, tot_t - pre_t + lat)
        gatet_ref[:, rows] = jnp.where(row_t < N_GATES, g_t, jax.nn.sigmoid(at))

    ri = lax.broadcasted_iota(jnp.int32, (C, C), 0)
    ci = lax.broadcasted_iota(jnp.int32, (C, C), 1)
    bx = (ri >> GDN_BLOCK_BITS) ^ (ci >> GDN_BLOCK_BITS)
    blev_ref[...] = jnp.where(bx == 0, 0, 32 - lax.clz(bx))
    if has_s0:
        s_ref[...] = s0_ref[...]
    else:
        s_ref[...] = jnp.zeros_like(s_ref)
    acc_ref[...] = jnp.zeros_like(acc_ref)

    units = [(d, h) for d in range(DIRS) for h in range(H)]

    def chunk_step(n, carry):
        eye = jnp.where(ri == ci, 1.0, 0.0)
        blev = blev_ref[...]
        rows_d = [pl.ds(pl.multiple_of(c * C, C), C) for c in (n, n_chunks - 1 - n)]
        incl_d = [ri >= ci, ri <= ci]
        strict_d = [ri > ci, ri < ci]
        q_l, k_l, eg_l, et_l, ek_l, rhs_l, a_l, attn_l = ([] for _ in range(8))
        for d, h in units:
            rows, cols, cg = rows_d[d], slice(h * HEAD_DIM, (h + 1) * HEAD_DIM), d * N_HEADS + h
            q, k, v = qn_ref[rows, cols], kn_ref[rows, cols], vn_ref[rows, cols]
            g_i = jnp.broadcast_to(gate_ref[rows, cg:cg + 1], (C, C))
            b_i = jnp.broadcast_to(gate_ref[rows, N_GATES + cg:N_GATES + cg + 1], (C, C))
            tot = jnp.broadcast_to(gate_ref[rows, 2 * N_GATES + cg:2 * N_GATES + cg + 1], (C, C))
            g_j = gatet_ref[cg:cg + 1, rows]
            decay = jnp.where(incl_d[d], jnp.exp(jnp.minimum(g_i - g_j, 0.0)), 0.0)
            e_g = jnp.exp(g_i)
            kb = k.astype(BF16)
            a_l.append(jnp.where(strict_d[d], _dot_nt(kb, kb) * b_i * decay, 0.0))
            attn_l.append(_dot_nt(q, kb) * decay)
            rhs_l.append(jnp.concatenate([v * b_i, k * (b_i * e_g)], axis=1).astype(BF16))
            q_l.append(q * e_g)
            ek_l.append(k * jnp.exp(tot - g_i))
            et_l.append(jnp.exp(tot))
        b_l = [-jnp.where(blev == 0, a, 0.0) for a in a_l]
        p_l = [eye + b for b in b_l]
        for _ in range(GDN_BLOCK_BITS - 1):
            b_l = [_dot(b, b) for b in b_l]
            p_l = [p + _dot(p, b) for p, b in zip(p_l, b_l)]
        for lvl in range(1, GDN_MERGES + 1):
            ep_l = [_dot(jnp.where(blev == lvl, a, 0.0), p) for a, p in zip(a_l, p_l)]
            p_l = [p - _dot(p, ep) for p, ep in zip(p_l, ep_l)]
        uw_l = [_dot(p, rhs) for p, rhs in zip(p_l, rhs_l)]
        s_l = [s_ref[d, h] for d, h in units]
        vn_l = [uw[:, :HEAD_DIM] - _dot(uw[:, HEAD_DIM:], s) for uw, s in zip(uw_l, s_l)]
        for i, (d, h) in enumerate(units):
            cols = slice(h * HEAD_DIM, (h + 1) * HEAD_DIM)
            acc_ref[rows_d[d], cols] += _dot(jnp.concatenate([q_l[i], attn_l[i]], axis=1),
                                             jnp.concatenate([s_l[i], vn_l[i]], axis=0))
            s_ref[d, h] = s_l[i] * et_l[i] + _dot_tn(ek_l[i], vn_l[i])
        return carry

    lax.fori_loop(0, n_chunks, chunk_step, 0)
    sf_ref[...] = s_ref[...]

    def finish(n, carry):
        rows = pl.ds(pl.multiple_of(n * C, C), C)
        for h in range(H):
            cols = slice(h * HEAD_DIM, (h + 1) * HEAD_DIM)
            o = acc_ref[rows, cols]
            y = o * lax.rsqrt(jnp.mean(o * o, axis=-1, keepdims=True) + EPS) * nw_ref[...]
            o_ref[rows, cols] = (y * _silu(g_ref[rows, cols])).astype(BF16)
        return carry

    lax.fori_loop(0, n_chunks, finish, 0)


def _gdn_pallas(pm, ps, ps_t, conv_w, a_log, dt_bias, norm_w, n_seq, seq_len, row_off, s0, layer):
    has_s0 = s0 is not None
    st_shape = (DIRS, N_HEADS, HEAD_DIM, HEAD_DIM)
    par = jnp.stack([a_log.reshape(N_GATES), dt_bias.reshape(N_GATES)])
    par_row = jnp.pad(par, ((0, 0), (0, HEAD_DIM - N_GATES)))
    par_col = jnp.pad(par.T, ((0, N_GATES), (0, 0)))
    full = lambda shape: pl.BlockSpec(shape, lambda b: (0,) * len(shape))
    in_specs = [_seq_spec(seq_len, GROUP_WIDTH, row_off, cb) for cb in (4, 5, 6, 7)]
    in_specs += [_seq_spec(seq_len, PROJ_SMALL, row_off, 0),
                 pl.BlockSpec((2 * N_GATES, seq_len), lambda b: (0, row_off + b)),
                 full((GDN_CONV, 3 * GROUP_WIDTH)), full((2, HEAD_DIM)), full((2 * N_GATES, 2)), full((1, HEAD_DIM))]
    args = [pm] * 4 + [ps, ps_t, conv_w, par_row, par_col, norm_w.reshape(1, HEAD_DIM)]
    if has_s0:
        in_specs.append(pl.BlockSpec((None, None) + st_shape, lambda b: (b, layer, 0, 0, 0, 0)))
        args.append(s0)
    seq_f32 = lambda width: pltpu.VMEM((seq_len, width), F32)
    return pl.pallas_call(
        functools.partial(_gdn_body, seq_len=seq_len, has_s0=has_s0),
        out_shape=(jax.ShapeDtypeStruct((n_seq * seq_len, GROUP_WIDTH), BF16),
                   jax.ShapeDtypeStruct((n_seq,) + st_shape, F32)),
        grid=(n_seq,),
        in_specs=in_specs,
        out_specs=(pl.BlockSpec((seq_len, GROUP_WIDTH), lambda b: (b, 0)),
                   pl.BlockSpec((None,) + st_shape, lambda b: (b, 0, 0, 0, 0))),
        scratch_shapes=[seq_f32(GROUP_WIDTH), pltpu.VMEM(st_shape, F32),
                        seq_f32(GROUP_WIDTH), seq_f32(GROUP_WIDTH), seq_f32(GROUP_WIDTH),
                        seq_f32(HEAD_DIM), pltpu.VMEM((2 * N_GATES, seq_len), F32),
                        pltpu.VMEM((MIX_CHUNK, MIX_CHUNK), jnp.int32)],
        compiler_params=_mix_params(1),
        name="gated_delta",
    )(*args)


def kernel(x_prompt, x_sample, state_ret, state_gdn, state_hgrn, state_s5_re, state_s5_im, c, c_ctx, norm1_w, norm2_w, final_norm_w, ada_w, ada_b, in_proj, out_proj, ret_decay_logit, gdn_conv, gdn_a_log, gdn_dt_bias, gdn_norm_w, hg_lb_param, hg_norm_w, s5_a_re, s5_a_im, s5_b_re, s5_b_im, s5_c_re, s5_c_im, s5_log_step, s5_d, s5_glu_w, s5_glu_b, ffn_w1, ffn_w3, ffn_w2):
    lb_soft = jax.nn.softmax(hg_lb_param, axis=0)
    lower_bounds = jnp.cumsum(lb_soft, axis=0) - lb_soft[0]
    rope2 = _rope_tables(DEC_SEQ)

    cvec = jnp.zeros((N_SEQ_ROWS, D_MODEL), F32).at[0].set(c_ctx).at[1:1 + DEC_BATCH].set(c)
    mod_all = _ada(cvec, ada_w, ada_b).reshape(DEPTH, N_SEQ_ROWS, N_MOD, 1, D_MODEL)

    w_main = jnp.concatenate([in_proj[:, :, :8 * GROUP_WIDTH], in_proj[:, :, 8 * GROUP_WIDTH + 16:]],
                             axis=-1).astype(BF16)
    w_small = jnp.pad(in_proj[:, :, 8 * GROUP_WIDTH:8 * GROUP_WIDTH + 16],
                      ((0, 0), (0, 0), (0, PROJ_SMALL - 16))).astype(BF16)
    w_out = out_proj.astype(BF16)
    w1, w3, w2 = ffn_w1.astype(BF16), ffn_w3.astype(BF16), ffn_w2.astype(BF16)

    x = jnp.concatenate([x_prompt.reshape(N_CTX_TOK, D_MODEL), x_sample.reshape(N_LAT_TOK, D_MODEL)], axis=0)
    hgrn_t = jnp.swapaxes(state_hgrn, -1, -2)
    s5_x0 = (state_s5_re.reshape(DEC_BATCH, DEPTH, DIRS, 1, S5_X), state_s5_im.reshape(DEC_BATCH, DEPTH, DIRS, 1, S5_X))
    lat_off = N_CTX_TOK // DEC_SEQ
    ctx_states = []
    for i in range(DEPTH):
        mod = mod_all[i]
        pm, ps = _inproj(x, norm1_w[i][None], mod, w_main[i], w_small[i])
        ps_t = ps[:, :2 * N_GATES].T
        lg = jax.nn.log_sigmoid(ret_decay_logit[i])
        tables = _s5_tables(s5_a_re[i], s5_a_im[i], s5_log_step[i], s5_b_re[i], s5_b_im[i], s5_c_re[i], s5_c_im[i])
        ctx = (BATCH, SEQ, 0)
        lat = (DEC_BATCH, DEC_SEQ, lat_off)

        ret_c, rs = _retention_pallas(pm, lg, *ctx, None, None, i)
        ret_l, _ = _retention_pallas(pm, lg, *lat, rope2, state_ret, i)
        gdn_args = (pm, ps, ps_t, gdn_conv[i], gdn_a_log[i], gdn_dt_bias[i], gdn_norm_w[i])
        gdn_c, gs = _gdn_pallas(*gdn_args, *ctx, None, i)
        gdn_l, _ = _gdn_pallas(*gdn_args, *lat, state_gdn, i)
        hg_c, hs = _gla_pallas(pm, lower_bounds[i], hg_norm_w[i], *ctx, None, i)
        hg_l, _ = _gla_pallas(pm, lower_bounds[i], hg_norm_w[i], *lat, hgrn_t, i)
        s5_args = (pm, tables, s5_d[i], s5_glu_w[i], s5_glu_b[i])
        s5_c, xr, xi = _s5_pallas(*s5_args, *ctx, None, i)
        s5_l, _, _ = _s5_pallas(*s5_args, *lat, s5_x0, i)
        ctx_states.append((rs, gs, jnp.swapaxes(hs, -1, -2), xr.reshape(BATCH, DIRS, S5_GROUPS, S5_N),
                           xi.reshape(BATCH, DIRS, S5_GROUPS, S5_N)))

        parts = ((ret_c, ret_l), (gdn_c, gdn_l), (hg_c, hg_l), (s5_c, s5_l))
        x = _outproj(parts, w_out[i], x, mod)
        x = _ffn(x, norm2_w[i][None], mod, w1[i], w3[i], w2[i], final_norm_w[None], i == DEPTH - 1)

    y_prompt = x[0].reshape(BATCH, SEQ, D_MODEL)
    y_sample = x[1].reshape(DEC_BATCH, DEC_SEQ, D_MODEL)
    new_states = tuple(jnp.stack([s[j] for s in ctx_states], axis=1) for j in range(5))
    return (y_prompt, y_sample) + new_states
```

```python
import functools
import math

import jax
import jax.numpy as jnp
from jax import lax
from jax.experimental import pallas as pl
from jax.experimental.pallas import tpu as pltpu

F32 = jnp.float32
BF16 = jnp.bfloat16

D_MODEL = 2048
BATCH = 16
SEQ = 256
DEPTH = 2
DEC_BATCH = 8
DEC_SEQ = 1024
GRID_W = 64
HEAD_DIM = 128
GROUP_WIDTH = 512
N_HEADS = 4
S5_CH = 16
S5_GROUPS = 32
S5_N = 64
GDN_CONV = 3
CHUNK = 64
HG_CHUNK = 16
ROPE_BASE = 10000.0
FFN_HIDDEN = 5632
N_MOD = 6
EPS = 1e-6
LB_FLOOR = 1e-30

N_CTX_TOK = BATCH * SEQ
N_LAT_TOK = DEC_BATCH * DEC_SEQ
N_TOK = N_CTX_TOK + N_LAT_TOK
N_SEQ_ROWS = 16
PROJ_MAIN = 14 * GROUP_WIDTH
PROJ_SMALL = 128
VMEM_LIMIT = 56 * 1024 * 1024


def _seq_row(tile, tm):
    n_ctx = N_CTX_TOK // tm
    per_lat = DEC_SEQ // tm
    return jnp.where(tile < n_ctx, 0, 1 + (tile - n_ctx) // per_lat)


def _stream_specs(tm, width, n_col=1):
    n_ctx = N_CTX_TOK // tm

    def ctx_map(i, j):
        return jnp.minimum(i, n_ctx - 1), (jnp.where(i < n_ctx, j, n_col - 1) if n_col > 1 else 0)

    def lat_map(i, j):
        return jnp.maximum(i - n_ctx, 0), (jnp.where(i >= n_ctx, j, 0) if n_col > 1 else 0)

    return pl.BlockSpec((tm, width), ctx_map), pl.BlockSpec((tm, width), lat_map)


def _on_stream(tile, tm, fn):
    n_ctx = N_CTX_TOK // tm
    pl.when(tile < n_ctx)(functools.partial(fn, 0))
    pl.when(tile >= n_ctx)(functools.partial(fn, 1))


def _ada_body(c_ref, w_ref, b_ref, o_ref):
    cv = c_ref[...]
    s = cv * jax.nn.sigmoid(cv)
    o_ref[0] = jnp.dot(s.astype(BF16), w_ref[0].astype(BF16), preferred_element_type=F32) + b_ref[0]


def _ada(cvec, ada_w, ada_b):
    tn = 1024
    n = N_MOD * D_MODEL
    return pl.pallas_call(
        _ada_body,
        out_shape=jax.ShapeDtypeStruct((DEPTH, N_SEQ_ROWS, n), F32),
        grid=(DEPTH, n // tn),
        in_specs=[
            pl.BlockSpec((N_SEQ_ROWS, D_MODEL), lambda l, j: (0, 0)),
            pl.BlockSpec((1, D_MODEL, tn), lambda l, j: (l, 0, j)),
            pl.BlockSpec((1, 1, tn), lambda l, j: (l, 0, j)),
        ],
        out_specs=pl.BlockSpec((1, N_SEQ_ROWS, tn), lambda l, j: (l, 0, j)),
        compiler_params=pltpu.CompilerParams(
            dimension_semantics=("parallel", "parallel"), vmem_limit_bytes=VMEM_LIMIT),
        name="ada_mod",
    )(cvec, ada_w, ada_b.reshape(DEPTH, 1, n))


def _norm_mod(x, nw, sc, sh):
    ms = jnp.mean(x * x, axis=-1, keepdims=True)
    y = x * lax.rsqrt(ms + EPS) * nw
    return y * (1.0 + sc) + sh


PROJ_TM = 1024


def _inproj_body(x_ref, nw_ref, sc_ref, sh_ref, w_ref, ws_ref, o_ref, os_ref, h_ref):
    @pl.when(pl.program_id(1) == 0)
    def _():
        hb = _norm_mod(x_ref[...], nw_ref[...], sc_ref[...], sh_ref[...]).astype(BF16)
        h_ref[...] = hb
        os_ref[...] = jnp.dot(hb, ws_ref[...], preferred_element_type=F32)

    o_ref[...] = jnp.dot(h_ref[...], w_ref[...], preferred_element_type=F32)


def _inproj(x, nw, mod, w_main, w_small, layer):
    tm, tn = PROJ_TM, 1024
    return pl.pallas_call(
        _inproj_body,
        out_shape=(jax.ShapeDtypeStruct((N_TOK, PROJ_MAIN), F32),
                   jax.ShapeDtypeStruct((N_TOK, PROJ_SMALL), F32)),
        grid=(N_TOK // tm, PROJ_MAIN // tn),
        in_specs=[
            pl.BlockSpec((tm, D_MODEL), lambda i, j: (i, 0)),
            pl.BlockSpec((1, D_MODEL), lambda i, j: (0, 0)),
            pl.BlockSpec((None, None, 1, D_MODEL), lambda i, j: (_seq_row(i, tm), 1, 0, 0)),
            pl.BlockSpec((None, None, 1, D_MODEL), lambda i, j: (_seq_row(i, tm), 0, 0, 0)),
            pl.BlockSpec((None, D_MODEL, tn), lambda i, j: (layer, 0, j)),
            pl.BlockSpec((None, D_MODEL, PROJ_SMALL), lambda i, j: (layer, 0, 0)),
        ],
        out_specs=(pl.BlockSpec((tm, tn), lambda i, j: (i, j)),
                   pl.BlockSpec((tm, PROJ_SMALL), lambda i, j: (i, 0))),
        scratch_shapes=[pltpu.VMEM((tm, D_MODEL), BF16)],
        compiler_params=pltpu.CompilerParams(
            dimension_semantics=("parallel", "arbitrary"), vmem_limit_bytes=VMEM_LIMIT),
        name="in_proj",
    )(x, nw, mod, mod, w_main, w_small)


def _outproj_body(*refs):
    n_mix = 4
    m_refs, (w_ref, x_ref, g_ref, o_ref) = refs[:2 * n_mix], refs[2 * n_mix:]

    def compute(stream):
        acc = None
        for p in range(n_mix):
            part = jnp.dot(m_refs[2 * p + stream][...], w_ref[p * GROUP_WIDTH:(p + 1) * GROUP_WIDTH, :],
                           preferred_element_type=F32)
            acc = part if acc is None else acc + part
        o_ref[...] = x_ref[...] + g_ref[...] * acc

    _on_stream(pl.program_id(0), PROJ_TM, compute)


def _outproj(parts, w, x, mod, layer):
    tm, tn = PROJ_TM, 1024
    in_specs = []
    for _ in parts:
        in_specs += list(_stream_specs(tm, GROUP_WIDTH))
    in_specs += [pl.BlockSpec((None, D_MODEL, tn), lambda i, j: (layer, 0, j)),
                 pl.BlockSpec((tm, tn), lambda i, j: (i, j)),
                 pl.BlockSpec((None, None, 1, tn), lambda i, j: (_seq_row(i, tm), 2, 0, j))]
    return pl.pallas_call(
        _outproj_body,
        out_shape=jax.ShapeDtypeStruct((N_TOK, D_MODEL), F32),
        grid=(N_TOK // tm, D_MODEL // tn),
        in_specs=in_specs,
        out_specs=pl.BlockSpec((tm, tn), lambda i, j: (i, j)),
        compiler_params=pltpu.CompilerParams(
            dimension_semantics=("parallel", "arbitrary"), vmem_limit_bytes=VMEM_LIMIT),
        name="out_proj",
    )(*(a for pair in parts for a in pair), w, x, mod)


FFN_TM = 512


def _ffn_body(x_ref, nw_ref, sc_ref, sh_ref, g_ref, w1_ref, w3_ref, w2_ref, fw_ref, *rest, final_norm):
    o_refs, (h_ref, acc_ref) = rest[:-2], rest[-2:]
    i, k = pl.program_id(0), pl.program_id(1)

    @pl.when(k == 0)
    def _():
        h_ref[...] = _norm_mod(x_ref[...], nw_ref[...], sc_ref[...], sh_ref[...]).astype(BF16)
        acc_ref[...] = jnp.zeros_like(acc_ref)

    h = h_ref[...]
    a = jnp.dot(h, w1_ref[...], preferred_element_type=F32)
    b = jnp.dot(h, w3_ref[...], preferred_element_type=F32)
    g = (a * jax.nn.sigmoid(a) * b).astype(BF16)
    acc_ref[...] += jnp.dot(g, w2_ref[...], preferred_element_type=F32)

    @pl.when(k == pl.num_programs(1) - 1)
    def _():
        y = x_ref[...] + g_ref[...] * acc_ref[...]
        if not final_norm:
            o_refs[0][...] = y
        else:
            ms = jnp.mean(y * y, axis=-1, keepdims=True)
            y = y * lax.rsqrt(ms + EPS) * fw_ref[...]

            def write(stream):
                o_refs[stream][...] = y

            _on_stream(i, FFN_TM, write)


def _ffn(x, nw, mod, w1, w3, w2, fw, layer):
    final_norm = layer == DEPTH - 1
    tm, th = FFN_TM, 512
    if final_norm:
        out_shape = (jax.ShapeDtypeStruct((N_CTX_TOK, D_MODEL), F32), jax.ShapeDtypeStruct((N_LAT_TOK, D_MODEL), F32))
        out_specs = _stream_specs(tm, D_MODEL)
    else:
        out_shape = jax.ShapeDtypeStruct((N_TOK, D_MODEL), F32)
        out_specs = pl.BlockSpec((tm, D_MODEL), lambda i, k: (i, 0))
    return pl.pallas_call(
        functools.partial(_ffn_body, final_norm=final_norm),
        out_shape=out_shape,
        grid=(N_TOK // tm, FFN_HIDDEN // th),
        in_specs=[
            pl.BlockSpec((tm, D_MODEL), lambda i, k: (i, 0)),
            pl.BlockSpec((1, D_MODEL), lambda i, k: (0, 0)),
            pl.BlockSpec((None, None, 1, D_MODEL), lambda i, k: (_seq_row(i, tm), 4, 0, 0)),
            pl.BlockSpec((None, None, 1, D_MODEL), lambda i, k: (_seq_row(i, tm), 3, 0, 0)),
            pl.BlockSpec((None, None, 1, D_MODEL), lambda i, k: (_seq_row(i, tm), 5, 0, 0)),
            pl.BlockSpec((None, D_MODEL, th), lambda i, k: (layer, 0, k)),
            pl.BlockSpec((None, D_MODEL, th), lambda i, k: (layer, 0, k)),
            pl.BlockSpec((None, th, D_MODEL), lambda i, k: (layer, k, 0)),
            pl.BlockSpec((1, D_MODEL), lambda i, k: (0, 0)),
        ],
        out_specs=out_specs,
        scratch_shapes=[pltpu.VMEM((tm, D_MODEL), BF16), pltpu.VMEM((tm, D_MODEL), F32)],
        compiler_params=pltpu.CompilerParams(
            dimension_semantics=("arbitrary", "arbitrary"), vmem_limit_bytes=VMEM_LIMIT),
        name="ffn",
    )(x, nw, mod, mod, mod, w1, w3, w2, fw)


def _head_layernorm(x):
    mu = jnp.mean(x, axis=-1, keepdims=True)
    xc = x - mu
    return xc * lax.rsqrt(jnp.mean(xc * xc, axis=-1, keepdims=True) + EPS)


def _rmsnorm(x, w):
    y = x * lax.rsqrt(jnp.mean(x * x, axis=-1, keepdims=True) + EPS)
    return y * w


def _l2norm(x):
    return x * lax.rsqrt(jnp.sum(x * x, axis=-1, keepdims=True) + EPS)


def _heads(t, n_heads):
    b, l, _ = t.shape
    return t.reshape(b, l, n_heads, -1).transpose(0, 2, 1, 3)


def _merge(t):
    b, h, l, d = t.shape
    return t.transpose(0, 2, 1, 3).reshape(b, l, h * d)


def _chunks(t, size):
    return t.reshape(t.shape[:2] + (t.shape[2] // size, size) + t.shape[3:])


def _rev(t):
    return jnp.flip(t, axis=2)


def _axial_rope(l):
    n_rows = l // GRID_W
    t_row = jnp.repeat(jnp.arange(n_rows, dtype=F32), GRID_W)
    t_col = jnp.tile(jnp.arange(GRID_W, dtype=F32), n_rows)
    n_freq = HEAD_DIM // 4
    inv = ROPE_BASE ** (-jnp.arange(n_freq, dtype=F32) / n_freq)
    ang = jnp.concatenate([t_row[:, None] * inv, t_col[:, None] * inv], axis=-1)
    return jnp.cos(ang), jnp.sin(ang)


def _apply_rope(x, cos, sin):
    x1, x2 = jnp.split(x, 2, axis=-1)
    return jnp.concatenate([x1 * cos - x2 * sin, x1 * sin + x2 * cos], axis=-1)


def _dwconv(x, w):
    ch = x.shape[-1]
    pad = (GDN_CONV - 1) // 2
    return lax.conv_general_dilated(x, w[:, None, :], window_strides=(1,), padding=[(pad, pad)],
                                    dimension_numbers=('NWC', 'WIO', 'NWC'), feature_group_count=ch)


def _retention(q, k, v, log_gamma, s0):
    b, h, l, d = q.shape
    qc, kc, vc = _chunks(q, CHUNK), _chunks(k, CHUNK), _chunks(v, CHUNK)
    pos = jnp.arange(CHUNK, dtype=F32)
    rel = pos[:, None] - pos[None, :]
    intra = jnp.where(rel >= 0, jnp.exp(jnp.maximum(rel, 0.0) * log_gamma[:, None, None]), 0.0)
    q_dec = jnp.exp((pos + 1.0) * log_gamma[:, None])
    k_dec = jnp.exp((CHUNK - 1.0 - pos) * log_gamma[:, None])
    c_dec = jnp.exp(CHUNK * log_gamma)[None, :, None, None]
    scores = jnp.einsum('bhnid,bhnjd->bhnij', qc, kc) * intra[None, :, None]
    o_intra = jnp.einsum('bhnij,bhnje->bhnie', scores, vc)
    kv = jnp.einsum('bhnjd,hj,bhnje->bhnde', kc, k_dec, vc)

    def step(s, kv_n):
        return c_dec * s + kv_n, s

    s_fin, s_prev = lax.scan(step, s0, jnp.moveaxis(kv, 2, 0))
    o_inter = jnp.einsum('bhnid,hi,bhnde->bhnie', qc, q_dec, jnp.moveaxis(s_prev, 0, 2))
    return (o_intra + o_inter).reshape(b, h, l, d), s_fin


def _gated_delta(q, k, v, log_alpha, beta, s0):
    b, h, l, _ = q.shape
    qc, kc, vc = _chunks(q, CHUNK), _chunks(k, CHUNK), _chunks(v, CHUNK)
    g = jnp.cumsum(_chunks(log_alpha, CHUNK), axis=-1)
    bc = _chunks(beta, CHUNK)
    idx = jnp.arange(CHUNK)
    incl = idx[:, None] >= idx[None, :]
    strict = idx[:, None] > idx[None, :]
    decay = jnp.exp(jnp.where(incl, g[..., :, None] - g[..., None, :], -jnp.inf))
    kb = kc * bc[..., None]
    a_mat = jnp.where(strict, jnp.einsum('bhnid,bhnjd->bhnij', kb, kc) * decay, 0.0)
    eye = jnp.eye(CHUNK, dtype=F32)
    t_mat = lax.linalg.triangular_solve(a_mat + eye, jnp.broadcast_to(eye, a_mat.shape),
                                        left_side=True, lower=True, unit_diagonal=True)
    u = jnp.einsum('bhnij,bhnje->bhnie', t_mat, vc * bc[..., None])
    w = jnp.einsum('bhnij,bhnjd->bhnid', t_mat, kb * jnp.exp(g)[..., None])
    attn = jnp.einsum('bhnid,bhnjd->bhnij', qc, kc) * decay
    qg = qc * jnp.exp(g)[..., None]
    kt = kc * jnp.exp(g[..., -1:] - g)[..., None]
    cd = jnp.exp(g[..., -1])

    def step(s, xs):
        u_n, w_n, qg_n, at_n, kt_n, cd_n = xs
        v_new = u_n - jnp.einsum('bhcd,bhde->bhce', w_n, s)
        o = jnp.einsum('bhcd,bhde->bhce', qg_n, s) + jnp.einsum('bhcs,bhse->bhce', at_n, v_new)
        s = s * cd_n[..., None, None] + jnp.einsum('bhcd,bhce->bhde', kt_n, v_new)
        return s, o

    xs = (jnp.moveaxis(u, 2, 0), jnp.moveaxis(w, 2, 0), jnp.moveaxis(qg, 2, 0),
          jnp.moveaxis(attn, 2, 0), jnp.moveaxis(kt, 2, 0), jnp.moveaxis(cd, 2, 0))
    s_fin, o = lax.scan(step, s0, xs)
    return jnp.moveaxis(o, 0, 2).reshape(b, h, l, -1), s_fin


def _gla(q, k, v, log_f, s0):
    b, h, l, _ = q.shape
    qc, kc, vc, fc = (_chunks(q, HG_CHUNK), _chunks(k, HG_CHUNK), _chunks(v, HG_CHUNK), _chunks(log_f, HG_CHUNK))
    cum = jnp.cumsum(fc, axis=-2)
    idx = jnp.arange(HG_CHUNK)
    incl = (idx[:, None] >= idx[None, :])[:, :, None]
    dec = jnp.exp(jnp.where(incl, cum[..., :, None, :] - cum[..., None, :, :], -jnp.inf))
    attn = jnp.einsum('bhnid,bhnjd,bhnijd->bhnij', qc, kc, dec)
    o_intra = jnp.einsum('bhnij,bhnje->bhnie', attn, vc)
    q_in = qc * jnp.exp(cum)
    k_tail = kc * jnp.exp(cum[..., -1:, :] - cum)
    cd = jnp.exp(cum[..., -1, :])
    kv = jnp.einsum('bhnjd,bhnje->bhnde', k_tail, vc)

    def step(s, xs):
        kv_n, cd_n = xs
        return cd_n[..., None] * s + kv_n, s

    s_fin, s_prev = lax.scan(step, s0, (jnp.moveaxis(kv, 2, 0), jnp.moveaxis(cd, 2, 0)))
    o_inter = jnp.einsum('bhnid,bhnde->bhnie', q_in, jnp.moveaxis(s_prev, 0, 2))
    return (o_intra + o_inter).reshape(b, h, l, -1), s_fin


def _cplx_combine(e1, e2):
    a1r, a1i, b1r, b1i = e1
    a2r, a2i, b2r, b2i = e2
    return (a1r * a2r - a1i * a2i, a1r * a2i + a1i * a2r,
            a2r * b1r - a2i * b1i + b2r, a2r * b1i + a2i * b1r + b2i)


def _s5(u, a_re, a_im, log_step, b_re, b_im, c_re, c_im, x0_re, x0_im):
    dt = jnp.exp(log_step)[:, None]
    mag = jnp.exp(a_re * dt)
    ab_re = mag * jnp.cos(a_im * dt)
    ab_im = mag * jnp.sin(a_im * dt)
    den = a_re * a_re + a_im * a_im
    nr = ab_re - 1.0
    f_re = (nr * a_re + ab_im * a_im) / den
    f_im = (ab_im * a_re - nr * a_im) / den
    bb_re = f_re[..., None] * b_re - f_im[..., None] * b_im
    bb_im = f_re[..., None] * b_im + f_im[..., None] * b_re
    bu_re = jnp.einsum('blgc,gnc->blgn', u, bb_re)
    bu_im = jnp.einsum('blgc,gnc->blgn', u, bb_im)
    bu_re = bu_re.at[:, 0].add(ab_re * x0_re - ab_im * x0_im)
    bu_im = bu_im.at[:, 0].add(ab_re * x0_im + ab_im * x0_re)
    a_full_re = jnp.broadcast_to(ab_re, bu_re.shape)
    a_full_im = jnp.broadcast_to(ab_im, bu_im.shape)
    _, _, x_re, x_im = lax.associative_scan(_cplx_combine, (a_full_re, a_full_im, bu_re, bu_im), axis=1)
    y = jnp.einsum('blgn,gcn->blgc', x_re, c_re) - jnp.einsum('blgn,gcn->blgc', x_im, c_im)
    return y, x_re[:, -1], x_im[:, -1]


def _mixer_jax(pm, ps, lp, lower_bound, s_ret, s_gdn, s_hg, s_s5_re, s_s5_im, rope):
    bsz, l, _ = pm.shape
    w = GROUP_WIDTH
    sl = lambda i, n=1: pm[..., i * w:(i + n) * w]
    r_q, r_k, r_v, r_g, d_q, d_k, d_v, d_g = (sl(i) for i in range(8))
    h_q, h_f, h_i, h_g, s_u = sl(8), sl(9, 2), sl(11), sl(12), sl(13)
    d_a, d_b = ps[..., :8], ps[..., 8:16]

    q = _heads(r_q, N_HEADS)
    k = _heads(r_k, N_HEADS)
    v = _heads(r_v, N_HEADS)
    if rope is not None:
        q = _apply_rope(q, *rope)
        k = _apply_rope(k, *rope)
    k = k * HEAD_DIM ** -0.5
    log_gamma = jax.nn.log_sigmoid(lp['ret_decay_logit'])
    o_f, rs_f = _retention(q, k, v, log_gamma[0], s_ret[:, 0])
    o_b, rs_b = _retention(_rev(q), _rev(k), _rev(v), log_gamma[1], s_ret[:, 1])
    ret_out = _merge(_head_layernorm(o_f + _rev(o_b))) * jax.nn.silu(r_g)

    qkv = jax.nn.silu(_dwconv(jnp.concatenate([d_q, d_k, d_v], axis=-1), lp['gdn_conv']))
    gq, gk, gv = jnp.split(qkv, 3, axis=-1)
    q = _l2norm(_heads(gq, N_HEADS)) * HEAD_DIM ** -0.5
    k = _l2norm(_heads(gk, N_HEADS))
    v = _heads(gv, N_HEADS)
    a = d_a.reshape(bsz, l, 2, N_HEADS)
    bt = d_b.reshape(bsz, l, 2, N_HEADS)
    log_alpha = (-jnp.exp(lp['gdn_a_log']) * jax.nn.softplus(a + lp['gdn_dt_bias'])).transpose(2, 0, 3, 1)
    beta = jax.nn.sigmoid(bt).transpose(2, 0, 3, 1)
    o_f, gs_f = _gated_delta(q, k, v, log_alpha[0], beta[0], s_gdn[:, 0])
    o_b, gs_b = _gated_delta(_rev(q), _rev(k), _rev(v), _rev(log_alpha[1]), _rev(beta[1]), s_gdn[:, 1])
    gdn_out = _merge(_rmsnorm(o_f + _rev(o_b), lp['gdn_norm_w'])) * jax.nn.silu(d_g)

    q = _heads(jax.nn.silu(h_q), N_HEADS) * HEAD_DIM ** -0.5
    v = _heads(h_i, N_HEADS)
    fx = h_f.reshape(bsz, l, 2, w)
    log_f = jnp.logaddexp(jnp.log(jnp.maximum(lower_bound, LB_FLOOR)),
                          jnp.log1p(-lower_bound) + jax.nn.log_sigmoid(fx))
    key_in = (1.0 - lower_bound) * jax.nn.sigmoid(-fx)
    o_f, hs_f = _gla(q, _heads(key_in[:, :, 0], N_HEADS), v, _heads(log_f[:, :, 0], N_HEADS), s_hg[:, 0])
    o_b, hs_b = _gla(_rev(q), _rev(_heads(key_in[:, :, 1], N_HEADS)), _rev(v),
                     _rev(_heads(log_f[:, :, 1], N_HEADS)), s_hg[:, 1])
    hg_out = _merge(_rmsnorm(o_f + _rev(o_b), lp['hg_norm_w'])) * jax.nn.silu(h_g)

    u = s_u.reshape(bsz, l, S5_GROUPS, S5_CH)
    p = [lp[nm] for nm in ('s5_a_re', 's5_a_im', 's5_log_step', 's5_b_re', 's5_b_im', 's5_c_re', 's5_c_im')]
    y_f, xr_f, xi_f = _s5(u, p[0][0], p[1][0], p[2][0], p[3][0], p[4][0], p[5][0], p[6][0],
                          s_s5_re[:, 0], s_s5_im[:, 0])
    y_b, xr_b, xi_b = _s5(jnp.flip(u, 1), p[0][1], p[1][1], p[2][1], p[3][1], p[4][1], p[5][1], p[6][1],
                          s_s5_re[:, 1], s_s5_im[:, 1])
    y = (y_f + jnp.flip(y_b, 1)).reshape(bsz, l, w) + lp['s5_d'] * s_u
    z = jax.nn.gelu(y)
    s5_out = z * jax.nn.sigmoid(z @ lp['s5_glu_w'] + lp['s5_glu_b'])

    mixed = jnp.concatenate([ret_out, gdn_out, hg_out, s5_out], axis=-1).astype(BF16)
    new_states = (jnp.stack([rs_f, rs_b], axis=1), jnp.stack([gs_f, gs_b], axis=1),
                  jnp.stack([hs_f, hs_b], axis=1), jnp.stack([xr_f, xr_b], axis=1),
                  jnp.stack([xi_f, xi_b], axis=1))
    return mixed, new_states


MIX_CHUNK = 128
DIRS = 2
NT_DIMS = (((1,), (1,)), ((), ()))
TN_DIMS = (((0,), (0,)), ((), ()))


def _dot(a, b):
    return jnp.dot(a.astype(BF16), b.astype(BF16), preferred_element_type=F32)


def _dot_nt(a, b):
    return lax.dot_general(a.astype(BF16), b.astype(BF16), NT_DIMS, preferred_element_type=F32)


def _dot_tn(a, b):
    return lax.dot_general(a.astype(BF16), b.astype(BF16), TN_DIMS, preferred_element_type=F32)


def _silu(x):
    return x * jax.nn.sigmoid(x)


def _seq_spec(seq_len, width, row_off, col_blk):
    return pl.BlockSpec((seq_len, width), lambda b: (row_off + b, col_blk))


def _mix_params(n_par):
    return pltpu.CompilerParams(dimension_semantics=("parallel",) * n_par, vmem_limit_bytes=VMEM_LIMIT)


def _ret_body(*refs, n_chunks, use_rope, has_s0):
    it = iter(refs)
    lg_ref, q_ref, k_ref, v_ref, g_ref = (next(it) for _ in range(5))
    cos_ref, sin_ref = (next(it), next(it)) if use_rope else (None, None)
    s0_ref = next(it) if has_s0 else None
    o_ref, sf_ref, acc_ref, s_ref, intra_ref, qd_ref, kd_ref, cd_ref = (next(it) for _ in range(8))
    C, H = MIX_CHUNK, N_HEADS

    row = lax.broadcasted_iota(jnp.int32, (C, C), 0)
    col = lax.broadcasted_iota(jnp.int32, (C, C), 1)
    rel = (row - col).astype(F32)
    pos = lax.broadcasted_iota(jnp.int32, (C, HEAD_DIM), 0).astype(F32)
    for d in range(DIRS):
        for h in range(H):
            lg = lg_ref[d, h]
            if d == 0:
                intra_ref[d, h] = jnp.where(rel >= 0, jnp.exp(jnp.maximum(rel, 0.0) * lg), 0.0)
                qd_ref[d, h] = jnp.exp((pos + 1.0) * lg)
                kd_ref[d, h] = jnp.exp((C - 1.0 - pos) * lg)
            else:
                intra_ref[d, h] = jnp.where(rel <= 0, jnp.exp(jnp.maximum(-rel, 0.0) * lg), 0.0)
                qd_ref[d, h] = jnp.exp((C - pos) * lg)
                kd_ref[d, h] = jnp.exp(pos * lg)
            cd_ref[d, h] = jnp.exp(jnp.full((C, HEAD_DIM), C, F32) * lg)
    if has_s0:
        s_ref[...] = s0_ref[...]
    else:
        s_ref[...] = jnp.zeros_like(s_ref)
    acc_ref[...] = jnp.zeros_like(acc_ref)

    units = [(d, h) for d in range(DIRS) for h in range(H)]

    def chunk_step(n, carry):
        rows_d = [pl.ds(pl.multiple_of(c * C, C), C) for c in (n, n_chunks - 1 - n)]
        q_l, k_l, v_l, sc_l = [], [], [], []
        for d, h in units:
            rows, cols = rows_d[d], slice(h * HEAD_DIM, (h + 1) * HEAD_DIM)
            q, k = q_ref[rows, cols], k_ref[rows, cols]
            if use_rope:
                cs, sn = cos_ref[rows, :], sin_ref[rows, :]
                q = q * cs + pltpu.roll(q, HEAD_DIM // 2, 1) * sn
                k = k * cs + pltpu.roll(k, HEAD_DIM // 2, 1) * sn
            k = k * HEAD_DIM ** -0.5
            q_l.append(q)
            k_l.append(k)
            v_l.append(v_ref[rows, cols].astype(BF16))
            sc_l.append(_dot_nt(q, k) * intra_ref[d, h])
        for i, (d, h) in enumerate(units):
            cols = slice(h * HEAD_DIM, (h + 1) * HEAD_DIM)
            s = s_ref[d, h]
            acc_ref[rows_d[d], cols] += _dot(jnp.concatenate([sc_l[i], q_l[i] * qd_ref[d, h]], axis=1),
                                             jnp.concatenate([v_l[i], s.astype(BF16)], axis=0))
            s_ref[d, h] = cd_ref[d, h] * s + _dot_tn(k_l[i] * kd_ref[d, h], v_l[i])
        return carry

    lax.fori_loop(0, n_chunks, chunk_step, 0)
    sf_ref[...] = s_ref[...]

    def finish(n, carry):
        rows = pl.ds(pl.multiple_of(n * C, C), C)
        for h in range(H):
            cols = slice(h * HEAD_DIM, (h + 1) * HEAD_DIM)
            o = acc_ref[rows, cols]
            mu = jnp.mean(o, axis=-1, keepdims=True)
            oc = o - mu
            y = oc * lax.rsqrt(jnp.mean(oc * oc, axis=-1, keepdims=True) + EPS)
            o_ref[rows, cols] = (y * _silu(g_ref[rows, cols])).astype(BF16)
        return carry

    lax.fori_loop(0, n_chunks, finish, 0)


def _retention_pallas(pm, log_gamma, n_seq, seq_len, row_off, rope2, s0, layer):
    use_rope, has_s0 = rope2 is not None, s0 is not None
    st_shape = (DIRS, N_HEADS, HEAD_DIM, HEAD_DIM)
    in_specs = [pl.BlockSpec(memory_space=pltpu.SMEM)]
    in_specs += [_seq_spec(seq_len, GROUP_WIDTH, row_off, cb) for cb in range(4)]
    args = [log_gamma, pm, pm, pm, pm]
    if use_rope:
        in_specs += [pl.BlockSpec((seq_len, HEAD_DIM), lambda b: (0, 0))] * 2
        args += list(rope2)
    if has_s0:
        in_specs.append(pl.BlockSpec((None, None) + st_shape, lambda b: (b, layer, 0, 0, 0, 0)))
        args.append(s0)
    return pl.pallas_call(
        functools.partial(_ret_body, n_chunks=seq_len // MIX_CHUNK, use_rope=use_rope, has_s0=has_s0),
        out_shape=(jax.ShapeDtypeStruct((n_seq * seq_len, GROUP_WIDTH), BF16),
                   jax.ShapeDtypeStruct((n_seq,) + st_shape, F32)),
        grid=(n_seq,),
        in_specs=in_specs,
        out_specs=(pl.BlockSpec((seq_len, GROUP_WIDTH), lambda b: (b, 0)),
                   pl.BlockSpec((None,) + st_shape, lambda b: (b, 0, 0, 0, 0))),
        scratch_shapes=[pltpu.VMEM((seq_len, GROUP_WIDTH), F32), pltpu.VMEM(st_shape, F32),
                        pltpu.VMEM((DIRS, N_HEADS, MIX_CHUNK, MIX_CHUNK), F32),
                        pltpu.VMEM((DIRS, N_HEADS, MIX_CHUNK, HEAD_DIM), F32),
                        pltpu.VMEM((DIRS, N_HEADS, MIX_CHUNK, HEAD_DIM), F32),
                        pltpu.VMEM((DIRS, N_HEADS, MIX_CHUNK, HEAD_DIM), F32)],
        compiler_params=_mix_params(1),
        name="retention",
    )(*args)


def _rope_tables(l):
    cos, sin = _axial_rope(l)
    return jnp.concatenate([cos, cos], axis=-1), jnp.concatenate([-sin, sin], axis=-1)


S5_HALF_G = S5_GROUPS // 2
S5_HALF_U = S5_HALF_G * S5_CH
S5_HALF_X = S5_HALF_G * S5_N
S5_X = S5_GROUPS * S5_N
S5_TC = 512
S5_BLK = 8
S5_TABS = 4


def _s5_tables(a_re, a_im, log_step, b_re, b_im, c_re, c_im):
    dt = jnp.exp(log_step)[..., None]
    mag = jnp.exp(a_re * dt)
    ab_re = mag * jnp.cos(a_im * dt)
    ab_im = mag * jnp.sin(a_im * dt)
    den = a_re * a_re + a_im * a_im
    nr = ab_re - 1.0
    f_re = (nr * a_re + ab_im * a_im) / den
    f_im = (ab_im * a_re - nr * a_im) / den
    bb_re = f_re[..., None] * b_re - f_im[..., None] * b_im
    bb_im = f_re[..., None] * b_im + f_im[..., None] * b_re
    eye = jnp.eye(S5_HALF_G, dtype=F32)

    def in_mat(bb):
        bb = bb.reshape(DIRS, 2, S5_HALF_G, S5_N, S5_CH)
        return jnp.einsum('dhgnc,gk->dhgckn', bb, eye).reshape(DIRS, 2, S5_HALF_U, S5_HALF_X)

    def out_mat(cc):
        cc = cc.reshape(DIRS, 2, S5_HALF_G, S5_CH, S5_N)
        return jnp.einsum('dhgcn,gk->dhgnkc', cc, eye).reshape(DIRS, 2, S5_HALF_X, S5_HALF_U)

    bm = jnp.concatenate([in_mat(bb_re), in_mat(bb_im)], axis=-1).astype(BF16)
    cm = jnp.concatenate([out_mat(c_re), -out_mat(c_im)], axis=-2).astype(BF16)
    t = jnp.arange(S5_BLK, dtype=F32)
    order = jnp.stack([t, S5_BLK - 1.0 - t])
    shifts = 2.0 ** jnp.arange(S5_TABS - 1, dtype=F32)
    expo = jnp.concatenate([jnp.where(order[:, None, :] >= shifts[None, :, None], shifts[None, :, None], jnp.nan),
                            order[:, None, :] + 1.0], axis=1)
    live = ~jnp.isnan(expo)
    e = jnp.where(live, expo, 0.0)[..., None]
    adt_re = (a_re * dt).reshape(DIRS, 1, 1, S5_X)
    adt_im = (a_im * dt).reshape(DIRS, 1, 1, S5_X)
    pmag = jnp.where(live[..., None], jnp.exp(e * adt_re), 0.0)
    pw_re = pmag * jnp.cos(e * adt_im)
    pw_im = pmag * jnp.sin(e * adt_im)
    return bm, cm, pw_re, pw_im


def _gelu_tanh(x):
    return 0.5 * x * (1.0 + jnp.tanh(math.sqrt(2.0 / math.pi) * (x + 0.044715 * (x * x * x))))


def _s5_body(*refs, seq_len, has_s0):
    it = iter(refs)
    u_ref, bm_ref, cm_ref, pwr_ref, pwi_ref, d_ref, gw_ref, gb_ref = (next(it) for _ in range(8))
    x0r_ref, x0i_ref = (next(it), next(it)) if has_s0 else (None, None)
    o_ref, sfr_ref, sfi_ref, y_ref, xr_ref, xi_ref = (next(it) for _ in range(6))
    tc = min(seq_len, S5_TC)
    n_tiles = seq_len // tc
    n_blk = tc // S5_BLK

    y_ref[...] = d_ref[...] * u_ref[...]
    for d in range(DIRS):
        last = S5_BLK - 1 if d == 0 else 0
        for hf in range(2):
            xs = slice(hf * S5_HALF_X, (hf + 1) * S5_HALF_X)
            us = slice(hf * S5_HALF_U, (hf + 1) * S5_HALF_U)

            def scan_block(j, carry, d=d, xs=xs, last=last):
                car_re, car_im = carry
                blk = j if d == 0 else n_blk - 1 - j
                rows = pl.ds(pl.multiple_of(blk * S5_BLK, S5_BLK), S5_BLK)
                xr, xi = xr_ref[rows, :], xi_ref[rows, :]
                for k in range(S5_TABS - 1):
                    s = 1 << k
                    shift = s if d == 0 else S5_BLK - s
                    p_re, p_im = pwr_ref[d, k, :, xs], pwi_ref[d, k, :, xs]
                    sr, si = pltpu.roll(xr, shift, 0), pltpu.roll(xi, shift, 0)
                    xr, xi = xr + p_re * sr - p_im * si, xi + p_re * si + p_im * sr
                p_re, p_im = pwr_ref[d, S5_TABS - 1, :, xs], pwi_ref[d, S5_TABS - 1, :, xs]
                xr, xi = xr + p_re * car_re - p_im * car_im, xi + p_re * car_im + p_im * car_re
                xr_ref[rows, :] = xr
                xi_ref[rows, :] = xi
                return xr[last:last + 1, :], xi[last:last + 1, :]

            def scan_tile(i, carry, d=d, hf=hf, us=us, scan_block=scan_block):
                tile = i if d == 0 else n_tiles - 1 - i
                rows = pl.ds(pl.multiple_of(tile * tc, tc), tc)
                bu = _dot(u_ref[rows, us], bm_ref[d, hf])
                xr_ref[...] = bu[:, :S5_HALF_X]
                xi_ref[...] = bu[:, S5_HALF_X:]
                carry = lax.fori_loop(0, n_blk, scan_block, carry)
                y_ref[rows, us] += (_dot(xr_ref[...], cm_ref[d, hf, :S5_HALF_X, :])
                                    + _dot(xi_ref[...], cm_ref[d, hf, S5_HALF_X:, :]))
                return carry

            if has_s0:
                carry0 = (x0r_ref[d, :, xs], x0i_ref[d, :, xs])
            else:
                carry0 = (jnp.zeros((1, S5_HALF_X), F32), jnp.zeros((1, S5_HALF_X), F32))
            car_re, car_im = lax.fori_loop(0, n_tiles, scan_tile, carry0)
            sfr_ref[d, :, xs] = car_re
            sfi_ref[d, :, xs] = car_im

    z = _gelu_tanh(y_ref[...])
    o_ref[...] = (z * jax.nn.sigmoid(_dot(z, gw_ref[...]) + gb_ref[...])).astype(BF16)


def _s5_pallas(pm, tables, s5_d, glu_w, glu_b, n_seq, seq_len, row_off, x0, layer):
    bm, cm, pw_re, pw_im = tables
    has_s0 = x0 is not None
    full = lambda shape: pl.BlockSpec(shape, lambda b: (0,) * len(shape))
    in_specs = [_seq_spec(seq_len, GROUP_WIDTH, row_off, 13),
                full(bm.shape), full(cm.shape), full(pw_re.shape), full(pw_im.shape),
                full((1, GROUP_WIDTH)), full((GROUP_WIDTH, GROUP_WIDTH)), full((1, GROUP_WIDTH))]
    args = [pm, bm, cm, pw_re, pw_im, s5_d.reshape(1, GROUP_WIDTH), glu_w.astype(BF16),
            glu_b.reshape(1, GROUP_WIDTH)]
    if has_s0:
        in_specs += [pl.BlockSpec((None, None, DIRS, 1, S5_X), lambda b: (b, layer, 0, 0, 0))] * 2
        args += list(x0)
    st = jax.ShapeDtypeStruct((n_seq, DIRS, 1, S5_X), F32)
    st_spec = pl.BlockSpec((None, DIRS, 1, S5_X), lambda b: (b, 0, 0, 0))
    tc = min(seq_len, S5_TC)
    return pl.pallas_call(
        functools.partial(_s5_body, seq_len=seq_len, has_s0=has_s0),
        out_shape=(jax.ShapeDtypeStruct((n_seq * seq_len, GROUP_WIDTH), BF16), st, st),
        grid=(n_seq,),
        in_specs=in_specs,
        out_specs=(pl.BlockSpec((seq_len, GROUP_WIDTH), lambda b: (b, 0)), st_spec, st_spec),
        scratch_shapes=[pltpu.VMEM((seq_len, GROUP_WIDTH), F32),
                        pltpu.VMEM((tc, S5_HALF_X), F32), pltpu.VMEM((tc, S5_HALF_X), F32)],
        compiler_params=_mix_params(1),
        name="s5",
    )(*args)


GLA_LEVELS = 7


def _chunk_cumsum_rows(x, rowi):
    s = 1
    while s < MIX_CHUNK:
        x = x + jnp.where(rowi >= s, pltpu.roll(x, s, 0), 0.0)
        s *= 2
    return x


def _gla_body(*refs, n_chunks, has_s0):
    it = iter(refs)
    q_ref, f0_ref, f1_ref, i_ref, g_ref, lb_ref, nw_ref = (next(it) for _ in range(7))
    s0_ref = next(it) if has_s0 else None
    o_ref, sf_ref, acc_ref, s_ref, code_ref = (next(it) for _ in range(5))
    C, H = MIX_CHUNK, N_HEADS
    f_refs = (f0_ref, f1_ref)

    rowi = lax.broadcasted_iota(jnp.int32, (C, HEAD_DIM), 0)
    ri = lax.broadcasted_iota(jnp.int32, (C, C), 0)
    ci = lax.broadcasted_iota(jnp.int32, (C, C), 1)
    top_bit = 31 - lax.clz(ri ^ ci)
    code_ref[...] = jnp.where(ri > ci, top_bit, jnp.where(ri < ci, -1 - top_bit, GLA_LEVELS))
    if has_s0:
        s_ref[...] = s0_ref[...]
    else:
        s_ref[...] = jnp.zeros_like(s_ref)
    acc_ref[...] = jnp.zeros_like(acc_ref)

    def chunk_step(n, carry):
        for d in range(DIRS):
            c = n if d == 0 else n_chunks - 1 - n
            rows = pl.ds(pl.multiple_of(c * C, C), C)
            for h in range(H):
                cols = slice(h * HEAD_DIM, (h + 1) * HEAD_DIM)
                code = code_ref[...]
                q = _silu(q_ref[rows, cols]) * HEAD_DIM ** -0.5
                v = i_ref[rows, cols].astype(BF16)
                fx = f_refs[d][rows, cols]
                lb = lb_ref[d:d + 1, cols]
                sig = 1.0 / (1.0 + jnp.exp(-fx))
                logf = jnp.log(jnp.maximum(lb, LB_FLOOR) + (1.0 - lb) * sig)
                k = (1.0 - lb) * (1.0 - sig)
                cum = _chunk_cumsum_rows(logf, rowi)
                own = cum
                attn = jnp.where(code == GLA_LEVELS, _dot_nt(q, k), 0.0)
                for lvl in range(GLA_LEVELS):
                    m = 1 << lvl
                    prev = pltpu.roll(own, m, 0)
                    pre = jnp.minimum(cum - prev, 0.0)
                    suf = own - cum
                    if d == 0:
                        sc = _dot_nt(q * jnp.exp(pre), k * jnp.exp(suf))
                        hit = code == lvl
                    else:
                        sc = _dot_nt(q * jnp.exp(suf + logf), k * jnp.exp(jnp.minimum(pre - logf, 0.0)))
                        hit = code == -1 - lvl
                    attn = jnp.where(hit, sc, attn)
                    own = jnp.where(((rowi >> lvl) & 1) == 0, pltpu.roll(own, C - m, 0), own)
                tot = own
                st = s_ref[d, h]
                if d == 0:
                    q_in, k_out = q * jnp.exp(cum), k * jnp.exp(tot - cum)
                else:
                    q_in, k_out = q * jnp.exp(tot - cum + logf), k * jnp.exp(cum - logf)
                acc_ref[rows, cols] += _dot(attn, v) + _dot_nt(q_in, st)
                s_ref[d, h] = jnp.exp(tot[0:1, :]) * st + _dot_tn(v, k_out)
        return carry

    lax.fori_loop(0, n_chunks, chunk_step, 0)
    sf_ref[...] = s_ref[...]

    def finish(n, carry):
        rows = pl.ds(pl.multiple_of(n * C, C), C)
        for h in range(H):
            cols = slice(h * HEAD_DIM, (h + 1) * HEAD_DIM)
            o = acc_ref[rows, cols]
            y = o * lax.rsqrt(jnp.mean(o * o, axis=-1, keepdims=True) + EPS) * nw_ref[...]
            o_ref[rows, cols] = (y * _silu(g_ref[rows, cols])).astype(BF16)
        return carry

    lax.fori_loop(0, n_chunks, finish, 0)


def _gla_pallas(pm, lower_bound, norm_w, n_seq, seq_len, row_off, s0_t, layer):
    has_s0 = s0_t is not None
    st_shape = (DIRS, N_HEADS, HEAD_DIM, HEAD_DIM)
    in_specs = [_seq_spec(seq_len, GROUP_WIDTH, row_off, cb) for cb in (8, 9, 10, 11, 12)]
    in_specs += [pl.BlockSpec((DIRS, GROUP_WIDTH), lambda b: (0, 0)), pl.BlockSpec((1, HEAD_DIM), lambda b: (0, 0))]
    args = [pm] * 5 + [lower_bound, norm_w.reshape(1, HEAD_DIM)]
    if has_s0:
        in_specs.append(pl.BlockSpec((None, None) + st_shape, lambda b: (b, layer, 0, 0, 0, 0)))
        args.append(s0_t)
    return pl.pallas_call(
        functools.partial(_gla_body, n_chunks=seq_len // MIX_CHUNK, has_s0=has_s0),
        out_shape=(jax.ShapeDtypeStruct((n_seq * seq_len, GROUP_WIDTH), BF16),
                   jax.ShapeDtypeStruct((n_seq,) + st_shape, F32)),
        grid=(n_seq,),
        in_specs=in_specs,
        out_specs=(pl.BlockSpec((seq_len, GROUP_WIDTH), lambda b: (b, 0)),
                   pl.BlockSpec((None,) + st_shape, lambda b: (b, 0, 0, 0, 0))),
        scratch_shapes=[pltpu.VMEM((seq_len, GROUP_WIDTH), F32), pltpu.VMEM(st_shape, F32),
                        pltpu.VMEM((MIX_CHUNK, MIX_CHUNK), jnp.int32)],
        compiler_params=_mix_params(1),
        name="hgrn2",
    )(*args)


GDN_BLOCK_BITS = 4
GDN_MERGES = 3
N_GATES = DIRS * N_HEADS


def _softplus(x):
    return jnp.maximum(x, 0.0) + jnp.log1p(jnp.exp(-jnp.abs(x)))


def _gdn_body(*refs, seq_len, has_s0):
    it = iter(refs)
    (q_ref, k_ref, v_ref, g_ref, ps_ref, pst_ref, cw_ref, prow_ref, pcol_ref, nw_ref) = (next(it) for _ in range(10))
    s0_ref = next(it) if has_s0 else None
    (o_ref, sf_ref, acc_ref, s_ref, qn_ref, kn_ref, vn_ref, gate_ref, gatet_ref, blev_ref) = (
        next(it) for _ in range(10))
    C, H, L = MIX_CHUNK, N_HEADS, seq_len
    n_chunks = L // C
    w = GROUP_WIDTH

    rowl = lax.broadcasted_iota(jnp.int32, (L, HEAD_DIM), 0)
    for part, (src, dst) in enumerate(((q_ref, qn_ref), (k_ref, kn_ref), (v_ref, vn_ref))):
        for h in range(H):
            cols = slice(h * HEAD_DIM, (h + 1) * HEAD_DIM)
            wc = slice(part * w + h * HEAD_DIM, part * w + (h + 1) * HEAD_DIM)
            x = src[:, cols]
            x_prev = jnp.where(rowl >= 1, pltpu.roll(x, 1, 0), 0.0)
            x_next = jnp.where(rowl < L - 1, pltpu.roll(x, L - 1, 0), 0.0)
            y = _silu(cw_ref[0:1, wc] * x_prev + cw_ref[1:2, wc] * x + cw_ref[2:3, wc] * x_next)
            if part < 2:
                y = y * lax.rsqrt(jnp.sum(y * y, axis=-1, keepdims=True) + EPS)
            if part == 0:
                y = y * HEAD_DIM ** -0.5
            dst[:, cols] = y

    rowi = lax.broadcasted_iota(jnp.int32, (C, HEAD_DIM), 0)
    lane = lax.broadcasted_iota(jnp.int32, (C, HEAD_DIM), 1)
    lane_t = lax.broadcasted_iota(jnp.int32, (2 * N_GATES, C), 1)
    row_t = lax.broadcasted_iota(jnp.int32, (2 * N_GATES, C), 0)
    for c in range(n_chunks):
        rows = slice(c * C, (c + 1) * C)
        a = ps_ref[rows, :]
        la = -jnp.exp(prow_ref[0:1, :]) * _softplus(a + prow_ref[1:2, :])
        pre = _chunk_cumsum_rows(la, rowi)
        tot = jnp.broadcast_to(pre[C - 1:C, :], (C, HEAD_DIM))
        g = jnp.where(lane < N_HEADS, pre, tot - pre + la)
        gate_ref[rows, :] = jnp.where(lane < N_GATES, g,
                                      jnp.where(lane < 2 * N_GATES, jax.nn.sigmoid(a),
                                                pltpu.roll(tot, 2 * N_GATES, 1)))
        at = pst_ref[:, rows]
        lat = -jnp.exp(pcol_ref[:, 0:1]) * _softplus(at + pcol_ref[:, 1:2])
        pre_t, s = lat, 1
        while s < C:
            pre_t = pre_t + jnp.where(lane_t >= s, pltpu.roll(pre_t, s, 1), 0.0)
            s *= 2
        tot_t = jnp.broadcast_to(pre_t[:, C - 1:C], (2 * N_GATES, C))
        g_t = jnp.where(row_t < N_HEADS, pre_t, tot_t - pre_t + lat)
        gatet_ref[:, rows] = jnp.where(row_t < N_GATES, g_t, jax.nn.sigmoid(at))

    ri = lax.broadcasted_iota(jnp.int32, (C, C), 0)
    ci = lax.broadcasted_iota(jnp.int32, (C, C), 1)
    bx = (ri >> GDN_BLOCK_BITS) ^ (ci >> GDN_BLOCK_BITS)
    blev_ref[...] = jnp.where(bx == 0, 0, 32 - lax.clz(bx))
    if has_s0:
        s_ref[...] = s0_ref[...]
    else:
        s_ref[...] = jnp.zeros_like(s_ref)
    acc_ref[...] = jnp.zeros_like(acc_ref)

    units = [(d, h) for d in range(DIRS) for h in range(H)]

    def chunk_step(n, carry):
        eye = jnp.where(ri == ci, 1.0, 0.0)
        blev = blev_ref[...]
        rows_d = [pl.ds(pl.multiple_of(c * C, C), C) for c in (n, n_chunks - 1 - n)]
        incl_d = [ri >= ci, ri <= ci]
        strict_d = [ri > ci, ri < ci]
        q_l, k_l, eg_l, et_l, ek_l, rhs_l, a_l, attn_l = ([] for _ in range(8))
        for d, h in units:
            rows, cols, cg = rows_d[d], slice(h * HEAD_DIM, (h + 1) * HEAD_DIM), d * N_HEADS + h
            q, k, v = qn_ref[rows, cols], kn_ref[rows, cols], vn_ref[rows, cols]
            g_i = jnp.broadcast_to(gate_ref[rows, cg:cg + 1], (C, C))
            b_i = jnp.broadcast_to(gate_ref[rows, N_GATES + cg:N_GATES + cg + 1], (C, C))
            tot = jnp.broadcast_to(gate_ref[rows, 2 * N_GATES + cg:2 * N_GATES + cg + 1], (C, C))
            g_j = gatet_ref[cg:cg + 1, rows]
            decay = jnp.where(incl_d[d], jnp.exp(jnp.minimum(g_i - g_j, 0.0)), 0.0)
            e_g = jnp.exp(g_i)
            kb = k.astype(BF16)
            a_l.append(jnp.where(strict_d[d], _dot_nt(kb, kb) * b_i * decay, 0.0))
            attn_l.append(_dot_nt(q, kb) * decay)
            rhs_l.append(jnp.concatenate([v * b_i, k * (b_i * e_g)], axis=1).astype(BF16))
            q_l.append(q * e_g)
            ek_l.append(k * jnp.exp(tot - g_i))
            et_l.append(jnp.exp(tot))
        b_l = [-jnp.where(blev == 0, a, 0.0) for a in a_l]
        p_l = [eye + b for b in b_l]
        for _ in range(GDN_BLOCK_BITS - 1):
            b_l = [_dot(b, b) for b in b_l]
            p_l = [p + _dot(p, b) for p, b in zip(p_l, b_l)]
        for lvl in range(1, GDN_MERGES + 1):
            ep_l = [_dot(jnp.where(blev == lvl, a, 0.0), p) for a, p in zip(a_l, p_l)]
            p_l = [p - _dot(p, ep) for p, ep in zip(p_l, ep_l)]
        uw_l = [_dot(p, rhs) for p, rhs in zip(p_l, rhs_l)]
        s_l = [s_ref[d, h] for d, h in units]
        vn_l = [uw[:, :HEAD_DIM] - _dot(uw[:, HEAD_DIM:], s) for uw, s in zip(uw_l, s_l)]
        for i, (d, h) in enumerate(units):
            cols = slice(h * HEAD_DIM, (h + 1) * HEAD_DIM)
            acc_ref[rows_d[d], cols] += _dot(jnp.concatenate([q_l[i], attn_l[i]], axis=1),
                                             jnp.concatenate([s_l[i], vn_l[i]], axis=0))
            s_ref[d, h] = s_l[i] * et_l[i] + _dot_tn(ek_l[i], vn_l[i])
        return carry

    lax.fori_loop(0, n_chunks, chunk_step, 0)
    sf_ref[...] = s_ref[...]

    def finish(n, carry):
        rows = pl.ds(pl.multiple_of(n * C, C), C)
        for h in range(H):
            cols = slice(h * HEAD_DIM, (h + 1) * HEAD_DIM)
            o = acc_ref[rows, cols]
            y = o * lax.rsqrt(jnp.mean(o * o, axis=-1, keepdims=True) + EPS) * nw_ref[...]
            o_ref[rows, cols] = (y * _silu(g_ref[rows, cols])).astype(BF16)
        return carry

    lax.fori_loop(0, n_chunks, finish, 0)


def _gdn_pallas(pm, ps, ps_t, conv_w, a_log, dt_bias, norm_w, n_seq, seq_len, row_off, s0, layer):
    has_s0 = s0 is not None
    st_shape = (DIRS, N_HEADS, HEAD_DIM, HEAD_DIM)
    par = jnp.stack([a_log.reshape(N_GATES), dt_bias.reshape(N_GATES)])
    par_row = jnp.pad(par, ((0, 0), (0, HEAD_DIM - N_GATES)))
    par_col = jnp.pad(par.T, ((0, N_GATES), (0, 0)))
    full = lambda shape: pl.BlockSpec(shape, lambda b: (0,) * len(shape))
    in_specs = [_seq_spec(seq_len, GROUP_WIDTH, row_off, cb) for cb in (4, 5, 6, 7)]
    in_specs += [_seq_spec(seq_len, PROJ_SMALL, row_off, 0),
                 pl.BlockSpec((2 * N_GATES, seq_len), lambda b: (0, row_off + b)),
                 full((GDN_CONV, 3 * GROUP_WIDTH)), full((2, HEAD_DIM)), full((2 * N_GATES, 2)), full((1, HEAD_DIM))]
    args = [pm] * 4 + [ps, ps_t, conv_w, par_row, par_col, norm_w.reshape(1, HEAD_DIM)]
    if has_s0:
        in_specs.append(pl.BlockSpec((None, None) + st_shape, lambda b: (b, layer, 0, 0, 0, 0)))
        args.append(s0)
    seq_f32 = lambda width: pltpu.VMEM((seq_len, width), F32)
    return pl.pallas_call(
        functools.partial(_gdn_body, seq_len=seq_len, has_s0=has_s0),
        out_shape=(jax.ShapeDtypeStruct((n_seq * seq_len, GROUP_WIDTH), BF16),
                   jax.ShapeDtypeStruct((n_seq,) + st_shape, F32)),
        grid=(n_seq,),
        in_specs=in_specs,
        out_specs=(pl.BlockSpec((seq_len, GROUP_WIDTH), lambda b: (b, 0)),
                   pl.BlockSpec((None,) + st_shape, lambda b: (b, 0, 0, 0, 0))),
        scratch_shapes=[seq_f32(GROUP_WIDTH), pltpu.VMEM(st_shape, F32),
                        seq_f32(GROUP_WIDTH), seq_f32(GROUP_WIDTH), seq_f32(GROUP_WIDTH),
                        seq_f32(HEAD_DIM), pltpu.VMEM((2 * N_GATES, seq_len), F32),
                        pltpu.VMEM((MIX_CHUNK, MIX_CHUNK), jnp.int32)],
        compiler_params=_mix_params(1),
        name="gated_delta",
    )(*args)


def kernel(x_prompt, x_sample, state_ret, state_gdn, state_hgrn, state_s5_re, state_s5_im, c, c_ctx, norm1_w, norm2_w, final_norm_w, ada_w, ada_b, in_proj, out_proj, ret_decay_logit, gdn_conv, gdn_a_log, gdn_dt_bias, gdn_norm_w, hg_lb_param, hg_norm_w, s5_a_re, s5_a_im, s5_b_re, s5_b_im, s5_c_re, s5_c_im, s5_log_step, s5_d, s5_glu_w, s5_glu_b, ffn_w1, ffn_w3, ffn_w2):
    lb_soft = jax.nn.softmax(hg_lb_param, axis=0)
    lower_bounds = jnp.cumsum(lb_soft, axis=0) - lb_soft[0]
    rope2 = _rope_tables(DEC_SEQ)

    cvec = jnp.zeros((N_SEQ_ROWS, D_MODEL), F32).at[0].set(c_ctx).at[1:1 + DEC_BATCH].set(c)
    mod_all = _ada(cvec, ada_w, ada_b).reshape(DEPTH, N_SEQ_ROWS, N_MOD, 1, D_MODEL)

    w_main = jnp.concatenate([in_proj[:, :, :8 * GROUP_WIDTH], in_proj[:, :, 8 * GROUP_WIDTH + 16:]],
                             axis=-1).astype(BF16)
    w_small = jnp.pad(in_proj[:, :, 8 * GROUP_WIDTH:8 * GROUP_WIDTH + 16],
                      ((0, 0), (0, 0), (0, PROJ_SMALL - 16))).astype(BF16)
    w_out = out_proj.astype(BF16)
    w1, w3, w2 = ffn_w1.astype(BF16), ffn_w3.astype(BF16), ffn_w2.astype(BF16)

    x = jnp.concatenate([x_prompt.reshape(N_CTX_TOK, D_MODEL), x_sample.reshape(N_LAT_TOK, D_MODEL)], axis=0)
    hgrn_t = jnp.swapaxes(state_hgrn, -1, -2)
    s5_x0 = (state_s5_re.reshape(DEC_BATCH, DEPTH, DIRS, 1, S5_X), state_s5_im.reshape(DEC_BATCH, DEPTH, DIRS, 1, S5_X))
    lat_off = N_CTX_TOK // DEC_SEQ
    ctx_states = []
    for i in range(DEPTH):
        mod = mod_all[i]
        pm, ps = _inproj(x, norm1_w[i][None], mod, w_main, w_small, i)
        ps_t = ps[:, :2 * N_GATES].T
        lg = jax.nn.log_sigmoid(ret_decay_logit[i])
        tables = _s5_tables(s5_a_re[i], s5_a_im[i], s5_log_step[i], s5_b_re[i], s5_b_im[i], s5_c_re[i], s5_c_im[i])
        ctx = (BATCH, SEQ, 0)
        lat = (DEC_BATCH, DEC_SEQ, lat_off)

        ret_c, rs = _retention_pallas(pm, lg, *ctx, None, None, i)
        ret_l, _ = _retention_pallas(pm, lg, *lat, rope2, state_ret, i)
        gdn_args = (pm, ps, ps_t, gdn_conv[i], gdn_a_log[i], gdn_dt_bias[i], gdn_norm_w[i])
        gdn_c, gs = _gdn_pallas(*gdn_args, *ctx, None, i)
        gdn_l, _ = _gdn_pallas(*gdn_args, *lat, state_gdn, i)
        hg_c, hs = _gla_pallas(pm, lower_bounds[i], hg_norm_w[i], *ctx, None, i)
        hg_l, _ = _gla_pallas(pm, lower_bounds[i], hg_norm_w[i], *lat, hgrn_t, i)
        s5_args = (pm, tables, s5_d[i], s5_glu_w[i], s5_glu_b[i])
        s5_c, xr, xi = _s5_pallas(*s5_args, *ctx, None, i)
        s5_l, _, _ = _s5_pallas(*s5_args, *lat, s5_x0, i)
        ctx_states.append((rs, gs, jnp.swapaxes(hs, -1, -2), xr.reshape(BATCH, DIRS, S5_GROUPS, S5_N),
                           xi.reshape(BATCH, DIRS, S5_GROUPS, S5_N)))

        parts = ((ret_c, ret_l), (gdn_c, gdn_l), (hg_c, hg_l), (s5_c, s5_l))
        x = _outproj(parts, w_out, x, mod, i)
        x = _ffn(x, norm2_w[i][None], mod, w1, w3, w2, final_norm_w[None], i)

    y_prompt = x[0].reshape(BATCH, SEQ, D_MODEL)
    y_sample = x[1].reshape(DEC_BATCH, DEC_SEQ, D_MODEL)
    new_states = tuple(jnp.stack([s[j] for s in ctx_states], axis=1) for j in range(5))
    return (y_prompt, y_sample) + new_states
```

```python
import functools
import math

import jax
import jax.numpy as jnp
from jax import lax
from jax.experimental import pallas as pl
from jax.experimental.pallas import tpu as pltpu

F32 = jnp.float32
BF16 = jnp.bfloat16

D_MODEL = 2048
BATCH = 16
SEQ = 256
DEPTH = 2
DEC_BATCH = 8
DEC_SEQ = 1024
GRID_W = 64
HEAD_DIM = 128
GROUP_WIDTH = 512
N_HEADS = 4
S5_CH = 16
S5_GROUPS = 32
S5_N = 64
GDN_CONV = 3
CHUNK = 64
HG_CHUNK = 16
ROPE_BASE = 10000.0
FFN_HIDDEN = 5632
N_MOD = 6
EPS = 1e-6
LB_FLOOR = 1e-30

N_CTX_TOK = BATCH * SEQ
N_LAT_TOK = DEC_BATCH * DEC_SEQ
N_TOK = N_CTX_TOK + N_LAT_TOK
N_SEQ_ROWS = 16
PROJ_MAIN = 14 * GROUP_WIDTH
PROJ_SMALL = 128
VMEM_LIMIT = 56 * 1024 * 1024


def _seq_row(tile, tm):
    n_ctx = N_CTX_TOK // tm
    per_lat = DEC_SEQ // tm
    return jnp.where(tile < n_ctx, 0, 1 + (tile - n_ctx) // per_lat)


def _stream_specs(tm, width, n_col=1):
    n_ctx = N_CTX_TOK // tm

    def ctx_map(i, j):
        return jnp.minimum(i, n_ctx - 1), (jnp.where(i < n_ctx, j, n_col - 1) if n_col > 1 else 0)

    def lat_map(i, j):
        return jnp.maximum(i - n_ctx, 0), (jnp.where(i >= n_ctx, j, 0) if n_col > 1 else 0)

    return pl.BlockSpec((tm, width), ctx_map), pl.BlockSpec((tm, width), lat_map)


def _on_stream(tile, tm, fn):
    n_ctx = N_CTX_TOK // tm
    pl.when(tile < n_ctx)(functools.partial(fn, 0))
    pl.when(tile >= n_ctx)(functools.partial(fn, 1))


def _ada_body(c_ref, w_ref, b_ref, o_ref):
    cv = c_ref[...]
    s = cv * jax.nn.sigmoid(cv)
    o_ref[0] = jnp.dot(s.astype(BF16), w_ref[0].astype(BF16), preferred_element_type=F32) + b_ref[0]


def _ada(cvec, ada_w, ada_b):
    tn = 1024
    n = N_MOD * D_MODEL
    return pl.pallas_call(
        _ada_body,
        out_shape=jax.ShapeDtypeStruct((DEPTH, N_SEQ_ROWS, n), F32),
        grid=(DEPTH, n // tn),
        in_specs=[
            pl.BlockSpec((N_SEQ_ROWS, D_MODEL), lambda l, j: (0, 0)),
            pl.BlockSpec((1, D_MODEL, tn), lambda l, j: (l, 0, j)),
            pl.BlockSpec((1, 1, tn), lambda l, j: (l, 0, j)),
        ],
        out_specs=pl.BlockSpec((1, N_SEQ_ROWS, tn), lambda l, j: (l, 0, j)),
        compiler_params=pltpu.CompilerParams(
            dimension_semantics=("parallel", "parallel"), vmem_limit_bytes=VMEM_LIMIT),
        name="ada_mod",
    )(cvec, ada_w, ada_b.reshape(DEPTH, 1, n))


def _norm_mod(x, nw, sc, sh):
    ms = jnp.mean(x * x, axis=-1, keepdims=True)
    y = x * lax.rsqrt(ms + EPS) * nw
    return y * (1.0 + sc) + sh


PROJ_TM = 1024


def _inproj_body(x_ref, nw_ref, sc_ref, sh_ref, w_ref, ws_ref, o_ref, os_ref, h_ref):
    @pl.when(pl.program_id(1) == 0)
    def _():
        hb = _norm_mod(x_ref[...], nw_ref[...], sc_ref[...], sh_ref[...]).astype(BF16)
        h_ref[...] = hb
        os_ref[...] = jnp.dot(hb, ws_ref[...], preferred_element_type=F32)

    o_ref[...] = jnp.dot(h_ref[...], w_ref[...], preferred_element_type=F32)


def _inproj(x, nw, mod, w_main, w_small, layer):
    tm, tn = PROJ_TM, 1024
    return pl.pallas_call(
        _inproj_body,
        out_shape=(jax.ShapeDtypeStruct((N_TOK, PROJ_MAIN), F32),
                   jax.ShapeDtypeStruct((N_TOK, PROJ_SMALL), F32)),
        grid=(N_TOK // tm, PROJ_MAIN // tn),
        in_specs=[
            pl.BlockSpec((tm, D_MODEL), lambda i, j: (i, 0)),
            pl.BlockSpec((1, D_MODEL), lambda i, j: (0, 0)),
            pl.BlockSpec((None, None, 1, D_MODEL), lambda i, j: (_seq_row(i, tm), 1, 0, 0)),
            pl.BlockSpec((None, None, 1, D_MODEL), lambda i, j: (_seq_row(i, tm), 0, 0, 0)),
            pl.BlockSpec((None, D_MODEL, tn), lambda i, j: (layer, 0, j)),
            pl.BlockSpec((None, D_MODEL, PROJ_SMALL), lambda i, j: (layer, 0, 0)),
        ],
        out_specs=(pl.BlockSpec((tm, tn), lambda i, j: (i, j)),
                   pl.BlockSpec((tm, PROJ_SMALL), lambda i, j: (i, 0))),
        scratch_shapes=[pltpu.VMEM((tm, D_MODEL), BF16)],
        compiler_params=pltpu.CompilerParams(
            dimension_semantics=("parallel", "arbitrary"), vmem_limit_bytes=VMEM_LIMIT),
        name="in_proj",
    )(x, nw, mod, mod, w_main, w_small)


def _outproj_body(*refs):
    n_mix = 4
    m_refs, (w_ref, x_ref, g_ref, o_ref) = refs[:2 * n_mix], refs[2 * n_mix:]

    def compute(stream):
        acc = None
        for p in range(n_mix):
            part = jnp.dot(m_refs[2 * p + stream][...], w_ref[p * GROUP_WIDTH:(p + 1) * GROUP_WIDTH, :],
                           preferred_element_type=F32)
            acc = part if acc is None else acc + part
        o_ref[...] = x_ref[...] + g_ref[...] * acc

    _on_stream(pl.program_id(0), PROJ_TM, compute)


def _outproj(parts, w, x, mod, layer):
    tm, tn = PROJ_TM, 1024
    in_specs = []
    for _ in parts:
        in_specs += list(_stream_specs(tm, GROUP_WIDTH))
    in_specs += [pl.BlockSpec((None, D_MODEL, tn), lambda i, j: (layer, 0, j)),
                 pl.BlockSpec((tm, tn), lambda i, j: (i, j)),
                 pl.BlockSpec((None, None, 1, tn), lambda i, j: (_seq_row(i, tm), 2, 0, j))]
    return pl.pallas_call(
        _outproj_body,
        out_shape=jax.ShapeDtypeStruct((N_TOK, D_MODEL), F32),
        grid=(N_TOK // tm, D_MODEL // tn),
        in_specs=in_specs,
        out_specs=pl.BlockSpec((tm, tn), lambda i, j: (i, j)),
        compiler_params=pltpu.CompilerParams(
            dimension_semantics=("parallel", "arbitrary"), vmem_limit_bytes=VMEM_LIMIT),
        name="out_proj",
    )(*(a for pair in parts for a in pair), w, x, mod)


FFN_TM = 512


def _ffn_body(x_ref, nw_ref, sc_ref, sh_ref, g_ref, w1_ref, w3_ref, w2_ref, fw_ref, *rest, final_norm):
    o_refs, (h_ref, acc_ref) = rest[:-2], rest[-2:]
    i, k = pl.program_id(0), pl.program_id(1)

    @pl.when(k == 0)
    def _():
        h_ref[...] = _norm_mod(x_ref[...], nw_ref[...], sc_ref[...], sh_ref[...]).astype(BF16)
        acc_ref[...] = jnp.zeros_like(acc_ref)

    h = h_ref[...]
    a = jnp.dot(h, w1_ref[...], preferred_element_type=F32)
    b = jnp.dot(h, w3_ref[...], preferred_element_type=F32)
    g = (a * jax.nn.sigmoid(a) * b).astype(BF16)
    acc_ref[...] += jnp.dot(g, w2_ref[...], preferred_element_type=F32)

    @pl.when(k == pl.num_programs(1) - 1)
    def _():
        y = x_ref[...] + g_ref[...] * acc_ref[...]
        if not final_norm:
            o_refs[0][...] = y
        else:
            ms = jnp.mean(y * y, axis=-1, keepdims=True)
            y = y * lax.rsqrt(ms + EPS) * fw_ref[...]

            def write(stream):
                o_refs[stream][...] = y

            _on_stream(i, FFN_TM, write)


def _ffn(x, nw, mod, w1, w3, w2, fw, layer):
    final_norm = layer == DEPTH - 1
    tm, th = FFN_TM, 512
    if final_norm:
        out_shape = (jax.ShapeDtypeStruct((N_CTX_TOK, D_MODEL), F32), jax.ShapeDtypeStruct((N_LAT_TOK, D_MODEL), F32))
        out_specs = _stream_specs(tm, D_MODEL)
    else:
        out_shape = jax.ShapeDtypeStruct((N_TOK, D_MODEL), F32)
        out_specs = pl.BlockSpec((tm, D_MODEL), lambda i, k: (i, 0))
    return pl.pallas_call(
        functools.partial(_ffn_body, final_norm=final_norm),
        out_shape=out_shape,
        grid=(N_TOK // tm, FFN_HIDDEN // th),
        in_specs=[
            pl.BlockSpec((tm, D_MODEL), lambda i, k: (i, 0)),
            pl.BlockSpec((1, D_MODEL), lambda i, k: (0, 0)),
            pl.BlockSpec((None, None, 1, D_MODEL), lambda i, k: (_seq_row(i, tm), 4, 0, 0)),
            pl.BlockSpec((None, None, 1, D_MODEL), lambda i, k: (_seq_row(i, tm), 3, 0, 0)),
            pl.BlockSpec((None, None, 1, D_MODEL), lambda i, k: (_seq_row(i, tm), 5, 0, 0)),
            pl.BlockSpec((None, D_MODEL, th), lambda i, k: (layer, 0, k)),
            pl.BlockSpec((None, D_MODEL, th), lambda i, k: (layer, 0, k)),
            pl.BlockSpec((None, th, D_MODEL), lambda i, k: (layer, k, 0)),
            pl.BlockSpec((1, D_MODEL), lambda i, k: (0, 0)),
        ],
        out_specs=out_specs,
        scratch_shapes=[pltpu.VMEM((tm, D_MODEL), BF16), pltpu.VMEM((tm, D_MODEL), F32)],
        compiler_params=pltpu.CompilerParams(
            dimension_semantics=("arbitrary", "arbitrary"), vmem_limit_bytes=VMEM_LIMIT),
        name="ffn",
    )(x, nw, mod, mod, mod, w1, w3, w2, fw)


def _axial_rope(l):
    n_rows = l // GRID_W
    t_row = jnp.repeat(jnp.arange(n_rows, dtype=F32), GRID_W)
    t_col = jnp.tile(jnp.arange(GRID_W, dtype=F32), n_rows)
    n_freq = HEAD_DIM // 4
    inv = ROPE_BASE ** (-jnp.arange(n_freq, dtype=F32) / n_freq)
    ang = jnp.concatenate([t_row[:, None] * inv, t_col[:, None] * inv], axis=-1)
    return jnp.cos(ang), jnp.sin(ang)


MIX_CHUNK = 128
DIRS = 2
NT_DIMS = (((1,), (1,)), ((), ()))
TN_DIMS = (((0,), (0,)), ((), ()))


def _dot(a, b):
    return jnp.dot(a.astype(BF16), b.astype(BF16), preferred_element_type=F32)


def _dot_nt(a, b):
    return lax.dot_general(a.astype(BF16), b.astype(BF16), NT_DIMS, preferred_element_type=F32)


def _dot_tn(a, b):
    return lax.dot_general(a.astype(BF16), b.astype(BF16), TN_DIMS, preferred_element_type=F32)


def _silu(x):
    return x * jax.nn.sigmoid(x)


def _seq_spec(seq_len, width, row_off, col_blk):
    return pl.BlockSpec((seq_len, width), lambda b: (row_off + b, col_blk))


def _mix_params(n_par):
    return pltpu.CompilerParams(dimension_semantics=("parallel",) * n_par, vmem_limit_bytes=VMEM_LIMIT)


def _ret_body(*refs, n_chunks, use_rope, has_s0):
    it = iter(refs)
    lg_ref, q_ref, k_ref, v_ref, g_ref = (next(it) for _ in range(5))
    cos_ref, sin_ref = (next(it), next(it)) if use_rope else (None, None)
    s0_ref = next(it) if has_s0 else None
    o_ref, sf_ref, acc_ref, s_ref, intra_ref, qd_ref, kd_ref, cd_ref = (next(it) for _ in range(8))
    C, H = MIX_CHUNK, N_HEADS

    row = lax.broadcasted_iota(jnp.int32, (C, C), 0)
    col = lax.broadcasted_iota(jnp.int32, (C, C), 1)
    rel = (row - col).astype(F32)
    pos = lax.broadcasted_iota(jnp.int32, (C, HEAD_DIM), 0).astype(F32)
    for d in range(DIRS):
        for h in range(H):
            lg = lg_ref[d, h]
            if d == 0:
                intra_ref[d, h] = jnp.where(rel >= 0, jnp.exp(jnp.maximum(rel, 0.0) * lg), 0.0)
                qd_ref[d, h] = jnp.exp((pos + 1.0) * lg)
                kd_ref[d, h] = jnp.exp((C - 1.0 - pos) * lg)
            else:
                intra_ref[d, h] = jnp.where(rel <= 0, jnp.exp(jnp.maximum(-rel, 0.0) * lg), 0.0)
                qd_ref[d, h] = jnp.exp((C - pos) * lg)
                kd_ref[d, h] = jnp.exp(pos * lg)
            cd_ref[d, h] = jnp.exp(jnp.full((C, HEAD_DIM), C, F32) * lg)
    if has_s0:
        s_ref[...] = s0_ref[...]
    else:
        s_ref[...] = jnp.zeros_like(s_ref)
    acc_ref[...] = jnp.zeros_like(acc_ref)

    units = [(d, h) for d in range(DIRS) for h in range(H)]

    def chunk_step(n, carry):
        rows_d = [pl.ds(pl.multiple_of(c * C, C), C) for c in (n, n_chunks - 1 - n)]
        q_l, k_l, v_l, sc_l = [], [], [], []
        for d, h in units:
            rows, cols = rows_d[d], slice(h * HEAD_DIM, (h + 1) * HEAD_DIM)
            q, k = q_ref[rows, cols], k_ref[rows, cols]
            if use_rope:
                cs, sn = cos_ref[rows, :], sin_ref[rows, :]
                q = q * cs + pltpu.roll(q, HEAD_DIM // 2, 1) * sn
                k = k * cs + pltpu.roll(k, HEAD_DIM // 2, 1) * sn
            k = k * HEAD_DIM ** -0.5
            q_l.append(q)
            k_l.append(k)
            v_l.append(v_ref[rows, cols].astype(BF16))
            sc_l.append(_dot_nt(q, k) * intra_ref[d, h])
        for i, (d, h) in enumerate(units):
            cols = slice(h * HEAD_DIM, (h + 1) * HEAD_DIM)
            s = s_ref[d, h]
            acc_ref[rows_d[d], cols] += _dot(jnp.concatenate([sc_l[i], q_l[i] * qd_ref[d, h]], axis=1),
                                             jnp.concatenate([v_l[i], s.astype(BF16)], axis=0))
            s_ref[d, h] = cd_ref[d, h] * s + _dot_tn(k_l[i] * kd_ref[d, h], v_l[i])
        return carry

    lax.fori_loop(0, n_chunks, chunk_step, 0)
    sf_ref[...] = s_ref[...]

    def finish(n, carry):
        rows = pl.ds(pl.multiple_of(n * C, C), C)
        for h in range(H):
            cols = slice(h * HEAD_DIM, (h + 1) * HEAD_DIM)
            o = acc_ref[rows, cols]
            mu = jnp.mean(o, axis=-1, keepdims=True)
            oc = o - mu
            y = oc * lax.rsqrt(jnp.mean(oc * oc, axis=-1, keepdims=True) + EPS)
            o_ref[rows, cols] = (y * _silu(g_ref[rows, cols])).astype(BF16)
        return carry

    lax.fori_loop(0, n_chunks, finish, 0)


def _retention_pallas(pm, log_gamma, n_seq, seq_len, row_off, rope2, s0, layer):
    use_rope, has_s0 = rope2 is not None, s0 is not None
    st_shape = (DIRS, N_HEADS, HEAD_DIM, HEAD_DIM)
    in_specs = [pl.BlockSpec(memory_space=pltpu.SMEM)]
    in_specs += [_seq_spec(seq_len, GROUP_WIDTH, row_off, cb) for cb in range(4)]
    args = [log_gamma, pm, pm, pm, pm]
    if use_rope:
        in_specs += [pl.BlockSpec((seq_len, HEAD_DIM), lambda b: (0, 0))] * 2
        args += list(rope2)
    if has_s0:
        in_specs.append(pl.BlockSpec((None, None) + st_shape, lambda b: (b, layer, 0, 0, 0, 0)))
        args.append(s0)
    return pl.pallas_call(
        functools.partial(_ret_body, n_chunks=seq_len // MIX_CHUNK, use_rope=use_rope, has_s0=has_s0),
        out_shape=(jax.ShapeDtypeStruct((n_seq * seq_len, GROUP_WIDTH), BF16),
                   jax.ShapeDtypeStruct((n_seq,) + st_shape, F32)),
        grid=(n_seq,),
        in_specs=in_specs,
        out_specs=(pl.BlockSpec((seq_len, GROUP_WIDTH), lambda b: (b, 0)),
                   pl.BlockSpec((None,) + st_shape, lambda b: (b, 0, 0, 0, 0))),
        scratch_shapes=[pltpu.VMEM((seq_len, GROUP_WIDTH), F32), pltpu.VMEM(st_shape, F32),
                        pltpu.VMEM((DIRS, N_HEADS, MIX_CHUNK, MIX_CHUNK), F32),
                        pltpu.VMEM((DIRS, N_HEADS, MIX_CHUNK, HEAD_DIM), F32),
                        pltpu.VMEM((DIRS, N_HEADS, MIX_CHUNK, HEAD_DIM), F32),
                        pltpu.VMEM((DIRS, N_HEADS, MIX_CHUNK, HEAD_DIM), F32)],
        compiler_params=_mix_params(1),
        name="retention",
    )(*args)


def _rope_tables(l):
    cos, sin = _axial_rope(l)
    return jnp.concatenate([cos, cos], axis=-1), jnp.concatenate([-sin, sin], axis=-1)


S5_HALF_G = S5_GROUPS // 2
S5_HALF_U = S5_HALF_G * S5_CH
S5_HALF_X = S5_HALF_G * S5_N
S5_X = S5_GROUPS * S5_N
S5_TC = 512
S5_BLK = 8
S5_TABS = 4


def _s5_tables(a_re, a_im, log_step, b_re, b_im, c_re, c_im):
    dt = jnp.exp(log_step)[..., None]
    mag = jnp.exp(a_re * dt)
    ab_re = mag * jnp.cos(a_im * dt)
    ab_im = mag * jnp.sin(a_im * dt)
    den = a_re * a_re + a_im * a_im
    nr = ab_re - 1.0
    f_re = (nr * a_re + ab_im * a_im) / den
    f_im = (ab_im * a_re - nr * a_im) / den
    bb_re = f_re[..., None] * b_re - f_im[..., None] * b_im
    bb_im = f_re[..., None] * b_im + f_im[..., None] * b_re
    eye = jnp.eye(S5_HALF_G, dtype=F32)

    def in_mat(bb):
        bb = bb.reshape(DIRS, 2, S5_HALF_G, S5_N, S5_CH)
        return jnp.einsum('dhgnc,gk->dhgckn', bb, eye).reshape(DIRS, 2, S5_HALF_U, S5_HALF_X)

    def out_mat(cc):
        cc = cc.reshape(DIRS, 2, S5_HALF_G, S5_CH, S5_N)
        return jnp.einsum('dhgcn,gk->dhgnkc', cc, eye).reshape(DIRS, 2, S5_HALF_X, S5_HALF_U)

    bm = jnp.concatenate([in_mat(bb_re), in_mat(bb_im)], axis=-1).astype(BF16)
    cm = jnp.concatenate([out_mat(c_re), -out_mat(c_im)], axis=-2).astype(BF16)
    t = jnp.arange(S5_BLK, dtype=F32)
    order = jnp.stack([t, S5_BLK - 1.0 - t])
    shifts = 2.0 ** jnp.arange(S5_TABS - 1, dtype=F32)
    expo = jnp.concatenate([jnp.where(order[:, None, :] >= shifts[None, :, None], shifts[None, :, None], jnp.nan),
                            order[:, None, :] + 1.0], axis=1)
    live = ~jnp.isnan(expo)
    e = jnp.where(live, expo, 0.0)[..., None]
    adt_re = (a_re * dt).reshape(DIRS, 1, 1, S5_X)
    adt_im = (a_im * dt).reshape(DIRS, 1, 1, S5_X)
    pmag = jnp.where(live[..., None], jnp.exp(e * adt_re), 0.0)
    pw_re = pmag * jnp.cos(e * adt_im)
    pw_im = pmag * jnp.sin(e * adt_im)
    return bm, cm, pw_re, pw_im


def _gelu_tanh(x):
    return 0.5 * x * (1.0 + jnp.tanh(math.sqrt(2.0 / math.pi) * (x + 0.044715 * (x * x * x))))


def _s5_body(*refs, seq_len, has_s0):
    it = iter(refs)
    u_ref, bm_ref, cm_ref, pwr_ref, pwi_ref, d_ref, gw_ref, gb_ref = (next(it) for _ in range(8))
    x0r_ref, x0i_ref = (next(it), next(it)) if has_s0 else (None, None)
    o_ref, sfr_ref, sfi_ref, y_ref, xr_ref, xi_ref, xb_ref = (next(it) for _ in range(7))
    tc = min(seq_len, S5_TC)
    n_tiles = seq_len // tc
    n_pair = tc // (2 * S5_BLK)

    y_ref[...] = d_ref[...] * u_ref[...]
    for d in range(DIRS):
        last = S5_BLK - 1 if d == 0 else 0
        for hf in range(2):
            xs = slice(hf * S5_HALF_X, (hf + 1) * S5_HALF_X)
            us = slice(hf * S5_HALF_U, (hf + 1) * S5_HALF_U)

            def scan_block(x_in, carry, d=d, xs=xs, last=last):
                xr, xi = x_in
                car_re, car_im = carry
                for k in range(S5_TABS - 1):
                    s = 1 << k
                    shift = s if d == 0 else S5_BLK - s
                    p_re, p_im = pwr_ref[d, k, :, xs], pwi_ref[d, k, :, xs]
                    sr, si = pltpu.roll(xr, shift, 0), pltpu.roll(xi, shift, 0)
                    xr, xi = xr + p_re * sr - p_im * si, xi + p_re * si + p_im * sr
                p_re, p_im = pwr_ref[d, S5_TABS - 1, :, xs], pwi_ref[d, S5_TABS - 1, :, xs]
                xr, xi = xr + p_re * car_re - p_im * car_im, xi + p_re * car_im + p_im * car_re
                return (xr, xi), (xr[last:last + 1, :], xi[last:last + 1, :])

            def scan_pair(j, carry, d=d, scan_block=scan_block):
                pair = j if d == 0 else n_pair - 1 - j
                rows = pl.ds(pl.multiple_of(pair * 2 * S5_BLK, 2 * S5_BLK), 2 * S5_BLK)
                xr2, xi2 = xr_ref[rows, :], xi_ref[rows, :]
                halves = [(xr2[:S5_BLK], xi2[:S5_BLK]), (xr2[S5_BLK:], xi2[S5_BLK:])]
                order = (0, 1) if d == 0 else (1, 0)
                out = [None, None]
                for idx in order:
                    out[idx], carry = scan_block(halves[idx], carry)
                xb_ref[rows, :S5_HALF_X] = jnp.concatenate([out[0][0], out[1][0]], axis=0).astype(BF16)
                xb_ref[rows, S5_HALF_X:] = jnp.concatenate([out[0][1], out[1][1]], axis=0).astype(BF16)
                return carry

            def scan_tile(i, carry, d=d, hf=hf, us=us, scan_pair=scan_pair):
                tile = i if d == 0 else n_tiles - 1 - i
                rows = pl.ds(pl.multiple_of(tile * tc, tc), tc)
                bu = _dot(u_ref[rows, us], bm_ref[d, hf])
                xr_ref[...] = bu[:, :S5_HALF_X]
                xi_ref[...] = bu[:, S5_HALF_X:]
                carry = lax.fori_loop(0, n_pair, scan_pair, carry)
                y_ref[rows, us] += jnp.dot(xb_ref[...], cm_ref[d, hf], preferred_element_type=F32)
                return carry

            if has_s0:
                carry0 = (x0r_ref[d, :, xs], x0i_ref[d, :, xs])
            else:
                carry0 = (jnp.zeros((1, S5_HALF_X), F32), jnp.zeros((1, S5_HALF_X), F32))
            car_re, car_im = lax.fori_loop(0, n_tiles, scan_tile, carry0)
            sfr_ref[d, :, xs] = car_re
            sfi_ref[d, :, xs] = car_im

    z = _gelu_tanh(y_ref[...])
    o_ref[...] = (z * jax.nn.sigmoid(_dot(z, gw_ref[...]) + gb_ref[...])).astype(BF16)


def _s5_pallas(pm, tables, s5_d, glu_w, glu_b, n_seq, seq_len, row_off, x0, layer):
    bm, cm, pw_re, pw_im = tables
    has_s0 = x0 is not None
    full = lambda shape: pl.BlockSpec(shape, lambda b: (0,) * len(shape))
    in_specs = [_seq_spec(seq_len, GROUP_WIDTH, row_off, 13),
                full(bm.shape), full(cm.shape), full(pw_re.shape), full(pw_im.shape),
                full((1, GROUP_WIDTH)), full((GROUP_WIDTH, GROUP_WIDTH)), full((1, GROUP_WIDTH))]
    args = [pm, bm, cm, pw_re, pw_im, s5_d.reshape(1, GROUP_WIDTH), glu_w.astype(BF16),
            glu_b.reshape(1, GROUP_WIDTH)]
    if has_s0:
        in_specs += [pl.BlockSpec((None, None, DIRS, 1, S5_X), lambda b: (b, layer, 0, 0, 0))] * 2
        args += list(x0)
    st = jax.ShapeDtypeStruct((n_seq, DIRS, 1, S5_X), F32)
    st_spec = pl.BlockSpec((None, DIRS, 1, S5_X), lambda b: (b, 0, 0, 0))
    tc = min(seq_len, S5_TC)
    return pl.pallas_call(
        functools.partial(_s5_body, seq_len=seq_len, has_s0=has_s0),
        out_shape=(jax.ShapeDtypeStruct((n_seq * seq_len, GROUP_WIDTH), BF16), st, st),
        grid=(n_seq,),
        in_specs=in_specs,
        out_specs=(pl.BlockSpec((seq_len, GROUP_WIDTH), lambda b: (b, 0)), st_spec, st_spec),
        scratch_shapes=[pltpu.VMEM((seq_len, GROUP_WIDTH), F32),
                        pltpu.VMEM((tc, S5_HALF_X), F32), pltpu.VMEM((tc, S5_HALF_X), F32),
                        pltpu.VMEM((tc, 2 * S5_HALF_X), BF16)],
        compiler_params=_mix_params(1),
        name="s5",
    )(*args)


GLA_LEVELS = 7


def _chunk_cumsum_rows(x, rowi):
    s = 1
    while s < MIX_CHUNK:
        x = x + jnp.where(rowi >= s, pltpu.roll(x, s, 0), 0.0)
        s *= 2
    return x


def _gla_body(*refs, n_chunks, has_s0):
    it = iter(refs)
    q_ref, f0_ref, f1_ref, i_ref, g_ref, lb_ref, nw_ref = (next(it) for _ in range(7))
    s0_ref = next(it) if has_s0 else None
    o_ref, sf_ref, acc_ref, s_ref, code_ref = (next(it) for _ in range(5))
    C, H = MIX_CHUNK, N_HEADS
    f_refs = (f0_ref, f1_ref)

    rowi = lax.broadcasted_iota(jnp.int32, (C, HEAD_DIM), 0)
    ri = lax.broadcasted_iota(jnp.int32, (C, C), 0)
    ci = lax.broadcasted_iota(jnp.int32, (C, C), 1)
    top_bit = 31 - lax.clz(ri ^ ci)
    code_ref[...] = jnp.where(ri > ci, top_bit, jnp.where(ri < ci, -1 - top_bit, GLA_LEVELS))
    if has_s0:
        s_ref[...] = s0_ref[...]
    else:
        s_ref[...] = jnp.zeros_like(s_ref)
    acc_ref[...] = jnp.zeros_like(acc_ref)

    def chunk_step(n, carry):
        for d in range(DIRS):
            c = n if d == 0 else n_chunks - 1 - n
            rows = pl.ds(pl.multiple_of(c * C, C), C)
            for h in range(H):
                cols = slice(h * HEAD_DIM, (h + 1) * HEAD_DIM)
                code = code_ref[...]
                q = _silu(q_ref[rows, cols]) * HEAD_DIM ** -0.5
                v = i_ref[rows, cols].astype(BF16)
                fx = f_refs[d][rows, cols]
                lb = lb_ref[d:d + 1, cols]
                sig = 1.0 / (1.0 + jnp.exp(-fx))
                logf = jnp.log2(jnp.maximum(lb, LB_FLOOR) + (1.0 - lb) * sig)
                k = (1.0 - lb) * (1.0 - sig)
                cum = _chunk_cumsum_rows(logf, rowi)
                own = cum
                attn = jnp.where(code == GLA_LEVELS, _dot_nt(q, k), 0.0)
                for lvl in range(GLA_LEVELS):
                    m = 1 << lvl
                    prev = pltpu.roll(own, m, 0)
                    pre = jnp.minimum(cum - prev, 0.0)
                    suf = own - cum
                    if d == 0:
                        sc = _dot_nt(q * jnp.exp2(pre), k * jnp.exp2(suf))
                        hit = code == lvl
                    else:
                        sc = _dot_nt(q * jnp.exp2(suf + logf), k * jnp.exp2(jnp.minimum(pre - logf, 0.0)))
                        hit = code == -1 - lvl
                    attn = jnp.where(hit, sc, attn)
                    own = jnp.where(((rowi >> lvl) & 1) == 0, pltpu.roll(own, C - m, 0), own)
                tot = own
                st = s_ref[d, h]
                if d == 0:
                    q_in, k_out = q * jnp.exp2(cum), k * jnp.exp2(tot - cum)
                else:
                    q_in, k_out = q * jnp.exp2(tot - cum + logf), k * jnp.exp2(cum - logf)
                acc_ref[rows, cols] += _dot(attn, v) + _dot_nt(q_in, st)
                s_ref[d, h] = jnp.exp2(tot[0:1, :]) * st + _dot_tn(v, k_out)
        return carry

    lax.fori_loop(0, n_chunks, chunk_step, 0)
    sf_ref[...] = s_ref[...]

    def finish(n, carry):
        rows = pl.ds(pl.multiple_of(n * C, C), C)
        for h in range(H):
            cols = slice(h * HEAD_DIM, (h + 1) * HEAD_DIM)
            o = acc_ref[rows, cols]
            y = o * lax.rsqrt(jnp.mean(o * o, axis=-1, keepdims=True) + EPS) * nw_ref[...]
            o_ref[rows, cols] = (y * _silu(g_ref[rows, cols])).astype(BF16)
        return carry

    lax.fori_loop(0, n_chunks, finish, 0)


def _gla_pallas(pm, lower_bound, norm_w, n_seq, seq_len, row_off, s0_t, layer):
    has_s0 = s0_t is not None
    st_shape = (DIRS, N_HEADS, HEAD_DIM, HEAD_DIM)
    in_specs = [_seq_spec(seq_len, GROUP_WIDTH, row_off, cb) for cb in (8, 9, 10, 11, 12)]
    in_specs += [pl.BlockSpec((DIRS, GROUP_WIDTH), lambda b: (0, 0)), pl.BlockSpec((1, HEAD_DIM), lambda b: (0, 0))]
    args = [pm] * 5 + [lower_bound, norm_w.reshape(1, HEAD_DIM)]
    if has_s0:
        in_specs.append(pl.BlockSpec((None, None) + st_shape, lambda b: (b, layer, 0, 0, 0, 0)))
        args.append(s0_t)
    return pl.pallas_call(
        functools.partial(_gla_body, n_chunks=seq_len // MIX_CHUNK, has_s0=has_s0),
        out_shape=(jax.ShapeDtypeStruct((n_seq * seq_len, GROUP_WIDTH), BF16),
                   jax.ShapeDtypeStruct((n_seq,) + st_shape, F32)),
        grid=(n_seq,),
        in_specs=in_specs,
        out_specs=(pl.BlockSpec((seq_len, GROUP_WIDTH), lambda b: (b, 0)),
                   pl.BlockSpec((None,) + st_shape, lambda b: (b, 0, 0, 0, 0))),
        scratch_shapes=[pltpu.VMEM((seq_len, GROUP_WIDTH), F32), pltpu.VMEM(st_shape, F32),
                        pltpu.VMEM((MIX_CHUNK, MIX_CHUNK), jnp.int32)],
        compiler_params=_mix_params(1),
        name="hgrn2",
    )(*args)


GDN_BLOCK_BITS = 4
GDN_MERGES = 3
N_GATES = DIRS * N_HEADS


def _softplus(x):
    return jnp.maximum(x, 0.0) + jnp.log1p(jnp.exp(-jnp.abs(x)))


def _gdn_body(*refs, seq_len, has_s0):
    it = iter(refs)
    (q_ref, k_ref, v_ref, g_ref, ps_ref, pst_ref, cw_ref, prow_ref, pcol_ref, nw_ref) = (next(it) for _ in range(10))
    s0_ref = next(it) if has_s0 else None
    (o_ref, sf_ref, acc_ref, s_ref, qn_ref, kn_ref, vn_ref, gate_ref, gatet_ref, blev_ref) = (
        next(it) for _ in range(10))
    C, H, L = MIX_CHUNK, N_HEADS, seq_len
    n_chunks = L // C
    w = GROUP_WIDTH

    rowl = lax.broadcasted_iota(jnp.int32, (L, HEAD_DIM), 0)
    for part, (src, dst) in enumerate(((q_ref, qn_ref), (k_ref, kn_ref), (v_ref, vn_ref))):
        for h in range(H):
            cols = slice(h * HEAD_DIM, (h + 1) * HEAD_DIM)
            wc = slice(part * w + h * HEAD_DIM, part * w + (h + 1) * HEAD_DIM)
            x = src[:, cols]
            x_prev = jnp.where(rowl >= 1, pltpu.roll(x, 1, 0), 0.0)
            x_next = jnp.where(rowl < L - 1, pltpu.roll(x, L - 1, 0), 0.0)
            y = _silu(cw_ref[0:1, wc] * x_prev + cw_ref[1:2, wc] * x + cw_ref[2:3, wc] * x_next)
            if part < 2:
                y = y * lax.rsqrt(jnp.sum(y * y, axis=-1, keepdims=True) + EPS)
            if part == 0:
                y = y * HEAD_DIM ** -0.5
            dst[:, cols] = y

    rowi = lax.broadcasted_iota(jnp.int32, (C, HEAD_DIM), 0)
    lane = lax.broadcasted_iota(jnp.int32, (C, HEAD_DIM), 1)
    lane_t = lax.broadcasted_iota(jnp.int32, (2 * N_GATES, C), 1)
    row_t = lax.broadcasted_iota(jnp.int32, (2 * N_GATES, C), 0)
    for c in range(n_chunks):
        rows = slice(c * C, (c + 1) * C)
        a = ps_ref[rows, :]
        la = -jnp.exp(prow_ref[0:1, :]) * _softplus(a + prow_ref[1:2, :])
        pre = _chunk_cumsum_rows(la, rowi)
        tot = jnp.broadcast_to(pre[C - 1:C, :], (C, HEAD_DIM))
        g = jnp.where(lane < N_HEADS, pre, tot - pre + la)
        gate_ref[rows, :] = jnp.where(lane < N_GATES, g,
                                      jnp.where(lane < 2 * N_GATES, jax.nn.sigmoid(a),
                                                pltpu.roll(tot, 2 * N_GATES, 1)))
        at = pst_ref[:, rows]
        lat = -jnp.exp(pcol_ref[:, 0:1]) * _softplus(at + pcol_ref[:, 1:2])
        pre_t, s = lat, 1
        while s < C:
            pre_t = pre_t + jnp.where(lane_t >= s, pltpu.roll(pre_t, s, 1), 0.0)
            s *= 2
        tot_t = jnp.broadcast_to(pre_t[:, C - 1:C], (2 * N_GATES, C))
        g_t = jnp.where(row_t < N_HEADS, pre_t, tot_t - pre_t + lat)
        gatet_ref[:, rows] = jnp.where(row_t < N_GATES, g_t, jax.nn.sigmoid(at))

    ri = lax.broadcasted_iota(jnp.int32, (C, C), 0)
    ci = lax.broadcasted_iota(jnp.int32, (C, C), 1)
    bx = (ri >> GDN_BLOCK_BITS) ^ (ci >> GDN_BLOCK_BITS)
    blev_ref[...] = jnp.where(bx == 0, 0, 32 - lax.clz(bx))
    if has_s0:
        s_ref[...] = s0_ref[...]
    else:
        s_ref[...] = jnp.zeros_like(s_ref)
    acc_ref[...] = jnp.zeros_like(acc_ref)

    units = [(d, h) for d in range(DIRS) for h in range(H)]

    def chunk_step(n, carry):
        eye = jnp.where(ri == ci, 1.0, 0.0)
        blev = blev_ref[...]
        rows_d = [pl.ds(pl.multiple_of(c * C, C), C) for c in (n, n_chunks - 1 - n)]
        incl_d = [ri >= ci, ri <= ci]
        strict_d = [ri > ci, ri < ci]
        q_l, k_l, eg_l, et_l, ek_l, rhs_l, a_l, attn_l = ([] for _ in range(8))
        for d, h in units:
            rows, cols, cg = rows_d[d], slice(h * HEAD_DIM, (h + 1) * HEAD_DIM), d * N_HEADS + h
            q, k, v = qn_ref[rows, cols], kn_ref[rows, cols], vn_ref[rows, cols]
            g_i = jnp.broadcast_to(gate_ref[rows, cg:cg + 1], (C, C))
            b_i = jnp.broadcast_to(gate_ref[rows, N_GATES + cg:N_GATES + cg + 1], (C, C))
            tot = jnp.broadcast_to(gate_ref[rows, 2 * N_GATES + cg:2 * N_GATES + cg + 1], (C, C))
            g_j = gatet_ref[cg:cg + 1, rows]
            decay = jnp.where(incl_d[d], jnp.exp(jnp.minimum(g_i - g_j, 0.0)), 0.0)
            e_g = jnp.exp(g_i)
            kb = k.astype(BF16)
            a_l.append(jnp.where(strict_d[d], _dot_nt(kb, kb) * b_i * decay, 0.0))
            attn_l.append(_dot_nt(q, kb) * decay)
            rhs_l.append(jnp.concatenate([v * b_i, k * (b_i * e_g)], axis=1).astype(BF16))
            q_l.append(q * e_g)
            ek_l.append(k * jnp.exp(tot - g_i))
            et_l.append(jnp.exp(tot))
        b_l = [-jnp.where(blev == 0, a, 0.0) for a in a_l]
        p_l = [eye + b for b in b_l]
        for _ in range(GDN_BLOCK_BITS - 1):
            b_l = [_dot(b, b) for b in b_l]
            p_l = [p + _dot(p, b) for p, b in zip(p_l, b_l)]
        for lvl in range(1, GDN_MERGES + 1):
            ep_l = [_dot(jnp.where(blev == lvl, a, 0.0), p) for a, p in zip(a_l, p_l)]
            p_l = [p - _dot(p, ep) for p, ep in zip(p_l, ep_l)]
        uw_l = [_dot(p, rhs) for p, rhs in zip(p_l, rhs_l)]
        s_l = [s_ref[d, h] for d, h in units]
        vn_l = [uw[:, :HEAD_DIM] - _dot(uw[:, HEAD_DIM:], s) for uw, s in zip(uw_l, s_l)]
        for i, (d, h) in enumerate(units):
            cols = slice(h * HEAD_DIM, (h + 1) * HEAD_DIM)
            acc_ref[rows_d[d], cols] += _dot(jnp.concatenate([q_l[i], attn_l[i]], axis=1),
                                             jnp.concatenate([s_l[i], vn_l[i]], axis=0))
            s_ref[d, h] = s_l[i] * et_l[i] + _dot_tn(ek_l[i], vn_l[i])
        return carry

    lax.fori_loop(0, n_chunks, chunk_step, 0)
    sf_ref[...] = s_ref[...]

    def finish(n, carry):
        rows = pl.ds(pl.multiple_of(n * C, C), C)
        for h in range(H):
            cols = slice(h * HEAD_DIM, (h + 1) * HEAD_DIM)
            o = acc_ref[rows, cols]
            y = o * lax.rsqrt(jnp.mean(o * o, axis=-1, keepdims=True) + EPS) * nw_ref[...]
            o_ref[rows, cols] = (y * _silu(g_ref[rows, cols])).astype(BF16)
        return carry

    lax.fori_loop(0, n_chunks, finish, 0)


def _gdn_pallas(pm, ps, ps_t, conv_w, a_log, dt_bias, norm_w, n_seq, seq_len, row_off, s0, layer):
    has_s0 = s0 is not None
    st_shape = (DIRS, N_HEADS, HEAD_DIM, HEAD_DIM)
    par = jnp.stack([a_log.reshape(N_GATES), dt_bias.reshape(N_GATES)])
    par_row = jnp.pad(par, ((0, 0), (0, HEAD_DIM - N_GATES)))
    par_col = jnp.pad(par.T, ((0, N_GATES), (0, 0)))
    full = lambda shape: pl.BlockSpec(shape, lambda b: (0,) * len(shape))
    in_specs = [_seq_spec(seq_len, GROUP_WIDTH, row_off, cb) for cb in (4, 5, 6, 7)]
    in_specs += [_seq_spec(seq_len, PROJ_SMALL, row_off, 0),
                 pl.BlockSpec((2 * N_GATES, seq_len), lambda b: (0, row_off + b)),
                 full((GDN_CONV, 3 * GROUP_WIDTH)), full((2, HEAD_DIM)), full((2 * N_GATES, 2)), full((1, HEAD_DIM))]
    args = [pm] * 4 + [ps, ps_t, conv_w, par_row, par_col, norm_w.reshape(1, HEAD_DIM)]
    if has_s0:
        in_specs.append(pl.BlockSpec((None, None) + st_shape, lambda b: (b, layer, 0, 0, 0, 0)))
        args.append(s0)
    seq_f32 = lambda width: pltpu.VMEM((seq_len, width), F32)
    return pl.pallas_call(
        functools.partial(_gdn_body, seq_len=seq_len, has_s0=has_s0),
        out_shape=(jax.ShapeDtypeStruct((n_seq * seq_len, GROUP_WIDTH), BF16),
                   jax.ShapeDtypeStruct((n_seq,) + st_shape, F32)),
        grid=(n_seq,),
        in_specs=in_specs,
        out_specs=(pl.BlockSpec((seq_len, GROUP_WIDTH), lambda b: (b, 0)),
                   pl.BlockSpec((None,) + st_shape, lambda b: (b, 0, 0, 0, 0))),
        scratch_shapes=[seq_f32(GROUP_WIDTH), pltpu.VMEM(st_shape, F32),
                        seq_f32(GROUP_WIDTH), seq_f32(GROUP_WIDTH), seq_f32(GROUP_WIDTH),
                        seq_f32(HEAD_DIM), pltpu.VMEM((2 * N_GATES, seq_len), F32),
                        pltpu.VMEM((MIX_CHUNK, MIX_CHUNK), jnp.int32)],
        compiler_params=_mix_params(1),
        name="gated_delta",
    )(*args)


def kernel(x_prompt, x_sample, state_ret, state_gdn, state_hgrn, state_s5_re, state_s5_im, c, c_ctx, norm1_w, norm2_w, final_norm_w, ada_w, ada_b, in_proj, out_proj, ret_decay_logit, gdn_conv, gdn_a_log, gdn_dt_bias, gdn_norm_w, hg_lb_param, hg_norm_w, s5_a_re, s5_a_im, s5_b_re, s5_b_im, s5_c_re, s5_c_im, s5_log_step, s5_d, s5_glu_w, s5_glu_b, ffn_w1, ffn_w3, ffn_w2):
    lb_soft = jax.nn.softmax(hg_lb_param, axis=0)
    lower_bounds = jnp.cumsum(lb_soft, axis=0) - lb_soft[0]
    rope2 = _rope_tables(DEC_SEQ)

    cvec = jnp.zeros((N_SEQ_ROWS, D_MODEL), F32).at[0].set(c_ctx).at[1:1 + DEC_BATCH].set(c)
    mod_all = _ada(cvec, ada_w, ada_b).reshape(DEPTH, N_SEQ_ROWS, N_MOD, 1, D_MODEL)

    gate0 = 8 * GROUP_WIDTH
    w_in = in_proj.astype(BF16)
    w_main = jnp.concatenate([w_in[:, :, :gate0], w_in[:, :, gate0 + 2 * N_GATES:]], axis=-1)
    w_small = jnp.pad(w_in[:, :, gate0:gate0 + 2 * N_GATES], ((0, 0), (0, 0), (0, PROJ_SMALL - 2 * N_GATES)))
    w_out = out_proj.astype(BF16)
    w1, w3, w2 = ffn_w1.astype(BF16), ffn_w3.astype(BF16), ffn_w2.astype(BF16)

    x = jnp.concatenate([x_prompt.reshape(N_CTX_TOK, D_MODEL), x_sample.reshape(N_LAT_TOK, D_MODEL)], axis=0)
    hgrn_t = jnp.swapaxes(state_hgrn, -1, -2)
    s5_x0 = (state_s5_re.reshape(DEC_BATCH, DEPTH, DIRS, 1, S5_X), state_s5_im.reshape(DEC_BATCH, DEPTH, DIRS, 1, S5_X))
    lat_off = N_CTX_TOK // DEC_SEQ
    ctx_states = []
    for i in range(DEPTH):
        mod = mod_all[i]
        pm, ps = _inproj(x, norm1_w[i][None], mod, w_main, w_small, i)
        ps_t = ps[:, :2 * N_GATES].T
        lg = jax.nn.log_sigmoid(ret_decay_logit[i])
        tables = _s5_tables(s5_a_re[i], s5_a_im[i], s5_log_step[i], s5_b_re[i], s5_b_im[i], s5_c_re[i], s5_c_im[i])
        ctx = (BATCH, SEQ, 0)
        lat = (DEC_BATCH, DEC_SEQ, lat_off)

        ret_c, rs = _retention_pallas(pm, lg, *ctx, None, None, i)
        ret_l, _ = _retention_pallas(pm, lg, *lat, rope2, state_ret, i)
        gdn_args = (pm, ps, ps_t, gdn_conv[i], gdn_a_log[i], gdn_dt_bias[i], gdn_norm_w[i])
        gdn_c, gs = _gdn_pallas(*gdn_args, *ctx, None, i)
        gdn_l, _ = _gdn_pallas(*gdn_args, *lat, state_gdn, i)
        hg_c, hs = _gla_pallas(pm, lower_bounds[i], hg_norm_w[i], *ctx, None, i)
        hg_l, _ = _gla_pallas(pm, lower_bounds[i], hg_norm_w[i], *lat, hgrn_t, i)
        s5_args = (pm, tables, s5_d[i], s5_glu_w[i], s5_glu_b[i])
        s5_c, xr, xi = _s5_pallas(*s5_args, *ctx, None, i)
        s5_l, _, _ = _s5_pallas(*s5_args, *lat, s5_x0, i)
        ctx_states.append((rs, gs, jnp.swapaxes(hs, -1, -2), xr.reshape(BATCH, DIRS, S5_GROUPS, S5_N),
                           xi.reshape(BATCH, DIRS, S5_GROUPS, S5_N)))

        parts = ((ret_c, ret_l), (gdn_c, gdn_l), (hg_c, hg_l), (s5_c, s5_l))
        x = _outproj(parts, w_out, x, mod, i)
        x = _ffn(x, norm2_w[i][None], mod, w1, w3, w2, final_norm_w[None], i)

    y_prompt = x[0].reshape(BATCH, SEQ, D_MODEL)
    y_sample = x[1].reshape(DEC_BATCH, DEC_SEQ, D_MODEL)
    new_states = tuple(jnp.stack([s[j] for s in ctx_states], axis=1) for j in range(5))
    return (y_prompt, y_sample) + new_states
```

```python
import functools
import math

import jax
import jax.numpy as jnp
from jax import lax
from jax.experimental import pallas as pl
from jax.experimental.pallas import tpu as pltpu

F32 = jnp.float32
BF16 = jnp.bfloat16

D_MODEL = 2048
BATCH = 16
SEQ = 256
DEPTH = 2
DEC_BATCH = 8
DEC_SEQ = 1024
GRID_W = 64
HEAD_DIM = 128
GROUP_WIDTH = 512
N_HEADS = 4
S5_CH = 16
S5_GROUPS = 32
S5_N = 64
GDN_CONV = 3
CHUNK = 64
HG_CHUNK = 16
ROPE_BASE = 10000.0
FFN_HIDDEN = 5632
N_MOD = 6
EPS = 1e-6
LB_FLOOR = 1e-30

N_CTX_TOK = BATCH * SEQ
N_LAT_TOK = DEC_BATCH * DEC_SEQ
N_TOK = N_CTX_TOK + N_LAT_TOK
N_SEQ_ROWS = 16
PROJ_MAIN = 14 * GROUP_WIDTH
PROJ_SMALL = 128
VMEM_LIMIT = 56 * 1024 * 1024


def _seq_row(tile, tm):
    n_ctx = N_CTX_TOK // tm
    per_lat = DEC_SEQ // tm
    return jnp.where(tile < n_ctx, 0, 1 + (tile - n_ctx) // per_lat)


def _stream_specs(tm, width, n_col=1):
    n_ctx = N_CTX_TOK // tm

    def ctx_map(i, j):
        return jnp.minimum(i, n_ctx - 1), (jnp.where(i < n_ctx, j, n_col - 1) if n_col > 1 else 0)

    def lat_map(i, j):
        return jnp.maximum(i - n_ctx, 0), (jnp.where(i >= n_ctx, j, 0) if n_col > 1 else 0)

    return pl.BlockSpec((tm, width), ctx_map), pl.BlockSpec((tm, width), lat_map)


def _on_stream(tile, tm, fn):
    n_ctx = N_CTX_TOK // tm
    pl.when(tile < n_ctx)(functools.partial(fn, 0))
    pl.when(tile >= n_ctx)(functools.partial(fn, 1))


def _ada_body(c_ref, w_ref, b_ref, o_ref):
    cv = c_ref[...]
    s = cv * jax.nn.sigmoid(cv)
    o_ref[0] = jnp.dot(s.astype(BF16), w_ref[0].astype(BF16), preferred_element_type=F32) + b_ref[0]


def _ada(cvec, ada_w, ada_b):
    tn = 1024
    n = N_MOD * D_MODEL
    return pl.pallas_call(
        _ada_body,
        out_shape=jax.ShapeDtypeStruct((DEPTH, N_SEQ_ROWS, n), F32),
        grid=(DEPTH, n // tn),
        in_specs=[
            pl.BlockSpec((N_SEQ_ROWS, D_MODEL), lambda l, j: (0, 0)),
            pl.BlockSpec((1, D_MODEL, tn), lambda l, j: (l, 0, j)),
            pl.BlockSpec((1, 1, tn), lambda l, j: (l, 0, j)),
        ],
        out_specs=pl.BlockSpec((1, N_SEQ_ROWS, tn), lambda l, j: (l, 0, j)),
        compiler_params=pltpu.CompilerParams(
            dimension_semantics=("parallel", "parallel"), vmem_limit_bytes=VMEM_LIMIT),
        name="ada_mod",
    )(cvec, ada_w, ada_b.reshape(DEPTH, 1, n))


def _norm_mod(x, nw, sc, sh):
    ms = jnp.mean(x * x, axis=-1, keepdims=True)
    y = x * lax.rsqrt(ms + EPS) * nw
    return y * (1.0 + sc) + sh


PROJ_TM = 1024


def _inproj_body(x_ref, nw_ref, sc_ref, sh_ref, w_ref, ws_ref, o_ref, os_ref, h_ref):
    @pl.when(pl.program_id(1) == 0)
    def _():
        hb = _norm_mod(x_ref[...], nw_ref[...], sc_ref[...], sh_ref[...]).astype(BF16)
        h_ref[...] = hb
        os_ref[...] = jnp.dot(hb, ws_ref[...], preferred_element_type=F32)

    o_ref[...] = jnp.dot(h_ref[...], w_ref[...], preferred_element_type=F32)


def _inproj(x, nw, mod, w_main, w_small, layer):
    tm, tn = PROJ_TM, 1024
    return pl.pallas_call(
        _inproj_body,
        out_shape=(jax.ShapeDtypeStruct((N_TOK, PROJ_MAIN), F32),
                   jax.ShapeDtypeStruct((N_TOK, PROJ_SMALL), F32)),
        grid=(N_TOK // tm, PROJ_MAIN // tn),
        in_specs=[
            pl.BlockSpec((tm, D_MODEL), lambda i, j: (i, 0)),
            pl.BlockSpec((1, D_MODEL), lambda i, j: (0, 0)),
            pl.BlockSpec((None, None, 1, D_MODEL), lambda i, j: (_seq_row(i, tm), 1, 0, 0)),
            pl.BlockSpec((None, None, 1, D_MODEL), lambda i, j: (_seq_row(i, tm), 0, 0, 0)),
            pl.BlockSpec((None, D_MODEL, tn), lambda i, j: (layer, 0, j)),
            pl.BlockSpec((None, D_MODEL, PROJ_SMALL), lambda i, j: (layer, 0, 0)),
        ],
        out_specs=(pl.BlockSpec((tm, tn), lambda i, j: (i, j)),
                   pl.BlockSpec((tm, PROJ_SMALL), lambda i, j: (i, 0))),
        scratch_shapes=[pltpu.VMEM((tm, D_MODEL), BF16)],
        compiler_params=pltpu.CompilerParams(
            dimension_semantics=("parallel", "arbitrary"), vmem_limit_bytes=VMEM_LIMIT),
        name="in_proj",
    )(x, nw, mod, mod, w_main, w_small)


def _outproj_body(*refs):
    n_mix = 4
    m_refs, (w_ref, x_ref, g_ref, o_ref) = refs[:2 * n_mix], refs[2 * n_mix:]

    def compute(stream):
        acc = None
        for p in range(n_mix):
            part = jnp.dot(m_refs[2 * p + stream][...], w_ref[p * GROUP_WIDTH:(p + 1) * GROUP_WIDTH, :],
                           preferred_element_type=F32)
            acc = part if acc is None else acc + part
        o_ref[...] = x_ref[...] + g_ref[...] * acc

    _on_stream(pl.program_id(0), PROJ_TM, compute)


def _outproj(parts, w, x, mod, layer):
    tm, tn = PROJ_TM, 1024
    in_specs = []
    for _ in parts:
        in_specs += list(_stream_specs(tm, GROUP_WIDTH))
    in_specs += [pl.BlockSpec((None, D_MODEL, tn), lambda i, j: (layer, 0, j)),
                 pl.BlockSpec((tm, tn), lambda i, j: (i, j)),
                 pl.BlockSpec((None, None, 1, tn), lambda i, j: (_seq_row(i, tm), 2, 0, j))]
    return pl.pallas_call(
        _outproj_body,
        out_shape=jax.ShapeDtypeStruct((N_TOK, D_MODEL), F32),
        grid=(N_TOK // tm, D_MODEL // tn),
        in_specs=in_specs,
        out_specs=pl.BlockSpec((tm, tn), lambda i, j: (i, j)),
        compiler_params=pltpu.CompilerParams(
            dimension_semantics=("parallel", "arbitrary"), vmem_limit_bytes=VMEM_LIMIT),
        name="out_proj",
    )(*(a for pair in parts for a in pair), w, x, mod)


FFN_TM = 512


def _ffn_body(x_ref, nw_ref, sc_ref, sh_ref, g_ref, w1_ref, w3_ref, w2_ref, fw_ref, *rest, final_norm):
    o_refs, (h_ref, acc_ref) = rest[:-2], rest[-2:]
    i, k = pl.program_id(0), pl.program_id(1)

    @pl.when(k == 0)
    def _():
        h_ref[...] = _norm_mod(x_ref[...], nw_ref[...], sc_ref[...], sh_ref[...]).astype(BF16)
        acc_ref[...] = jnp.zeros_like(acc_ref)

    h = h_ref[...]
    a = jnp.dot(h, w1_ref[...], preferred_element_type=F32)
    b = jnp.dot(h, w3_ref[...], preferred_element_type=F32)
    g = (a * jax.nn.sigmoid(a) * b).astype(BF16)
    acc_ref[...] += jnp.dot(g, w2_ref[...], preferred_element_type=F32)

    @pl.when(k == pl.num_programs(1) - 1)
    def _():
        y = x_ref[...] + g_ref[...] * acc_ref[...]
        if not final_norm:
            o_refs[0][...] = y
        else:
            ms = jnp.mean(y * y, axis=-1, keepdims=True)
            y = y * lax.rsqrt(ms + EPS) * fw_ref[...]

            def write(stream):
                o_refs[stream][...] = y

            _on_stream(i, FFN_TM, write)


def _ffn(x, nw, mod, w1, w3, w2, fw, layer):
    final_norm = layer == DEPTH - 1
    tm, th = FFN_TM, 512
    if final_norm:
        out_shape = (jax.ShapeDtypeStruct((N_CTX_TOK, D_MODEL), F32), jax.ShapeDtypeStruct((N_LAT_TOK, D_MODEL), F32))
        out_specs = _stream_specs(tm, D_MODEL)
    else:
        out_shape = jax.ShapeDtypeStruct((N_TOK, D_MODEL), F32)
        out_specs = pl.BlockSpec((tm, D_MODEL), lambda i, k: (i, 0))
    return pl.pallas_call(
        functools.partial(_ffn_body, final_norm=final_norm),
        out_shape=out_shape,
        grid=(N_TOK // tm, FFN_HIDDEN // th),
        in_specs=[
            pl.BlockSpec((tm, D_MODEL), lambda i, k: (i, 0)),
            pl.BlockSpec((1, D_MODEL), lambda i, k: (0, 0)),
            pl.BlockSpec((None, None, 1, D_MODEL), lambda i, k: (_seq_row(i, tm), 4, 0, 0)),
            pl.BlockSpec((None, None, 1, D_MODEL), lambda i, k: (_seq_row(i, tm), 3, 0, 0)),
            pl.BlockSpec((None, None, 1, D_MODEL), lambda i, k: (_seq_row(i, tm), 5, 0, 0)),
            pl.BlockSpec((None, D_MODEL, th), lambda i, k: (layer, 0, k)),
            pl.BlockSpec((None, D_MODEL, th), lambda i, k: (layer, 0, k)),
            pl.BlockSpec((None, th, D_MODEL), lambda i, k: (layer, k, 0)),
            pl.BlockSpec((1, D_MODEL), lambda i, k: (0, 0)),
        ],
        out_specs=out_specs,
        scratch_shapes=[pltpu.VMEM((tm, D_MODEL), BF16), pltpu.VMEM((tm, D_MODEL), F32)],
        compiler_params=pltpu.CompilerParams(
            dimension_semantics=("arbitrary", "arbitrary"), vmem_limit_bytes=VMEM_LIMIT),
        name="ffn",
    )(x, nw, mod, mod, mod, w1, w3, w2, fw)


def _axial_rope(l):
    n_rows = l // GRID_W
    t_row = jnp.repeat(jnp.arange(n_rows, dtype=F32), GRID_W)
    t_col = jnp.tile(jnp.arange(GRID_W, dtype=F32), n_rows)
    n_freq = HEAD_DIM // 4
    inv = ROPE_BASE ** (-jnp.arange(n_freq, dtype=F32) / n_freq)
    ang = jnp.concatenate([t_row[:, None] * inv, t_col[:, None] * inv], axis=-1)
    return jnp.cos(ang), jnp.sin(ang)


MIX_CHUNK = 128
DIRS = 2
NT_DIMS = (((1,), (1,)), ((), ()))
TN_DIMS = (((0,), (0,)), ((), ()))


def _dot(a, b):
    return jnp.dot(a.astype(BF16), b.astype(BF16), preferred_element_type=F32)


def _dot_nt(a, b):
    return lax.dot_general(a.astype(BF16), b.astype(BF16), NT_DIMS, preferred_element_type=F32)


def _dot_tn(a, b):
    return lax.dot_general(a.astype(BF16), b.astype(BF16), TN_DIMS, preferred_element_type=F32)


def _silu(x):
    return x * jax.nn.sigmoid(x)


def _seq_spec(seq_len, width, row_off, col_blk):
    return pl.BlockSpec((seq_len, width), lambda b: (row_off + b, col_blk))


def _mix_params(n_par):
    return pltpu.CompilerParams(dimension_semantics=("parallel",) * n_par, vmem_limit_bytes=VMEM_LIMIT)


def _ret_body(*refs, n_chunks, use_rope, has_s0):
    it = iter(refs)
    lg_ref, q_ref, k_ref, v_ref, g_ref = (next(it) for _ in range(5))
    cos_ref, sin_ref = (next(it), next(it)) if use_rope else (None, None)
    s0_ref = next(it) if has_s0 else None
    o_ref, sf_ref, acc_ref, s_ref, intra_ref, qd_ref, kd_ref, cd_ref = (next(it) for _ in range(8))
    C, H = MIX_CHUNK, N_HEADS

    row = lax.broadcasted_iota(jnp.int32, (C, C), 0)
    col = lax.broadcasted_iota(jnp.int32, (C, C), 1)
    rel = (row - col).astype(F32)
    pos = lax.broadcasted_iota(jnp.int32, (C, HEAD_DIM), 0).astype(F32)
    for d in range(DIRS):
        for h in range(H):
            lg = lg_ref[d, h]
            if d == 0:
                intra_ref[d, h] = jnp.where(rel >= 0, jnp.exp(jnp.maximum(rel, 0.0) * lg), 0.0)
                qd_ref[d, h] = jnp.exp((pos + 1.0) * lg)
                kd_ref[d, h] = jnp.exp((C - 1.0 - pos) * lg)
            else:
                intra_ref[d, h] = jnp.where(rel <= 0, jnp.exp(jnp.maximum(-rel, 0.0) * lg), 0.0)
                qd_ref[d, h] = jnp.exp((C - pos) * lg)
                kd_ref[d, h] = jnp.exp(pos * lg)
            cd_ref[d, h] = jnp.exp(jnp.full((C, HEAD_DIM), C, F32) * lg)
    if has_s0:
        s_ref[...] = s0_ref[...]
    else:
        s_ref[...] = jnp.zeros_like(s_ref)
    acc_ref[...] = jnp.zeros_like(acc_ref)

    units = [(d, h) for d in range(DIRS) for h in range(H)]

    def chunk_step(n, carry):
        rows_d = [pl.ds(pl.multiple_of(c * C, C), C) for c in (n, n_chunks - 1 - n)]
        q_l, k_l, v_l, sc_l = [], [], [], []
        for d, h in units:
            rows, cols = rows_d[d], slice(h * HEAD_DIM, (h + 1) * HEAD_DIM)
            q, k = q_ref[rows, cols], k_ref[rows, cols]
            if use_rope:
                cs, sn = cos_ref[rows, :], sin_ref[rows, :]
                q = q * cs + pltpu.roll(q, HEAD_DIM // 2, 1) * sn
                k = k * cs + pltpu.roll(k, HEAD_DIM // 2, 1) * sn
            k = k * HEAD_DIM ** -0.5
            q_l.append(q)
            k_l.append(k)
            v_l.append(v_ref[rows, cols].astype(BF16))
            sc_l.append(_dot_nt(q, k) * intra_ref[d, h])
        for i, (d, h) in enumerate(units):
            cols = slice(h * HEAD_DIM, (h + 1) * HEAD_DIM)
            s = s_ref[d, h]
            acc_ref[rows_d[d], cols] += _dot(jnp.concatenate([sc_l[i], q_l[i] * qd_ref[d, h]], axis=1),
                                             jnp.concatenate([v_l[i], s.astype(BF16)], axis=0))
            s_ref[d, h] = cd_ref[d, h] * s + _dot_tn(k_l[i] * kd_ref[d, h], v_l[i])
        return carry

    lax.fori_loop(0, n_chunks, chunk_step, 0)
    sf_ref[...] = s_ref[...]

    def finish(n, carry):
        rows = pl.ds(pl.multiple_of(n * C, C), C)
        for h in range(H):
            cols = slice(h * HEAD_DIM, (h + 1) * HEAD_DIM)
            o = acc_ref[rows, cols]
            mu = jnp.mean(o, axis=-1, keepdims=True)
            oc = o - mu
            y = oc * lax.rsqrt(jnp.mean(oc * oc, axis=-1, keepdims=True) + EPS)
            o_ref[rows, cols] = (y * _silu(g_ref[rows, cols])).astype(BF16)
        return carry

    lax.fori_loop(0, n_chunks, finish, 0)


def _retention_pallas(pm, log_gamma, n_seq, seq_len, row_off, rope2, s0, layer):
    use_rope, has_s0 = rope2 is not None, s0 is not None
    st_shape = (DIRS, N_HEADS, HEAD_DIM, HEAD_DIM)
    in_specs = [pl.BlockSpec(memory_space=pltpu.SMEM)]
    in_specs += [_seq_spec(seq_len, GROUP_WIDTH, row_off, cb) for cb in range(4)]
    args = [log_gamma, pm, pm, pm, pm]
    if use_rope:
        in_specs += [pl.BlockSpec((seq_len, HEAD_DIM), lambda b: (0, 0))] * 2
        args += list(rope2)
    if has_s0:
        in_specs.append(pl.BlockSpec((None, None) + st_shape, lambda b: (b, layer, 0, 0, 0, 0)))
        args.append(s0)
    return pl.pallas_call(
        functools.partial(_ret_body, n_chunks=seq_len // MIX_CHUNK, use_rope=use_rope, has_s0=has_s0),
        out_shape=(jax.ShapeDtypeStruct((n_seq * seq_len, GROUP_WIDTH), BF16),
                   jax.ShapeDtypeStruct((n_seq,) + st_shape, F32)),
        grid=(n_seq,),
        in_specs=in_specs,
        out_specs=(pl.BlockSpec((seq_len, GROUP_WIDTH), lambda b: (b, 0)),
                   pl.BlockSpec((None,) + st_shape, lambda b: (b, 0, 0, 0, 0))),
        scratch_shapes=[pltpu.VMEM((seq_len, GROUP_WIDTH), F32), pltpu.VMEM(st_shape, F32),
                        pltpu.VMEM((DIRS, N_HEADS, MIX_CHUNK, MIX_CHUNK), F32),
                        pltpu.VMEM((DIRS, N_HEADS, MIX_CHUNK, HEAD_DIM), F32),
                        pltpu.VMEM((DIRS, N_HEADS, MIX_CHUNK, HEAD_DIM), F32),
                        pltpu.VMEM((DIRS, N_HEADS, MIX_CHUNK, HEAD_DIM), F32)],
        compiler_params=_mix_params(1),
        name="retention",
    )(*args)


def _rope_tables(l):
    cos, sin = _axial_rope(l)
    return jnp.concatenate([cos, cos], axis=-1), jnp.concatenate([-sin, sin], axis=-1)


S5_HALF_G = S5_GROUPS // 2
S5_HALF_U = S5_HALF_G * S5_CH
S5_HALF_X = S5_HALF_G * S5_N
S5_X = S5_GROUPS * S5_N
S5_TC = 512
S5_BLK = 8
S5_TABS = 4


def _s5_tables(a_re, a_im, log_step, b_re, b_im, c_re, c_im):
    dt = jnp.exp(log_step)[..., None]
    mag = jnp.exp(a_re * dt)
    ab_re = mag * jnp.cos(a_im * dt)
    ab_im = mag * jnp.sin(a_im * dt)
    den = a_re * a_re + a_im * a_im
    nr = ab_re - 1.0
    f_re = (nr * a_re + ab_im * a_im) / den
    f_im = (ab_im * a_re - nr * a_im) / den
    bb_re = f_re[..., None] * b_re - f_im[..., None] * b_im
    bb_im = f_re[..., None] * b_im + f_im[..., None] * b_re
    eye = jnp.eye(S5_HALF_G, dtype=F32)

    def in_mat(bb):
        bb = bb.reshape(DIRS, 2, S5_HALF_G, S5_N, S5_CH)
        return jnp.einsum('dhgnc,gk->dhgckn', bb, eye).reshape(DIRS, 2, S5_HALF_U, S5_HALF_X)

    def out_mat(cc):
        cc = cc.reshape(DIRS, 2, S5_HALF_G, S5_CH, S5_N)
        return jnp.einsum('dhgcn,gk->dhgnkc', cc, eye).reshape(DIRS, 2, S5_HALF_X, S5_HALF_U)

    bm = jnp.concatenate([in_mat(bb_re), in_mat(bb_im)], axis=-1).astype(BF16)
    cm = jnp.concatenate([out_mat(c_re), -out_mat(c_im)], axis=-2).astype(BF16)
    t = jnp.arange(S5_BLK, dtype=F32)
    order = jnp.stack([t, S5_BLK - 1.0 - t])
    shifts = 2.0 ** jnp.arange(S5_TABS - 1, dtype=F32)
    expo = jnp.concatenate([jnp.where(order[:, None, :] >= shifts[None, :, None], shifts[None, :, None], jnp.nan),
                            order[:, None, :] + 1.0], axis=1)
    live = ~jnp.isnan(expo)
    e = jnp.where(live, expo, 0.0)[..., None]
    adt_re = (a_re * dt).reshape(DIRS, 1, 1, S5_X)
    adt_im = (a_im * dt).reshape(DIRS, 1, 1, S5_X)
    pmag = jnp.where(live[..., None], jnp.exp(e * adt_re), 0.0)
    pw_re = pmag * jnp.cos(e * adt_im)
    pw_im = pmag * jnp.sin(e * adt_im)
    return bm, cm, pw_re, pw_im


def _gelu_tanh(x):
    return 0.5 * x * (1.0 + jnp.tanh(math.sqrt(2.0 / math.pi) * (x + 0.044715 * (x * x * x))))


def _s5_body(*refs, seq_len, has_s0):
    it = iter(refs)
    u_ref, bm_ref, cm_ref, pwr_ref, pwi_ref, d_ref, gw_ref, gb_ref = (next(it) for _ in range(8))
    x0r_ref, x0i_ref = (next(it), next(it)) if has_s0 else (None, None)
    o_ref, sfr_ref, sfi_ref, y_ref, xr_ref, xi_ref, xb_ref = (next(it) for _ in range(7))
    tc = min(seq_len, S5_TC)
    n_tiles = seq_len // tc
    n_pair = tc // (2 * S5_BLK)

    y_ref[...] = d_ref[...] * u_ref[...]
    for d in range(DIRS):
        last = S5_BLK - 1 if d == 0 else 0
        for hf in range(2):
            xs = slice(hf * S5_HALF_X, (hf + 1) * S5_HALF_X)
            us = slice(hf * S5_HALF_U, (hf + 1) * S5_HALF_U)

            def scan_block(x_in, carry, d=d, xs=xs, last=last):
                xr, xi = x_in
                car_re, car_im = carry
                for k in range(S5_TABS - 1):
                    s = 1 << k
                    shift = s if d == 0 else S5_BLK - s
                    p_re, p_im = pwr_ref[d, k, :, xs], pwi_ref[d, k, :, xs]
                    sr, si = pltpu.roll(xr, shift, 0), pltpu.roll(xi, shift, 0)
                    xr, xi = xr + p_re * sr - p_im * si, xi + p_re * si + p_im * sr
                p_re, p_im = pwr_ref[d, S5_TABS - 1, :, xs], pwi_ref[d, S5_TABS - 1, :, xs]
                xr, xi = xr + p_re * car_re - p_im * car_im, xi + p_re * car_im + p_im * car_re
                return (xr, xi), (xr[last:last + 1, :], xi[last:last + 1, :])

            def scan_pair(j, carry, d=d, scan_block=scan_block):
                pair = j if d == 0 else n_pair - 1 - j
                rows = pl.ds(pl.multiple_of(pair * 2 * S5_BLK, 2 * S5_BLK), 2 * S5_BLK)
                xr2, xi2 = xr_ref[rows, :], xi_ref[rows, :]
                halves = [(xr2[:S5_BLK], xi2[:S5_BLK]), (xr2[S5_BLK:], xi2[S5_BLK:])]
                order = (0, 1) if d == 0 else (1, 0)
                out = [None, None]
                for idx in order:
                    out[idx], carry = scan_block(halves[idx], carry)
                xb_ref[rows, :S5_HALF_X] = jnp.concatenate([out[0][0], out[1][0]], axis=0).astype(BF16)
                xb_ref[rows, S5_HALF_X:] = jnp.concatenate([out[0][1], out[1][1]], axis=0).astype(BF16)
                return carry

            def scan_tile(i, carry, d=d, hf=hf, us=us, scan_pair=scan_pair):
                tile = i if d == 0 else n_tiles - 1 - i
                rows = pl.ds(pl.multiple_of(tile * tc, tc), tc)
                bu = _dot(u_ref[rows, us], bm_ref[d, hf])
                xr_ref[...] = bu[:, :S5_HALF_X]
                xi_ref[...] = bu[:, S5_HALF_X:]
                carry = lax.fori_loop(0, n_pair, scan_pair, carry)
                y_ref[rows, us] += jnp.dot(xb_ref[...], cm_ref[d, hf], preferred_element_type=F32)
                return carry

            if has_s0:
                carry0 = (x0r_ref[d, :, xs], x0i_ref[d, :, xs])
            else:
                carry0 = (jnp.zeros((1, S5_HALF_X), F32), jnp.zeros((1, S5_HALF_X), F32))
            car_re, car_im = lax.fori_loop(0, n_tiles, scan_tile, carry0)
            sfr_ref[d, :, xs] = car_re
            sfi_ref[d, :, xs] = car_im

    z = _gelu_tanh(y_ref[...])
    o_ref[...] = (z * jax.nn.sigmoid(_dot(z, gw_ref[...]) + gb_ref[...])).astype(BF16)


def _s5_pallas(pm, tables, s5_d, glu_w, glu_b, n_seq, seq_len, row_off, x0, layer):
    bm, cm, pw_re, pw_im = tables
    has_s0 = x0 is not None
    full = lambda shape: pl.BlockSpec(shape, lambda b: (0,) * len(shape))
    of_layer = lambda t: pl.BlockSpec((None,) + t.shape[1:], lambda b: (layer,) + (0,) * (t.ndim - 1))
    in_specs = [_seq_spec(seq_len, GROUP_WIDTH, row_off, 13),
                of_layer(bm), of_layer(cm), of_layer(pw_re), of_layer(pw_im),
                full((1, GROUP_WIDTH)), full((GROUP_WIDTH, GROUP_WIDTH)), full((1, GROUP_WIDTH))]
    args = [pm, bm, cm, pw_re, pw_im, s5_d.reshape(1, GROUP_WIDTH), glu_w.astype(BF16),
            glu_b.reshape(1, GROUP_WIDTH)]
    if has_s0:
        in_specs += [pl.BlockSpec((None, None, DIRS, 1, S5_X), lambda b: (b, layer, 0, 0, 0))] * 2
        args += list(x0)
    st = jax.ShapeDtypeStruct((n_seq, DIRS, 1, S5_X), F32)
    st_spec = pl.BlockSpec((None, DIRS, 1, S5_X), lambda b: (b, 0, 0, 0))
    tc = min(seq_len, S5_TC)
    return pl.pallas_call(
        functools.partial(_s5_body, seq_len=seq_len, has_s0=has_s0),
        out_shape=(jax.ShapeDtypeStruct((n_seq * seq_len, GROUP_WIDTH), BF16), st, st),
        grid=(n_seq,),
        in_specs=in_specs,
        out_specs=(pl.BlockSpec((seq_len, GROUP_WIDTH), lambda b: (b, 0)), st_spec, st_spec),
        scratch_shapes=[pltpu.VMEM((seq_len, GROUP_WIDTH), F32),
                        pltpu.VMEM((tc, S5_HALF_X), F32), pltpu.VMEM((tc, S5_HALF_X), F32),
                        pltpu.VMEM((tc, 2 * S5_HALF_X), BF16)],
        compiler_params=_mix_params(1),
        name="s5",
    )(*args)


GLA_LEVELS = 7


def _chunk_cumsum_rows(x, rowi):
    s = 1
    while s < MIX_CHUNK:
        x = x + jnp.where(rowi >= s, pltpu.roll(x, s, 0), 0.0)
        s *= 2
    return x


def _gla_body(*refs, n_chunks, has_s0):
    it = iter(refs)
    q_ref, f0_ref, f1_ref, i_ref, g_ref, lb_ref, nw_ref = (next(it) for _ in range(7))
    s0_ref = next(it) if has_s0 else None
    o_ref, sf_ref, acc_ref, s_ref, code_ref = (next(it) for _ in range(5))
    C, H = MIX_CHUNK, N_HEADS
    f_refs = (f0_ref, f1_ref)

    rowi = lax.broadcasted_iota(jnp.int32, (C, HEAD_DIM), 0)
    ri = lax.broadcasted_iota(jnp.int32, (C, C), 0)
    ci = lax.broadcasted_iota(jnp.int32, (C, C), 1)
    top_bit = 31 - lax.clz(ri ^ ci)
    code_ref[...] = jnp.where(ri > ci, top_bit, jnp.where(ri < ci, -1 - top_bit, GLA_LEVELS))
    if has_s0:
        s_ref[...] = s0_ref[...]
    else:
        s_ref[...] = jnp.zeros_like(s_ref)
    acc_ref[...] = jnp.zeros_like(acc_ref)

    def chunk_step(n, carry):
        for d in range(DIRS):
            c = n if d == 0 else n_chunks - 1 - n
            rows = pl.ds(pl.multiple_of(c * C, C), C)
            for h in range(H):
                cols = slice(h * HEAD_DIM, (h + 1) * HEAD_DIM)
                code = code_ref[...]
                q = _silu(q_ref[rows, cols]) * HEAD_DIM ** -0.5
                v = i_ref[rows, cols].astype(BF16)
                fx = f_refs[d][rows, cols]
                lb = lb_ref[d:d + 1, cols]
                sig = 1.0 / (1.0 + jnp.exp(-fx))
                logf = jnp.log2(jnp.maximum(lb, LB_FLOOR) + (1.0 - lb) * sig)
                k = (1.0 - lb) * (1.0 - sig)
                cum = _chunk_cumsum_rows(logf, rowi)
                own = cum
                attn = jnp.where(code == GLA_LEVELS, _dot_nt(q, k), 0.0)
                for lvl in range(GLA_LEVELS):
                    m = 1 << lvl
                    prev = pltpu.roll(own, m, 0)
                    pre = jnp.minimum(cum - prev, 0.0)
                    suf = own - cum
                    if d == 0:
                        sc = _dot_nt(q * jnp.exp2(pre), k * jnp.exp2(suf))
                        hit = code == lvl
                    else:
                        sc = _dot_nt(q * jnp.exp2(suf + logf), k * jnp.exp2(jnp.minimum(pre - logf, 0.0)))
                        hit = code == -1 - lvl
                    attn = jnp.where(hit, sc, attn)
                    own = jnp.where(((rowi >> lvl) & 1) == 0, pltpu.roll(own, C - m, 0), own)
                tot = own
                st = s_ref[d, h]
                if d == 0:
                    q_in, k_out = q * jnp.exp2(cum), k * jnp.exp2(tot - cum)
                else:
                    q_in, k_out = q * jnp.exp2(tot - cum + logf), k * jnp.exp2(cum - logf)
                acc_ref[rows, cols] += _dot(attn, v) + _dot_nt(q_in, st)
                s_ref[d, h] = jnp.exp2(tot[0:1, :]) * st + _dot_tn(v, k_out)
        return carry

    lax.fori_loop(0, n_chunks, chunk_step, 0)
    sf_ref[...] = s_ref[...]

    def finish(n, carry):
        rows = pl.ds(pl.multiple_of(n * C, C), C)
        for h in range(H):
            cols = slice(h * HEAD_DIM, (h + 1) * HEAD_DIM)
            o = acc_ref[rows, cols]
            y = o * lax.rsqrt(jnp.mean(o * o, axis=-1, keepdims=True) + EPS) * nw_ref[...]
            o_ref[rows, cols] = (y * _silu(g_ref[rows, cols])).astype(BF16)
        return carry

    lax.fori_loop(0, n_chunks, finish, 0)


def _gla_pallas(pm, lower_bound, norm_w, n_seq, seq_len, row_off, s0_t, layer):
    has_s0 = s0_t is not None
    st_shape = (DIRS, N_HEADS, HEAD_DIM, HEAD_DIM)
    in_specs = [_seq_spec(seq_len, GROUP_WIDTH, row_off, cb) for cb in (8, 9, 10, 11, 12)]
    in_specs += [pl.BlockSpec((DIRS, GROUP_WIDTH), lambda b: (0, 0)), pl.BlockSpec((1, HEAD_DIM), lambda b: (0, 0))]
    args = [pm] * 5 + [lower_bound, norm_w.reshape(1, HEAD_DIM)]
    if has_s0:
        in_specs.append(pl.BlockSpec((None, None) + st_shape, lambda b: (b, layer, 0, 0, 0, 0)))
        args.append(s0_t)
    return pl.pallas_call(
        functools.partial(_gla_body, n_chunks=seq_len // MIX_CHUNK, has_s0=has_s0),
        out_shape=(jax.ShapeDtypeStruct((n_seq * seq_len, GROUP_WIDTH), BF16),
                   jax.ShapeDtypeStruct((n_seq,) + st_shape, F32)),
        grid=(n_seq,),
        in_specs=in_specs,
        out_specs=(pl.BlockSpec((seq_len, GROUP_WIDTH), lambda b: (b, 0)),
                   pl.BlockSpec((None,) + st_shape, lambda b: (b, 0, 0, 0, 0))),
        scratch_shapes=[pltpu.VMEM((seq_len, GROUP_WIDTH), F32), pltpu.VMEM(st_shape, F32),
                        pltpu.VMEM((MIX_CHUNK, MIX_CHUNK), jnp.int32)],
        compiler_params=_mix_params(1),
        name="hgrn2",
    )(*args)


GDN_BLOCK_BITS = 4
GDN_MERGES = 3
N_GATES = DIRS * N_HEADS


def _softplus(x):
    return jnp.maximum(x, 0.0) + jnp.log1p(jnp.exp(-jnp.abs(x)))


def _gdn_body(*refs, seq_len, n_sub, has_s0):
    it = iter(refs)
    (q_ref, k_ref, v_ref, g_ref, ps_ref, pst_ref, cw_ref, prow_ref, pcol_ref, nw_ref) = (next(it) for _ in range(10))
    s0_ref = next(it) if has_s0 else None
    (o_ref, sf_ref, acc_ref, s_ref, qn_ref, kn_ref, vn_ref, gate_ref, gatet_ref, blev_ref) = (
        next(it) for _ in range(10))
    C, H, L = MIX_CHUNK, N_HEADS, seq_len
    R = n_sub * L
    n_chunks = L // C
    w = GROUP_WIDTH

    pos = lax.broadcasted_iota(jnp.int32, (R, HEAD_DIM), 0) & (L - 1)
    for part, (src, dst) in enumerate(((q_ref, qn_ref), (k_ref, kn_ref), (v_ref, vn_ref))):
        for h in range(H):
            cols = slice(h * HEAD_DIM, (h + 1) * HEAD_DIM)
            wc = slice(part * w + h * HEAD_DIM, part * w + (h + 1) * HEAD_DIM)
            x = src[:, cols]
            x_prev = jnp.where(pos >= 1, pltpu.roll(x, 1, 0), 0.0)
            x_next = jnp.where(pos < L - 1, pltpu.roll(x, R - 1, 0), 0.0)
            y = _silu(cw_ref[0:1, wc] * x_prev + cw_ref[1:2, wc] * x + cw_ref[2:3, wc] * x_next)
            if part < 2:
                y = y * lax.rsqrt(jnp.sum(y * y, axis=-1, keepdims=True) + EPS)
            if part == 0:
                y = y * HEAD_DIM ** -0.5
            dst[:, cols] = y

    rowi = lax.broadcasted_iota(jnp.int32, (C, HEAD_DIM), 0)
    lane = lax.broadcasted_iota(jnp.int32, (C, HEAD_DIM), 1)
    lane_t = lax.broadcasted_iota(jnp.int32, (2 * N_GATES, C), 1)
    row_t = lax.broadcasted_iota(jnp.int32, (2 * N_GATES, C), 0)
    for c in range(R // C):
        rows = slice(c * C, (c + 1) * C)
        a = ps_ref[rows, :]
        la = -jnp.exp(prow_ref[0:1, :]) * _softplus(a + prow_ref[1:2, :])
        pre = _chunk_cumsum_rows(la, rowi)
        tot = jnp.broadcast_to(pre[C - 1:C, :], (C, HEAD_DIM))
        g = jnp.where(lane < N_HEADS, pre, tot - pre + la)
        gate_ref[rows, :] = jnp.where(lane < N_GATES, g,
                                      jnp.where(lane < 2 * N_GATES, jax.nn.sigmoid(a),
                                                pltpu.roll(tot, 2 * N_GATES, 1)))
        at = pst_ref[:, rows]
        lat = -jnp.exp(pcol_ref[:, 0:1]) * _softplus(at + pcol_ref[:, 1:2])
        pre_t, s = lat, 1
        while s < C:
            pre_t = pre_t + jnp.where(lane_t >= s, pltpu.roll(pre_t, s, 1), 0.0)
            s *= 2
        tot_t = jnp.broadcast_to(pre_t[:, C - 1:C], (2 * N_GATES, C))
        g_t = jnp.where(row_t < N_HEADS, pre_t, tot_t - pre_t + lat)
        gatet_ref[:, rows] = jnp.where(row_t < N_GATES, g_t, jax.nn.sigmoid(at))

    ri = lax.broadcasted_iota(jnp.int32, (C, C), 0)
    ci = lax.broadcasted_iota(jnp.int32, (C, C), 1)
    bx = (ri >> GDN_BLOCK_BITS) ^ (ci >> GDN_BLOCK_BITS)
    blev_ref[...] = jnp.where(bx == 0, 0, 32 - lax.clz(bx))
    if has_s0:
        s_ref[...] = s0_ref[...]
    else:
        s_ref[...] = jnp.zeros_like(s_ref)
    acc_ref[...] = jnp.zeros_like(acc_ref)

    units = [(s, d, h) for s in range(n_sub) for d in range(DIRS) for h in range(H)]

    def chunk_step(n, carry):
        eye = jnp.where(ri == ci, 1.0, 0.0)
        blev = blev_ref[...]
        rows_sd = [[pl.ds(pl.multiple_of(s * L + c * C, C), C) for c in (n, n_chunks - 1 - n)] for s in range(n_sub)]
        incl_d = [ri >= ci, ri <= ci]
        strict_d = [ri > ci, ri < ci]
        q_l, et_l, ek_l, rhs_l, a_l, attn_l = ([] for _ in range(6))
        for s, d, h in units:
            rows, cols, cg = rows_sd[s][d], slice(h * HEAD_DIM, (h + 1) * HEAD_DIM), d * N_HEADS + h
            q, k, v = qn_ref[rows, cols], kn_ref[rows, cols], vn_ref[rows, cols]
            g_i = jnp.broadcast_to(gate_ref[rows, cg:cg + 1], (C, C))
            b_i = jnp.broadcast_to(gate_ref[rows, N_GATES + cg:N_GATES + cg + 1], (C, C))
            tot = jnp.broadcast_to(gate_ref[rows, 2 * N_GATES + cg:2 * N_GATES + cg + 1], (C, C))
            g_j = gatet_ref[cg:cg + 1, rows]
            decay = jnp.where(incl_d[d], jnp.exp(jnp.minimum(g_i - g_j, 0.0)), 0.0)
            e_g = jnp.exp(g_i)
            kb = k.astype(BF16)
            a_l.append(jnp.where(strict_d[d], _dot_nt(kb, kb) * b_i * decay, 0.0))
            attn_l.append(_dot_nt(q, kb) * decay)
            rhs_l.append(jnp.concatenate([v * b_i, k * (b_i * e_g)], axis=1).astype(BF16))
            q_l.append(q * e_g)
            ek_l.append(k * jnp.exp(tot - g_i))
            et_l.append(jnp.exp(tot))
        b_l = [-jnp.where(blev == 0, a, 0.0) for a in a_l]
        p_l = [eye + b for b in b_l]
        for _ in range(GDN_BLOCK_BITS - 1):
            b_l = [_dot(b, b) for b in b_l]
            p_l = [p + _dot(p, b) for p, b in zip(p_l, b_l)]
        for lvl in range(1, GDN_MERGES + 1):
            ep_l = [_dot(jnp.where(blev == lvl, a, 0.0), p) for a, p in zip(a_l, p_l)]
            p_l = [p - _dot(p, ep) for p, ep in zip(p_l, ep_l)]
        uw_l = [_dot(p, rhs) for p, rhs in zip(p_l, rhs_l)]
        s_l = [s_ref[s * DIRS + d, h] for s, d, h in units]
        vn_l = [uw[:, :HEAD_DIM] - _dot(uw[:, HEAD_DIM:], st) for uw, st in zip(uw_l, s_l)]
        for i, (s, d, h) in enumerate(units):
            cols = slice(h * HEAD_DIM, (h + 1) * HEAD_DIM)
            acc_ref[rows_sd[s][d], cols] += _dot(jnp.concatenate([q_l[i], attn_l[i]], axis=1),
                                                 jnp.concatenate([s_l[i], vn_l[i]], axis=0))
            s_ref[s * DIRS + d, h] = s_l[i] * et_l[i] + _dot_tn(ek_l[i], vn_l[i])
        return carry

    lax.fori_loop(0, n_chunks, chunk_step, 0)
    sf_ref[...] = s_ref[...]

    def finish(n, carry):
        rows = pl.ds(pl.multiple_of(n * C, C), C)
        for h in range(H):
            cols = slice(h * HEAD_DIM, (h + 1) * HEAD_DIM)
            o = acc_ref[rows, cols]
            y = o * lax.rsqrt(jnp.mean(o * o, axis=-1, keepdims=True) + EPS) * nw_ref[...]
            o_ref[rows, cols] = (y * _silu(g_ref[rows, cols])).astype(BF16)
        return carry

    lax.fori_loop(0, R // C, finish, 0)


GDN_CTX_SUB = 2


def _gdn_pallas(pm, ps, ps_t, conv_w, a_log, dt_bias, norm_w, n_seq, seq_len, row_off, s0, layer):
    has_s0 = s0 is not None
    n_sub = 1 if has_s0 else GDN_CTX_SUB
    n_seq, seq_len, row_off = n_seq // n_sub, seq_len * n_sub, row_off // n_sub
    st_shape = (n_sub * DIRS, N_HEADS, HEAD_DIM, HEAD_DIM)
    par = jnp.stack([a_log.reshape(N_GATES), dt_bias.reshape(N_GATES)])
    par_row = jnp.pad(par, ((0, 0), (0, HEAD_DIM - N_GATES)))
    par_col = jnp.pad(par.T, ((0, N_GATES), (0, 0)))
    full = lambda shape: pl.BlockSpec(shape, lambda b: (0,) * len(shape))
    in_specs = [_seq_spec(seq_len, GROUP_WIDTH, row_off, cb) for cb in (4, 5, 6, 7)]
    in_specs += [_seq_spec(seq_len, PROJ_SMALL, row_off, 0),
                 pl.BlockSpec((2 * N_GATES, seq_len), lambda b: (0, row_off + b)),
                 full((GDN_CONV, 3 * GROUP_WIDTH)), full((2, HEAD_DIM)), full((2 * N_GATES, 2)), full((1, HEAD_DIM))]
    args = [pm] * 4 + [ps, ps_t, conv_w, par_row, par_col, norm_w.reshape(1, HEAD_DIM)]
    if has_s0:
        in_specs.append(pl.BlockSpec((None, None) + st_shape, lambda b: (b, layer, 0, 0, 0, 0)))
        args.append(s0)
    seq_f32 = lambda width: pltpu.VMEM((seq_len, width), F32)
    out, s_fin = pl.pallas_call(
        functools.partial(_gdn_body, seq_len=seq_len // n_sub, n_sub=n_sub, has_s0=has_s0),
        out_shape=(jax.ShapeDtypeStruct((n_seq * seq_len, GROUP_WIDTH), BF16),
                   jax.ShapeDtypeStruct((n_seq,) + st_shape, F32)),
        grid=(n_seq,),
        in_specs=in_specs,
        out_specs=(pl.BlockSpec((seq_len, GROUP_WIDTH), lambda b: (b, 0)),
                   pl.BlockSpec((None,) + st_shape, lambda b: (b, 0, 0, 0, 0))),
        scratch_shapes=[seq_f32(GROUP_WIDTH), pltpu.VMEM(st_shape, F32),
                        seq_f32(GROUP_WIDTH), seq_f32(GROUP_WIDTH), seq_f32(GROUP_WIDTH),
                        seq_f32(HEAD_DIM), pltpu.VMEM((2 * N_GATES, seq_len), F32),
                        pltpu.VMEM((MIX_CHUNK, MIX_CHUNK), jnp.int32)],
        compiler_params=_mix_params(1),
        name="gated_delta",
    )(*args)
    return out, s_fin.reshape(n_seq * n_sub, DIRS, N_HEADS, HEAD_DIM, HEAD_DIM)


def kernel(x_prompt, x_sample, state_ret, state_gdn, state_hgrn, state_s5_re, state_s5_im, c, c_ctx, norm1_w, norm2_w, final_norm_w, ada_w, ada_b, in_proj, out_proj, ret_decay_logit, gdn_conv, gdn_a_log, gdn_dt_bias, gdn_norm_w, hg_lb_param, hg_norm_w, s5_a_re, s5_a_im, s5_b_re, s5_b_im, s5_c_re, s5_c_im, s5_log_step, s5_d, s5_glu_w, s5_glu_b, ffn_w1, ffn_w3, ffn_w2):
    lb_soft = jax.nn.softmax(hg_lb_param, axis=0)
    lower_bounds = jnp.cumsum(lb_soft, axis=0) - lb_soft[0]
    rope2 = _rope_tables(DEC_SEQ)

    cvec = jnp.zeros((N_SEQ_ROWS, D_MODEL), F32).at[0].set(c_ctx).at[1:1 + DEC_BATCH].set(c)
    mod_all = _ada(cvec, ada_w, ada_b).reshape(DEPTH, N_SEQ_ROWS, N_MOD, 1, D_MODEL)

    gate0 = 8 * GROUP_WIDTH
    w_in = in_proj.astype(BF16)
    w_main = jnp.concatenate([w_in[:, :, :gate0], w_in[:, :, gate0 + 2 * N_GATES:]], axis=-1)
    w_small = jnp.pad(w_in[:, :, gate0:gate0 + 2 * N_GATES], ((0, 0), (0, 0), (0, PROJ_SMALL - 2 * N_GATES)))
    w_out = out_proj.astype(BF16)
    w1, w3, w2 = ffn_w1.astype(BF16), ffn_w3.astype(BF16), ffn_w2.astype(BF16)

    x = jnp.concatenate([x_prompt.reshape(N_CTX_TOK, D_MODEL), x_sample.reshape(N_LAT_TOK, D_MODEL)], axis=0)
    hgrn_t = jnp.swapaxes(state_hgrn, -1, -2)
    s5_x0 = (state_s5_re.reshape(DEC_BATCH, DEPTH, DIRS, 1, S5_X), state_s5_im.reshape(DEC_BATCH, DEPTH, DIRS, 1, S5_X))
    lat_off = N_CTX_TOK // DEC_SEQ
    tables = jax.vmap(_s5_tables)(s5_a_re, s5_a_im, s5_log_step, s5_b_re, s5_b_im, s5_c_re, s5_c_im)
    ctx_states = []
    for i in range(DEPTH):
        mod = mod_all[i]
        pm, ps = _inproj(x, norm1_w[i][None], mod, w_main, w_small, i)
        ps_t = ps[:, :2 * N_GATES].T
        lg = jax.nn.log_sigmoid(ret_decay_logit[i])
        ctx = (BATCH, SEQ, 0)
        lat = (DEC_BATCH, DEC_SEQ, lat_off)

        ret_c, rs = _retention_pallas(pm, lg, *ctx, None, None, i)
        ret_l, _ = _retention_pallas(pm, lg, *lat, rope2, state_ret, i)
        gdn_args = (pm, ps, ps_t, gdn_conv[i], gdn_a_log[i], gdn_dt_bias[i], gdn_norm_w[i])
        gdn_c, gs = _gdn_pallas(*gdn_args, *ctx, None, i)
        gdn_l, _ = _gdn_pallas(*gdn_args, *lat, state_gdn, i)
        hg_c, hs = _gla_pallas(pm, lower_bounds[i], hg_norm_w[i], *ctx, None, i)
        hg_l, _ = _gla_pallas(pm, lower_bounds[i], hg_norm_w[i], *lat, hgrn_t, i)
        s5_args = (pm, tables, s5_d[i], s5_glu_w[i], s5_glu_b[i])
        s5_c, xr, xi = _s5_pallas(*s5_args, *ctx, None, i)
        s5_l, _, _ = _s5_pallas(*s5_args, *lat, s5_x0, i)
        ctx_states.append((rs, gs, jnp.swapaxes(hs, -1, -2), xr.reshape(BATCH, DIRS, S5_GROUPS, S5_N),
                           xi.reshape(BATCH, DIRS, S5_GROUPS, S5_N)))

        parts = ((ret_c, ret_l), (gdn_c, gdn_l), (hg_c, hg_l), (s5_c, s5_l))
        x = _outproj(parts, w_out, x, mod, i)
        x = _ffn(x, norm2_w[i][None], mod, w1, w3, w2, final_norm_w[None], i)

    y_prompt = x[0].reshape(BATCH, SEQ, D_MODEL)
    y_sample = x[1].reshape(DEC_BATCH, DEC_SEQ, D_MODEL)
    new_states = tuple(jnp.stack([s[j] for s in ctx_states], axis=1) for j in range(5))
    return (y_prompt, y_sample) + new_states
```

```python
import functools
import math

import jax
import jax.numpy as jnp
from jax import lax
from jax.experimental import pallas as pl
from jax.experimental.pallas import tpu as pltpu

F32 = jnp.float32
BF16 = jnp.bfloat16

D_MODEL = 2048
BATCH = 16
SEQ = 256
DEPTH = 2
DEC_BATCH = 8
DEC_SEQ = 1024
GRID_W = 64
HEAD_DIM = 128
GROUP_WIDTH = 512
N_HEADS = 4
S5_CH = 16
S5_GROUPS = 32
S5_N = 64
GDN_CONV = 3
CHUNK = 64
HG_CHUNK = 16
ROPE_BASE = 10000.0
FFN_HIDDEN = 5632
N_MOD = 6
EPS = 1e-6
LB_FLOOR = 1e-30

N_CTX_TOK = BATCH * SEQ
N_LAT_TOK = DEC_BATCH * DEC_SEQ
N_TOK = N_CTX_TOK + N_LAT_TOK
N_SEQ_ROWS = 16
PROJ_MAIN = 14 * GROUP_WIDTH
PROJ_SMALL = 128
VMEM_LIMIT = 56 * 1024 * 1024
VMEM_LIMIT_FFN = 60 * 1024 * 1024


def _seq_row(tile, tm):
    n_ctx = N_CTX_TOK // tm
    per_lat = DEC_SEQ // tm
    return jnp.where(tile < n_ctx, 0, 1 + (tile - n_ctx) // per_lat)


def _stream_specs(tm, width, n_col=1):
    n_ctx = N_CTX_TOK // tm

    def ctx_map(i, j):
        return jnp.minimum(i, n_ctx - 1), (jnp.where(i < n_ctx, j, n_col - 1) if n_col > 1 else 0)

    def lat_map(i, j):
        return jnp.maximum(i - n_ctx, 0), (jnp.where(i >= n_ctx, j, 0) if n_col > 1 else 0)

    return pl.BlockSpec((tm, width), ctx_map), pl.BlockSpec((tm, width), lat_map)


def _on_stream(tile, tm, fn):
    n_ctx = N_CTX_TOK // tm
    pl.when(tile < n_ctx)(functools.partial(fn, 0))
    pl.when(tile >= n_ctx)(functools.partial(fn, 1))


def _ada_body(c_ref, w_ref, b_ref, o_ref):
    cv = c_ref[...]
    s = cv * jax.nn.sigmoid(cv)
    o_ref[0] = jnp.dot(s.astype(BF16), w_ref[0].astype(BF16), preferred_element_type=F32) + b_ref[0]


def _ada(cvec, ada_w, ada_b):
    tn = 1024
    n = N_MOD * D_MODEL
    return pl.pallas_call(
        _ada_body,
        out_shape=jax.ShapeDtypeStruct((DEPTH, N_SEQ_ROWS, n), F32),
        grid=(DEPTH, n // tn),
        in_specs=[
            pl.BlockSpec((N_SEQ_ROWS, D_MODEL), lambda l, j: (0, 0)),
            pl.BlockSpec((1, D_MODEL, tn), lambda l, j: (l, 0, j)),
            pl.BlockSpec((1, 1, tn), lambda l, j: (l, 0, j)),
        ],
        out_specs=pl.BlockSpec((1, N_SEQ_ROWS, tn), lambda l, j: (l, 0, j)),
        compiler_params=pltpu.CompilerParams(
            dimension_semantics=("parallel", "parallel"), vmem_limit_bytes=VMEM_LIMIT),
        name="ada_mod",
    )(cvec, ada_w, ada_b.reshape(DEPTH, 1, n))


def _norm_mod(x, nw, sc, sh):
    ms = jnp.mean(x * x, axis=-1, keepdims=True)
    y = x * lax.rsqrt(ms + EPS) * nw
    return y * (1.0 + sc) + sh


PROJ_TM = 1024


def _inproj_body(x_ref, nw_ref, sc_ref, sh_ref, w_ref, ws_ref, o_ref, os_ref, h_ref):
    @pl.when(pl.program_id(1) == 0)
    def _():
        hb = _norm_mod(x_ref[...], nw_ref[...], sc_ref[...], sh_ref[...]).astype(BF16)
        h_ref[...] = hb
        os_ref[...] = jnp.dot(hb, ws_ref[...], preferred_element_type=F32)

    o_ref[...] = jnp.dot(h_ref[...], w_ref[...], preferred_element_type=F32)


def _inproj(x, nw, mod, w_main, w_small, layer):
    tm, tn = PROJ_TM, 1024
    return pl.pallas_call(
        _inproj_body,
        out_shape=(jax.ShapeDtypeStruct((N_TOK, PROJ_MAIN), F32),
                   jax.ShapeDtypeStruct((N_TOK, PROJ_SMALL), F32)),
        grid=(N_TOK // tm, PROJ_MAIN // tn),
        in_specs=[
            pl.BlockSpec((tm, D_MODEL), lambda i, j: (i, 0)),
            pl.BlockSpec((1, D_MODEL), lambda i, j: (0, 0)),
            pl.BlockSpec((None, None, 1, D_MODEL), lambda i, j: (_seq_row(i, tm), 1, 0, 0)),
            pl.BlockSpec((None, None, 1, D_MODEL), lambda i, j: (_seq_row(i, tm), 0, 0, 0)),
            pl.BlockSpec((None, D_MODEL, tn), lambda i, j: (layer, 0, j)),
            pl.BlockSpec((None, D_MODEL, PROJ_SMALL), lambda i, j: (layer, 0, 0)),
        ],
        out_specs=(pl.BlockSpec((tm, tn), lambda i, j: (i, j)),
                   pl.BlockSpec((tm, PROJ_SMALL), lambda i, j: (i, 0))),
        scratch_shapes=[pltpu.VMEM((tm, D_MODEL), BF16)],
        compiler_params=pltpu.CompilerParams(
            dimension_semantics=("parallel", "arbitrary"), vmem_limit_bytes=VMEM_LIMIT),
        name="in_proj",
    )(x, nw, mod, mod, w_main, w_small)


def _outproj_body(*refs):
    n_mix = 4
    m_refs, (w_ref, x_ref, g_ref, o_ref) = refs[:2 * n_mix], refs[2 * n_mix:]

    def compute(stream):
        acc = None
        for p in range(n_mix):
            part = jnp.dot(m_refs[2 * p + stream][...], w_ref[p * GROUP_WIDTH:(p + 1) * GROUP_WIDTH, :],
                           preferred_element_type=F32)
            acc = part if acc is None else acc + part
        o_ref[...] = x_ref[...] + g_ref[...] * acc

    _on_stream(pl.program_id(0), PROJ_TM, compute)


def _outproj(parts, w, x, mod, layer):
    tm, tn = PROJ_TM, 1024
    in_specs = []
    for _ in parts:
        in_specs += list(_stream_specs(tm, GROUP_WIDTH))
    in_specs += [pl.BlockSpec((None, D_MODEL, tn), lambda i, j: (layer, 0, j)),
                 pl.BlockSpec((tm, tn), lambda i, j: (i, j)),
                 pl.BlockSpec((None, None, 1, tn), lambda i, j: (_seq_row(i, tm), 2, 0, j))]
    return pl.pallas_call(
        _outproj_body,
        out_shape=jax.ShapeDtypeStruct((N_TOK, D_MODEL), F32),
        grid=(N_TOK // tm, D_MODEL // tn),
        in_specs=in_specs,
        out_specs=pl.BlockSpec((tm, tn), lambda i, j: (i, j)),
        compiler_params=pltpu.CompilerParams(
            dimension_semantics=("parallel", "arbitrary"), vmem_limit_bytes=VMEM_LIMIT),
        name="out_proj",
    )(*(a for pair in parts for a in pair), w, x, mod)


FFN_TM = 512


FFN_PRE_ROWS = 64


def _ffn_body(x_ref, xn_ref, nw_ref, sc_ref, sh_ref, scn_ref, shn_ref, g_ref, w1_ref, w3_ref, w2_ref, fw_ref, *rest,
              final_norm):
    o_refs, (h_ref, acc_ref) = rest[:-2], rest[-2:]
    i, k = pl.program_id(0), pl.program_id(1)
    slot = i % 2

    @pl.when(k == 0)
    def _():
        acc_ref[...] = jnp.zeros_like(acc_ref)

    @pl.when((i == 0) & (k == 0))
    def _():
        h_ref[0] = _norm_mod(x_ref[...], nw_ref[...], sc_ref[...], sh_ref[...]).astype(BF16)

    h = h_ref[slot]
    a = jnp.dot(h, w1_ref[...], preferred_element_type=F32)
    b = jnp.dot(h, w3_ref[...], preferred_element_type=F32)
    g = (a * jax.nn.sigmoid(a) * b).astype(BF16)
    acc_ref[...] += jnp.dot(g, w2_ref[...], preferred_element_type=F32)

    r = jnp.minimum(k, FFN_TM // FFN_PRE_ROWS - 1)
    rows = pl.ds(pl.multiple_of(r * FFN_PRE_ROWS, FFN_PRE_ROWS), FFN_PRE_ROWS)
    h_ref[1 - slot, rows, :] = _norm_mod(xn_ref[rows, :], nw_ref[...], scn_ref[...], shn_ref[...]).astype(BF16)

    @pl.when(k == pl.num_programs(1) - 1)
    def _():
        y = x_ref[...] + g_ref[...] * acc_ref[...]
        if not final_norm:
            o_refs[0][...] = y
        else:
            ms = jnp.mean(y * y, axis=-1, keepdims=True)
            y = y * lax.rsqrt(ms + EPS) * fw_ref[...]

            def write(stream):
                o_refs[stream][...] = y

            _on_stream(i, FFN_TM, write)


def _ffn(x, nw, mod, w1, w3, w2, fw, layer):
    final_norm = layer == DEPTH - 1
    tm, th = FFN_TM, 512
    if final_norm:
        out_shape = (jax.ShapeDtypeStruct((N_CTX_TOK, D_MODEL), F32), jax.ShapeDtypeStruct((N_LAT_TOK, D_MODEL), F32))
        out_specs = _stream_specs(tm, D_MODEL)
    else:
        out_shape = jax.ShapeDtypeStruct((N_TOK, D_MODEL), F32)
        out_specs = pl.BlockSpec((tm, D_MODEL), lambda i, k: (i, 0))
    n_tiles = N_TOK // tm
    nxt = lambda i: jnp.minimum(i + 1, n_tiles - 1)
    return pl.pallas_call(
        functools.partial(_ffn_body, final_norm=final_norm),
        out_shape=out_shape,
        grid=(n_tiles, FFN_HIDDEN // th),
        in_specs=[
            pl.BlockSpec((tm, D_MODEL), lambda i, k: (i, 0)),
            pl.BlockSpec((tm, D_MODEL), lambda i, k: (nxt(i), 0)),
            pl.BlockSpec((1, D_MODEL), lambda i, k: (0, 0)),
            pl.BlockSpec((None, None, 1, D_MODEL), lambda i, k: (_seq_row(i, tm), 4, 0, 0)),
            pl.BlockSpec((None, None, 1, D_MODEL), lambda i, k: (_seq_row(i, tm), 3, 0, 0)),
            pl.BlockSpec((None, None, 1, D_MODEL), lambda i, k: (_seq_row(nxt(i), tm), 4, 0, 0)),
            pl.BlockSpec((None, None, 1, D_MODEL), lambda i, k: (_seq_row(nxt(i), tm), 3, 0, 0)),
            pl.BlockSpec((None, None, 1, D_MODEL), lambda i, k: (_seq_row(i, tm), 5, 0, 0)),
            pl.BlockSpec((None, D_MODEL, th), lambda i, k: (layer, 0, k)),
            pl.BlockSpec((None, D_MODEL, th), lambda i, k: (layer, 0, k)),
            pl.BlockSpec((None, th, D_MODEL), lambda i, k: (layer, k, 0)),
            pl.BlockSpec((1, D_MODEL), lambda i, k: (0, 0)),
        ],
        out_specs=out_specs,
        scratch_shapes=[pltpu.VMEM((2, tm, D_MODEL), BF16), pltpu.VMEM((tm, D_MODEL), F32)],
        compiler_params=pltpu.CompilerParams(
            dimension_semantics=("arbitrary", "arbitrary"), vmem_limit_bytes=VMEM_LIMIT_FFN),
        name="ffn",
    )(x, x, nw, mod, mod, mod, mod, mod, w1, w3, w2, fw)


def _axial_rope(l):
    n_rows = l // GRID_W
    t_row = jnp.repeat(jnp.arange(n_rows, dtype=F32), GRID_W)
    t_col = jnp.tile(jnp.arange(GRID_W, dtype=F32), n_rows)
    n_freq = HEAD_DIM // 4
    inv = ROPE_BASE ** (-jnp.arange(n_freq, dtype=F32) / n_freq)
    ang = jnp.concatenate([t_row[:, None] * inv, t_col[:, None] * inv], axis=-1)
    return jnp.cos(ang), jnp.sin(ang)


MIX_CHUNK = 128
DIRS = 2
NT_DIMS = (((1,), (1,)), ((), ()))
TN_DIMS = (((0,), (0,)), ((), ()))


def _dot(a, b):
    return jnp.dot(a.astype(BF16), b.astype(BF16), preferred_element_type=F32)


def _dot_nt(a, b):
    return lax.dot_general(a.astype(BF16), b.astype(BF16), NT_DIMS, preferred_element_type=F32)


def _dot_tn(a, b):
    return lax.dot_general(a.astype(BF16), b.astype(BF16), TN_DIMS, preferred_element_type=F32)


def _silu(x):
    return x * jax.nn.sigmoid(x)


def _seq_spec(seq_len, width, row_off, col_blk):
    return pl.BlockSpec((seq_len, width), lambda b: (row_off + b, col_blk))


def _mix_params(n_par):
    return pltpu.CompilerParams(dimension_semantics=("parallel",) * n_par, vmem_limit_bytes=VMEM_LIMIT)


def _ret_body(*refs, n_chunks, use_rope, has_s0):
    it = iter(refs)
    lg_ref, q_ref, k_ref, v_ref, g_ref = (next(it) for _ in range(5))
    cos_ref, sin_ref = (next(it), next(it)) if use_rope else (None, None)
    s0_ref = next(it) if has_s0 else None
    o_ref, sf_ref, acc_ref, s_ref, intra_ref, qd_ref, kd_ref, cd_ref = (next(it) for _ in range(8))
    C, H = MIX_CHUNK, N_HEADS

    row = lax.broadcasted_iota(jnp.int32, (C, C), 0)
    col = lax.broadcasted_iota(jnp.int32, (C, C), 1)
    rel = (row - col).astype(F32)
    pos = lax.broadcasted_iota(jnp.int32, (C, HEAD_DIM), 0).astype(F32)
    for d in range(DIRS):
        for h in range(H):
            lg = lg_ref[d, h]
            if d == 0:
                intra_ref[d, h] = jnp.where(rel >= 0, jnp.exp(jnp.maximum(rel, 0.0) * lg), 0.0)
                qd_ref[d, h] = jnp.exp((pos + 1.0) * lg)
                kd_ref[d, h] = jnp.exp((C - 1.0 - pos) * lg)
            else:
                intra_ref[d, h] = jnp.where(rel <= 0, jnp.exp(jnp.maximum(-rel, 0.0) * lg), 0.0)
                qd_ref[d, h] = jnp.exp((C - pos) * lg)
                kd_ref[d, h] = jnp.exp(pos * lg)
            cd_ref[d, h] = jnp.exp(jnp.full((C, HEAD_DIM), C, F32) * lg)
    if has_s0:
        s_ref[...] = s0_ref[...]
    else:
        s_ref[...] = jnp.zeros_like(s_ref)
    acc_ref[...] = jnp.zeros_like(acc_ref)

    units = [(d, h) for d in range(DIRS) for h in range(H)]

    def chunk_step(n, carry):
        rows_d = [pl.ds(pl.multiple_of(c * C, C), C) for c in (n, n_chunks - 1 - n)]
        q_l, k_l, v_l, sc_l = [], [], [], []
        for d, h in units:
            rows, cols = rows_d[d], slice(h * HEAD_DIM, (h + 1) * HEAD_DIM)
            q, k = q_ref[rows, cols], k_ref[rows, cols]
            if use_rope:
                cs, sn = cos_ref[rows, :], sin_ref[rows, :]
                q = q * cs + pltpu.roll(q, HEAD_DIM // 2, 1) * sn
                k = k * cs + pltpu.roll(k, HEAD_DIM // 2, 1) * sn
            k = k * HEAD_DIM ** -0.5
            q_l.append(q)
            k_l.append(k)
            v_l.append(v_ref[rows, cols].astype(BF16))
            sc_l.append(_dot_nt(q, k) * intra_ref[d, h])
        for i, (d, h) in enumerate(units):
            cols = slice(h * HEAD_DIM, (h + 1) * HEAD_DIM)
            s = s_ref[d, h]
            acc_ref[rows_d[d], cols] += _dot(jnp.concatenate([sc_l[i], q_l[i] * qd_ref[d, h]], axis=1),
                                             jnp.concatenate([v_l[i], s.astype(BF16)], axis=0))
            s_ref[d, h] = cd_ref[d, h] * s + _dot_tn(k_l[i] * kd_ref[d, h], v_l[i])
        return carry

    lax.fori_loop(0, n_chunks, chunk_step, 0)
    sf_ref[...] = s_ref[...]

    def finish(n, carry):
        rows = pl.ds(pl.multiple_of(n * C, C), C)
        for h in range(H):
            cols = slice(h * HEAD_DIM, (h + 1) * HEAD_DIM)
            o = acc_ref[rows, cols]
            mu = jnp.mean(o, axis=-1, keepdims=True)
            oc = o - mu
            y = oc * lax.rsqrt(jnp.mean(oc * oc, axis=-1, keepdims=True) + EPS)
            o_ref[rows, cols] = (y * _silu(g_ref[rows, cols])).astype(BF16)
        return carry

    lax.fori_loop(0, n_chunks, finish, 0)


def _retention_pallas(pm, log_gamma, n_seq, seq_len, row_off, rope2, s0, layer):
    use_rope, has_s0 = rope2 is not None, s0 is not None
    st_shape = (DIRS, N_HEADS, HEAD_DIM, HEAD_DIM)
    in_specs = [pl.BlockSpec(memory_space=pltpu.SMEM)]
    in_specs += [_seq_spec(seq_len, GROUP_WIDTH, row_off, cb) for cb in range(4)]
    args = [log_gamma, pm, pm, pm, pm]
    if use_rope:
        in_specs += [pl.BlockSpec((seq_len, HEAD_DIM), lambda b: (0, 0))] * 2
        args += list(rope2)
    if has_s0:
        in_specs.append(pl.BlockSpec((None, None) + st_shape, lambda b: (b, layer, 0, 0, 0, 0)))
        args.append(s0)
    return pl.pallas_call(
        functools.partial(_ret_body, n_chunks=seq_len // MIX_CHUNK, use_rope=use_rope, has_s0=has_s0),
        out_shape=(jax.ShapeDtypeStruct((n_seq * seq_len, GROUP_WIDTH), BF16),
                   jax.ShapeDtypeStruct((n_seq,) + st_shape, F32)),
        grid=(n_seq,),
        in_specs=in_specs,
        out_specs=(pl.BlockSpec((seq_len, GROUP_WIDTH), lambda b: (b, 0)),
                   pl.BlockSpec((None,) + st_shape, lambda b: (b, 0, 0, 0, 0))),
        scratch_shapes=[pltpu.VMEM((seq_len, GROUP_WIDTH), F32), pltpu.VMEM(st_shape, F32),
                        pltpu.VMEM((DIRS, N_HEADS, MIX_CHUNK, MIX_CHUNK), F32),
                        pltpu.VMEM((DIRS, N_HEADS, MIX_CHUNK, HEAD_DIM), F32),
                        pltpu.VMEM((DIRS, N_HEADS, MIX_CHUNK, HEAD_DIM), F32),
                        pltpu.VMEM((DIRS, N_HEADS, MIX_CHUNK, HEAD_DIM), F32)],
        compiler_params=_mix_params(1),
        name="retention",
    )(*args)


def _rope_tables(l):
    cos, sin = _axial_rope(l)
    return jnp.concatenate([cos, cos], axis=-1), jnp.concatenate([-sin, sin], axis=-1)


S5_HALF_G = S5_GROUPS // 2
S5_HALF_U = S5_HALF_G * S5_CH
S5_HALF_X = S5_HALF_G * S5_N
S5_X = S5_GROUPS * S5_N
S5_TC = 512
S5_BLK = 8
S5_TABS = 4


def _s5_tables(a_re, a_im, log_step, b_re, b_im, c_re, c_im):
    dt = jnp.exp(log_step)[..., None]
    mag = jnp.exp(a_re * dt)
    ab_re = mag * jnp.cos(a_im * dt)
    ab_im = mag * jnp.sin(a_im * dt)
    den = a_re * a_re + a_im * a_im
    nr = ab_re - 1.0
    f_re = (nr * a_re + ab_im * a_im) / den
    f_im = (ab_im * a_re - nr * a_im) / den
    bb_re = f_re[..., None] * b_re - f_im[..., None] * b_im
    bb_im = f_re[..., None] * b_im + f_im[..., None] * b_re
    eye = jnp.eye(S5_HALF_G, dtype=F32)

    def in_mat(bb):
        bb = bb.reshape(DIRS, 2, S5_HALF_G, S5_N, S5_CH)
        return jnp.einsum('dhgnc,gk->dhgckn', bb, eye).reshape(DIRS, 2, S5_HALF_U, S5_HALF_X)

    def out_mat(cc):
        cc = cc.reshape(DIRS, 2, S5_HALF_G, S5_CH, S5_N)
        return jnp.einsum('dhgcn,gk->dhgnkc', cc, eye).reshape(DIRS, 2, S5_HALF_X, S5_HALF_U)

    bm = jnp.concatenate([in_mat(bb_re), in_mat(bb_im)], axis=-1).astype(BF16)
    cm = jnp.concatenate([out_mat(c_re), -out_mat(c_im)], axis=-2).astype(BF16)
    t = jnp.arange(S5_BLK, dtype=F32)
    order = jnp.stack([t, S5_BLK - 1.0 - t])
    shifts = 2.0 ** jnp.arange(S5_TABS - 1, dtype=F32)
    expo = jnp.concatenate([jnp.where(order[:, None, :] >= shifts[None, :, None], shifts[None, :, None], jnp.nan),
                            order[:, None, :] + 1.0], axis=1)
    live = ~jnp.isnan(expo)
    e = jnp.where(live, expo, 0.0)[..., None]
    adt_re = (a_re * dt).reshape(DIRS, 1, 1, S5_X)
    adt_im = (a_im * dt).reshape(DIRS, 1, 1, S5_X)
    pmag = jnp.where(live[..., None], jnp.exp(e * adt_re), 0.0)
    pw_re = pmag * jnp.cos(e * adt_im)
    pw_im = pmag * jnp.sin(e * adt_im)
    return bm, cm, pw_re, pw_im


def _gelu_tanh(x):
    return 0.5 * x * (1.0 + jnp.tanh(math.sqrt(2.0 / math.pi) * (x + 0.044715 * (x * x * x))))


def _s5_body(*refs, seq_len, has_s0):
    it = iter(refs)
    u_ref, bm_ref, cm_ref, pwr_ref, pwi_ref, d_ref, gw_ref, gb_ref = (next(it) for _ in range(8))
    x0r_ref, x0i_ref = (next(it), next(it)) if has_s0 else (None, None)
    o_ref, sfr_ref, sfi_ref, y_ref, xr_ref, xi_ref, xb_ref = (next(it) for _ in range(7))
    tc = min(seq_len, S5_TC)
    n_tiles = seq_len // tc
    n_pair = tc // (2 * S5_BLK)

    y_ref[...] = d_ref[...] * u_ref[...]
    for d in range(DIRS):
        last = S5_BLK - 1 if d == 0 else 0
        for hf in range(2):
            xs = slice(hf * S5_HALF_X, (hf + 1) * S5_HALF_X)
            us = slice(hf * S5_HALF_U, (hf + 1) * S5_HALF_U)

            def scan_block(x_in, carry, d=d, xs=xs, last=last):
                xr, xi = x_in
                car_re, car_im = carry
                for k in range(S5_TABS - 1):
                    s = 1 << k
                    shift = s if d == 0 else S5_BLK - s
                    p_re, p_im = pwr_ref[d, k, :, xs], pwi_ref[d, k, :, xs]
                    sr, si = pltpu.roll(xr, shift, 0), pltpu.roll(xi, shift, 0)
                    xr, xi = xr + p_re * sr - p_im * si, xi + p_re * si + p_im * sr
                p_re, p_im = pwr_ref[d, S5_TABS - 1, :, xs], pwi_ref[d, S5_TABS - 1, :, xs]
                xr, xi = xr + p_re * car_re - p_im * car_im, xi + p_re * car_im + p_im * car_re
                return (xr, xi), (xr[last:last + 1, :], xi[last:last + 1, :])

            def scan_pair(j, carry, d=d, scan_block=scan_block):
                pair = j if d == 0 else n_pair - 1 - j
                rows = pl.ds(pl.multiple_of(pair * 2 * S5_BLK, 2 * S5_BLK), 2 * S5_BLK)
                xr2, xi2 = xr_ref[rows, :], xi_ref[rows, :]
                halves = [(xr2[:S5_BLK], xi2[:S5_BLK]), (xr2[S5_BLK:], xi2[S5_BLK:])]
                order = (0, 1) if d == 0 else (1, 0)
                out = [None, None]
                for idx in order:
                    out[idx], carry = scan_block(halves[idx], carry)
                xb_ref[rows, :S5_HALF_X] = jnp.concatenate([out[0][0], out[1][0]], axis=0).astype(BF16)
                xb_ref[rows, S5_HALF_X:] = jnp.concatenate([out[0][1], out[1][1]], axis=0).astype(BF16)
                return carry

            def scan_tile(i, carry, d=d, hf=hf, us=us, scan_pair=scan_pair):
                tile = i if d == 0 else n_tiles - 1 - i
                rows = pl.ds(pl.multiple_of(tile * tc, tc), tc)
                bu = _dot(u_ref[rows, us], bm_ref[d, hf])
                xr_ref[...] = bu[:, :S5_HALF_X]
                xi_ref[...] = bu[:, S5_HALF_X:]
                carry = lax.fori_loop(0, n_pair, scan_pair, carry)
                y_ref[rows, us] += jnp.dot(xb_ref[...], cm_ref[d, hf], preferred_element_type=F32)
                return carry

            if has_s0:
                carry0 = (x0r_ref[d, :, xs], x0i_ref[d, :, xs])
            else:
                carry0 = (jnp.zeros((1, S5_HALF_X), F32), jnp.zeros((1, S5_HALF_X), F32))
            car_re, car_im = lax.fori_loop(0, n_tiles, scan_tile, carry0)
            sfr_ref[d, :, xs] = car_re
            sfi_ref[d, :, xs] = car_im

    z = _gelu_tanh(y_ref[...])
    o_ref[...] = (z * jax.nn.sigmoid(_dot(z, gw_ref[...]) + gb_ref[...])).astype(BF16)


def _s5_pallas(pm, tables, s5_d, glu_w, glu_b, n_seq, seq_len, row_off, x0, layer):
    bm, cm, pw_re, pw_im = tables
    has_s0 = x0 is not None
    full = lambda shape: pl.BlockSpec(shape, lambda b: (0,) * len(shape))
    of_layer = lambda t: pl.BlockSpec((None,) + t.shape[1:], lambda b: (layer,) + (0,) * (t.ndim - 1))
    in_specs = [_seq_spec(seq_len, GROUP_WIDTH, row_off, 13),
                of_layer(bm), of_layer(cm), of_layer(pw_re), of_layer(pw_im),
                full((1, GROUP_WIDTH)), full((GROUP_WIDTH, GROUP_WIDTH)), full((1, GROUP_WIDTH))]
    args = [pm, bm, cm, pw_re, pw_im, s5_d.reshape(1, GROUP_WIDTH), glu_w.astype(BF16),
            glu_b.reshape(1, GROUP_WIDTH)]
    if has_s0:
        in_specs += [pl.BlockSpec((None, None, DIRS, 1, S5_X), lambda b: (b, layer, 0, 0, 0))] * 2
        args += list(x0)
    st = jax.ShapeDtypeStruct((n_seq, DIRS, 1, S5_X), F32)
    st_spec = pl.BlockSpec((None, DIRS, 1, S5_X), lambda b: (b, 0, 0, 0))
    tc = min(seq_len, S5_TC)
    return pl.pallas_call(
        functools.partial(_s5_body, seq_len=seq_len, has_s0=has_s0),
        out_shape=(jax.ShapeDtypeStruct((n_seq * seq_len, GROUP_WIDTH), BF16), st, st),
        grid=(n_seq,),
        in_specs=in_specs,
        out_specs=(pl.BlockSpec((seq_len, GROUP_WIDTH), lambda b: (b, 0)), st_spec, st_spec),
        scratch_shapes=[pltpu.VMEM((seq_len, GROUP_WIDTH), F32),
                        pltpu.VMEM((tc, S5_HALF_X), F32), pltpu.VMEM((tc, S5_HALF_X), F32),
                        pltpu.VMEM((tc, 2 * S5_HALF_X), BF16)],
        compiler_params=_mix_params(1),
        name="s5",
    )(*args)


GLA_LEVELS = 7


def _chunk_cumsum_rows(x, rowi):
    s = 1
    while s < MIX_CHUNK:
        x = x + jnp.where(rowi >= s, pltpu.roll(x, s, 0), 0.0)
        s *= 2
    return x


def _gla_body(*refs, n_chunks, has_s0):
    it = iter(refs)
    q_ref, f0_ref, f1_ref, i_ref, g_ref, lb_ref, nw_ref = (next(it) for _ in range(7))
    s0_ref = next(it) if has_s0 else None
    o_ref, sf_ref, acc_ref, s_ref, code_ref = (next(it) for _ in range(5))
    C, H = MIX_CHUNK, N_HEADS
    f_refs = (f0_ref, f1_ref)

    rowi = lax.broadcasted_iota(jnp.int32, (C, HEAD_DIM), 0)
    ri = lax.broadcasted_iota(jnp.int32, (C, C), 0)
    ci = lax.broadcasted_iota(jnp.int32, (C, C), 1)
    top_bit = 31 - lax.clz(ri ^ ci)
    code_ref[...] = jnp.where(ri > ci, top_bit, jnp.where(ri < ci, -1 - top_bit, GLA_LEVELS))
    if has_s0:
        s_ref[...] = s0_ref[...]
    else:
        s_ref[...] = jnp.zeros_like(s_ref)
    acc_ref[...] = jnp.zeros_like(acc_ref)

    def chunk_step(n, carry):
        for d in range(DIRS):
            c = n if d == 0 else n_chunks - 1 - n
            rows = pl.ds(pl.multiple_of(c * C, C), C)
            for h in range(H):
                cols = slice(h * HEAD_DIM, (h + 1) * HEAD_DIM)
                code = code_ref[...]
                q = _silu(q_ref[rows, cols]) * HEAD_DIM ** -0.5
                v = i_ref[rows, cols].astype(BF16)
                fx = f_refs[d][rows, cols]
                lb = lb_ref[d:d + 1, cols]
                sig = 1.0 / (1.0 + jnp.exp(-fx))
                logf = jnp.log2(jnp.maximum(lb, LB_FLOOR) + (1.0 - lb) * sig)
                k = (1.0 - lb) * (1.0 - sig)
                cum = _chunk_cumsum_rows(logf, rowi)
                own = cum
                attn = jnp.where(code == GLA_LEVELS, _dot_nt(q, k), 0.0)
                for lvl in range(GLA_LEVELS):
                    m = 1 << lvl
                    prev = pltpu.roll(own, m, 0)
                    pre = jnp.minimum(cum - prev, 0.0)
                    suf = own - cum
                    if d == 0:
                        sc = _dot_nt(q * jnp.exp2(pre), k * jnp.exp2(suf))
                        hit = code == lvl
                    else:
                        sc = _dot_nt(q * jnp.exp2(suf + logf), k * jnp.exp2(jnp.minimum(pre - logf, 0.0)))
                        hit = code == -1 - lvl
                    attn = jnp.where(hit, sc, attn)
                    own = jnp.where(((rowi >> lvl) & 1) == 0, pltpu.roll(own, C - m, 0), own)
                tot = own
                st = s_ref[d, h]
                if d == 0:
                    q_in, k_out = q * jnp.exp2(cum), k * jnp.exp2(tot - cum)
                else:
                    q_in, k_out = q * jnp.exp2(tot - cum + logf), k * jnp.exp2(cum - logf)
                acc_ref[rows, cols] += _dot(attn, v) + _dot_nt(q_in, st)
                s_ref[d, h] = jnp.exp2(tot[0:1, :]) * st + _dot_tn(v, k_out)
        return carry

    lax.fori_loop(0, n_chunks, chunk_step, 0)
    sf_ref[...] = s_ref[...]

    def finish(n, carry):
        rows = pl.ds(pl.multiple_of(n * C, C), C)
        for h in range(H):
            cols = slice(h * HEAD_DIM, (h + 1) * HEAD_DIM)
            o = acc_ref[rows, cols]
            y = o * lax.rsqrt(jnp.mean(o * o, axis=-1, keepdims=True) + EPS) * nw_ref[...]
            o_ref[rows, cols] = (y * _silu(g_ref[rows, cols])).astype(BF16)
        return carry

    lax.fori_loop(0, n_chunks, finish, 0)


def _gla_pallas(pm, lower_bound, norm_w, n_seq, seq_len, row_off, s0_t, layer):
    has_s0 = s0_t is not None
    st_shape = (DIRS, N_HEADS, HEAD_DIM, HEAD_DIM)
    in_specs = [_seq_spec(seq_len, GROUP_WIDTH, row_off, cb) for cb in (8, 9, 10, 11, 12)]
    in_specs += [pl.BlockSpec((DIRS, GROUP_WIDTH), lambda b: (0, 0)), pl.BlockSpec((1, HEAD_DIM), lambda b: (0, 0))]
    args = [pm] * 5 + [lower_bound, norm_w.reshape(1, HEAD_DIM)]
    if has_s0:
        in_specs.append(pl.BlockSpec((None, None) + st_shape, lambda b: (b, layer, 0, 0, 0, 0)))
        args.append(s0_t)
    return pl.pallas_call(
        functools.partial(_gla_body, n_chunks=seq_len // MIX_CHUNK, has_s0=has_s0),
        out_shape=(jax.ShapeDtypeStruct((n_seq * seq_len, GROUP_WIDTH), BF16),
                   jax.ShapeDtypeStruct((n_seq,) + st_shape, F32)),
        grid=(n_seq,),
        in_specs=in_specs,
        out_specs=(pl.BlockSpec((seq_len, GROUP_WIDTH), lambda b: (b, 0)),
                   pl.BlockSpec((None,) + st_shape, lambda b: (b, 0, 0, 0, 0))),
        scratch_shapes=[pltpu.VMEM((seq_len, GROUP_WIDTH), F32), pltpu.VMEM(st_shape, F32),
                        pltpu.VMEM((MIX_CHUNK, MIX_CHUNK), jnp.int32)],
        compiler_params=_mix_params(1),
        name="hgrn2",
    )(*args)


GDN_BLOCK_BITS = 4
GDN_MERGES = 3
N_GATES = DIRS * N_HEADS


def _softplus(x):
    return jnp.maximum(x, 0.0) + jnp.log1p(jnp.exp(-jnp.abs(x)))


def _gdn_body(*refs, seq_len, n_sub, has_s0):
    it = iter(refs)
    (q_ref, k_ref, v_ref, g_ref, ps_ref, pst_ref, cw_ref, prow_ref, pcol_ref, nw_ref) = (next(it) for _ in range(10))
    s0_ref = next(it) if has_s0 else None
    (o_ref, sf_ref, acc_ref, s_ref, qn_ref, kn_ref, vn_ref, gate_ref, gatet_ref, blev_ref) = (
        next(it) for _ in range(10))
    C, H, L = MIX_CHUNK, N_HEADS, seq_len
    R = n_sub * L
    n_chunks = L // C
    w = GROUP_WIDTH

    pos = lax.broadcasted_iota(jnp.int32, (R, HEAD_DIM), 0) & (L - 1)
    for part, (src, dst) in enumerate(((q_ref, qn_ref), (k_ref, kn_ref), (v_ref, vn_ref))):
        for h in range(H):
            cols = slice(h * HEAD_DIM, (h + 1) * HEAD_DIM)
            wc = slice(part * w + h * HEAD_DIM, part * w + (h + 1) * HEAD_DIM)
            x = src[:, cols]
            x_prev = jnp.where(pos >= 1, pltpu.roll(x, 1, 0), 0.0)
            x_next = jnp.where(pos < L - 1, pltpu.roll(x, R - 1, 0), 0.0)
            y = _silu(cw_ref[0:1, wc] * x_prev + cw_ref[1:2, wc] * x + cw_ref[2:3, wc] * x_next)
            if part < 2:
                y = y * lax.rsqrt(jnp.sum(y * y, axis=-1, keepdims=True) + EPS)
            if part == 0:
                y = y * HEAD_DIM ** -0.5
            dst[:, cols] = y

    rowi = lax.broadcasted_iota(jnp.int32, (C, HEAD_DIM), 0)
    lane = lax.broadcasted_iota(jnp.int32, (C, HEAD_DIM), 1)
    lane_t = lax.broadcasted_iota(jnp.int32, (2 * N_GATES, C), 1)
    row_t = lax.broadcasted_iota(jnp.int32, (2 * N_GATES, C), 0)
    for c in range(R // C):
        rows = slice(c * C, (c + 1) * C)
        a = ps_ref[rows, :]
        la = -jnp.exp(prow_ref[0:1, :]) * _softplus(a + prow_ref[1:2, :])
        pre = _chunk_cumsum_rows(la, rowi)
        tot = jnp.broadcast_to(pre[C - 1:C, :], (C, HEAD_DIM))
        g = jnp.where(lane < N_HEADS, pre, tot - pre + la)
        gate_ref[rows, :] = jnp.where(lane < N_GATES, g,
                                      jnp.where(lane < 2 * N_GATES, jax.nn.sigmoid(a),
                                                pltpu.roll(tot, 2 * N_GATES, 1)))
        at = pst_ref[:, rows]
        lat = -jnp.exp(pcol_ref[:, 0:1]) * _softplus(at + pcol_ref[:, 1:2])
        pre_t, s = lat, 1
        while s < C:
            pre_t = pre_t + jnp.where(lane_t >= s, pltpu.roll(pre_t, s, 1), 0.0)
            s *= 2
        tot_t = jnp.broadcast_to(pre_t[:, C - 1:C], (2 * N_GATES, C))
        g_t = jnp.where(row_t < N_HEADS, pre_t, tot_t - pre_t + lat)
        gatet_ref[:, rows] = jnp.where(row_t < N_GATES, g_t, jax.nn.sigmoid(at))

    ri = lax.broadcasted_iota(jnp.int32, (C, C), 0)
    ci = lax.broadcasted_iota(jnp.int32, (C, C), 1)
    bx = (ri >> GDN_BLOCK_BITS) ^ (ci >> GDN_BLOCK_BITS)
    blev_ref[...] = jnp.where(bx == 0, 0, 32 - lax.clz(bx))
    if has_s0:
        s_ref[...] = s0_ref[...]
    else:
        s_ref[...] = jnp.zeros_like(s_ref)
    acc_ref[...] = jnp.zeros_like(acc_ref)

    units = [(s, d, h) for s in range(n_sub) for d in range(DIRS) for h in range(H)]

    def chunk_step(n, carry):
        eye = jnp.where(ri == ci, 1.0, 0.0)
        blev = blev_ref[...]
        rows_sd = [[pl.ds(pl.multiple_of(s * L + c * C, C), C) for c in (n, n_chunks - 1 - n)] for s in range(n_sub)]
        incl_d = [ri >= ci, ri <= ci]
        strict_d = [ri > ci, ri < ci]
        q_l, et_l, ek_l, rhs_l, a_l, attn_l = ([] for _ in range(6))
        for s, d, h in units:
            rows, cols, cg = rows_sd[s][d], slice(h * HEAD_DIM, (h + 1) * HEAD_DIM), d * N_HEADS + h
            q, k, v = qn_ref[rows, cols], kn_ref[rows, cols], vn_ref[rows, cols]
            g_i = jnp.broadcast_to(gate_ref[rows, cg:cg + 1], (C, C))
            b_i = jnp.broadcast_to(gate_ref[rows, N_GATES + cg:N_GATES + cg + 1], (C, C))
            tot = jnp.broadcast_to(gate_ref[rows, 2 * N_GATES + cg:2 * N_GATES + cg + 1], (C, C))
            g_j = gatet_ref[cg:cg + 1, rows]
            decay = jnp.where(incl_d[d], jnp.exp(jnp.minimum(g_i - g_j, 0.0)), 0.0)
            e_g = jnp.exp(g_i)
            kb = k.astype(BF16)
            a_l.append(jnp.where(strict_d[d], _dot_nt(kb, kb) * b_i * decay, 0.0))
            attn_l.append(_dot_nt(q, kb) * decay)
            rhs_l.append(jnp.concatenate([v * b_i, k * (b_i * e_g)], axis=1).astype(BF16))
            q_l.append(q * e_g)
            ek_l.append(k * jnp.exp(tot - g_i))
            et_l.append(jnp.exp(tot))
        b_l = [-jnp.where(blev == 0, a, 0.0) for a in a_l]
        p_l = [eye + b for b in b_l]
        for _ in range(GDN_BLOCK_BITS - 1):
            b_l = [_dot(b, b) for b in b_l]
            p_l = [p + _dot(p, b) for p, b in zip(p_l, b_l)]
        for lvl in range(1, GDN_MERGES + 1):
            ep_l = [_dot(jnp.where(blev == lvl, a, 0.0), p) for a, p in zip(a_l, p_l)]
            p_l = [p - _dot(p, ep) for p, ep in zip(p_l, ep_l)]
        uw_l = [_dot(p, rhs) for p, rhs in zip(p_l, rhs_l)]
        s_l = [s_ref[s * DIRS + d, h] for s, d, h in units]
        vn_l = [uw[:, :HEAD_DIM] - _dot(uw[:, HEAD_DIM:], st) for uw, st in zip(uw_l, s_l)]
        for i, (s, d, h) in enumerate(units):
            cols = slice(h * HEAD_DIM, (h + 1) * HEAD_DIM)
            acc_ref[rows_sd[s][d], cols] += _dot(jnp.concatenate([q_l[i], attn_l[i]], axis=1),
                                                 jnp.concatenate([s_l[i], vn_l[i]], axis=0))
            s_ref[s * DIRS + d, h] = s_l[i] * et_l[i] + _dot_tn(ek_l[i], vn_l[i])
        return carry

    lax.fori_loop(0, n_chunks, chunk_step, 0)
    sf_ref[...] = s_ref[...]

    def finish(n, carry):
        rows = pl.ds(pl.multiple_of(n * C, C), C)
        for h in range(H):
            cols = slice(h * HEAD_DIM, (h + 1) * HEAD_DIM)
            o = acc_ref[rows, cols]
            y = o * lax.rsqrt(jnp.mean(o * o, axis=-1, keepdims=True) + EPS) * nw_ref[...]
            o_ref[rows, cols] = (y * _silu(g_ref[rows, cols])).astype(BF16)
        return carry

    lax.fori_loop(0, R // C, finish, 0)


GDN_CTX_SUB = 2


def _gdn_pallas(pm, ps, ps_t, conv_w, a_log, dt_bias, norm_w, n_seq, seq_len, row_off, s0, layer):
    has_s0 = s0 is not None
    n_sub = 1 if has_s0 else GDN_CTX_SUB
    n_seq, seq_len, row_off = n_seq // n_sub, seq_len * n_sub, row_off // n_sub
    st_shape = (n_sub * DIRS, N_HEADS, HEAD_DIM, HEAD_DIM)
    par = jnp.stack([a_log.reshape(N_GATES), dt_bias.reshape(N_GATES)])
    par_row = jnp.pad(par, ((0, 0), (0, HEAD_DIM - N_GATES)))
    par_col = jnp.pad(par.T, ((0, N_GATES), (0, 0)))
    full = lambda shape: pl.BlockSpec(shape, lambda b: (0,) * len(shape))
    in_specs = [_seq_spec(seq_len, GROUP_WIDTH, row_off, cb) for cb in (4, 5, 6, 7)]
    in_specs += [_seq_spec(seq_len, PROJ_SMALL, row_off, 0),
                 pl.BlockSpec((2 * N_GATES, seq_len), lambda b: (0, row_off + b)),
                 full((GDN_CONV, 3 * GROUP_WIDTH)), full((2, HEAD_DIM)), full((2 * N_GATES, 2)), full((1, HEAD_DIM))]
    args = [pm] * 4 + [ps, ps_t, conv_w, par_row, par_col, norm_w.reshape(1, HEAD_DIM)]
    if has_s0:
        in_specs.append(pl.BlockSpec((None, None) + st_shape, lambda b: (b, layer, 0, 0, 0, 0)))
        args.append(s0)
    seq_f32 = lambda width: pltpu.VMEM((seq_len, width), F32)
    out, s_fin = pl.pallas_call(
        functools.partial(_gdn_body, seq_len=seq_len // n_sub, n_sub=n_sub, has_s0=has_s0),
        out_shape=(jax.ShapeDtypeStruct((n_seq * seq_len, GROUP_WIDTH), BF16),
                   jax.ShapeDtypeStruct((n_seq,) + st_shape, F32)),
        grid=(n_seq,),
        in_specs=in_specs,
        out_specs=(pl.BlockSpec((seq_len, GROUP_WIDTH), lambda b: (b, 0)),
                   pl.BlockSpec((None,) + st_shape, lambda b: (b, 0, 0, 0, 0))),
        scratch_shapes=[seq_f32(GROUP_WIDTH), pltpu.VMEM(st_shape, F32),
                        seq_f32(GROUP_WIDTH), seq_f32(GROUP_WIDTH), seq_f32(GROUP_WIDTH),
                        seq_f32(HEAD_DIM), pltpu.VMEM((2 * N_GATES, seq_len), F32),
                        pltpu.VMEM((MIX_CHUNK, MIX_CHUNK), jnp.int32)],
        compiler_params=_mix_params(1),
        name="gated_delta",
    )(*args)
    return out, s_fin.reshape(n_seq * n_sub, DIRS, N_HEADS, HEAD_DIM, HEAD_DIM)


def kernel(x_prompt, x_sample, state_ret, state_gdn, state_hgrn, state_s5_re, state_s5_im, c, c_ctx, norm1_w, norm2_w, final_norm_w, ada_w, ada_b, in_proj, out_proj, ret_decay_logit, gdn_conv, gdn_a_log, gdn_dt_bias, gdn_norm_w, hg_lb_param, hg_norm_w, s5_a_re, s5_a_im, s5_b_re, s5_b_im, s5_c_re, s5_c_im, s5_log_step, s5_d, s5_glu_w, s5_glu_b, ffn_w1, ffn_w3, ffn_w2):
    lb_soft = jax.nn.softmax(hg_lb_param, axis=0)
    lower_bounds = jnp.cumsum(lb_soft, axis=0) - lb_soft[0]
    rope2 = _rope_tables(DEC_SEQ)

    cvec = jnp.zeros((N_SEQ_ROWS, D_MODEL), F32).at[0].set(c_ctx).at[1:1 + DEC_BATCH].set(c)
    mod_all = _ada(cvec, ada_w, ada_b).reshape(DEPTH, N_SEQ_ROWS, N_MOD, 1, D_MODEL)

    gate0 = 8 * GROUP_WIDTH
    w_in = in_proj.astype(BF16)
    w_main = jnp.concatenate([w_in[:, :, :gate0], w_in[:, :, gate0 + 2 * N_GATES:]], axis=-1)
    w_small = jnp.pad(w_in[:, :, gate0:gate0 + 2 * N_GATES], ((0, 0), (0, 0), (0, PROJ_SMALL - 2 * N_GATES)))
    w_out = out_proj.astype(BF16)
    w1, w3, w2 = ffn_w1.astype(BF16), ffn_w3.astype(BF16), ffn_w2.astype(BF16)

    x = jnp.concatenate([x_prompt.reshape(N_CTX_TOK, D_MODEL), x_sample.reshape(N_LAT_TOK, D_MODEL)], axis=0)
    hgrn_t = jnp.swapaxes(state_hgrn, -1, -2)
    s5_x0 = (state_s5_re.reshape(DEC_BATCH, DEPTH, DIRS, 1, S5_X), state_s5_im.reshape(DEC_BATCH, DEPTH, DIRS, 1, S5_X))
    lat_off = N_CTX_TOK // DEC_SEQ
    tables = jax.vmap(_s5_tables)(s5_a_re, s5_a_im, s5_log_step, s5_b_re, s5_b_im, s5_c_re, s5_c_im)
    ctx_states = []
    for i in range(DEPTH):
        mod = mod_all[i]
        pm, ps = _inproj(x, norm1_w[i][None], mod, w_main, w_small, i)
        ps_t = ps[:, :2 * N_GATES].T
        lg = jax.nn.log_sigmoid(ret_decay_logit[i])
        ctx = (BATCH, SEQ, 0)
        lat = (DEC_BATCH, DEC_SEQ, lat_off)

        ret_c, rs = _retention_pallas(pm, lg, *ctx, None, None, i)
        ret_l, _ = _retention_pallas(pm, lg, *lat, rope2, state_ret, i)
        gdn_args = (pm, ps, ps_t, gdn_conv[i], gdn_a_log[i], gdn_dt_bias[i], gdn_norm_w[i])
        gdn_c, gs = _gdn_pallas(*gdn_args, *ctx, None, i)
        gdn_l, _ = _gdn_pallas(*gdn_args, *lat, state_gdn, i)
        hg_c, hs = _gla_pallas(pm, lower_bounds[i], hg_norm_w[i], *ctx, None, i)
        hg_l, _ = _gla_pallas(pm, lower_bounds[i], hg_norm_w[i], *lat, hgrn_t, i)
        s5_args = (pm, tables, s5_d[i], s5_glu_w[i], s5_glu_b[i])
        s5_c, xr, xi = _s5_pallas(*s5_args, *ctx, None, i)
        s5_l, _, _ = _s5_pallas(*s5_args, *lat, s5_x0, i)
        ctx_states.append((rs, gs, jnp.swapaxes(hs, -1, -2), xr.reshape(BATCH, DIRS, S5_GROUPS, S5_N),
                           xi.reshape(BATCH, DIRS, S5_GROUPS, S5_N)))

        parts = ((ret_c, ret_l), (gdn_c, gdn_l), (hg_c, hg_l), (s5_c, s5_l))
        x = _outproj(parts, w_out, x, mod, i)
        x = _ffn(x, norm2_w[i][None], mod, w1, w3, w2, final_norm_w[None], i)

    y_prompt = x[0].reshape(BATCH, SEQ, D_MODEL)
    y_sample = x[1].reshape(DEC_BATCH, DEC_SEQ, D_MODEL)
    new_states = tuple(jnp.stack([s[j] for s in ctx_states], axis=1) for j in range(5))
    return (y_prompt, y_sample) + new_states
```

```python
import functools
import math

import jax
import jax.numpy as jnp
from jax import lax
from jax.experimental import pallas as pl
from jax.experimental.pallas import tpu as pltpu

F32 = jnp.float32
BF16 = jnp.bfloat16

D_MODEL = 2048
BATCH = 16
SEQ = 256
DEPTH = 2
DEC_BATCH = 8
DEC_SEQ = 1024
GRID_W = 64
HEAD_DIM = 128
GROUP_WIDTH = 512
N_HEADS = 4
S5_CH = 16
S5_GROUPS = 32
S5_N = 64
GDN_CONV = 3
CHUNK = 64
HG_CHUNK = 16
ROPE_BASE = 10000.0
FFN_HIDDEN = 5632
N_MOD = 6
EPS = 1e-6
LB_FLOOR = 1e-30

N_CTX_TOK = BATCH * SEQ
N_LAT_TOK = DEC_BATCH * DEC_SEQ
N_TOK = N_CTX_TOK + N_LAT_TOK
N_SEQ_ROWS = 16
PROJ_MAIN = 14 * GROUP_WIDTH
PROJ_SMALL = 128
VMEM_LIMIT = 56 * 1024 * 1024


def _seq_row(tile, tm):
    n_ctx = N_CTX_TOK // tm
    per_lat = DEC_SEQ // tm
    return jnp.where(tile < n_ctx, 0, 1 + (tile - n_ctx) // per_lat)


def _stream_specs(tm, width, n_col=1):
    n_ctx = N_CTX_TOK // tm

    def ctx_map(i, j):
        return jnp.minimum(i, n_ctx - 1), (jnp.where(i < n_ctx, j, n_col - 1) if n_col > 1 else 0)

    def lat_map(i, j):
        return jnp.maximum(i - n_ctx, 0), (jnp.where(i >= n_ctx, j, 0) if n_col > 1 else 0)

    return pl.BlockSpec((tm, width), ctx_map), pl.BlockSpec((tm, width), lat_map)


def _on_stream(tile, tm, fn):
    n_ctx = N_CTX_TOK // tm
    pl.when(tile < n_ctx)(functools.partial(fn, 0))
    pl.when(tile >= n_ctx)(functools.partial(fn, 1))


def _ada_body(c_ref, w_ref, b_ref, o_ref):
    cv = c_ref[...]
    s = cv * jax.nn.sigmoid(cv)
    o_ref[0] = jnp.dot(s.astype(BF16), w_ref[0].astype(BF16), preferred_element_type=F32) + b_ref[0]


def _ada(cvec, ada_w, ada_b):
    tn = 1024
    n = N_MOD * D_MODEL
    return pl.pallas_call(
        _ada_body,
        out_shape=jax.ShapeDtypeStruct((DEPTH, N_SEQ_ROWS, n), F32),
        grid=(DEPTH, n // tn),
        in_specs=[
            pl.BlockSpec((N_SEQ_ROWS, D_MODEL), lambda l, j: (0, 0)),
            pl.BlockSpec((1, D_MODEL, tn), lambda l, j: (l, 0, j)),
            pl.BlockSpec((1, 1, tn), lambda l, j: (l, 0, j)),
        ],
        out_specs=pl.BlockSpec((1, N_SEQ_ROWS, tn), lambda l, j: (l, 0, j)),
        compiler_params=pltpu.CompilerParams(
            dimension_semantics=("parallel", "parallel"), vmem_limit_bytes=VMEM_LIMIT),
        name="ada_mod",
    )(cvec, ada_w, ada_b.reshape(DEPTH, 1, n))


def _norm_mod(x, nw, sc, sh):
    ms = jnp.mean(x * x, axis=-1, keepdims=True)
    y = x * lax.rsqrt(ms + EPS) * nw
    return y * (1.0 + sc) + sh


PROJ_TM = 1024
PROJ_TN = 1024


def _inproj_body(x_ref, nw_ref, sc_ref, sh_ref, w_ref, ws_ref, o_ref, os_ref, h_ref):
    @pl.when(pl.program_id(1) == 0)
    def _():
        hb = _norm_mod(x_ref[...], nw_ref[...], sc_ref[...], sh_ref[...]).astype(BF16)
        h_ref[...] = hb
        os_ref[...] = jnp.dot(hb, ws_ref[...], preferred_element_type=F32)

    o_ref[...] = jnp.dot(h_ref[...], w_ref[...], preferred_element_type=F32)


def _inproj(x, nw, mod, w_main, w_small, layer):
    tm, tn = PROJ_TM, PROJ_TN
    return pl.pallas_call(
        _inproj_body,
        out_shape=(jax.ShapeDtypeStruct((N_TOK, PROJ_MAIN), F32),
                   jax.ShapeDtypeStruct((N_TOK, PROJ_SMALL), F32)),
        grid=(N_TOK // tm, PROJ_MAIN // tn),
        in_specs=[
            pl.BlockSpec((tm, D_MODEL), lambda i, j: (i, 0)),
            pl.BlockSpec((1, D_MODEL), lambda i, j: (0, 0)),
            pl.BlockSpec((None, None, 1, D_MODEL), lambda i, j: (_seq_row(i, tm), 1, 0, 0)),
            pl.BlockSpec((None, None, 1, D_MODEL), lambda i, j: (_seq_row(i, tm), 0, 0, 0)),
            pl.BlockSpec((None, None, D_MODEL, tn), lambda i, j: (layer, j, 0, 0)),
            pl.BlockSpec((None, D_MODEL, PROJ_SMALL), lambda i, j: (layer, 0, 0)),
        ],
        out_specs=(pl.BlockSpec((tm, tn), lambda i, j: (i, j)),
                   pl.BlockSpec((tm, PROJ_SMALL), lambda i, j: (i, 0))),
        scratch_shapes=[pltpu.VMEM((tm, D_MODEL), BF16)],
        compiler_params=pltpu.CompilerParams(
            dimension_semantics=("parallel", "arbitrary"), vmem_limit_bytes=VMEM_LIMIT),
        name="in_proj",
    )(x, nw, mod, mod, w_main, w_small)


def _outproj_body(*refs):
    n_mix = 4
    m_refs, (w_ref, x_ref, g_ref, o_ref) = refs[:2 * n_mix], refs[2 * n_mix:]

    def compute(stream):
        acc = None
        for p in range(n_mix):
            part = jnp.dot(m_refs[2 * p + stream][...], w_ref[p * GROUP_WIDTH:(p + 1) * GROUP_WIDTH, :],
                           preferred_element_type=F32)
            acc = part if acc is None else acc + part
        o_ref[...] = x_ref[...] + g_ref[...] * acc

    _on_stream(pl.program_id(0), PROJ_TM, compute)


def _outproj(parts, w, x, mod, layer):
    tm, tn = PROJ_TM, 1024
    in_specs = []
    for _ in parts:
        in_specs += list(_stream_specs(tm, GROUP_WIDTH))
    in_specs += [pl.BlockSpec((None, D_MODEL, tn), lambda i, j: (layer, 0, j)),
                 pl.BlockSpec((tm, tn), lambda i, j: (i, j)),
                 pl.BlockSpec((None, None, 1, tn), lambda i, j: (_seq_row(i, tm), 2, 0, j))]
    return pl.pallas_call(
        _outproj_body,
        out_shape=jax.ShapeDtypeStruct((N_TOK, D_MODEL), F32),
        grid=(N_TOK // tm, D_MODEL // tn),
        in_specs=in_specs,
        out_specs=pl.BlockSpec((tm, tn), lambda i, j: (i, j)),
        compiler_params=pltpu.CompilerParams(
            dimension_semantics=("parallel", "arbitrary"), vmem_limit_bytes=VMEM_LIMIT),
        name="out_proj",
    )(*(a for pair in parts for a in pair), w, x, mod)


FFN_TM = 512
FFN_TH = 512


def _ffn_body(x_ref, nw_ref, sc_ref, sh_ref, g_ref, w1_ref, w3_ref, w2_ref, fw_ref, *rest, final_norm):
    o_refs, (h_ref, acc_ref) = rest[:-2], rest[-2:]
    i, k = pl.program_id(0), pl.program_id(1)

    @pl.when(k == 0)
    def _():
        h_ref[...] = _norm_mod(x_ref[...], nw_ref[...], sc_ref[...], sh_ref[...]).astype(BF16)
        acc_ref[...] = jnp.zeros_like(acc_ref)

    h = h_ref[...]
    a = jnp.dot(h, w1_ref[...], preferred_element_type=F32)
    b = jnp.dot(h, w3_ref[...], preferred_element_type=F32)
    g = (a * jax.nn.sigmoid(a) * b).astype(BF16)
    acc_ref[...] += jnp.dot(g, w2_ref[...], preferred_element_type=F32)

    @pl.when(k == pl.num_programs(1) - 1)
    def _():
        y = x_ref[...] + g_ref[...] * acc_ref[...]
        if not final_norm:
            o_refs[0][...] = y
        else:
            ms = jnp.mean(y * y, axis=-1, keepdims=True)
            y = y * lax.rsqrt(ms + EPS) * fw_ref[...]

            def write(stream):
                o_refs[stream][...] = y

            _on_stream(i, FFN_TM, write)


def _ffn(x, nw, mod, w1, w3, w2, fw, layer):
    final_norm = layer == DEPTH - 1
    tm, th = FFN_TM, FFN_TH
    if final_norm:
        out_shape = (jax.ShapeDtypeStruct((N_CTX_TOK, D_MODEL), F32), jax.ShapeDtypeStruct((N_LAT_TOK, D_MODEL), F32))
        out_specs = _stream_specs(tm, D_MODEL)
    else:
        out_shape = jax.ShapeDtypeStruct((N_TOK, D_MODEL), F32)
        out_specs = pl.BlockSpec((tm, D_MODEL), lambda i, k: (i, 0))
    return pl.pallas_call(
        functools.partial(_ffn_body, final_norm=final_norm),
        out_shape=out_shape,
        grid=(N_TOK // tm, FFN_HIDDEN // th),
        in_specs=[
            pl.BlockSpec((tm, D_MODEL), lambda i, k: (i, 0)),
            pl.BlockSpec((1, D_MODEL), lambda i, k: (0, 0)),
            pl.BlockSpec((None, None, 1, D_MODEL), lambda i, k: (_seq_row(i, tm), 4, 0, 0)),
            pl.BlockSpec((None, None, 1, D_MODEL), lambda i, k: (_seq_row(i, tm), 3, 0, 0)),
            pl.BlockSpec((None, None, 1, D_MODEL), lambda i, k: (_seq_row(i, tm), 5, 0, 0)),
            pl.BlockSpec((None, None, D_MODEL, th), lambda i, k: (layer, k, 0, 0)),
            pl.BlockSpec((None, None, D_MODEL, th), lambda i, k: (layer, k, 0, 0)),
            pl.BlockSpec((None, th, D_MODEL), lambda i, k: (layer, k, 0)),
            pl.BlockSpec((1, D_MODEL), lambda i, k: (0, 0)),
        ],
        out_specs=out_specs,
        scratch_shapes=[pltpu.VMEM((tm, D_MODEL), BF16), pltpu.VMEM((tm, D_MODEL), F32)],
        compiler_params=pltpu.CompilerParams(
            dimension_semantics=("arbitrary", "arbitrary"), vmem_limit_bytes=VMEM_LIMIT),
        name="ffn",
    )(x, nw, mod, mod, mod, w1, w3, w2, fw)


def _axial_rope(l):
    n_rows = l // GRID_W
    t_row = jnp.repeat(jnp.arange(n_rows, dtype=F32), GRID_W)
    t_col = jnp.tile(jnp.arange(GRID_W, dtype=F32), n_rows)
    n_freq = HEAD_DIM // 4
    inv = ROPE_BASE ** (-jnp.arange(n_freq, dtype=F32) / n_freq)
    ang = jnp.concatenate([t_row[:, None] * inv, t_col[:, None] * inv], axis=-1)
    return jnp.cos(ang), jnp.sin(ang)


MIX_CHUNK = 128
DIRS = 2
NT_DIMS = (((1,), (1,)), ((), ()))
TN_DIMS = (((0,), (0,)), ((), ()))


def _dot(a, b):
    return jnp.dot(a.astype(BF16), b.astype(BF16), preferred_element_type=F32)


def _dot_nt(a, b):
    return lax.dot_general(a.astype(BF16), b.astype(BF16), NT_DIMS, preferred_element_type=F32)


def _dot_tn(a, b):
    return lax.dot_general(a.astype(BF16), b.astype(BF16), TN_DIMS, preferred_element_type=F32)


def _silu(x):
    return x * jax.nn.sigmoid(x)


def _seq_spec(seq_len, width, row_off, col_blk):
    return pl.BlockSpec((seq_len, width), lambda b: (row_off + b, col_blk))


def _mix_params(n_par):
    return pltpu.CompilerParams(dimension_semantics=("parallel",) * n_par, vmem_limit_bytes=VMEM_LIMIT)


def _ret_body(*refs, n_chunks, use_rope, has_s0):
    it = iter(refs)
    lg_ref, q_ref, k_ref, v_ref, g_ref = (next(it) for _ in range(5))
    cos_ref, sin_ref = (next(it), next(it)) if use_rope else (None, None)
    s0_ref = next(it) if has_s0 else None
    o_ref, sf_ref, acc_ref, s_ref, intra_ref, qd_ref, kd_ref, cd_ref = (next(it) for _ in range(8))
    C, H = MIX_CHUNK, N_HEADS

    row = lax.broadcasted_iota(jnp.int32, (C, C), 0)
    col = lax.broadcasted_iota(jnp.int32, (C, C), 1)
    rel = (row - col).astype(F32)
    pos = lax.broadcasted_iota(jnp.int32, (C, HEAD_DIM), 0).astype(F32)
    for d in range(DIRS):
        for h in range(H):
            lg = lg_ref[d, h]
            if d == 0:
                intra_ref[d, h] = jnp.where(rel >= 0, jnp.exp(jnp.maximum(rel, 0.0) * lg), 0.0)
                qd_ref[d, h] = jnp.exp((pos + 1.0) * lg)
                kd_ref[d, h] = jnp.exp((C - 1.0 - pos) * lg)
            else:
                intra_ref[d, h] = jnp.where(rel <= 0, jnp.exp(jnp.maximum(-rel, 0.0) * lg), 0.0)
                qd_ref[d, h] = jnp.exp((C - pos) * lg)
                kd_ref[d, h] = jnp.exp(pos * lg)
            cd_ref[d, h] = jnp.exp(jnp.full((C, HEAD_DIM), C, F32) * lg)
    if has_s0:
        s_ref[...] = s0_ref[...]
    else:
        s_ref[...] = jnp.zeros_like(s_ref)
    acc_ref[...] = jnp.zeros_like(acc_ref)

    units = [(d, h) for d in range(DIRS) for h in range(H)]

    def chunk_step(n, carry):
        rows_d = [pl.ds(pl.multiple_of(c * C, C), C) for c in (n, n_chunks - 1 - n)]
        q_l, k_l, v_l, sc_l = [], [], [], []
        for d, h in units:
            rows, cols = rows_d[d], slice(h * HEAD_DIM, (h + 1) * HEAD_DIM)
            q, k = q_ref[rows, cols], k_ref[rows, cols]
            if use_rope:
                cs, sn = cos_ref[rows, :], sin_ref[rows, :]
                q = q * cs + pltpu.roll(q, HEAD_DIM // 2, 1) * sn
                k = k * cs + pltpu.roll(k, HEAD_DIM // 2, 1) * sn
            k = k * HEAD_DIM ** -0.5
            q_l.append(q)
            k_l.append(k)
            v_l.append(v_ref[rows, cols].astype(BF16))
            sc_l.append(_dot_nt(q, k) * intra_ref[d, h])
        for i, (d, h) in enumerate(units):
            cols = slice(h * HEAD_DIM, (h + 1) * HEAD_DIM)
            s = s_ref[d, h]
            acc_ref[rows_d[d], cols] += _dot(jnp.concatenate([sc_l[i], q_l[i] * qd_ref[d, h]], axis=1),
                                             jnp.concatenate([v_l[i], s.astype(BF16)], axis=0))
            s_ref[d, h] = cd_ref[d, h] * s + _dot_tn(k_l[i] * kd_ref[d, h], v_l[i])
        return carry

    lax.fori_loop(0, n_chunks, chunk_step, 0)
    sf_ref[...] = s_ref[...]

    def finish(n, carry):
        rows = pl.ds(pl.multiple_of(n * C, C), C)
        for h in range(H):
            cols = slice(h * HEAD_DIM, (h + 1) * HEAD_DIM)
            o = acc_ref[rows, cols]
            mu = jnp.mean(o, axis=-1, keepdims=True)
            oc = o - mu
            y = oc * lax.rsqrt(jnp.mean(oc * oc, axis=-1, keepdims=True) + EPS)
            o_ref[rows, cols] = (y * _silu(g_ref[rows, cols])).astype(BF16)
        return carry

    lax.fori_loop(0, n_chunks, finish, 0)


def _retention_pallas(pm, log_gamma, n_seq, seq_len, row_off, rope2, s0, layer):
    use_rope, has_s0 = rope2 is not None, s0 is not None
    st_shape = (DIRS, N_HEADS, HEAD_DIM, HEAD_DIM)
    in_specs = [pl.BlockSpec(memory_space=pltpu.SMEM)]
    in_specs += [_seq_spec(seq_len, GROUP_WIDTH, row_off, cb) for cb in range(4)]
    args = [log_gamma, pm, pm, pm, pm]
    if use_rope:
        in_specs += [pl.BlockSpec((seq_len, HEAD_DIM), lambda b: (0, 0))] * 2
        args += list(rope2)
    if has_s0:
        in_specs.append(pl.BlockSpec((None, None) + st_shape, lambda b: (b, layer, 0, 0, 0, 0)))
        args.append(s0)
    return pl.pallas_call(
        functools.partial(_ret_body, n_chunks=seq_len // MIX_CHUNK, use_rope=use_rope, has_s0=has_s0),
        out_shape=(jax.ShapeDtypeStruct((n_seq * seq_len, GROUP_WIDTH), BF16),
                   jax.ShapeDtypeStruct((n_seq,) + st_shape, F32)),
        grid=(n_seq,),
        in_specs=in_specs,
        out_specs=(pl.BlockSpec((seq_len, GROUP_WIDTH), lambda b: (b, 0)),
                   pl.BlockSpec((None,) + st_shape, lambda b: (b, 0, 0, 0, 0))),
        scratch_shapes=[pltpu.VMEM((seq_len, GROUP_WIDTH), F32), pltpu.VMEM(st_shape, F32),
                        pltpu.VMEM((DIRS, N_HEADS, MIX_CHUNK, MIX_CHUNK), F32),
                        pltpu.VMEM((DIRS, N_HEADS, MIX_CHUNK, HEAD_DIM), F32),
                        pltpu.VMEM((DIRS, N_HEADS, MIX_CHUNK, HEAD_DIM), F32),
                        pltpu.VMEM((DIRS, N_HEADS, MIX_CHUNK, HEAD_DIM), F32)],
        compiler_params=_mix_params(1),
        name="retention",
    )(*args)


def _rope_tables(l):
    cos, sin = _axial_rope(l)
    return jnp.concatenate([cos, cos], axis=-1), jnp.concatenate([-sin, sin], axis=-1)


S5_HALF_G = S5_GROUPS // 2
S5_HALF_U = S5_HALF_G * S5_CH
S5_HALF_X = S5_HALF_G * S5_N
S5_X = S5_GROUPS * S5_N
S5_TC = 512
S5_BLK = 8
S5_TABS = 4


def _s5_tables(a_re, a_im, log_step, b_re, b_im, c_re, c_im):
    dt = jnp.exp(log_step)[..., None]
    mag = jnp.exp(a_re * dt)
    ab_re = mag * jnp.cos(a_im * dt)
    ab_im = mag * jnp.sin(a_im * dt)
    den = a_re * a_re + a_im * a_im
    nr = ab_re - 1.0
    f_re = (nr * a_re + ab_im * a_im) / den
    f_im = (ab_im * a_re - nr * a_im) / den
    bb_re = f_re[..., None] * b_re - f_im[..., None] * b_im
    bb_im = f_re[..., None] * b_im + f_im[..., None] * b_re
    eye = jnp.eye(S5_HALF_G, dtype=F32)

    def in_mat(bb):
        bb = bb.reshape(DIRS, 2, S5_HALF_G, S5_N, S5_CH)
        return jnp.einsum('dhgnc,gk->dhgckn', bb, eye).reshape(DIRS, 2, S5_HALF_U, S5_HALF_X)

    def out_mat(cc):
        cc = cc.reshape(DIRS, 2, S5_HALF_G, S5_CH, S5_N)
        return jnp.einsum('dhgcn,gk->dhgnkc', cc, eye).reshape(DIRS, 2, S5_HALF_X, S5_HALF_U)

    bm = jnp.concatenate([in_mat(bb_re), in_mat(bb_im)], axis=-1).astype(BF16)
    cm = jnp.concatenate([out_mat(c_re), -out_mat(c_im)], axis=-2).astype(BF16)
    t = jnp.arange(S5_BLK, dtype=F32)
    order = jnp.stack([t, S5_BLK - 1.0 - t])
    shifts = 2.0 ** jnp.arange(S5_TABS - 1, dtype=F32)
    expo = jnp.concatenate([jnp.where(order[:, None, :] >= shifts[None, :, None], shifts[None, :, None], jnp.nan),
                            order[:, None, :] + 1.0], axis=1)
    live = ~jnp.isnan(expo)
    e = jnp.where(live, expo, 0.0)[..., None]
    adt_re = (a_re * dt).reshape(DIRS, 1, 1, S5_X)
    adt_im = (a_im * dt).reshape(DIRS, 1, 1, S5_X)
    pmag = jnp.where(live[..., None], jnp.exp(e * adt_re), 0.0)
    pw_re = pmag * jnp.cos(e * adt_im)
    pw_im = pmag * jnp.sin(e * adt_im)
    return bm, cm, pw_re, pw_im


def _gelu_tanh(x):
    return 0.5 * x * (1.0 + jnp.tanh(math.sqrt(2.0 / math.pi) * (x + 0.044715 * (x * x * x))))


def _s5_body(*refs, seq_len, has_s0):
    it = iter(refs)
    u_ref, bm_ref, cm_ref, pwr_ref, pwi_ref, d_ref, gw_ref, gb_ref = (next(it) for _ in range(8))
    x0r_ref, x0i_ref = (next(it), next(it)) if has_s0 else (None, None)
    o_ref, sfr_ref, sfi_ref, y_ref, xr_ref, xi_ref, xb_ref = (next(it) for _ in range(7))
    tc = min(seq_len, S5_TC)
    n_tiles = seq_len // tc
    n_pair = tc // (2 * S5_BLK)

    y_ref[...] = d_ref[...] * u_ref[...]
    for d in range(DIRS):
        last = S5_BLK - 1 if d == 0 else 0
        for hf in range(2):
            xs = slice(hf * S5_HALF_X, (hf + 1) * S5_HALF_X)
            us = slice(hf * S5_HALF_U, (hf + 1) * S5_HALF_U)

            def scan_block(x_in, carry, d=d, xs=xs, last=last):
                xr, xi = x_in
                car_re, car_im = carry
                for k in range(S5_TABS - 1):
                    s = 1 << k
                    shift = s if d == 0 else S5_BLK - s
                    p_re, p_im = pwr_ref[d, k, :, xs], pwi_ref[d, k, :, xs]
                    sr, si = pltpu.roll(xr, shift, 0), pltpu.roll(xi, shift, 0)
                    xr, xi = xr + p_re * sr - p_im * si, xi + p_re * si + p_im * sr
                p_re, p_im = pwr_ref[d, S5_TABS - 1, :, xs], pwi_ref[d, S5_TABS - 1, :, xs]
                xr, xi = xr + p_re * car_re - p_im * car_im, xi + p_re * car_im + p_im * car_re
                return (xr, xi), (xr[last:last + 1, :], xi[last:last + 1, :])

            def scan_pair(j, carry, d=d, scan_block=scan_block):
                pair = j if d == 0 else n_pair - 1 - j
                rows = pl.ds(pl.multiple_of(pair * 2 * S5_BLK, 2 * S5_BLK), 2 * S5_BLK)
                xr2, xi2 = xr_ref[rows, :], xi_ref[rows, :]
                halves = [(xr2[:S5_BLK], xi2[:S5_BLK]), (xr2[S5_BLK:], xi2[S5_BLK:])]
                order = (0, 1) if d == 0 else (1, 0)
                out = [None, None]
                for idx in order:
                    out[idx], carry = scan_block(halves[idx], carry)
                xb_ref[rows, :S5_HALF_X] = jnp.concatenate([out[0][0], out[1][0]], axis=0).astype(BF16)
                xb_ref[rows, S5_HALF_X:] = jnp.concatenate([out[0][1], out[1][1]], axis=0).astype(BF16)
                return carry

            def scan_tile(i, carry, d=d, hf=hf, us=us, scan_pair=scan_pair):
                tile = i if d == 0 else n_tiles - 1 - i
                rows = pl.ds(pl.multiple_of(tile * tc, tc), tc)
                bu = _dot(u_ref[rows, us], bm_ref[d, hf])
                xr_ref[...] = bu[:, :S5_HALF_X]
                xi_ref[...] = bu[:, S5_HALF_X:]
                carry = lax.fori_loop(0, n_pair, scan_pair, carry)
                y_ref[rows, us] += jnp.dot(xb_ref[...], cm_ref[d, hf], preferred_element_type=F32)
                return carry

            if has_s0:
                carry0 = (x0r_ref[d, :, xs], x0i_ref[d, :, xs])
            else:
                carry0 = (jnp.zeros((1, S5_HALF_X), F32), jnp.zeros((1, S5_HALF_X), F32))
            car_re, car_im = lax.fori_loop(0, n_tiles, scan_tile, carry0)
            sfr_ref[d, :, xs] = car_re
            sfi_ref[d, :, xs] = car_im

    z = _gelu_tanh(y_ref[...])
    o_ref[...] = (z * jax.nn.sigmoid(_dot(z, gw_ref[...]) + gb_ref[...])).astype(BF16)


def _s5_pallas(pm, tables, s5_d, glu_w, glu_b, n_seq, seq_len, row_off, x0, layer):
    bm, cm, pw_re, pw_im = tables
    has_s0 = x0 is not None
    full = lambda shape: pl.BlockSpec(shape, lambda b: (0,) * len(shape))
    of_layer = lambda t: pl.BlockSpec((None,) + t.shape[1:], lambda b: (layer,) + (0,) * (t.ndim - 1))
    in_specs = [_seq_spec(seq_len, GROUP_WIDTH, row_off, 13),
                of_layer(bm), of_layer(cm), of_layer(pw_re), of_layer(pw_im),
                full((1, GROUP_WIDTH)), full((GROUP_WIDTH, GROUP_WIDTH)), full((1, GROUP_WIDTH))]
    args = [pm, bm, cm, pw_re, pw_im, s5_d.reshape(1, GROUP_WIDTH), glu_w.astype(BF16),
            glu_b.reshape(1, GROUP_WIDTH)]
    if has_s0:
        in_specs += [pl.BlockSpec((None, None, DIRS, 1, S5_X), lambda b: (b, layer, 0, 0, 0))] * 2
        args += list(x0)
    st = jax.ShapeDtypeStruct((n_seq, DIRS, 1, S5_X), F32)
    st_spec = pl.BlockSpec((None, DIRS, 1, S5_X), lambda b: (b, 0, 0, 0))
    tc = min(seq_len, S5_TC)
    return pl.pallas_call(
        functools.partial(_s5_body, seq_len=seq_len, has_s0=has_s0),
        out_shape=(jax.ShapeDtypeStruct((n_seq * seq_len, GROUP_WIDTH), BF16), st, st),
        grid=(n_seq,),
        in_specs=in_specs,
        out_specs=(pl.BlockSpec((seq_len, GROUP_WIDTH), lambda b: (b, 0)), st_spec, st_spec),
        scratch_shapes=[pltpu.VMEM((seq_len, GROUP_WIDTH), F32),
                        pltpu.VMEM((tc, S5_HALF_X), F32), pltpu.VMEM((tc, S5_HALF_X), F32),
                        pltpu.VMEM((tc, 2 * S5_HALF_X), BF16)],
        compiler_params=_mix_params(1),
        name="s5",
    )(*args)


GLA_LEVELS = 7


def _chunk_cumsum_rows(x, rowi):
    s = 1
    while s < MIX_CHUNK:
        x = x + jnp.where(rowi >= s, pltpu.roll(x, s, 0), 0.0)
        s *= 2
    return x


def _gla_body(*refs, n_chunks, has_s0):
    it = iter(refs)
    q_ref, f0_ref, f1_ref, i_ref, g_ref, lb_ref, nw_ref = (next(it) for _ in range(7))
    s0_ref = next(it) if has_s0 else None
    o_ref, sf_ref, acc_ref, s_ref, code_ref = (next(it) for _ in range(5))
    C, H = MIX_CHUNK, N_HEADS
    f_refs = (f0_ref, f1_ref)

    rowi = lax.broadcasted_iota(jnp.int32, (C, HEAD_DIM), 0)
    ri = lax.broadcasted_iota(jnp.int32, (C, C), 0)
    ci = lax.broadcasted_iota(jnp.int32, (C, C), 1)
    top_bit = 31 - lax.clz(ri ^ ci)
    code_ref[...] = jnp.where(ri > ci, top_bit, jnp.where(ri < ci, -1 - top_bit, GLA_LEVELS))
    if has_s0:
        s_ref[...] = s0_ref[...]
    else:
        s_ref[...] = jnp.zeros_like(s_ref)
    acc_ref[...] = jnp.zeros_like(acc_ref)

    def chunk_step(n, carry):
        for d in range(DIRS):
            c = n if d == 0 else n_chunks - 1 - n
            rows = pl.ds(pl.multiple_of(c * C, C), C)
            for h in range(H):
                cols = slice(h * HEAD_DIM, (h + 1) * HEAD_DIM)
                code = code_ref[...]
                q = _silu(q_ref[rows, cols]) * HEAD_DIM ** -0.5
                v = i_ref[rows, cols].astype(BF16)
                fx = f_refs[d][rows, cols]
                lb = lb_ref[d:d + 1, cols]
                sig = 1.0 / (1.0 + jnp.exp(-fx))
                logf = jnp.log2(jnp.maximum(lb, LB_FLOOR) + (1.0 - lb) * sig)
                k = (1.0 - lb) * (1.0 - sig)
                cum = _chunk_cumsum_rows(logf, rowi)
                own = cum
                attn = jnp.where(code == GLA_LEVELS, _dot_nt(q, k), 0.0)
                for lvl in range(GLA_LEVELS):
                    m = 1 << lvl
                    prev = pltpu.roll(own, m, 0)
                    pre = jnp.minimum(cum - prev, 0.0)
                    suf = own - cum
                    if d == 0:
                        sc = _dot_nt(q * jnp.exp2(pre), k * jnp.exp2(suf))
                        hit = code == lvl
                    else:
                        sc = _dot_nt(q * jnp.exp2(suf + logf), k * jnp.exp2(jnp.minimum(pre - logf, 0.0)))
                        hit = code == -1 - lvl
                    attn = jnp.where(hit, sc, attn)
                    own = jnp.where(((rowi >> lvl) & 1) == 0, pltpu.roll(own, C - m, 0), own)
                tot = own
                st = s_ref[d, h]
                if d == 0:
                    q_in, k_out = q * jnp.exp2(cum), k * jnp.exp2(tot - cum)
                else:
                    q_in, k_out = q * jnp.exp2(tot - cum + logf), k * jnp.exp2(cum - logf)
                acc_ref[rows, cols] += _dot(attn, v) + _dot_nt(q_in, st)
                s_ref[d, h] = jnp.exp2(tot[0:1, :]) * st + _dot_tn(v, k_out)
        return carry

    lax.fori_loop(0, n_chunks, chunk_step, 0)
    sf_ref[...] = s_ref[...]

    def finish(n, carry):
        rows = pl.ds(pl.multiple_of(n * C, C), C)
        for h in range(H):
            cols = slice(h * HEAD_DIM, (h + 1) * HEAD_DIM)
            o = acc_ref[rows, cols]
            y = o * lax.rsqrt(jnp.mean(o * o, axis=-1, keepdims=True) + EPS) * nw_ref[...]
            o_ref[rows, cols] = (y * _silu(g_ref[rows, cols])).astype(BF16)
        return carry

    lax.fori_loop(0, n_chunks, finish, 0)


def _gla_pallas(pm, lower_bound, norm_w, n_seq, seq_len, row_off, s0_t, layer):
    has_s0 = s0_t is not None
    st_shape = (DIRS, N_HEADS, HEAD_DIM, HEAD_DIM)
    in_specs = [_seq_spec(seq_len, GROUP_WIDTH, row_off, cb) for cb in (8, 9, 10, 11, 12)]
    in_specs += [pl.BlockSpec((DIRS, GROUP_WIDTH), lambda b: (0, 0)), pl.BlockSpec((1, HEAD_DIM), lambda b: (0, 0))]
    args = [pm] * 5 + [lower_bound, norm_w.reshape(1, HEAD_DIM)]
    if has_s0:
        in_specs.append(pl.BlockSpec((None, None) + st_shape, lambda b: (b, layer, 0, 0, 0, 0)))
        args.append(s0_t)
    return pl.pallas_call(
        functools.partial(_gla_body, n_chunks=seq_len // MIX_CHUNK, has_s0=has_s0),
        out_shape=(jax.ShapeDtypeStruct((n_seq * seq_len, GROUP_WIDTH), BF16),
                   jax.ShapeDtypeStruct((n_seq,) + st_shape, F32)),
        grid=(n_seq,),
        in_specs=in_specs,
        out_specs=(pl.BlockSpec((seq_len, GROUP_WIDTH), lambda b: (b, 0)),
                   pl.BlockSpec((None,) + st_shape, lambda b: (b, 0, 0, 0, 0))),
        scratch_shapes=[pltpu.VMEM((seq_len, GROUP_WIDTH), F32), pltpu.VMEM(st_shape, F32),
                        pltpu.VMEM((MIX_CHUNK, MIX_CHUNK), jnp.int32)],
        compiler_params=_mix_params(1),
        name="hgrn2",
    )(*args)


GDN_BLOCK_BITS = 4
GDN_MERGES = 3
N_GATES = DIRS * N_HEADS


def _softplus(x):
    return jnp.maximum(x, 0.0) + jnp.log1p(jnp.exp(-jnp.abs(x)))


def _gdn_body(*refs, seq_len, n_sub, has_s0):
    it = iter(refs)
    (q_ref, k_ref, v_ref, g_ref, ps_ref, pst_ref, cw_ref, prow_ref, pcol_ref, nw_ref) = (next(it) for _ in range(10))
    s0_ref = next(it) if has_s0 else None
    (o_ref, sf_ref, acc_ref, s_ref, qn_ref, kn_ref, vn_ref, gate_ref, gatet_ref, blev_ref) = (
        next(it) for _ in range(10))
    C, H, L = MIX_CHUNK, N_HEADS, seq_len
    R = n_sub * L
    n_chunks = L // C
    w = GROUP_WIDTH

    pos = lax.broadcasted_iota(jnp.int32, (R, HEAD_DIM), 0) & (L - 1)
    for part, (src, dst) in enumerate(((q_ref, qn_ref), (k_ref, kn_ref), (v_ref, vn_ref))):
        for h in range(H):
            cols = slice(h * HEAD_DIM, (h + 1) * HEAD_DIM)
            wc = slice(part * w + h * HEAD_DIM, part * w + (h + 1) * HEAD_DIM)
            x = src[:, cols]
            x_prev = jnp.where(pos >= 1, pltpu.roll(x, 1, 0), 0.0)
            x_next = jnp.where(pos < L - 1, pltpu.roll(x, R - 1, 0), 0.0)
            y = _silu(cw_ref[0:1, wc] * x_prev + cw_ref[1:2, wc] * x + cw_ref[2:3, wc] * x_next)
            if part < 2:
                y = y * lax.rsqrt(jnp.sum(y * y, axis=-1, keepdims=True) + EPS)
            if part == 0:
                y = y * HEAD_DIM ** -0.5
            dst[:, cols] = y

    rowi = lax.broadcasted_iota(jnp.int32, (C, HEAD_DIM), 0)
    lane = lax.broadcasted_iota(jnp.int32, (C, HEAD_DIM), 1)
    lane_t = lax.broadcasted_iota(jnp.int32, (2 * N_GATES, C), 1)
    row_t = lax.broadcasted_iota(jnp.int32, (2 * N_GATES, C), 0)
    for c in range(R // C):
        rows = slice(c * C, (c + 1) * C)
        a = ps_ref[rows, :]
        la = -jnp.exp(prow_ref[0:1, :]) * _softplus(a + prow_ref[1:2, :])
        pre = _chunk_cumsum_rows(la, rowi)
        tot = jnp.broadcast_to(pre[C - 1:C, :], (C, HEAD_DIM))
        g = jnp.where(lane < N_HEADS, pre, tot - pre + la)
        gate_ref[rows, :] = jnp.where(lane < N_GATES, g,
                                      jnp.where(lane < 2 * N_GATES, jax.nn.sigmoid(a),
                                                pltpu.roll(tot, 2 * N_GATES, 1)))
        at = pst_ref[:, rows]
        lat = -jnp.exp(pcol_ref[:, 0:1]) * _softplus(at + pcol_ref[:, 1:2])
        pre_t, s = lat, 1
        while s < C:
            pre_t = pre_t + jnp.where(lane_t >= s, pltpu.roll(pre_t, s, 1), 0.0)
            s *= 2
        tot_t = jnp.broadcast_to(pre_t[:, C - 1:C], (2 * N_GATES, C))
        g_t = jnp.where(row_t < N_HEADS, pre_t, tot_t - pre_t + lat)
        gatet_ref[:, rows] = jnp.where(row_t < N_GATES, g_t, jax.nn.sigmoid(at))

    ri = lax.broadcasted_iota(jnp.int32, (C, C), 0)
    ci = lax.broadcasted_iota(jnp.int32, (C, C), 1)
    bx = (ri >> GDN_BLOCK_BITS) ^ (ci >> GDN_BLOCK_BITS)
    blev_ref[...] = jnp.where(bx == 0, 0, 32 - lax.clz(bx))
    if has_s0:
        s_ref[...] = s0_ref[...]
    else:
        s_ref[...] = jnp.zeros_like(s_ref)
    acc_ref[...] = jnp.zeros_like(acc_ref)

    units = [(s, d, h) for s in range(n_sub) for d in range(DIRS) for h in range(H)]

    def chunk_step(n, carry):
        eye = jnp.where(ri == ci, 1.0, 0.0)
        blev = blev_ref[...]
        rows_sd = [[pl.ds(pl.multiple_of(s * L + c * C, C), C) for c in (n, n_chunks - 1 - n)] for s in range(n_sub)]
        incl_d = [ri >= ci, ri <= ci]
        strict_d = [ri > ci, ri < ci]
        q_l, et_l, ek_l, rhs_l, a_l, attn_l = ([] for _ in range(6))
        for s, d, h in units:
            rows, cols, cg = rows_sd[s][d], slice(h * HEAD_DIM, (h + 1) * HEAD_DIM), d * N_HEADS + h
            q, k, v = qn_ref[rows, cols], kn_ref[rows, cols], vn_ref[rows, cols]
            g_i = jnp.broadcast_to(gate_ref[rows, cg:cg + 1], (C, C))
            b_i = jnp.broadcast_to(gate_ref[rows, N_GATES + cg:N_GATES + cg + 1], (C, C))
            tot = jnp.broadcast_to(gate_ref[rows, 2 * N_GATES + cg:2 * N_GATES + cg + 1], (C, C))
            g_j = gatet_ref[cg:cg + 1, rows]
            decay = jnp.where(incl_d[d], jnp.exp(jnp.minimum(g_i - g_j, 0.0)), 0.0)
            e_g = jnp.exp(g_i)
            kb = k.astype(BF16)
            a_l.append(jnp.where(strict_d[d], _dot_nt(kb, kb) * b_i * decay, 0.0))
            attn_l.append(_dot_nt(q, kb) * decay)
            rhs_l.append(jnp.concatenate([v * b_i, k * (b_i * e_g)], axis=1).astype(BF16))
            q_l.append(q * e_g)
            ek_l.append(k * jnp.exp(tot - g_i))
            et_l.append(jnp.exp(tot))
        b_l = [-jnp.where(blev == 0, a, 0.0) for a in a_l]
        p_l = [eye + b for b in b_l]
        for _ in range(GDN_BLOCK_BITS - 1):
            b_l = [_dot(b, b) for b in b_l]
            p_l = [p + _dot(p, b) for p, b in zip(p_l, b_l)]
        for lvl in range(1, GDN_MERGES + 1):
            ep_l = [_dot(jnp.where(blev == lvl, a, 0.0), p) for a, p in zip(a_l, p_l)]
            p_l = [p - _dot(p, ep) for p, ep in zip(p_l, ep_l)]
        uw_l = [_dot(p, rhs) for p, rhs in zip(p_l, rhs_l)]
        s_l = [s_ref[s * DIRS + d, h] for s, d, h in units]
        vn_l = [uw[:, :HEAD_DIM] - _dot(uw[:, HEAD_DIM:], st) for uw, st in zip(uw_l, s_l)]
        for i, (s, d, h) in enumerate(units):
            cols = slice(h * HEAD_DIM, (h + 1) * HEAD_DIM)
            acc_ref[rows_sd[s][d], cols] += _dot(jnp.concatenate([q_l[i], attn_l[i]], axis=1),
                                                 jnp.concatenate([s_l[i], vn_l[i]], axis=0))
            s_ref[s * DIRS + d, h] = s_l[i] * et_l[i] + _dot_tn(ek_l[i], vn_l[i])
        return carry

    lax.fori_loop(0, n_chunks, chunk_step, 0)
    sf_ref[...] = s_ref[...]

    def finish(n, carry):
        rows = pl.ds(pl.multiple_of(n * C, C), C)
        for h in range(H):
            cols = slice(h * HEAD_DIM, (h + 1) * HEAD_DIM)
            o = acc_ref[rows, cols]
            y = o * lax.rsqrt(jnp.mean(o * o, axis=-1, keepdims=True) + EPS) * nw_ref[...]
            o_ref[rows, cols] = (y * _silu(g_ref[rows, cols])).astype(BF16)
        return carry

    lax.fori_loop(0, R // C, finish, 0)


GDN_CTX_SUB = 2


def _gdn_pallas(pm, ps, ps_t, conv_w, a_log, dt_bias, norm_w, n_seq, seq_len, row_off, s0, layer):
    has_s0 = s0 is not None
    n_sub = 1 if has_s0 else GDN_CTX_SUB
    n_seq, seq_len, row_off = n_seq // n_sub, seq_len * n_sub, row_off // n_sub
    st_shape = (n_sub * DIRS, N_HEADS, HEAD_DIM, HEAD_DIM)
    par = jnp.stack([a_log.reshape(N_GATES), dt_bias.reshape(N_GATES)])
    par_row = jnp.pad(par, ((0, 0), (0, HEAD_DIM - N_GATES)))
    par_col = jnp.pad(par.T, ((0, N_GATES), (0, 0)))
    full = lambda shape: pl.BlockSpec(shape, lambda b: (0,) * len(shape))
    in_specs = [_seq_spec(seq_len, GROUP_WIDTH, row_off, cb) for cb in (4, 5, 6, 7)]
    in_specs += [_seq_spec(seq_len, PROJ_SMALL, row_off, 0),
                 pl.BlockSpec((2 * N_GATES, seq_len), lambda b: (0, row_off + b)),
                 full((GDN_CONV, 3 * GROUP_WIDTH)), full((2, HEAD_DIM)), full((2 * N_GATES, 2)), full((1, HEAD_DIM))]
    args = [pm] * 4 + [ps, ps_t, conv_w, par_row, par_col, norm_w.reshape(1, HEAD_DIM)]
    if has_s0:
        in_specs.append(pl.BlockSpec((None, None) + st_shape, lambda b: (b, layer, 0, 0, 0, 0)))
        args.append(s0)
    seq_f32 = lambda width: pltpu.VMEM((seq_len, width), F32)
    out, s_fin = pl.pallas_call(
        functools.partial(_gdn_body, seq_len=seq_len // n_sub, n_sub=n_sub, has_s0=has_s0),
        out_shape=(jax.ShapeDtypeStruct((n_seq * seq_len, GROUP_WIDTH), BF16),
                   jax.ShapeDtypeStruct((n_seq,) + st_shape, F32)),
        grid=(n_seq,),
        in_specs=in_specs,
        out_specs=(pl.BlockSpec((seq_len, GROUP_WIDTH), lambda b: (b, 0)),
                   pl.BlockSpec((None,) + st_shape, lambda b: (b, 0, 0, 0, 0))),
        scratch_shapes=[seq_f32(GROUP_WIDTH), pltpu.VMEM(st_shape, F32),
                        seq_f32(GROUP_WIDTH), seq_f32(GROUP_WIDTH), seq_f32(GROUP_WIDTH),
                        seq_f32(HEAD_DIM), pltpu.VMEM((2 * N_GATES, seq_len), F32),
                        pltpu.VMEM((MIX_CHUNK, MIX_CHUNK), jnp.int32)],
        compiler_params=_mix_params(1),
        name="gated_delta",
    )(*args)
    return out, s_fin.reshape(n_seq * n_sub, DIRS, N_HEADS, HEAD_DIM, HEAD_DIM)


def kernel(x_prompt, x_sample, state_ret, state_gdn, state_hgrn, state_s5_re, state_s5_im, c, c_ctx, norm1_w, norm2_w, final_norm_w, ada_w, ada_b, in_proj, out_proj, ret_decay_logit, gdn_conv, gdn_a_log, gdn_dt_bias, gdn_norm_w, hg_lb_param, hg_norm_w, s5_a_re, s5_a_im, s5_b_re, s5_b_im, s5_c_re, s5_c_im, s5_log_step, s5_d, s5_glu_w, s5_glu_b, ffn_w1, ffn_w3, ffn_w2):
    lb_soft = jax.nn.softmax(hg_lb_param, axis=0)
    lower_bounds = jnp.cumsum(lb_soft, axis=0) - lb_soft[0]
    rope2 = _rope_tables(DEC_SEQ)

    cvec = jnp.zeros((N_SEQ_ROWS, D_MODEL), F32).at[0].set(c_ctx).at[1:1 + DEC_BATCH].set(c)
    mod_all = _ada(cvec, ada_w, ada_b).reshape(DEPTH, N_SEQ_ROWS, N_MOD, 1, D_MODEL)

    def col_tiles(w, tn):
        depth, k, n = w.shape
        return w.astype(BF16).reshape(depth, k, n // tn, tn).transpose(0, 2, 1, 3)

    gate0 = 8 * GROUP_WIDTH
    w_in = in_proj.astype(BF16)
    w_main = col_tiles(jnp.concatenate([w_in[:, :, :gate0], w_in[:, :, gate0 + 2 * N_GATES:]], axis=-1), PROJ_TN)
    w_small = jnp.pad(w_in[:, :, gate0:gate0 + 2 * N_GATES], ((0, 0), (0, 0), (0, PROJ_SMALL - 2 * N_GATES)))
    w_out = out_proj.astype(BF16)
    w1, w3, w2 = col_tiles(ffn_w1, FFN_TH), col_tiles(ffn_w3, FFN_TH), ffn_w2.astype(BF16)

    x = jnp.concatenate([x_prompt.reshape(N_CTX_TOK, D_MODEL), x_sample.reshape(N_LAT_TOK, D_MODEL)], axis=0)
    hgrn_t = jnp.swapaxes(state_hgrn, -1, -2)
    s5_x0 = (state_s5_re.reshape(DEC_BATCH, DEPTH, DIRS, 1, S5_X), state_s5_im.reshape(DEC_BATCH, DEPTH, DIRS, 1, S5_X))
    lat_off = N_CTX_TOK // DEC_SEQ
    tables = jax.vmap(_s5_tables)(s5_a_re, s5_a_im, s5_log_step, s5_b_re, s5_b_im, s5_c_re, s5_c_im)
    ctx_states = []
    for i in range(DEPTH):
        mod = mod_all[i]
        pm, ps = _inproj(x, norm1_w[i][None], mod, w_main, w_small, i)
        ps_t = ps[:, :2 * N_GATES].T
        lg = jax.nn.log_sigmoid(ret_decay_logit[i])
        ctx = (BATCH, SEQ, 0)
        lat = (DEC_BATCH, DEC_SEQ, lat_off)

        ret_c, rs = _retention_pallas(pm, lg, *ctx, None, None, i)
        ret_l, _ = _retention_pallas(pm, lg, *lat, rope2, state_ret, i)
        gdn_args = (pm, ps, ps_t, gdn_conv[i], gdn_a_log[i], gdn_dt_bias[i], gdn_norm_w[i])
        gdn_c, gs = _gdn_pallas(*gdn_args, *ctx, None, i)
        gdn_l, _ = _gdn_pallas(*gdn_args, *lat, state_gdn, i)
        hg_c, hs = _gla_pallas(pm, lower_bounds[i], hg_norm_w[i], *ctx, None, i)
        hg_l, _ = _gla_pallas(pm, lower_bounds[i], hg_norm_w[i], *lat, hgrn_t, i)
        s5_args = (pm, tables, s5_d[i], s5_glu_w[i], s5_glu_b[i])
        s5_c, xr, xi = _s5_pallas(*s5_args, *ctx, None, i)
        s5_l, _, _ = _s5_pallas(*s5_args, *lat, s5_x0, i)
        ctx_states.append((rs, gs, jnp.swapaxes(hs, -1, -2), xr.reshape(BATCH, DIRS, S5_GROUPS, S5_N),
                           xi.reshape(BATCH, DIRS, S5_GROUPS, S5_N)))

        parts = ((ret_c, ret_l), (gdn_c, gdn_l), (hg_c, hg_l), (s5_c, s5_l))
        x = _outproj(parts, w_out, x, mod, i)
        x = _ffn(x, norm2_w[i][None], mod, w1, w3, w2, final_norm_w[None], i)

    y_prompt = x[0].reshape(BATCH, SEQ, D_MODEL)
    y_sample = x[1].reshape(DEC_BATCH, DEC_SEQ, D_MODEL)
    new_states = tuple(jnp.stack([s[j] for s in ctx_states], axis=1) for j in range(5))
    return (y_prompt, y_sample) + new_states
```

```python
import functools
import math

import jax
import jax.numpy as jnp
from jax import lax
from jax.experimental import pallas as pl
from jax.experimental.pallas import tpu as pltpu

F32 = jnp.float32
BF16 = jnp.bfloat16

D_MODEL = 2048
BATCH = 16
SEQ = 256
DEPTH = 2
DEC_BATCH = 8
DEC_SEQ = 1024
GRID_W = 64
HEAD_DIM = 128
GROUP_WIDTH = 512
N_HEADS = 4
S5_CH = 16
S5_GROUPS = 32
S5_N = 64
GDN_CONV = 3
CHUNK = 64
HG_CHUNK = 16
ROPE_BASE = 10000.0
FFN_HIDDEN = 5632
N_MOD = 6
EPS = 1e-6
LB_FLOOR = 1e-30

N_CTX_TOK = BATCH * SEQ
N_LAT_TOK = DEC_BATCH * DEC_SEQ
N_TOK = N_CTX_TOK + N_LAT_TOK
N_SEQ_ROWS = 16
PROJ_MAIN = 14 * GROUP_WIDTH
PROJ_SMALL = 128
VMEM_LIMIT = 56 * 1024 * 1024


def _seq_row(tile, tm):
    n_ctx = N_CTX_TOK // tm
    per_lat = DEC_SEQ // tm
    return jnp.where(tile < n_ctx, 0, 1 + (tile - n_ctx) // per_lat)


def _stream_specs(tm, width, n_col=1):
    n_ctx = N_CTX_TOK // tm

    def ctx_map(i, j):
        return jnp.minimum(i, n_ctx - 1), (jnp.where(i < n_ctx, j, n_col - 1) if n_col > 1 else 0)

    def lat_map(i, j):
        return jnp.maximum(i - n_ctx, 0), (jnp.where(i >= n_ctx, j, 0) if n_col > 1 else 0)

    return pl.BlockSpec((tm, width), ctx_map), pl.BlockSpec((tm, width), lat_map)


def _on_stream(tile, tm, fn):
    n_ctx = N_CTX_TOK // tm
    pl.when(tile < n_ctx)(functools.partial(fn, 0))
    pl.when(tile >= n_ctx)(functools.partial(fn, 1))


def _ada_body(c_ref, w_ref, b_ref, o_ref):
    cv = c_ref[...]
    s = cv * jax.nn.sigmoid(cv)
    o_ref[0] = jnp.dot(s.astype(BF16), w_ref[0].astype(BF16), preferred_element_type=F32) + b_ref[0]


def _ada(cvec, ada_w, ada_b):
    tn = 1024
    n = N_MOD * D_MODEL
    return pl.pallas_call(
        _ada_body,
        out_shape=jax.ShapeDtypeStruct((DEPTH, N_SEQ_ROWS, n), F32),
        grid=(DEPTH, n // tn),
        in_specs=[
            pl.BlockSpec((N_SEQ_ROWS, D_MODEL), lambda l, j: (0, 0)),
            pl.BlockSpec((1, D_MODEL, tn), lambda l, j: (l, 0, j)),
            pl.BlockSpec((1, 1, tn), lambda l, j: (l, 0, j)),
        ],
        out_specs=pl.BlockSpec((1, N_SEQ_ROWS, tn), lambda l, j: (l, 0, j)),
        compiler_params=pltpu.CompilerParams(
            dimension_semantics=("parallel", "parallel"), vmem_limit_bytes=VMEM_LIMIT),
        name="ada_mod",
    )(cvec, ada_w, ada_b.reshape(DEPTH, 1, n))


def _norm_mod(x, nw, sc, sh):
    ms = jnp.mean(x * x, axis=-1, keepdims=True)
    y = x * lax.rsqrt(ms + EPS) * nw
    return y * (1.0 + sc) + sh


PROJ_TM = 1024
PROJ_TN = 1024


def _inproj_body(x_ref, nw_ref, sc_ref, sh_ref, w_ref, ws_ref, o_ref, os_ref, h_ref):
    @pl.when(pl.program_id(1) == 0)
    def _():
        hb = _norm_mod(x_ref[...], nw_ref[...], sc_ref[...], sh_ref[...]).astype(BF16)
        h_ref[...] = hb
        os_ref[...] = jnp.dot(hb, ws_ref[...], preferred_element_type=F32)

    o_ref[...] = jnp.dot(h_ref[...], w_ref[...], preferred_element_type=F32)


def _inproj(x, nw, mod, w_main, w_small, layer):
    tm, tn = PROJ_TM, PROJ_TN
    return pl.pallas_call(
        _inproj_body,
        out_shape=(jax.ShapeDtypeStruct((N_TOK, PROJ_MAIN), F32),
                   jax.ShapeDtypeStruct((N_TOK, PROJ_SMALL), F32)),
        grid=(N_TOK // tm, PROJ_MAIN // tn),
        in_specs=[
            pl.BlockSpec((tm, D_MODEL), lambda i, j: (i, 0)),
            pl.BlockSpec((1, D_MODEL), lambda i, j: (0, 0)),
            pl.BlockSpec((None, None, 1, D_MODEL), lambda i, j: (_seq_row(i, tm), 1, 0, 0)),
            pl.BlockSpec((None, None, 1, D_MODEL), lambda i, j: (_seq_row(i, tm), 0, 0, 0)),
            pl.BlockSpec((None, D_MODEL, tn), lambda i, j: (layer, 0, j)),
            pl.BlockSpec((None, D_MODEL, PROJ_SMALL), lambda i, j: (layer, 0, 0)),
        ],
        out_specs=(pl.BlockSpec((tm, tn), lambda i, j: (i, j)),
                   pl.BlockSpec((tm, PROJ_SMALL), lambda i, j: (i, 0))),
        scratch_shapes=[pltpu.VMEM((tm, D_MODEL), BF16)],
        compiler_params=pltpu.CompilerParams(
            dimension_semantics=("parallel", "arbitrary"), vmem_limit_bytes=VMEM_LIMIT),
        name="in_proj",
    )(x, nw, mod, mod, w_main, w_small)


def _outproj_body(*refs):
    n_mix = 4
    m_refs, (w_ref, x_ref, g_ref, o_ref) = refs[:2 * n_mix], refs[2 * n_mix:]

    def compute(stream):
        acc = None
        for p in range(n_mix):
            part = jnp.dot(m_refs[2 * p + stream][...], w_ref[p * GROUP_WIDTH:(p + 1) * GROUP_WIDTH, :],
                           preferred_element_type=F32)
            acc = part if acc is None else acc + part
        o_ref[...] = x_ref[...] + g_ref[...] * acc

    _on_stream(pl.program_id(0), PROJ_TM, compute)


def _outproj(parts, w, x, mod, layer):
    tm, tn = PROJ_TM, 1024
    in_specs = []
    for _ in parts:
        in_specs += list(_stream_specs(tm, GROUP_WIDTH))
    in_specs += [pl.BlockSpec((None, D_MODEL, tn), lambda i, j: (layer, 0, j)),
                 pl.BlockSpec((tm, tn), lambda i, j: (i, j)),
                 pl.BlockSpec((None, None, 1, tn), lambda i, j: (_seq_row(i, tm), 2, 0, j))]
    return pl.pallas_call(
        _outproj_body,
        out_shape=jax.ShapeDtypeStruct((N_TOK, D_MODEL), F32),
        grid=(N_TOK // tm, D_MODEL // tn),
        in_specs=in_specs,
        out_specs=pl.BlockSpec((tm, tn), lambda i, j: (i, j)),
        compiler_params=pltpu.CompilerParams(
            dimension_semantics=("parallel", "arbitrary"), vmem_limit_bytes=VMEM_LIMIT),
        name="out_proj",
    )(*(a for pair in parts for a in pair), w, x, mod)


FFN_TM = 512
FFN_TH = 512


def _ffn_body(x_ref, nw_ref, sc_ref, sh_ref, g_ref, w1_ref, w3_ref, w2_ref, fw_ref, *rest, final_norm):
    o_refs, (h_ref, acc_ref) = rest[:-2], rest[-2:]
    i, k = pl.program_id(0), pl.program_id(1)

    @pl.when(k == 0)
    def _():
        h_ref[...] = _norm_mod(x_ref[...], nw_ref[...], sc_ref[...], sh_ref[...]).astype(BF16)
        acc_ref[...] = jnp.zeros_like(acc_ref)

    h = h_ref[...]
    a = jnp.dot(h, w1_ref[...], preferred_element_type=F32)
    b = jnp.dot(h, w3_ref[...], preferred_element_type=F32)
    g = (a * jax.nn.sigmoid(a) * b).astype(BF16)
    acc_ref[...] += jnp.dot(g, w2_ref[...], preferred_element_type=F32)

    @pl.when(k == pl.num_programs(1) - 1)
    def _():
        y = x_ref[...] + g_ref[...] * acc_ref[...]
        if not final_norm:
            o_refs[0][...] = y
        else:
            ms = jnp.mean(y * y, axis=-1, keepdims=True)
            y = y * lax.rsqrt(ms + EPS) * fw_ref[...]

            def write(stream):
                o_refs[stream][...] = y

            _on_stream(i, FFN_TM, write)


def _ffn(x, nw, mod, w1, w3, w2, fw, layer):
    final_norm = layer == DEPTH - 1
    tm, th = FFN_TM, FFN_TH
    if final_norm:
        out_shape = (jax.ShapeDtypeStruct((N_CTX_TOK, D_MODEL), F32), jax.ShapeDtypeStruct((N_LAT_TOK, D_MODEL), F32))
        out_specs = _stream_specs(tm, D_MODEL)
    else:
        out_shape = jax.ShapeDtypeStruct((N_TOK, D_MODEL), F32)
        out_specs = pl.BlockSpec((tm, D_MODEL), lambda i, k: (i, 0))
    return pl.pallas_call(
        functools.partial(_ffn_body, final_norm=final_norm),
        out_shape=out_shape,
        grid=(N_TOK // tm, FFN_HIDDEN // th),
        in_specs=[
            pl.BlockSpec((tm, D_MODEL), lambda i, k: (i, 0)),
            pl.BlockSpec((1, D_MODEL), lambda i, k: (0, 0)),
            pl.BlockSpec((None, None, 1, D_MODEL), lambda i, k: (_seq_row(i, tm), 4, 0, 0)),
            pl.BlockSpec((None, None, 1, D_MODEL), lambda i, k: (_seq_row(i, tm), 3, 0, 0)),
            pl.BlockSpec((None, None, 1, D_MODEL), lambda i, k: (_seq_row(i, tm), 5, 0, 0)),
            pl.BlockSpec((None, D_MODEL, th), lambda i, k: (layer, 0, k)),
            pl.BlockSpec((None, D_MODEL, th), lambda i, k: (layer, 0, k)),
            pl.BlockSpec((None, th, D_MODEL), lambda i, k: (layer, k, 0)),
            pl.BlockSpec((1, D_MODEL), lambda i, k: (0, 0)),
        ],
        out_specs=out_specs,
        scratch_shapes=[pltpu.VMEM((tm, D_MODEL), BF16), pltpu.VMEM((tm, D_MODEL), F32)],
        compiler_params=pltpu.CompilerParams(
            dimension_semantics=("arbitrary", "arbitrary"), vmem_limit_bytes=VMEM_LIMIT),
        name="ffn",
    )(x, nw, mod, mod, mod, w1, w3, w2, fw)


def _axial_rope(l):
    n_rows = l // GRID_W
    t_row = jnp.repeat(jnp.arange(n_rows, dtype=F32), GRID_W)
    t_col = jnp.tile(jnp.arange(GRID_W, dtype=F32), n_rows)
    n_freq = HEAD_DIM // 4
    inv = ROPE_BASE ** (-jnp.arange(n_freq, dtype=F32) / n_freq)
    ang = jnp.concatenate([t_row[:, None] * inv, t_col[:, None] * inv], axis=-1)
    return jnp.cos(ang), jnp.sin(ang)


MIX_CHUNK = 128
DIRS = 2
NT_DIMS = (((1,), (1,)), ((), ()))
TN_DIMS = (((0,), (0,)), ((), ()))


def _dot(a, b):
    return jnp.dot(a.astype(BF16), b.astype(BF16), preferred_element_type=F32)


def _dot_nt(a, b):
    return lax.dot_general(a.astype(BF16), b.astype(BF16), NT_DIMS, preferred_element_type=F32)


def _dot_tn(a, b):
    return lax.dot_general(a.astype(BF16), b.astype(BF16), TN_DIMS, preferred_element_type=F32)


def _silu(x):
    return x * jax.nn.sigmoid(x)


def _seq_spec(seq_len, width, row_off, col_blk):
    return pl.BlockSpec((seq_len, width), lambda b: (row_off + b, col_blk))


def _mix_params(n_par):
    return pltpu.CompilerParams(dimension_semantics=("parallel",) * n_par, vmem_limit_bytes=VMEM_LIMIT)


def _ret_body(*refs, n_chunks, use_rope, has_s0):
    it = iter(refs)
    lg_ref, q_ref, k_ref, v_ref, g_ref = (next(it) for _ in range(5))
    cos_ref, sin_ref = (next(it), next(it)) if use_rope else (None, None)
    s0_ref = next(it) if has_s0 else None
    o_ref, sf_ref, acc_ref, s_ref, intra_ref, qd_ref, kd_ref, cd_ref = (next(it) for _ in range(8))
    C, H = MIX_CHUNK, N_HEADS

    row = lax.broadcasted_iota(jnp.int32, (C, C), 0)
    col = lax.broadcasted_iota(jnp.int32, (C, C), 1)
    rel = (row - col).astype(F32)
    pos = lax.broadcasted_iota(jnp.int32, (C, HEAD_DIM), 0).astype(F32)
    for d in range(DIRS):
        for h in range(H):
            lg = lg_ref[d, h]
            if d == 0:
                intra_ref[d, h] = jnp.where(rel >= 0, jnp.exp(jnp.maximum(rel, 0.0) * lg), 0.0)
                qd_ref[d, h] = jnp.exp((pos + 1.0) * lg)
                kd_ref[d, h] = jnp.exp((C - 1.0 - pos) * lg)
            else:
                intra_ref[d, h] = jnp.where(rel <= 0, jnp.exp(jnp.maximum(-rel, 0.0) * lg), 0.0)
                qd_ref[d, h] = jnp.exp((C - pos) * lg)
                kd_ref[d, h] = jnp.exp(pos * lg)
            cd_ref[d, h] = jnp.exp(jnp.full((C, HEAD_DIM), C, F32) * lg)
    if has_s0:
        s_ref[...] = s0_ref[...]
    else:
        s_ref[...] = jnp.zeros_like(s_ref)
    acc_ref[...] = jnp.zeros_like(acc_ref)

    units = [(d, h) for d in range(DIRS) for h in range(H)]

    def chunk_step(n, carry):
        rows_d = [pl.ds(pl.multiple_of(c * C, C), C) for c in (n, n_chunks - 1 - n)]
        q_l, k_l, v_l, sc_l = [], [], [], []
        for d, h in units:
            rows, cols = rows_d[d], slice(h * HEAD_DIM, (h + 1) * HEAD_DIM)
            q, k = q_ref[rows, cols], k_ref[rows, cols]
            if use_rope:
                cs, sn = cos_ref[rows, :], sin_ref[rows, :]
                q = q * cs + pltpu.roll(q, HEAD_DIM // 2, 1) * sn
                k = k * cs + pltpu.roll(k, HEAD_DIM // 2, 1) * sn
            k = k * HEAD_DIM ** -0.5
            q_l.append(q)
            k_l.append(k)
            v_l.append(v_ref[rows, cols].astype(BF16))
            sc_l.append(_dot_nt(q, k) * intra_ref[d, h])
        for i, (d, h) in enumerate(units):
            cols = slice(h * HEAD_DIM, (h + 1) * HEAD_DIM)
            s = s_ref[d, h]
            acc_ref[rows_d[d], cols] += _dot(jnp.concatenate([sc_l[i], q_l[i] * qd_ref[d, h]], axis=1),
                                             jnp.concatenate([v_l[i], s.astype(BF16)], axis=0))
            s_ref[d, h] = cd_ref[d, h] * s + _dot_tn(k_l[i] * kd_ref[d, h], v_l[i])
        return carry

    lax.fori_loop(0, n_chunks, chunk_step, 0)
    sf_ref[...] = s_ref[...]

    def finish(n, carry):
        rows = pl.ds(pl.multiple_of(n * C, C), C)
        for h in range(H):
            cols = slice(h * HEAD_DIM, (h + 1) * HEAD_DIM)
            o = acc_ref[rows, cols]
            mu = jnp.mean(o, axis=-1, keepdims=True)
            oc = o - mu
            y = oc * lax.rsqrt(jnp.mean(oc * oc, axis=-1, keepdims=True) + EPS)
            o_ref[rows, cols] = (y * _silu(g_ref[rows, cols])).astype(BF16)
        return carry

    lax.fori_loop(0, n_chunks, finish, 0)


def _retention_pallas(pm, log_gamma, n_seq, seq_len, row_off, rope2, s0, layer):
    use_rope, has_s0 = rope2 is not None, s0 is not None
    st_shape = (DIRS, N_HEADS, HEAD_DIM, HEAD_DIM)
    in_specs = [pl.BlockSpec(memory_space=pltpu.SMEM)]
    in_specs += [_seq_spec(seq_len, GROUP_WIDTH, row_off, cb) for cb in range(4)]
    args = [log_gamma, pm, pm, pm, pm]
    if use_rope:
        in_specs += [pl.BlockSpec((seq_len, HEAD_DIM), lambda b: (0, 0))] * 2
        args += list(rope2)
    if has_s0:
        in_specs.append(pl.BlockSpec((None, None) + st_shape, lambda b: (b, layer, 0, 0, 0, 0)))
        args.append(s0)
    return pl.pallas_call(
        functools.partial(_ret_body, n_chunks=seq_len // MIX_CHUNK, use_rope=use_rope, has_s0=has_s0),
        out_shape=(jax.ShapeDtypeStruct((n_seq * seq_len, GROUP_WIDTH), BF16),
                   jax.ShapeDtypeStruct((n_seq,) + st_shape, F32)),
        grid=(n_seq,),
        in_specs=in_specs,
        out_specs=(pl.BlockSpec((seq_len, GROUP_WIDTH), lambda b: (b, 0)),
                   pl.BlockSpec((None,) + st_shape, lambda b: (b, 0, 0, 0, 0))),
        scratch_shapes=[pltpu.VMEM((seq_len, GROUP_WIDTH), F32), pltpu.VMEM(st_shape, F32),
                        pltpu.VMEM((DIRS, N_HEADS, MIX_CHUNK, MIX_CHUNK), F32),
                        pltpu.VMEM((DIRS, N_HEADS, MIX_CHUNK, HEAD_DIM), F32),
                        pltpu.VMEM((DIRS, N_HEADS, MIX_CHUNK, HEAD_DIM), F32),
                        pltpu.VMEM((DIRS, N_HEADS, MIX_CHUNK, HEAD_DIM), F32)],
        compiler_params=_mix_params(1),
        name="retention",
    )(*args)


def _rope_tables(l):
    cos, sin = _axial_rope(l)
    return jnp.concatenate([cos, cos], axis=-1), jnp.concatenate([-sin, sin], axis=-1)


S5_HALF_G = S5_GROUPS // 2
S5_HALF_U = S5_HALF_G * S5_CH
S5_HALF_X = S5_HALF_G * S5_N
S5_X = S5_GROUPS * S5_N
S5_TC = 512
S5_BLK = 8
S5_TABS = 4


def _s5_tables(a_re, a_im, log_step, b_re, b_im, c_re, c_im):
    dt = jnp.exp(log_step)[..., None]
    mag = jnp.exp(a_re * dt)
    ab_re = mag * jnp.cos(a_im * dt)
    ab_im = mag * jnp.sin(a_im * dt)
    den = a_re * a_re + a_im * a_im
    nr = ab_re - 1.0
    f_re = (nr * a_re + ab_im * a_im) / den
    f_im = (ab_im * a_re - nr * a_im) / den
    bb_re = f_re[..., None] * b_re - f_im[..., None] * b_im
    bb_im = f_re[..., None] * b_im + f_im[..., None] * b_re
    eye = jnp.eye(S5_HALF_G, dtype=F32)

    def in_mat(bb):
        bb = bb.reshape(DIRS, 2, S5_HALF_G, S5_N, S5_CH)
        return jnp.einsum('dhgnc,gk->dhgckn', bb, eye).reshape(DIRS, 2, S5_HALF_U, S5_HALF_X)

    def out_mat(cc):
        cc = cc.reshape(DIRS, 2, S5_HALF_G, S5_CH, S5_N)
        return jnp.einsum('dhgcn,gk->dhgnkc', cc, eye).reshape(DIRS, 2, S5_HALF_X, S5_HALF_U)

    bm = jnp.concatenate([in_mat(bb_re), in_mat(bb_im)], axis=-1).astype(BF16)
    cm = jnp.concatenate([out_mat(c_re), -out_mat(c_im)], axis=-2).astype(BF16)
    t = jnp.arange(S5_BLK, dtype=F32)
    order = jnp.stack([t, S5_BLK - 1.0 - t])
    shifts = 2.0 ** jnp.arange(S5_TABS - 1, dtype=F32)
    expo = jnp.concatenate([jnp.where(order[:, None, :] >= shifts[None, :, None], shifts[None, :, None], jnp.nan),
                            order[:, None, :] + 1.0], axis=1)
    live = ~jnp.isnan(expo)
    e = jnp.where(live, expo, 0.0)[..., None]
    adt_re = (a_re * dt).reshape(DIRS, 1, 1, S5_X)
    adt_im = (a_im * dt).reshape(DIRS, 1, 1, S5_X)
    pmag = jnp.where(live[..., None], jnp.exp(e * adt_re), 0.0)
    pw_re = pmag * jnp.cos(e * adt_im)
    pw_im = pmag * jnp.sin(e * adt_im)
    return bm, cm, pw_re, pw_im


def _gelu_tanh(x):
    return 0.5 * x * (1.0 + jnp.tanh(math.sqrt(2.0 / math.pi) * (x + 0.044715 * (x * x * x))))


def _s5_body(*refs, seq_len, has_s0):
    it = iter(refs)
    u_ref, bm_ref, cm_ref, pwr_ref, pwi_ref, d_ref, gw_ref, gb_ref = (next(it) for _ in range(8))
    x0r_ref, x0i_ref = (next(it), next(it)) if has_s0 else (None, None)
    o_ref, sfr_ref, sfi_ref, y_ref, xr_ref, xi_ref, xb_ref = (next(it) for _ in range(7))
    tc = min(seq_len, S5_TC)
    n_tiles = seq_len // tc
    n_pair = tc // (2 * S5_BLK)

    y_ref[...] = d_ref[...] * u_ref[...]
    for d in range(DIRS):
        last = S5_BLK - 1 if d == 0 else 0
        for hf in range(2):
            xs = slice(hf * S5_HALF_X, (hf + 1) * S5_HALF_X)
            us = slice(hf * S5_HALF_U, (hf + 1) * S5_HALF_U)

            def scan_block(x_in, carry, d=d, xs=xs, last=last):
                xr, xi = x_in
                car_re, car_im = carry
                for k in range(S5_TABS - 1):
                    s = 1 << k
                    shift = s if d == 0 else S5_BLK - s
                    p_re, p_im = pwr_ref[d, k, :, xs], pwi_ref[d, k, :, xs]
                    sr, si = pltpu.roll(xr, shift, 0), pltpu.roll(xi, shift, 0)
                    xr, xi = xr + p_re * sr - p_im * si, xi + p_re * si + p_im * sr
                p_re, p_im = pwr_ref[d, S5_TABS - 1, :, xs], pwi_ref[d, S5_TABS - 1, :, xs]
                xr, xi = xr + p_re * car_re - p_im * car_im, xi + p_re * car_im + p_im * car_re
                return (xr, xi), (xr[last:last + 1, :], xi[last:last + 1, :])

            def scan_pair(j, carry, d=d, scan_block=scan_block):
                pair = j if d == 0 else n_pair - 1 - j
                rows = pl.ds(pl.multiple_of(pair * 2 * S5_BLK, 2 * S5_BLK), 2 * S5_BLK)
                xr2, xi2 = xr_ref[rows, :], xi_ref[rows, :]
                halves = [(xr2[:S5_BLK], xi2[:S5_BLK]), (xr2[S5_BLK:], xi2[S5_BLK:])]
                order = (0, 1) if d == 0 else (1, 0)
                out = [None, None]
                for idx in order:
                    out[idx], carry = scan_block(halves[idx], carry)
                xb_ref[rows, :S5_HALF_X] = jnp.concatenate([out[0][0], out[1][0]], axis=0).astype(BF16)
                xb_ref[rows, S5_HALF_X:] = jnp.concatenate([out[0][1], out[1][1]], axis=0).astype(BF16)
                return carry

            def scan_tile(i, carry, d=d, hf=hf, us=us, scan_pair=scan_pair):
                tile = i if d == 0 else n_tiles - 1 - i
                rows = pl.ds(pl.multiple_of(tile * tc, tc), tc)
                bu = _dot(u_ref[rows, us], bm_ref[d, hf])
                xr_ref[...] = bu[:, :S5_HALF_X]
                xi_ref[...] = bu[:, S5_HALF_X:]
                carry = lax.fori_loop(0, n_pair, scan_pair, carry)
                y_ref[rows, us] += jnp.dot(xb_ref[...], cm_ref[d, hf], preferred_element_type=F32)
                return carry

            if has_s0:
                carry0 = (x0r_ref[d, :, xs], x0i_ref[d, :, xs])
            else:
                carry0 = (jnp.zeros((1, S5_HALF_X), F32), jnp.zeros((1, S5_HALF_X), F32))
            car_re, car_im = lax.fori_loop(0, n_tiles, scan_tile, carry0)
            sfr_ref[d, :, xs] = car_re
            sfi_ref[d, :, xs] = car_im

    z = _gelu_tanh(y_ref[...])
    o_ref[...] = (z * jax.nn.sigmoid(_dot(z, gw_ref[...]) + gb_ref[...])).astype(BF16)


def _s5_pallas(pm, tables, s5_d, glu_w, glu_b, n_seq, seq_len, row_off, x0, layer):
    bm, cm, pw_re, pw_im = tables
    has_s0 = x0 is not None
    full = lambda shape: pl.BlockSpec(shape, lambda b: (0,) * len(shape))
    of_layer = lambda t: pl.BlockSpec((None,) + t.shape[1:], lambda b: (layer,) + (0,) * (t.ndim - 1))
    in_specs = [_seq_spec(seq_len, GROUP_WIDTH, row_off, 13),
                of_layer(bm), of_layer(cm), of_layer(pw_re), of_layer(pw_im),
                full((1, GROUP_WIDTH)), full((GROUP_WIDTH, GROUP_WIDTH)), full((1, GROUP_WIDTH))]
    args = [pm, bm, cm, pw_re, pw_im, s5_d.reshape(1, GROUP_WIDTH), glu_w.astype(BF16),
            glu_b.reshape(1, GROUP_WIDTH)]
    if has_s0:
        in_specs += [pl.BlockSpec((None, None, DIRS, 1, S5_X), lambda b: (b, layer, 0, 0, 0))] * 2
        args += list(x0)
    st = jax.ShapeDtypeStruct((n_seq, DIRS, 1, S5_X), F32)
    st_spec = pl.BlockSpec((None, DIRS, 1, S5_X), lambda b: (b, 0, 0, 0))
    tc = min(seq_len, S5_TC)
    return pl.pallas_call(
        functools.partial(_s5_body, seq_len=seq_len, has_s0=has_s0),
        out_shape=(jax.ShapeDtypeStruct((n_seq * seq_len, GROUP_WIDTH), BF16), st, st),
        grid=(n_seq,),
        in_specs=in_specs,
        out_specs=(pl.BlockSpec((seq_len, GROUP_WIDTH), lambda b: (b, 0)), st_spec, st_spec),
        scratch_shapes=[pltpu.VMEM((seq_len, GROUP_WIDTH), F32),
                        pltpu.VMEM((tc, S5_HALF_X), F32), pltpu.VMEM((tc, S5_HALF_X), F32),
                        pltpu.VMEM((tc, 2 * S5_HALF_X), BF16)],
        compiler_params=_mix_params(1),
        name="s5",
    )(*args)


GLA_LEVELS = 7


def _chunk_cumsum_rows(x, rowi):
    s = 1
    while s < MIX_CHUNK:
        x = x + jnp.where(rowi >= s, pltpu.roll(x, s, 0), 0.0)
        s *= 2
    return x


def _gla_body(*refs, n_chunks, has_s0):
    it = iter(refs)
    q_ref, f0_ref, f1_ref, i_ref, g_ref, lb_ref, nw_ref = (next(it) for _ in range(7))
    s0_ref = next(it) if has_s0 else None
    o_ref, sf_ref, acc_ref, s_ref, code_ref = (next(it) for _ in range(5))
    C, H = MIX_CHUNK, N_HEADS
    f_refs = (f0_ref, f1_ref)

    rowi = lax.broadcasted_iota(jnp.int32, (C, HEAD_DIM), 0)
    ri = lax.broadcasted_iota(jnp.int32, (C, C), 0)
    ci = lax.broadcasted_iota(jnp.int32, (C, C), 1)
    top_bit = 31 - lax.clz(ri ^ ci)
    code_ref[...] = jnp.where(ri > ci, top_bit, jnp.where(ri < ci, -1 - top_bit, GLA_LEVELS))
    if has_s0:
        s_ref[...] = s0_ref[...]
    else:
        s_ref[...] = jnp.zeros_like(s_ref)
    acc_ref[...] = jnp.zeros_like(acc_ref)

    def chunk_step(n, carry):
        for d in range(DIRS):
            c = n if d == 0 else n_chunks - 1 - n
            rows = pl.ds(pl.multiple_of(c * C, C), C)
            for h in range(H):
                cols = slice(h * HEAD_DIM, (h + 1) * HEAD_DIM)
                code = code_ref[...]
                q = _silu(q_ref[rows, cols]) * HEAD_DIM ** -0.5
                v = i_ref[rows, cols].astype(BF16)
                fx = f_refs[d][rows, cols]
                lb = lb_ref[d:d + 1, cols]
                sig = 1.0 / (1.0 + jnp.exp(-fx))
                logf = jnp.log2(jnp.maximum(lb, LB_FLOOR) + (1.0 - lb) * sig)
                k = (1.0 - lb) * (1.0 - sig)
                cum = _chunk_cumsum_rows(logf, rowi)
                own = cum
                attn = jnp.where(code == GLA_LEVELS, _dot_nt(q, k), 0.0)
                for lvl in range(GLA_LEVELS):
                    m = 1 << lvl
                    prev = pltpu.roll(own, m, 0)
                    pre = jnp.minimum(cum - prev, 0.0)
                    suf = own - cum
                    if d == 0:
                        sc = _dot_nt(q * jnp.exp2(pre), k * jnp.exp2(suf))
                        hit = code == lvl
                    else:
                        sc = _dot_nt(q * jnp.exp2(suf + logf), k * jnp.exp2(jnp.minimum(pre - logf, 0.0)))
                        hit = code == -1 - lvl
                    attn = jnp.where(hit, sc, attn)
                    own = jnp.where(((rowi >> lvl) & 1) == 0, pltpu.roll(own, C - m, 0), own)
                tot = own
                st = s_ref[d, h]
                if d == 0:
                    q_in, k_out = q * jnp.exp2(cum), k * jnp.exp2(tot - cum)
                else:
                    q_in, k_out = q * jnp.exp2(tot - cum + logf), k * jnp.exp2(cum - logf)
                acc_ref[rows, cols] += _dot(attn, v) + _dot_nt(q_in, st)
                s_ref[d, h] = jnp.exp2(tot[0:1, :]) * st + _dot_tn(v, k_out)
        return carry

    lax.fori_loop(0, n_chunks, chunk_step, 0)
    sf_ref[...] = s_ref[...]

    def finish(n, carry):
        rows = pl.ds(pl.multiple_of(n * C, C), C)
        for h in range(H):
            cols = slice(h * HEAD_DIM, (h + 1) * HEAD_DIM)
            o = acc_ref[rows, cols]
            y = o * lax.rsqrt(jnp.mean(o * o, axis=-1, keepdims=True) + EPS) * nw_ref[...]
            o_ref[rows, cols] = (y * _silu(g_ref[rows, cols])).astype(BF16)
        return carry

    lax.fori_loop(0, n_chunks, finish, 0)


def _gla_pallas(pm, lower_bound, norm_w, n_seq, seq_len, row_off, s0_t, layer):
    has_s0 = s0_t is not None
    st_shape = (DIRS, N_HEADS, HEAD_DIM, HEAD_DIM)
    in_specs = [_seq_spec(seq_len, GROUP_WIDTH, row_off, cb) for cb in (8, 9, 10, 11, 12)]
    in_specs += [pl.BlockSpec((DIRS, GROUP_WIDTH), lambda b: (0, 0)), pl.BlockSpec((1, HEAD_DIM), lambda b: (0, 0))]
    args = [pm] * 5 + [lower_bound, norm_w.reshape(1, HEAD_DIM)]
    if has_s0:
        in_specs.append(pl.BlockSpec((None, None) + st_shape, lambda b: (b, layer, 0, 0, 0, 0)))
        args.append(s0_t)
    return pl.pallas_call(
        functools.partial(_gla_body, n_chunks=seq_len // MIX_CHUNK, has_s0=has_s0),
        out_shape=(jax.ShapeDtypeStruct((n_seq * seq_len, GROUP_WIDTH), BF16),
                   jax.ShapeDtypeStruct((n_seq,) + st_shape, F32)),
        grid=(n_seq,),
        in_specs=in_specs,
        out_specs=(pl.BlockSpec((seq_len, GROUP_WIDTH), lambda b: (b, 0)),
                   pl.BlockSpec((None,) + st_shape, lambda b: (b, 0, 0, 0, 0))),
        scratch_shapes=[pltpu.VMEM((seq_len, GROUP_WIDTH), F32), pltpu.VMEM(st_shape, F32),
                        pltpu.VMEM((MIX_CHUNK, MIX_CHUNK), jnp.int32)],
        compiler_params=_mix_params(1),
        name="hgrn2",
    )(*args)


GDN_BLOCK_BITS = 4
GDN_MERGES = 3
N_GATES = DIRS * N_HEADS


def _softplus(x):
    return jnp.maximum(x, 0.0) + jnp.log1p(jnp.exp(-jnp.abs(x)))


def _gdn_body(*refs, seq_len, n_sub, has_s0):
    it = iter(refs)
    (q_ref, k_ref, v_ref, g_ref, ps_ref, pst_ref, cw_ref, prow_ref, pcol_ref, nw_ref) = (next(it) for _ in range(10))
    s0_ref = next(it) if has_s0 else None
    (o_ref, sf_ref, acc_ref, s_ref, qn_ref, kn_ref, vn_ref, gate_ref, gatet_ref, blev_ref) = (
        next(it) for _ in range(10))
    C, H, L = MIX_CHUNK, N_HEADS, seq_len
    R = n_sub * L
    n_chunks = L // C
    w = GROUP_WIDTH

    pos = lax.broadcasted_iota(jnp.int32, (R, HEAD_DIM), 0) & (L - 1)
    for part, (src, dst) in enumerate(((q_ref, qn_ref), (k_ref, kn_ref), (v_ref, vn_ref))):
        for h in range(H):
            cols = slice(h * HEAD_DIM, (h + 1) * HEAD_DIM)
            wc = slice(part * w + h * HEAD_DIM, part * w + (h + 1) * HEAD_DIM)
            x = src[:, cols]
            x_prev = jnp.where(pos >= 1, pltpu.roll(x, 1, 0), 0.0)
            x_next = jnp.where(pos < L - 1, pltpu.roll(x, R - 1, 0), 0.0)
            y = _silu(cw_ref[0:1, wc] * x_prev + cw_ref[1:2, wc] * x + cw_ref[2:3, wc] * x_next)
            if part < 2:
                y = y * lax.rsqrt(jnp.sum(y * y, axis=-1, keepdims=True) + EPS)
            if part == 0:
                y = y * HEAD_DIM ** -0.5
            dst[:, cols] = y

    rowi = lax.broadcasted_iota(jnp.int32, (C, HEAD_DIM), 0)
    lane = lax.broadcasted_iota(jnp.int32, (C, HEAD_DIM), 1)
    lane_t = lax.broadcasted_iota(jnp.int32, (2 * N_GATES, C), 1)
    row_t = lax.broadcasted_iota(jnp.int32, (2 * N_GATES, C), 0)
    for c in range(R // C):
        rows = slice(c * C, (c + 1) * C)
        a = ps_ref[rows, :]
        la = -jnp.exp(prow_ref[0:1, :]) * _softplus(a + prow_ref[1:2, :])
        pre = _chunk_cumsum_rows(la, rowi)
        tot = jnp.broadcast_to(pre[C - 1:C, :], (C, HEAD_DIM))
        g = jnp.where(lane < N_HEADS, pre, tot - pre + la)
        gate_ref[rows, :] = jnp.where(lane < N_GATES, g,
                                      jnp.where(lane < 2 * N_GATES, jax.nn.sigmoid(a),
                                                pltpu.roll(tot, 2 * N_GATES, 1)))
        at = pst_ref[:, rows]
        lat = -jnp.exp(pcol_ref[:, 0:1]) * _softplus(at + pcol_ref[:, 1:2])
        pre_t, s = lat, 1
        while s < C:
            pre_t = pre_t + jnp.where(lane_t >= s, pltpu.roll(pre_t, s, 1), 0.0)
            s *= 2
        tot_t = jnp.broadcast_to(pre_t[:, C - 1:C], (2 * N_GATES, C))
        g_t = jnp.where(row_t < N_HEADS, pre_t, tot_t - pre_t + lat)
        gatet_ref[:, rows] = jnp.where(row_t < N_GATES, g_t, jax.nn.sigmoid(at))

    ri = lax.broadcasted_iota(jnp.int32, (C, C), 0)
    ci = lax.broadcasted_iota(jnp.int32, (C, C), 1)
    bx = (ri >> GDN_BLOCK_BITS) ^ (ci >> GDN_BLOCK_BITS)
    blev_ref[...] = jnp.where(bx == 0, 0, 32 - lax.clz(bx))
    if has_s0:
        for s in range(n_sub):
            s_ref[s * DIRS:(s + 1) * DIRS] = s0_ref[s]
    else:
        s_ref[...] = jnp.zeros_like(s_ref)
    acc_ref[...] = jnp.zeros_like(acc_ref)

    units = [(s, d, h) for s in range(n_sub) for d in range(DIRS) for h in range(H)]

    def chunk_step(n, carry):
        eye = jnp.where(ri == ci, 1.0, 0.0)
        blev = blev_ref[...]
        rows_sd = [[pl.ds(pl.multiple_of(s * L + c * C, C), C) for c in (n, n_chunks - 1 - n)] for s in range(n_sub)]
        incl_d = [ri >= ci, ri <= ci]
        strict_d = [ri > ci, ri < ci]
        q_l, et_l, ek_l, rhs_l, a_l, attn_l = ([] for _ in range(6))
        for s, d, h in units:
            rows, cols, cg = rows_sd[s][d], slice(h * HEAD_DIM, (h + 1) * HEAD_DIM), d * N_HEADS + h
            q, k, v = qn_ref[rows, cols], kn_ref[rows, cols], vn_ref[rows, cols]
            g_i = jnp.broadcast_to(gate_ref[rows, cg:cg + 1], (C, C))
            b_i = jnp.broadcast_to(gate_ref[rows, N_GATES + cg:N_GATES + cg + 1], (C, C))
            tot = jnp.broadcast_to(gate_ref[rows, 2 * N_GATES + cg:2 * N_GATES + cg + 1], (C, C))
            g_j = gatet_ref[cg:cg + 1, rows]
            decay = jnp.where(incl_d[d], jnp.exp(jnp.minimum(g_i - g_j, 0.0)), 0.0)
            e_g = jnp.exp(g_i)
            kb = k.astype(BF16)
            a_l.append(jnp.where(strict_d[d], _dot_nt(kb, kb) * b_i * decay, 0.0))
            attn_l.append(_dot_nt(q, kb) * decay)
            rhs_l.append(jnp.concatenate([v * b_i, k * (b_i * e_g)], axis=1).astype(BF16))
            q_l.append(q * e_g)
            ek_l.append(k * jnp.exp(tot - g_i))
            et_l.append(jnp.exp(tot))
        b_l = [-jnp.where(blev == 0, a, 0.0) for a in a_l]
        p_l = [eye + b for b in b_l]
        for _ in range(GDN_BLOCK_BITS - 1):
            b_l = [_dot(b, b) for b in b_l]
            p_l = [p + _dot(p, b) for p, b in zip(p_l, b_l)]
        for lvl in range(1, GDN_MERGES + 1):
            ep_l = [_dot(jnp.where(blev == lvl, a, 0.0), p) for a, p in zip(a_l, p_l)]
            p_l = [p - _dot(p, ep) for p, ep in zip(p_l, ep_l)]
        uw_l = [_dot(p, rhs) for p, rhs in zip(p_l, rhs_l)]
        s_l = [s_ref[s * DIRS + d, h] for s, d, h in units]
        vn_l = [uw[:, :HEAD_DIM] - _dot(uw[:, HEAD_DIM:], st) for uw, st in zip(uw_l, s_l)]
        for i, (s, d, h) in enumerate(units):
            cols = slice(h * HEAD_DIM, (h + 1) * HEAD_DIM)
            acc_ref[rows_sd[s][d], cols] += _dot(jnp.concatenate([q_l[i], attn_l[i]], axis=1),
                                                 jnp.concatenate([s_l[i], vn_l[i]], axis=0))
            s_ref[s * DIRS + d, h] = s_l[i] * et_l[i] + _dot_tn(ek_l[i], vn_l[i])
        return carry

    lax.fori_loop(0, n_chunks, chunk_step, 0)
    sf_ref[...] = s_ref[...]

    def finish(n, carry):
        rows = pl.ds(pl.multiple_of(n * C, C), C)
        for h in range(H):
            cols = slice(h * HEAD_DIM, (h + 1) * HEAD_DIM)
            o = acc_ref[rows, cols]
            y = o * lax.rsqrt(jnp.mean(o * o, axis=-1, keepdims=True) + EPS) * nw_ref[...]
            o_ref[rows, cols] = (y * _silu(g_ref[rows, cols])).astype(BF16)
        return carry

    lax.fori_loop(0, R // C, finish, 0)


GDN_CTX_SUB = 2
GDN_LAT_SUB = 2
GDN_DOUBLE_BUFFER_BYTES = 2 * 1024 * 1024


def _gdn_pallas(pm, ps, ps_t, conv_w, a_log, dt_bias, norm_w, n_seq, seq_len, row_off, s0, layer):
    has_s0 = s0 is not None
    n_sub = GDN_LAT_SUB if has_s0 else GDN_CTX_SUB
    n_seq, seq_len, row_off = n_seq // n_sub, seq_len * n_sub, row_off // n_sub
    st_shape = (n_sub * DIRS, N_HEADS, HEAD_DIM, HEAD_DIM)
    par = jnp.stack([a_log.reshape(N_GATES), dt_bias.reshape(N_GATES)])
    par_row = jnp.pad(par, ((0, 0), (0, HEAD_DIM - N_GATES)))
    par_col = jnp.pad(par.T, ((0, N_GATES), (0, 0)))
    full = lambda shape: pl.BlockSpec(shape, lambda b: (0,) * len(shape))
    big_mode = pl.Buffered(1) if seq_len * GROUP_WIDTH * 4 > GDN_DOUBLE_BUFFER_BYTES else None
    in_specs = [pl.BlockSpec((seq_len, GROUP_WIDTH), functools.partial(lambda cb, b: (row_off + b, cb), cb),
                             pipeline_mode=big_mode) for cb in (4, 5, 6, 7)]
    in_specs += [_seq_spec(seq_len, PROJ_SMALL, row_off, 0),
                 pl.BlockSpec((2 * N_GATES, seq_len), lambda b: (0, row_off + b)),
                 full((GDN_CONV, 3 * GROUP_WIDTH)), full((2, HEAD_DIM)), full((2 * N_GATES, 2)), full((1, HEAD_DIM))]
    args = [pm] * 4 + [ps, ps_t, conv_w, par_row, par_col, norm_w.reshape(1, HEAD_DIM)]
    if has_s0:
        in_specs.append(pl.BlockSpec((n_sub, None, DIRS, N_HEADS, HEAD_DIM, HEAD_DIM),
                                     lambda b: (b, layer, 0, 0, 0, 0)))
        args.append(s0)
    seq_f32 = lambda width: pltpu.VMEM((seq_len, width), F32)
    out, s_fin = pl.pallas_call(
        functools.partial(_gdn_body, seq_len=seq_len // n_sub, n_sub=n_sub, has_s0=has_s0),
        out_shape=(jax.ShapeDtypeStruct((n_seq * seq_len, GROUP_WIDTH), BF16),
                   jax.ShapeDtypeStruct((n_seq,) + st_shape, F32)),
        grid=(n_seq,),
        in_specs=in_specs,
        out_specs=(pl.BlockSpec((seq_len, GROUP_WIDTH), lambda b: (b, 0)),
                   pl.BlockSpec((None,) + st_shape, lambda b: (b, 0, 0, 0, 0))),
        scratch_shapes=[seq_f32(GROUP_WIDTH), pltpu.VMEM(st_shape, F32),
                        seq_f32(GROUP_WIDTH), seq_f32(GROUP_WIDTH), seq_f32(GROUP_WIDTH),
                        seq_f32(HEAD_DIM), pltpu.VMEM((2 * N_GATES, seq_len), F32),
                        pltpu.VMEM((MIX_CHUNK, MIX_CHUNK), jnp.int32)],
        compiler_params=_mix_params(1),
        name="gated_delta",
    )(*args)
    return out, s_fin.reshape(n_seq * n_sub, DIRS, N_HEADS, HEAD_DIM, HEAD_DIM)


def kernel(x_prompt, x_sample, state_ret, state_gdn, state_hgrn, state_s5_re, state_s5_im, c, c_ctx, norm1_w, norm2_w, final_norm_w, ada_w, ada_b, in_proj, out_proj, ret_decay_logit, gdn_conv, gdn_a_log, gdn_dt_bias, gdn_norm_w, hg_lb_param, hg_norm_w, s5_a_re, s5_a_im, s5_b_re, s5_b_im, s5_c_re, s5_c_im, s5_log_step, s5_d, s5_glu_w, s5_glu_b, ffn_w1, ffn_w3, ffn_w2):
    lb_soft = jax.nn.softmax(hg_lb_param, axis=0)
    lower_bounds = jnp.cumsum(lb_soft, axis=0) - lb_soft[0]
    rope2 = _rope_tables(DEC_SEQ)

    cvec = jnp.zeros((N_SEQ_ROWS, D_MODEL), F32).at[0].set(c_ctx).at[1:1 + DEC_BATCH].set(c)
    mod_all = _ada(cvec, ada_w, ada_b).reshape(DEPTH, N_SEQ_ROWS, N_MOD, 1, D_MODEL)

    gate0 = 8 * GROUP_WIDTH
    w_in = in_proj.astype(BF16)
    w_main = jnp.concatenate([w_in[:, :, :gate0], w_in[:, :, gate0 + 2 * N_GATES:]], axis=-1)
    w_small = jnp.pad(w_in[:, :, gate0:gate0 + 2 * N_GATES], ((0, 0), (0, 0), (0, PROJ_SMALL - 2 * N_GATES)))
    w_out = out_proj.astype(BF16)
    w1, w3, w2 = ffn_w1.astype(BF16), ffn_w3.astype(BF16), ffn_w2.astype(BF16)

    x = jnp.concatenate([x_prompt.reshape(N_CTX_TOK, D_MODEL), x_sample.reshape(N_LAT_TOK, D_MODEL)], axis=0)
    hgrn_t = jnp.swapaxes(state_hgrn, -1, -2)
    s5_x0 = (state_s5_re.reshape(DEC_BATCH, DEPTH, DIRS, 1, S5_X), state_s5_im.reshape(DEC_BATCH, DEPTH, DIRS, 1, S5_X))
    lat_off = N_CTX_TOK // DEC_SEQ
    tables = jax.vmap(_s5_tables)(s5_a_re, s5_a_im, s5_log_step, s5_b_re, s5_b_im, s5_c_re, s5_c_im)
    ctx_states = []
    for i in range(DEPTH):
        mod = mod_all[i]
        pm, ps = _inproj(x, norm1_w[i][None], mod, w_main, w_small, i)
        ps_t = ps[:, :2 * N_GATES].T
        lg = jax.nn.log_sigmoid(ret_decay_logit[i])
        ctx = (BATCH, SEQ, 0)
        lat = (DEC_BATCH, DEC_SEQ, lat_off)

        ret_c, rs = _retention_pallas(pm, lg, *ctx, None, None, i)
        ret_l, _ = _retention_pallas(pm, lg, *lat, rope2, state_ret, i)
        gdn_args = (pm, ps, ps_t, gdn_conv[i], gdn_a_log[i], gdn_dt_bias[i], gdn_norm_w[i])
        gdn_c, gs = _gdn_pallas(*gdn_args, *ctx, None, i)
        gdn_l, _ = _gdn_pallas(*gdn_args, *lat, state_gdn, i)
        hg_c, hs = _gla_pallas(pm, lower_bounds[i], hg_norm_w[i], *ctx, None, i)
        hg_l, _ = _gla_pallas(pm, lower_bounds[i], hg_norm_w[i], *lat, hgrn_t, i)
        s5_args = (pm, tables, s5_d[i], s5_glu_w[i], s5_glu_b[i])
        s5_c, xr, xi = _s5_pallas(*s5_args, *ctx, None, i)
        s5_l, _, _ = _s5_pallas(*s5_args, *lat, s5_x0, i)
        ctx_states.append((rs, gs, jnp.swapaxes(hs, -1, -2), xr.reshape(BATCH, DIRS, S5_GROUPS, S5_N),
                           xi.reshape(BATCH, DIRS, S5_GROUPS, S5_N)))

        parts = ((ret_c, ret_l), (gdn_c, gdn_l), (hg_c, hg_l), (s5_c, s5_l))
        x = _outproj(parts, w_out, x, mod, i)
        x = _ffn(x, norm2_w[i][None], mod, w1, w3, w2, final_norm_w[None], i)

    y_prompt = x[0].reshape(BATCH, SEQ, D_MODEL)
    y_sample = x[1].reshape(DEC_BATCH, DEC_SEQ, D_MODEL)
    new_states = tuple(jnp.stack([s[j] for s in ctx_states], axis=1) for j in range(5))
    return (y_prompt, y_sample) + new_states
```

```python
import functools
import math

import jax
import jax.numpy as jnp
from jax import lax
from jax.experimental import pallas as pl
from jax.experimental.pallas import tpu as pltpu

F32 = jnp.float32
BF16 = jnp.bfloat16

D_MODEL = 2048
BATCH = 16
SEQ = 256
DEPTH = 2
DEC_BATCH = 8
DEC_SEQ = 1024
GRID_W = 64
HEAD_DIM = 128
GROUP_WIDTH = 512
N_HEADS = 4
S5_CH = 16
S5_GROUPS = 32
S5_N = 64
GDN_CONV = 3
CHUNK = 64
HG_CHUNK = 16
ROPE_BASE = 10000.0
FFN_HIDDEN = 5632
N_MOD = 6
EPS = 1e-6
LB_FLOOR = 1e-30

N_CTX_TOK = BATCH * SEQ
N_LAT_TOK = DEC_BATCH * DEC_SEQ
N_TOK = N_CTX_TOK + N_LAT_TOK
N_SEQ_ROWS = 16
PROJ_MAIN = 14 * GROUP_WIDTH
PROJ_SMALL = 128
VMEM_LIMIT = 56 * 1024 * 1024


def _seq_row(tile, tm):
    n_ctx = N_CTX_TOK // tm
    per_lat = DEC_SEQ // tm
    return jnp.where(tile < n_ctx, 0, 1 + (tile - n_ctx) // per_lat)


def _stream_specs(tm, width, n_col=1):
    n_ctx = N_CTX_TOK // tm

    def ctx_map(i, j):
        return jnp.minimum(i, n_ctx - 1), (jnp.where(i < n_ctx, j, n_col - 1) if n_col > 1 else 0)

    def lat_map(i, j):
        return jnp.maximum(i - n_ctx, 0), (jnp.where(i >= n_ctx, j, 0) if n_col > 1 else 0)

    return pl.BlockSpec((tm, width), ctx_map), pl.BlockSpec((tm, width), lat_map)


def _on_stream(tile, tm, fn):
    n_ctx = N_CTX_TOK // tm
    pl.when(tile < n_ctx)(functools.partial(fn, 0))
    pl.when(tile >= n_ctx)(functools.partial(fn, 1))


def _ada_body(c_ref, w_ref, b_ref, o_ref):
    cv = c_ref[...]
    s = cv * jax.nn.sigmoid(cv)
    o_ref[0] = jnp.dot(s.astype(BF16), w_ref[0].astype(BF16), preferred_element_type=F32) + b_ref[0]


def _ada(cvec, ada_w, ada_b):
    tn = 2048
    n = N_MOD * D_MODEL
    return pl.pallas_call(
        _ada_body,
        out_shape=jax.ShapeDtypeStruct((DEPTH, N_SEQ_ROWS, n), F32),
        grid=(DEPTH, n // tn),
        in_specs=[
            pl.BlockSpec((N_SEQ_ROWS, D_MODEL), lambda l, j: (0, 0)),
            pl.BlockSpec((1, D_MODEL, tn), lambda l, j: (l, 0, j)),
            pl.BlockSpec((1, 1, tn), lambda l, j: (l, 0, j)),
        ],
        out_specs=pl.BlockSpec((1, N_SEQ_ROWS, tn), lambda l, j: (l, 0, j)),
        compiler_params=pltpu.CompilerParams(
            dimension_semantics=("parallel", "parallel"), vmem_limit_bytes=VMEM_LIMIT),
        name="ada_mod",
    )(cvec, ada_w, ada_b.reshape(DEPTH, 1, n))


def _norm_mod(x, nw, sc, sh):
    ms = jnp.mean(x * x, axis=-1, keepdims=True)
    y = x * lax.rsqrt(ms + EPS) * nw
    return y * (1.0 + sc) + sh


PROJ_TM = 1024
PROJ_TN = 1024


def _inproj_body(x_ref, nw_ref, sc_ref, sh_ref, w_ref, ws_ref, o_ref, os_ref, h_ref):
    @pl.when(pl.program_id(1) == 0)
    def _():
        hb = _norm_mod(x_ref[...], nw_ref[...], sc_ref[...], sh_ref[...]).astype(BF16)
        h_ref[...] = hb
        os_ref[...] = jnp.dot(hb, ws_ref[...], preferred_element_type=F32)

    o_ref[...] = jnp.dot(h_ref[...], w_ref[...], preferred_element_type=F32)


def _inproj(x, nw, mod, w_main, w_small, layer):
    tm, tn = PROJ_TM, PROJ_TN
    return pl.pallas_call(
        _inproj_body,
        out_shape=(jax.ShapeDtypeStruct((N_TOK, PROJ_MAIN), F32),
                   jax.ShapeDtypeStruct((N_TOK, PROJ_SMALL), F32)),
        grid=(N_TOK // tm, PROJ_MAIN // tn),
        in_specs=[
            pl.BlockSpec((tm, D_MODEL), lambda i, j: (i, 0)),
            pl.BlockSpec((1, D_MODEL), lambda i, j: (0, 0)),
            pl.BlockSpec((None, None, 1, D_MODEL), lambda i, j: (_seq_row(i, tm), 1, 0, 0)),
            pl.BlockSpec((None, None, 1, D_MODEL), lambda i, j: (_seq_row(i, tm), 0, 0, 0)),
            pl.BlockSpec((None, D_MODEL, tn), lambda i, j: (layer, 0, j)),
            pl.BlockSpec((None, D_MODEL, PROJ_SMALL), lambda i, j: (layer, 0, 0)),
        ],
        out_specs=(pl.BlockSpec((tm, tn), lambda i, j: (i, j)),
                   pl.BlockSpec((tm, PROJ_SMALL), lambda i, j: (i, 0))),
        scratch_shapes=[pltpu.VMEM((tm, D_MODEL), BF16)],
        compiler_params=pltpu.CompilerParams(
            dimension_semantics=("parallel", "arbitrary"), vmem_limit_bytes=VMEM_LIMIT),
        name="in_proj",
    )(x, nw, mod, mod, w_main, w_small)


def _outproj_body(*refs):
    n_mix = 4
    m_refs, (w_ref, x_ref, g_ref, o_ref) = refs[:2 * n_mix], refs[2 * n_mix:]

    def compute(stream):
        acc = None
        for p in range(n_mix):
            part = jnp.dot(m_refs[2 * p + stream][...], w_ref[p * GROUP_WIDTH:(p + 1) * GROUP_WIDTH, :],
                           preferred_element_type=F32)
            acc = part if acc is None else acc + part
        o_ref[...] = x_ref[...] + g_ref[...] * acc

    _on_stream(pl.program_id(0), PROJ_TM, compute)


def _outproj(parts, w, x, mod, layer):
    tm, tn = PROJ_TM, 1024
    in_specs = []
    for _ in parts:
        in_specs += list(_stream_specs(tm, GROUP_WIDTH))
    in_specs += [pl.BlockSpec((None, D_MODEL, tn), lambda i, j: (layer, 0, j)),
                 pl.BlockSpec((tm, tn), lambda i, j: (i, j)),
                 pl.BlockSpec((None, None, 1, tn), lambda i, j: (_seq_row(i, tm), 2, 0, j))]
    return pl.pallas_call(
        _outproj_body,
        out_shape=jax.ShapeDtypeStruct((N_TOK, D_MODEL), F32),
        grid=(N_TOK // tm, D_MODEL // tn),
        in_specs=in_specs,
        out_specs=pl.BlockSpec((tm, tn), lambda i, j: (i, j)),
        compiler_params=pltpu.CompilerParams(
            dimension_semantics=("parallel", "arbitrary"), vmem_limit_bytes=VMEM_LIMIT),
        name="out_proj",
    )(*(a for pair in parts for a in pair), w, x, mod)


FFN_TM = 512
FFN_TH = 512


def _ffn_body(x_ref, nw_ref, sc_ref, sh_ref, g_ref, w1_ref, w3_ref, w2_ref, fw_ref, *rest, final_norm):
    o_refs, (h_ref, acc_ref) = rest[:-2], rest[-2:]
    i, k = pl.program_id(0), pl.program_id(1)

    @pl.when(k == 0)
    def _():
        h_ref[...] = _norm_mod(x_ref[...], nw_ref[...], sc_ref[...], sh_ref[...]).astype(BF16)
        acc_ref[...] = jnp.zeros_like(acc_ref)

    h = h_ref[...]
    a = jnp.dot(h, w1_ref[...], preferred_element_type=F32)
    b = jnp.dot(h, w3_ref[...], preferred_element_type=F32)
    g = (a * jax.nn.sigmoid(a) * b).astype(BF16)
    acc_ref[...] += jnp.dot(g, w2_ref[...], preferred_element_type=F32)

    @pl.when(k == pl.num_programs(1) - 1)
    def _():
        y = x_ref[...] + g_ref[...] * acc_ref[...]
        if not final_norm:
            o_refs[0][...] = y
        else:
            ms = jnp.mean(y * y, axis=-1, keepdims=True)
            y = y * lax.rsqrt(ms + EPS) * fw_ref[...]

            def write(stream):
                o_refs[stream][...] = y

            _on_stream(i, FFN_TM, write)


def _ffn(x, nw, mod, w1, w3, w2, fw, layer):
    final_norm = layer == DEPTH - 1
    tm, th = FFN_TM, FFN_TH
    if final_norm:
        out_shape = (jax.ShapeDtypeStruct((N_CTX_TOK, D_MODEL), F32), jax.ShapeDtypeStruct((N_LAT_TOK, D_MODEL), F32))
        out_specs = _stream_specs(tm, D_MODEL)
    else:
        out_shape = jax.ShapeDtypeStruct((N_TOK, D_MODEL), F32)
        out_specs = pl.BlockSpec((tm, D_MODEL), lambda i, k: (i, 0))
    return pl.pallas_call(
        functools.partial(_ffn_body, final_norm=final_norm),
        out_shape=out_shape,
        grid=(N_TOK // tm, FFN_HIDDEN // th),
        in_specs=[
            pl.BlockSpec((tm, D_MODEL), lambda i, k: (i, 0)),
            pl.BlockSpec((1, D_MODEL), lambda i, k: (0, 0)),
            pl.BlockSpec((None, None, 1, D_MODEL), lambda i, k: (_seq_row(i, tm), 4, 0, 0)),
            pl.BlockSpec((None, None, 1, D_MODEL), lambda i, k: (_seq_row(i, tm), 3, 0, 0)),
            pl.BlockSpec((None, None, 1, D_MODEL), lambda i, k: (_seq_row(i, tm), 5, 0, 0)),
            pl.BlockSpec((None, D_MODEL, th), lambda i, k: (layer, 0, k)),
            pl.BlockSpec((None, D_MODEL, th), lambda i, k: (layer, 0, k)),
            pl.BlockSpec((None, th, D_MODEL), lambda i, k: (layer, k, 0)),
            pl.BlockSpec((1, D_MODEL), lambda i, k: (0, 0)),
        ],
        out_specs=out_specs,
        scratch_shapes=[pltpu.VMEM((tm, D_MODEL), BF16), pltpu.VMEM((tm, D_MODEL), F32)],
        compiler_params=pltpu.CompilerParams(
            dimension_semantics=("arbitrary", "arbitrary"), vmem_limit_bytes=VMEM_LIMIT),
        name="ffn",
    )(x, nw, mod, mod, mod, w1, w3, w2, fw)


def _axial_rope(l):
    n_rows = l // GRID_W
    t_row = jnp.repeat(jnp.arange(n_rows, dtype=F32), GRID_W)
    t_col = jnp.tile(jnp.arange(GRID_W, dtype=F32), n_rows)
    n_freq = HEAD_DIM // 4
    inv = ROPE_BASE ** (-jnp.arange(n_freq, dtype=F32) / n_freq)
    ang = jnp.concatenate([t_row[:, None] * inv, t_col[:, None] * inv], axis=-1)
    return jnp.cos(ang), jnp.sin(ang)


MIX_CHUNK = 128
DIRS = 2
NT_DIMS = (((1,), (1,)), ((), ()))
TN_DIMS = (((0,), (0,)), ((), ()))


def _dot(a, b):
    return jnp.dot(a.astype(BF16), b.astype(BF16), preferred_element_type=F32)


def _dot_nt(a, b):
    return lax.dot_general(a.astype(BF16), b.astype(BF16), NT_DIMS, preferred_element_type=F32)


def _dot_tn(a, b):
    return lax.dot_general(a.astype(BF16), b.astype(BF16), TN_DIMS, preferred_element_type=F32)


def _silu(x):
    return x * jax.nn.sigmoid(x)


def _seq_spec(seq_len, width, row_off, col_blk):
    return pl.BlockSpec((seq_len, width), lambda b: (row_off + b, col_blk))


def _mix_params(n_par):
    return pltpu.CompilerParams(dimension_semantics=("parallel",) * n_par, vmem_limit_bytes=VMEM_LIMIT)


def _ret_body(*refs, n_chunks, use_rope, has_s0):
    it = iter(refs)
    lg_ref, q_ref, k_ref, v_ref, g_ref = (next(it) for _ in range(5))
    cos_ref, sin_ref = (next(it), next(it)) if use_rope else (None, None)
    s0_ref = next(it) if has_s0 else None
    o_ref, sf_ref, acc_ref, s_ref, intra_ref, qd_ref, kd_ref, cd_ref = (next(it) for _ in range(8))
    C, H = MIX_CHUNK, N_HEADS

    row = lax.broadcasted_iota(jnp.int32, (C, C), 0)
    col = lax.broadcasted_iota(jnp.int32, (C, C), 1)
    rel = (row - col).astype(F32)
    pos = lax.broadcasted_iota(jnp.int32, (C, HEAD_DIM), 0).astype(F32)
    for d in range(DIRS):
        for h in range(H):
            lg = lg_ref[d, h]
            if d == 0:
                intra_ref[d, h] = jnp.where(rel >= 0, jnp.exp(jnp.maximum(rel, 0.0) * lg), 0.0)
                qd_ref[d, h] = jnp.exp((pos + 1.0) * lg)
                kd_ref[d, h] = jnp.exp((C - 1.0 - pos) * lg)
            else:
                intra_ref[d, h] = jnp.where(rel <= 0, jnp.exp(jnp.maximum(-rel, 0.0) * lg), 0.0)
                qd_ref[d, h] = jnp.exp((C - pos) * lg)
                kd_ref[d, h] = jnp.exp(pos * lg)
            cd_ref[d, h] = jnp.exp(jnp.full((C, HEAD_DIM), C, F32) * lg)
    if has_s0:
        s_ref[...] = s0_ref[...]
    else:
        s_ref[...] = jnp.zeros_like(s_ref)
    acc_ref[...] = jnp.zeros_like(acc_ref)

    units = [(d, h) for d in range(DIRS) for h in range(H)]

    def chunk_step(n, carry):
        rows_d = [pl.ds(pl.multiple_of(c * C, C), C) for c in (n, n_chunks - 1 - n)]
        q_l, k_l, v_l, sc_l = [], [], [], []
        for d, h in units:
            rows, cols = rows_d[d], slice(h * HEAD_DIM, (h + 1) * HEAD_DIM)
            q, k = q_ref[rows, cols], k_ref[rows, cols]
            if use_rope:
                cs, sn = cos_ref[rows, :], sin_ref[rows, :]
                q = q * cs + pltpu.roll(q, HEAD_DIM // 2, 1) * sn
                k = k * cs + pltpu.roll(k, HEAD_DIM // 2, 1) * sn
            k = k * HEAD_DIM ** -0.5
            q_l.append(q)
            k_l.append(k)
            v_l.append(v_ref[rows, cols].astype(BF16))
            sc_l.append(_dot_nt(q, k) * intra_ref[d, h])
        for i, (d, h) in enumerate(units):
            cols = slice(h * HEAD_DIM, (h + 1) * HEAD_DIM)
            s = s_ref[d, h]
            acc_ref[rows_d[d], cols] += _dot(jnp.concatenate([sc_l[i], q_l[i] * qd_ref[d, h]], axis=1),
                                             jnp.concatenate([v_l[i], s.astype(BF16)], axis=0))
            s_ref[d, h] = cd_ref[d, h] * s + _dot_tn(k_l[i] * kd_ref[d, h], v_l[i])
        return carry

    lax.fori_loop(0, n_chunks, chunk_step, 0)
    sf_ref[...] = s_ref[...]

    def finish(n, carry):
        rows = pl.ds(pl.multiple_of(n * C, C), C)
        for h in range(H):
            cols = slice(h * HEAD_DIM, (h + 1) * HEAD_DIM)
            o = acc_ref[rows, cols]
            mu = jnp.mean(o, axis=-1, keepdims=True)
            oc = o - mu
            y = oc * lax.rsqrt(jnp.mean(oc * oc, axis=-1, keepdims=True) + EPS)
            o_ref[rows, cols] = (y * _silu(g_ref[rows, cols])).astype(BF16)
        return carry

    lax.fori_loop(0, n_chunks, finish, 0)


def _retention_pallas(pm, log_gamma, n_seq, seq_len, row_off, rope2, s0, layer):
    use_rope, has_s0 = rope2 is not None, s0 is not None
    st_shape = (DIRS, N_HEADS, HEAD_DIM, HEAD_DIM)
    in_specs = [pl.BlockSpec(memory_space=pltpu.SMEM)]
    in_specs += [_seq_spec(seq_len, GROUP_WIDTH, row_off, cb) for cb in range(4)]
    args = [log_gamma, pm, pm, pm, pm]
    if use_rope:
        in_specs += [pl.BlockSpec((seq_len, HEAD_DIM), lambda b: (0, 0))] * 2
        args += list(rope2)
    if has_s0:
        in_specs.append(pl.BlockSpec((None, None) + st_shape, lambda b: (b, layer, 0, 0, 0, 0)))
        args.append(s0)
    return pl.pallas_call(
        functools.partial(_ret_body, n_chunks=seq_len // MIX_CHUNK, use_rope=use_rope, has_s0=has_s0),
        out_shape=(jax.ShapeDtypeStruct((n_seq * seq_len, GROUP_WIDTH), BF16),
                   jax.ShapeDtypeStruct((n_seq,) + st_shape, F32)),
        grid=(n_seq,),
        in_specs=in_specs,
        out_specs=(pl.BlockSpec((seq_len, GROUP_WIDTH), lambda b: (b, 0)),
                   pl.BlockSpec((None,) + st_shape, lambda b: (b, 0, 0, 0, 0))),
        scratch_shapes=[pltpu.VMEM((seq_len, GROUP_WIDTH), F32), pltpu.VMEM(st_shape, F32),
                        pltpu.VMEM((DIRS, N_HEADS, MIX_CHUNK, MIX_CHUNK), F32),
                        pltpu.VMEM((DIRS, N_HEADS, MIX_CHUNK, HEAD_DIM), F32),
                        pltpu.VMEM((DIRS, N_HEADS, MIX_CHUNK, HEAD_DIM), F32),
                        pltpu.VMEM((DIRS, N_HEADS, MIX_CHUNK, HEAD_DIM), F32)],
        compiler_params=_mix_params(1),
        name="retention",
    )(*args)


def _rope_tables(l):
    cos, sin = _axial_rope(l)
    return jnp.concatenate([cos, cos], axis=-1), jnp.concatenate([-sin, sin], axis=-1)


S5_HALF_G = S5_GROUPS // 2
S5_HALF_U = S5_HALF_G * S5_CH
S5_HALF_X = S5_HALF_G * S5_N
S5_X = S5_GROUPS * S5_N
S5_TC = 512
S5_BLK = 8
S5_TABS = 4


def _s5_tables(a_re, a_im, log_step, b_re, b_im, c_re, c_im):
    dt = jnp.exp(log_step)[..., None]
    mag = jnp.exp(a_re * dt)
    ab_re = mag * jnp.cos(a_im * dt)
    ab_im = mag * jnp.sin(a_im * dt)
    den = a_re * a_re + a_im * a_im
    nr = ab_re - 1.0
    f_re = (nr * a_re + ab_im * a_im) / den
    f_im = (ab_im * a_re - nr * a_im) / den
    bb_re = f_re[..., None] * b_re - f_im[..., None] * b_im
    bb_im = f_re[..., None] * b_im + f_im[..., None] * b_re
    eye = jnp.eye(S5_HALF_G, dtype=F32)

    def in_mat(bb):
        bb = bb.reshape(DIRS, 2, S5_HALF_G, S5_N, S5_CH)
        return jnp.einsum('dhgnc,gk->dhgckn', bb, eye).reshape(DIRS, 2, S5_HALF_U, S5_HALF_X)

    def out_mat(cc):
        cc = cc.reshape(DIRS, 2, S5_HALF_G, S5_CH, S5_N)
        return jnp.einsum('dhgcn,gk->dhgnkc', cc, eye).reshape(DIRS, 2, S5_HALF_X, S5_HALF_U)

    bm = jnp.concatenate([in_mat(bb_re), in_mat(bb_im)], axis=-1).astype(BF16)
    cm = jnp.concatenate([out_mat(c_re), -out_mat(c_im)], axis=-2).astype(BF16)
    t = jnp.arange(S5_BLK, dtype=F32)
    order = jnp.stack([t, S5_BLK - 1.0 - t])
    shifts = 2.0 ** jnp.arange(S5_TABS - 1, dtype=F32)
    expo = jnp.concatenate([jnp.where(order[:, None, :] >= shifts[None, :, None], shifts[None, :, None], jnp.nan),
                            order[:, None, :] + 1.0], axis=1)
    live = ~jnp.isnan(expo)
    e = jnp.where(live, expo, 0.0)[..., None]
    adt_re = (a_re * dt).reshape(DIRS, 1, 1, S5_X)
    adt_im = (a_im * dt).reshape(DIRS, 1, 1, S5_X)
    pmag = jnp.where(live[..., None], jnp.exp(e * adt_re), 0.0)
    pw_re = pmag * jnp.cos(e * adt_im)
    pw_im = pmag * jnp.sin(e * adt_im)
    return bm, cm, pw_re, pw_im


def _gelu_tanh(x):
    return 0.5 * x * (1.0 + jnp.tanh(math.sqrt(2.0 / math.pi) * (x + 0.044715 * (x * x * x))))


def _s5_body(*refs, seq_len, has_s0):
    it = iter(refs)
    u_ref, bm_ref, cm_ref, pwr_ref, pwi_ref, d_ref, gw_ref, gb_ref = (next(it) for _ in range(8))
    x0r_ref, x0i_ref = (next(it), next(it)) if has_s0 else (None, None)
    o_ref, sfr_ref, sfi_ref, y_ref, xr_ref, xi_ref, xb_ref = (next(it) for _ in range(7))
    tc = min(seq_len, S5_TC)
    n_tiles = seq_len // tc
    n_pair = tc // (2 * S5_BLK)

    y_ref[...] = d_ref[...] * u_ref[...]
    for d in range(DIRS):
        last = S5_BLK - 1 if d == 0 else 0
        for hf in range(2):
            xs = slice(hf * S5_HALF_X, (hf + 1) * S5_HALF_X)
            us = slice(hf * S5_HALF_U, (hf + 1) * S5_HALF_U)

            def scan_block(x_in, carry, d=d, xs=xs, last=last):
                xr, xi = x_in
                car_re, car_im = carry
                for k in range(S5_TABS - 1):
                    s = 1 << k
                    shift = s if d == 0 else S5_BLK - s
                    p_re, p_im = pwr_ref[d, k, :, xs], pwi_ref[d, k, :, xs]
                    sr, si = pltpu.roll(xr, shift, 0), pltpu.roll(xi, shift, 0)
                    xr, xi = xr + p_re * sr - p_im * si, xi + p_re * si + p_im * sr
                p_re, p_im = pwr_ref[d, S5_TABS - 1, :, xs], pwi_ref[d, S5_TABS - 1, :, xs]
                xr, xi = xr + p_re * car_re - p_im * car_im, xi + p_re * car_im + p_im * car_re
                return (xr, xi), (xr[last:last + 1, :], xi[last:last + 1, :])

            def scan_pair(j, carry, d=d, scan_block=scan_block):
                pair = j if d == 0 else n_pair - 1 - j
                rows = pl.ds(pl.multiple_of(pair * 2 * S5_BLK, 2 * S5_BLK), 2 * S5_BLK)
                xr2, xi2 = xr_ref[rows, :], xi_ref[rows, :]
                halves = [(xr2[:S5_BLK], xi2[:S5_BLK]), (xr2[S5_BLK:], xi2[S5_BLK:])]
                order = (0, 1) if d == 0 else (1, 0)
                out = [None, None]
                for idx in order:
                    out[idx], carry = scan_block(halves[idx], carry)
                xb_ref[rows, :S5_HALF_X] = jnp.concatenate([out[0][0], out[1][0]], axis=0).astype(BF16)
                xb_ref[rows, S5_HALF_X:] = jnp.concatenate([out[0][1], out[1][1]], axis=0).astype(BF16)
                return carry

            def scan_tile(i, carry, d=d, hf=hf, us=us, scan_pair=scan_pair):
                tile = i if d == 0 else n_tiles - 1 - i
                rows = pl.ds(pl.multiple_of(tile * tc, tc), tc)
                bu = _dot(u_ref[rows, us], bm_ref[d, hf])
                xr_ref[...] = bu[:, :S5_HALF_X]
                xi_ref[...] = bu[:, S5_HALF_X:]
                carry = lax.fori_loop(0, n_pair, scan_pair, carry)
                y_ref[rows, us] += jnp.dot(xb_ref[...], cm_ref[d, hf], preferred_element_type=F32)
                return carry

            if has_s0:
                carry0 = (x0r_ref[d, :, xs], x0i_ref[d, :, xs])
            else:
                carry0 = (jnp.zeros((1, S5_HALF_X), F32), jnp.zeros((1, S5_HALF_X), F32))
            car_re, car_im = lax.fori_loop(0, n_tiles, scan_tile, carry0)
            sfr_ref[d, :, xs] = car_re
            sfi_ref[d, :, xs] = car_im

    z = _gelu_tanh(y_ref[...])
    o_ref[...] = (z * jax.nn.sigmoid(_dot(z, gw_ref[...]) + gb_ref[...])).astype(BF16)


def _s5_pallas(pm, tables, s5_d, glu_w, glu_b, n_seq, seq_len, row_off, x0, layer):
    bm, cm, pw_re, pw_im = tables
    has_s0 = x0 is not None
    full = lambda shape: pl.BlockSpec(shape, lambda b: (0,) * len(shape))
    of_layer = lambda t: pl.BlockSpec((None,) + t.shape[1:], lambda b: (layer,) + (0,) * (t.ndim - 1))
    in_specs = [_seq_spec(seq_len, GROUP_WIDTH, row_off, 13),
                of_layer(bm), of_layer(cm), of_layer(pw_re), of_layer(pw_im),
                full((1, GROUP_WIDTH)), full((GROUP_WIDTH, GROUP_WIDTH)), full((1, GROUP_WIDTH))]
    args = [pm, bm, cm, pw_re, pw_im, s5_d.reshape(1, GROUP_WIDTH), glu_w.astype(BF16),
            glu_b.reshape(1, GROUP_WIDTH)]
    if has_s0:
        in_specs += [pl.BlockSpec((None, None, DIRS, 1, S5_X), lambda b: (b, layer, 0, 0, 0))] * 2
        args += list(x0)
    st = jax.ShapeDtypeStruct((n_seq, DIRS, 1, S5_X), F32)
    st_spec = pl.BlockSpec((None, DIRS, 1, S5_X), lambda b: (b, 0, 0, 0))
    tc = min(seq_len, S5_TC)
    return pl.pallas_call(
        functools.partial(_s5_body, seq_len=seq_len, has_s0=has_s0),
        out_shape=(jax.ShapeDtypeStruct((n_seq * seq_len, GROUP_WIDTH), BF16), st, st),
        grid=(n_seq,),
        in_specs=in_specs,
        out_specs=(pl.BlockSpec((seq_len, GROUP_WIDTH), lambda b: (b, 0)), st_spec, st_spec),
        scratch_shapes=[pltpu.VMEM((seq_len, GROUP_WIDTH), F32),
                        pltpu.VMEM((tc, S5_HALF_X), F32), pltpu.VMEM((tc, S5_HALF_X), F32),
                        pltpu.VMEM((tc, 2 * S5_HALF_X), BF16)],
        compiler_params=_mix_params(1),
        name="s5",
    )(*args)


GLA_LEVELS = 7


def _chunk_cumsum_rows(x, rowi):
    s = 1
    while s < MIX_CHUNK:
        x = x + jnp.where(rowi >= s, pltpu.roll(x, s, 0), 0.0)
        s *= 2
    return x


def _gla_body(*refs, n_chunks, has_s0):
    it = iter(refs)
    q_ref, f0_ref, f1_ref, i_ref, g_ref, lb_ref, nw_ref = (next(it) for _ in range(7))
    s0_ref = next(it) if has_s0 else None
    o_ref, sf_ref, acc_ref, s_ref, code_ref = (next(it) for _ in range(5))
    C, H = MIX_CHUNK, N_HEADS
    f_refs = (f0_ref, f1_ref)

    rowi = lax.broadcasted_iota(jnp.int32, (C, HEAD_DIM), 0)
    ri = lax.broadcasted_iota(jnp.int32, (C, C), 0)
    ci = lax.broadcasted_iota(jnp.int32, (C, C), 1)
    top_bit = 31 - lax.clz(ri ^ ci)
    code_ref[...] = jnp.where(ri > ci, top_bit, jnp.where(ri < ci, -1 - top_bit, GLA_LEVELS))
    if has_s0:
        s_ref[...] = s0_ref[...]
    else:
        s_ref[...] = jnp.zeros_like(s_ref)
    acc_ref[...] = jnp.zeros_like(acc_ref)

    def chunk_step(n, carry):
        for d in range(DIRS):
            c = n if d == 0 else n_chunks - 1 - n
            rows = pl.ds(pl.multiple_of(c * C, C), C)
            for h in range(H):
                cols = slice(h * HEAD_DIM, (h + 1) * HEAD_DIM)
                code = code_ref[...]
                q = _silu(q_ref[rows, cols]) * HEAD_DIM ** -0.5
                v = i_ref[rows, cols].astype(BF16)
                fx = f_refs[d][rows, cols]
                lb = lb_ref[d:d + 1, cols]
                sig = 1.0 / (1.0 + jnp.exp(-fx))
                logf = jnp.log2(jnp.maximum(lb, LB_FLOOR) + (1.0 - lb) * sig)
                k = (1.0 - lb) * (1.0 - sig)
                cum = _chunk_cumsum_rows(logf, rowi)
                own = cum
                attn = jnp.where(code == GLA_LEVELS, _dot_nt(q, k), 0.0)
                for lvl in range(GLA_LEVELS):
                    m = 1 << lvl
                    prev = pltpu.roll(own, m, 0)
                    pre = jnp.minimum(cum - prev, 0.0)
                    suf = own - cum
                    if d == 0:
                        sc = _dot_nt(q * jnp.exp2(pre), k * jnp.exp2(suf))
                        hit = code == lvl
                    else:
                        sc = _dot_nt(q * jnp.exp2(suf + logf), k * jnp.exp2(jnp.minimum(pre - logf, 0.0)))
                        hit = code == -1 - lvl
                    attn = jnp.where(hit, sc, attn)
                    own = jnp.where(((rowi >> lvl) & 1) == 0, pltpu.roll(own, C - m, 0), own)
                tot = own
                st = s_ref[d, h]
                if d == 0:
                    q_in, k_out = q * jnp.exp2(cum), k * jnp.exp2(tot - cum)
                else:
                    q_in, k_out = q * jnp.exp2(tot - cum + logf), k * jnp.exp2(cum - logf)
                acc_ref[rows, cols] += _dot(attn, v) + _dot_nt(q_in, st)
                s_ref[d, h] = jnp.exp2(tot[0:1, :]) * st + _dot_tn(v, k_out)
        return carry

    lax.fori_loop(0, n_chunks, chunk_step, 0)
    sf_ref[...] = s_ref[...]

    def finish(n, carry):
        rows = pl.ds(pl.multiple_of(n * C, C), C)
        for h in range(H):
            cols = slice(h * HEAD_DIM, (h + 1) * HEAD_DIM)
            o = acc_ref[rows, cols]
            y = o * lax.rsqrt(jnp.mean(o * o, axis=-1, keepdims=True) + EPS) * nw_ref[...]
            o_ref[rows, cols] = (y * _silu(g_ref[rows, cols])).astype(BF16)
        return carry

    lax.fori_loop(0, n_chunks, finish, 0)


def _gla_pallas(pm, lower_bound, norm_w, n_seq, seq_len, row_off, s0_t, layer):
    has_s0 = s0_t is not None
    st_shape = (DIRS, N_HEADS, HEAD_DIM, HEAD_DIM)
    in_specs = [_seq_spec(seq_len, GROUP_WIDTH, row_off, cb) for cb in (8, 9, 10, 11, 12)]
    in_specs += [pl.BlockSpec((DIRS, GROUP_WIDTH), lambda b: (0, 0)), pl.BlockSpec((1, HEAD_DIM), lambda b: (0, 0))]
    args = [pm] * 5 + [lower_bound, norm_w.reshape(1, HEAD_DIM)]
    if has_s0:
        in_specs.append(pl.BlockSpec((None, None) + st_shape, lambda b: (b, layer, 0, 0, 0, 0)))
        args.append(s0_t)
    return pl.pallas_call(
        functools.partial(_gla_body, n_chunks=seq_len // MIX_CHUNK, has_s0=has_s0),
        out_shape=(jax.ShapeDtypeStruct((n_seq * seq_len, GROUP_WIDTH), BF16),
                   jax.ShapeDtypeStruct((n_seq,) + st_shape, F32)),
        grid=(n_seq,),
        in_specs=in_specs,
        out_specs=(pl.BlockSpec((seq_len, GROUP_WIDTH), lambda b: (b, 0)),
                   pl.BlockSpec((None,) + st_shape, lambda b: (b, 0, 0, 0, 0))),
        scratch_shapes=[pltpu.VMEM((seq_len, GROUP_WIDTH), F32), pltpu.VMEM(st_shape, F32),
                        pltpu.VMEM((MIX_CHUNK, MIX_CHUNK), jnp.int32)],
        compiler_params=_mix_params(1),
        name="hgrn2",
    )(*args)


GDN_BLOCK_BITS = 4
GDN_MERGES = 3
N_GATES = DIRS * N_HEADS


def _softplus(x):
    return jnp.maximum(x, 0.0) + jnp.log1p(jnp.exp(-jnp.abs(x)))


def _gdn_body(*refs, seq_len, n_sub, has_s0):
    it = iter(refs)
    (q_ref, k_ref, v_ref, g_ref, ps_ref, pst_ref, cw_ref, prow_ref, pcol_ref, nw_ref) = (next(it) for _ in range(10))
    s0_ref = next(it) if has_s0 else None
    (o_ref, sf_ref, acc_ref, s_ref, qn_ref, kn_ref, vn_ref, gate_ref, gatet_ref, blev_ref) = (
        next(it) for _ in range(10))
    C, H, L = MIX_CHUNK, N_HEADS, seq_len
    R = n_sub * L
    n_chunks = L // C
    w = GROUP_WIDTH

    pos = lax.broadcasted_iota(jnp.int32, (R, HEAD_DIM), 0) & (L - 1)
    for part, (src, dst) in enumerate(((q_ref, qn_ref), (k_ref, kn_ref), (v_ref, vn_ref))):
        for h in range(H):
            cols = slice(h * HEAD_DIM, (h + 1) * HEAD_DIM)
            wc = slice(part * w + h * HEAD_DIM, part * w + (h + 1) * HEAD_DIM)
            x = src[:, cols]
            x_prev = jnp.where(pos >= 1, pltpu.roll(x, 1, 0), 0.0)
            x_next = jnp.where(pos < L - 1, pltpu.roll(x, R - 1, 0), 0.0)
            y = _silu(cw_ref[0:1, wc] * x_prev + cw_ref[1:2, wc] * x + cw_ref[2:3, wc] * x_next)
            if part < 2:
                y = y * lax.rsqrt(jnp.sum(y * y, axis=-1, keepdims=True) + EPS)
            if part == 0:
                y = y * HEAD_DIM ** -0.5
            dst[:, cols] = y

    rowi = lax.broadcasted_iota(jnp.int32, (C, HEAD_DIM), 0)
    lane = lax.broadcasted_iota(jnp.int32, (C, HEAD_DIM), 1)
    lane_t = lax.broadcasted_iota(jnp.int32, (2 * N_GATES, C), 1)
    row_t = lax.broadcasted_iota(jnp.int32, (2 * N_GATES, C), 0)
    for c in range(R // C):
        rows = slice(c * C, (c + 1) * C)
        a = ps_ref[rows, :]
        la = -jnp.exp(prow_ref[0:1, :]) * _softplus(a + prow_ref[1:2, :])
        pre = _chunk_cumsum_rows(la, rowi)
        tot = jnp.broadcast_to(pre[C - 1:C, :], (C, HEAD_DIM))
        g = jnp.where(lane < N_HEADS, pre, tot - pre + la)
        gate_ref[rows, :] = jnp.where(lane < N_GATES, g,
                                      jnp.where(lane < 2 * N_GATES, jax.nn.sigmoid(a),
                                                pltpu.roll(tot, 2 * N_GATES, 1)))
        at = pst_ref[:, rows]
        lat = -jnp.exp(pcol_ref[:, 0:1]) * _softplus(at + pcol_ref[:, 1:2])
        pre_t, s = lat, 1
        while s < C:
            pre_t = pre_t + jnp.where(lane_t >= s, pltpu.roll(pre_t, s, 1), 0.0)
            s *= 2
        tot_t = jnp.broadcast_to(pre_t[:, C - 1:C], (2 * N_GATES, C))
        g_t = jnp.where(row_t < N_HEADS, pre_t, tot_t - pre_t + lat)
        gatet_ref[:, rows] = jnp.where(row_t < N_GATES, g_t, jax.nn.sigmoid(at))

    ri = lax.broadcasted_iota(jnp.int32, (C, C), 0)
    ci = lax.broadcasted_iota(jnp.int32, (C, C), 1)
    bx = (ri >> GDN_BLOCK_BITS) ^ (ci >> GDN_BLOCK_BITS)
    blev_ref[...] = jnp.where(bx == 0, 0, 32 - lax.clz(bx))
    if has_s0:
        for s in range(n_sub):
            s_ref[s * DIRS:(s + 1) * DIRS] = s0_ref[s]
    else:
        s_ref[...] = jnp.zeros_like(s_ref)
    acc_ref[...] = jnp.zeros_like(acc_ref)

    units = [(s, d, h) for s in range(n_sub) for d in range(DIRS) for h in range(H)]

    def chunk_step(n, carry):
        eye = jnp.where(ri == ci, 1.0, 0.0)
        blev = blev_ref[...]
        rows_sd = [[pl.ds(pl.multiple_of(s * L + c * C, C), C) for c in (n, n_chunks - 1 - n)] for s in range(n_sub)]
        incl_d = [ri >= ci, ri <= ci]
        strict_d = [ri > ci, ri < ci]
        q_l, et_l, ek_l, rhs_l, a_l, attn_l = ([] for _ in range(6))
        for s, d, h in units:
            rows, cols, cg = rows_sd[s][d], slice(h * HEAD_DIM, (h + 1) * HEAD_DIM), d * N_HEADS + h
            q, k, v = qn_ref[rows, cols], kn_ref[rows, cols], vn_ref[rows, cols]
            g_i = jnp.broadcast_to(gate_ref[rows, cg:cg + 1], (C, C))
            b_i = jnp.broadcast_to(gate_ref[rows, N_GATES + cg:N_GATES + cg + 1], (C, C))
            tot = jnp.broadcast_to(gate_ref[rows, 2 * N_GATES + cg:2 * N_GATES + cg + 1], (C, C))
            g_j = gatet_ref[cg:cg + 1, rows]
            decay = jnp.where(incl_d[d], jnp.exp(jnp.minimum(g_i - g_j, 0.0)), 0.0)
            e_g = jnp.exp(g_i)
            kb = k.astype(BF16)
            a_l.append(jnp.where(strict_d[d], _dot_nt(kb, kb) * b_i * decay, 0.0))
            attn_l.append(_dot_nt(q, kb) * decay)
            rhs_l.append(jnp.concatenate([v * b_i, k * (b_i * e_g)], axis=1).astype(BF16))
            q_l.append(q * e_g)
            ek_l.append(k * jnp.exp(tot - g_i))
            et_l.append(jnp.exp(tot))
        b_l = [-jnp.where(blev == 0, a, 0.0) for a in a_l]
        p_l = [eye + b for b in b_l]
        for _ in range(GDN_BLOCK_BITS - 1):
            b_l = [_dot(b, b) for b in b_l]
            p_l = [p + _dot(p, b) for p, b in zip(p_l, b_l)]
        for lvl in range(1, GDN_MERGES + 1):
            ep_l = [_dot(jnp.where(blev == lvl, a, 0.0), p) for a, p in zip(a_l, p_l)]
            p_l = [p - _dot(p, ep) for p, ep in zip(p_l, ep_l)]
        uw_l = [_dot(p, rhs) for p, rhs in zip(p_l, rhs_l)]
        s_l = [s_ref[s * DIRS + d, h] for s, d, h in units]
        vn_l = [uw[:, :HEAD_DIM] - _dot(uw[:, HEAD_DIM:], st) for uw, st in zip(uw_l, s_l)]
        for i, (s, d, h) in enumerate(units):
            cols = slice(h * HEAD_DIM, (h + 1) * HEAD_DIM)
            acc_ref[rows_sd[s][d], cols] += _dot(jnp.concatenate([q_l[i], attn_l[i]], axis=1),
                                                 jnp.concatenate([s_l[i], vn_l[i]], axis=0))
            s_ref[s * DIRS + d, h] = s_l[i] * et_l[i] + _dot_tn(ek_l[i], vn_l[i])
        return carry

    lax.fori_loop(0, n_chunks, chunk_step, 0)
    sf_ref[...] = s_ref[...]

    def finish(n, carry):
        rows = pl.ds(pl.multiple_of(n * C, C), C)
        for h in range(H):
            cols = slice(h * HEAD_DIM, (h + 1) * HEAD_DIM)
            o = acc_ref[rows, cols]
            y = o * lax.rsqrt(jnp.mean(o * o, axis=-1, keepdims=True) + EPS) * nw_ref[...]
            o_ref[rows, cols] = (y * _silu(g_ref[rows, cols])).astype(BF16)
        return carry

    lax.fori_loop(0, R // C, finish, 0)


GDN_CTX_SUB = 4
GDN_LAT_SUB = 2
GDN_DOUBLE_BUFFER_BYTES = 2 * 1024 * 1024


def _gdn_pallas(pm, ps, ps_t, conv_w, a_log, dt_bias, norm_w, n_seq, seq_len, row_off, s0, layer):
    has_s0 = s0 is not None
    n_sub = GDN_LAT_SUB if has_s0 else GDN_CTX_SUB
    n_seq, seq_len, row_off = n_seq // n_sub, seq_len * n_sub, row_off // n_sub
    st_shape = (n_sub * DIRS, N_HEADS, HEAD_DIM, HEAD_DIM)
    par = jnp.stack([a_log.reshape(N_GATES), dt_bias.reshape(N_GATES)])
    par_row = jnp.pad(par, ((0, 0), (0, HEAD_DIM - N_GATES)))
    par_col = jnp.pad(par.T, ((0, N_GATES), (0, 0)))
    full = lambda shape: pl.BlockSpec(shape, lambda b: (0,) * len(shape))
    big_mode = pl.Buffered(1) if seq_len * GROUP_WIDTH * 4 > GDN_DOUBLE_BUFFER_BYTES else None
    in_specs = [pl.BlockSpec((seq_len, GROUP_WIDTH), functools.partial(lambda cb, b: (row_off + b, cb), cb),
                             pipeline_mode=big_mode) for cb in (4, 5, 6, 7)]
    in_specs += [_seq_spec(seq_len, PROJ_SMALL, row_off, 0),
                 pl.BlockSpec((2 * N_GATES, seq_len), lambda b: (0, row_off + b)),
                 full((GDN_CONV, 3 * GROUP_WIDTH)), full((2, HEAD_DIM)), full((2 * N_GATES, 2)), full((1, HEAD_DIM))]
    args = [pm] * 4 + [ps, ps_t, conv_w, par_row, par_col, norm_w.reshape(1, HEAD_DIM)]
    if has_s0:
        in_specs.append(pl.BlockSpec((n_sub, None, DIRS, N_HEADS, HEAD_DIM, HEAD_DIM),
                                     lambda b: (b, layer, 0, 0, 0, 0)))
        args.append(s0)
    seq_f32 = lambda width: pltpu.VMEM((seq_len, width), F32)
    out, s_fin = pl.pallas_call(
        functools.partial(_gdn_body, seq_len=seq_len // n_sub, n_sub=n_sub, has_s0=has_s0),
        out_shape=(jax.ShapeDtypeStruct((n_seq * seq_len, GROUP_WIDTH), BF16),
                   jax.ShapeDtypeStruct((n_seq,) + st_shape, F32)),
        grid=(n_seq,),
        in_specs=in_specs,
        out_specs=(pl.BlockSpec((seq_len, GROUP_WIDTH), lambda b: (b, 0)),
                   pl.BlockSpec((None,) + st_shape, lambda b: (b, 0, 0, 0, 0))),
        scratch_shapes=[seq_f32(GROUP_WIDTH), pltpu.VMEM(st_shape, F32),
                        seq_f32(GROUP_WIDTH), seq_f32(GROUP_WIDTH), seq_f32(GROUP_WIDTH),
                        seq_f32(HEAD_DIM), pltpu.VMEM((2 * N_GATES, seq_len), F32),
                        pltpu.VMEM((MIX_CHUNK, MIX_CHUNK), jnp.int32)],
        compiler_params=_mix_params(1),
        name="gated_delta",
    )(*args)
    return out, s_fin.reshape(n_seq * n_sub, DIRS, N_HEADS, HEAD_DIM, HEAD_DIM)


def kernel(x_prompt, x_sample, state_ret, state_gdn, state_hgrn, state_s5_re, state_s5_im, c, c_ctx, norm1_w, norm2_w, final_norm_w, ada_w, ada_b, in_proj, out_proj, ret_decay_logit, gdn_conv, gdn_a_log, gdn_dt_bias, gdn_norm_w, hg_lb_param, hg_norm_w, s5_a_re, s5_a_im, s5_b_re, s5_b_im, s5_c_re, s5_c_im, s5_log_step, s5_d, s5_glu_w, s5_glu_b, ffn_w1, ffn_w3, ffn_w2):
    lb_soft = jax.nn.softmax(hg_lb_param, axis=0)
    lower_bounds = jnp.cumsum(lb_soft, axis=0) - lb_soft[0]
    rope2 = _rope_tables(DEC_SEQ)

    cvec = jnp.zeros((N_SEQ_ROWS, D_MODEL), F32).at[0].set(c_ctx).at[1:1 + DEC_BATCH].set(c)
    mod_all = _ada(cvec, ada_w, ada_b).reshape(DEPTH, N_SEQ_ROWS, N_MOD, 1, D_MODEL)

    gate0 = 8 * GROUP_WIDTH
    w_in = in_proj.astype(BF16)
    w_main = jnp.concatenate([w_in[:, :, :gate0], w_in[:, :, gate0 + 2 * N_GATES:]], axis=-1)
    w_small = jnp.pad(w_in[:, :, gate0:gate0 + 2 * N_GATES], ((0, 0), (0, 0), (0, PROJ_SMALL - 2 * N_GATES)))
    w_out = out_proj.astype(BF16)
    w1, w3, w2 = ffn_w1.astype(BF16), ffn_w3.astype(BF16), ffn_w2.astype(BF16)

    x = jnp.concatenate([x_prompt.reshape(N_CTX_TOK, D_MODEL), x_sample.reshape(N_LAT_TOK, D_MODEL)], axis=0)
    hgrn_t = jnp.swapaxes(state_hgrn, -1, -2)
    s5_x0 = (state_s5_re.reshape(DEC_BATCH, DEPTH, DIRS, 1, S5_X), state_s5_im.reshape(DEC_BATCH, DEPTH, DIRS, 1, S5_X))
    lat_off = N_CTX_TOK // DEC_SEQ
    tables = jax.vmap(_s5_tables)(s5_a_re, s5_a_im, s5_log_step, s5_b_re, s5_b_im, s5_c_re, s5_c_im)
    ctx_states = []
    for i in range(DEPTH):
        mod = mod_all[i]
        pm, ps = _inproj(x, norm1_w[i][None], mod, w_main, w_small, i)
        ps_t = ps[:, :2 * N_GATES].T
        lg = jax.nn.log_sigmoid(ret_decay_logit[i])
        ctx = (BATCH, SEQ, 0)
        lat = (DEC_BATCH, DEC_SEQ, lat_off)

        ret_c, rs = _retention_pallas(pm, lg, *ctx, None, None, i)
        ret_l, _ = _retention_pallas(pm, lg, *lat, rope2, state_ret, i)
        gdn_args = (pm, ps, ps_t, gdn_conv[i], gdn_a_log[i], gdn_dt_bias[i], gdn_norm_w[i])
        gdn_c, gs = _gdn_pallas(*gdn_args, *ctx, None, i)
        gdn_l, _ = _gdn_pallas(*gdn_args, *lat, state_gdn, i)
        hg_c, hs = _gla_pallas(pm, lower_bounds[i], hg_norm_w[i], *ctx, None, i)
        hg_l, _ = _gla_pallas(pm, lower_bounds[i], hg_norm_w[i], *lat, hgrn_t, i)
        s5_args = (pm, tables, s5_d[i], s5_glu_w[i], s5_glu_b[i])
        s5_c, xr, xi = _s5_pallas(*s5_args, *ctx, None, i)
        s5_l, _, _ = _s5_pallas(*s5_args, *lat, s5_x0, i)
        ctx_states.append((rs, gs, jnp.swapaxes(hs, -1, -2), xr.reshape(BATCH, DIRS, S5_GROUPS, S5_N),
                           xi.reshape(BATCH, DIRS, S5_GROUPS, S5_N)))

        parts = ((ret_c, ret_l), (gdn_c, gdn_l), (hg_c, hg_l), (s5_c, s5_l))
        x = _outproj(parts, w_out, x, mod, i)
        x = _ffn(x, norm2_w[i][None], mod, w1, w3, w2, final_norm_w[None], i)

    y_prompt = x[0].reshape(BATCH, SEQ, D_MODEL)
    y_sample = x[1].reshape(DEC_BATCH, DEC_SEQ, D_MODEL)
    new_states = tuple(jnp.stack([s[j] for s in ctx_states], axis=1) for j in range(5))
    return (y_prompt, y_sample) + new_states
```

```python
import functools
import math

import jax
import jax.numpy as jnp
from jax import lax
from jax.experimental import pallas as pl
from jax.experimental.pallas import tpu as pltpu

F32 = jnp.float32
BF16 = jnp.bfloat16

D_MODEL = 2048
BATCH = 16
SEQ = 256
DEPTH = 2
DEC_BATCH = 8
DEC_SEQ = 1024
GRID_W = 64
HEAD_DIM = 128
GROUP_WIDTH = 512
N_HEADS = 4
S5_CH = 16
S5_GROUPS = 32
S5_N = 64
GDN_CONV = 3
ROPE_BASE = 10000.0
FFN_HIDDEN = 5632
N_MOD = 6
EPS = 1e-6
LB_FLOOR = 1e-30

N_CTX_TOK = BATCH * SEQ
N_LAT_TOK = DEC_BATCH * DEC_SEQ
N_TOK = N_CTX_TOK + N_LAT_TOK
N_SEQ_ROWS = 16
PROJ_MAIN = 14 * GROUP_WIDTH
PROJ_SMALL = 128
V7X_VMEM_BYTES = 64 * 1024 * 1024
VMEM_LIMIT = V7X_VMEM_BYTES - 8 * 1024 * 1024
ADA_TN = 1024
OUT_TN = 1024


def _seq_row(tile, tm):
    n_ctx = N_CTX_TOK // tm
    per_lat = DEC_SEQ // tm
    return jnp.where(tile < n_ctx, 0, 1 + (tile - n_ctx) // per_lat)


def _stream_specs(tm, width, n_col=1):
    n_ctx = N_CTX_TOK // tm

    def ctx_map(i, j):
        return jnp.minimum(i, n_ctx - 1), (jnp.where(i < n_ctx, j, n_col - 1) if n_col > 1 else 0)

    def lat_map(i, j):
        return jnp.maximum(i - n_ctx, 0), (jnp.where(i >= n_ctx, j, 0) if n_col > 1 else 0)

    return pl.BlockSpec((tm, width), ctx_map), pl.BlockSpec((tm, width), lat_map)


def _on_stream(tile, tm, fn):
    n_ctx = N_CTX_TOK // tm
    pl.when(tile < n_ctx)(functools.partial(fn, 0))
    pl.when(tile >= n_ctx)(functools.partial(fn, 1))


def _ada_body(c_ref, w_ref, b_ref, o_ref):
    cv = c_ref[...]
    s = cv * jax.nn.sigmoid(cv)
    o_ref[0] = jnp.dot(s.astype(BF16), w_ref[0].astype(BF16), preferred_element_type=F32) + b_ref[0]


def _ada(cvec, ada_w, ada_b):
    tn = ADA_TN
    n = N_MOD * D_MODEL
    return pl.pallas_call(
        _ada_body,
        out_shape=jax.ShapeDtypeStruct((DEPTH, N_SEQ_ROWS, n), F32),
        grid=(DEPTH, n // tn),
        in_specs=[
            pl.BlockSpec((N_SEQ_ROWS, D_MODEL), lambda l, j: (0, 0)),
            pl.BlockSpec((1, D_MODEL, tn), lambda l, j: (l, 0, j)),
            pl.BlockSpec((1, 1, tn), lambda l, j: (l, 0, j)),
        ],
        out_specs=pl.BlockSpec((1, N_SEQ_ROWS, tn), lambda l, j: (l, 0, j)),
        compiler_params=pltpu.CompilerParams(
            dimension_semantics=("parallel", "parallel"), vmem_limit_bytes=VMEM_LIMIT),
        name="ada_mod",
    )(cvec, ada_w, ada_b.reshape(DEPTH, 1, n))


def _norm_mod(x, nw, sc, sh):
    ms = jnp.mean(x * x, axis=-1, keepdims=True)
    y = x * lax.rsqrt(ms + EPS) * nw
    return y * (1.0 + sc) + sh


PROJ_TM = 1024
PROJ_TN = 1024


def _inproj_body(x_ref, nw_ref, sc_ref, sh_ref, w_ref, ws_ref, o_ref, os_ref, h_ref):
    @pl.when(pl.program_id(1) == 0)
    def _():
        hb = _norm_mod(x_ref[...], nw_ref[...], sc_ref[...], sh_ref[...]).astype(BF16)
        h_ref[...] = hb
        os_ref[...] = jnp.dot(hb, ws_ref[...], preferred_element_type=F32)

    o_ref[...] = jnp.dot(h_ref[...], w_ref[...], preferred_element_type=F32)


def _inproj(x, nw, mod, w_main, w_small, layer):
    tm, tn = PROJ_TM, PROJ_TN
    return pl.pallas_call(
        _inproj_body,
        out_shape=(jax.ShapeDtypeStruct((N_TOK, PROJ_MAIN), F32),
                   jax.ShapeDtypeStruct((N_TOK, PROJ_SMALL), F32)),
        grid=(N_TOK // tm, PROJ_MAIN // tn),
        in_specs=[
            pl.BlockSpec((tm, D_MODEL), lambda i, j: (i, 0)),
            pl.BlockSpec((1, D_MODEL), lambda i, j: (0, 0)),
            pl.BlockSpec((None, None, 1, D_MODEL), lambda i, j: (_seq_row(i, tm), 1, 0, 0)),
            pl.BlockSpec((None, None, 1, D_MODEL), lambda i, j: (_seq_row(i, tm), 0, 0, 0)),
            pl.BlockSpec((None, D_MODEL, tn), lambda i, j: (layer, 0, j)),
            pl.BlockSpec((None, D_MODEL, PROJ_SMALL), lambda i, j: (layer, 0, 0)),
        ],
        out_specs=(pl.BlockSpec((tm, tn), lambda i, j: (i, j)),
                   pl.BlockSpec((tm, PROJ_SMALL), lambda i, j: (i, 0))),
        scratch_shapes=[pltpu.VMEM((tm, D_MODEL), BF16)],
        compiler_params=pltpu.CompilerParams(
            dimension_semantics=("parallel", "arbitrary"), vmem_limit_bytes=VMEM_LIMIT),
        name="in_proj",
    )(x, nw, mod, mod, w_main, w_small)


def _outproj_body(*refs):
    n_mix = 4
    m_refs, (w_ref, x_ref, g_ref, o_ref) = refs[:2 * n_mix], refs[2 * n_mix:]

    def compute(stream):
        acc = None
        for p in range(n_mix):
            part = jnp.dot(m_refs[2 * p + stream][...], w_ref[p * GROUP_WIDTH:(p + 1) * GROUP_WIDTH, :],
                           preferred_element_type=F32)
            acc = part if acc is None else acc + part
        o_ref[...] = x_ref[...] + g_ref[...] * acc

    _on_stream(pl.program_id(0), PROJ_TM, compute)


def _outproj(parts, w, x, mod, layer):
    tm, tn = PROJ_TM, OUT_TN
    in_specs = []
    for _ in parts:
        in_specs += list(_stream_specs(tm, GROUP_WIDTH))
    in_specs += [pl.BlockSpec((None, D_MODEL, tn), lambda i, j: (layer, 0, j)),
                 pl.BlockSpec((tm, tn), lambda i, j: (i, j)),
                 pl.BlockSpec((None, None, 1, tn), lambda i, j: (_seq_row(i, tm), 2, 0, j))]
    return pl.pallas_call(
        _outproj_body,
        out_shape=jax.ShapeDtypeStruct((N_TOK, D_MODEL), F32),
        grid=(N_TOK // tm, D_MODEL // tn),
        in_specs=in_specs,
        out_specs=pl.BlockSpec((tm, tn), lambda i, j: (i, j)),
        compiler_params=pltpu.CompilerParams(
            dimension_semantics=("parallel", "arbitrary"), vmem_limit_bytes=VMEM_LIMIT),
        name="out_proj",
    )(*(a for pair in parts for a in pair), w, x, mod)


FFN_TM = 512
FFN_TH = 512


def _ffn_body(x_ref, nw_ref, sc_ref, sh_ref, g_ref, w1_ref, w3_ref, w2_ref, fw_ref, *rest, final_norm):
    o_refs, (h_ref, acc_ref) = rest[:-2], rest[-2:]
    i, k = pl.program_id(0), pl.program_id(1)

    @pl.when(k == 0)
    def _():
        h_ref[...] = _norm_mod(x_ref[...], nw_ref[...], sc_ref[...], sh_ref[...]).astype(BF16)
        acc_ref[...] = jnp.zeros_like(acc_ref)

    h = h_ref[...]
    a = jnp.dot(h, w1_ref[...], preferred_element_type=F32)
    b = jnp.dot(h, w3_ref[...], preferred_element_type=F32)
    g = (a * jax.nn.sigmoid(a) * b).astype(BF16)
    acc_ref[...] += jnp.dot(g, w2_ref[...], preferred_element_type=F32)

    @pl.when(k == pl.num_programs(1) - 1)
    def _():
        y = x_ref[...] + g_ref[...] * acc_ref[...]
        if not final_norm:
            o_refs[0][...] = y
        else:
            ms = jnp.mean(y * y, axis=-1, keepdims=True)
            y = y * lax.rsqrt(ms + EPS) * fw_ref[...]

            def write(stream):
                o_refs[stream][...] = y

            _on_stream(i, FFN_TM, write)


def _ffn(x, nw, mod, w1, w3, w2, fw, layer):
    final_norm = layer == DEPTH - 1
    tm, th = FFN_TM, FFN_TH
    if final_norm:
        out_shape = (jax.ShapeDtypeStruct((N_CTX_TOK, D_MODEL), F32), jax.ShapeDtypeStruct((N_LAT_TOK, D_MODEL), F32))
        out_specs = _stream_specs(tm, D_MODEL)
    else:
        out_shape = jax.ShapeDtypeStruct((N_TOK, D_MODEL), F32)
        out_specs = pl.BlockSpec((tm, D_MODEL), lambda i, k: (i, 0))
    return pl.pallas_call(
        functools.partial(_ffn_body, final_norm=final_norm),
        out_shape=out_shape,
        grid=(N_TOK // tm, FFN_HIDDEN // th),
        in_specs=[
            pl.BlockSpec((tm, D_MODEL), lambda i, k: (i, 0)),
            pl.BlockSpec((1, D_MODEL), lambda i, k: (0, 0)),
            pl.BlockSpec((None, None, 1, D_MODEL), lambda i, k: (_seq_row(i, tm), 4, 0, 0)),
            pl.BlockSpec((None, None, 1, D_MODEL), lambda i, k: (_seq_row(i, tm), 3, 0, 0)),
            pl.BlockSpec((None, None, 1, D_MODEL), lambda i, k: (_seq_row(i, tm), 5, 0, 0)),
            pl.BlockSpec((None, D_MODEL, th), lambda i, k: (layer, 0, k)),
            pl.BlockSpec((None, D_MODEL, th), lambda i, k: (layer, 0, k)),
            pl.BlockSpec((None, th, D_MODEL), lambda i, k: (layer, k, 0)),
            pl.BlockSpec((1, D_MODEL), lambda i, k: (0, 0)),
        ],
        out_specs=out_specs,
        scratch_shapes=[pltpu.VMEM((tm, D_MODEL), BF16), pltpu.VMEM((tm, D_MODEL), F32)],
        compiler_params=pltpu.CompilerParams(
            dimension_semantics=("arbitrary", "arbitrary"), vmem_limit_bytes=VMEM_LIMIT),
        name="ffn",
    )(x, nw, mod, mod, mod, w1, w3, w2, fw)


def _axial_rope(l):
    n_rows = l // GRID_W
    t_row = jnp.repeat(jnp.arange(n_rows, dtype=F32), GRID_W)
    t_col = jnp.tile(jnp.arange(GRID_W, dtype=F32), n_rows)
    n_freq = HEAD_DIM // 4
    inv = ROPE_BASE ** (-jnp.arange(n_freq, dtype=F32) / n_freq)
    ang = jnp.concatenate([t_row[:, None] * inv, t_col[:, None] * inv], axis=-1)
    return jnp.cos(ang), jnp.sin(ang)


MIX_CHUNK = 128
DIRS = 2
NT_DIMS = (((1,), (1,)), ((), ()))
TN_DIMS = (((0,), (0,)), ((), ()))


def _dot(a, b):
    return jnp.dot(a.astype(BF16), b.astype(BF16), preferred_element_type=F32)


def _dot_nt(a, b):
    return lax.dot_general(a.astype(BF16), b.astype(BF16), NT_DIMS, preferred_element_type=F32)


def _dot_tn(a, b):
    return lax.dot_general(a.astype(BF16), b.astype(BF16), TN_DIMS, preferred_element_type=F32)


def _silu(x):
    return x * jax.nn.sigmoid(x)


def _seq_spec(seq_len, width, row_off, col_blk, pipeline_mode=None):
    return pl.BlockSpec((seq_len, width), lambda b: (row_off + b, col_blk), pipeline_mode=pipeline_mode)


def _mix_params(n_par):
    return pltpu.CompilerParams(dimension_semantics=("parallel",) * n_par, vmem_limit_bytes=VMEM_LIMIT)


def _ret_body(*refs, n_chunks, use_rope, has_s0):
    it = iter(refs)
    lg_ref, q_ref, k_ref, v_ref, g_ref = (next(it) for _ in range(5))
    cos_ref, sin_ref = (next(it), next(it)) if use_rope else (None, None)
    s0_ref = next(it) if has_s0 else None
    o_ref, sf_ref, acc_ref, s_ref, intra_ref, qd_ref, kd_ref, cd_ref = (next(it) for _ in range(8))
    C, H = MIX_CHUNK, N_HEADS

    row = lax.broadcasted_iota(jnp.int32, (C, C), 0)
    col = lax.broadcasted_iota(jnp.int32, (C, C), 1)
    rel = (row - col).astype(F32)
    pos = lax.broadcasted_iota(jnp.int32, (C, HEAD_DIM), 0).astype(F32)
    for d in range(DIRS):
        for h in range(H):
            lg = lg_ref[d, h]
            if d == 0:
                intra_ref[d, h] = jnp.where(rel >= 0, jnp.exp(jnp.maximum(rel, 0.0) * lg), 0.0)
                qd_ref[d, h] = jnp.exp((pos + 1.0) * lg)
                kd_ref[d, h] = jnp.exp((C - 1.0 - pos) * lg)
            else:
                intra_ref[d, h] = jnp.where(rel <= 0, jnp.exp(jnp.maximum(-rel, 0.0) * lg), 0.0)
                qd_ref[d, h] = jnp.exp((C - pos) * lg)
                kd_ref[d, h] = jnp.exp(pos * lg)
            cd_ref[d, h] = jnp.exp(jnp.full((C, HEAD_DIM), C, F32) * lg)
    if has_s0:
        s_ref[...] = s0_ref[...]
    else:
        s_ref[...] = jnp.zeros_like(s_ref)
    acc_ref[...] = jnp.zeros_like(acc_ref)

    units = [(d, h) for d in range(DIRS) for h in range(H)]

    def chunk_step(n, carry):
        rows_d = [pl.ds(pl.multiple_of(c * C, C), C) for c in (n, n_chunks - 1 - n)]
        q_l, k_l, v_l, sc_l = [], [], [], []
        for d, h in units:
            rows, cols = rows_d[d], slice(h * HEAD_DIM, (h + 1) * HEAD_DIM)
            q, k = q_ref[rows, cols], k_ref[rows, cols]
            if use_rope:
                cs, sn = cos_ref[rows, :], sin_ref[rows, :]
                q = q * cs + pltpu.roll(q, HEAD_DIM // 2, 1) * sn
                k = k * cs + pltpu.roll(k, HEAD_DIM // 2, 1) * sn
            k = k * HEAD_DIM ** -0.5
            q_l.append(q)
            k_l.append(k)
            v_l.append(v_ref[rows, cols].astype(BF16))
            sc_l.append(_dot_nt(q, k) * intra_ref[d, h])
        for i, (d, h) in enumerate(units):
            cols = slice(h * HEAD_DIM, (h + 1) * HEAD_DIM)
            s = s_ref[d, h]
            acc_ref[rows_d[d], cols] += _dot(jnp.concatenate([sc_l[i], q_l[i] * qd_ref[d, h]], axis=1),
                                             jnp.concatenate([v_l[i], s.astype(BF16)], axis=0))
            s_ref[d, h] = cd_ref[d, h] * s + _dot_tn(k_l[i] * kd_ref[d, h], v_l[i])
        return carry

    lax.fori_loop(0, n_chunks, chunk_step, 0)
    sf_ref[...] = s_ref[...]

    def finish(n, carry):
        rows = pl.ds(pl.multiple_of(n * C, C), C)
        for h in range(H):
            cols = slice(h * HEAD_DIM, (h + 1) * HEAD_DIM)
            o = acc_ref[rows, cols]
            mu = jnp.mean(o, axis=-1, keepdims=True)
            oc = o - mu
            y = oc * lax.rsqrt(jnp.mean(oc * oc, axis=-1, keepdims=True) + EPS)
            o_ref[rows, cols] = (y * _silu(g_ref[rows, cols])).astype(BF16)
        return carry

    lax.fori_loop(0, n_chunks, finish, 0)


def _retention_pallas(pm, log_gamma, n_seq, seq_len, row_off, rope2, s0, layer):
    use_rope, has_s0 = rope2 is not None, s0 is not None
    st_shape = (DIRS, N_HEADS, HEAD_DIM, HEAD_DIM)
    in_specs = [pl.BlockSpec(memory_space=pltpu.SMEM)]
    in_specs += [_seq_spec(seq_len, GROUP_WIDTH, row_off, cb) for cb in range(4)]
    args = [log_gamma, pm, pm, pm, pm]
    if use_rope:
        in_specs += [pl.BlockSpec((seq_len, HEAD_DIM), lambda b: (0, 0))] * 2
        args += list(rope2)
    if has_s0:
        in_specs.append(pl.BlockSpec((None, None) + st_shape, lambda b: (b, layer, 0, 0, 0, 0)))
        args.append(s0)
    return pl.pallas_call(
        functools.partial(_ret_body, n_chunks=seq_len // MIX_CHUNK, use_rope=use_rope, has_s0=has_s0),
        out_shape=(jax.ShapeDtypeStruct((n_seq * seq_len, GROUP_WIDTH), BF16),
                   jax.ShapeDtypeStruct((n_seq,) + st_shape, F32)),
        grid=(n_seq,),
        in_specs=in_specs,
        out_specs=(pl.BlockSpec((seq_len, GROUP_WIDTH), lambda b: (b, 0)),
                   pl.BlockSpec((None,) + st_shape, lambda b: (b, 0, 0, 0, 0))),
        scratch_shapes=[pltpu.VMEM((seq_len, GROUP_WIDTH), F32), pltpu.VMEM(st_shape, F32),
                        pltpu.VMEM((DIRS, N_HEADS, MIX_CHUNK, MIX_CHUNK), F32),
                        pltpu.VMEM((DIRS, N_HEADS, MIX_CHUNK, HEAD_DIM), F32),
                        pltpu.VMEM((DIRS, N_HEADS, MIX_CHUNK, HEAD_DIM), F32),
                        pltpu.VMEM((DIRS, N_HEADS, MIX_CHUNK, HEAD_DIM), F32)],
        compiler_params=_mix_params(1),
        name="retention",
    )(*args)


def _rope_tables(l):
    cos, sin = _axial_rope(l)
    return jnp.concatenate([cos, cos], axis=-1), jnp.concatenate([-sin, sin], axis=-1)


S5_HALF_G = S5_GROUPS // 2
S5_HALF_U = S5_HALF_G * S5_CH
S5_HALF_X = S5_HALF_G * S5_N
S5_X = S5_GROUPS * S5_N
S5_TC = 512
S5_BLK = 8
S5_TABS = 4


def _s5_tables(a_re, a_im, log_step, b_re, b_im, c_re, c_im):
    dt = jnp.exp(log_step)[..., None]
    mag = jnp.exp(a_re * dt)
    ab_re = mag * jnp.cos(a_im * dt)
    ab_im = mag * jnp.sin(a_im * dt)
    den = a_re * a_re + a_im * a_im
    nr = ab_re - 1.0
    f_re = (nr * a_re + ab_im * a_im) / den
    f_im = (ab_im * a_re - nr * a_im) / den
    bb_re = f_re[..., None] * b_re - f_im[..., None] * b_im
    bb_im = f_re[..., None] * b_im + f_im[..., None] * b_re
    eye = jnp.eye(S5_HALF_G, dtype=F32)

    def in_mat(bb):
        bb = bb.reshape(DIRS, 2, S5_HALF_G, S5_N, S5_CH)
        return jnp.einsum('dhgnc,gk->dhgckn', bb, eye).reshape(DIRS, 2, S5_HALF_U, S5_HALF_X)

    def out_mat(cc):
        cc = cc.reshape(DIRS, 2, S5_HALF_G, S5_CH, S5_N)
        return jnp.einsum('dhgcn,gk->dhgnkc', cc, eye).reshape(DIRS, 2, S5_HALF_X, S5_HALF_U)

    bm = jnp.concatenate([in_mat(bb_re), in_mat(bb_im)], axis=-1).astype(BF16)
    cm = jnp.concatenate([out_mat(c_re), -out_mat(c_im)], axis=-2).astype(BF16)
    t = jnp.arange(S5_BLK, dtype=F32)
    order = jnp.stack([t, S5_BLK - 1.0 - t])
    shifts = 2.0 ** jnp.arange(S5_TABS - 1, dtype=F32)
    expo = jnp.concatenate([jnp.where(order[:, None, :] >= shifts[None, :, None], shifts[None, :, None], jnp.nan),
                            order[:, None, :] + 1.0], axis=1)
    live = ~jnp.isnan(expo)
    e = jnp.where(live, expo, 0.0)[..., None]
    adt_re = (a_re * dt).reshape(DIRS, 1, 1, S5_X)
    adt_im = (a_im * dt).reshape(DIRS, 1, 1, S5_X)
    pmag = jnp.where(live[..., None], jnp.exp(e * adt_re), 0.0)
    pw_re = pmag * jnp.cos(e * adt_im)
    pw_im = pmag * jnp.sin(e * adt_im)
    return bm, cm, pw_re, pw_im


def _gelu_tanh(x):
    return 0.5 * x * (1.0 + jnp.tanh(math.sqrt(2.0 / math.pi) * (x + 0.044715 * (x * x * x))))


def _s5_body(*refs, seq_len, has_s0):
    it = iter(refs)
    u_ref, bm_ref, cm_ref, pwr_ref, pwi_ref, d_ref, gw_ref, gb_ref = (next(it) for _ in range(8))
    x0r_ref, x0i_ref = (next(it), next(it)) if has_s0 else (None, None)
    o_ref, sfr_ref, sfi_ref, y_ref, xr_ref, xi_ref, xb_ref = (next(it) for _ in range(7))
    tc = min(seq_len, S5_TC)
    n_tiles = seq_len // tc
    n_pair = tc // (2 * S5_BLK)

    y_ref[...] = d_ref[...] * u_ref[...]
    for d in range(DIRS):
        last = S5_BLK - 1 if d == 0 else 0
        for hf in range(2):
            xs = slice(hf * S5_HALF_X, (hf + 1) * S5_HALF_X)
            us = slice(hf * S5_HALF_U, (hf + 1) * S5_HALF_U)

            def scan_block(x_in, carry, d=d, xs=xs, last=last):
                xr, xi = x_in
                car_re, car_im = carry
                for k in range(S5_TABS - 1):
                    s = 1 << k
                    shift = s if d == 0 else S5_BLK - s
                    p_re, p_im = pwr_ref[d, k, :, xs], pwi_ref[d, k, :, xs]
                    sr, si = pltpu.roll(xr, shift, 0), pltpu.roll(xi, shift, 0)
                    xr, xi = xr + p_re * sr - p_im * si, xi + p_re * si + p_im * sr
                p_re, p_im = pwr_ref[d, S5_TABS - 1, :, xs], pwi_ref[d, S5_TABS - 1, :, xs]
                xr, xi = xr + p_re * car_re - p_im * car_im, xi + p_re * car_im + p_im * car_re
                return (xr, xi), (xr[last:last + 1, :], xi[last:last + 1, :])

            def scan_pair(j, carry, d=d, scan_block=scan_block):
                pair = j if d == 0 else n_pair - 1 - j
                rows = pl.ds(pl.multiple_of(pair * 2 * S5_BLK, 2 * S5_BLK), 2 * S5_BLK)
                xr2, xi2 = xr_ref[rows, :], xi_ref[rows, :]
                halves = [(xr2[:S5_BLK], xi2[:S5_BLK]), (xr2[S5_BLK:], xi2[S5_BLK:])]
                order = (0, 1) if d == 0 else (1, 0)
                out = [None, None]
                for idx in order:
                    out[idx], carry = scan_block(halves[idx], carry)
                xb_ref[rows, :S5_HALF_X] = jnp.concatenate([out[0][0], out[1][0]], axis=0).astype(BF16)
                xb_ref[rows, S5_HALF_X:] = jnp.concatenate([out[0][1], out[1][1]], axis=0).astype(BF16)
                return carry

            def scan_tile(i, carry, d=d, hf=hf, us=us, scan_pair=scan_pair):
                tile = i if d == 0 else n_tiles - 1 - i
                rows = pl.ds(pl.multiple_of(tile * tc, tc), tc)
                bu = _dot(u_ref[rows, us], bm_ref[d, hf])
                xr_ref[...] = bu[:, :S5_HALF_X]
                xi_ref[...] = bu[:, S5_HALF_X:]
                carry = lax.fori_loop(0, n_pair, scan_pair, carry)
                y_ref[rows, us] += jnp.dot(xb_ref[...], cm_ref[d, hf], preferred_element_type=F32)
                return carry

            if has_s0:
                carry0 = (x0r_ref[d, :, xs], x0i_ref[d, :, xs])
            else:
                carry0 = (jnp.zeros((1, S5_HALF_X), F32), jnp.zeros((1, S5_HALF_X), F32))
            car_re, car_im = lax.fori_loop(0, n_tiles, scan_tile, carry0)
            sfr_ref[d, :, xs] = car_re
            sfi_ref[d, :, xs] = car_im

    z = _gelu_tanh(y_ref[...])
    o_ref[...] = (z * jax.nn.sigmoid(_dot(z, gw_ref[...]) + gb_ref[...])).astype(BF16)


def _s5_pallas(pm, tables, s5_d, glu_w, glu_b, n_seq, seq_len, row_off, x0, layer):
    bm, cm, pw_re, pw_im = tables
    has_s0 = x0 is not None
    full = lambda shape: pl.BlockSpec(shape, lambda b: (0,) * len(shape))
    of_layer = lambda t: pl.BlockSpec((None,) + t.shape[1:], lambda b: (layer,) + (0,) * (t.ndim - 1))
    in_specs = [_seq_spec(seq_len, GROUP_WIDTH, row_off, 13),
                of_layer(bm), of_layer(cm), of_layer(pw_re), of_layer(pw_im),
                full((1, GROUP_WIDTH)), full((GROUP_WIDTH, GROUP_WIDTH)), full((1, GROUP_WIDTH))]
    args = [pm, bm, cm, pw_re, pw_im, s5_d.reshape(1, GROUP_WIDTH), glu_w.astype(BF16),
            glu_b.reshape(1, GROUP_WIDTH)]
    if has_s0:
        in_specs += [pl.BlockSpec((None, None, DIRS, 1, S5_X), lambda b: (b, layer, 0, 0, 0))] * 2
        args += list(x0)
    st = jax.ShapeDtypeStruct((n_seq, DIRS, 1, S5_X), F32)
    st_spec = pl.BlockSpec((None, DIRS, 1, S5_X), lambda b: (b, 0, 0, 0))
    tc = min(seq_len, S5_TC)
    return pl.pallas_call(
        functools.partial(_s5_body, seq_len=seq_len, has_s0=has_s0),
        out_shape=(jax.ShapeDtypeStruct((n_seq * seq_len, GROUP_WIDTH), BF16), st, st),
        grid=(n_seq,),
        in_specs=in_specs,
        out_specs=(pl.BlockSpec((seq_len, GROUP_WIDTH), lambda b: (b, 0)), st_spec, st_spec),
        scratch_shapes=[pltpu.VMEM((seq_len, GROUP_WIDTH), F32),
                        pltpu.VMEM((tc, S5_HALF_X), F32), pltpu.VMEM((tc, S5_HALF_X), F32),
                        pltpu.VMEM((tc, 2 * S5_HALF_X), BF16)],
        compiler_params=_mix_params(1),
        name="s5",
    )(*args)


GLA_LEVELS = 7


def _chunk_cumsum_rows(x, rowi):
    s = 1
    while s < MIX_CHUNK:
        x = x + jnp.where(rowi >= s, pltpu.roll(x, s, 0), 0.0)
        s *= 2
    return x


def _gla_body(*refs, n_chunks, has_s0):
    it = iter(refs)
    q_ref, f0_ref, f1_ref, i_ref, g_ref, lb_ref, nw_ref = (next(it) for _ in range(7))
    s0_ref = next(it) if has_s0 else None
    o_ref, sf_ref, acc_ref, s_ref, code_ref = (next(it) for _ in range(5))
    C, H = MIX_CHUNK, N_HEADS
    f_refs = (f0_ref, f1_ref)

    rowi = lax.broadcasted_iota(jnp.int32, (C, HEAD_DIM), 0)
    ri = lax.broadcasted_iota(jnp.int32, (C, C), 0)
    ci = lax.broadcasted_iota(jnp.int32, (C, C), 1)
    top_bit = 31 - lax.clz(ri ^ ci)
    code_ref[...] = jnp.where(ri > ci, top_bit, jnp.where(ri < ci, -1 - top_bit, GLA_LEVELS))
    if has_s0:
        s_ref[...] = s0_ref[...]
    else:
        s_ref[...] = jnp.zeros_like(s_ref)
    acc_ref[...] = jnp.zeros_like(acc_ref)

    def chunk_step(n, carry):
        for d in range(DIRS):
            c = n if d == 0 else n_chunks - 1 - n
            rows = pl.ds(pl.multiple_of(c * C, C), C)
            for h in range(H):
                cols = slice(h * HEAD_DIM, (h + 1) * HEAD_DIM)
                code = code_ref[...]
                q = _silu(q_ref[rows, cols]) * HEAD_DIM ** -0.5
                v = i_ref[rows, cols].astype(BF16)
                fx = f_refs[d][rows, cols]
                lb = lb_ref[d:d + 1, cols]
                sig = 1.0 / (1.0 + jnp.exp(-fx))
                logf = jnp.log2(jnp.maximum(lb, LB_FLOOR) + (1.0 - lb) * sig)
                k = (1.0 - lb) * (1.0 - sig)
                cum = _chunk_cumsum_rows(logf, rowi)
                own = cum
                attn = jnp.where(code == GLA_LEVELS, _dot_nt(q, k), 0.0)
                for lvl in range(GLA_LEVELS):
                    m = 1 << lvl
                    prev = pltpu.roll(own, m, 0)
                    pre = jnp.minimum(cum - prev, 0.0)
                    suf = own - cum
                    if d == 0:
                        sc = _dot_nt(q * jnp.exp2(pre), k * jnp.exp2(suf))
                        hit = code == lvl
                    else:
                        sc = _dot_nt(q * jnp.exp2(suf + logf), k * jnp.exp2(jnp.minimum(pre - logf, 0.0)))
                        hit = code == -1 - lvl
                    attn = jnp.where(hit, sc, attn)
                    own = jnp.where(((rowi >> lvl) & 1) == 0, pltpu.roll(own, C - m, 0), own)
                tot = own
                st = s_ref[d, h]
                if d == 0:
                    q_in, k_out = q * jnp.exp2(cum), k * jnp.exp2(tot - cum)
                else:
                    q_in, k_out = q * jnp.exp2(tot - cum + logf), k * jnp.exp2(cum - logf)
                acc_ref[rows, cols] += _dot(attn, v) + _dot_nt(q_in, st)
                s_ref[d, h] = jnp.exp2(tot[0:1, :]) * st + _dot_tn(v, k_out)
        return carry

    lax.fori_loop(0, n_chunks, chunk_step, 0)
    sf_ref[...] = s_ref[...]

    def finish(n, carry):
        rows = pl.ds(pl.multiple_of(n * C, C), C)
        for h in range(H):
            cols = slice(h * HEAD_DIM, (h + 1) * HEAD_DIM)
            o = acc_ref[rows, cols]
            y = o * lax.rsqrt(jnp.mean(o * o, axis=-1, keepdims=True) + EPS) * nw_ref[...]
            o_ref[rows, cols] = (y * _silu(g_ref[rows, cols])).astype(BF16)
        return carry

    lax.fori_loop(0, n_chunks, finish, 0)


def _gla_pallas(pm, lower_bound, norm_w, n_seq, seq_len, row_off, s0_t, layer):
    has_s0 = s0_t is not None
    st_shape = (DIRS, N_HEADS, HEAD_DIM, HEAD_DIM)
    in_specs = [_seq_spec(seq_len, GROUP_WIDTH, row_off, cb) for cb in (8, 9, 10, 11, 12)]
    in_specs += [pl.BlockSpec((DIRS, GROUP_WIDTH), lambda b: (0, 0)), pl.BlockSpec((1, HEAD_DIM), lambda b: (0, 0))]
    args = [pm] * 5 + [lower_bound, norm_w.reshape(1, HEAD_DIM)]
    if has_s0:
        in_specs.append(pl.BlockSpec((None, None) + st_shape, lambda b: (b, layer, 0, 0, 0, 0)))
        args.append(s0_t)
    return pl.pallas_call(
        functools.partial(_gla_body, n_chunks=seq_len // MIX_CHUNK, has_s0=has_s0),
        out_shape=(jax.ShapeDtypeStruct((n_seq * seq_len, GROUP_WIDTH), BF16),
                   jax.ShapeDtypeStruct((n_seq,) + st_shape, F32)),
        grid=(n_seq,),
        in_specs=in_specs,
        out_specs=(pl.BlockSpec((seq_len, GROUP_WIDTH), lambda b: (b, 0)),
                   pl.BlockSpec((None,) + st_shape, lambda b: (b, 0, 0, 0, 0))),
        scratch_shapes=[pltpu.VMEM((seq_len, GROUP_WIDTH), F32), pltpu.VMEM(st_shape, F32),
                        pltpu.VMEM((MIX_CHUNK, MIX_CHUNK), jnp.int32)],
        compiler_params=_mix_params(1),
        name="hgrn2",
    )(*args)


GDN_BLOCK_BITS = 4
GDN_MERGES = 3
N_GATES = DIRS * N_HEADS


def _softplus(x):
    return jnp.maximum(x, 0.0) + jnp.log1p(jnp.exp(-jnp.abs(x)))


def _gdn_body(*refs, seq_len, n_sub, has_s0):
    it = iter(refs)
    (q_ref, k_ref, v_ref, g_ref, ps_ref, pst_ref, cw_ref, prow_ref, pcol_ref, nw_ref) = (next(it) for _ in range(10))
    s0_ref = next(it) if has_s0 else None
    (o_ref, sf_ref, acc_ref, s_ref, qn_ref, kn_ref, vn_ref, gate_ref, gatet_ref, blev_ref) = (
        next(it) for _ in range(10))
    C, H, L = MIX_CHUNK, N_HEADS, seq_len
    R = n_sub * L
    n_chunks = L // C
    w = GROUP_WIDTH

    pos = lax.broadcasted_iota(jnp.int32, (R, HEAD_DIM), 0) & (L - 1)
    for part, (src, dst) in enumerate(((q_ref, qn_ref), (k_ref, kn_ref), (v_ref, vn_ref))):
        for h in range(H):
            cols = slice(h * HEAD_DIM, (h + 1) * HEAD_DIM)
            wc = slice(part * w + h * HEAD_DIM, part * w + (h + 1) * HEAD_DIM)
            x = src[:, cols]
            x_prev = jnp.where(pos >= 1, pltpu.roll(x, 1, 0), 0.0)
            x_next = jnp.where(pos < L - 1, pltpu.roll(x, R - 1, 0), 0.0)
            y = _silu(cw_ref[0:1, wc] * x_prev + cw_ref[1:2, wc] * x + cw_ref[2:3, wc] * x_next)
            if part < 2:
                y = y * lax.rsqrt(jnp.sum(y * y, axis=-1, keepdims=True) + EPS)
            if part == 0:
                y = y * HEAD_DIM ** -0.5
            dst[:, cols] = y

    rowi = lax.broadcasted_iota(jnp.int32, (C, HEAD_DIM), 0)
    lane = lax.broadcasted_iota(jnp.int32, (C, HEAD_DIM), 1)
    lane_t = lax.broadcasted_iota(jnp.int32, (2 * N_GATES, C), 1)
    row_t = lax.broadcasted_iota(jnp.int32, (2 * N_GATES, C), 0)
    for c in range(R // C):
        rows = slice(c * C, (c + 1) * C)
        a = ps_ref[rows, :]
        la = -jnp.exp(prow_ref[0:1, :]) * _softplus(a + prow_ref[1:2, :])
        pre = _chunk_cumsum_rows(la, rowi)
        tot = jnp.broadcast_to(pre[C - 1:C, :], (C, HEAD_DIM))
        g = jnp.where(lane < N_HEADS, pre, tot - pre + la)
        gate_ref[rows, :] = jnp.where(lane < N_GATES, g,
                                      jnp.where(lane < 2 * N_GATES, jax.nn.sigmoid(a),
                                                pltpu.roll(tot, 2 * N_GATES, 1)))
        at = pst_ref[:, rows]
        lat = -jnp.exp(pcol_ref[:, 0:1]) * _softplus(at + pcol_ref[:, 1:2])
        pre_t, s = lat, 1
        while s < C:
            pre_t = pre_t + jnp.where(lane_t >= s, pltpu.roll(pre_t, s, 1), 0.0)
            s *= 2
        tot_t = jnp.broadcast_to(pre_t[:, C - 1:C], (2 * N_GATES, C))
        g_t = jnp.where(row_t < N_HEADS, pre_t, tot_t - pre_t + lat)
        gatet_ref[:, rows] = jnp.where(row_t < N_GATES, g_t, jax.nn.sigmoid(at))

    ri = lax.broadcasted_iota(jnp.int32, (C, C), 0)
    ci = lax.broadcasted_iota(jnp.int32, (C, C), 1)
    bx = (ri >> GDN_BLOCK_BITS) ^ (ci >> GDN_BLOCK_BITS)
    blev_ref[...] = jnp.where(bx == 0, 0, 32 - lax.clz(bx))
    if has_s0:
        for s in range(n_sub):
            s_ref[s * DIRS:(s + 1) * DIRS] = s0_ref[s]
    else:
        s_ref[...] = jnp.zeros_like(s_ref)
    acc_ref[...] = jnp.zeros_like(acc_ref)

    units = [(s, d, h) for s in range(n_sub) for d in range(DIRS) for h in range(H)]

    def chunk_step(n, carry):
        eye = jnp.where(ri == ci, 1.0, 0.0)
        blev = blev_ref[...]
        rows_sd = [[pl.ds(pl.multiple_of(s * L + c * C, C), C) for c in (n, n_chunks - 1 - n)] for s in range(n_sub)]
        incl_d = [ri >= ci, ri <= ci]
        strict_d = [ri > ci, ri < ci]
        q_l, et_l, ek_l, rhs_l, a_l, attn_l = ([] for _ in range(6))
        for s, d, h in units:
            rows, cols, cg = rows_sd[s][d], slice(h * HEAD_DIM, (h + 1) * HEAD_DIM), d * N_HEADS + h
            q, k, v = qn_ref[rows, cols], kn_ref[rows, cols], vn_ref[rows, cols]
            g_i = jnp.broadcast_to(gate_ref[rows, cg:cg + 1], (C, C))
            b_i = jnp.broadcast_to(gate_ref[rows, N_GATES + cg:N_GATES + cg + 1], (C, C))
            tot = jnp.broadcast_to(gate_ref[rows, 2 * N_GATES + cg:2 * N_GATES + cg + 1], (C, C))
            g_j = gatet_ref[cg:cg + 1, rows]
            decay = jnp.where(incl_d[d], jnp.exp(jnp.minimum(g_i - g_j, 0.0)), 0.0)
            e_g = jnp.exp(g_i)
            kb = k.astype(BF16)
            a_l.append(jnp.where(strict_d[d], _dot_nt(kb, kb) * b_i * decay, 0.0))
            attn_l.append(_dot_nt(q, kb) * decay)
            rhs_l.append(jnp.concatenate([v * b_i, k * (b_i * e_g)], axis=1).astype(BF16))
            q_l.append(q * e_g)
            ek_l.append(k * jnp.exp(tot - g_i))
            et_l.append(jnp.exp(tot))
        b_l = [-jnp.where(blev == 0, a, 0.0) for a in a_l]
        p_l = [eye + b for b in b_l]
        for _ in range(GDN_BLOCK_BITS - 1):
            b_l = [_dot(b, b) for b in b_l]
            p_l = [p + _dot(p, b) for p, b in zip(p_l, b_l)]
        for lvl in range(1, GDN_MERGES + 1):
            ep_l = [_dot(jnp.where(blev == lvl, a, 0.0), p) for a, p in zip(a_l, p_l)]
            p_l = [p - _dot(p, ep) for p, ep in zip(p_l, ep_l)]
        uw_l = [_dot(p, rhs) for p, rhs in zip(p_l, rhs_l)]
        s_l = [s_ref[s * DIRS + d, h] for s, d, h in units]
        vn_l = [uw[:, :HEAD_DIM] - _dot(uw[:, HEAD_DIM:], st) for uw, st in zip(uw_l, s_l)]
        for i, (s, d, h) in enumerate(units):
            cols = slice(h * HEAD_DIM, (h + 1) * HEAD_DIM)
            acc_ref[rows_sd[s][d], cols] += _dot(jnp.concatenate([q_l[i], attn_l[i]], axis=1),
                                                 jnp.concatenate([s_l[i], vn_l[i]], axis=0))
            s_ref[s * DIRS + d, h] = s_l[i] * et_l[i] + _dot_tn(ek_l[i], vn_l[i])
        return carry

    lax.fori_loop(0, n_chunks, chunk_step, 0)
    sf_ref[...] = s_ref[...]

    def finish(n, carry):
        rows = pl.ds(pl.multiple_of(n * C, C), C)
        for h in range(H):
            cols = slice(h * HEAD_DIM, (h + 1) * HEAD_DIM)
            o = acc_ref[rows, cols]
            y = o * lax.rsqrt(jnp.mean(o * o, axis=-1, keepdims=True) + EPS) * nw_ref[...]
            o_ref[rows, cols] = (y * _silu(g_ref[rows, cols])).astype(BF16)
        return carry

    lax.fori_loop(0, R // C, finish, 0)


GDN_CTX_SUB = 2
GDN_LAT_SUB = 2
GDN_DOUBLE_BUFFER_BYTES = 2 * 1024 * 1024


def _gdn_pallas(pm, ps, ps_t, conv_w, a_log, dt_bias, norm_w, n_seq, seq_len, row_off, s0, layer):
    has_s0 = s0 is not None
    n_sub = GDN_LAT_SUB if has_s0 else GDN_CTX_SUB
    n_seq, seq_len, row_off = n_seq // n_sub, seq_len * n_sub, row_off // n_sub
    st_shape = (n_sub * DIRS, N_HEADS, HEAD_DIM, HEAD_DIM)
    par = jnp.stack([a_log.reshape(N_GATES), dt_bias.reshape(N_GATES)])
    par_row = jnp.pad(par, ((0, 0), (0, HEAD_DIM - N_GATES)))
    par_col = jnp.pad(par.T, ((0, N_GATES), (0, 0)))
    full = lambda shape: pl.BlockSpec(shape, lambda b: (0,) * len(shape))
    big_mode = pl.Buffered(1) if seq_len * GROUP_WIDTH * 4 > GDN_DOUBLE_BUFFER_BYTES else None
    in_specs = [_seq_spec(seq_len, GROUP_WIDTH, row_off, cb, big_mode) for cb in (4, 5, 6, 7)]
    in_specs += [_seq_spec(seq_len, PROJ_SMALL, row_off, 0),
                 pl.BlockSpec((2 * N_GATES, seq_len), lambda b: (0, row_off + b)),
                 full((GDN_CONV, 3 * GROUP_WIDTH)), full((2, HEAD_DIM)), full((2 * N_GATES, 2)), full((1, HEAD_DIM))]
    args = [pm] * 4 + [ps, ps_t, conv_w, par_row, par_col, norm_w.reshape(1, HEAD_DIM)]
    if has_s0:
        in_specs.append(pl.BlockSpec((n_sub, None, DIRS, N_HEADS, HEAD_DIM, HEAD_DIM),
                                     lambda b: (b, layer, 0, 0, 0, 0)))
        args.append(s0)
    seq_f32 = lambda width: pltpu.VMEM((seq_len, width), F32)
    out, s_fin = pl.pallas_call(
        functools.partial(_gdn_body, seq_len=seq_len // n_sub, n_sub=n_sub, has_s0=has_s0),
        out_shape=(jax.ShapeDtypeStruct((n_seq * seq_len, GROUP_WIDTH), BF16),
                   jax.ShapeDtypeStruct((n_seq,) + st_shape, F32)),
        grid=(n_seq,),
        in_specs=in_specs,
        out_specs=(pl.BlockSpec((seq_len, GROUP_WIDTH), lambda b: (b, 0)),
                   pl.BlockSpec((None,) + st_shape, lambda b: (b, 0, 0, 0, 0))),
        scratch_shapes=[seq_f32(GROUP_WIDTH), pltpu.VMEM(st_shape, F32),
                        seq_f32(GROUP_WIDTH), seq_f32(GROUP_WIDTH), seq_f32(GROUP_WIDTH),
                        seq_f32(HEAD_DIM), pltpu.VMEM((2 * N_GATES, seq_len), F32),
                        pltpu.VMEM((MIX_CHUNK, MIX_CHUNK), jnp.int32)],
        compiler_params=_mix_params(1),
        name="gated_delta",
    )(*args)
    return out, s_fin.reshape(n_seq * n_sub, DIRS, N_HEADS, HEAD_DIM, HEAD_DIM)


def kernel(x_prompt, x_sample, state_ret, state_gdn, state_hgrn, state_s5_re, state_s5_im, c, c_ctx, norm1_w, norm2_w, final_norm_w, ada_w, ada_b, in_proj, out_proj, ret_decay_logit, gdn_conv, gdn_a_log, gdn_dt_bias, gdn_norm_w, hg_lb_param, hg_norm_w, s5_a_re, s5_a_im, s5_b_re, s5_b_im, s5_c_re, s5_c_im, s5_log_step, s5_d, s5_glu_w, s5_glu_b, ffn_w1, ffn_w3, ffn_w2):
    lb_soft = jax.nn.softmax(hg_lb_param, axis=0)
    lower_bounds = jnp.cumsum(lb_soft, axis=0) - lb_soft[0]
    rope2 = _rope_tables(DEC_SEQ)

    cvec = jnp.zeros((N_SEQ_ROWS, D_MODEL), F32).at[0].set(c_ctx).at[1:1 + DEC_BATCH].set(c)
    mod_all = _ada(cvec, ada_w, ada_b).reshape(DEPTH, N_SEQ_ROWS, N_MOD, 1, D_MODEL)

    gate0 = 8 * GROUP_WIDTH
    w_in = in_proj.astype(BF16)
    w_main = jnp.concatenate([w_in[:, :, :gate0], w_in[:, :, gate0 + 2 * N_GATES:]], axis=-1)
    w_small = jnp.pad(w_in[:, :, gate0:gate0 + 2 * N_GATES], ((0, 0), (0, 0), (0, PROJ_SMALL - 2 * N_GATES)))
    w_out = out_proj.astype(BF16)
    w1, w3, w2 = ffn_w1.astype(BF16), ffn_w3.astype(BF16), ffn_w2.astype(BF16)

    x = jnp.concatenate([x_prompt.reshape(N_CTX_TOK, D_MODEL), x_sample.reshape(N_LAT_TOK, D_MODEL)], axis=0)
    hgrn_t = jnp.swapaxes(state_hgrn, -1, -2)
    s5_x0 = (state_s5_re.reshape(DEC_BATCH, DEPTH, DIRS, 1, S5_X), state_s5_im.reshape(DEC_BATCH, DEPTH, DIRS, 1, S5_X))
    lat_off = N_CTX_TOK // DEC_SEQ
    tables = jax.vmap(_s5_tables)(s5_a_re, s5_a_im, s5_log_step, s5_b_re, s5_b_im, s5_c_re, s5_c_im)
    ctx_states = []
    for i in range(DEPTH):
        mod = mod_all[i]
        pm, ps = _inproj(x, norm1_w[i][None], mod, w_main, w_small, i)
        ps_t = ps[:, :2 * N_GATES].T
        lg = jax.nn.log_sigmoid(ret_decay_logit[i])
        ctx = (BATCH, SEQ, 0)
        lat = (DEC_BATCH, DEC_SEQ, lat_off)

        ret_c, rs = _retention_pallas(pm, lg, *ctx, None, None, i)
        ret_l, _ = _retention_pallas(pm, lg, *lat, rope2, state_ret, i)
        gdn_args = (pm, ps, ps_t, gdn_conv[i], gdn_a_log[i], gdn_dt_bias[i], gdn_norm_w[i])
        gdn_c, gs = _gdn_pallas(*gdn_args, *ctx, None, i)
        gdn_l, _ = _gdn_pallas(*gdn_args, *lat, state_gdn, i)
        hg_c, hs = _gla_pallas(pm, lower_bounds[i], hg_norm_w[i], *ctx, None, i)
        hg_l, _ = _gla_pallas(pm, lower_bounds[i], hg_norm_w[i], *lat, hgrn_t, i)
        s5_args = (pm, tables, s5_d[i], s5_glu_w[i], s5_glu_b[i])
        s5_c, xr, xi = _s5_pallas(*s5_args, *ctx, None, i)
        s5_l, _, _ = _s5_pallas(*s5_args, *lat, s5_x0, i)
        ctx_states.append((rs, gs, jnp.swapaxes(hs, -1, -2), xr.reshape(BATCH, DIRS, S5_GROUPS, S5_N),
                           xi.reshape(BATCH, DIRS, S5_GROUPS, S5_N)))

        parts = ((ret_c, ret_l), (gdn_c, gdn_l), (hg_c, hg_l), (s5_c, s5_l))
        x = _outproj(parts, w_out, x, mod, i)
        x = _ffn(x, norm2_w[i][None], mod, w1, w3, w2, final_norm_w[None], i)

    y_prompt = x[0].reshape(BATCH, SEQ, D_MODEL)
    y_sample = x[1].reshape(DEC_BATCH, DEC_SEQ, D_MODEL)
    new_states = tuple(jnp.stack([s[j] for s in ctx_states], axis=1) for j in range(5))
    return (y_prompt, y_sample) + new_states
```

```python
import functools
import math

import jax
import jax.numpy as jnp
from jax import lax
from jax.experimental import pallas as pl
from jax.experimental.pallas import tpu as pltpu

F32 = jnp.float32
BF16 = jnp.bfloat16

D_MODEL = 2048
BATCH = 16
SEQ = 256
DEPTH = 2
DEC_BATCH = 8
DEC_SEQ = 1024
GRID_W = 64
HEAD_DIM = 128
GROUP_WIDTH = 512
N_HEADS = 4
S5_CH = 16
S5_GROUPS = 32
S5_N = 64
GDN_CONV = 3
ROPE_BASE = 10000.0
FFN_HIDDEN = 5632
N_MOD = 6
EPS = 1e-6
LB_FLOOR = 1e-30

N_CTX_TOK = BATCH * SEQ
N_LAT_TOK = DEC_BATCH * DEC_SEQ
N_TOK = N_CTX_TOK + N_LAT_TOK
N_SEQ_ROWS = 16
PROJ_MAIN = 14 * GROUP_WIDTH
PROJ_SMALL = 128
V7X_VMEM_BYTES = 64 * 1024 * 1024
VMEM_LIMIT = V7X_VMEM_BYTES - 8 * 1024 * 1024
ADA_TN = 1024
OUT_TN = 1024


def _seq_row(tile, tm):
    n_ctx = N_CTX_TOK // tm
    per_lat = DEC_SEQ // tm
    return jnp.where(tile < n_ctx, 0, 1 + (tile - n_ctx) // per_lat)


def _stream_specs(tm, width, n_col=1):
    n_ctx = N_CTX_TOK // tm

    def ctx_map(i, j):
        return jnp.minimum(i, n_ctx - 1), (jnp.where(i < n_ctx, j, n_col - 1) if n_col > 1 else 0)

    def lat_map(i, j):
        return jnp.maximum(i - n_ctx, 0), (jnp.where(i >= n_ctx, j, 0) if n_col > 1 else 0)

    return pl.BlockSpec((tm, width), ctx_map), pl.BlockSpec((tm, width), lat_map)


def _on_stream(tile, tm, fn):
    n_ctx = N_CTX_TOK // tm
    pl.when(tile < n_ctx)(functools.partial(fn, 0))
    pl.when(tile >= n_ctx)(functools.partial(fn, 1))


def _ada_body(c_ref, w_ref, b_ref, o_ref):
    cv = c_ref[...]
    s = cv * jax.nn.sigmoid(cv)
    o_ref[0] = jnp.dot(s.astype(BF16), w_ref[0].astype(BF16), preferred_element_type=F32) + b_ref[0]


def _ada(cvec, ada_w, ada_b):
    tn = ADA_TN
    n = N_MOD * D_MODEL
    return pl.pallas_call(
        _ada_body,
        out_shape=jax.ShapeDtypeStruct((DEPTH, N_SEQ_ROWS, n), F32),
        grid=(DEPTH, n // tn),
        in_specs=[
            pl.BlockSpec((N_SEQ_ROWS, D_MODEL), lambda l, j: (0, 0)),
            pl.BlockSpec((1, D_MODEL, tn), lambda l, j: (l, 0, j)),
            pl.BlockSpec((1, 1, tn), lambda l, j: (l, 0, j)),
        ],
        out_specs=pl.BlockSpec((1, N_SEQ_ROWS, tn), lambda l, j: (l, 0, j)),
        compiler_params=pltpu.CompilerParams(
            dimension_semantics=("parallel", "parallel"), vmem_limit_bytes=VMEM_LIMIT),
        name="ada_mod",
    )(cvec, ada_w, ada_b.reshape(DEPTH, 1, n))


def _norm_mod(x, nw, sc, sh):
    ms = jnp.mean(x * x, axis=-1, keepdims=True)
    y = x * lax.rsqrt(ms + EPS) * nw
    return y * (1.0 + sc) + sh


PROJ_TM = 1024
PROJ_TN = 1024


def _inproj_body(x_ref, nw_ref, sc_ref, sh_ref, w_ref, ws_ref, o_ref, os_ref, h_ref):
    @pl.when(pl.program_id(1) == 0)
    def _():
        hb = _norm_mod(x_ref[...], nw_ref[...], sc_ref[...], sh_ref[...]).astype(BF16)
        h_ref[...] = hb
        os_ref[...] = jnp.dot(hb, ws_ref[...], preferred_element_type=F32)

    o_ref[...] = jnp.dot(h_ref[...], w_ref[...], preferred_element_type=F32)


def _inproj(x, nw, mod, w_main, w_small, layer):
    tm, tn = PROJ_TM, PROJ_TN
    return pl.pallas_call(
        _inproj_body,
        out_shape=(jax.ShapeDtypeStruct((N_TOK, PROJ_MAIN), F32),
                   jax.ShapeDtypeStruct((N_TOK, PROJ_SMALL), F32)),
        grid=(N_TOK // tm, PROJ_MAIN // tn),
        in_specs=[
            pl.BlockSpec((tm, D_MODEL), lambda i, j: (i, 0)),
            pl.BlockSpec((1, D_MODEL), lambda i, j: (0, 0)),
            pl.BlockSpec((None, None, 1, D_MODEL), lambda i, j: (_seq_row(i, tm), 1, 0, 0)),
            pl.BlockSpec((None, None, 1, D_MODEL), lambda i, j: (_seq_row(i, tm), 0, 0, 0)),
            pl.BlockSpec((None, D_MODEL, tn), lambda i, j: (layer, 0, j)),
            pl.BlockSpec((None, D_MODEL, PROJ_SMALL), lambda i, j: (layer, 0, 0)),
        ],
        out_specs=(pl.BlockSpec((tm, tn), lambda i, j: (i, j)),
                   pl.BlockSpec((tm, PROJ_SMALL), lambda i, j: (i, 0))),
        scratch_shapes=[pltpu.VMEM((tm, D_MODEL), BF16)],
        compiler_params=pltpu.CompilerParams(
            dimension_semantics=("parallel", "arbitrary"), vmem_limit_bytes=VMEM_LIMIT),
        name="in_proj",
    )(x, nw, mod, mod, w_main, w_small)


def _outproj_body(*refs):
    n_mix = 4
    m_refs, (w_ref, x_ref, g_ref, o_ref) = refs[:2 * n_mix], refs[2 * n_mix:]

    def compute(stream):
        acc = None
        for p in range(n_mix):
            part = jnp.dot(m_refs[2 * p + stream][...], w_ref[p * GROUP_WIDTH:(p + 1) * GROUP_WIDTH, :],
                           preferred_element_type=F32)
            acc = part if acc is None else acc + part
        o_ref[...] = x_ref[...] + g_ref[...] * acc

    _on_stream(pl.program_id(0), PROJ_TM, compute)


def _outproj(parts, w, x, mod, layer):
    tm, tn = PROJ_TM, OUT_TN
    in_specs = []
    for _ in parts:
        in_specs += list(_stream_specs(tm, GROUP_WIDTH))
    in_specs += [pl.BlockSpec((None, D_MODEL, tn), lambda i, j: (layer, 0, j)),
                 pl.BlockSpec((tm, tn), lambda i, j: (i, j)),
                 pl.BlockSpec((None, None, 1, tn), lambda i, j: (_seq_row(i, tm), 2, 0, j))]
    return pl.pallas_call(
        _outproj_body,
        out_shape=jax.ShapeDtypeStruct((N_TOK, D_MODEL), F32),
        grid=(N_TOK // tm, D_MODEL // tn),
        in_specs=in_specs,
        out_specs=pl.BlockSpec((tm, tn), lambda i, j: (i, j)),
        compiler_params=pltpu.CompilerParams(
            dimension_semantics=("parallel", "arbitrary"), vmem_limit_bytes=VMEM_LIMIT),
        name="out_proj",
    )(*(a for pair in parts for a in pair), w, x, mod)


FFN_TM = 512
FFN_TH = 512


def _ffn_body(x_ref, nw_ref, sc_ref, sh_ref, g_ref, w1_ref, w3_ref, w2_ref, fw_ref, *rest, final_norm):
    o_refs, (h_ref, acc_ref) = rest[:-2], rest[-2:]
    i, k = pl.program_id(0), pl.program_id(1)

    @pl.when(k == 0)
    def _():
        h_ref[...] = _norm_mod(x_ref[...], nw_ref[...], sc_ref[...], sh_ref[...]).astype(BF16)
        acc_ref[...] = jnp.zeros_like(acc_ref)

    h = h_ref[...]
    a = jnp.dot(h, w1_ref[...], preferred_element_type=F32)
    b = jnp.dot(h, w3_ref[...], preferred_element_type=F32)
    g = (a * jax.nn.sigmoid(a) * b).astype(BF16)
    acc_ref[...] += jnp.dot(g, w2_ref[...], preferred_element_type=F32)

    @pl.when(k == pl.num_programs(1) - 1)
    def _():
        y = x_ref[...] + g_ref[...] * acc_ref[...]
        if not final_norm:
            o_refs[0][...] = y
        else:
            ms = jnp.mean(y * y, axis=-1, keepdims=True)
            y = y * lax.rsqrt(ms + EPS) * fw_ref[...]

            def write(stream):
                o_refs[stream][...] = y

            _on_stream(i, FFN_TM, write)


def _ffn(x, nw, mod, w1, w3, w2, fw, layer):
    final_norm = layer == DEPTH - 1
    tm, th = FFN_TM, FFN_TH
    if final_norm:
        out_shape = (jax.ShapeDtypeStruct((N_CTX_TOK, D_MODEL), F32), jax.ShapeDtypeStruct((N_LAT_TOK, D_MODEL), F32))
        out_specs = _stream_specs(tm, D_MODEL)
    else:
        out_shape = jax.ShapeDtypeStruct((N_TOK, D_MODEL), F32)
        out_specs = pl.BlockSpec((tm, D_MODEL), lambda i, k: (i, 0))
    return pl.pallas_call(
        functools.partial(_ffn_body, final_norm=final_norm),
        out_shape=out_shape,
        grid=(N_TOK // tm, FFN_HIDDEN // th),
        in_specs=[
            pl.BlockSpec((tm, D_MODEL), lambda i, k: (i, 0)),
            pl.BlockSpec((1, D_MODEL), lambda i, k: (0, 0)),
            pl.BlockSpec((None, None, 1, D_MODEL), lambda i, k: (_seq_row(i, tm), 4, 0, 0)),
            pl.BlockSpec((None, None, 1, D_MODEL), lambda i, k: (_seq_row(i, tm), 3, 0, 0)),
            pl.BlockSpec((None, None, 1, D_MODEL), lambda i, k: (_seq_row(i, tm), 5, 0, 0)),
            pl.BlockSpec((None, D_MODEL, th), lambda i, k: (layer, 0, k)),
            pl.BlockSpec((None, D_MODEL, th), lambda i, k: (layer, 0, k)),
            pl.BlockSpec((None, th, D_MODEL), lambda i, k: (layer, k, 0)),
            pl.BlockSpec((1, D_MODEL), lambda i, k: (0, 0)),
        ],
        out_specs=out_specs,
        scratch_shapes=[pltpu.VMEM((tm, D_MODEL), BF16), pltpu.VMEM((tm, D_MODEL), F32)],
        compiler_params=pltpu.CompilerParams(
            dimension_semantics=("arbitrary", "arbitrary"), vmem_limit_bytes=VMEM_LIMIT),
        name="ffn",
    )(x, nw, mod, mod, mod, w1, w3, w2, fw)


def _axial_rope(l):
    n_rows = l // GRID_W
    t_row = jnp.repeat(jnp.arange(n_rows, dtype=F32), GRID_W)
    t_col = jnp.tile(jnp.arange(GRID_W, dtype=F32), n_rows)
    n_freq = HEAD_DIM // 4
    inv = ROPE_BASE ** (-jnp.arange(n_freq, dtype=F32) / n_freq)
    ang = jnp.concatenate([t_row[:, None] * inv, t_col[:, None] * inv], axis=-1)
    return jnp.cos(ang), jnp.sin(ang)


MIX_CHUNK = 128
DIRS = 2
NT_DIMS = (((1,), (1,)), ((), ()))
TN_DIMS = (((0,), (0,)), ((), ()))


def _dot(a, b):
    return jnp.dot(a.astype(BF16), b.astype(BF16), preferred_element_type=F32)


def _dot_nt(a, b):
    return lax.dot_general(a.astype(BF16), b.astype(BF16), NT_DIMS, preferred_element_type=F32)


def _dot_tn(a, b):
    return lax.dot_general(a.astype(BF16), b.astype(BF16), TN_DIMS, preferred_element_type=F32)


def _silu(x):
    return x * jax.nn.sigmoid(x)


def _seq_spec(seq_len, width, row_off, col_blk, pipeline_mode=None):
    return pl.BlockSpec((seq_len, width), lambda b: (row_off + b, col_blk), pipeline_mode=pipeline_mode)


def _mix_params(n_par):
    return pltpu.CompilerParams(dimension_semantics=("parallel",) * n_par, vmem_limit_bytes=VMEM_LIMIT)


def _ret_body(*refs, n_chunks, use_rope, has_s0):
    it = iter(refs)
    lg_ref, q_ref, k_ref, v_ref, g_ref = (next(it) for _ in range(5))
    cos_ref, sin_ref = (next(it), next(it)) if use_rope else (None, None)
    s0_ref = next(it) if has_s0 else None
    o_ref, sf_ref, acc_ref, s_ref, intra_ref, qd_ref, kd_ref, cd_ref = (next(it) for _ in range(8))
    C, H = MIX_CHUNK, N_HEADS

    row = lax.broadcasted_iota(jnp.int32, (C, C), 0)
    col = lax.broadcasted_iota(jnp.int32, (C, C), 1)
    rel = (row - col).astype(F32)
    pos = lax.broadcasted_iota(jnp.int32, (C, HEAD_DIM), 0).astype(F32)
    for d in range(DIRS):
        for h in range(H):
            lg = lg_ref[d, h]
            if d == 0:
                intra_ref[d, h] = jnp.where(rel >= 0, jnp.exp(jnp.maximum(rel, 0.0) * lg), 0.0)
                qd_ref[d, h] = jnp.exp((pos + 1.0) * lg)
                kd_ref[d, h] = jnp.exp((C - 1.0 - pos) * lg)
            else:
                intra_ref[d, h] = jnp.where(rel <= 0, jnp.exp(jnp.maximum(-rel, 0.0) * lg), 0.0)
                qd_ref[d, h] = jnp.exp((C - pos) * lg)
                kd_ref[d, h] = jnp.exp(pos * lg)
            cd_ref[d, h] = jnp.exp(jnp.full((C, HEAD_DIM), C, F32) * lg)
    if has_s0:
        s_ref[...] = s0_ref[...]
    else:
        s_ref[...] = jnp.zeros_like(s_ref)
    acc_ref[...] = jnp.zeros_like(acc_ref)

    units = [(d, h) for d in range(DIRS) for h in range(H)]

    def chunk_step(n, carry):
        rows_d = [pl.ds(pl.multiple_of(c * C, C), C) for c in (n, n_chunks - 1 - n)]
        q_l, k_l, v_l, sc_l = [], [], [], []
        for d, h in units:
            rows, cols = rows_d[d], slice(h * HEAD_DIM, (h + 1) * HEAD_DIM)
            q, k = q_ref[rows, cols], k_ref[rows, cols]
            if use_rope:
                cs, sn = cos_ref[rows, :], sin_ref[rows, :]
                q = q * cs + pltpu.roll(q, HEAD_DIM // 2, 1) * sn
                k = k * cs + pltpu.roll(k, HEAD_DIM // 2, 1) * sn
            k = k * HEAD_DIM ** -0.5
            q_l.append(q)
            k_l.append(k)
            v_l.append(v_ref[rows, cols].astype(BF16))
            sc_l.append(_dot_nt(q, k) * intra_ref[d, h])
        for i, (d, h) in enumerate(units):
            cols = slice(h * HEAD_DIM, (h + 1) * HEAD_DIM)
            s = s_ref[d, h]
            acc_ref[rows_d[d], cols] += _dot(jnp.concatenate([sc_l[i], q_l[i] * qd_ref[d, h]], axis=1),
                                             jnp.concatenate([v_l[i], s.astype(BF16)], axis=0))
            s_ref[d, h] = cd_ref[d, h] * s + _dot_tn(k_l[i] * kd_ref[d, h], v_l[i])
        return carry

    lax.fori_loop(0, n_chunks, chunk_step, 0)
    sf_ref[...] = s_ref[...]

    def finish(n, carry):
        rows = pl.ds(pl.multiple_of(n * C, C), C)
        for h in range(H):
            cols = slice(h * HEAD_DIM, (h + 1) * HEAD_DIM)
            o = acc_ref[rows, cols]
            mu = jnp.mean(o, axis=-1, keepdims=True)
            oc = o - mu
            y = oc * lax.rsqrt(jnp.mean(oc * oc, axis=-1, keepdims=True) + EPS)
            o_ref[rows, cols] = (y * _silu(g_ref[rows, cols])).astype(BF16)
        return carry

    lax.fori_loop(0, n_chunks, finish, 0)


def _retention_pallas(pm, log_gamma, n_seq, seq_len, row_off, rope2, s0, layer):
    use_rope, has_s0 = rope2 is not None, s0 is not None
    st_shape = (DIRS, N_HEADS, HEAD_DIM, HEAD_DIM)
    in_specs = [pl.BlockSpec(memory_space=pltpu.SMEM)]
    in_specs += [_seq_spec(seq_len, GROUP_WIDTH, row_off, cb) for cb in range(4)]
    args = [log_gamma, pm, pm, pm, pm]
    if use_rope:
        in_specs += [pl.BlockSpec((seq_len, HEAD_DIM), lambda b: (0, 0))] * 2
        args += list(rope2)
    if has_s0:
        in_specs.append(pl.BlockSpec((None, None) + st_shape, lambda b: (b, layer, 0, 0, 0, 0)))
        args.append(s0)
    return pl.pallas_call(
        functools.partial(_ret_body, n_chunks=seq_len // MIX_CHUNK, use_rope=use_rope, has_s0=has_s0),
        out_shape=(jax.ShapeDtypeStruct((n_seq * seq_len, GROUP_WIDTH), BF16),
                   jax.ShapeDtypeStruct((n_seq,) + st_shape, F32)),
        grid=(n_seq,),
        in_specs=in_specs,
        out_specs=(pl.BlockSpec((seq_len, GROUP_WIDTH), lambda b: (b, 0)),
                   pl.BlockSpec((None,) + st_shape, lambda b: (b, 0, 0, 0, 0))),
        scratch_shapes=[pltpu.VMEM((seq_len, GROUP_WIDTH), F32), pltpu.VMEM(st_shape, F32),
                        pltpu.VMEM((DIRS, N_HEADS, MIX_CHUNK, MIX_CHUNK), F32),
                        pltpu.VMEM((DIRS, N_HEADS, MIX_CHUNK, HEAD_DIM), F32),
                        pltpu.VMEM((DIRS, N_HEADS, MIX_CHUNK, HEAD_DIM), F32),
                        pltpu.VMEM((DIRS, N_HEADS, MIX_CHUNK, HEAD_DIM), F32)],
        compiler_params=_mix_params(1),
        name="retention",
    )(*args)


def _rope_tables(l):
    cos, sin = _axial_rope(l)
    return jnp.concatenate([cos, cos], axis=-1), jnp.concatenate([-sin, sin], axis=-1)


S5_HALF_G = S5_GROUPS // 2
S5_HALF_U = S5_HALF_G * S5_CH
S5_HALF_X = S5_HALF_G * S5_N
S5_X = S5_GROUPS * S5_N
S5_TC = 512
S5_BLK = 8
S5_TABS = 4


def _s5_tables(a_re, a_im, log_step, b_re, b_im, c_re, c_im):
    dt = jnp.exp(log_step)[..., None]
    mag = jnp.exp(a_re * dt)
    ab_re = mag * jnp.cos(a_im * dt)
    ab_im = mag * jnp.sin(a_im * dt)
    den = a_re * a_re + a_im * a_im
    nr = ab_re - 1.0
    f_re = (nr * a_re + ab_im * a_im) / den
    f_im = (ab_im * a_re - nr * a_im) / den
    bb_re = f_re[..., None] * b_re - f_im[..., None] * b_im
    bb_im = f_re[..., None] * b_im + f_im[..., None] * b_re
    eye = jnp.eye(S5_HALF_G, dtype=F32)

    def in_mat(bb):
        bb = bb.reshape(DIRS, 2, S5_HALF_G, S5_N, S5_CH)
        return jnp.einsum('dhgnc,gk->dhgckn', bb, eye).reshape(DIRS, 2, S5_HALF_U, S5_HALF_X)

    def out_mat(cc):
        cc = cc.reshape(DIRS, 2, S5_HALF_G, S5_CH, S5_N)
        return jnp.einsum('dhgcn,gk->dhgnkc', cc, eye).reshape(DIRS, 2, S5_HALF_X, S5_HALF_U)

    bm = jnp.concatenate([in_mat(bb_re), in_mat(bb_im)], axis=-1).astype(BF16)
    cm = jnp.concatenate([out_mat(c_re), -out_mat(c_im)], axis=-2).astype(BF16)
    t = jnp.arange(S5_BLK, dtype=F32)
    order = jnp.stack([t, S5_BLK - 1.0 - t])
    shifts = 2.0 ** jnp.arange(S5_TABS - 1, dtype=F32)
    expo = jnp.concatenate([jnp.where(order[:, None, :] >= shifts[None, :, None], shifts[None, :, None], jnp.nan),
                            order[:, None, :] + 1.0], axis=1)
    live = ~jnp.isnan(expo)
    e = jnp.where(live, expo, 0.0)[..., None]
    adt_re = (a_re * dt).reshape(DIRS, 1, 1, S5_X)
    adt_im = (a_im * dt).reshape(DIRS, 1, 1, S5_X)
    pmag = jnp.where(live[..., None], jnp.exp(e * adt_re), 0.0)
    pw_re = pmag * jnp.cos(e * adt_im)
    pw_im = pmag * jnp.sin(e * adt_im)
    return bm, cm, pw_re, pw_im


def _gelu_tanh(x):
    return 0.5 * x * (1.0 + jnp.tanh(math.sqrt(2.0 / math.pi) * (x + 0.044715 * (x * x * x))))


def _s5_body(*refs, seq_len, has_s0):
    it = iter(refs)
    u_ref, bm_ref, cm_ref, pwr_ref, pwi_ref, d_ref, gw_ref, gb_ref = (next(it) for _ in range(8))
    x0r_ref, x0i_ref = (next(it), next(it)) if has_s0 else (None, None)
    o_ref, sfr_ref, sfi_ref, y_ref, xr_ref, xi_ref, xb_ref = (next(it) for _ in range(7))
    tc = min(seq_len, S5_TC)
    n_tiles = seq_len // tc
    n_pair = tc // (2 * S5_BLK)

    y_ref[...] = d_ref[...] * u_ref[...]
    unit = 0
    for d in range(DIRS):
        last = S5_BLK - 1 if d == 0 else 0
        for hf in range(2):
            xs = slice(hf * S5_HALF_X, (hf + 1) * S5_HALF_X)
            us = slice(hf * S5_HALF_U, (hf + 1) * S5_HALF_U)

            def scan_block(x_in, carry, d=d, xs=xs, last=last):
                xr, xi = x_in
                car_re, car_im = carry
                for k in range(S5_TABS - 1):
                    s = 1 << k
                    shift = s if d == 0 else S5_BLK - s
                    p_re, p_im = pwr_ref[d, k, :, xs], pwi_ref[d, k, :, xs]
                    sr, si = pltpu.roll(xr, shift, 0), pltpu.roll(xi, shift, 0)
                    xr, xi = xr + p_re * sr - p_im * si, xi + p_re * si + p_im * sr
                p_re, p_im = pwr_ref[d, S5_TABS - 1, :, xs], pwi_ref[d, S5_TABS - 1, :, xs]
                xr, xi = xr + p_re * car_re - p_im * car_im, xi + p_re * car_im + p_im * car_re
                return (xr, xi), (xr[last:last + 1, :], xi[last:last + 1, :])

            if has_s0:
                carry = (x0r_ref[d, :, xs], x0i_ref[d, :, xs])
            else:
                carry = (jnp.zeros((1, S5_HALF_X), F32), jnp.zeros((1, S5_HALF_X), F32))
            for i in range(n_tiles):
                slot = unit % 2
                unit += 1
                tile = i if d == 0 else n_tiles - 1 - i
                rows_t = slice(tile * tc, (tile + 1) * tc)
                bu = _dot(u_ref[rows_t, us], bm_ref[d, hf])
                xr_ref[slot] = bu[:, :S5_HALF_X]
                xi_ref[slot] = bu[:, S5_HALF_X:]
                for j in range(n_pair):
                    pair = j if d == 0 else n_pair - 1 - j
                    rows = slice(pair * 2 * S5_BLK, (pair + 1) * 2 * S5_BLK)
                    xr2, xi2 = xr_ref[slot, rows, :], xi_ref[slot, rows, :]
                    halves = [(xr2[:S5_BLK], xi2[:S5_BLK]), (xr2[S5_BLK:], xi2[S5_BLK:])]
                    out = [None, None]
                    for idx in ((0, 1) if d == 0 else (1, 0)):
                        out[idx], carry = scan_block(halves[idx], carry)
                    xb_ref[slot, rows, :S5_HALF_X] = jnp.concatenate([out[0][0], out[1][0]], axis=0).astype(BF16)
                    xb_ref[slot, rows, S5_HALF_X:] = jnp.concatenate([out[0][1], out[1][1]], axis=0).astype(BF16)
                y_ref[rows_t, us] += jnp.dot(xb_ref[slot], cm_ref[d, hf], preferred_element_type=F32)
            sfr_ref[d, :, xs] = carry[0]
            sfi_ref[d, :, xs] = carry[1]

    z = _gelu_tanh(y_ref[...])
    o_ref[...] = (z * jax.nn.sigmoid(_dot(z, gw_ref[...]) + gb_ref[...])).astype(BF16)


def _s5_pallas(pm, tables, s5_d, glu_w, glu_b, n_seq, seq_len, row_off, x0, layer):
    bm, cm, pw_re, pw_im = tables
    has_s0 = x0 is not None
    full = lambda shape: pl.BlockSpec(shape, lambda b: (0,) * len(shape))
    of_layer = lambda t: pl.BlockSpec((None,) + t.shape[1:], lambda b: (layer,) + (0,) * (t.ndim - 1))
    in_specs = [_seq_spec(seq_len, GROUP_WIDTH, row_off, 13),
                of_layer(bm), of_layer(cm), of_layer(pw_re), of_layer(pw_im),
                full((1, GROUP_WIDTH)), full((GROUP_WIDTH, GROUP_WIDTH)), full((1, GROUP_WIDTH))]
    args = [pm, bm, cm, pw_re, pw_im, s5_d.reshape(1, GROUP_WIDTH), glu_w.astype(BF16),
            glu_b.reshape(1, GROUP_WIDTH)]
    if has_s0:
        in_specs += [pl.BlockSpec((None, None, DIRS, 1, S5_X), lambda b: (b, layer, 0, 0, 0))] * 2
        args += list(x0)
    st = jax.ShapeDtypeStruct((n_seq, DIRS, 1, S5_X), F32)
    st_spec = pl.BlockSpec((None, DIRS, 1, S5_X), lambda b: (b, 0, 0, 0))
    tc = min(seq_len, S5_TC)
    return pl.pallas_call(
        functools.partial(_s5_body, seq_len=seq_len, has_s0=has_s0),
        out_shape=(jax.ShapeDtypeStruct((n_seq * seq_len, GROUP_WIDTH), BF16), st, st),
        grid=(n_seq,),
        in_specs=in_specs,
        out_specs=(pl.BlockSpec((seq_len, GROUP_WIDTH), lambda b: (b, 0)), st_spec, st_spec),
        scratch_shapes=[pltpu.VMEM((seq_len, GROUP_WIDTH), F32),
                        pltpu.VMEM((2, tc, S5_HALF_X), F32), pltpu.VMEM((2, tc, S5_HALF_X), F32),
                        pltpu.VMEM((2, tc, 2 * S5_HALF_X), BF16)],
        compiler_params=_mix_params(1),
        name="s5",
    )(*args)


GLA_LEVELS = 7


def _chunk_cumsum_rows(x, rowi):
    s = 1
    while s < MIX_CHUNK:
        x = x + jnp.where(rowi >= s, pltpu.roll(x, s, 0), 0.0)
        s *= 2
    return x


def _gla_body(*refs, n_chunks, has_s0):
    it = iter(refs)
    q_ref, f0_ref, f1_ref, i_ref, g_ref, lb_ref, nw_ref = (next(it) for _ in range(7))
    s0_ref = next(it) if has_s0 else None
    o_ref, sf_ref, acc_ref, s_ref, code_ref = (next(it) for _ in range(5))
    C, H = MIX_CHUNK, N_HEADS
    f_refs = (f0_ref, f1_ref)

    rowi = lax.broadcasted_iota(jnp.int32, (C, HEAD_DIM), 0)
    ri = lax.broadcasted_iota(jnp.int32, (C, C), 0)
    ci = lax.broadcasted_iota(jnp.int32, (C, C), 1)
    top_bit = 31 - lax.clz(ri ^ ci)
    code_ref[...] = jnp.where(ri > ci, top_bit, jnp.where(ri < ci, -1 - top_bit, GLA_LEVELS))
    if has_s0:
        s_ref[...] = s0_ref[...]
    else:
        s_ref[...] = jnp.zeros_like(s_ref)
    acc_ref[...] = jnp.zeros_like(acc_ref)

    def chunk_step(n, carry):
        for d in range(DIRS):
            c = n if d == 0 else n_chunks - 1 - n
            rows = pl.ds(pl.multiple_of(c * C, C), C)
            for h in range(H):
                cols = slice(h * HEAD_DIM, (h + 1) * HEAD_DIM)
                code = code_ref[...]
                q = _silu(q_ref[rows, cols]) * HEAD_DIM ** -0.5
                v = i_ref[rows, cols].astype(BF16)
                fx = f_refs[d][rows, cols]
                lb = lb_ref[d:d + 1, cols]
                sig = 1.0 / (1.0 + jnp.exp(-fx))
                logf = jnp.log2(jnp.maximum(lb, LB_FLOOR) + (1.0 - lb) * sig)
                k = (1.0 - lb) * (1.0 - sig)
                cum = _chunk_cumsum_rows(logf, rowi)
                own = cum
                attn = jnp.where(code == GLA_LEVELS, _dot_nt(q, k), 0.0)
                for lvl in range(GLA_LEVELS):
                    m = 1 << lvl
                    prev = pltpu.roll(own, m, 0)
                    pre = jnp.minimum(cum - prev, 0.0)
                    suf = own - cum
                    if d == 0:
                        sc = _dot_nt(q * jnp.exp2(pre), k * jnp.exp2(suf))
                        hit = code == lvl
                    else:
                        sc = _dot_nt(q * jnp.exp2(suf + logf), k * jnp.exp2(jnp.minimum(pre - logf, 0.0)))
                        hit = code == -1 - lvl
                    attn = jnp.where(hit, sc, attn)
                    own = jnp.where(((rowi >> lvl) & 1) == 0, pltpu.roll(own, C - m, 0), own)
                tot = own
                st = s_ref[d, h]
                if d == 0:
                    q_in, k_out = q * jnp.exp2(cum), k * jnp.exp2(tot - cum)
                else:
                    q_in, k_out = q * jnp.exp2(tot - cum + logf), k * jnp.exp2(cum - logf)
                acc_ref[rows, cols] += _dot(attn, v) + _dot_nt(q_in, st)
                s_ref[d, h] = jnp.exp2(tot[0:1, :]) * st + _dot_tn(v, k_out)
        return carry

    lax.fori_loop(0, n_chunks, chunk_step, 0)
    sf_ref[...] = s_ref[...]

    def finish(n, carry):
        rows = pl.ds(pl.multiple_of(n * C, C), C)
        for h in range(H):
            cols = slice(h * HEAD_DIM, (h + 1) * HEAD_DIM)
            o = acc_ref[rows, cols]
            y = o * lax.rsqrt(jnp.mean(o * o, axis=-1, keepdims=True) + EPS) * nw_ref[...]
            o_ref[rows, cols] = (y * _silu(g_ref[rows, cols])).astype(BF16)
        return carry

    lax.fori_loop(0, n_chunks, finish, 0)


def _gla_pallas(pm, lower_bound, norm_w, n_seq, seq_len, row_off, s0_t, layer):
    has_s0 = s0_t is not None
    st_shape = (DIRS, N_HEADS, HEAD_DIM, HEAD_DIM)
    in_specs = [_seq_spec(seq_len, GROUP_WIDTH, row_off, cb) for cb in (8, 9, 10, 11, 12)]
    in_specs += [pl.BlockSpec((DIRS, GROUP_WIDTH), lambda b: (0, 0)), pl.BlockSpec((1, HEAD_DIM), lambda b: (0, 0))]
    args = [pm] * 5 + [lower_bound, norm_w.reshape(1, HEAD_DIM)]
    if has_s0:
        in_specs.append(pl.BlockSpec((None, None) + st_shape, lambda b: (b, layer, 0, 0, 0, 0)))
        args.append(s0_t)
    return pl.pallas_call(
        functools.partial(_gla_body, n_chunks=seq_len // MIX_CHUNK, has_s0=has_s0),
        out_shape=(jax.ShapeDtypeStruct((n_seq * seq_len, GROUP_WIDTH), BF16),
                   jax.ShapeDtypeStruct((n_seq,) + st_shape, F32)),
        grid=(n_seq,),
        in_specs=in_specs,
        out_specs=(pl.BlockSpec((seq_len, GROUP_WIDTH), lambda b: (b, 0)),
                   pl.BlockSpec((None,) + st_shape, lambda b: (b, 0, 0, 0, 0))),
        scratch_shapes=[pltpu.VMEM((seq_len, GROUP_WIDTH), F32), pltpu.VMEM(st_shape, F32),
                        pltpu.VMEM((MIX_CHUNK, MIX_CHUNK), jnp.int32)],
        compiler_params=_mix_params(1),
        name="hgrn2",
    )(*args)


GDN_BLOCK_BITS = 4
GDN_MERGES = 3
N_GATES = DIRS * N_HEADS


def _softplus(x):
    return jnp.maximum(x, 0.0) + jnp.log1p(jnp.exp(-jnp.abs(x)))


def _gdn_body(*refs, seq_len, n_sub, has_s0):
    it = iter(refs)
    (q_ref, k_ref, v_ref, g_ref, ps_ref, pst_ref, cw_ref, prow_ref, pcol_ref, nw_ref) = (next(it) for _ in range(10))
    s0_ref = next(it) if has_s0 else None
    (o_ref, sf_ref, acc_ref, s_ref, qn_ref, kn_ref, vn_ref, gate_ref, gatet_ref, blev_ref) = (
        next(it) for _ in range(10))
    C, H, L = MIX_CHUNK, N_HEADS, seq_len
    R = n_sub * L
    n_chunks = L // C
    w = GROUP_WIDTH

    pos = lax.broadcasted_iota(jnp.int32, (R, HEAD_DIM), 0) & (L - 1)
    for part, (src, dst) in enumerate(((q_ref, qn_ref), (k_ref, kn_ref), (v_ref, vn_ref))):
        for h in range(H):
            cols = slice(h * HEAD_DIM, (h + 1) * HEAD_DIM)
            wc = slice(part * w + h * HEAD_DIM, part * w + (h + 1) * HEAD_DIM)
            x = src[:, cols]
            x_prev = jnp.where(pos >= 1, pltpu.roll(x, 1, 0), 0.0)
            x_next = jnp.where(pos < L - 1, pltpu.roll(x, R - 1, 0), 0.0)
            y = _silu(cw_ref[0:1, wc] * x_prev + cw_ref[1:2, wc] * x + cw_ref[2:3, wc] * x_next)
            if part < 2:
                y = y * lax.rsqrt(jnp.sum(y * y, axis=-1, keepdims=True) + EPS)
            if part == 0:
                y = y * HEAD_DIM ** -0.5
            dst[:, cols] = y

    rowi = lax.broadcasted_iota(jnp.int32, (C, HEAD_DIM), 0)
    lane = lax.broadcasted_iota(jnp.int32, (C, HEAD_DIM), 1)
    lane_t = lax.broadcasted_iota(jnp.int32, (2 * N_GATES, C), 1)
    row_t = lax.broadcasted_iota(jnp.int32, (2 * N_GATES, C), 0)
    for c in range(R // C):
        rows = slice(c * C, (c + 1) * C)
        a = ps_ref[rows, :]
        la = -jnp.exp(prow_ref[0:1, :]) * _softplus(a + prow_ref[1:2, :])
        pre = _chunk_cumsum_rows(la, rowi)
        tot = jnp.broadcast_to(pre[C - 1:C, :], (C, HEAD_DIM))
        g = jnp.where(lane < N_HEADS, pre, tot - pre + la)
        gate_ref[rows, :] = jnp.where(lane < N_GATES, g,
                                      jnp.where(lane < 2 * N_GATES, jax.nn.sigmoid(a),
                                                pltpu.roll(tot, 2 * N_GATES, 1)))
        at = pst_ref[:, rows]
        lat = -jnp.exp(pcol_ref[:, 0:1]) * _softplus(at + pcol_ref[:, 1:2])
        pre_t, s = lat, 1
        while s < C:
            pre_t = pre_t + jnp.where(lane_t >= s, pltpu.roll(pre_t, s, 1), 0.0)
            s *= 2
        tot_t = jnp.broadcast_to(pre_t[:, C - 1:C], (2 * N_GATES, C))
        g_t = jnp.where(row_t < N_HEADS, pre_t, tot_t - pre_t + lat)
        gatet_ref[:, rows] = jnp.where(row_t < N_GATES, g_t, jax.nn.sigmoid(at))

    ri = lax.broadcasted_iota(jnp.int32, (C, C), 0)
    ci = lax.broadcasted_iota(jnp.int32, (C, C), 1)
    bx = (ri >> GDN_BLOCK_BITS) ^ (ci >> GDN_BLOCK_BITS)
    blev_ref[...] = jnp.where(bx == 0, 0, 32 - lax.clz(bx))
    if has_s0:
        for s in range(n_sub):
            s_ref[s * DIRS:(s + 1) * DIRS] = s0_ref[s]
    else:
        s_ref[...] = jnp.zeros_like(s_ref)
    acc_ref[...] = jnp.zeros_like(acc_ref)

    units = [(s, d, h) for s in range(n_sub) for d in range(DIRS) for h in range(H)]

    def chunk_step(n, carry):
        eye = jnp.where(ri == ci, 1.0, 0.0)
        blev = blev_ref[...]
        rows_sd = [[pl.ds(pl.multiple_of(s * L + c * C, C), C) for c in (n, n_chunks - 1 - n)] for s in range(n_sub)]
        incl_d = [ri >= ci, ri <= ci]
        strict_d = [ri > ci, ri < ci]
        q_l, et_l, ek_l, rhs_l, a_l, attn_l = ([] for _ in range(6))
        for s, d, h in units:
            rows, cols, cg = rows_sd[s][d], slice(h * HEAD_DIM, (h + 1) * HEAD_DIM), d * N_HEADS + h
            q, k, v = qn_ref[rows, cols], kn_ref[rows, cols], vn_ref[rows, cols]
            g_i = jnp.broadcast_to(gate_ref[rows, cg:cg + 1], (C, C))
            b_i = jnp.broadcast_to(gate_ref[rows, N_GATES + cg:N_GATES + cg + 1], (C, C))
            tot = jnp.broadcast_to(gate_ref[rows, 2 * N_GATES + cg:2 * N_GATES + cg + 1], (C, C))
            g_j = gatet_ref[cg:cg + 1, rows]
            decay = jnp.where(incl_d[d], jnp.exp(jnp.minimum(g_i - g_j, 0.0)), 0.0)
            e_g = jnp.exp(g_i)
            kb = k.astype(BF16)
            a_l.append(jnp.where(strict_d[d], _dot_nt(kb, kb) * b_i * decay, 0.0))
            attn_l.append(_dot_nt(q, kb) * decay)
            rhs_l.append(jnp.concatenate([v * b_i, k * (b_i * e_g)], axis=1).astype(BF16))
            q_l.append(q * e_g)
            ek_l.append(k * jnp.exp(tot - g_i))
            et_l.append(jnp.exp(tot))
        b_l = [-jnp.where(blev == 0, a, 0.0) for a in a_l]
        p_l = [eye + b for b in b_l]
        for _ in range(GDN_BLOCK_BITS - 1):
            b_l = [_dot(b, b) for b in b_l]
            p_l = [p + _dot(p, b) for p, b in zip(p_l, b_l)]
        for lvl in range(1, GDN_MERGES + 1):
            ep_l = [_dot(jnp.where(blev == lvl, a, 0.0), p) for a, p in zip(a_l, p_l)]
            p_l = [p - _dot(p, ep) for p, ep in zip(p_l, ep_l)]
        uw_l = [_dot(p, rhs) for p, rhs in zip(p_l, rhs_l)]
        s_l = [s_ref[s * DIRS + d, h] for s, d, h in units]
        vn_l = [uw[:, :HEAD_DIM] - _dot(uw[:, HEAD_DIM:], st) for uw, st in zip(uw_l, s_l)]
        for i, (s, d, h) in enumerate(units):
            cols = slice(h * HEAD_DIM, (h + 1) * HEAD_DIM)
            acc_ref[rows_sd[s][d], cols] += _dot(jnp.concatenate([q_l[i], attn_l[i]], axis=1),
                                                 jnp.concatenate([s_l[i], vn_l[i]], axis=0))
            s_ref[s * DIRS + d, h] = s_l[i] * et_l[i] + _dot_tn(ek_l[i], vn_l[i])
        return carry

    lax.fori_loop(0, n_chunks, chunk_step, 0)
    sf_ref[...] = s_ref[...]

    def finish(n, carry):
        rows = pl.ds(pl.multiple_of(n * C, C), C)
        for h in range(H):
            cols = slice(h * HEAD_DIM, (h + 1) * HEAD_DIM)
            o = acc_ref[rows, cols]
            y = o * lax.rsqrt(jnp.mean(o * o, axis=-1, keepdims=True) + EPS) * nw_ref[...]
            o_ref[rows, cols] = (y * _silu(g_ref[rows, cols])).astype(BF16)
        return carry

    lax.fori_loop(0, R // C, finish, 0)


GDN_CTX_SUB = 2
GDN_LAT_SUB = 2
GDN_DOUBLE_BUFFER_BYTES = 2 * 1024 * 1024


def _gdn_pallas(pm, ps, ps_t, conv_w, a_log, dt_bias, norm_w, n_seq, seq_len, row_off, s0, layer):
    has_s0 = s0 is not None
    n_sub = GDN_LAT_SUB if has_s0 else GDN_CTX_SUB
    n_seq, seq_len, row_off = n_seq // n_sub, seq_len * n_sub, row_off // n_sub
    st_shape = (n_sub * DIRS, N_HEADS, HEAD_DIM, HEAD_DIM)
    par = jnp.stack([a_log.reshape(N_GATES), dt_bias.reshape(N_GATES)])
    par_row = jnp.pad(par, ((0, 0), (0, HEAD_DIM - N_GATES)))
    par_col = jnp.pad(par.T, ((0, N_GATES), (0, 0)))
    full = lambda shape: pl.BlockSpec(shape, lambda b: (0,) * len(shape))
    big_mode = pl.Buffered(1) if seq_len * GROUP_WIDTH * 4 > GDN_DOUBLE_BUFFER_BYTES else None
    in_specs = [_seq_spec(seq_len, GROUP_WIDTH, row_off, cb, big_mode) for cb in (4, 5, 6, 7)]
    in_specs += [_seq_spec(seq_len, PROJ_SMALL, row_off, 0),
                 pl.BlockSpec((2 * N_GATES, seq_len), lambda b: (0, row_off + b)),
                 full((GDN_CONV, 3 * GROUP_WIDTH)), full((2, HEAD_DIM)), full((2 * N_GATES, 2)), full((1, HEAD_DIM))]
    args = [pm] * 4 + [ps, ps_t, conv_w, par_row, par_col, norm_w.reshape(1, HEAD_DIM)]
    if has_s0:
        in_specs.append(pl.BlockSpec((n_sub, None, DIRS, N_HEADS, HEAD_DIM, HEAD_DIM),
                                     lambda b: (b, layer, 0, 0, 0, 0)))
        args.append(s0)
    seq_f32 = lambda width: pltpu.VMEM((seq_len, width), F32)
    out, s_fin = pl.pallas_call(
        functools.partial(_gdn_body, seq_len=seq_len // n_sub, n_sub=n_sub, has_s0=has_s0),
        out_shape=(jax.ShapeDtypeStruct((n_seq * seq_len, GROUP_WIDTH), BF16),
                   jax.ShapeDtypeStruct((n_seq,) + st_shape, F32)),
        grid=(n_seq,),
        in_specs=in_specs,
        out_specs=(pl.BlockSpec((seq_len, GROUP_WIDTH), lambda b: (b, 0)),
                   pl.BlockSpec((None,) + st_shape, lambda b: (b, 0, 0, 0, 0))),
        scratch_shapes=[seq_f32(GROUP_WIDTH), pltpu.VMEM(st_shape, F32),
                        seq_f32(GROUP_WIDTH), seq_f32(GROUP_WIDTH), seq_f32(GROUP_WIDTH),
                        seq_f32(HEAD_DIM), pltpu.VMEM((2 * N_GATES, seq_len), F32),
                        pltpu.VMEM((MIX_CHUNK, MIX_CHUNK), jnp.int32)],
        compiler_params=_mix_params(1),
        name="gated_delta",
    )(*args)
    return out, s_fin.reshape(n_seq * n_sub, DIRS, N_HEADS, HEAD_DIM, HEAD_DIM)


def kernel(x_prompt, x_sample, state_ret, state_gdn, state_hgrn, state_s5_re, state_s5_im, c, c_ctx, norm1_w, norm2_w, final_norm_w, ada_w, ada_b, in_proj, out_proj, ret_decay_logit, gdn_conv, gdn_a_log, gdn_dt_bias, gdn_norm_w, hg_lb_param, hg_norm_w, s5_a_re, s5_a_im, s5_b_re, s5_b_im, s5_c_re, s5_c_im, s5_log_step, s5_d, s5_glu_w, s5_glu_b, ffn_w1, ffn_w3, ffn_w2):
    lb_soft = jax.nn.softmax(hg_lb_param, axis=0)
    lower_bounds = jnp.cumsum(lb_soft, axis=0) - lb_soft[0]
    rope2 = _rope_tables(DEC_SEQ)

    cvec = jnp.zeros((N_SEQ_ROWS, D_MODEL), F32).at[0].set(c_ctx).at[1:1 + DEC_BATCH].set(c)
    mod_all = _ada(cvec, ada_w, ada_b).reshape(DEPTH, N_SEQ_ROWS, N_MOD, 1, D_MODEL)

    gate0 = 8 * GROUP_WIDTH
    w_in = in_proj.astype(BF16)
    w_main = jnp.concatenate([w_in[:, :, :gate0], w_in[:, :, gate0 + 2 * N_GATES:]], axis=-1)
    w_small = jnp.pad(w_in[:, :, gate0:gate0 + 2 * N_GATES], ((0, 0), (0, 0), (0, PROJ_SMALL - 2 * N_GATES)))
    w_out = out_proj.astype(BF16)
    w1, w3, w2 = ffn_w1.astype(BF16), ffn_w3.astype(BF16), ffn_w2.astype(BF16)

    x = jnp.concatenate([x_prompt.reshape(N_CTX_TOK, D_MODEL), x_sample.reshape(N_LAT_TOK, D_MODEL)], axis=0)
    hgrn_t = jnp.swapaxes(state_hgrn, -1, -2)
    s5_x0 = (state_s5_re.reshape(DEC_BATCH, DEPTH, DIRS, 1, S5_X), state_s5_im.reshape(DEC_BATCH, DEPTH, DIRS, 1, S5_X))
    lat_off = N_CTX_TOK // DEC_SEQ
    tables = jax.vmap(_s5_tables)(s5_a_re, s5_a_im, s5_log_step, s5_b_re, s5_b_im, s5_c_re, s5_c_im)
    ctx_states = []
    for i in range(DEPTH):
        mod = mod_all[i]
        pm, ps = _inproj(x, norm1_w[i][None], mod, w_main, w_small, i)
        ps_t = ps[:, :2 * N_GATES].T
        lg = jax.nn.log_sigmoid(ret_decay_logit[i])
        ctx = (BATCH, SEQ, 0)
        lat = (DEC_BATCH, DEC_SEQ, lat_off)

        ret_c, rs = _retention_pallas(pm, lg, *ctx, None, None, i)
        ret_l, _ = _retention_pallas(pm, lg, *lat, rope2, state_ret, i)
        gdn_args = (pm, ps, ps_t, gdn_conv[i], gdn_a_log[i], gdn_dt_bias[i], gdn_norm_w[i])
        gdn_c, gs = _gdn_pallas(*gdn_args, *ctx, None, i)
        gdn_l, _ = _gdn_pallas(*gdn_args, *lat, state_gdn, i)
        hg_c, hs = _gla_pallas(pm, lower_bounds[i], hg_norm_w[i], *ctx, None, i)
        hg_l, _ = _gla_pallas(pm, lower_bounds[i], hg_norm_w[i], *lat, hgrn_t, i)
        s5_args = (pm, tables, s5_d[i], s5_glu_w[i], s5_glu_b[i])
        s5_c, xr, xi = _s5_pallas(*s5_args, *ctx, None, i)
        s5_l, _, _ = _s5_pallas(*s5_args, *lat, s5_x0, i)
        ctx_states.append((rs, gs, jnp.swapaxes(hs, -1, -2), xr.reshape(BATCH, DIRS, S5_GROUPS, S5_N),
                           xi.reshape(BATCH, DIRS, S5_GROUPS, S5_N)))

        parts = ((ret_c, ret_l), (gdn_c, gdn_l), (hg_c, hg_l), (s5_c, s5_l))
        x = _outproj(parts, w_out, x, mod, i)
        x = _ffn(x, norm2_w[i][None], mod, w1, w3, w2, final_norm_w[None], i)

    y_prompt = x[0].reshape(BATCH, SEQ, D_MODEL)
    y_sample = x[1].reshape(DEC_BATCH, DEC_SEQ, D_MODEL)
    new_states = tuple(jnp.stack([s[j] for s in ctx_states], axis=1) for j in range(5))
    return (y_prompt, y_sample) + new_states
```

```python
import functools
import math

import jax
import jax.numpy as jnp
from jax import lax
from jax.experimental import pallas as pl
from jax.experimental.pallas import tpu as pltpu

F32 = jnp.float32
BF16 = jnp.bfloat16

D_MODEL = 2048
BATCH = 16
SEQ = 256
DEPTH = 2
DEC_BATCH = 8
DEC_SEQ = 1024
GRID_W = 64
HEAD_DIM = 128
GROUP_WIDTH = 512
N_HEADS = 4
S5_CH = 16
S5_GROUPS = 32
S5_N = 64
GDN_CONV = 3
ROPE_BASE = 10000.0
FFN_HIDDEN = 5632
N_MOD = 6
EPS = 1e-6
LB_FLOOR = 1e-30

N_CTX_TOK = BATCH * SEQ
N_LAT_TOK = DEC_BATCH * DEC_SEQ
N_TOK = N_CTX_TOK + N_LAT_TOK
N_SEQ_ROWS = 16
PROJ_MAIN = 14 * GROUP_WIDTH
PROJ_SMALL = 128
V7X_VMEM_BYTES = 64 * 1024 * 1024
VMEM_LIMIT = V7X_VMEM_BYTES - 8 * 1024 * 1024
ADA_TN = 1024
OUT_TN = 1024


def _seq_row(tile, tm):
    n_ctx = N_CTX_TOK // tm
    per_lat = DEC_SEQ // tm
    return jnp.where(tile < n_ctx, 0, 1 + (tile - n_ctx) // per_lat)


def _stream_specs(tm, width, n_col=1):
    n_ctx = N_CTX_TOK // tm

    def ctx_map(i, j):
        return jnp.minimum(i, n_ctx - 1), (jnp.where(i < n_ctx, j, n_col - 1) if n_col > 1 else 0)

    def lat_map(i, j):
        return jnp.maximum(i - n_ctx, 0), (jnp.where(i >= n_ctx, j, 0) if n_col > 1 else 0)

    return pl.BlockSpec((tm, width), ctx_map), pl.BlockSpec((tm, width), lat_map)


def _on_stream(tile, tm, fn):
    n_ctx = N_CTX_TOK // tm
    pl.when(tile < n_ctx)(functools.partial(fn, 0))
    pl.when(tile >= n_ctx)(functools.partial(fn, 1))


def _ada_body(c_ref, w_ref, b_ref, o_ref):
    cv = c_ref[...]
    s = cv * jax.nn.sigmoid(cv)
    o_ref[0] = jnp.dot(s.astype(BF16), w_ref[0].astype(BF16), preferred_element_type=F32) + b_ref[0]


def _ada(cvec, ada_w, ada_b):
    tn = ADA_TN
    n = N_MOD * D_MODEL
    return pl.pallas_call(
        _ada_body,
        out_shape=jax.ShapeDtypeStruct((DEPTH, N_SEQ_ROWS, n), F32),
        grid=(DEPTH, n // tn),
        in_specs=[
            pl.BlockSpec((N_SEQ_ROWS, D_MODEL), lambda l, j: (0, 0)),
            pl.BlockSpec((1, D_MODEL, tn), lambda l, j: (l, 0, j)),
            pl.BlockSpec((1, 1, tn), lambda l, j: (l, 0, j)),
        ],
        out_specs=pl.BlockSpec((1, N_SEQ_ROWS, tn), lambda l, j: (l, 0, j)),
        compiler_params=pltpu.CompilerParams(
            dimension_semantics=("parallel", "parallel"), vmem_limit_bytes=VMEM_LIMIT),
        name="ada_mod",
    )(cvec, ada_w, ada_b.reshape(DEPTH, 1, n))


def _norm_mod(x, nw, sc, sh):
    ms = jnp.mean(x * x, axis=-1, keepdims=True)
    y = x * lax.rsqrt(ms + EPS) * nw
    return y * (1.0 + sc) + sh


PROJ_TM = 1024
PROJ_TN = 1024


def _inproj_body(x_ref, nw_ref, sc_ref, sh_ref, w_ref, ws_ref, o_ref, os_ref, h_ref):
    @pl.when(pl.program_id(1) == 0)
    def _():
        hb = _norm_mod(x_ref[...], nw_ref[...], sc_ref[...], sh_ref[...]).astype(BF16)
        h_ref[...] = hb
        os_ref[...] = jnp.dot(hb, ws_ref[...], preferred_element_type=F32)

    o_ref[...] = jnp.dot(h_ref[...], w_ref[...], preferred_element_type=F32)


def _inproj(x, nw, mod, w_main, w_small, layer):
    tm, tn = PROJ_TM, PROJ_TN
    return pl.pallas_call(
        _inproj_body,
        out_shape=(jax.ShapeDtypeStruct((N_TOK, PROJ_MAIN), F32),
                   jax.ShapeDtypeStruct((N_TOK, PROJ_SMALL), F32)),
        grid=(N_TOK // tm, PROJ_MAIN // tn),
        in_specs=[
            pl.BlockSpec((tm, D_MODEL), lambda i, j: (i, 0)),
            pl.BlockSpec((1, D_MODEL), lambda i, j: (0, 0)),
            pl.BlockSpec((None, None, 1, D_MODEL), lambda i, j: (_seq_row(i, tm), 1, 0, 0)),
            pl.BlockSpec((None, None, 1, D_MODEL), lambda i, j: (_seq_row(i, tm), 0, 0, 0)),
            pl.BlockSpec((None, D_MODEL, tn), lambda i, j: (layer, 0, j)),
            pl.BlockSpec((None, D_MODEL, PROJ_SMALL), lambda i, j: (layer, 0, 0)),
        ],
        out_specs=(pl.BlockSpec((tm, tn), lambda i, j: (i, j)),
                   pl.BlockSpec((tm, PROJ_SMALL), lambda i, j: (i, 0))),
        scratch_shapes=[pltpu.VMEM((tm, D_MODEL), BF16)],
        compiler_params=pltpu.CompilerParams(
            dimension_semantics=("parallel", "arbitrary"), vmem_limit_bytes=VMEM_LIMIT),
        name="in_proj",
    )(x, nw, mod, mod, w_main, w_small)


def _outproj_body(*refs):
    n_mix = 4
    m_refs, (w_ref, x_ref, g_ref, o_ref) = refs[:2 * n_mix], refs[2 * n_mix:]

    def compute(stream):
        acc = None
        for p in range(n_mix):
            part = jnp.dot(m_refs[2 * p + stream][...], w_ref[p * GROUP_WIDTH:(p + 1) * GROUP_WIDTH, :],
                           preferred_element_type=F32)
            acc = part if acc is None else acc + part
        o_ref[...] = x_ref[...] + g_ref[...] * acc

    _on_stream(pl.program_id(0), PROJ_TM, compute)


def _outproj(parts, w, x, mod, layer):
    tm, tn = PROJ_TM, OUT_TN
    in_specs = []
    for _ in parts:
        in_specs += list(_stream_specs(tm, GROUP_WIDTH))
    in_specs += [pl.BlockSpec((None, D_MODEL, tn), lambda i, j: (layer, 0, j)),
                 pl.BlockSpec((tm, tn), lambda i, j: (i, j)),
                 pl.BlockSpec((None, None, 1, tn), lambda i, j: (_seq_row(i, tm), 2, 0, j))]
    return pl.pallas_call(
        _outproj_body,
        out_shape=jax.ShapeDtypeStruct((N_TOK, D_MODEL), F32),
        grid=(N_TOK // tm, D_MODEL // tn),
        in_specs=in_specs,
        out_specs=pl.BlockSpec((tm, tn), lambda i, j: (i, j)),
        compiler_params=pltpu.CompilerParams(
            dimension_semantics=("parallel", "arbitrary"), vmem_limit_bytes=VMEM_LIMIT),
        name="out_proj",
    )(*(a for pair in parts for a in pair), w, x, mod)


FFN_TM = 512
FFN_TH = 512


def _ffn_body(x_ref, nw_ref, sc_ref, sh_ref, g_ref, w1_ref, w3_ref, w2_ref, fw_ref, *rest, final_norm):
    o_refs, (h_ref, acc_ref) = rest[:-2], rest[-2:]
    i, k = pl.program_id(0), pl.program_id(1)

    @pl.when(k == 0)
    def _():
        h_ref[...] = _norm_mod(x_ref[...], nw_ref[...], sc_ref[...], sh_ref[...]).astype(BF16)
        acc_ref[...] = jnp.zeros_like(acc_ref)

    h = h_ref[...]
    a = jnp.dot(h, w1_ref[...], preferred_element_type=F32)
    b = jnp.dot(h, w3_ref[...], preferred_element_type=F32)
    g = (a * jax.nn.sigmoid(a) * b).astype(BF16)
    acc_ref[...] += jnp.dot(g, w2_ref[...], preferred_element_type=F32)

    @pl.when(k == pl.num_programs(1) - 1)
    def _():
        y = x_ref[...] + g_ref[...] * acc_ref[...]
        if not final_norm:
            o_refs[0][...] = y
        else:
            ms = jnp.mean(y * y, axis=-1, keepdims=True)
            y = y * lax.rsqrt(ms + EPS) * fw_ref[...]

            def write(stream):
                o_refs[stream][...] = y

            _on_stream(i, FFN_TM, write)


def _ffn(x, nw, mod, w1, w3, w2, fw, layer):
    final_norm = layer == DEPTH - 1
    tm, th = FFN_TM, FFN_TH
    if final_norm:
        out_shape = (jax.ShapeDtypeStruct((N_CTX_TOK, D_MODEL), F32), jax.ShapeDtypeStruct((N_LAT_TOK, D_MODEL), F32))
        out_specs = _stream_specs(tm, D_MODEL)
    else:
        out_shape = jax.ShapeDtypeStruct((N_TOK, D_MODEL), F32)
        out_specs = pl.BlockSpec((tm, D_MODEL), lambda i, k: (i, 0))
    return pl.pallas_call(
        functools.partial(_ffn_body, final_norm=final_norm),
        out_shape=out_shape,
        grid=(N_TOK // tm, FFN_HIDDEN // th),
        in_specs=[
            pl.BlockSpec((tm, D_MODEL), lambda i, k: (i, 0)),
            pl.BlockSpec((1, D_MODEL), lambda i, k: (0, 0)),
            pl.BlockSpec((None, None, 1, D_MODEL), lambda i, k: (_seq_row(i, tm), 4, 0, 0)),
            pl.BlockSpec((None, None, 1, D_MODEL), lambda i, k: (_seq_row(i, tm), 3, 0, 0)),
            pl.BlockSpec((None, None, 1, D_MODEL), lambda i, k: (_seq_row(i, tm), 5, 0, 0)),
            pl.BlockSpec((None, D_MODEL, th), lambda i, k: (layer, 0, k)),
            pl.BlockSpec((None, D_MODEL, th), lambda i, k: (layer, 0, k)),
            pl.BlockSpec((None, th, D_MODEL), lambda i, k: (layer, k, 0)),
            pl.BlockSpec((1, D_MODEL), lambda i, k: (0, 0)),
        ],
        out_specs=out_specs,
        scratch_shapes=[pltpu.VMEM((tm, D_MODEL), BF16), pltpu.VMEM((tm, D_MODEL), F32)],
        compiler_params=pltpu.CompilerParams(
            dimension_semantics=("arbitrary", "arbitrary"), vmem_limit_bytes=VMEM_LIMIT),
        name="ffn",
    )(x, nw, mod, mod, mod, w1, w3, w2, fw)


def _axial_rope(l):
    n_rows = l // GRID_W
    t_row = jnp.repeat(jnp.arange(n_rows, dtype=F32), GRID_W)
    t_col = jnp.tile(jnp.arange(GRID_W, dtype=F32), n_rows)
    n_freq = HEAD_DIM // 4
    inv = ROPE_BASE ** (-jnp.arange(n_freq, dtype=F32) / n_freq)
    ang = jnp.concatenate([t_row[:, None] * inv, t_col[:, None] * inv], axis=-1)
    return jnp.cos(ang), jnp.sin(ang)


MIX_CHUNK = 128
DIRS = 2
NT_DIMS = (((1,), (1,)), ((), ()))
TN_DIMS = (((0,), (0,)), ((), ()))


def _dot(a, b):
    return jnp.dot(a.astype(BF16), b.astype(BF16), preferred_element_type=F32)


def _dot_nt(a, b):
    return lax.dot_general(a.astype(BF16), b.astype(BF16), NT_DIMS, preferred_element_type=F32)


def _dot_tn(a, b):
    return lax.dot_general(a.astype(BF16), b.astype(BF16), TN_DIMS, preferred_element_type=F32)


def _silu(x):
    return x * jax.nn.sigmoid(x)


def _seq_spec(seq_len, width, row_off, col_blk, pipeline_mode=None):
    return pl.BlockSpec((seq_len, width), lambda b: (row_off + b, col_blk), pipeline_mode=pipeline_mode)


def _mix_params(n_par):
    return pltpu.CompilerParams(dimension_semantics=("parallel",) * n_par, vmem_limit_bytes=VMEM_LIMIT)


def _ret_body(*refs, n_chunks, use_rope, has_s0):
    it = iter(refs)
    lg_ref, q_ref, k_ref, v_ref, g_ref = (next(it) for _ in range(5))
    cos_ref, sin_ref = (next(it), next(it)) if use_rope else (None, None)
    s0_ref = next(it) if has_s0 else None
    o_ref, sf_ref, acc_ref, s_ref, intra_ref, qd_ref, kd_ref, cd_ref = (next(it) for _ in range(8))
    C, H = MIX_CHUNK, N_HEADS

    row = lax.broadcasted_iota(jnp.int32, (C, C), 0)
    col = lax.broadcasted_iota(jnp.int32, (C, C), 1)
    rel = (row - col).astype(F32)
    pos = lax.broadcasted_iota(jnp.int32, (C, HEAD_DIM), 0).astype(F32)
    for d in range(DIRS):
        for h in range(H):
            lg = lg_ref[d, h]
            if d == 0:
                intra_ref[d, h] = jnp.where(rel >= 0, jnp.exp(jnp.maximum(rel, 0.0) * lg), 0.0)
                qd_ref[d, h] = jnp.exp((pos + 1.0) * lg)
                kd_ref[d, h] = jnp.exp((C - 1.0 - pos) * lg)
            else:
                intra_ref[d, h] = jnp.where(rel <= 0, jnp.exp(jnp.maximum(-rel, 0.0) * lg), 0.0)
                qd_ref[d, h] = jnp.exp((C - pos) * lg)
                kd_ref[d, h] = jnp.exp(pos * lg)
            cd_ref[d, h] = jnp.exp(jnp.full((C, HEAD_DIM), C, F32) * lg)
    if has_s0:
        s_ref[...] = s0_ref[...]
    else:
        s_ref[...] = jnp.zeros_like(s_ref)
    acc_ref[...] = jnp.zeros_like(acc_ref)

    units = [(d, h) for d in range(DIRS) for h in range(H)]

    def chunk_step(n):
        rows_d = [slice(c * C, (c + 1) * C) for c in (n, n_chunks - 1 - n)]
        q_l, k_l, v_l, sc_l = [], [], [], []
        for d, h in units:
            rows, cols = rows_d[d], slice(h * HEAD_DIM, (h + 1) * HEAD_DIM)
            q, k = q_ref[rows, cols], k_ref[rows, cols]
            if use_rope:
                cs, sn = cos_ref[rows, :], sin_ref[rows, :]
                q = q * cs + pltpu.roll(q, HEAD_DIM // 2, 1) * sn
                k = k * cs + pltpu.roll(k, HEAD_DIM // 2, 1) * sn
            k = k * HEAD_DIM ** -0.5
            q_l.append(q)
            k_l.append(k)
            v_l.append(v_ref[rows, cols].astype(BF16))
            sc_l.append(_dot_nt(q, k) * intra_ref[d, h])
        for i, (d, h) in enumerate(units):
            cols = slice(h * HEAD_DIM, (h + 1) * HEAD_DIM)
            s = s_ref[d, h]
            acc_ref[rows_d[d], cols] += _dot(jnp.concatenate([sc_l[i], q_l[i] * qd_ref[d, h]], axis=1),
                                             jnp.concatenate([v_l[i], s.astype(BF16)], axis=0))
            s_ref[d, h] = cd_ref[d, h] * s + _dot_tn(k_l[i] * kd_ref[d, h], v_l[i])

    for n in range(n_chunks):
        chunk_step(n)
    sf_ref[...] = s_ref[...]

    for n in range(n_chunks):
        rows = slice(n * C, (n + 1) * C)
        for h in range(H):
            cols = slice(h * HEAD_DIM, (h + 1) * HEAD_DIM)
            o = acc_ref[rows, cols]
            mu = jnp.mean(o, axis=-1, keepdims=True)
            oc = o - mu
            y = oc * lax.rsqrt(jnp.mean(oc * oc, axis=-1, keepdims=True) + EPS)
            o_ref[rows, cols] = (y * _silu(g_ref[rows, cols])).astype(BF16)


def _retention_pallas(pm, log_gamma, n_seq, seq_len, row_off, rope2, s0, layer):
    use_rope, has_s0 = rope2 is not None, s0 is not None
    st_shape = (DIRS, N_HEADS, HEAD_DIM, HEAD_DIM)
    in_specs = [pl.BlockSpec(memory_space=pltpu.SMEM)]
    in_specs += [_seq_spec(seq_len, GROUP_WIDTH, row_off, cb) for cb in range(4)]
    args = [log_gamma, pm, pm, pm, pm]
    if use_rope:
        in_specs += [pl.BlockSpec((seq_len, HEAD_DIM), lambda b: (0, 0))] * 2
        args += list(rope2)
    if has_s0:
        in_specs.append(pl.BlockSpec((None, None) + st_shape, lambda b: (b, layer, 0, 0, 0, 0)))
        args.append(s0)
    return pl.pallas_call(
        functools.partial(_ret_body, n_chunks=seq_len // MIX_CHUNK, use_rope=use_rope, has_s0=has_s0),
        out_shape=(jax.ShapeDtypeStruct((n_seq * seq_len, GROUP_WIDTH), BF16),
                   jax.ShapeDtypeStruct((n_seq,) + st_shape, F32)),
        grid=(n_seq,),
        in_specs=in_specs,
        out_specs=(pl.BlockSpec((seq_len, GROUP_WIDTH), lambda b: (b, 0)),
                   pl.BlockSpec((None,) + st_shape, lambda b: (b, 0, 0, 0, 0))),
        scratch_shapes=[pltpu.VMEM((seq_len, GROUP_WIDTH), F32), pltpu.VMEM(st_shape, F32),
                        pltpu.VMEM((DIRS, N_HEADS, MIX_CHUNK, MIX_CHUNK), F32),
                        pltpu.VMEM((DIRS, N_HEADS, MIX_CHUNK, HEAD_DIM), F32),
                        pltpu.VMEM((DIRS, N_HEADS, MIX_CHUNK, HEAD_DIM), F32),
                        pltpu.VMEM((DIRS, N_HEADS, MIX_CHUNK, HEAD_DIM), F32)],
        compiler_params=_mix_params(1),
        name="retention",
    )(*args)


def _rope_tables(l):
    cos, sin = _axial_rope(l)
    return jnp.concatenate([cos, cos], axis=-1), jnp.concatenate([-sin, sin], axis=-1)


S5_HALF_G = S5_GROUPS // 2
S5_HALF_U = S5_HALF_G * S5_CH
S5_HALF_X = S5_HALF_G * S5_N
S5_X = S5_GROUPS * S5_N
S5_TC = 512
S5_BLK = 8
S5_TABS = 4


def _s5_tables(a_re, a_im, log_step, b_re, b_im, c_re, c_im):
    dt = jnp.exp(log_step)[..., None]
    mag = jnp.exp(a_re * dt)
    ab_re = mag * jnp.cos(a_im * dt)
    ab_im = mag * jnp.sin(a_im * dt)
    den = a_re * a_re + a_im * a_im
    nr = ab_re - 1.0
    f_re = (nr * a_re + ab_im * a_im) / den
    f_im = (ab_im * a_re - nr * a_im) / den
    bb_re = f_re[..., None] * b_re - f_im[..., None] * b_im
    bb_im = f_re[..., None] * b_im + f_im[..., None] * b_re
    eye = jnp.eye(S5_HALF_G, dtype=F32)

    def in_mat(bb):
        bb = bb.reshape(DIRS, 2, S5_HALF_G, S5_N, S5_CH)
        return jnp.einsum('dhgnc,gk->dhgckn', bb, eye).reshape(DIRS, 2, S5_HALF_U, S5_HALF_X)

    def out_mat(cc):
        cc = cc.reshape(DIRS, 2, S5_HALF_G, S5_CH, S5_N)
        return jnp.einsum('dhgcn,gk->dhgnkc', cc, eye).reshape(DIRS, 2, S5_HALF_X, S5_HALF_U)

    bm = jnp.concatenate([in_mat(bb_re), in_mat(bb_im)], axis=-1).astype(BF16)
    cm = jnp.concatenate([out_mat(c_re), -out_mat(c_im)], axis=-2).astype(BF16)
    t = jnp.arange(S5_BLK, dtype=F32)
    order = jnp.stack([t, S5_BLK - 1.0 - t])
    shifts = 2.0 ** jnp.arange(S5_TABS - 1, dtype=F32)
    expo = jnp.concatenate([jnp.where(order[:, None, :] >= shifts[None, :, None], shifts[None, :, None], jnp.nan),
                            order[:, None, :] + 1.0], axis=1)
    live = ~jnp.isnan(expo)
    e = jnp.where(live, expo, 0.0)[..., None]
    adt_re = (a_re * dt).reshape(DIRS, 1, 1, S5_X)
    adt_im = (a_im * dt).reshape(DIRS, 1, 1, S5_X)
    pmag = jnp.where(live[..., None], jnp.exp(e * adt_re), 0.0)
    pw_re = pmag * jnp.cos(e * adt_im)
    pw_im = pmag * jnp.sin(e * adt_im)
    return bm, cm, pw_re, pw_im


def _gelu_tanh(x):
    return 0.5 * x * (1.0 + jnp.tanh(math.sqrt(2.0 / math.pi) * (x + 0.044715 * (x * x * x))))


def _s5_body(*refs, seq_len, has_s0):
    it = iter(refs)
    u_ref, bm_ref, cm_ref, pwr_ref, pwi_ref, d_ref, gw_ref, gb_ref = (next(it) for _ in range(8))
    x0r_ref, x0i_ref = (next(it), next(it)) if has_s0 else (None, None)
    o_ref, sfr_ref, sfi_ref, y_ref, xr_ref, xi_ref, xb_ref = (next(it) for _ in range(7))
    tc = min(seq_len, S5_TC)
    n_tiles = seq_len // tc
    n_pair = tc // (2 * S5_BLK)

    y_ref[...] = d_ref[...] * u_ref[...]
    unit = 0
    for d in range(DIRS):
        last = S5_BLK - 1 if d == 0 else 0
        for hf in range(2):
            xs = slice(hf * S5_HALF_X, (hf + 1) * S5_HALF_X)
            us = slice(hf * S5_HALF_U, (hf + 1) * S5_HALF_U)

            def scan_block(x_in, carry, d=d, xs=xs, last=last):
                xr, xi = x_in
                car_re, car_im = carry
                for k in range(S5_TABS - 1):
                    s = 1 << k
                    shift = s if d == 0 else S5_BLK - s
                    p_re, p_im = pwr_ref[d, k, :, xs], pwi_ref[d, k, :, xs]
                    sr, si = pltpu.roll(xr, shift, 0), pltpu.roll(xi, shift, 0)
                    xr, xi = xr + p_re * sr - p_im * si, xi + p_re * si + p_im * sr
                p_re, p_im = pwr_ref[d, S5_TABS - 1, :, xs], pwi_ref[d, S5_TABS - 1, :, xs]
                xr, xi = xr + p_re * car_re - p_im * car_im, xi + p_re * car_im + p_im * car_re
                return (xr, xi), (xr[last:last + 1, :], xi[last:last + 1, :])

            if has_s0:
                carry = (x0r_ref[d, :, xs], x0i_ref[d, :, xs])
            else:
                carry = (jnp.zeros((1, S5_HALF_X), F32), jnp.zeros((1, S5_HALF_X), F32))
            for i in range(n_tiles):
                slot = unit % 2
                unit += 1
                tile = i if d == 0 else n_tiles - 1 - i
                rows_t = slice(tile * tc, (tile + 1) * tc)
                bu = _dot(u_ref[rows_t, us], bm_ref[d, hf])
                xr_ref[slot] = bu[:, :S5_HALF_X]
                xi_ref[slot] = bu[:, S5_HALF_X:]
                for j in range(n_pair):
                    pair = j if d == 0 else n_pair - 1 - j
                    rows = slice(pair * 2 * S5_BLK, (pair + 1) * 2 * S5_BLK)
                    xr2, xi2 = xr_ref[slot, rows, :], xi_ref[slot, rows, :]
                    halves = [(xr2[:S5_BLK], xi2[:S5_BLK]), (xr2[S5_BLK:], xi2[S5_BLK:])]
                    out = [None, None]
                    for idx in ((0, 1) if d == 0 else (1, 0)):
                        out[idx], carry = scan_block(halves[idx], carry)
                    xb_ref[slot, rows, :S5_HALF_X] = jnp.concatenate([out[0][0], out[1][0]], axis=0).astype(BF16)
                    xb_ref[slot, rows, S5_HALF_X:] = jnp.concatenate([out[0][1], out[1][1]], axis=0).astype(BF16)
                y_ref[rows_t, us] += jnp.dot(xb_ref[slot], cm_ref[d, hf], preferred_element_type=F32)
            sfr_ref[d, :, xs] = carry[0]
            sfi_ref[d, :, xs] = carry[1]

    z = _gelu_tanh(y_ref[...])
    o_ref[...] = (z * jax.nn.sigmoid(_dot(z, gw_ref[...]) + gb_ref[...])).astype(BF16)


def _s5_pallas(pm, tables, s5_d, glu_w, glu_b, n_seq, seq_len, row_off, x0, layer):
    bm, cm, pw_re, pw_im = tables
    has_s0 = x0 is not None
    full = lambda shape: pl.BlockSpec(shape, lambda b: (0,) * len(shape))
    of_layer = lambda t: pl.BlockSpec((None,) + t.shape[1:], lambda b: (layer,) + (0,) * (t.ndim - 1))
    in_specs = [_seq_spec(seq_len, GROUP_WIDTH, row_off, 13),
                of_layer(bm), of_layer(cm), of_layer(pw_re), of_layer(pw_im),
                full((1, GROUP_WIDTH)), full((GROUP_WIDTH, GROUP_WIDTH)), full((1, GROUP_WIDTH))]
    args = [pm, bm, cm, pw_re, pw_im, s5_d.reshape(1, GROUP_WIDTH), glu_w.astype(BF16),
            glu_b.reshape(1, GROUP_WIDTH)]
    if has_s0:
        in_specs += [pl.BlockSpec((None, None, DIRS, 1, S5_X), lambda b: (b, layer, 0, 0, 0))] * 2
        args += list(x0)
    st = jax.ShapeDtypeStruct((n_seq, DIRS, 1, S5_X), F32)
    st_spec = pl.BlockSpec((None, DIRS, 1, S5_X), lambda b: (b, 0, 0, 0))
    tc = min(seq_len, S5_TC)
    return pl.pallas_call(
        functools.partial(_s5_body, seq_len=seq_len, has_s0=has_s0),
        out_shape=(jax.ShapeDtypeStruct((n_seq * seq_len, GROUP_WIDTH), BF16), st, st),
        grid=(n_seq,),
        in_specs=in_specs,
        out_specs=(pl.BlockSpec((seq_len, GROUP_WIDTH), lambda b: (b, 0)), st_spec, st_spec),
        scratch_shapes=[pltpu.VMEM((seq_len, GROUP_WIDTH), F32),
                        pltpu.VMEM((2, tc, S5_HALF_X), F32), pltpu.VMEM((2, tc, S5_HALF_X), F32),
                        pltpu.VMEM((2, tc, 2 * S5_HALF_X), BF16)],
        compiler_params=_mix_params(1),
        name="s5",
    )(*args)


GLA_LEVELS = 7


def _chunk_cumsum_rows(x, rowi):
    s = 1
    while s < MIX_CHUNK:
        x = x + jnp.where(rowi >= s, pltpu.roll(x, s, 0), 0.0)
        s *= 2
    return x


def _gla_body(*refs, n_chunks, has_s0):
    it = iter(refs)
    q_ref, f0_ref, f1_ref, i_ref, g_ref, lb_ref, nw_ref = (next(it) for _ in range(7))
    s0_ref = next(it) if has_s0 else None
    o_ref, sf_ref, acc_ref, s_ref, code_ref = (next(it) for _ in range(5))
    C, H = MIX_CHUNK, N_HEADS
    f_refs = (f0_ref, f1_ref)

    rowi = lax.broadcasted_iota(jnp.int32, (C, HEAD_DIM), 0)
    ri = lax.broadcasted_iota(jnp.int32, (C, C), 0)
    ci = lax.broadcasted_iota(jnp.int32, (C, C), 1)
    top_bit = 31 - lax.clz(ri ^ ci)
    code_ref[...] = jnp.where(ri > ci, top_bit, jnp.where(ri < ci, -1 - top_bit, GLA_LEVELS))
    if has_s0:
        s_ref[...] = s0_ref[...]
    else:
        s_ref[...] = jnp.zeros_like(s_ref)
    acc_ref[...] = jnp.zeros_like(acc_ref)

    def chunk_step(n, carry):
        for d in range(DIRS):
            c = n if d == 0 else n_chunks - 1 - n
            rows = pl.ds(pl.multiple_of(c * C, C), C)
            for h in range(H):
                cols = slice(h * HEAD_DIM, (h + 1) * HEAD_DIM)
                code = code_ref[...]
                q = _silu(q_ref[rows, cols]) * HEAD_DIM ** -0.5
                v = i_ref[rows, cols].astype(BF16)
                fx = f_refs[d][rows, cols]
                lb = lb_ref[d:d + 1, cols]
                sig = 1.0 / (1.0 + jnp.exp(-fx))
                logf = jnp.log2(jnp.maximum(lb, LB_FLOOR) + (1.0 - lb) * sig)
                k = (1.0 - lb) * (1.0 - sig)
                cum = _chunk_cumsum_rows(logf, rowi)
                own = cum
                attn = jnp.where(code == GLA_LEVELS, _dot_nt(q, k), 0.0)
                for lvl in range(GLA_LEVELS):
                    m = 1 << lvl
                    prev = pltpu.roll(own, m, 0)
                    pre = jnp.minimum(cum - prev, 0.0)
                    suf = own - cum
                    if d == 0:
                        sc = _dot_nt(q * jnp.exp2(pre), k * jnp.exp2(suf))
                        hit = code == lvl
                    else:
                        sc = _dot_nt(q * jnp.exp2(suf + logf), k * jnp.exp2(jnp.minimum(pre - logf, 0.0)))
                        hit = code == -1 - lvl
                    attn = jnp.where(hit, sc, attn)
                    own = jnp.where(((rowi >> lvl) & 1) == 0, pltpu.roll(own, C - m, 0), own)
                tot = own
                st = s_ref[d, h]
                if d == 0:
                    q_in, k_out = q * jnp.exp2(cum), k * jnp.exp2(tot - cum)
                else:
                    q_in, k_out = q * jnp.exp2(tot - cum + logf), k * jnp.exp2(cum - logf)
                acc_ref[rows, cols] += _dot(attn, v) + _dot_nt(q_in, st)
                s_ref[d, h] = jnp.exp2(tot[0:1, :]) * st + _dot_tn(v, k_out)
        return carry

    lax.fori_loop(0, n_chunks, chunk_step, 0)
    sf_ref[...] = s_ref[...]

    def finish(n, carry):
        rows = pl.ds(pl.multiple_of(n * C, C), C)
        for h in range(H):
            cols = slice(h * HEAD_DIM, (h + 1) * HEAD_DIM)
            o = acc_ref[rows, cols]
            y = o * lax.rsqrt(jnp.mean(o * o, axis=-1, keepdims=True) + EPS) * nw_ref[...]
            o_ref[rows, cols] = (y * _silu(g_ref[rows, cols])).astype(BF16)
        return carry

    lax.fori_loop(0, n_chunks, finish, 0)


def _gla_pallas(pm, lower_bound, norm_w, n_seq, seq_len, row_off, s0_t, layer):
    has_s0 = s0_t is not None
    st_shape = (DIRS, N_HEADS, HEAD_DIM, HEAD_DIM)
    in_specs = [_seq_spec(seq_len, GROUP_WIDTH, row_off, cb) for cb in (8, 9, 10, 11, 12)]
    in_specs += [pl.BlockSpec((DIRS, GROUP_WIDTH), lambda b: (0, 0)), pl.BlockSpec((1, HEAD_DIM), lambda b: (0, 0))]
    args = [pm] * 5 + [lower_bound, norm_w.reshape(1, HEAD_DIM)]
    if has_s0:
        in_specs.append(pl.BlockSpec((None, None) + st_shape, lambda b: (b, layer, 0, 0, 0, 0)))
        args.append(s0_t)
    return pl.pallas_call(
        functools.partial(_gla_body, n_chunks=seq_len // MIX_CHUNK, has_s0=has_s0),
        out_shape=(jax.ShapeDtypeStruct((n_seq * seq_len, GROUP_WIDTH), BF16),
                   jax.ShapeDtypeStruct((n_seq,) + st_shape, F32)),
        grid=(n_seq,),
        in_specs=in_specs,
        out_specs=(pl.BlockSpec((seq_len, GROUP_WIDTH), lambda b: (b, 0)),
                   pl.BlockSpec((None,) + st_shape, lambda b: (b, 0, 0, 0, 0))),
        scratch_shapes=[pltpu.VMEM((seq_len, GROUP_WIDTH), F32), pltpu.VMEM(st_shape, F32),
                        pltpu.VMEM((MIX_CHUNK, MIX_CHUNK), jnp.int32)],
        compiler_params=_mix_params(1),
        name="hgrn2",
    )(*args)


GDN_BLOCK_BITS = 4
GDN_MERGES = 3
N_GATES = DIRS * N_HEADS


def _softplus(x):
    return jnp.maximum(x, 0.0) + jnp.log1p(jnp.exp(-jnp.abs(x)))


def _gdn_body(*refs, seq_len, n_sub, has_s0):
    it = iter(refs)
    (q_ref, k_ref, v_ref, g_ref, ps_ref, pst_ref, cw_ref, prow_ref, pcol_ref, nw_ref) = (next(it) for _ in range(10))
    s0_ref = next(it) if has_s0 else None
    (o_ref, sf_ref, acc_ref, s_ref, qn_ref, kn_ref, vn_ref, gate_ref, gatet_ref, blev_ref) = (
        next(it) for _ in range(10))
    C, H, L = MIX_CHUNK, N_HEADS, seq_len
    R = n_sub * L
    n_chunks = L // C
    w = GROUP_WIDTH

    pos = lax.broadcasted_iota(jnp.int32, (R, HEAD_DIM), 0) & (L - 1)
    for part, (src, dst) in enumerate(((q_ref, qn_ref), (k_ref, kn_ref), (v_ref, vn_ref))):
        for h in range(H):
            cols = slice(h * HEAD_DIM, (h + 1) * HEAD_DIM)
            wc = slice(part * w + h * HEAD_DIM, part * w + (h + 1) * HEAD_DIM)
            x = src[:, cols]
            x_prev = jnp.where(pos >= 1, pltpu.roll(x, 1, 0), 0.0)
            x_next = jnp.where(pos < L - 1, pltpu.roll(x, R - 1, 0), 0.0)
            y = _silu(cw_ref[0:1, wc] * x_prev + cw_ref[1:2, wc] * x + cw_ref[2:3, wc] * x_next)
            if part < 2:
                y = y * lax.rsqrt(jnp.sum(y * y, axis=-1, keepdims=True) + EPS)
            if part == 0:
                y = y * HEAD_DIM ** -0.5
            dst[:, cols] = y

    rowi = lax.broadcasted_iota(jnp.int32, (C, HEAD_DIM), 0)
    lane = lax.broadcasted_iota(jnp.int32, (C, HEAD_DIM), 1)
    lane_t = lax.broadcasted_iota(jnp.int32, (2 * N_GATES, C), 1)
    row_t = lax.broadcasted_iota(jnp.int32, (2 * N_GATES, C), 0)
    for c in range(R // C):
        rows = slice(c * C, (c + 1) * C)
        a = ps_ref[rows, :]
        la = -jnp.exp(prow_ref[0:1, :]) * _softplus(a + prow_ref[1:2, :])
        pre = _chunk_cumsum_rows(la, rowi)
        tot = jnp.broadcast_to(pre[C - 1:C, :], (C, HEAD_DIM))
        g = jnp.where(lane < N_HEADS, pre, tot - pre + la)
        gate_ref[rows, :] = jnp.where(lane < N_GATES, g,
                                      jnp.where(lane < 2 * N_GATES, jax.nn.sigmoid(a),
                                                pltpu.roll(tot, 2 * N_GATES, 1)))
        at = pst_ref[:, rows]
        lat = -jnp.exp(pcol_ref[:, 0:1]) * _softplus(at + pcol_ref[:, 1:2])
        pre_t, s = lat, 1
        while s < C:
            pre_t = pre_t + jnp.where(lane_t >= s, pltpu.roll(pre_t, s, 1), 0.0)
            s *= 2
        tot_t = jnp.broadcast_to(pre_t[:, C - 1:C], (2 * N_GATES, C))
        g_t = jnp.where(row_t < N_HEADS, pre_t, tot_t - pre_t + lat)
        gatet_ref[:, rows] = jnp.where(row_t < N_GATES, g_t, jax.nn.sigmoid(at))

    ri = lax.broadcasted_iota(jnp.int32, (C, C), 0)
    ci = lax.broadcasted_iota(jnp.int32, (C, C), 1)
    bx = (ri >> GDN_BLOCK_BITS) ^ (ci >> GDN_BLOCK_BITS)
    blev_ref[...] = jnp.where(bx == 0, 0, 32 - lax.clz(bx))
    if has_s0:
        for s in range(n_sub):
            s_ref[s * DIRS:(s + 1) * DIRS] = s0_ref[s]
    else:
        s_ref[...] = jnp.zeros_like(s_ref)
    acc_ref[...] = jnp.zeros_like(acc_ref)

    units = [(s, d, h) for s in range(n_sub) for d in range(DIRS) for h in range(H)]

    def chunk_step(n):
        eye = jnp.where(ri == ci, 1.0, 0.0)
        blev = blev_ref[...]
        rows_sd = [[slice(s * L + c * C, s * L + (c + 1) * C) for c in (n, n_chunks - 1 - n)] for s in range(n_sub)]
        incl_d = [ri >= ci, ri <= ci]
        strict_d = [ri > ci, ri < ci]
        q_l, et_l, ek_l, rhs_l, a_l, attn_l = ([] for _ in range(6))
        for s, d, h in units:
            rows, cols, cg = rows_sd[s][d], slice(h * HEAD_DIM, (h + 1) * HEAD_DIM), d * N_HEADS + h
            q, k, v = qn_ref[rows, cols], kn_ref[rows, cols], vn_ref[rows, cols]
            g_i = jnp.broadcast_to(gate_ref[rows, cg:cg + 1], (C, C))
            b_i = jnp.broadcast_to(gate_ref[rows, N_GATES + cg:N_GATES + cg + 1], (C, C))
            tot = jnp.broadcast_to(gate_ref[rows, 2 * N_GATES + cg:2 * N_GATES + cg + 1], (C, C))
            g_j = gatet_ref[cg:cg + 1, rows]
            decay = jnp.where(incl_d[d], jnp.exp(jnp.minimum(g_i - g_j, 0.0)), 0.0)
            e_g = jnp.exp(g_i)
            kb = k.astype(BF16)
            a_l.append(jnp.where(strict_d[d], _dot_nt(kb, kb) * b_i * decay, 0.0))
            attn_l.append(_dot_nt(q, kb) * decay)
            rhs_l.append(jnp.concatenate([v * b_i, k * (b_i * e_g)], axis=1).astype(BF16))
            q_l.append(q * e_g)
            ek_l.append(k * jnp.exp(tot - g_i))
            et_l.append(jnp.exp(tot))
        b_l = [-jnp.where(blev == 0, a, 0.0) for a in a_l]
        p_l = [eye + b for b in b_l]
        for _ in range(GDN_BLOCK_BITS - 1):
            b_l = [_dot(b, b) for b in b_l]
            p_l = [p + _dot(p, b) for p, b in zip(p_l, b_l)]
        for lvl in range(1, GDN_MERGES + 1):
            ep_l = [_dot(jnp.where(blev == lvl, a, 0.0), p) for a, p in zip(a_l, p_l)]
            p_l = [p - _dot(p, ep) for p, ep in zip(p_l, ep_l)]
        uw_l = [_dot(p, rhs) for p, rhs in zip(p_l, rhs_l)]
        s_l = [s_ref[s * DIRS + d, h] for s, d, h in units]
        vn_l = [uw[:, :HEAD_DIM] - _dot(uw[:, HEAD_DIM:], st) for uw, st in zip(uw_l, s_l)]
        for i, (s, d, h) in enumerate(units):
            cols = slice(h * HEAD_DIM, (h + 1) * HEAD_DIM)
            acc_ref[rows_sd[s][d], cols] += _dot(jnp.concatenate([q_l[i], attn_l[i]], axis=1),
                                                 jnp.concatenate([s_l[i], vn_l[i]], axis=0))
            s_ref[s * DIRS + d, h] = s_l[i] * et_l[i] + _dot_tn(ek_l[i], vn_l[i])

    for n in range(n_chunks):
        chunk_step(n)
    sf_ref[...] = s_ref[...]

    for n in range(R // C):
        rows = slice(n * C, (n + 1) * C)
        for h in range(H):
            cols = slice(h * HEAD_DIM, (h + 1) * HEAD_DIM)
            o = acc_ref[rows, cols]
            y = o * lax.rsqrt(jnp.mean(o * o, axis=-1, keepdims=True) + EPS) * nw_ref[...]
            o_ref[rows, cols] = (y * _silu(g_ref[rows, cols])).astype(BF16)


GDN_CTX_SUB = 2
GDN_LAT_SUB = 1
GDN_DOUBLE_BUFFER_BYTES = 2 * 1024 * 1024


def _gdn_pallas(pm, ps, ps_t, conv_w, a_log, dt_bias, norm_w, n_seq, seq_len, row_off, s0, layer):
    has_s0 = s0 is not None
    n_sub = GDN_LAT_SUB if has_s0 else GDN_CTX_SUB
    n_seq, seq_len, row_off = n_seq // n_sub, seq_len * n_sub, row_off // n_sub
    st_shape = (n_sub * DIRS, N_HEADS, HEAD_DIM, HEAD_DIM)
    par = jnp.stack([a_log.reshape(N_GATES), dt_bias.reshape(N_GATES)])
    par_row = jnp.pad(par, ((0, 0), (0, HEAD_DIM - N_GATES)))
    par_col = jnp.pad(par.T, ((0, N_GATES), (0, 0)))
    full = lambda shape: pl.BlockSpec(shape, lambda b: (0,) * len(shape))
    big_mode = pl.Buffered(1) if seq_len * GROUP_WIDTH * 4 > GDN_DOUBLE_BUFFER_BYTES else None
    in_specs = [_seq_spec(seq_len, GROUP_WIDTH, row_off, cb, big_mode) for cb in (4, 5, 6, 7)]
    in_specs += [_seq_spec(seq_len, PROJ_SMALL, row_off, 0),
                 pl.BlockSpec((2 * N_GATES, seq_len), lambda b: (0, row_off + b)),
                 full((GDN_CONV, 3 * GROUP_WIDTH)), full((2, HEAD_DIM)), full((2 * N_GATES, 2)), full((1, HEAD_DIM))]
    args = [pm] * 4 + [ps, ps_t, conv_w, par_row, par_col, norm_w.reshape(1, HEAD_DIM)]
    if has_s0:
        in_specs.append(pl.BlockSpec((n_sub, None, DIRS, N_HEADS, HEAD_DIM, HEAD_DIM),
                                     lambda b: (b, layer, 0, 0, 0, 0)))
        args.append(s0)
    seq_f32 = lambda width: pltpu.VMEM((seq_len, width), F32)
    out, s_fin = pl.pallas_call(
        functools.partial(_gdn_body, seq_len=seq_len // n_sub, n_sub=n_sub, has_s0=has_s0),
        out_shape=(jax.ShapeDtypeStruct((n_seq * seq_len, GROUP_WIDTH), BF16),
                   jax.ShapeDtypeStruct((n_seq,) + st_shape, F32)),
        grid=(n_seq,),
        in_specs=in_specs,
        out_specs=(pl.BlockSpec((seq_len, GROUP_WIDTH), lambda b: (b, 0)),
                   pl.BlockSpec((None,) + st_shape, lambda b: (b, 0, 0, 0, 0))),
        scratch_shapes=[seq_f32(GROUP_WIDTH), pltpu.VMEM(st_shape, F32),
                        seq_f32(GROUP_WIDTH), seq_f32(GROUP_WIDTH), seq_f32(GROUP_WIDTH),
                        seq_f32(HEAD_DIM), pltpu.VMEM((2 * N_GATES, seq_len), F32),
                        pltpu.VMEM((MIX_CHUNK, MIX_CHUNK), jnp.int32)],
        compiler_params=_mix_params(1),
        name="gated_delta",
    )(*args)
    return out, s_fin.reshape(n_seq * n_sub, DIRS, N_HEADS, HEAD_DIM, HEAD_DIM)


def kernel(x_prompt, x_sample, state_ret, state_gdn, state_hgrn, state_s5_re, state_s5_im, c, c_ctx, norm1_w, norm2_w, final_norm_w, ada_w, ada_b, in_proj, out_proj, ret_decay_logit, gdn_conv, gdn_a_log, gdn_dt_bias, gdn_norm_w, hg_lb_param, hg_norm_w, s5_a_re, s5_a_im, s5_b_re, s5_b_im, s5_c_re, s5_c_im, s5_log_step, s5_d, s5_glu_w, s5_glu_b, ffn_w1, ffn_w3, ffn_w2):
    lb_soft = jax.nn.softmax(hg_lb_param, axis=0)
    lower_bounds = jnp.cumsum(lb_soft, axis=0) - lb_soft[0]
    rope2 = _rope_tables(DEC_SEQ)

    cvec = jnp.zeros((N_SEQ_ROWS, D_MODEL), F32).at[0].set(c_ctx).at[1:1 + DEC_BATCH].set(c)
    mod_all = _ada(cvec, ada_w, ada_b).reshape(DEPTH, N_SEQ_ROWS, N_MOD, 1, D_MODEL)

    gate0 = 8 * GROUP_WIDTH
    w_in = in_proj.astype(BF16)
    w_main = jnp.concatenate([w_in[:, :, :gate0], w_in[:, :, gate0 + 2 * N_GATES:]], axis=-1)
    w_small = jnp.pad(w_in[:, :, gate0:gate0 + 2 * N_GATES], ((0, 0), (0, 0), (0, PROJ_SMALL - 2 * N_GATES)))
    w_out = out_proj.astype(BF16)
    w1, w3, w2 = ffn_w1.astype(BF16), ffn_w3.astype(BF16), ffn_w2.astype(BF16)

    x = jnp.concatenate([x_prompt.reshape(N_CTX_TOK, D_MODEL), x_sample.reshape(N_LAT_TOK, D_MODEL)], axis=0)
    hgrn_t = jnp.swapaxes(state_hgrn, -1, -2)
    s5_x0 = (state_s5_re.reshape(DEC_BATCH, DEPTH, DIRS, 1, S5_X), state_s5_im.reshape(DEC_BATCH, DEPTH, DIRS, 1, S5_X))
    lat_off = N_CTX_TOK // DEC_SEQ
    tables = jax.vmap(_s5_tables)(s5_a_re, s5_a_im, s5_log_step, s5_b_re, s5_b_im, s5_c_re, s5_c_im)
    ctx_states = []
    for i in range(DEPTH):
        mod = mod_all[i]
        pm, ps = _inproj(x, norm1_w[i][None], mod, w_main, w_small, i)
        ps_t = ps[:, :2 * N_GATES].T
        lg = jax.nn.log_sigmoid(ret_decay_logit[i])
        ctx = (BATCH, SEQ, 0)
        lat = (DEC_BATCH, DEC_SEQ, lat_off)

        ret_c, rs = _retention_pallas(pm, lg, *ctx, None, None, i)
        ret_l, _ = _retention_pallas(pm, lg, *lat, rope2, state_ret, i)
        gdn_args = (pm, ps, ps_t, gdn_conv[i], gdn_a_log[i], gdn_dt_bias[i], gdn_norm_w[i])
        gdn_c, gs = _gdn_pallas(*gdn_args, *ctx, None, i)
        gdn_l, _ = _gdn_pallas(*gdn_args, *lat, state_gdn, i)
        hg_c, hs = _gla_pallas(pm, lower_bounds[i], hg_norm_w[i], *ctx, None, i)
        hg_l, _ = _gla_pallas(pm, lower_bounds[i], hg_norm_w[i], *lat, hgrn_t, i)
        s5_args = (pm, tables, s5_d[i], s5_glu_w[i], s5_glu_b[i])
        s5_c, xr, xi = _s5_pallas(*s5_args, *ctx, None, i)
        s5_l, _, _ = _s5_pallas(*s5_args, *lat, s5_x0, i)
        ctx_states.append((rs, gs, jnp.swapaxes(hs, -1, -2), xr.reshape(BATCH, DIRS, S5_GROUPS, S5_N),
                           xi.reshape(BATCH, DIRS, S5_GROUPS, S5_N)))

        parts = ((ret_c, ret_l), (gdn_c, gdn_l), (hg_c, hg_l), (s5_c, s5_l))
        x = _outproj(parts, w_out, x, mod, i)
        x = _ffn(x, norm2_w[i][None], mod, w1, w3, w2, final_norm_w[None], i)

    y_prompt = x[0].reshape(BATCH, SEQ, D_MODEL)
    y_sample = x[1].reshape(DEC_BATCH, DEC_SEQ, D_MODEL)
    new_states = tuple(jnp.stack([s[j] for s in ctx_states], axis=1) for j in range(5))
    return (y_prompt, y_sample) + new_states
```

```python
import functools
import math

import jax
import jax.numpy as jnp
from jax import lax
from jax.experimental import pallas as pl
from jax.experimental.pallas import tpu as pltpu

F32 = jnp.float32
BF16 = jnp.bfloat16

D_MODEL = 2048
BATCH = 16
SEQ = 256
DEPTH = 2
DEC_BATCH = 8
DEC_SEQ = 1024
GRID_W = 64
HEAD_DIM = 128
GROUP_WIDTH = 512
N_HEADS = 4
S5_CH = 16
S5_GROUPS = 32
S5_N = 64
GDN_CONV = 3
ROPE_BASE = 10000.0
FFN_HIDDEN = 5632
N_MOD = 6
EPS = 1e-6
LB_FLOOR = 1e-30

N_CTX_TOK = BATCH * SEQ
N_LAT_TOK = DEC_BATCH * DEC_SEQ
N_TOK = N_CTX_TOK + N_LAT_TOK
N_SEQ_ROWS = 16
PROJ_MAIN = 14 * GROUP_WIDTH
PROJ_SMALL = 128
V7X_VMEM_BYTES = 64 * 1024 * 1024
VMEM_LIMIT = V7X_VMEM_BYTES - 8 * 1024 * 1024
ADA_TN = 1024
OUT_TN = 1024


def _seq_row(tile, tm):
    n_ctx = N_CTX_TOK // tm
    per_lat = DEC_SEQ // tm
    return jnp.where(tile < n_ctx, 0, 1 + (tile - n_ctx) // per_lat)


def _stream_specs(tm, width, n_col=1):
    n_ctx = N_CTX_TOK // tm

    def ctx_map(i, j):
        return jnp.minimum(i, n_ctx - 1), (jnp.where(i < n_ctx, j, n_col - 1) if n_col > 1 else 0)

    def lat_map(i, j):
        return jnp.maximum(i - n_ctx, 0), (jnp.where(i >= n_ctx, j, 0) if n_col > 1 else 0)

    return pl.BlockSpec((tm, width), ctx_map), pl.BlockSpec((tm, width), lat_map)


def _on_stream(tile, tm, fn):
    n_ctx = N_CTX_TOK // tm
    pl.when(tile < n_ctx)(functools.partial(fn, 0))
    pl.when(tile >= n_ctx)(functools.partial(fn, 1))


def _ada_body(c_ref, w_ref, b_ref, o_ref):
    cv = c_ref[...]
    s = cv * jax.nn.sigmoid(cv)
    o_ref[0] = jnp.dot(s.astype(BF16), w_ref[0].astype(BF16), preferred_element_type=F32) + b_ref[0]


def _ada(cvec, ada_w, ada_b):
    tn = ADA_TN
    n = N_MOD * D_MODEL
    return pl.pallas_call(
        _ada_body,
        out_shape=jax.ShapeDtypeStruct((DEPTH, N_SEQ_ROWS, n), F32),
        grid=(DEPTH, n // tn),
        in_specs=[
            pl.BlockSpec((N_SEQ_ROWS, D_MODEL), lambda l, j: (0, 0)),
            pl.BlockSpec((1, D_MODEL, tn), lambda l, j: (l, 0, j)),
            pl.BlockSpec((1, 1, tn), lambda l, j: (l, 0, j)),
        ],
        out_specs=pl.BlockSpec((1, N_SEQ_ROWS, tn), lambda l, j: (l, 0, j)),
        compiler_params=pltpu.CompilerParams(
            dimension_semantics=("parallel", "parallel"), vmem_limit_bytes=VMEM_LIMIT),
        name="ada_mod",
    )(cvec, ada_w, ada_b.reshape(DEPTH, 1, n))


def _norm_mod(x, nw, sc, sh):
    ms = jnp.mean(x * x, axis=-1, keepdims=True)
    y = x * lax.rsqrt(ms + EPS) * nw
    return y * (1.0 + sc) + sh


PROJ_TM = 1024
PROJ_TN = 1024


def _inproj_body(x_ref, nw_ref, sc_ref, sh_ref, w_ref, ws_ref, o_ref, os_ref, h_ref):
    @pl.when(pl.program_id(1) == 0)
    def _():
        hb = _norm_mod(x_ref[...], nw_ref[...], sc_ref[...], sh_ref[...]).astype(BF16)
        h_ref[...] = hb
        os_ref[...] = jnp.dot(hb, ws_ref[...], preferred_element_type=F32)

    o_ref[...] = jnp.dot(h_ref[...], w_ref[...], preferred_element_type=F32)


def _inproj(x, nw, mod, w_main, w_small, layer):
    tm, tn = PROJ_TM, PROJ_TN
    return pl.pallas_call(
        _inproj_body,
        out_shape=(jax.ShapeDtypeStruct((N_TOK, PROJ_MAIN), F32),
                   jax.ShapeDtypeStruct((N_TOK, PROJ_SMALL), F32)),
        grid=(N_TOK // tm, PROJ_MAIN // tn),
        in_specs=[
            pl.BlockSpec((tm, D_MODEL), lambda i, j: (i, 0)),
            pl.BlockSpec((1, D_MODEL), lambda i, j: (0, 0)),
            pl.BlockSpec((None, None, 1, D_MODEL), lambda i, j: (_seq_row(i, tm), 1, 0, 0)),
            pl.BlockSpec((None, None, 1, D_MODEL), lambda i, j: (_seq_row(i, tm), 0, 0, 0)),
            pl.BlockSpec((None, D_MODEL, tn), lambda i, j: (layer, 0, j)),
            pl.BlockSpec((None, D_MODEL, PROJ_SMALL), lambda i, j: (layer, 0, 0)),
        ],
        out_specs=(pl.BlockSpec((tm, tn), lambda i, j: (i, j)),
                   pl.BlockSpec((tm, PROJ_SMALL), lambda i, j: (i, 0))),
        scratch_shapes=[pltpu.VMEM((tm, D_MODEL), BF16)],
        compiler_params=pltpu.CompilerParams(
            dimension_semantics=("parallel", "arbitrary"), vmem_limit_bytes=VMEM_LIMIT),
        name="in_proj",
    )(x, nw, mod, mod, w_main, w_small)


def _outproj_body(*refs):
    n_mix = 4
    m_refs, (w_ref, x_ref, g_ref, o_ref) = refs[:2 * n_mix], refs[2 * n_mix:]

    def compute(stream):
        acc = None
        for p in range(n_mix):
            part = jnp.dot(m_refs[2 * p + stream][...], w_ref[p * GROUP_WIDTH:(p + 1) * GROUP_WIDTH, :],
                           preferred_element_type=F32)
            acc = part if acc is None else acc + part
        o_ref[...] = x_ref[...] + g_ref[...] * acc

    _on_stream(pl.program_id(0), PROJ_TM, compute)


def _outproj(parts, w, x, mod, layer):
    tm, tn = PROJ_TM, OUT_TN
    in_specs = []
    for _ in parts:
        in_specs += list(_stream_specs(tm, GROUP_WIDTH))
    in_specs += [pl.BlockSpec((None, D_MODEL, tn), lambda i, j: (layer, 0, j)),
                 pl.BlockSpec((tm, tn), lambda i, j: (i, j)),
                 pl.BlockSpec((None, None, 1, tn), lambda i, j: (_seq_row(i, tm), 2, 0, j))]
    return pl.pallas_call(
        _outproj_body,
        out_shape=jax.ShapeDtypeStruct((N_TOK, D_MODEL), F32),
        grid=(N_TOK // tm, D_MODEL // tn),
        in_specs=in_specs,
        out_specs=pl.BlockSpec((tm, tn), lambda i, j: (i, j)),
        compiler_params=pltpu.CompilerParams(
            dimension_semantics=("parallel", "arbitrary"), vmem_limit_bytes=VMEM_LIMIT),
        name="out_proj",
    )(*(a for pair in parts for a in pair), w, x, mod)


FFN_TM = 512
FFN_TH = 512


def _ffn_body(x_ref, nw_ref, sc_ref, sh_ref, g_ref, w1_ref, w3_ref, w2_ref, fw_ref, *rest, final_norm):
    o_refs, (h_ref, acc_ref) = rest[:-2], rest[-2:]
    i, k = pl.program_id(0), pl.program_id(1)

    @pl.when(k == 0)
    def _():
        h_ref[...] = _norm_mod(x_ref[...], nw_ref[...], sc_ref[...], sh_ref[...]).astype(BF16)
        acc_ref[...] = jnp.zeros_like(acc_ref)

    h = h_ref[...]
    a = jnp.dot(h, w1_ref[...], preferred_element_type=F32)
    b = jnp.dot(h, w3_ref[...], preferred_element_type=F32)
    g = (a * jax.nn.sigmoid(a) * b).astype(BF16)
    acc_ref[...] += jnp.dot(g, w2_ref[...], preferred_element_type=F32)

    @pl.when(k == pl.num_programs(1) - 1)
    def _():
        y = x_ref[...] + g_ref[...] * acc_ref[...]
        if not final_norm:
            o_refs[0][...] = y
        else:
            ms = jnp.mean(y * y, axis=-1, keepdims=True)
            y = y * lax.rsqrt(ms + EPS) * fw_ref[...]

            def write(stream):
                o_refs[stream][...] = y

            _on_stream(i, FFN_TM, write)


def _ffn(x, nw, mod, w1, w3, w2, fw, layer):
    final_norm = layer == DEPTH - 1
    tm, th = FFN_TM, FFN_TH
    if final_norm:
        out_shape = (jax.ShapeDtypeStruct((N_CTX_TOK, D_MODEL), F32), jax.ShapeDtypeStruct((N_LAT_TOK, D_MODEL), F32))
        out_specs = _stream_specs(tm, D_MODEL)
    else:
        out_shape = jax.ShapeDtypeStruct((N_TOK, D_MODEL), F32)
        out_specs = pl.BlockSpec((tm, D_MODEL), lambda i, k: (i, 0))
    return pl.pallas_call(
        functools.partial(_ffn_body, final_norm=final_norm),
        out_shape=out_shape,
        grid=(N_TOK // tm, FFN_HIDDEN // th),
        in_specs=[
            pl.BlockSpec((tm, D_MODEL), lambda i, k: (i, 0)),
            pl.BlockSpec((1, D_MODEL), lambda i, k: (0, 0)),
            pl.BlockSpec((None, None, 1, D_MODEL), lambda i, k: (_seq_row(i, tm), 4, 0, 0)),
            pl.BlockSpec((None, None, 1, D_MODEL), lambda i, k: (_seq_row(i, tm), 3, 0, 0)),
            pl.BlockSpec((None, None, 1, D_MODEL), lambda i, k: (_seq_row(i, tm), 5, 0, 0)),
            pl.BlockSpec((None, D_MODEL, th), lambda i, k: (layer, 0, k)),
            pl.BlockSpec((None, D_MODEL, th), lambda i, k: (layer, 0, k)),
            pl.BlockSpec((None, th, D_MODEL), lambda i, k: (layer, k, 0)),
            pl.BlockSpec((1, D_MODEL), lambda i, k: (0, 0)),
        ],
        out_specs=out_specs,
        scratch_shapes=[pltpu.VMEM((tm, D_MODEL), BF16), pltpu.VMEM((tm, D_MODEL), F32)],
        compiler_params=pltpu.CompilerParams(
            dimension_semantics=("arbitrary", "arbitrary"), vmem_limit_bytes=VMEM_LIMIT),
        name="ffn",
    )(x, nw, mod, mod, mod, w1, w3, w2, fw)


def _axial_rope(l):
    n_rows = l // GRID_W
    t_row = jnp.repeat(jnp.arange(n_rows, dtype=F32), GRID_W)
    t_col = jnp.tile(jnp.arange(GRID_W, dtype=F32), n_rows)
    n_freq = HEAD_DIM // 4
    inv = ROPE_BASE ** (-jnp.arange(n_freq, dtype=F32) / n_freq)
    ang = jnp.concatenate([t_row[:, None] * inv, t_col[:, None] * inv], axis=-1)
    return jnp.cos(ang), jnp.sin(ang)


MIX_CHUNK = 128
DIRS = 2
NT_DIMS = (((1,), (1,)), ((), ()))
TN_DIMS = (((0,), (0,)), ((), ()))


def _dot(a, b):
    return jnp.dot(a.astype(BF16), b.astype(BF16), preferred_element_type=F32)


def _dot_nt(a, b):
    return lax.dot_general(a.astype(BF16), b.astype(BF16), NT_DIMS, preferred_element_type=F32)


def _dot_tn(a, b):
    return lax.dot_general(a.astype(BF16), b.astype(BF16), TN_DIMS, preferred_element_type=F32)


def _silu(x):
    return x * jax.nn.sigmoid(x)


def _seq_spec(seq_len, width, row_off, col_blk, pipeline_mode=None):
    return pl.BlockSpec((seq_len, width), lambda b: (row_off + b, col_blk), pipeline_mode=pipeline_mode)


def _mix_params(n_par):
    return pltpu.CompilerParams(dimension_semantics=("parallel",) * n_par, vmem_limit_bytes=VMEM_LIMIT)


def _ret_body(*refs, n_chunks, use_rope, has_s0):
    it = iter(refs)
    lg_ref, q_ref, k_ref, v_ref, g_ref = (next(it) for _ in range(5))
    cos_ref, sin_ref = (next(it), next(it)) if use_rope else (None, None)
    s0_ref = next(it) if has_s0 else None
    o_ref, sf_ref, acc_ref, s_ref, intra_ref, qd_ref, kd_ref, cd_ref = (next(it) for _ in range(8))
    C, H = MIX_CHUNK, N_HEADS

    row = lax.broadcasted_iota(jnp.int32, (C, C), 0)
    col = lax.broadcasted_iota(jnp.int32, (C, C), 1)
    rel = (row - col).astype(F32)
    pos = lax.broadcasted_iota(jnp.int32, (C, HEAD_DIM), 0).astype(F32)
    for d in range(DIRS):
        for h in range(H):
            lg = lg_ref[d, h]
            if d == 0:
                intra_ref[d, h] = jnp.where(rel >= 0, jnp.exp(jnp.maximum(rel, 0.0) * lg), 0.0)
                qd_ref[d, h] = jnp.exp((pos + 1.0) * lg)
                kd_ref[d, h] = jnp.exp((C - 1.0 - pos) * lg)
            else:
                intra_ref[d, h] = jnp.where(rel <= 0, jnp.exp(jnp.maximum(-rel, 0.0) * lg), 0.0)
                qd_ref[d, h] = jnp.exp((C - pos) * lg)
                kd_ref[d, h] = jnp.exp(pos * lg)
            cd_ref[d, h] = jnp.exp(jnp.full((C, HEAD_DIM), C, F32) * lg)
    if has_s0:
        s_ref[...] = s0_ref[...]
    else:
        s_ref[...] = jnp.zeros_like(s_ref)
    acc_ref[...] = jnp.zeros_like(acc_ref)

    units = [(d, h) for d in range(DIRS) for h in range(H)]

    def chunk_step(n):
        rows_d = [slice(c * C, (c + 1) * C) for c in (n, n_chunks - 1 - n)]
        q_l, k_l, v_l, sc_l = [], [], [], []
        for d, h in units:
            rows, cols = rows_d[d], slice(h * HEAD_DIM, (h + 1) * HEAD_DIM)
            q, k = q_ref[rows, cols], k_ref[rows, cols]
            if use_rope:
                cs, sn = cos_ref[rows, :], sin_ref[rows, :]
                q = q * cs + pltpu.roll(q, HEAD_DIM // 2, 1) * sn
                k = k * cs + pltpu.roll(k, HEAD_DIM // 2, 1) * sn
            k = k * HEAD_DIM ** -0.5
            q_l.append(q)
            k_l.append(k)
            v_l.append(v_ref[rows, cols].astype(BF16))
            sc_l.append(_dot_nt(q, k) * intra_ref[d, h])
        for i, (d, h) in enumerate(units):
            cols = slice(h * HEAD_DIM, (h + 1) * HEAD_DIM)
            s = s_ref[d, h]
            acc_ref[rows_d[d], cols] += _dot(jnp.concatenate([sc_l[i], q_l[i] * qd_ref[d, h]], axis=1),
                                             jnp.concatenate([v_l[i], s.astype(BF16)], axis=0))
            s_ref[d, h] = cd_ref[d, h] * s + _dot_tn(k_l[i] * kd_ref[d, h], v_l[i])

    for n in range(n_chunks):
        chunk_step(n)
    sf_ref[...] = s_ref[...]

    for n in range(n_chunks):
        rows = slice(n * C, (n + 1) * C)
        for h in range(H):
            cols = slice(h * HEAD_DIM, (h + 1) * HEAD_DIM)
            o = acc_ref[rows, cols]
            mu = jnp.mean(o, axis=-1, keepdims=True)
            oc = o - mu
            y = oc * lax.rsqrt(jnp.mean(oc * oc, axis=-1, keepdims=True) + EPS)
            o_ref[rows, cols] = (y * _silu(g_ref[rows, cols])).astype(BF16)


def _retention_pallas(pm, log_gamma, n_seq, seq_len, row_off, rope2, s0, layer):
    use_rope, has_s0 = rope2 is not None, s0 is not None
    st_shape = (DIRS, N_HEADS, HEAD_DIM, HEAD_DIM)
    in_specs = [pl.BlockSpec(memory_space=pltpu.SMEM)]
    in_specs += [_seq_spec(seq_len, GROUP_WIDTH, row_off, cb) for cb in range(4)]
    args = [log_gamma, pm, pm, pm, pm]
    if use_rope:
        in_specs += [pl.BlockSpec((seq_len, HEAD_DIM), lambda b: (0, 0))] * 2
        args += list(rope2)
    if has_s0:
        in_specs.append(pl.BlockSpec((None, None) + st_shape, lambda b: (b, layer, 0, 0, 0, 0)))
        args.append(s0)
    return pl.pallas_call(
        functools.partial(_ret_body, n_chunks=seq_len // MIX_CHUNK, use_rope=use_rope, has_s0=has_s0),
        out_shape=(jax.ShapeDtypeStruct((n_seq * seq_len, GROUP_WIDTH), BF16),
                   jax.ShapeDtypeStruct((n_seq,) + st_shape, F32)),
        grid=(n_seq,),
        in_specs=in_specs,
        out_specs=(pl.BlockSpec((seq_len, GROUP_WIDTH), lambda b: (b, 0)),
                   pl.BlockSpec((None,) + st_shape, lambda b: (b, 0, 0, 0, 0))),
        scratch_shapes=[pltpu.VMEM((seq_len, GROUP_WIDTH), F32), pltpu.VMEM(st_shape, F32),
                        pltpu.VMEM((DIRS, N_HEADS, MIX_CHUNK, MIX_CHUNK), F32),
                        pltpu.VMEM((DIRS, N_HEADS, MIX_CHUNK, HEAD_DIM), F32),
                        pltpu.VMEM((DIRS, N_HEADS, MIX_CHUNK, HEAD_DIM), F32),
                        pltpu.VMEM((DIRS, N_HEADS, MIX_CHUNK, HEAD_DIM), F32)],
        compiler_params=_mix_params(1),
        name="retention",
    )(*args)


def _rope_tables(l):
    cos, sin = _axial_rope(l)
    return jnp.concatenate([cos, cos], axis=-1), jnp.concatenate([-sin, sin], axis=-1)


S5_HALF_G = S5_GROUPS // 2
S5_HALF_U = S5_HALF_G * S5_CH
S5_HALF_X = S5_HALF_G * S5_N
S5_X = S5_GROUPS * S5_N
S5_TC = 512
S5_BLK = 8
S5_TABS = 4


def _s5_tables(a_re, a_im, log_step, b_re, b_im, c_re, c_im):
    dt = jnp.exp(log_step)[..., None]
    mag = jnp.exp(a_re * dt)
    ab_re = mag * jnp.cos(a_im * dt)
    ab_im = mag * jnp.sin(a_im * dt)
    den = a_re * a_re + a_im * a_im
    nr = ab_re - 1.0
    f_re = (nr * a_re + ab_im * a_im) / den
    f_im = (ab_im * a_re - nr * a_im) / den
    bb_re = f_re[..., None] * b_re - f_im[..., None] * b_im
    bb_im = f_re[..., None] * b_im + f_im[..., None] * b_re
    eye = jnp.eye(S5_HALF_G, dtype=F32)

    def in_mat(bb):
        bb = bb.reshape(DIRS, 2, S5_HALF_G, S5_N, S5_CH)
        return jnp.einsum('dhgnc,gk->dhgckn', bb, eye).reshape(DIRS, 2, S5_HALF_U, S5_HALF_X)

    def out_mat(cc):
        cc = cc.reshape(DIRS, 2, S5_HALF_G, S5_CH, S5_N)
        return jnp.einsum('dhgcn,gk->dhgnkc', cc, eye).reshape(DIRS, 2, S5_HALF_X, S5_HALF_U)

    bm = jnp.concatenate([in_mat(bb_re), in_mat(bb_im)], axis=-1).astype(BF16)
    cm = jnp.concatenate([out_mat(c_re), -out_mat(c_im)], axis=-2).astype(BF16)
    t = jnp.arange(S5_BLK, dtype=F32)
    order = jnp.stack([t, S5_BLK - 1.0 - t])
    shifts = 2.0 ** jnp.arange(S5_TABS - 1, dtype=F32)
    expo = jnp.concatenate([jnp.where(order[:, None, :] >= shifts[None, :, None], shifts[None, :, None], jnp.nan),
                            order[:, None, :] + 1.0], axis=1)
    live = ~jnp.isnan(expo)
    e = jnp.where(live, expo, 0.0)[..., None]
    adt_re = (a_re * dt).reshape(DIRS, 1, 1, S5_X)
    adt_im = (a_im * dt).reshape(DIRS, 1, 1, S5_X)
    pmag = jnp.where(live[..., None], jnp.exp(e * adt_re), 0.0)
    pw_re = pmag * jnp.cos(e * adt_im)
    pw_im = pmag * jnp.sin(e * adt_im)
    return bm, cm, pw_re, pw_im


def _gelu_tanh(x):
    return 0.5 * x * (1.0 + jnp.tanh(math.sqrt(2.0 / math.pi) * (x + 0.044715 * (x * x * x))))


def _s5_body(*refs, seq_len, has_s0):
    it = iter(refs)
    u_ref, bm_ref, cm_ref, pwr_ref, pwi_ref, d_ref, gw_ref, gb_ref = (next(it) for _ in range(8))
    x0r_ref, x0i_ref = (next(it), next(it)) if has_s0 else (None, None)
    o_ref, sfr_ref, sfi_ref, y_ref, xr_ref, xi_ref, xb_ref = (next(it) for _ in range(7))
    tc = min(seq_len, S5_TC)
    n_tiles = seq_len // tc
    n_pair = tc // (2 * S5_BLK)

    y_ref[...] = d_ref[...] * u_ref[...]
    unit = 0
    for d in range(DIRS):
        last = S5_BLK - 1 if d == 0 else 0
        for hf in range(2):
            xs = slice(hf * S5_HALF_X, (hf + 1) * S5_HALF_X)
            us = slice(hf * S5_HALF_U, (hf + 1) * S5_HALF_U)

            def scan_block(x_in, carry, d=d, xs=xs, last=last):
                xr, xi = x_in
                car_re, car_im = carry
                for k in range(S5_TABS - 1):
                    s = 1 << k
                    shift = s if d == 0 else S5_BLK - s
                    p_re, p_im = pwr_ref[d, k, :, xs], pwi_ref[d, k, :, xs]
                    sr, si = pltpu.roll(xr, shift, 0), pltpu.roll(xi, shift, 0)
                    xr, xi = xr + p_re * sr - p_im * si, xi + p_re * si + p_im * sr
                p_re, p_im = pwr_ref[d, S5_TABS - 1, :, xs], pwi_ref[d, S5_TABS - 1, :, xs]
                xr, xi = xr + p_re * car_re - p_im * car_im, xi + p_re * car_im + p_im * car_re
                return (xr, xi), (xr[last:last + 1, :], xi[last:last + 1, :])

            if has_s0:
                carry = (x0r_ref[d, :, xs], x0i_ref[d, :, xs])
            else:
                carry = (jnp.zeros((1, S5_HALF_X), F32), jnp.zeros((1, S5_HALF_X), F32))
            for i in range(n_tiles):
                slot = unit % 2
                unit += 1
                tile = i if d == 0 else n_tiles - 1 - i
                rows_t = slice(tile * tc, (tile + 1) * tc)
                bu = _dot(u_ref[rows_t, us], bm_ref[d, hf])
                xr_ref[slot] = bu[:, :S5_HALF_X]
                xi_ref[slot] = bu[:, S5_HALF_X:]
                for j in range(n_pair):
                    pair = j if d == 0 else n_pair - 1 - j
                    rows = slice(pair * 2 * S5_BLK, (pair + 1) * 2 * S5_BLK)
                    xr2, xi2 = xr_ref[slot, rows, :], xi_ref[slot, rows, :]
                    halves = [(xr2[:S5_BLK], xi2[:S5_BLK]), (xr2[S5_BLK:], xi2[S5_BLK:])]
                    out = [None, None]
                    for idx in ((0, 1) if d == 0 else (1, 0)):
                        out[idx], carry = scan_block(halves[idx], carry)
                    xb_ref[slot, rows, :S5_HALF_X] = jnp.concatenate([out[0][0], out[1][0]], axis=0).astype(BF16)
                    xb_ref[slot, rows, S5_HALF_X:] = jnp.concatenate([out[0][1], out[1][1]], axis=0).astype(BF16)
                y_ref[rows_t, us] += jnp.dot(xb_ref[slot], cm_ref[d, hf], preferred_element_type=F32)
            sfr_ref[d, :, xs] = carry[0]
            sfi_ref[d, :, xs] = carry[1]

    z = _gelu_tanh(y_ref[...])
    o_ref[...] = (z * jax.nn.sigmoid(_dot(z, gw_ref[...]) + gb_ref[...])).astype(BF16)


def _s5_pallas(pm, tables, s5_d, glu_w, glu_b, n_seq, seq_len, row_off, x0, layer):
    bm, cm, pw_re, pw_im = tables
    has_s0 = x0 is not None
    full = lambda shape: pl.BlockSpec(shape, lambda b: (0,) * len(shape))
    of_layer = lambda t: pl.BlockSpec((None,) + t.shape[1:], lambda b: (layer,) + (0,) * (t.ndim - 1))
    in_specs = [_seq_spec(seq_len, GROUP_WIDTH, row_off, 13),
                of_layer(bm), of_layer(cm), of_layer(pw_re), of_layer(pw_im),
                full((1, GROUP_WIDTH)), full((GROUP_WIDTH, GROUP_WIDTH)), full((1, GROUP_WIDTH))]
    args = [pm, bm, cm, pw_re, pw_im, s5_d.reshape(1, GROUP_WIDTH), glu_w.astype(BF16),
            glu_b.reshape(1, GROUP_WIDTH)]
    if has_s0:
        in_specs += [pl.BlockSpec((None, None, DIRS, 1, S5_X), lambda b: (b, layer, 0, 0, 0))] * 2
        args += list(x0)
    st = jax.ShapeDtypeStruct((n_seq, DIRS, 1, S5_X), F32)
    st_spec = pl.BlockSpec((None, DIRS, 1, S5_X), lambda b: (b, 0, 0, 0))
    tc = min(seq_len, S5_TC)
    return pl.pallas_call(
        functools.partial(_s5_body, seq_len=seq_len, has_s0=has_s0),
        out_shape=(jax.ShapeDtypeStruct((n_seq * seq_len, GROUP_WIDTH), BF16), st, st),
        grid=(n_seq,),
        in_specs=in_specs,
        out_specs=(pl.BlockSpec((seq_len, GROUP_WIDTH), lambda b: (b, 0)), st_spec, st_spec),
        scratch_shapes=[pltpu.VMEM((seq_len, GROUP_WIDTH), F32),
                        pltpu.VMEM((2, tc, S5_HALF_X), F32), pltpu.VMEM((2, tc, S5_HALF_X), F32),
                        pltpu.VMEM((2, tc, 2 * S5_HALF_X), BF16)],
        compiler_params=_mix_params(1),
        name="s5",
    )(*args)


GLA_LEVELS = 7


def _chunk_cumsum_rows(x, rowi):
    s = 1
    while s < MIX_CHUNK:
        x = x + jnp.where(rowi >= s, pltpu.roll(x, s, 0), 0.0)
        s *= 2
    return x


def _gla_body(*refs, n_chunks, has_s0):
    it = iter(refs)
    q_ref, f0_ref, f1_ref, i_ref, g_ref, lb_ref, nw_ref = (next(it) for _ in range(7))
    s0_ref = next(it) if has_s0 else None
    o_ref, sf_ref, acc_ref, s_ref, code_ref = (next(it) for _ in range(5))
    C, H = MIX_CHUNK, N_HEADS
    f_refs = (f0_ref, f1_ref)

    rowi = lax.broadcasted_iota(jnp.int32, (C, HEAD_DIM), 0)
    ri = lax.broadcasted_iota(jnp.int32, (C, C), 0)
    ci = lax.broadcasted_iota(jnp.int32, (C, C), 1)
    top_bit = 31 - lax.clz(ri ^ ci)
    code_ref[...] = jnp.where(ri > ci, top_bit, jnp.where(ri < ci, -1 - top_bit, GLA_LEVELS))
    if has_s0:
        s_ref[...] = s0_ref[...]
    else:
        s_ref[...] = jnp.zeros_like(s_ref)
    acc_ref[...] = jnp.zeros_like(acc_ref)

    def chunk_step(n, carry):
        for d in range(DIRS):
            c = n if d == 0 else n_chunks - 1 - n
            rows = pl.ds(pl.multiple_of(c * C, C), C)
            for h in range(H):
                cols = slice(h * HEAD_DIM, (h + 1) * HEAD_DIM)
                code = code_ref[...]
                q = _silu(q_ref[rows, cols]) * HEAD_DIM ** -0.5
                v = i_ref[rows, cols].astype(BF16)
                fx = f_refs[d][rows, cols]
                lb = lb_ref[d:d + 1, cols]
                sig = 1.0 / (1.0 + jnp.exp(-fx))
                logf = jnp.log2(jnp.maximum(lb, LB_FLOOR) + (1.0 - lb) * sig)
                k = (1.0 - lb) * (1.0 - sig)
                cum = _chunk_cumsum_rows(logf, rowi)
                own = cum
                attn = jnp.where(code == GLA_LEVELS, _dot_nt(q, k), 0.0)
                for lvl in range(GLA_LEVELS):
                    m = 1 << lvl
                    prev = pltpu.roll(own, m, 0)
                    pre = jnp.minimum(cum - prev, 0.0)
                    suf = own - cum
                    if d == 0:
                        sc = _dot_nt(q * jnp.exp2(pre), k * jnp.exp2(suf))
                        hit = code == lvl
                    else:
                        sc = _dot_nt(q * jnp.exp2(suf + logf), k * jnp.exp2(jnp.minimum(pre - logf, 0.0)))
                        hit = code == -1 - lvl
                    attn = jnp.where(hit, sc, attn)
                    own = jnp.where(((rowi >> lvl) & 1) == 0, pltpu.roll(own, C - m, 0), own)
                tot = own
                st = s_ref[d, h]
                if d == 0:
                    q_in, k_out = q * jnp.exp2(cum), k * jnp.exp2(tot - cum)
                else:
                    q_in, k_out = q * jnp.exp2(tot - cum + logf), k * jnp.exp2(cum - logf)
                acc_ref[rows, cols] += _dot(attn, v) + _dot_nt(q_in, st)
                s_ref[d, h] = jnp.exp2(tot[0:1, :]) * st + _dot_tn(v, k_out)
        return carry

    lax.fori_loop(0, n_chunks, chunk_step, 0)
    sf_ref[...] = s_ref[...]

    def finish(n, carry):
        rows = pl.ds(pl.multiple_of(n * C, C), C)
        for h in range(H):
            cols = slice(h * HEAD_DIM, (h + 1) * HEAD_DIM)
            o = acc_ref[rows, cols]
            y = o * lax.rsqrt(jnp.mean(o * o, axis=-1, keepdims=True) + EPS) * nw_ref[...]
            o_ref[rows, cols] = (y * _silu(g_ref[rows, cols])).astype(BF16)
        return carry

    lax.fori_loop(0, n_chunks, finish, 0)


def _gla_pallas(pm, lower_bound, norm_w, n_seq, seq_len, row_off, s0_t, layer):
    has_s0 = s0_t is not None
    st_shape = (DIRS, N_HEADS, HEAD_DIM, HEAD_DIM)
    in_specs = [_seq_spec(seq_len, GROUP_WIDTH, row_off, cb) for cb in (8, 9, 10, 11, 12)]
    in_specs += [pl.BlockSpec((DIRS, GROUP_WIDTH), lambda b: (0, 0)), pl.BlockSpec((1, HEAD_DIM), lambda b: (0, 0))]
    args = [pm] * 5 + [lower_bound, norm_w.reshape(1, HEAD_DIM)]
    if has_s0:
        in_specs.append(pl.BlockSpec((None, None) + st_shape, lambda b: (b, layer, 0, 0, 0, 0)))
        args.append(s0_t)
    return pl.pallas_call(
        functools.partial(_gla_body, n_chunks=seq_len // MIX_CHUNK, has_s0=has_s0),
        out_shape=(jax.ShapeDtypeStruct((n_seq * seq_len, GROUP_WIDTH), BF16),
                   jax.ShapeDtypeStruct((n_seq,) + st_shape, F32)),
        grid=(n_seq,),
        in_specs=in_specs,
        out_specs=(pl.BlockSpec((seq_len, GROUP_WIDTH), lambda b: (b, 0)),
                   pl.BlockSpec((None,) + st_shape, lambda b: (b, 0, 0, 0, 0))),
        scratch_shapes=[pltpu.VMEM((seq_len, GROUP_WIDTH), F32), pltpu.VMEM(st_shape, F32),
                        pltpu.VMEM((MIX_CHUNK, MIX_CHUNK), jnp.int32)],
        compiler_params=_mix_params(1),
        name="hgrn2",
    )(*args)


GDN_BLOCK_BITS = 4
GDN_MERGES = 3
N_GATES = DIRS * N_HEADS


def _softplus(x):
    return jnp.maximum(x, 0.0) + jnp.log1p(jnp.exp(-jnp.abs(x)))


def _gdn_body(*refs, seq_len, n_sub, has_s0):
    it = iter(refs)
    (q_ref, k_ref, v_ref, g_ref, ps_ref, pst_ref, cw_ref, prow_ref, pcol_ref, nw_ref) = (next(it) for _ in range(10))
    s0_ref = next(it) if has_s0 else None
    (o_ref, sf_ref, acc_ref, s_ref, qn_ref, kn_ref, vn_ref, gate_ref, gatet_ref, blev_ref) = (
        next(it) for _ in range(10))
    C, H, L = MIX_CHUNK, N_HEADS, seq_len
    R = n_sub * L
    n_chunks = L // C
    w = GROUP_WIDTH

    rowi = lax.broadcasted_iota(jnp.int32, (C, HEAD_DIM), 0)
    lane = lax.broadcasted_iota(jnp.int32, (C, HEAD_DIM), 1)
    lane_t = lax.broadcasted_iota(jnp.int32, (2 * N_GATES, C), 1)
    row_t = lax.broadcasted_iota(jnp.int32, (2 * N_GATES, C), 0)

    def prepare(c):
        rows = slice(c * C, (c + 1) * C)
        first, final = c % n_chunks == 0, c % n_chunks == n_chunks - 1
        for part, (src, dst) in enumerate(((q_ref, qn_ref), (k_ref, kn_ref), (v_ref, vn_ref))):
            for h in range(H):
                cols = slice(h * HEAD_DIM, (h + 1) * HEAD_DIM)
                wc = slice(part * w + h * HEAD_DIM, part * w + (h + 1) * HEAD_DIM)
                x = src[rows, cols]
                before = jnp.zeros((1, HEAD_DIM), F32) if first else src[c * C - 1:c * C, cols]
                after = jnp.zeros((1, HEAD_DIM), F32) if final else src[(c + 1) * C:(c + 1) * C + 1, cols]
                x_prev = jnp.where(rowi == 0, before, pltpu.roll(x, 1, 0))
                x_next = jnp.where(rowi == C - 1, after, pltpu.roll(x, C - 1, 0))
                y = _silu(cw_ref[0:1, wc] * x_prev + cw_ref[1:2, wc] * x + cw_ref[2:3, wc] * x_next)
                if part < 2:
                    y = y * lax.rsqrt(jnp.sum(y * y, axis=-1, keepdims=True) + EPS)
                if part == 0:
                    y = y * HEAD_DIM ** -0.5
                dst[rows, cols] = y

        a = ps_ref[rows, :]
        la = -jnp.exp(prow_ref[0:1, :]) * _softplus(a + prow_ref[1:2, :])
        pre = _chunk_cumsum_rows(la, rowi)
        tot = jnp.broadcast_to(pre[C - 1:C, :], (C, HEAD_DIM))
        g = jnp.where(lane < N_HEADS, pre, tot - pre + la)
        gate_ref[rows, :] = jnp.where(lane < N_GATES, g,
                                      jnp.where(lane < 2 * N_GATES, jax.nn.sigmoid(a),
                                                pltpu.roll(tot, 2 * N_GATES, 1)))
        at = pst_ref[:, rows]
        lat = -jnp.exp(pcol_ref[:, 0:1]) * _softplus(at + pcol_ref[:, 1:2])
        pre_t, s = lat, 1
        while s < C:
            pre_t = pre_t + jnp.where(lane_t >= s, pltpu.roll(pre_t, s, 1), 0.0)
            s *= 2
        tot_t = jnp.broadcast_to(pre_t[:, C - 1:C], (2 * N_GATES, C))
        g_t = jnp.where(row_t < N_HEADS, pre_t, tot_t - pre_t + lat)
        gatet_ref[:, rows] = jnp.where(row_t < N_GATES, g_t, jax.nn.sigmoid(at))

    ri = lax.broadcasted_iota(jnp.int32, (C, C), 0)
    ci = lax.broadcasted_iota(jnp.int32, (C, C), 1)
    bx = (ri >> GDN_BLOCK_BITS) ^ (ci >> GDN_BLOCK_BITS)
    blev_ref[...] = jnp.where(bx == 0, 0, 32 - lax.clz(bx))
    if has_s0:
        for s in range(n_sub):
            s_ref[s * DIRS:(s + 1) * DIRS] = s0_ref[s]
    else:
        s_ref[...] = jnp.zeros_like(s_ref)
    acc_ref[...] = jnp.zeros_like(acc_ref)

    units = [(s, d, h) for s in range(n_sub) for d in range(DIRS) for h in range(H)]

    def chunk_step(n):
        eye = jnp.where(ri == ci, 1.0, 0.0)
        blev = blev_ref[...]
        rows_sd = [[slice(s * L + c * C, s * L + (c + 1) * C) for c in (n, n_chunks - 1 - n)] for s in range(n_sub)]
        incl_d = [ri >= ci, ri <= ci]
        strict_d = [ri > ci, ri < ci]
        q_l, et_l, ek_l, rhs_l, a_l, attn_l = ([] for _ in range(6))
        for s, d, h in units:
            rows, cols, cg = rows_sd[s][d], slice(h * HEAD_DIM, (h + 1) * HEAD_DIM), d * N_HEADS + h
            q, k, v = qn_ref[rows, cols], kn_ref[rows, cols], vn_ref[rows, cols]
            g_i = jnp.broadcast_to(gate_ref[rows, cg:cg + 1], (C, C))
            b_i = jnp.broadcast_to(gate_ref[rows, N_GATES + cg:N_GATES + cg + 1], (C, C))
            tot = jnp.broadcast_to(gate_ref[rows, 2 * N_GATES + cg:2 * N_GATES + cg + 1], (C, C))
            g_j = gatet_ref[cg:cg + 1, rows]
            decay = jnp.where(incl_d[d], jnp.exp(jnp.minimum(g_i - g_j, 0.0)), 0.0)
            e_g = jnp.exp(g_i)
            kb = k.astype(BF16)
            a_l.append(jnp.where(strict_d[d], _dot_nt(kb, kb) * b_i * decay, 0.0))
            attn_l.append(_dot_nt(q, kb) * decay)
            rhs_l.append(jnp.concatenate([v * b_i, k * (b_i * e_g)], axis=1).astype(BF16))
            q_l.append(q * e_g)
            ek_l.append(k * jnp.exp(tot - g_i))
            et_l.append(jnp.exp(tot))
        b_l = [-jnp.where(blev == 0, a, 0.0) for a in a_l]
        p_l = [eye + b for b in b_l]
        for _ in range(GDN_BLOCK_BITS - 1):
            b_l = [_dot(b, b) for b in b_l]
            p_l = [p + _dot(p, b) for p, b in zip(p_l, b_l)]
        for lvl in range(1, GDN_MERGES + 1):
            ep_l = [_dot(jnp.where(blev == lvl, a, 0.0), p) for a, p in zip(a_l, p_l)]
            p_l = [p - _dot(p, ep) for p, ep in zip(p_l, ep_l)]
        uw_l = [_dot(p, rhs) for p, rhs in zip(p_l, rhs_l)]
        s_l = [s_ref[s * DIRS + d, h] for s, d, h in units]
        vn_l = [uw[:, :HEAD_DIM] - _dot(uw[:, HEAD_DIM:], st) for uw, st in zip(uw_l, s_l)]
        for i, (s, d, h) in enumerate(units):
            cols = slice(h * HEAD_DIM, (h + 1) * HEAD_DIM)
            acc_ref[rows_sd[s][d], cols] += _dot(jnp.concatenate([q_l[i], attn_l[i]], axis=1),
                                                 jnp.concatenate([s_l[i], vn_l[i]], axis=0))
            s_ref[s * DIRS + d, h] = s_l[i] * et_l[i] + _dot_tn(ek_l[i], vn_l[i])

    for n in range(n_chunks):
        for c in sorted({n, n_chunks - 1 - n}):
            if min(c, n_chunks - 1 - c) == n:
                for s in range(n_sub):
                    prepare(s * n_chunks + c)
        chunk_step(n)
    sf_ref[...] = s_ref[...]

    for n in range(R // C):
        rows = slice(n * C, (n + 1) * C)
        for h in range(H):
            cols = slice(h * HEAD_DIM, (h + 1) * HEAD_DIM)
            o = acc_ref[rows, cols]
            y = o * lax.rsqrt(jnp.mean(o * o, axis=-1, keepdims=True) + EPS) * nw_ref[...]
            o_ref[rows, cols] = (y * _silu(g_ref[rows, cols])).astype(BF16)


GDN_CTX_SUB = 2
GDN_LAT_SUB = 1
GDN_DOUBLE_BUFFER_BYTES = 2 * 1024 * 1024


def _gdn_pallas(pm, ps, ps_t, conv_w, a_log, dt_bias, norm_w, n_seq, seq_len, row_off, s0, layer):
    has_s0 = s0 is not None
    n_sub = GDN_LAT_SUB if has_s0 else GDN_CTX_SUB
    n_seq, seq_len, row_off = n_seq // n_sub, seq_len * n_sub, row_off // n_sub
    st_shape = (n_sub * DIRS, N_HEADS, HEAD_DIM, HEAD_DIM)
    par = jnp.stack([a_log.reshape(N_GATES), dt_bias.reshape(N_GATES)])
    par_row = jnp.pad(par, ((0, 0), (0, HEAD_DIM - N_GATES)))
    par_col = jnp.pad(par.T, ((0, N_GATES), (0, 0)))
    full = lambda shape: pl.BlockSpec(shape, lambda b: (0,) * len(shape))
    big_mode = pl.Buffered(1) if seq_len * GROUP_WIDTH * 4 > GDN_DOUBLE_BUFFER_BYTES else None
    in_specs = [_seq_spec(seq_len, GROUP_WIDTH, row_off, cb, big_mode) for cb in (4, 5, 6, 7)]
    in_specs += [_seq_spec(seq_len, PROJ_SMALL, row_off, 0),
                 pl.BlockSpec((2 * N_GATES, seq_len), lambda b: (0, row_off + b)),
                 full((GDN_CONV, 3 * GROUP_WIDTH)), full((2, HEAD_DIM)), full((2 * N_GATES, 2)), full((1, HEAD_DIM))]
    args = [pm] * 4 + [ps, ps_t, conv_w, par_row, par_col, norm_w.reshape(1, HEAD_DIM)]
    if has_s0:
        in_specs.append(pl.BlockSpec((n_sub, None, DIRS, N_HEADS, HEAD_DIM, HEAD_DIM),
                                     lambda b: (b, layer, 0, 0, 0, 0)))
        args.append(s0)
    seq_f32 = lambda width: pltpu.VMEM((seq_len, width), F32)
    out, s_fin = pl.pallas_call(
        functools.partial(_gdn_body, seq_len=seq_len // n_sub, n_sub=n_sub, has_s0=has_s0),
        out_shape=(jax.ShapeDtypeStruct((n_seq * seq_len, GROUP_WIDTH), BF16),
                   jax.ShapeDtypeStruct((n_seq,) + st_shape, F32)),
        grid=(n_seq,),
        in_specs=in_specs,
        out_specs=(pl.BlockSpec((seq_len, GROUP_WIDTH), lambda b: (b, 0)),
                   pl.BlockSpec((None,) + st_shape, lambda b: (b, 0, 0, 0, 0))),
        scratch_shapes=[seq_f32(GROUP_WIDTH), pltpu.VMEM(st_shape, F32),
                        seq_f32(GROUP_WIDTH), seq_f32(GROUP_WIDTH), seq_f32(GROUP_WIDTH),
                        seq_f32(HEAD_DIM), pltpu.VMEM((2 * N_GATES, seq_len), F32),
                        pltpu.VMEM((MIX_CHUNK, MIX_CHUNK), jnp.int32)],
        compiler_params=_mix_params(1),
        name="gated_delta",
    )(*args)
    return out, s_fin.reshape(n_seq * n_sub, DIRS, N_HEADS, HEAD_DIM, HEAD_DIM)


def kernel(x_prompt, x_sample, state_ret, state_gdn, state_hgrn, state_s5_re, state_s5_im, c, c_ctx, norm1_w, norm2_w, final_norm_w, ada_w, ada_b, in_proj, out_proj, ret_decay_logit, gdn_conv, gdn_a_log, gdn_dt_bias, gdn_norm_w, hg_lb_param, hg_norm_w, s5_a_re, s5_a_im, s5_b_re, s5_b_im, s5_c_re, s5_c_im, s5_log_step, s5_d, s5_glu_w, s5_glu_b, ffn_w1, ffn_w3, ffn_w2):
    lb_soft = jax.nn.softmax(hg_lb_param, axis=0)
    lower_bounds = jnp.cumsum(lb_soft, axis=0) - lb_soft[0]
    rope2 = _rope_tables(DEC_SEQ)

    cvec = jnp.zeros((N_SEQ_ROWS, D_MODEL), F32).at[0].set(c_ctx).at[1:1 + DEC_BATCH].set(c)
    mod_all = _ada(cvec, ada_w, ada_b).reshape(DEPTH, N_SEQ_ROWS, N_MOD, 1, D_MODEL)

    gate0 = 8 * GROUP_WIDTH
    w_in = in_proj.astype(BF16)
    w_main = jnp.concatenate([w_in[:, :, :gate0], w_in[:, :, gate0 + 2 * N_GATES:]], axis=-1)
    w_small = jnp.pad(w_in[:, :, gate0:gate0 + 2 * N_GATES], ((0, 0), (0, 0), (0, PROJ_SMALL - 2 * N_GATES)))
    w_out = out_proj.astype(BF16)
    w1, w3, w2 = ffn_w1.astype(BF16), ffn_w3.astype(BF16), ffn_w2.astype(BF16)

    x = jnp.concatenate([x_prompt.reshape(N_CTX_TOK, D_MODEL), x_sample.reshape(N_LAT_TOK, D_MODEL)], axis=0)
    hgrn_t = jnp.swapaxes(state_hgrn, -1, -2)
    s5_x0 = (state_s5_re.reshape(DEC_BATCH, DEPTH, DIRS, 1, S5_X), state_s5_im.reshape(DEC_BATCH, DEPTH, DIRS, 1, S5_X))
    lat_off = N_CTX_TOK // DEC_SEQ
    tables = jax.vmap(_s5_tables)(s5_a_re, s5_a_im, s5_log_step, s5_b_re, s5_b_im, s5_c_re, s5_c_im)
    ctx_states = []
    for i in range(DEPTH):
        mod = mod_all[i]
        pm, ps = _inproj(x, norm1_w[i][None], mod, w_main, w_small, i)
        ps_t = ps[:, :2 * N_GATES].T
        lg = jax.nn.log_sigmoid(ret_decay_logit[i])
        ctx = (BATCH, SEQ, 0)
        lat = (DEC_BATCH, DEC_SEQ, lat_off)

        ret_c, rs = _retention_pallas(pm, lg, *ctx, None, None, i)
        ret_l, _ = _retention_pallas(pm, lg, *lat, rope2, state_ret, i)
        gdn_args = (pm, ps, ps_t, gdn_conv[i], gdn_a_log[i], gdn_dt_bias[i], gdn_norm_w[i])
        gdn_c, gs = _gdn_pallas(*gdn_args, *ctx, None, i)
        gdn_l, _ = _gdn_pallas(*gdn_args, *lat, state_gdn, i)
        hg_c, hs = _gla_pallas(pm, lower_bounds[i], hg_norm_w[i], *ctx, None, i)
        hg_l, _ = _gla_pallas(pm, lower_bounds[i], hg_norm_w[i], *lat, hgrn_t, i)
        s5_args = (pm, tables, s5_d[i], s5_glu_w[i], s5_glu_b[i])
        s5_c, xr, xi = _s5_pallas(*s5_args, *ctx, None, i)
        s5_l, _, _ = _s5_pallas(*s5_args, *lat, s5_x0, i)
        ctx_states.append((rs, gs, jnp.swapaxes(hs, -1, -2), xr.reshape(BATCH, DIRS, S5_GROUPS, S5_N),
                           xi.reshape(BATCH, DIRS, S5_GROUPS, S5_N)))

        parts = ((ret_c, ret_l), (gdn_c, gdn_l), (hg_c, hg_l), (s5_c, s5_l))
        x = _outproj(parts, w_out, x, mod, i)
        x = _ffn(x, norm2_w[i][None], mod, w1, w3, w2, final_norm_w[None], i)

    y_prompt = x[0].reshape(BATCH, SEQ, D_MODEL)
    y_sample = x[1].reshape(DEC_BATCH, DEC_SEQ, D_MODEL)
    new_states = tuple(jnp.stack([s[j] for s in ctx_states], axis=1) for j in range(5))
    return (y_prompt, y_sample) + new_states
```

```python
import functools
import math

import jax
import jax.numpy as jnp
from jax import lax
from jax.experimental import pallas as pl
from jax.experimental.pallas import tpu as pltpu

F32 = jnp.float32
BF16 = jnp.bfloat16

D_MODEL = 2048
BATCH = 16
SEQ = 256
DEPTH = 2
DEC_BATCH = 8
DEC_SEQ = 1024
GRID_W = 64
HEAD_DIM = 128
GROUP_WIDTH = 512
N_HEADS = 4
S5_CH = 16
S5_GROUPS = 32
S5_N = 64
GDN_CONV = 3
ROPE_BASE = 10000.0
FFN_HIDDEN = 5632
N_MOD = 6
EPS = 1e-6
LB_FLOOR = 1e-30

N_CTX_TOK = BATCH * SEQ
N_LAT_TOK = DEC_BATCH * DEC_SEQ
N_TOK = N_CTX_TOK + N_LAT_TOK
N_SEQ_ROWS = 16
PROJ_MAIN = 14 * GROUP_WIDTH
PROJ_SMALL = 128
V7X_VMEM_BYTES = 64 * 1024 * 1024
VMEM_LIMIT = V7X_VMEM_BYTES - 8 * 1024 * 1024
ADA_TN = 1024
OUT_TN = 1024


def _seq_row(tile, tm):
    n_ctx = N_CTX_TOK // tm
    per_lat = DEC_SEQ // tm
    return jnp.where(tile < n_ctx, 0, 1 + (tile - n_ctx) // per_lat)


def _stream_specs(tm, width, n_col=1):
    n_ctx = N_CTX_TOK // tm

    def ctx_map(i, j):
        return jnp.minimum(i, n_ctx - 1), (jnp.where(i < n_ctx, j, n_col - 1) if n_col > 1 else 0)

    def lat_map(i, j):
        return jnp.maximum(i - n_ctx, 0), (jnp.where(i >= n_ctx, j, 0) if n_col > 1 else 0)

    return pl.BlockSpec((tm, width), ctx_map), pl.BlockSpec((tm, width), lat_map)


def _on_stream(tile, tm, fn):
    n_ctx = N_CTX_TOK // tm
    pl.when(tile < n_ctx)(functools.partial(fn, 0))
    pl.when(tile >= n_ctx)(functools.partial(fn, 1))


def _ada_body(c_ref, w_ref, b_ref, o_ref):
    cv = c_ref[...]
    s = cv * jax.nn.sigmoid(cv)
    o_ref[0] = jnp.dot(s.astype(BF16), w_ref[0].astype(BF16), preferred_element_type=F32) + b_ref[0]


def _ada(cvec, ada_w, ada_b):
    tn = ADA_TN
    n = N_MOD * D_MODEL
    return pl.pallas_call(
        _ada_body,
        out_shape=jax.ShapeDtypeStruct((DEPTH, N_SEQ_ROWS, n), F32),
        grid=(DEPTH, n // tn),
        in_specs=[
            pl.BlockSpec((N_SEQ_ROWS, D_MODEL), lambda l, j: (0, 0)),
            pl.BlockSpec((1, D_MODEL, tn), lambda l, j: (l, 0, j)),
            pl.BlockSpec((1, 1, tn), lambda l, j: (l, 0, j)),
        ],
        out_specs=pl.BlockSpec((1, N_SEQ_ROWS, tn), lambda l, j: (l, 0, j)),
        compiler_params=pltpu.CompilerParams(
            dimension_semantics=("parallel", "parallel"), vmem_limit_bytes=VMEM_LIMIT),
        name="ada_mod",
    )(cvec, ada_w, ada_b.reshape(DEPTH, 1, n))


def _norm_mod(x, nw, sc, sh):
    ms = jnp.mean(x * x, axis=-1, keepdims=True)
    y = x * lax.rsqrt(ms + EPS) * nw
    return y * (1.0 + sc) + sh


PROJ_TM = 1024
PROJ_TN = 1024


def _inproj_body(x_ref, nw_ref, sc_ref, sh_ref, w_ref, ws_ref, o_ref, os_ref, h_ref):
    @pl.when(pl.program_id(1) == 0)
    def _():
        hb = _norm_mod(x_ref[...], nw_ref[...], sc_ref[...], sh_ref[...]).astype(BF16)
        h_ref[...] = hb
        os_ref[...] = jnp.dot(hb, ws_ref[...], preferred_element_type=F32)

    o_ref[...] = jnp.dot(h_ref[...], w_ref[...], preferred_element_type=F32)


def _inproj(x, nw, mod, w_main, w_small, layer):
    tm, tn = PROJ_TM, PROJ_TN
    return pl.pallas_call(
        _inproj_body,
        out_shape=(jax.ShapeDtypeStruct((N_TOK, PROJ_MAIN), F32),
                   jax.ShapeDtypeStruct((N_TOK, PROJ_SMALL), F32)),
        grid=(N_TOK // tm, PROJ_MAIN // tn),
        in_specs=[
            pl.BlockSpec((tm, D_MODEL), lambda i, j: (i, 0)),
            pl.BlockSpec((1, D_MODEL), lambda i, j: (0, 0)),
            pl.BlockSpec((None, None, 1, D_MODEL), lambda i, j: (_seq_row(i, tm), 1, 0, 0)),
            pl.BlockSpec((None, None, 1, D_MODEL), lambda i, j: (_seq_row(i, tm), 0, 0, 0)),
            pl.BlockSpec((None, D_MODEL, tn), lambda i, j: (layer, 0, j)),
            pl.BlockSpec((None, D_MODEL, PROJ_SMALL), lambda i, j: (layer, 0, 0)),
        ],
        out_specs=(pl.BlockSpec((tm, tn), lambda i, j: (i, j)),
                   pl.BlockSpec((tm, PROJ_SMALL), lambda i, j: (i, 0))),
        scratch_shapes=[pltpu.VMEM((tm, D_MODEL), BF16)],
        compiler_params=pltpu.CompilerParams(
            dimension_semantics=("parallel", "arbitrary"), vmem_limit_bytes=VMEM_LIMIT),
        name="in_proj",
    )(x, nw, mod, mod, w_main, w_small)


def _outproj_body(*refs):
    n_mix = 4
    m_refs, (w_ref, x_ref, g_ref, o_ref) = refs[:2 * n_mix], refs[2 * n_mix:]

    def compute(stream):
        acc = None
        for p in range(n_mix):
            part = jnp.dot(m_refs[2 * p + stream][...], w_ref[p * GROUP_WIDTH:(p + 1) * GROUP_WIDTH, :],
                           preferred_element_type=F32)
            acc = part if acc is None else acc + part
        o_ref[...] = x_ref[...] + g_ref[...] * acc

    _on_stream(pl.program_id(0), PROJ_TM, compute)


def _outproj(parts, w, x, mod, layer):
    tm, tn = PROJ_TM, OUT_TN
    in_specs = []
    for _ in parts:
        in_specs += list(_stream_specs(tm, GROUP_WIDTH))
    in_specs += [pl.BlockSpec((None, D_MODEL, tn), lambda i, j: (layer, 0, j)),
                 pl.BlockSpec((tm, tn), lambda i, j: (i, j)),
                 pl.BlockSpec((None, None, 1, tn), lambda i, j: (_seq_row(i, tm), 2, 0, j))]
    return pl.pallas_call(
        _outproj_body,
        out_shape=jax.ShapeDtypeStruct((N_TOK, D_MODEL), F32),
        grid=(N_TOK // tm, D_MODEL // tn),
        in_specs=in_specs,
        out_specs=pl.BlockSpec((tm, tn), lambda i, j: (i, j)),
        compiler_params=pltpu.CompilerParams(
            dimension_semantics=("parallel", "arbitrary"), vmem_limit_bytes=VMEM_LIMIT),
        name="out_proj",
    )(*(a for pair in parts for a in pair), w, x, mod)


FFN_TM = 512
FFN_TH = 512


def _ffn_body(x_ref, nw_ref, sc_ref, sh_ref, g_ref, w1_ref, w3_ref, w2_ref, fw_ref, *rest, final_norm):
    o_refs, (h_ref, acc_ref) = rest[:-2], rest[-2:]
    i, k = pl.program_id(0), pl.program_id(1)

    @pl.when(k == 0)
    def _():
        h_ref[...] = _norm_mod(x_ref[...], nw_ref[...], sc_ref[...], sh_ref[...]).astype(BF16)
        acc_ref[...] = jnp.zeros_like(acc_ref)

    h = h_ref[...]
    a = jnp.dot(h, w1_ref[...], preferred_element_type=F32)
    b = jnp.dot(h, w3_ref[...], preferred_element_type=F32)
    g = (a * jax.nn.sigmoid(a) * b).astype(BF16)
    acc_ref[...] += jnp.dot(g, w2_ref[...], preferred_element_type=F32)

    @pl.when(k == pl.num_programs(1) - 1)
    def _():
        y = x_ref[...] + g_ref[...] * acc_ref[...]
        if not final_norm:
            o_refs[0][...] = y
        else:
            ms = jnp.mean(y * y, axis=-1, keepdims=True)
            y = y * lax.rsqrt(ms + EPS) * fw_ref[...]

            def write(stream):
                o_refs[stream][...] = y

            _on_stream(i, FFN_TM, write)


def _ffn(x, nw, mod, w1, w3, w2, fw, layer):
    final_norm = layer == DEPTH - 1
    tm, th = FFN_TM, FFN_TH
    if final_norm:
        out_shape = (jax.ShapeDtypeStruct((N_CTX_TOK, D_MODEL), F32), jax.ShapeDtypeStruct((N_LAT_TOK, D_MODEL), F32))
        out_specs = _stream_specs(tm, D_MODEL)
    else:
        out_shape = jax.ShapeDtypeStruct((N_TOK, D_MODEL), F32)
        out_specs = pl.BlockSpec((tm, D_MODEL), lambda i, k: (i, 0))
    return pl.pallas_call(
        functools.partial(_ffn_body, final_norm=final_norm),
        out_shape=out_shape,
        grid=(N_TOK // tm, FFN_HIDDEN // th),
        in_specs=[
            pl.BlockSpec((tm, D_MODEL), lambda i, k: (i, 0)),
            pl.BlockSpec((1, D_MODEL), lambda i, k: (0, 0)),
            pl.BlockSpec((None, None, 1, D_MODEL), lambda i, k: (_seq_row(i, tm), 4, 0, 0)),
            pl.BlockSpec((None, None, 1, D_MODEL), lambda i, k: (_seq_row(i, tm), 3, 0, 0)),
            pl.BlockSpec((None, None, 1, D_MODEL), lambda i, k: (_seq_row(i, tm), 5, 0, 0)),
            pl.BlockSpec((None, D_MODEL, th), lambda i, k: (layer, 0, k)),
            pl.BlockSpec((None, D_MODEL, th), lambda i, k: (layer, 0, k)),
            pl.BlockSpec((None, th, D_MODEL), lambda i, k: (layer, k, 0)),
            pl.BlockSpec((1, D_MODEL), lambda i, k: (0, 0)),
        ],
        out_specs=out_specs,
        scratch_shapes=[pltpu.VMEM((tm, D_MODEL), BF16), pltpu.VMEM((tm, D_MODEL), F32)],
        compiler_params=pltpu.CompilerParams(
            dimension_semantics=("arbitrary", "arbitrary"), vmem_limit_bytes=VMEM_LIMIT),
        name="ffn",
    )(x, nw, mod, mod, mod, w1, w3, w2, fw)


def _axial_rope(l):
    n_rows = l // GRID_W
    t_row = jnp.repeat(jnp.arange(n_rows, dtype=F32), GRID_W)
    t_col = jnp.tile(jnp.arange(GRID_W, dtype=F32), n_rows)
    n_freq = HEAD_DIM // 4
    inv = ROPE_BASE ** (-jnp.arange(n_freq, dtype=F32) / n_freq)
    ang = jnp.concatenate([t_row[:, None] * inv, t_col[:, None] * inv], axis=-1)
    return jnp.cos(ang), jnp.sin(ang)


MIX_CHUNK = 128
DIRS = 2
NT_DIMS = (((1,), (1,)), ((), ()))
TN_DIMS = (((0,), (0,)), ((), ()))


def _dot(a, b):
    return jnp.dot(a.astype(BF16), b.astype(BF16), preferred_element_type=F32)


def _dot_nt(a, b):
    return lax.dot_general(a.astype(BF16), b.astype(BF16), NT_DIMS, preferred_element_type=F32)


def _dot_tn(a, b):
    return lax.dot_general(a.astype(BF16), b.astype(BF16), TN_DIMS, preferred_element_type=F32)


def _silu(x):
    return x * jax.nn.sigmoid(x)


def _seq_spec(seq_len, width, row_off, col_blk, pipeline_mode=None):
    return pl.BlockSpec((seq_len, width), lambda b: (row_off + b, col_blk), pipeline_mode=pipeline_mode)


def _mix_params(n_par):
    return pltpu.CompilerParams(dimension_semantics=("parallel",) * n_par, vmem_limit_bytes=VMEM_LIMIT)


def _ret_body(*refs, n_chunks, use_rope, has_s0):
    it = iter(refs)
    lg_ref, q_ref, k_ref, v_ref, g_ref = (next(it) for _ in range(5))
    cos_ref, sin_ref = (next(it), next(it)) if use_rope else (None, None)
    s0_ref = next(it) if has_s0 else None
    o_ref, sf_ref, acc_ref, s_ref, intra_ref, qd_ref, kd_ref, cd_ref = (next(it) for _ in range(8))
    C, H = MIX_CHUNK, N_HEADS

    row = lax.broadcasted_iota(jnp.int32, (C, C), 0)
    col = lax.broadcasted_iota(jnp.int32, (C, C), 1)
    rel = (row - col).astype(F32)
    pos = lax.broadcasted_iota(jnp.int32, (C, HEAD_DIM), 0).astype(F32)
    for d in range(DIRS):
        for h in range(H):
            lg = lg_ref[d, h]
            if d == 0:
                intra_ref[d, h] = jnp.where(rel >= 0, jnp.exp(jnp.maximum(rel, 0.0) * lg), 0.0)
                qd_ref[d, h] = jnp.exp((pos + 1.0) * lg)
                kd_ref[d, h] = jnp.exp((C - 1.0 - pos) * lg)
            else:
                intra_ref[d, h] = jnp.where(rel <= 0, jnp.exp(jnp.maximum(-rel, 0.0) * lg), 0.0)
                qd_ref[d, h] = jnp.exp((C - pos) * lg)
                kd_ref[d, h] = jnp.exp(pos * lg)
            cd_ref[d, h] = jnp.exp(jnp.full((C, HEAD_DIM), C, F32) * lg)
    if has_s0:
        s_ref[...] = s0_ref[...]
    else:
        s_ref[...] = jnp.zeros_like(s_ref)
    acc_ref[...] = jnp.zeros_like(acc_ref)

    units = [(d, h) for d in range(DIRS) for h in range(H)]

    def chunk_step(n):
        rows_d = [slice(c * C, (c + 1) * C) for c in (n, n_chunks - 1 - n)]
        q_l, k_l, v_l, sc_l = [], [], [], []
        for d, h in units:
            rows, cols = rows_d[d], slice(h * HEAD_DIM, (h + 1) * HEAD_DIM)
            q, k = q_ref[rows, cols], k_ref[rows, cols]
            if use_rope:
                cs, sn = cos_ref[rows, :], sin_ref[rows, :]
                q = q * cs + pltpu.roll(q, HEAD_DIM // 2, 1) * sn
                k = k * cs + pltpu.roll(k, HEAD_DIM // 2, 1) * sn
            k = k * HEAD_DIM ** -0.5
            q_l.append(q)
            k_l.append(k)
            v_l.append(v_ref[rows, cols].astype(BF16))
            sc_l.append(_dot_nt(q, k) * intra_ref[d, h])
        for i, (d, h) in enumerate(units):
            cols = slice(h * HEAD_DIM, (h + 1) * HEAD_DIM)
            s = s_ref[d, h]
            acc_ref[rows_d[d], cols] += _dot(jnp.concatenate([sc_l[i], q_l[i] * qd_ref[d, h]], axis=1),
                                             jnp.concatenate([v_l[i], s.astype(BF16)], axis=0))
            s_ref[d, h] = cd_ref[d, h] * s + _dot_tn(k_l[i] * kd_ref[d, h], v_l[i])

    for n in range(n_chunks):
        chunk_step(n)
    sf_ref[...] = s_ref[...]

    for n in range(n_chunks):
        rows = slice(n * C, (n + 1) * C)
        for h in range(H):
            cols = slice(h * HEAD_DIM, (h + 1) * HEAD_DIM)
            o = acc_ref[rows, cols]
            mu = jnp.mean(o, axis=-1, keepdims=True)
            oc = o - mu
            y = oc * lax.rsqrt(jnp.mean(oc * oc, axis=-1, keepdims=True) + EPS)
            o_ref[rows, cols] = (y * _silu(g_ref[rows, cols])).astype(BF16)


def _retention_pallas(pm, log_gamma, n_seq, seq_len, row_off, rope2, s0, layer):
    use_rope, has_s0 = rope2 is not None, s0 is not None
    st_shape = (DIRS, N_HEADS, HEAD_DIM, HEAD_DIM)
    in_specs = [pl.BlockSpec(memory_space=pltpu.SMEM)]
    in_specs += [_seq_spec(seq_len, GROUP_WIDTH, row_off, cb) for cb in range(4)]
    args = [log_gamma, pm, pm, pm, pm]
    if use_rope:
        in_specs += [pl.BlockSpec((seq_len, HEAD_DIM), lambda b: (0, 0))] * 2
        args += list(rope2)
    if has_s0:
        in_specs.append(pl.BlockSpec((None, None) + st_shape, lambda b: (b, layer, 0, 0, 0, 0)))
        args.append(s0)
    return pl.pallas_call(
        functools.partial(_ret_body, n_chunks=seq_len // MIX_CHUNK, use_rope=use_rope, has_s0=has_s0),
        out_shape=(jax.ShapeDtypeStruct((n_seq * seq_len, GROUP_WIDTH), BF16),
                   jax.ShapeDtypeStruct((n_seq,) + st_shape, F32)),
        grid=(n_seq,),
        in_specs=in_specs,
        out_specs=(pl.BlockSpec((seq_len, GROUP_WIDTH), lambda b: (b, 0)),
                   pl.BlockSpec((None,) + st_shape, lambda b: (b, 0, 0, 0, 0))),
        scratch_shapes=[pltpu.VMEM((seq_len, GROUP_WIDTH), F32), pltpu.VMEM(st_shape, F32),
                        pltpu.VMEM((DIRS, N_HEADS, MIX_CHUNK, MIX_CHUNK), F32),
                        pltpu.VMEM((DIRS, N_HEADS, MIX_CHUNK, HEAD_DIM), F32),
                        pltpu.VMEM((DIRS, N_HEADS, MIX_CHUNK, HEAD_DIM), F32),
                        pltpu.VMEM((DIRS, N_HEADS, MIX_CHUNK, HEAD_DIM), F32)],
        compiler_params=_mix_params(1),
        name="retention",
    )(*args)


def _rope_tables(l):
    cos, sin = _axial_rope(l)
    return jnp.concatenate([cos, cos], axis=-1), jnp.concatenate([-sin, sin], axis=-1)


S5_HALF_G = S5_GROUPS // 2
S5_HALF_U = S5_HALF_G * S5_CH
S5_HALF_X = S5_HALF_G * S5_N
S5_X = S5_GROUPS * S5_N
S5_TC = 512
S5_BLK = 8
S5_TABS = 4


def _s5_tables(a_re, a_im, log_step, b_re, b_im, c_re, c_im):
    dt = jnp.exp(log_step)[..., None]
    mag = jnp.exp(a_re * dt)
    ab_re = mag * jnp.cos(a_im * dt)
    ab_im = mag * jnp.sin(a_im * dt)
    den = a_re * a_re + a_im * a_im
    nr = ab_re - 1.0
    f_re = (nr * a_re + ab_im * a_im) / den
    f_im = (ab_im * a_re - nr * a_im) / den
    bb_re = f_re[..., None] * b_re - f_im[..., None] * b_im
    bb_im = f_re[..., None] * b_im + f_im[..., None] * b_re
    eye = jnp.eye(S5_HALF_G, dtype=F32)

    def in_mat(bb):
        bb = bb.reshape(DIRS, 2, S5_HALF_G, S5_N, S5_CH)
        return jnp.einsum('dhgnc,gk->dhgckn', bb, eye).reshape(DIRS, 2, S5_HALF_U, S5_HALF_X)

    def out_mat(cc):
        cc = cc.reshape(DIRS, 2, S5_HALF_G, S5_CH, S5_N)
        return jnp.einsum('dhgcn,gk->dhgnkc', cc, eye).reshape(DIRS, 2, S5_HALF_X, S5_HALF_U)

    bm = jnp.concatenate([in_mat(bb_re), in_mat(bb_im)], axis=-1).astype(BF16)
    cm = jnp.concatenate([out_mat(c_re), -out_mat(c_im)], axis=-2).astype(BF16)
    t = jnp.arange(S5_BLK, dtype=F32)
    order = jnp.stack([t, S5_BLK - 1.0 - t])
    shifts = 2.0 ** jnp.arange(S5_TABS - 1, dtype=F32)
    expo = jnp.concatenate([jnp.where(order[:, None, :] >= shifts[None, :, None], shifts[None, :, None], jnp.nan),
                            order[:, None, :] + 1.0], axis=1)
    live = ~jnp.isnan(expo)
    e = jnp.where(live, expo, 0.0)[..., None]
    adt_re = (a_re * dt).reshape(DIRS, 1, 1, S5_X)
    adt_im = (a_im * dt).reshape(DIRS, 1, 1, S5_X)
    pmag = jnp.where(live[..., None], jnp.exp(e * adt_re), 0.0)
    pw_re = pmag * jnp.cos(e * adt_im)
    pw_im = pmag * jnp.sin(e * adt_im)
    return bm, cm, pw_re, pw_im


def _gelu_tanh(x):
    return 0.5 * x * (1.0 + jnp.tanh(math.sqrt(2.0 / math.pi) * (x + 0.044715 * (x * x * x))))


def _s5_body(*refs, seq_len, has_s0):
    it = iter(refs)
    u_ref, bm_ref, cm_ref, pwr_ref, pwi_ref, d_ref, gw_ref, gb_ref = (next(it) for _ in range(8))
    x0r_ref, x0i_ref = (next(it), next(it)) if has_s0 else (None, None)
    o_ref, sfr_ref, sfi_ref, y_ref, xr_ref, xi_ref, xb_ref = (next(it) for _ in range(7))
    tc = min(seq_len, S5_TC)
    n_tiles = seq_len // tc
    n_pair = tc // (2 * S5_BLK)

    y_ref[...] = d_ref[...] * u_ref[...]
    unit = 0
    for d in range(DIRS):
        last = S5_BLK - 1 if d == 0 else 0
        for hf in range(2):
            xs = slice(hf * S5_HALF_X, (hf + 1) * S5_HALF_X)
            us = slice(hf * S5_HALF_U, (hf + 1) * S5_HALF_U)

            def scan_block(x_in, carry, d=d, xs=xs, last=last):
                xr, xi = x_in
                car_re, car_im = carry
                for k in range(S5_TABS - 1):
                    s = 1 << k
                    shift = s if d == 0 else S5_BLK - s
                    p_re, p_im = pwr_ref[d, k, :, xs], pwi_ref[d, k, :, xs]
                    sr, si = pltpu.roll(xr, shift, 0), pltpu.roll(xi, shift, 0)
                    xr, xi = xr + p_re * sr - p_im * si, xi + p_re * si + p_im * sr
                p_re, p_im = pwr_ref[d, S5_TABS - 1, :, xs], pwi_ref[d, S5_TABS - 1, :, xs]
                xr, xi = xr + p_re * car_re - p_im * car_im, xi + p_re * car_im + p_im * car_re
                return (xr, xi), (xr[last:last + 1, :], xi[last:last + 1, :])

            if has_s0:
                carry = (x0r_ref[d, :, xs], x0i_ref[d, :, xs])
            else:
                carry = (jnp.zeros((1, S5_HALF_X), F32), jnp.zeros((1, S5_HALF_X), F32))
            for i in range(n_tiles):
                slot = unit % 2
                unit += 1
                tile = i if d == 0 else n_tiles - 1 - i
                rows_t = slice(tile * tc, (tile + 1) * tc)
                bu = _dot(u_ref[rows_t, us], bm_ref[d, hf])
                xr_ref[slot] = bu[:, :S5_HALF_X]
                xi_ref[slot] = bu[:, S5_HALF_X:]
                for j in range(n_pair):
                    pair = j if d == 0 else n_pair - 1 - j
                    rows = slice(pair * 2 * S5_BLK, (pair + 1) * 2 * S5_BLK)
                    xr2, xi2 = xr_ref[slot, rows, :], xi_ref[slot, rows, :]
                    halves = [(xr2[:S5_BLK], xi2[:S5_BLK]), (xr2[S5_BLK:], xi2[S5_BLK:])]
                    out = [None, None]
                    for idx in ((0, 1) if d == 0 else (1, 0)):
                        out[idx], carry = scan_block(halves[idx], carry)
                    xb_ref[slot, rows, :S5_HALF_X] = jnp.concatenate([out[0][0], out[1][0]], axis=0).astype(BF16)
                    xb_ref[slot, rows, S5_HALF_X:] = jnp.concatenate([out[0][1], out[1][1]], axis=0).astype(BF16)
                y_ref[rows_t, us] += jnp.dot(xb_ref[slot], cm_ref[d, hf], preferred_element_type=F32)
            sfr_ref[d, :, xs] = carry[0]
            sfi_ref[d, :, xs] = carry[1]

    z = _gelu_tanh(y_ref[...])
    o_ref[...] = (z * jax.nn.sigmoid(_dot(z, gw_ref[...]) + gb_ref[...])).astype(BF16)


def _s5_pallas(pm, tables, s5_d, glu_w, glu_b, n_seq, seq_len, row_off, x0, layer):
    bm, cm, pw_re, pw_im = tables
    has_s0 = x0 is not None
    full = lambda shape: pl.BlockSpec(shape, lambda b: (0,) * len(shape))
    of_layer = lambda t: pl.BlockSpec((None,) + t.shape[1:], lambda b: (layer,) + (0,) * (t.ndim - 1))
    in_specs = [_seq_spec(seq_len, GROUP_WIDTH, row_off, 13),
                of_layer(bm), of_layer(cm), of_layer(pw_re), of_layer(pw_im),
                full((1, GROUP_WIDTH)), full((GROUP_WIDTH, GROUP_WIDTH)), full((1, GROUP_WIDTH))]
    args = [pm, bm, cm, pw_re, pw_im, s5_d.reshape(1, GROUP_WIDTH), glu_w.astype(BF16),
            glu_b.reshape(1, GROUP_WIDTH)]
    if has_s0:
        in_specs += [pl.BlockSpec((None, None, DIRS, 1, S5_X), lambda b: (b, layer, 0, 0, 0))] * 2
        args += list(x0)
    st = jax.ShapeDtypeStruct((n_seq, DIRS, 1, S5_X), F32)
    st_spec = pl.BlockSpec((None, DIRS, 1, S5_X), lambda b: (b, 0, 0, 0))
    tc = min(seq_len, S5_TC)
    return pl.pallas_call(
        functools.partial(_s5_body, seq_len=seq_len, has_s0=has_s0),
        out_shape=(jax.ShapeDtypeStruct((n_seq * seq_len, GROUP_WIDTH), BF16), st, st),
        grid=(n_seq,),
        in_specs=in_specs,
        out_specs=(pl.BlockSpec((seq_len, GROUP_WIDTH), lambda b: (b, 0)), st_spec, st_spec),
        scratch_shapes=[pltpu.VMEM((seq_len, GROUP_WIDTH), F32),
                        pltpu.VMEM((2, tc, S5_HALF_X), F32), pltpu.VMEM((2, tc, S5_HALF_X), F32),
                        pltpu.VMEM((2, tc, 2 * S5_HALF_X), BF16)],
        compiler_params=_mix_params(1),
        name="s5",
    )(*args)


GLA_LEVELS = 7


def _chunk_cumsum_rows(x, rowi):
    s = 1
    while s < MIX_CHUNK:
        x = x + jnp.where(rowi >= s, pltpu.roll(x, s, 0), 0.0)
        s *= 2
    return x


def _gla_body(*refs, n_chunks, has_s0):
    it = iter(refs)
    q_ref, f0_ref, f1_ref, i_ref, g_ref, lb_ref, nw_ref = (next(it) for _ in range(7))
    s0_ref = next(it) if has_s0 else None
    o_ref, sf_ref, acc_ref, s_ref, code_ref = (next(it) for _ in range(5))
    C, H = MIX_CHUNK, N_HEADS
    f_refs = (f0_ref, f1_ref)

    rowi = lax.broadcasted_iota(jnp.int32, (C, HEAD_DIM), 0)
    ri = lax.broadcasted_iota(jnp.int32, (C, C), 0)
    ci = lax.broadcasted_iota(jnp.int32, (C, C), 1)
    top_bit = 31 - lax.clz(ri ^ ci)
    code_ref[...] = jnp.where(ri > ci, top_bit, jnp.where(ri < ci, -1 - top_bit, GLA_LEVELS))
    if has_s0:
        s_ref[...] = s0_ref[...]
    else:
        s_ref[...] = jnp.zeros_like(s_ref)
    acc_ref[...] = jnp.zeros_like(acc_ref)

    def chunk_step(n, carry):
        for d in range(DIRS):
            c = n if d == 0 else n_chunks - 1 - n
            rows = pl.ds(pl.multiple_of(c * C, C), C)
            for h in range(H):
                cols = slice(h * HEAD_DIM, (h + 1) * HEAD_DIM)
                code = code_ref[...]
                q = _silu(q_ref[rows, cols]) * HEAD_DIM ** -0.5
                v = i_ref[rows, cols].astype(BF16)
                fx = f_refs[d][rows, cols]
                lb = lb_ref[d:d + 1, cols]
                sig = 1.0 / (1.0 + jnp.exp(-fx))
                logf = jnp.log2(jnp.maximum(lb, LB_FLOOR) + (1.0 - lb) * sig)
                k = (1.0 - lb) * (1.0 - sig)
                cum = _chunk_cumsum_rows(logf, rowi)
                own = cum
                attn = jnp.where(code == GLA_LEVELS, _dot_nt(q, k), 0.0)
                for lvl in range(GLA_LEVELS):
                    m = 1 << lvl
                    prev = pltpu.roll(own, m, 0)
                    pre = jnp.minimum(cum - prev, 0.0)
                    suf = own - cum
                    if d == 0:
                        sc = _dot_nt(q * jnp.exp2(pre), k * jnp.exp2(suf))
                        hit = code == lvl
                    else:
                        sc = _dot_nt(q * jnp.exp2(suf + logf), k * jnp.exp2(jnp.minimum(pre - logf, 0.0)))
                        hit = code == -1 - lvl
                    attn = jnp.where(hit, sc, attn)
                    own = jnp.where(((rowi >> lvl) & 1) == 0, pltpu.roll(own, C - m, 0), own)
                tot = own
                st = s_ref[d, h]
                if d == 0:
                    q_in, k_out = q * jnp.exp2(cum), k * jnp.exp2(tot - cum)
                else:
                    q_in, k_out = q * jnp.exp2(tot - cum + logf), k * jnp.exp2(cum - logf)
                acc_ref[rows, cols] += _dot(attn, v) + _dot_nt(q_in, st)
                s_ref[d, h] = jnp.exp2(tot[0:1, :]) * st + _dot_tn(v, k_out)
        return carry

    lax.fori_loop(0, n_chunks, chunk_step, 0)
    sf_ref[...] = s_ref[...]

    def finish(n, carry):
        rows = pl.ds(pl.multiple_of(n * C, C), C)
        for h in range(H):
            cols = slice(h * HEAD_DIM, (h + 1) * HEAD_DIM)
            o = acc_ref[rows, cols]
            y = o * lax.rsqrt(jnp.mean(o * o, axis=-1, keepdims=True) + EPS) * nw_ref[...]
            o_ref[rows, cols] = (y * _silu(g_ref[rows, cols])).astype(BF16)
        return carry

    lax.fori_loop(0, n_chunks, finish, 0)


def _gla_pallas(pm, lower_bound, norm_w, n_seq, seq_len, row_off, s0_t, layer):
    has_s0 = s0_t is not None
    st_shape = (DIRS, N_HEADS, HEAD_DIM, HEAD_DIM)
    in_specs = [_seq_spec(seq_len, GROUP_WIDTH, row_off, cb) for cb in (8, 9, 10, 11, 12)]
    in_specs += [pl.BlockSpec((DIRS, GROUP_WIDTH), lambda b: (0, 0)), pl.BlockSpec((1, HEAD_DIM), lambda b: (0, 0))]
    args = [pm] * 5 + [lower_bound, norm_w.reshape(1, HEAD_DIM)]
    if has_s0:
        in_specs.append(pl.BlockSpec((None, None) + st_shape, lambda b: (b, layer, 0, 0, 0, 0)))
        args.append(s0_t)
    return pl.pallas_call(
        functools.partial(_gla_body, n_chunks=seq_len // MIX_CHUNK, has_s0=has_s0),
        out_shape=(jax.ShapeDtypeStruct((n_seq * seq_len, GROUP_WIDTH), BF16),
                   jax.ShapeDtypeStruct((n_seq,) + st_shape, F32)),
        grid=(n_seq,),
        in_specs=in_specs,
        out_specs=(pl.BlockSpec((seq_len, GROUP_WIDTH), lambda b: (b, 0)),
                   pl.BlockSpec((None,) + st_shape, lambda b: (b, 0, 0, 0, 0))),
        scratch_shapes=[pltpu.VMEM((seq_len, GROUP_WIDTH), F32), pltpu.VMEM(st_shape, F32),
                        pltpu.VMEM((MIX_CHUNK, MIX_CHUNK), jnp.int32)],
        compiler_params=_mix_params(1),
        name="hgrn2",
    )(*args)


GDN_BLOCK_BITS = 4
GDN_MERGES = 3
N_GATES = DIRS * N_HEADS


def _softplus(x):
    return jnp.maximum(x, 0.0) + jnp.log1p(jnp.exp(-jnp.abs(x)))


def _gdn_body(*refs, seq_len, n_sub, has_s0):
    it = iter(refs)
    (q_ref, k_ref, v_ref, g_ref, ps_ref, pst_ref, cw_ref, prow_ref, pcol_ref, nw_ref) = (next(it) for _ in range(10))
    s0_ref = next(it) if has_s0 else None
    (o_ref, sf_ref, acc_ref, s_ref, qn_ref, kn_ref, vn_ref, gate_ref, gatet_ref, blev_ref) = (
        next(it) for _ in range(10))
    C, H, L = MIX_CHUNK, N_HEADS, seq_len
    R = n_sub * L
    n_chunks = L // C
    w = GROUP_WIDTH

    rowi = lax.broadcasted_iota(jnp.int32, (C, HEAD_DIM), 0)
    lane = lax.broadcasted_iota(jnp.int32, (C, HEAD_DIM), 1)
    lane_t = lax.broadcasted_iota(jnp.int32, (2 * N_GATES, C), 1)
    row_t = lax.broadcasted_iota(jnp.int32, (2 * N_GATES, C), 0)

    def prepare(c):
        rows = slice(c * C, (c + 1) * C)
        first, final = c % n_chunks == 0, c % n_chunks == n_chunks - 1
        for part, (src, dst) in enumerate(((q_ref, qn_ref), (k_ref, kn_ref), (v_ref, vn_ref))):
            for h in range(H):
                cols = slice(h * HEAD_DIM, (h + 1) * HEAD_DIM)
                wc = slice(part * w + h * HEAD_DIM, part * w + (h + 1) * HEAD_DIM)
                x = src[rows, cols]
                before = jnp.zeros((1, HEAD_DIM), F32) if first else src[c * C - 1:c * C, cols]
                after = jnp.zeros((1, HEAD_DIM), F32) if final else src[(c + 1) * C:(c + 1) * C + 1, cols]
                x_prev = jnp.where(rowi == 0, before, pltpu.roll(x, 1, 0))
                x_next = jnp.where(rowi == C - 1, after, pltpu.roll(x, C - 1, 0))
                y = _silu(cw_ref[0:1, wc] * x_prev + cw_ref[1:2, wc] * x + cw_ref[2:3, wc] * x_next)
                if part < 2:
                    y = y * lax.rsqrt(jnp.sum(y * y, axis=-1, keepdims=True) + EPS)
                if part == 0:
                    y = y * HEAD_DIM ** -0.5
                dst[rows, cols] = y

        a = ps_ref[rows, :]
        la = -jnp.exp(prow_ref[0:1, :]) * _softplus(a + prow_ref[1:2, :])
        pre = _chunk_cumsum_rows(la, rowi)
        tot = jnp.broadcast_to(pre[C - 1:C, :], (C, HEAD_DIM))
        g = jnp.where(lane < N_HEADS, pre, tot - pre + la)
        gate_ref[rows, :] = jnp.where(lane < N_GATES, g,
                                      jnp.where(lane < 2 * N_GATES, jax.nn.sigmoid(a),
                                                pltpu.roll(tot, 2 * N_GATES, 1)))
        at = pst_ref[:, rows]
        lat = -jnp.exp(pcol_ref[:, 0:1]) * _softplus(at + pcol_ref[:, 1:2])
        pre_t, s = lat, 1
        while s < C:
            pre_t = pre_t + jnp.where(lane_t >= s, pltpu.roll(pre_t, s, 1), 0.0)
            s *= 2
        tot_t = jnp.broadcast_to(pre_t[:, C - 1:C], (2 * N_GATES, C))
        g_t = jnp.where(row_t < N_HEADS, pre_t, tot_t - pre_t + lat)
        gatet_ref[:, rows] = jnp.where(row_t < N_GATES, g_t, jax.nn.sigmoid(at))

    ri = lax.broadcasted_iota(jnp.int32, (C, C), 0)
    ci = lax.broadcasted_iota(jnp.int32, (C, C), 1)
    bx = (ri >> GDN_BLOCK_BITS) ^ (ci >> GDN_BLOCK_BITS)
    blev_ref[...] = jnp.where(bx == 0, 0, 32 - lax.clz(bx))
    if has_s0:
        for s in range(n_sub):
            s_ref[s * DIRS:(s + 1) * DIRS] = s0_ref[s]
    else:
        s_ref[...] = jnp.zeros_like(s_ref)
    acc_ref[...] = jnp.zeros_like(acc_ref)

    units = [(s, d, h) for s in range(n_sub) for d in range(DIRS) for h in range(H)]

    def rows_of(n, s, d):
        c = n if d == 0 else n_chunks - 1 - n
        return slice(s * L + c * C, s * L + (c + 1) * C)

    def state_free_phases(step_units):
        eye = jnp.where(ri == ci, 1.0, 0.0)
        blev = blev_ref[...]
        incl_d = [ri >= ci, ri <= ci]
        strict_d = [ri > ci, ri < ci]
        q_l, et_l, ek_l, rhs_l, a_l, attn_l = ([] for _ in range(6))
        for n, s, d, h in step_units:
            rows, cols, cg = rows_of(n, s, d), slice(h * HEAD_DIM, (h + 1) * HEAD_DIM), d * N_HEADS + h
            q, k, v = qn_ref[rows, cols], kn_ref[rows, cols], vn_ref[rows, cols]
            g_i = jnp.broadcast_to(gate_ref[rows, cg:cg + 1], (C, C))
            b_i = jnp.broadcast_to(gate_ref[rows, N_GATES + cg:N_GATES + cg + 1], (C, C))
            tot = jnp.broadcast_to(gate_ref[rows, 2 * N_GATES + cg:2 * N_GATES + cg + 1], (C, C))
            g_j = gatet_ref[cg:cg + 1, rows]
            decay = jnp.where(incl_d[d], jnp.exp(jnp.minimum(g_i - g_j, 0.0)), 0.0)
            e_g = jnp.exp(g_i)
            kb = k.astype(BF16)
            a_l.append(jnp.where(strict_d[d], _dot_nt(kb, kb) * b_i * decay, 0.0))
            attn_l.append(_dot_nt(q, kb) * decay)
            rhs_l.append(jnp.concatenate([v * b_i, k * (b_i * e_g)], axis=1).astype(BF16))
            q_l.append(q * e_g)
            ek_l.append(k * jnp.exp(tot - g_i))
            et_l.append(jnp.exp(tot))
        b_l = [-jnp.where(blev == 0, a, 0.0) for a in a_l]
        p_l = [eye + b for b in b_l]
        for _ in range(GDN_BLOCK_BITS - 1):
            b_l = [_dot(b, b) for b in b_l]
            p_l = [p + _dot(p, b) for p, b in zip(p_l, b_l)]
        for lvl in range(1, GDN_MERGES + 1):
            ep_l = [_dot(jnp.where(blev == lvl, a, 0.0), p) for a, p in zip(a_l, p_l)]
            p_l = [p - _dot(p, ep) for p, ep in zip(p_l, ep_l)]
        uw_l = [_dot(p, rhs) for p, rhs in zip(p_l, rhs_l)]
        lhs_l = [jnp.concatenate([q, attn], axis=1).astype(BF16) for q, attn in zip(q_l, attn_l)]
        return list(zip(uw_l, lhs_l, ek_l, et_l))

    def state_phase(n, vals):
        s_l = [s_ref[s * DIRS + d, h] for s, d, h in units]
        vn_l = [uw[:, :HEAD_DIM] - _dot(uw[:, HEAD_DIM:], st) for (uw, _, _, _), st in zip(vals, s_l)]
        for i, (s, d, h) in enumerate(units):
            cols = slice(h * HEAD_DIM, (h + 1) * HEAD_DIM)
            _, lhs, ek, et = vals[i]
            acc_ref[rows_of(n, s, d), cols] += _dot(lhs, jnp.concatenate([s_l[i], vn_l[i]], axis=0))
            s_ref[s * DIRS + d, h] = s_l[i] * et + _dot_tn(ek, vn_l[i])

    for c in range(R // C):
        prepare(c)
    group = max(1, GDN_GROUP_UNITS // len(units))
    for n0 in range(0, n_chunks, group):
        steps = range(n0, min(n0 + group, n_chunks))
        vals = state_free_phases([(n,) + u for n in steps for u in units])
        for j, n in enumerate(steps):
            state_phase(n, vals[j * len(units):(j + 1) * len(units)])
    sf_ref[...] = s_ref[...]

    for n in range(R // C):
        rows = slice(n * C, (n + 1) * C)
        for h in range(H):
            cols = slice(h * HEAD_DIM, (h + 1) * HEAD_DIM)
            o = acc_ref[rows, cols]
            y = o * lax.rsqrt(jnp.mean(o * o, axis=-1, keepdims=True) + EPS) * nw_ref[...]
            o_ref[rows, cols] = (y * _silu(g_ref[rows, cols])).astype(BF16)


GDN_GROUP_UNITS = 32
GDN_CTX_SUB = 2
GDN_LAT_SUB = 1
GDN_DOUBLE_BUFFER_BYTES = 2 * 1024 * 1024


def _gdn_pallas(pm, ps, ps_t, conv_w, a_log, dt_bias, norm_w, n_seq, seq_len, row_off, s0, layer):
    has_s0 = s0 is not None
    n_sub = GDN_LAT_SUB if has_s0 else GDN_CTX_SUB
    n_seq, seq_len, row_off = n_seq // n_sub, seq_len * n_sub, row_off // n_sub
    st_shape = (n_sub * DIRS, N_HEADS, HEAD_DIM, HEAD_DIM)
    par = jnp.stack([a_log.reshape(N_GATES), dt_bias.reshape(N_GATES)])
    par_row = jnp.pad(par, ((0, 0), (0, HEAD_DIM - N_GATES)))
    par_col = jnp.pad(par.T, ((0, N_GATES), (0, 0)))
    full = lambda shape: pl.BlockSpec(shape, lambda b: (0,) * len(shape))
    big_mode = pl.Buffered(1) if seq_len * GROUP_WIDTH * 4 > GDN_DOUBLE_BUFFER_BYTES else None
    in_specs = [_seq_spec(seq_len, GROUP_WIDTH, row_off, cb, big_mode) for cb in (4, 5, 6, 7)]
    in_specs += [_seq_spec(seq_len, PROJ_SMALL, row_off, 0),
                 pl.BlockSpec((2 * N_GATES, seq_len), lambda b: (0, row_off + b)),
                 full((GDN_CONV, 3 * GROUP_WIDTH)), full((2, HEAD_DIM)), full((2 * N_GATES, 2)), full((1, HEAD_DIM))]
    args = [pm] * 4 + [ps, ps_t, conv_w, par_row, par_col, norm_w.reshape(1, HEAD_DIM)]
    if has_s0:
        in_specs.append(pl.BlockSpec((n_sub, None, DIRS, N_HEADS, HEAD_DIM, HEAD_DIM),
                                     lambda b: (b, layer, 0, 0, 0, 0)))
        args.append(s0)
    seq_f32 = lambda width: pltpu.VMEM((seq_len, width), F32)
    out, s_fin = pl.pallas_call(
        functools.partial(_gdn_body, seq_len=seq_len // n_sub, n_sub=n_sub, has_s0=has_s0),
        out_shape=(jax.ShapeDtypeStruct((n_seq * seq_len, GROUP_WIDTH), BF16),
                   jax.ShapeDtypeStruct((n_seq,) + st_shape, F32)),
        grid=(n_seq,),
        in_specs=in_specs,
        out_specs=(pl.BlockSpec((seq_len, GROUP_WIDTH), lambda b: (b, 0)),
                   pl.BlockSpec((None,) + st_shape, lambda b: (b, 0, 0, 0, 0))),
        scratch_shapes=[seq_f32(GROUP_WIDTH), pltpu.VMEM(st_shape, F32),
                        seq_f32(GROUP_WIDTH), seq_f32(GROUP_WIDTH), seq_f32(GROUP_WIDTH),
                        seq_f32(HEAD_DIM), pltpu.VMEM((2 * N_GATES, seq_len), F32),
                        pltpu.VMEM((MIX_CHUNK, MIX_CHUNK), jnp.int32)],
        compiler_params=_mix_params(1),
        name="gated_delta",
    )(*args)
    return out, s_fin.reshape(n_seq * n_sub, DIRS, N_HEADS, HEAD_DIM, HEAD_DIM)


def kernel(x_prompt, x_sample, state_ret, state_gdn, state_hgrn, state_s5_re, state_s5_im, c, c_ctx, norm1_w, norm2_w, final_norm_w, ada_w, ada_b, in_proj, out_proj, ret_decay_logit, gdn_conv, gdn_a_log, gdn_dt_bias, gdn_norm_w, hg_lb_param, hg_norm_w, s5_a_re, s5_a_im, s5_b_re, s5_b_im, s5_c_re, s5_c_im, s5_log_step, s5_d, s5_glu_w, s5_glu_b, ffn_w1, ffn_w3, ffn_w2):
    lb_soft = jax.nn.softmax(hg_lb_param, axis=0)
    lower_bounds = jnp.cumsum(lb_soft, axis=0) - lb_soft[0]
    rope2 = _rope_tables(DEC_SEQ)

    cvec = jnp.zeros((N_SEQ_ROWS, D_MODEL), F32).at[0].set(c_ctx).at[1:1 + DEC_BATCH].set(c)
    mod_all = _ada(cvec, ada_w, ada_b).reshape(DEPTH, N_SEQ_ROWS, N_MOD, 1, D_MODEL)

    gate0 = 8 * GROUP_WIDTH
    w_in = in_proj.astype(BF16)
    w_main = jnp.concatenate([w_in[:, :, :gate0], w_in[:, :, gate0 + 2 * N_GATES:]], axis=-1)
    w_small = jnp.pad(w_in[:, :, gate0:gate0 + 2 * N_GATES], ((0, 0), (0, 0), (0, PROJ_SMALL - 2 * N_GATES)))
    w_out = out_proj.astype(BF16)
    w1, w3, w2 = ffn_w1.astype(BF16), ffn_w3.astype(BF16), ffn_w2.astype(BF16)

    x = jnp.concatenate([x_prompt.reshape(N_CTX_TOK, D_MODEL), x_sample.reshape(N_LAT_TOK, D_MODEL)], axis=0)
    hgrn_t = jnp.swapaxes(state_hgrn, -1, -2)
    s5_x0 = (state_s5_re.reshape(DEC_BATCH, DEPTH, DIRS, 1, S5_X), state_s5_im.reshape(DEC_BATCH, DEPTH, DIRS, 1, S5_X))
    lat_off = N_CTX_TOK // DEC_SEQ
    tables = jax.vmap(_s5_tables)(s5_a_re, s5_a_im, s5_log_step, s5_b_re, s5_b_im, s5_c_re, s5_c_im)
    ctx_states = []
    for i in range(DEPTH):
        mod = mod_all[i]
        pm, ps = _inproj(x, norm1_w[i][None], mod, w_main, w_small, i)
        ps_t = ps[:, :2 * N_GATES].T
        lg = jax.nn.log_sigmoid(ret_decay_logit[i])
        ctx = (BATCH, SEQ, 0)
        lat = (DEC_BATCH, DEC_SEQ, lat_off)

        ret_c, rs = _retention_pallas(pm, lg, *ctx, None, None, i)
        ret_l, _ = _retention_pallas(pm, lg, *lat, rope2, state_ret, i)
        gdn_args = (pm, ps, ps_t, gdn_conv[i], gdn_a_log[i], gdn_dt_bias[i], gdn_norm_w[i])
        gdn_c, gs = _gdn_pallas(*gdn_args, *ctx, None, i)
        gdn_l, _ = _gdn_pallas(*gdn_args, *lat, state_gdn, i)
        hg_c, hs = _gla_pallas(pm, lower_bounds[i], hg_norm_w[i], *ctx, None, i)
        hg_l, _ = _gla_pallas(pm, lower_bounds[i], hg_norm_w[i], *lat, hgrn_t, i)
        s5_args = (pm, tables, s5_d[i], s5_glu_w[i], s5_glu_b[i])
        s5_c, xr, xi = _s5_pallas(*s5_args, *ctx, None, i)
        s5_l, _, _ = _s5_pallas(*s5_args, *lat, s5_x0, i)
        ctx_states.append((rs, gs, jnp.swapaxes(hs, -1, -2), xr.reshape(BATCH, DIRS, S5_GROUPS, S5_N),
                           xi.reshape(BATCH, DIRS, S5_GROUPS, S5_N)))

        parts = ((ret_c, ret_l), (gdn_c, gdn_l), (hg_c, hg_l), (s5_c, s5_l))
        x = _outproj(parts, w_out, x, mod, i)
        x = _ffn(x, norm2_w[i][None], mod, w1, w3, w2, final_norm_w[None], i)

    y_prompt = x[0].reshape(BATCH, SEQ, D_MODEL)
    y_sample = x[1].reshape(DEC_BATCH, DEC_SEQ, D_MODEL)
    new_states = tuple(jnp.stack([s[j] for s in ctx_states], axis=1) for j in range(5))
    return (y_prompt, y_sample) + new_states
```

```python
import functools
import math

import jax
import jax.numpy as jnp
from jax import lax
from jax.experimental import pallas as pl
from jax.experimental.pallas import tpu as pltpu

F32 = jnp.float32
BF16 = jnp.bfloat16

D_MODEL = 2048
BATCH = 16
SEQ = 256
DEPTH = 2
DEC_BATCH = 8
DEC_SEQ = 1024
GRID_W = 64
HEAD_DIM = 128
GROUP_WIDTH = 512
N_HEADS = 4
S5_CH = 16
S5_GROUPS = 32
S5_N = 64
GDN_CONV = 3
ROPE_BASE = 10000.0
FFN_HIDDEN = 5632
N_MOD = 6
EPS = 1e-6
LB_FLOOR = 1e-30

N_CTX_TOK = BATCH * SEQ
N_LAT_TOK = DEC_BATCH * DEC_SEQ
N_TOK = N_CTX_TOK + N_LAT_TOK
N_SEQ_ROWS = 16
PROJ_MAIN = 14 * GROUP_WIDTH
PROJ_SMALL = 128
V7X_VMEM_BYTES = 64 * 1024 * 1024
VMEM_LIMIT = V7X_VMEM_BYTES - 8 * 1024 * 1024
ADA_TN = 1024
OUT_TN = 1024


def _seq_row(tile, tm):
    n_ctx = N_CTX_TOK // tm
    per_lat = DEC_SEQ // tm
    return jnp.where(tile < n_ctx, 0, 1 + (tile - n_ctx) // per_lat)


def _stream_specs(tm, width, n_col=1):
    n_ctx = N_CTX_TOK // tm

    def ctx_map(i, j):
        return jnp.minimum(i, n_ctx - 1), (jnp.where(i < n_ctx, j, n_col - 1) if n_col > 1 else 0)

    def lat_map(i, j):
        return jnp.maximum(i - n_ctx, 0), (jnp.where(i >= n_ctx, j, 0) if n_col > 1 else 0)

    return pl.BlockSpec((tm, width), ctx_map), pl.BlockSpec((tm, width), lat_map)


def _on_stream(tile, tm, fn):
    n_ctx = N_CTX_TOK // tm
    pl.when(tile < n_ctx)(functools.partial(fn, 0))
    pl.when(tile >= n_ctx)(functools.partial(fn, 1))


def _ada_body(c_ref, w_ref, b_ref, o_ref):
    cv = c_ref[...]
    s = cv * jax.nn.sigmoid(cv)
    o_ref[0] = jnp.dot(s.astype(BF16), w_ref[0].astype(BF16), preferred_element_type=F32) + b_ref[0]


def _ada(cvec, ada_w, ada_b):
    tn = ADA_TN
    n = N_MOD * D_MODEL
    return pl.pallas_call(
        _ada_body,
        out_shape=jax.ShapeDtypeStruct((DEPTH, N_SEQ_ROWS, n), F32),
        grid=(DEPTH, n // tn),
        in_specs=[
            pl.BlockSpec((N_SEQ_ROWS, D_MODEL), lambda l, j: (0, 0)),
            pl.BlockSpec((1, D_MODEL, tn), lambda l, j: (l, 0, j)),
            pl.BlockSpec((1, 1, tn), lambda l, j: (l, 0, j)),
        ],
        out_specs=pl.BlockSpec((1, N_SEQ_ROWS, tn), lambda l, j: (l, 0, j)),
        compiler_params=pltpu.CompilerParams(
            dimension_semantics=("parallel", "parallel"), vmem_limit_bytes=VMEM_LIMIT),
        name="ada_mod",
    )(cvec, ada_w, ada_b.reshape(DEPTH, 1, n))


def _norm_mod(x, nw, sc, sh):
    ms = jnp.mean(x * x, axis=-1, keepdims=True)
    y = x * lax.rsqrt(ms + EPS) * nw
    return y * (1.0 + sc) + sh


PROJ_TM = 1024
PROJ_TN = 1024


def _inproj_body(x_ref, nw_ref, sc_ref, sh_ref, w_ref, ws_ref, o_ref, os_ref, h_ref):
    @pl.when(pl.program_id(1) == 0)
    def _():
        hb = _norm_mod(x_ref[...], nw_ref[...], sc_ref[...], sh_ref[...]).astype(BF16)
        h_ref[...] = hb
        os_ref[...] = jnp.dot(hb, ws_ref[...], preferred_element_type=F32)

    o_ref[...] = jnp.dot(h_ref[...], w_ref[...], preferred_element_type=F32)


def _inproj(x, nw, mod, w_main, w_small, layer):
    tm, tn = PROJ_TM, PROJ_TN
    return pl.pallas_call(
        _inproj_body,
        out_shape=(jax.ShapeDtypeStruct((N_TOK, PROJ_MAIN), F32),
                   jax.ShapeDtypeStruct((N_TOK, PROJ_SMALL), F32)),
        grid=(N_TOK // tm, PROJ_MAIN // tn),
        in_specs=[
            pl.BlockSpec((tm, D_MODEL), lambda i, j: (i, 0)),
            pl.BlockSpec((1, D_MODEL), lambda i, j: (0, 0)),
            pl.BlockSpec((None, None, 1, D_MODEL), lambda i, j: (_seq_row(i, tm), 1, 0, 0)),
            pl.BlockSpec((None, None, 1, D_MODEL), lambda i, j: (_seq_row(i, tm), 0, 0, 0)),
            pl.BlockSpec((None, D_MODEL, tn), lambda i, j: (layer, 0, j)),
            pl.BlockSpec((None, D_MODEL, PROJ_SMALL), lambda i, j: (layer, 0, 0)),
        ],
        out_specs=(pl.BlockSpec((tm, tn), lambda i, j: (i, j)),
                   pl.BlockSpec((tm, PROJ_SMALL), lambda i, j: (i, 0))),
        scratch_shapes=[pltpu.VMEM((tm, D_MODEL), BF16)],
        compiler_params=pltpu.CompilerParams(
            dimension_semantics=("parallel", "arbitrary"), vmem_limit_bytes=VMEM_LIMIT),
        name="in_proj",
    )(x, nw, mod, mod, w_main, w_small)


def _outproj_body(*refs):
    n_mix = 4
    m_refs, (w_ref, x_ref, g_ref, o_ref) = refs[:2 * n_mix], refs[2 * n_mix:]

    def compute(stream):
        acc = None
        for p in range(n_mix):
            part = jnp.dot(m_refs[2 * p + stream][...], w_ref[p * GROUP_WIDTH:(p + 1) * GROUP_WIDTH, :],
                           preferred_element_type=F32)
            acc = part if acc is None else acc + part
        o_ref[...] = x_ref[...] + g_ref[...] * acc

    _on_stream(pl.program_id(0), PROJ_TM, compute)


def _outproj(parts, w, x, mod, layer):
    tm, tn = PROJ_TM, OUT_TN
    in_specs = []
    for _ in parts:
        in_specs += list(_stream_specs(tm, GROUP_WIDTH))
    in_specs += [pl.BlockSpec((None, D_MODEL, tn), lambda i, j: (layer, 0, j)),
                 pl.BlockSpec((tm, tn), lambda i, j: (i, j)),
                 pl.BlockSpec((None, None, 1, tn), lambda i, j: (_seq_row(i, tm), 2, 0, j))]
    return pl.pallas_call(
        _outproj_body,
        out_shape=jax.ShapeDtypeStruct((N_TOK, D_MODEL), F32),
        grid=(N_TOK // tm, D_MODEL // tn),
        in_specs=in_specs,
        out_specs=pl.BlockSpec((tm, tn), lambda i, j: (i, j)),
        compiler_params=pltpu.CompilerParams(
            dimension_semantics=("parallel", "arbitrary"), vmem_limit_bytes=VMEM_LIMIT),
        name="out_proj",
    )(*(a for pair in parts for a in pair), w, x, mod)


FFN_TM = 512
FFN_TH = 512


def _ffn_body(x_ref, nw_ref, sc_ref, sh_ref, g_ref, w1_ref, w3_ref, w2_ref, fw_ref, *rest, final_norm):
    o_refs, (h_ref, acc_ref) = rest[:-2], rest[-2:]
    i, k = pl.program_id(0), pl.program_id(1)

    @pl.when(k == 0)
    def _():
        h_ref[...] = _norm_mod(x_ref[...], nw_ref[...], sc_ref[...], sh_ref[...]).astype(BF16)
        acc_ref[...] = jnp.zeros_like(acc_ref)

    h = h_ref[...]
    a = jnp.dot(h, w1_ref[...], preferred_element_type=F32)
    b = jnp.dot(h, w3_ref[...], preferred_element_type=F32)
    g = (a * jax.nn.sigmoid(a) * b).astype(BF16)
    acc_ref[...] += jnp.dot(g, w2_ref[...], preferred_element_type=F32)

    @pl.when(k == pl.num_programs(1) - 1)
    def _():
        y = x_ref[...] + g_ref[...] * acc_ref[...]
        if not final_norm:
            o_refs[0][...] = y
        else:
            ms = jnp.mean(y * y, axis=-1, keepdims=True)
            y = y * lax.rsqrt(ms + EPS) * fw_ref[...]

            def write(stream):
                o_refs[stream][...] = y

            _on_stream(i, FFN_TM, write)


def _ffn(x, nw, mod, w1, w3, w2, fw, layer):
    final_norm = layer == DEPTH - 1
    tm, th = FFN_TM, FFN_TH
    if final_norm:
        out_shape = (jax.ShapeDtypeStruct((N_CTX_TOK, D_MODEL), F32), jax.ShapeDtypeStruct((N_LAT_TOK, D_MODEL), F32))
        out_specs = _stream_specs(tm, D_MODEL)
    else:
        out_shape = jax.ShapeDtypeStruct((N_TOK, D_MODEL), F32)
        out_specs = pl.BlockSpec((tm, D_MODEL), lambda i, k: (i, 0))
    return pl.pallas_call(
        functools.partial(_ffn_body, final_norm=final_norm),
        out_shape=out_shape,
        grid=(N_TOK // tm, FFN_HIDDEN // th),
        in_specs=[
            pl.BlockSpec((tm, D_MODEL), lambda i, k: (i, 0)),
            pl.BlockSpec((1, D_MODEL), lambda i, k: (0, 0)),
            pl.BlockSpec((None, None, 1, D_MODEL), lambda i, k: (_seq_row(i, tm), 4, 0, 0)),
            pl.BlockSpec((None, None, 1, D_MODEL), lambda i, k: (_seq_row(i, tm), 3, 0, 0)),
            pl.BlockSpec((None, None, 1, D_MODEL), lambda i, k: (_seq_row(i, tm), 5, 0, 0)),
            pl.BlockSpec((None, D_MODEL, th), lambda i, k: (layer, 0, k)),
            pl.BlockSpec((None, D_MODEL, th), lambda i, k: (layer, 0, k)),
            pl.BlockSpec((None, th, D_MODEL), lambda i, k: (layer, k, 0)),
            pl.BlockSpec((1, D_MODEL), lambda i, k: (0, 0)),
        ],
        out_specs=out_specs,
        scratch_shapes=[pltpu.VMEM((tm, D_MODEL), BF16), pltpu.VMEM((tm, D_MODEL), F32)],
        compiler_params=pltpu.CompilerParams(
            dimension_semantics=("arbitrary", "arbitrary"), vmem_limit_bytes=VMEM_LIMIT),
        name="ffn",
    )(x, nw, mod, mod, mod, w1, w3, w2, fw)


def _axial_rope(l):
    n_rows = l // GRID_W
    t_row = jnp.repeat(jnp.arange(n_rows, dtype=F32), GRID_W)
    t_col = jnp.tile(jnp.arange(GRID_W, dtype=F32), n_rows)
    n_freq = HEAD_DIM // 4
    inv = ROPE_BASE ** (-jnp.arange(n_freq, dtype=F32) / n_freq)
    ang = jnp.concatenate([t_row[:, None] * inv, t_col[:, None] * inv], axis=-1)
    return jnp.cos(ang), jnp.sin(ang)


MIX_CHUNK = 128
DIRS = 2
NT_DIMS = (((1,), (1,)), ((), ()))
TN_DIMS = (((0,), (0,)), ((), ()))


def _dot(a, b):
    return jnp.dot(a.astype(BF16), b.astype(BF16), preferred_element_type=F32)


def _dot_nt(a, b):
    return lax.dot_general(a.astype(BF16), b.astype(BF16), NT_DIMS, preferred_element_type=F32)


def _dot_tn(a, b):
    return lax.dot_general(a.astype(BF16), b.astype(BF16), TN_DIMS, preferred_element_type=F32)


def _silu(x):
    return x * jax.nn.sigmoid(x)


def _seq_spec(seq_len, width, row_off, col_blk, pipeline_mode=None):
    return pl.BlockSpec((seq_len, width), lambda b: (row_off + b, col_blk), pipeline_mode=pipeline_mode)


def _mix_params(n_par):
    return pltpu.CompilerParams(dimension_semantics=("parallel",) * n_par, vmem_limit_bytes=VMEM_LIMIT)


def _ret_body(*refs, n_chunks, use_rope, has_s0):
    it = iter(refs)
    lg_ref, q_ref, k_ref, v_ref, g_ref = (next(it) for _ in range(5))
    cos_ref, sin_ref = (next(it), next(it)) if use_rope else (None, None)
    s0_ref = next(it) if has_s0 else None
    o_ref, sf_ref, acc_ref, s_ref, intra_ref, qd_ref, kd_ref, cd_ref = (next(it) for _ in range(8))
    C, H = MIX_CHUNK, N_HEADS

    row = lax.broadcasted_iota(jnp.int32, (C, C), 0)
    col = lax.broadcasted_iota(jnp.int32, (C, C), 1)
    rel = (row - col).astype(F32)
    pos = lax.broadcasted_iota(jnp.int32, (C, HEAD_DIM), 0).astype(F32)
    for d in range(DIRS):
        for h in range(H):
            lg = lg_ref[d, h]
            if d == 0:
                intra_ref[d, h] = jnp.where(rel >= 0, jnp.exp(jnp.maximum(rel, 0.0) * lg), 0.0)
                qd_ref[d, h] = jnp.exp((pos + 1.0) * lg)
                kd_ref[d, h] = jnp.exp((C - 1.0 - pos) * lg)
            else:
                intra_ref[d, h] = jnp.where(rel <= 0, jnp.exp(jnp.maximum(-rel, 0.0) * lg), 0.0)
                qd_ref[d, h] = jnp.exp((C - pos) * lg)
                kd_ref[d, h] = jnp.exp(pos * lg)
            cd_ref[d, h] = jnp.exp(jnp.full((C, HEAD_DIM), C, F32) * lg)
    if has_s0:
        s_ref[...] = s0_ref[...]
    else:
        s_ref[...] = jnp.zeros_like(s_ref)
    acc_ref[...] = jnp.zeros_like(acc_ref)

    units = [(d, h) for d in range(DIRS) for h in range(H)]

    def chunk_step(n):
        rows_d = [slice(c * C, (c + 1) * C) for c in (n, n_chunks - 1 - n)]
        q_l, k_l, v_l, sc_l = [], [], [], []
        for d, h in units:
            rows, cols = rows_d[d], slice(h * HEAD_DIM, (h + 1) * HEAD_DIM)
            q, k = q_ref[rows, cols], k_ref[rows, cols]
            if use_rope:
                cs, sn = cos_ref[rows, :], sin_ref[rows, :]
                q = q * cs + pltpu.roll(q, HEAD_DIM // 2, 1) * sn
                k = k * cs + pltpu.roll(k, HEAD_DIM // 2, 1) * sn
            k = k * HEAD_DIM ** -0.5
            q_l.append(q)
            k_l.append(k)
            v_l.append(v_ref[rows, cols].astype(BF16))
            sc_l.append(_dot_nt(q, k) * intra_ref[d, h])
        for i, (d, h) in enumerate(units):
            cols = slice(h * HEAD_DIM, (h + 1) * HEAD_DIM)
            s = s_ref[d, h]
            acc_ref[rows_d[d], cols] += _dot(jnp.concatenate([sc_l[i], q_l[i] * qd_ref[d, h]], axis=1),
                                             jnp.concatenate([v_l[i], s.astype(BF16)], axis=0))
            s_ref[d, h] = cd_ref[d, h] * s + _dot_tn(k_l[i] * kd_ref[d, h], v_l[i])

    for n in range(n_chunks):
        chunk_step(n)
    sf_ref[...] = s_ref[...]

    for n in range(n_chunks):
        rows = slice(n * C, (n + 1) * C)
        for h in range(H):
            cols = slice(h * HEAD_DIM, (h + 1) * HEAD_DIM)
            o = acc_ref[rows, cols]
            mu = jnp.mean(o, axis=-1, keepdims=True)
            oc = o - mu
            y = oc * lax.rsqrt(jnp.mean(oc * oc, axis=-1, keepdims=True) + EPS)
            o_ref[rows, cols] = (y * _silu(g_ref[rows, cols])).astype(BF16)


def _retention_pallas(pm, log_gamma, n_seq, seq_len, row_off, rope2, s0, layer):
    use_rope, has_s0 = rope2 is not None, s0 is not None
    st_shape = (DIRS, N_HEADS, HEAD_DIM, HEAD_DIM)
    in_specs = [pl.BlockSpec(memory_space=pltpu.SMEM)]
    in_specs += [_seq_spec(seq_len, GROUP_WIDTH, row_off, cb) for cb in range(4)]
    args = [log_gamma, pm, pm, pm, pm]
    if use_rope:
        in_specs += [pl.BlockSpec((seq_len, HEAD_DIM), lambda b: (0, 0))] * 2
        args += list(rope2)
    if has_s0:
        in_specs.append(pl.BlockSpec((None, None) + st_shape, lambda b: (b, layer, 0, 0, 0, 0)))
        args.append(s0)
    return pl.pallas_call(
        functools.partial(_ret_body, n_chunks=seq_len // MIX_CHUNK, use_rope=use_rope, has_s0=has_s0),
        out_shape=(jax.ShapeDtypeStruct((n_seq * seq_len, GROUP_WIDTH), BF16),
                   jax.ShapeDtypeStruct((n_seq,) + st_shape, F32)),
        grid=(n_seq,),
        in_specs=in_specs,
        out_specs=(pl.BlockSpec((seq_len, GROUP_WIDTH), lambda b: (b, 0)),
                   pl.BlockSpec((None,) + st_shape, lambda b: (b, 0, 0, 0, 0))),
        scratch_shapes=[pltpu.VMEM((seq_len, GROUP_WIDTH), F32), pltpu.VMEM(st_shape, F32),
                        pltpu.VMEM((DIRS, N_HEADS, MIX_CHUNK, MIX_CHUNK), F32),
                        pltpu.VMEM((DIRS, N_HEADS, MIX_CHUNK, HEAD_DIM), F32),
                        pltpu.VMEM((DIRS, N_HEADS, MIX_CHUNK, HEAD_DIM), F32),
                        pltpu.VMEM((DIRS, N_HEADS, MIX_CHUNK, HEAD_DIM), F32)],
        compiler_params=_mix_params(1),
        name="retention",
    )(*args)


def _rope_tables(l):
    cos, sin = _axial_rope(l)
    return jnp.concatenate([cos, cos], axis=-1), jnp.concatenate([-sin, sin], axis=-1)


S5_HALF_G = S5_GROUPS // 2
S5_HALF_U = S5_HALF_G * S5_CH
S5_HALF_X = S5_HALF_G * S5_N
S5_X = S5_GROUPS * S5_N
S5_TC = 512
S5_BLK = 8
S5_TABS = 4


def _s5_tables(a_re, a_im, log_step, b_re, b_im, c_re, c_im):
    dt = jnp.exp(log_step)[..., None]
    mag = jnp.exp(a_re * dt)
    ab_re = mag * jnp.cos(a_im * dt)
    ab_im = mag * jnp.sin(a_im * dt)
    den = a_re * a_re + a_im * a_im
    nr = ab_re - 1.0
    f_re = (nr * a_re + ab_im * a_im) / den
    f_im = (ab_im * a_re - nr * a_im) / den
    bb_re = f_re[..., None] * b_re - f_im[..., None] * b_im
    bb_im = f_re[..., None] * b_im + f_im[..., None] * b_re
    eye = jnp.eye(S5_HALF_G, dtype=F32)

    def in_mat(bb):
        bb = bb.reshape(DIRS, 2, S5_HALF_G, S5_N, S5_CH)
        return jnp.einsum('dhgnc,gk->dhgckn', bb, eye).reshape(DIRS, 2, S5_HALF_U, S5_HALF_X)

    def out_mat(cc):
        cc = cc.reshape(DIRS, 2, S5_HALF_G, S5_CH, S5_N)
        return jnp.einsum('dhgcn,gk->dhgnkc', cc, eye).reshape(DIRS, 2, S5_HALF_X, S5_HALF_U)

    bm = jnp.concatenate([in_mat(bb_re), in_mat(bb_im)], axis=-1).astype(BF16)
    cm = jnp.concatenate([out_mat(c_re), -out_mat(c_im)], axis=-2).astype(BF16)
    t = jnp.arange(S5_BLK, dtype=F32)
    order = jnp.stack([t, S5_BLK - 1.0 - t])
    shifts = 2.0 ** jnp.arange(S5_TABS - 1, dtype=F32)
    expo = jnp.concatenate([jnp.where(order[:, None, :] >= shifts[None, :, None], shifts[None, :, None], jnp.nan),
                            order[:, None, :] + 1.0], axis=1)
    live = ~jnp.isnan(expo)
    e = jnp.where(live, expo, 0.0)[..., None]
    adt_re = (a_re * dt).reshape(DIRS, 1, 1, S5_X)
    adt_im = (a_im * dt).reshape(DIRS, 1, 1, S5_X)
    pmag = jnp.where(live[..., None], jnp.exp(e * adt_re), 0.0)
    pw_re = pmag * jnp.cos(e * adt_im)
    pw_im = pmag * jnp.sin(e * adt_im)
    return bm, cm, pw_re, pw_im


def _gelu_tanh(x):
    return 0.5 * x * (1.0 + jnp.tanh(math.sqrt(2.0 / math.pi) * (x + 0.044715 * (x * x * x))))


def _s5_body(*refs, seq_len, has_s0):
    it = iter(refs)
    u_ref, bm_ref, cm_ref, pwr_ref, pwi_ref, d_ref, gw_ref, gb_ref = (next(it) for _ in range(8))
    x0r_ref, x0i_ref = (next(it), next(it)) if has_s0 else (None, None)
    o_ref, sfr_ref, sfi_ref, y_ref, xr_ref, xi_ref, xb_ref = (next(it) for _ in range(7))
    tc = min(seq_len, S5_TC)
    n_tiles = seq_len // tc
    n_pair = tc // (2 * S5_BLK)

    y_ref[...] = d_ref[...] * u_ref[...]
    unit = 0
    for d in range(DIRS):
        last = S5_BLK - 1 if d == 0 else 0
        for hf in range(2):
            xs = slice(hf * S5_HALF_X, (hf + 1) * S5_HALF_X)
            us = slice(hf * S5_HALF_U, (hf + 1) * S5_HALF_U)

            def scan_block(x_in, carry, d=d, xs=xs, last=last):
                xr, xi = x_in
                car_re, car_im = carry
                for k in range(S5_TABS - 1):
                    s = 1 << k
                    shift = s if d == 0 else S5_BLK - s
                    p_re, p_im = pwr_ref[d, k, :, xs], pwi_ref[d, k, :, xs]
                    sr, si = pltpu.roll(xr, shift, 0), pltpu.roll(xi, shift, 0)
                    xr, xi = xr + p_re * sr - p_im * si, xi + p_re * si + p_im * sr
                p_re, p_im = pwr_ref[d, S5_TABS - 1, :, xs], pwi_ref[d, S5_TABS - 1, :, xs]
                xr, xi = xr + p_re * car_re - p_im * car_im, xi + p_re * car_im + p_im * car_re
                return (xr, xi), (xr[last:last + 1, :], xi[last:last + 1, :])

            if has_s0:
                carry = (x0r_ref[d, :, xs], x0i_ref[d, :, xs])
            else:
                carry = (jnp.zeros((1, S5_HALF_X), F32), jnp.zeros((1, S5_HALF_X), F32))
            for i in range(n_tiles):
                slot = unit % 2
                unit += 1
                tile = i if d == 0 else n_tiles - 1 - i
                rows_t = slice(tile * tc, (tile + 1) * tc)
                bu = _dot(u_ref[rows_t, us], bm_ref[d, hf])
                xr_ref[slot] = bu[:, :S5_HALF_X]
                xi_ref[slot] = bu[:, S5_HALF_X:]
                for j in range(n_pair):
                    pair = j if d == 0 else n_pair - 1 - j
                    rows = slice(pair * 2 * S5_BLK, (pair + 1) * 2 * S5_BLK)
                    xr2, xi2 = xr_ref[slot, rows, :], xi_ref[slot, rows, :]
                    halves = [(xr2[:S5_BLK], xi2[:S5_BLK]), (xr2[S5_BLK:], xi2[S5_BLK:])]
                    out = [None, None]
                    for idx in ((0, 1) if d == 0 else (1, 0)):
                        out[idx], carry = scan_block(halves[idx], carry)
                    xb_ref[slot, rows, :S5_HALF_X] = jnp.concatenate([out[0][0], out[1][0]], axis=0).astype(BF16)
                    xb_ref[slot, rows, S5_HALF_X:] = jnp.concatenate([out[0][1], out[1][1]], axis=0).astype(BF16)
                y_ref[rows_t, us] += jnp.dot(xb_ref[slot], cm_ref[d, hf], preferred_element_type=F32)
            sfr_ref[d, :, xs] = carry[0]
            sfi_ref[d, :, xs] = carry[1]

    z = _gelu_tanh(y_ref[...])
    o_ref[...] = (z * jax.nn.sigmoid(_dot(z, gw_ref[...]) + gb_ref[...])).astype(BF16)


def _s5_pallas(pm, tables, s5_d, glu_w, glu_b, n_seq, seq_len, row_off, x0, layer):
    bm, cm, pw_re, pw_im = tables
    has_s0 = x0 is not None
    full = lambda shape: pl.BlockSpec(shape, lambda b: (0,) * len(shape))
    of_layer = lambda t: pl.BlockSpec((None,) + t.shape[1:], lambda b: (layer,) + (0,) * (t.ndim - 1))
    in_specs = [_seq_spec(seq_len, GROUP_WIDTH, row_off, 13),
                of_layer(bm), of_layer(cm), of_layer(pw_re), of_layer(pw_im),
                full((1, GROUP_WIDTH)), full((GROUP_WIDTH, GROUP_WIDTH)), full((1, GROUP_WIDTH))]
    args = [pm, bm, cm, pw_re, pw_im, s5_d.reshape(1, GROUP_WIDTH), glu_w.astype(BF16),
            glu_b.reshape(1, GROUP_WIDTH)]
    if has_s0:
        in_specs += [pl.BlockSpec((None, None, DIRS, 1, S5_X), lambda b: (b, layer, 0, 0, 0))] * 2
        args += list(x0)
    st = jax.ShapeDtypeStruct((n_seq, DIRS, 1, S5_X), F32)
    st_spec = pl.BlockSpec((None, DIRS, 1, S5_X), lambda b: (b, 0, 0, 0))
    tc = min(seq_len, S5_TC)
    return pl.pallas_call(
        functools.partial(_s5_body, seq_len=seq_len, has_s0=has_s0),
        out_shape=(jax.ShapeDtypeStruct((n_seq * seq_len, GROUP_WIDTH), BF16), st, st),
        grid=(n_seq,),
        in_specs=in_specs,
        out_specs=(pl.BlockSpec((seq_len, GROUP_WIDTH), lambda b: (b, 0)), st_spec, st_spec),
        scratch_shapes=[pltpu.VMEM((seq_len, GROUP_WIDTH), F32),
                        pltpu.VMEM((2, tc, S5_HALF_X), F32), pltpu.VMEM((2, tc, S5_HALF_X), F32),
                        pltpu.VMEM((2, tc, 2 * S5_HALF_X), BF16)],
        compiler_params=_mix_params(1),
        name="s5",
    )(*args)


GLA_LEVELS = 7


def _chunk_cumsum_rows(x, rowi):
    s = 1
    while s < MIX_CHUNK:
        x = x + jnp.where(rowi >= s, pltpu.roll(x, s, 0), 0.0)
        s *= 2
    return x


def _gla_body(*refs, n_chunks, has_s0):
    it = iter(refs)
    q_ref, f0_ref, f1_ref, i_ref, g_ref, lb_ref, nw_ref = (next(it) for _ in range(7))
    s0_ref = next(it) if has_s0 else None
    o_ref, sf_ref, acc_ref, s_ref, code_ref = (next(it) for _ in range(5))
    C, H = MIX_CHUNK, N_HEADS
    f_refs = (f0_ref, f1_ref)

    rowi = lax.broadcasted_iota(jnp.int32, (C, HEAD_DIM), 0)
    ri = lax.broadcasted_iota(jnp.int32, (C, C), 0)
    ci = lax.broadcasted_iota(jnp.int32, (C, C), 1)
    top_bit = 31 - lax.clz(ri ^ ci)
    code_ref[...] = jnp.where(ri > ci, top_bit, jnp.where(ri < ci, -1 - top_bit, GLA_LEVELS))
    if has_s0:
        s_ref[...] = s0_ref[...]
    else:
        s_ref[...] = jnp.zeros_like(s_ref)
    acc_ref[...] = jnp.zeros_like(acc_ref)

    def chunk_step(n, carry):
        for d in range(DIRS):
            c = n if d == 0 else n_chunks - 1 - n
            rows = pl.ds(pl.multiple_of(c * C, C), C)
            for h in range(H):
                cols = slice(h * HEAD_DIM, (h + 1) * HEAD_DIM)
                code = code_ref[...]
                q = _silu(q_ref[rows, cols]) * HEAD_DIM ** -0.5
                v = i_ref[rows, cols].astype(BF16)
                fx = f_refs[d][rows, cols]
                lb = lb_ref[d:d + 1, cols]
                sig = 1.0 / (1.0 + jnp.exp(-fx))
                logf = jnp.log2(jnp.maximum(lb, LB_FLOOR) + (1.0 - lb) * sig)
                k = (1.0 - lb) * (1.0 - sig)
                cum = _chunk_cumsum_rows(logf, rowi)
                own = cum
                attn = jnp.where(code == GLA_LEVELS, _dot_nt(q, k), 0.0)
                for lvl in range(GLA_LEVELS):
                    m = 1 << lvl
                    prev = pltpu.roll(own, m, 0)
                    pre = jnp.minimum(cum - prev, 0.0)
                    suf = own - cum
                    if d == 0:
                        sc = _dot_nt(q * jnp.exp2(pre), k * jnp.exp2(suf))
                        hit = code == lvl
                    else:
                        sc = _dot_nt(q * jnp.exp2(suf + logf), k * jnp.exp2(jnp.minimum(pre - logf, 0.0)))
                        hit = code == -1 - lvl
                    attn = jnp.where(hit, sc, attn)
                    own = jnp.where(((rowi >> lvl) & 1) == 0, pltpu.roll(own, C - m, 0), own)
                tot = own
                st = s_ref[d, h]
                if d == 0:
                    q_in, k_out = q * jnp.exp2(cum), k * jnp.exp2(tot - cum)
                else:
                    q_in, k_out = q * jnp.exp2(tot - cum + logf), k * jnp.exp2(cum - logf)
                acc_ref[rows, cols] += _dot(attn, v) + _dot_nt(q_in, st)
                s_ref[d, h] = jnp.exp2(tot[0:1, :]) * st + _dot_tn(v, k_out)
        return carry

    lax.fori_loop(0, n_chunks, chunk_step, 0)
    sf_ref[...] = s_ref[...]

    def finish(n, carry):
        rows = pl.ds(pl.multiple_of(n * C, C), C)
        for h in range(H):
            cols = slice(h * HEAD_DIM, (h + 1) * HEAD_DIM)
            o = acc_ref[rows, cols]
            y = o * lax.rsqrt(jnp.mean(o * o, axis=-1, keepdims=True) + EPS) * nw_ref[...]
            o_ref[rows, cols] = (y * _silu(g_ref[rows, cols])).astype(BF16)
        return carry

    lax.fori_loop(0, n_chunks, finish, 0)


def _gla_pallas(pm, lower_bound, norm_w, n_seq, seq_len, row_off, s0_t, layer):
    has_s0 = s0_t is not None
    st_shape = (DIRS, N_HEADS, HEAD_DIM, HEAD_DIM)
    in_specs = [_seq_spec(seq_len, GROUP_WIDTH, row_off, cb) for cb in (8, 9, 10, 11, 12)]
    in_specs += [pl.BlockSpec((DIRS, GROUP_WIDTH), lambda b: (0, 0)), pl.BlockSpec((1, HEAD_DIM), lambda b: (0, 0))]
    args = [pm] * 5 + [lower_bound, norm_w.reshape(1, HEAD_DIM)]
    if has_s0:
        in_specs.append(pl.BlockSpec((None, None) + st_shape, lambda b: (b, layer, 0, 0, 0, 0)))
        args.append(s0_t)
    return pl.pallas_call(
        functools.partial(_gla_body, n_chunks=seq_len // MIX_CHUNK, has_s0=has_s0),
        out_shape=(jax.ShapeDtypeStruct((n_seq * seq_len, GROUP_WIDTH), BF16),
                   jax.ShapeDtypeStruct((n_seq,) + st_shape, F32)),
        grid=(n_seq,),
        in_specs=in_specs,
        out_specs=(pl.BlockSpec((seq_len, GROUP_WIDTH), lambda b: (b, 0)),
                   pl.BlockSpec((None,) + st_shape, lambda b: (b, 0, 0, 0, 0))),
        scratch_shapes=[pltpu.VMEM((seq_len, GROUP_WIDTH), F32), pltpu.VMEM(st_shape, F32),
                        pltpu.VMEM((MIX_CHUNK, MIX_CHUNK), jnp.int32)],
        compiler_params=_mix_params(1),
        name="hgrn2",
    )(*args)


GDN_BLOCK_BITS = 4
GDN_MERGES = 3
N_GATES = DIRS * N_HEADS


def _softplus(x):
    return jnp.maximum(x, 0.0) + jnp.log1p(jnp.exp(-jnp.abs(x)))


def _gdn_body(*refs, seq_len, n_sub, has_s0):
    it = iter(refs)
    (q_ref, k_ref, v_ref, g_ref, ps_ref, pst_ref, cw_ref, prow_ref, pcol_ref, nw_ref) = (next(it) for _ in range(10))
    s0_ref = next(it) if has_s0 else None
    (o_ref, sf_ref, acc_ref, s_ref, qn_ref, kn_ref, vn_ref, gate_ref, gatet_ref, blev_ref) = (
        next(it) for _ in range(10))
    C, H, L = MIX_CHUNK, N_HEADS, seq_len
    R = n_sub * L
    n_chunks = L // C
    w = GROUP_WIDTH

    rowi = lax.broadcasted_iota(jnp.int32, (C, HEAD_DIM), 0)
    lane = lax.broadcasted_iota(jnp.int32, (C, HEAD_DIM), 1)
    lane_t = lax.broadcasted_iota(jnp.int32, (2 * N_GATES, C), 1)
    row_t = lax.broadcasted_iota(jnp.int32, (2 * N_GATES, C), 0)

    def prepare(c):
        rows = slice(c * C, (c + 1) * C)
        first, final = c % n_chunks == 0, c % n_chunks == n_chunks - 1
        for part, (src, dst) in enumerate(((q_ref, qn_ref), (k_ref, kn_ref), (v_ref, vn_ref))):
            for h in range(H):
                cols = slice(h * HEAD_DIM, (h + 1) * HEAD_DIM)
                wc = slice(part * w + h * HEAD_DIM, part * w + (h + 1) * HEAD_DIM)
                x = src[rows, cols]
                before = jnp.zeros((1, HEAD_DIM), F32) if first else src[c * C - 1:c * C, cols]
                after = jnp.zeros((1, HEAD_DIM), F32) if final else src[(c + 1) * C:(c + 1) * C + 1, cols]
                x_prev = jnp.where(rowi == 0, before, pltpu.roll(x, 1, 0))
                x_next = jnp.where(rowi == C - 1, after, pltpu.roll(x, C - 1, 0))
                y = _silu(cw_ref[0:1, wc] * x_prev + cw_ref[1:2, wc] * x + cw_ref[2:3, wc] * x_next)
                if part < 2:
                    y = y * lax.rsqrt(jnp.sum(y * y, axis=-1, keepdims=True) + EPS)
                if part == 0:
                    y = y * HEAD_DIM ** -0.5
                dst[rows, cols] = y

        a = ps_ref[rows, :]
        la = -jnp.exp(prow_ref[0:1, :]) * _softplus(a + prow_ref[1:2, :])
        pre = _chunk_cumsum_rows(la, rowi)
        tot = jnp.broadcast_to(pre[C - 1:C, :], (C, HEAD_DIM))
        g = jnp.where(lane < N_HEADS, pre, tot - pre + la)
        gate_ref[rows, :] = jnp.where(lane < N_GATES, g,
                                      jnp.where(lane < 2 * N_GATES, jax.nn.sigmoid(a),
                                                pltpu.roll(tot, 2 * N_GATES, 1)))
        at = pst_ref[:, rows]
        lat = -jnp.exp(pcol_ref[:, 0:1]) * _softplus(at + pcol_ref[:, 1:2])
        pre_t, s = lat, 1
        while s < C:
            pre_t = pre_t + jnp.where(lane_t >= s, pltpu.roll(pre_t, s, 1), 0.0)
            s *= 2
        tot_t = jnp.broadcast_to(pre_t[:, C - 1:C], (2 * N_GATES, C))
        g_t = jnp.where(row_t < N_HEADS, pre_t, tot_t - pre_t + lat)
        gatet_ref[:, rows] = jnp.where(row_t < N_GATES, g_t, jax.nn.sigmoid(at))

    ri = lax.broadcasted_iota(jnp.int32, (C, C), 0)
    ci = lax.broadcasted_iota(jnp.int32, (C, C), 1)
    bx = (ri >> GDN_BLOCK_BITS) ^ (ci >> GDN_BLOCK_BITS)
    blev_ref[...] = jnp.where(bx == 0, 0, 32 - lax.clz(bx))
    if has_s0:
        for s in range(n_sub):
            s_ref[s * DIRS:(s + 1) * DIRS] = s0_ref[s]
    else:
        s_ref[...] = jnp.zeros_like(s_ref)
    acc_ref[...] = jnp.zeros_like(acc_ref)

    units = [(s, d, h) for s in range(n_sub) for d in range(DIRS) for h in range(H)]

    def rows_of(n, s, d):
        c = n if d == 0 else n_chunks - 1 - n
        return slice(s * L + c * C, s * L + (c + 1) * C)

    def state_free_phases(step_units):
        eye = jnp.where(ri == ci, 1.0, 0.0)
        blev = blev_ref[...]
        incl_d = [ri >= ci, ri <= ci]
        strict_d = [ri > ci, ri < ci]
        q_l, et_l, ek_l, rhs_l, a_l, attn_l = ([] for _ in range(6))
        for n, s, d, h in step_units:
            rows, cols, cg = rows_of(n, s, d), slice(h * HEAD_DIM, (h + 1) * HEAD_DIM), d * N_HEADS + h
            q, k, v = qn_ref[rows, cols], kn_ref[rows, cols], vn_ref[rows, cols]
            g_i = jnp.broadcast_to(gate_ref[rows, cg:cg + 1], (C, C))
            b_i = jnp.broadcast_to(gate_ref[rows, N_GATES + cg:N_GATES + cg + 1], (C, C))
            tot = jnp.broadcast_to(gate_ref[rows, 2 * N_GATES + cg:2 * N_GATES + cg + 1], (C, C))
            g_j = gatet_ref[cg:cg + 1, rows]
            decay = jnp.where(incl_d[d], jnp.exp(jnp.minimum(g_i - g_j, 0.0)), 0.0)
            e_g = jnp.exp(g_i)
            kb = k.astype(BF16)
            a_l.append(jnp.where(strict_d[d], _dot_nt(kb, kb) * b_i * decay, 0.0))
            attn_l.append(_dot_nt(q, kb) * decay)
            rhs_l.append(jnp.concatenate([v * b_i, k * (b_i * e_g)], axis=1).astype(BF16))
            q_l.append(q * e_g)
            ek_l.append(k * jnp.exp(tot - g_i))
            et_l.append(jnp.exp(tot))
        b_l = [-jnp.where(blev == 0, a, 0.0) for a in a_l]
        p_l = [eye + b for b in b_l]
        for _ in range(GDN_BLOCK_BITS - 1):
            b_l = [_dot(b, b) for b in b_l]
            p_l = [p + _dot(p, b) for p, b in zip(p_l, b_l)]
        for lvl in range(1, GDN_MERGES + 1):
            ep_l = [_dot(jnp.where(blev == lvl, a, 0.0), p) for a, p in zip(a_l, p_l)]
            p_l = [p - _dot(p, ep) for p, ep in zip(p_l, ep_l)]
        uw_l = [_dot(p, rhs) for p, rhs in zip(p_l, rhs_l)]
        lhs_l = [jnp.concatenate([q, attn], axis=1).astype(BF16) for q, attn in zip(q_l, attn_l)]
        return list(zip(uw_l, lhs_l, ek_l, et_l))

    def state_phase(n, vals):
        s_l = [s_ref[s * DIRS + d, h] for s, d, h in units]
        vn_l = [uw[:, :HEAD_DIM] - _dot(uw[:, HEAD_DIM:], st) for (uw, _, _, _), st in zip(vals, s_l)]
        for i, (s, d, h) in enumerate(units):
            cols = slice(h * HEAD_DIM, (h + 1) * HEAD_DIM)
            _, lhs, ek, et = vals[i]
            acc_ref[rows_of(n, s, d), cols] += _dot(lhs, jnp.concatenate([s_l[i], vn_l[i]], axis=0))
            s_ref[s * DIRS + d, h] = s_l[i] * et + _dot_tn(ek, vn_l[i])

    for c in range(R // C):
        prepare(c)
    group = max(1, GDN_GROUP_UNITS // len(units))
    for n0 in range(0, n_chunks, group):
        steps = range(n0, min(n0 + group, n_chunks))
        vals = state_free_phases([(n,) + u for n in steps for u in units])
        for j, n in enumerate(steps):
            state_phase(n, vals[j * len(units):(j + 1) * len(units)])
    sf_ref[...] = s_ref[...]

    for n in range(R // C):
        rows = slice(n * C, (n + 1) * C)
        for h in range(H):
            cols = slice(h * HEAD_DIM, (h + 1) * HEAD_DIM)
            o = acc_ref[rows, cols]
            y = o * lax.rsqrt(jnp.mean(o * o, axis=-1, keepdims=True) + EPS) * nw_ref[...]
            o_ref[rows, cols] = (y * _silu(g_ref[rows, cols])).astype(BF16)


GDN_GROUP_UNITS = 64
GDN_CTX_SUB = 2
GDN_LAT_SUB = 1
GDN_DOUBLE_BUFFER_BYTES = 2 * 1024 * 1024


def _gdn_pallas(pm, ps, ps_t, conv_w, a_log, dt_bias, norm_w, n_seq, seq_len, row_off, s0, layer):
    has_s0 = s0 is not None
    n_sub = GDN_LAT_SUB if has_s0 else GDN_CTX_SUB
    n_seq, seq_len, row_off = n_seq // n_sub, seq_len * n_sub, row_off // n_sub
    st_shape = (n_sub * DIRS, N_HEADS, HEAD_DIM, HEAD_DIM)
    par = jnp.stack([a_log.reshape(N_GATES), dt_bias.reshape(N_GATES)])
    par_row = jnp.pad(par, ((0, 0), (0, HEAD_DIM - N_GATES)))
    par_col = jnp.pad(par.T, ((0, N_GATES), (0, 0)))
    full = lambda shape: pl.BlockSpec(shape, lambda b: (0,) * len(shape))
    big_mode = pl.Buffered(1) if seq_len * GROUP_WIDTH * 4 > GDN_DOUBLE_BUFFER_BYTES else None
    in_specs = [_seq_spec(seq_len, GROUP_WIDTH, row_off, cb, big_mode) for cb in (4, 5, 6, 7)]
    in_specs += [_seq_spec(seq_len, PROJ_SMALL, row_off, 0),
                 pl.BlockSpec((2 * N_GATES, seq_len), lambda b: (0, row_off + b)),
                 full((GDN_CONV, 3 * GROUP_WIDTH)), full((2, HEAD_DIM)), full((2 * N_GATES, 2)), full((1, HEAD_DIM))]
    args = [pm] * 4 + [ps, ps_t, conv_w, par_row, par_col, norm_w.reshape(1, HEAD_DIM)]
    if has_s0:
        in_specs.append(pl.BlockSpec((n_sub, None, DIRS, N_HEADS, HEAD_DIM, HEAD_DIM),
                                     lambda b: (b, layer, 0, 0, 0, 0)))
        args.append(s0)
    seq_f32 = lambda width: pltpu.VMEM((seq_len, width), F32)
    out, s_fin = pl.pallas_call(
        functools.partial(_gdn_body, seq_len=seq_len // n_sub, n_sub=n_sub, has_s0=has_s0),
        out_shape=(jax.ShapeDtypeStruct((n_seq * seq_len, GROUP_WIDTH), BF16),
                   jax.ShapeDtypeStruct((n_seq,) + st_shape, F32)),
        grid=(n_seq,),
        in_specs=in_specs,
        out_specs=(pl.BlockSpec((seq_len, GROUP_WIDTH), lambda b: (b, 0)),
                   pl.BlockSpec((None,) + st_shape, lambda b: (b, 0, 0, 0, 0))),
        scratch_shapes=[seq_f32(GROUP_WIDTH), pltpu.VMEM(st_shape, F32),
                        seq_f32(GROUP_WIDTH), seq_f32(GROUP_WIDTH), seq_f32(GROUP_WIDTH),
                        seq_f32(HEAD_DIM), pltpu.VMEM((2 * N_GATES, seq_len), F32),
                        pltpu.VMEM((MIX_CHUNK, MIX_CHUNK), jnp.int32)],
        compiler_params=_mix_params(1),
        name="gated_delta",
    )(*args)
    return out, s_fin.reshape(n_seq * n_sub, DIRS, N_HEADS, HEAD_DIM, HEAD_DIM)


def kernel(x_prompt, x_sample, state_ret, state_gdn, state_hgrn, state_s5_re, state_s5_im, c, c_ctx, norm1_w, norm2_w, final_norm_w, ada_w, ada_b, in_proj, out_proj, ret_decay_logit, gdn_conv, gdn_a_log, gdn_dt_bias, gdn_norm_w, hg_lb_param, hg_norm_w, s5_a_re, s5_a_im, s5_b_re, s5_b_im, s5_c_re, s5_c_im, s5_log_step, s5_d, s5_glu_w, s5_glu_b, ffn_w1, ffn_w3, ffn_w2):
    lb_soft = jax.nn.softmax(hg_lb_param, axis=0)
    lower_bounds = jnp.cumsum(lb_soft, axis=0) - lb_soft[0]
    rope2 = _rope_tables(DEC_SEQ)

    cvec = jnp.zeros((N_SEQ_ROWS, D_MODEL), F32).at[0].set(c_ctx).at[1:1 + DEC_BATCH].set(c)
    mod_all = _ada(cvec, ada_w, ada_b).reshape(DEPTH, N_SEQ_ROWS, N_MOD, 1, D_MODEL)

    gate0 = 8 * GROUP_WIDTH
    w_in = in_proj.astype(BF16)
    w_main = jnp.concatenate([w_in[:, :, :gate0], w_in[:, :, gate0 + 2 * N_GATES:]], axis=-1)
    w_small = jnp.pad(w_in[:, :, gate0:gate0 + 2 * N_GATES], ((0, 0), (0, 0), (0, PROJ_SMALL - 2 * N_GATES)))
    w_out = out_proj.astype(BF16)
    w1, w3, w2 = ffn_w1.astype(BF16), ffn_w3.astype(BF16), ffn_w2.astype(BF16)

    x = jnp.concatenate([x_prompt.reshape(N_CTX_TOK, D_MODEL), x_sample.reshape(N_LAT_TOK, D_MODEL)], axis=0)
    hgrn_t = jnp.swapaxes(state_hgrn, -1, -2)
    s5_x0 = (state_s5_re.reshape(DEC_BATCH, DEPTH, DIRS, 1, S5_X), state_s5_im.reshape(DEC_BATCH, DEPTH, DIRS, 1, S5_X))
    lat_off = N_CTX_TOK // DEC_SEQ
    tables = jax.vmap(_s5_tables)(s5_a_re, s5_a_im, s5_log_step, s5_b_re, s5_b_im, s5_c_re, s5_c_im)
    ctx_states = []
    for i in range(DEPTH):
        mod = mod_all[i]
        pm, ps = _inproj(x, norm1_w[i][None], mod, w_main, w_small, i)
        ps_t = ps[:, :2 * N_GATES].T
        lg = jax.nn.log_sigmoid(ret_decay_logit[i])
        ctx = (BATCH, SEQ, 0)
        lat = (DEC_BATCH, DEC_SEQ, lat_off)

        ret_c, rs = _retention_pallas(pm, lg, *ctx, None, None, i)
        ret_l, _ = _retention_pallas(pm, lg, *lat, rope2, state_ret, i)
        gdn_args = (pm, ps, ps_t, gdn_conv[i], gdn_a_log[i], gdn_dt_bias[i], gdn_norm_w[i])
        gdn_c, gs = _gdn_pallas(*gdn_args, *ctx, None, i)
        gdn_l, _ = _gdn_pallas(*gdn_args, *lat, state_gdn, i)
        hg_c, hs = _gla_pallas(pm, lower_bounds[i], hg_norm_w[i], *ctx, None, i)
        hg_l, _ = _gla_pallas(pm, lower_bounds[i], hg_norm_w[i], *lat, hgrn_t, i)
        s5_args = (pm, tables, s5_d[i], s5_glu_w[i], s5_glu_b[i])
        s5_c, xr, xi = _s5_pallas(*s5_args, *ctx, None, i)
        s5_l, _, _ = _s5_pallas(*s5_args, *lat, s5_x0, i)
        ctx_states.append((rs, gs, jnp.swapaxes(hs, -1, -2), xr.reshape(BATCH, DIRS, S5_GROUPS, S5_N),
                           xi.reshape(BATCH, DIRS, S5_GROUPS, S5_N)))

        parts = ((ret_c, ret_l), (gdn_c, gdn_l), (hg_c, hg_l), (s5_c, s5_l))
        x = _outproj(parts, w_out, x, mod, i)
        x = _ffn(x, norm2_w[i][None], mod, w1, w3, w2, final_norm_w[None], i)

    y_prompt = x[0].reshape(BATCH, SEQ, D_MODEL)
    y_sample = x[1].reshape(DEC_BATCH, DEC_SEQ, D_MODEL)
    new_states = tuple(jnp.stack([s[j] for s in ctx_states], axis=1) for j in range(5))
    return (y_prompt, y_sample) + new_states
```

```python
import functools
import math

import jax
import jax.numpy as jnp
from jax import lax
from jax.experimental import pallas as pl
from jax.experimental.pallas import tpu as pltpu

F32 = jnp.float32
BF16 = jnp.bfloat16

D_MODEL = 2048
BATCH = 16
SEQ = 256
DEPTH = 2
DEC_BATCH = 8
DEC_SEQ = 1024
GRID_W = 64
HEAD_DIM = 128
GROUP_WIDTH = 512
N_HEADS = 4
S5_CH = 16
S5_GROUPS = 32
S5_N = 64
GDN_CONV = 3
ROPE_BASE = 10000.0
FFN_HIDDEN = 5632
N_MOD = 6
EPS = 1e-6
LB_FLOOR = 1e-30

N_CTX_TOK = BATCH * SEQ
N_LAT_TOK = DEC_BATCH * DEC_SEQ
N_TOK = N_CTX_TOK + N_LAT_TOK
N_SEQ_ROWS = 16
PROJ_MAIN = 14 * GROUP_WIDTH
PROJ_SMALL = 128
V7X_VMEM_BYTES = 64 * 1024 * 1024
VMEM_LIMIT = V7X_VMEM_BYTES - 8 * 1024 * 1024
ADA_TN = 1024
OUT_TN = 1024


def _seq_row(tile, tm):
    n_ctx = N_CTX_TOK // tm
    per_lat = DEC_SEQ // tm
    return jnp.where(tile < n_ctx, 0, 1 + (tile - n_ctx) // per_lat)


def _stream_specs(tm, width, n_col=1):
    n_ctx = N_CTX_TOK // tm

    def ctx_map(i, j):
        return jnp.minimum(i, n_ctx - 1), (jnp.where(i < n_ctx, j, n_col - 1) if n_col > 1 else 0)

    def lat_map(i, j):
        return jnp.maximum(i - n_ctx, 0), (jnp.where(i >= n_ctx, j, 0) if n_col > 1 else 0)

    return pl.BlockSpec((tm, width), ctx_map), pl.BlockSpec((tm, width), lat_map)


def _on_stream(tile, tm, fn):
    n_ctx = N_CTX_TOK // tm
    pl.when(tile < n_ctx)(functools.partial(fn, 0))
    pl.when(tile >= n_ctx)(functools.partial(fn, 1))


def _ada_body(c_ref, w_ref, b_ref, o_ref):
    cv = c_ref[...]
    s = cv * jax.nn.sigmoid(cv)
    o_ref[0] = jnp.dot(s.astype(BF16), w_ref[0].astype(BF16), preferred_element_type=F32) + b_ref[0]


def _ada(cvec, ada_w, ada_b):
    tn = ADA_TN
    n = N_MOD * D_MODEL
    return pl.pallas_call(
        _ada_body,
        out_shape=jax.ShapeDtypeStruct((DEPTH, N_SEQ_ROWS, n), F32),
        grid=(DEPTH, n // tn),
        in_specs=[
            pl.BlockSpec((N_SEQ_ROWS, D_MODEL), lambda l, j: (0, 0)),
            pl.BlockSpec((1, D_MODEL, tn), lambda l, j: (l, 0, j)),
            pl.BlockSpec((1, 1, tn), lambda l, j: (l, 0, j)),
        ],
        out_specs=pl.BlockSpec((1, N_SEQ_ROWS, tn), lambda l, j: (l, 0, j)),
        compiler_params=pltpu.CompilerParams(
            dimension_semantics=("parallel", "parallel"), vmem_limit_bytes=VMEM_LIMIT),
        name="ada_mod",
    )(cvec, ada_w, ada_b.reshape(DEPTH, 1, n))


def _norm_mod(x, nw, sc, sh):
    ms = jnp.mean(x * x, axis=-1, keepdims=True)
    y = x * lax.rsqrt(ms + EPS) * nw
    return y * (1.0 + sc) + sh


PROJ_TM = 1024
PROJ_TN = 1024


def _inproj_body(x_ref, nw_ref, sc_ref, sh_ref, w_ref, ws_ref, o_ref, os_ref, h_ref):
    @pl.when(pl.program_id(1) == 0)
    def _():
        hb = _norm_mod(x_ref[...], nw_ref[...], sc_ref[...], sh_ref[...]).astype(BF16)
        h_ref[...] = hb
        os_ref[...] = jnp.dot(hb, ws_ref[...], preferred_element_type=F32)

    o_ref[...] = jnp.dot(h_ref[...], w_ref[...], preferred_element_type=F32)


def _inproj(x, nw, mod, w_main, w_small, layer):
    tm, tn = PROJ_TM, PROJ_TN
    return pl.pallas_call(
        _inproj_body,
        out_shape=(jax.ShapeDtypeStruct((N_TOK, PROJ_MAIN), F32),
                   jax.ShapeDtypeStruct((N_TOK, PROJ_SMALL), F32)),
        grid=(N_TOK // tm, PROJ_MAIN // tn),
        in_specs=[
            pl.BlockSpec((tm, D_MODEL), lambda i, j: (i, 0)),
            pl.BlockSpec((1, D_MODEL), lambda i, j: (0, 0)),
            pl.BlockSpec((None, None, 1, D_MODEL), lambda i, j: (_seq_row(i, tm), 1, 0, 0)),
            pl.BlockSpec((None, None, 1, D_MODEL), lambda i, j: (_seq_row(i, tm), 0, 0, 0)),
            pl.BlockSpec((None, D_MODEL, tn), lambda i, j: (layer, 0, j)),
            pl.BlockSpec((None, D_MODEL, PROJ_SMALL), lambda i, j: (layer, 0, 0)),
        ],
        out_specs=(pl.BlockSpec((tm, tn), lambda i, j: (i, j)),
                   pl.BlockSpec((tm, PROJ_SMALL), lambda i, j: (i, 0))),
        scratch_shapes=[pltpu.VMEM((tm, D_MODEL), BF16)],
        compiler_params=pltpu.CompilerParams(
            dimension_semantics=("parallel", "arbitrary"), vmem_limit_bytes=VMEM_LIMIT),
        name="in_proj",
    )(x, nw, mod, mod, w_main, w_small)


def _outproj_body(*refs):
    n_mix = 4
    m_refs, (w_ref, x_ref, g_ref, o_ref) = refs[:2 * n_mix], refs[2 * n_mix:]

    def compute(stream):
        acc = None
        for p in range(n_mix):
            part = jnp.dot(m_refs[2 * p + stream][...], w_ref[p * GROUP_WIDTH:(p + 1) * GROUP_WIDTH, :],
                           preferred_element_type=F32)
            acc = part if acc is None else acc + part
        o_ref[...] = x_ref[...] + g_ref[...] * acc

    _on_stream(pl.program_id(0), PROJ_TM, compute)


def _outproj(parts, w, x, mod, layer):
    tm, tn = PROJ_TM, OUT_TN
    in_specs = []
    for _ in parts:
        in_specs += list(_stream_specs(tm, GROUP_WIDTH))
    in_specs += [pl.BlockSpec((None, D_MODEL, tn), lambda i, j: (layer, 0, j)),
                 pl.BlockSpec((tm, tn), lambda i, j: (i, j)),
                 pl.BlockSpec((None, None, 1, tn), lambda i, j: (_seq_row(i, tm), 2, 0, j))]
    return pl.pallas_call(
        _outproj_body,
        out_shape=jax.ShapeDtypeStruct((N_TOK, D_MODEL), F32),
        grid=(N_TOK // tm, D_MODEL // tn),
        in_specs=in_specs,
        out_specs=pl.BlockSpec((tm, tn), lambda i, j: (i, j)),
        compiler_params=pltpu.CompilerParams(
            dimension_semantics=("parallel", "arbitrary"), vmem_limit_bytes=VMEM_LIMIT),
        name="out_proj",
    )(*(a for pair in parts for a in pair), w, x, mod)


FFN_TM = 512
FFN_TH = 512


def _ffn_body(x_ref, nw_ref, sc_ref, sh_ref, g_ref, w1_ref, w3_ref, w2_ref, fw_ref, *rest, final_norm):
    o_refs, (h_ref, acc_ref) = rest[:-2], rest[-2:]
    i, k = pl.program_id(0), pl.program_id(1)

    @pl.when(k == 0)
    def _():
        h_ref[...] = _norm_mod(x_ref[...], nw_ref[...], sc_ref[...], sh_ref[...]).astype(BF16)
        acc_ref[...] = jnp.zeros_like(acc_ref)

    h = h_ref[...]
    a = jnp.dot(h, w1_ref[...], preferred_element_type=F32)
    b = jnp.dot(h, w3_ref[...], preferred_element_type=F32)
    g = (a * jax.nn.sigmoid(a) * b).astype(BF16)
    acc_ref[...] += jnp.dot(g, w2_ref[...], preferred_element_type=F32)

    @pl.when(k == pl.num_programs(1) - 1)
    def _():
        y = x_ref[...] + g_ref[...] * acc_ref[...]
        if not final_norm:
            o_refs[0][...] = y
        else:
            ms = jnp.mean(y * y, axis=-1, keepdims=True)
            y = y * lax.rsqrt(ms + EPS) * fw_ref[...]

            def write(stream):
                o_refs[stream][...] = y

            _on_stream(i, FFN_TM, write)


def _ffn(x, nw, mod, w1, w3, w2, fw, layer):
    final_norm = layer == DEPTH - 1
    tm, th = FFN_TM, FFN_TH
    if final_norm:
        out_shape = (jax.ShapeDtypeStruct((N_CTX_TOK, D_MODEL), F32), jax.ShapeDtypeStruct((N_LAT_TOK, D_MODEL), F32))
        out_specs = _stream_specs(tm, D_MODEL)
    else:
        out_shape = jax.ShapeDtypeStruct((N_TOK, D_MODEL), F32)
        out_specs = pl.BlockSpec((tm, D_MODEL), lambda i, k: (i, 0))
    return pl.pallas_call(
        functools.partial(_ffn_body, final_norm=final_norm),
        out_shape=out_shape,
        grid=(N_TOK // tm, FFN_HIDDEN // th),
        in_specs=[
            pl.BlockSpec((tm, D_MODEL), lambda i, k: (i, 0)),
            pl.BlockSpec((1, D_MODEL), lambda i, k: (0, 0)),
            pl.BlockSpec((None, None, 1, D_MODEL), lambda i, k: (_seq_row(i, tm), 4, 0, 0)),
            pl.BlockSpec((None, None, 1, D_MODEL), lambda i, k: (_seq_row(i, tm), 3, 0, 0)),
            pl.BlockSpec((None, None, 1, D_MODEL), lambda i, k: (_seq_row(i, tm), 5, 0, 0)),
            pl.BlockSpec((None, D_MODEL, th), lambda i, k: (layer, 0, k)),
            pl.BlockSpec((None, D_MODEL, th), lambda i, k: (layer, 0, k)),
            pl.BlockSpec((None, th, D_MODEL), lambda i, k: (layer, k, 0)),
            pl.BlockSpec((1, D_MODEL), lambda i, k: (0, 0)),
        ],
        out_specs=out_specs,
        scratch_shapes=[pltpu.VMEM((tm, D_MODEL), BF16), pltpu.VMEM((tm, D_MODEL), F32)],
        compiler_params=pltpu.CompilerParams(
            dimension_semantics=("arbitrary", "arbitrary"), vmem_limit_bytes=VMEM_LIMIT),
        name="ffn",
    )(x, nw, mod, mod, mod, w1, w3, w2, fw)


def _axial_rope(l):
    n_rows = l // GRID_W
    t_row = jnp.repeat(jnp.arange(n_rows, dtype=F32), GRID_W)
    t_col = jnp.tile(jnp.arange(GRID_W, dtype=F32), n_rows)
    n_freq = HEAD_DIM // 4
    inv = ROPE_BASE ** (-jnp.arange(n_freq, dtype=F32) / n_freq)
    ang = jnp.concatenate([t_row[:, None] * inv, t_col[:, None] * inv], axis=-1)
    return jnp.cos(ang), jnp.sin(ang)


MIX_CHUNK = 128
DIRS = 2
NT_DIMS = (((1,), (1,)), ((), ()))
TN_DIMS = (((0,), (0,)), ((), ()))


def _dot(a, b):
    return jnp.dot(a.astype(BF16), b.astype(BF16), preferred_element_type=F32)


def _dot_nt(a, b):
    return lax.dot_general(a.astype(BF16), b.astype(BF16), NT_DIMS, preferred_element_type=F32)


def _dot_tn(a, b):
    return lax.dot_general(a.astype(BF16), b.astype(BF16), TN_DIMS, preferred_element_type=F32)


def _silu(x):
    return x * jax.nn.sigmoid(x)


def _seq_spec(seq_len, width, row_off, col_blk, pipeline_mode=None):
    return pl.BlockSpec((seq_len, width), lambda b: (row_off + b, col_blk), pipeline_mode=pipeline_mode)


def _mix_params(n_par):
    return pltpu.CompilerParams(dimension_semantics=("parallel",) * n_par, vmem_limit_bytes=VMEM_LIMIT)


def _ret_body(*refs, n_chunks, use_rope, has_s0):
    it = iter(refs)
    lg_ref, q_ref, k_ref, v_ref, g_ref = (next(it) for _ in range(5))
    cos_ref, sin_ref = (next(it), next(it)) if use_rope else (None, None)
    s0_ref = next(it) if has_s0 else None
    o_ref, sf_ref, acc_ref, s_ref, intra_ref, qd_ref, kd_ref, cd_ref = (next(it) for _ in range(8))
    C, H = MIX_CHUNK, N_HEADS

    row = lax.broadcasted_iota(jnp.int32, (C, C), 0)
    col = lax.broadcasted_iota(jnp.int32, (C, C), 1)
    rel = (row - col).astype(F32)
    pos = lax.broadcasted_iota(jnp.int32, (C, HEAD_DIM), 0).astype(F32)
    for d in range(DIRS):
        for h in range(H):
            lg = lg_ref[d, h]
            if d == 0:
                intra_ref[d, h] = jnp.where(rel >= 0, jnp.exp(jnp.maximum(rel, 0.0) * lg), 0.0)
                qd_ref[d, h] = jnp.exp((pos + 1.0) * lg)
                kd_ref[d, h] = jnp.exp((C - 1.0 - pos) * lg)
            else:
                intra_ref[d, h] = jnp.where(rel <= 0, jnp.exp(jnp.maximum(-rel, 0.0) * lg), 0.0)
                qd_ref[d, h] = jnp.exp((C - pos) * lg)
                kd_ref[d, h] = jnp.exp(pos * lg)
            cd_ref[d, h] = jnp.exp(jnp.full((C, HEAD_DIM), C, F32) * lg)
    if has_s0:
        s_ref[...] = s0_ref[...]
    else:
        s_ref[...] = jnp.zeros_like(s_ref)
    acc_ref[...] = jnp.zeros_like(acc_ref)

    units = [(d, h) for d in range(DIRS) for h in range(H)]

    def chunk_step(n):
        rows_d = [slice(c * C, (c + 1) * C) for c in (n, n_chunks - 1 - n)]
        q_l, k_l, v_l, sc_l = [], [], [], []
        for d, h in units:
            rows, cols = rows_d[d], slice(h * HEAD_DIM, (h + 1) * HEAD_DIM)
            q, k = q_ref[rows, cols], k_ref[rows, cols]
            if use_rope:
                cs, sn = cos_ref[rows, :], sin_ref[rows, :]
                q = q * cs + pltpu.roll(q, HEAD_DIM // 2, 1) * sn
                k = k * cs + pltpu.roll(k, HEAD_DIM // 2, 1) * sn
            k = k * HEAD_DIM ** -0.5
            q_l.append(q)
            k_l.append(k)
            v_l.append(v_ref[rows, cols].astype(BF16))
            sc_l.append(_dot_nt(q, k) * intra_ref[d, h])
        for i, (d, h) in enumerate(units):
            cols = slice(h * HEAD_DIM, (h + 1) * HEAD_DIM)
            s = s_ref[d, h]
            acc_ref[rows_d[d], cols] += _dot(jnp.concatenate([sc_l[i], q_l[i] * qd_ref[d, h]], axis=1),
                                             jnp.concatenate([v_l[i], s.astype(BF16)], axis=0))
            s_ref[d, h] = cd_ref[d, h] * s + _dot_tn(k_l[i] * kd_ref[d, h], v_l[i])

    for n in range(n_chunks):
        chunk_step(n)
    sf_ref[...] = s_ref[...]

    for n in range(n_chunks):
        rows = slice(n * C, (n + 1) * C)
        for h in range(H):
            cols = slice(h * HEAD_DIM, (h + 1) * HEAD_DIM)
            o = acc_ref[rows, cols]
            mu = jnp.mean(o, axis=-1, keepdims=True)
            oc = o - mu
            y = oc * lax.rsqrt(jnp.mean(oc * oc, axis=-1, keepdims=True) + EPS)
            o_ref[rows, cols] = (y * _silu(g_ref[rows, cols])).astype(BF16)


def _retention_pallas(pm, log_gamma, n_seq, seq_len, row_off, rope2, s0, layer):
    use_rope, has_s0 = rope2 is not None, s0 is not None
    st_shape = (DIRS, N_HEADS, HEAD_DIM, HEAD_DIM)
    in_specs = [pl.BlockSpec(memory_space=pltpu.SMEM)]
    in_specs += [_seq_spec(seq_len, GROUP_WIDTH, row_off, cb) for cb in range(4)]
    args = [log_gamma, pm, pm, pm, pm]
    if use_rope:
        in_specs += [pl.BlockSpec((seq_len, HEAD_DIM), lambda b: (0, 0))] * 2
        args += list(rope2)
    if has_s0:
        in_specs.append(pl.BlockSpec((None, None) + st_shape, lambda b: (b, layer, 0, 0, 0, 0)))
        args.append(s0)
    return pl.pallas_call(
        functools.partial(_ret_body, n_chunks=seq_len // MIX_CHUNK, use_rope=use_rope, has_s0=has_s0),
        out_shape=(jax.ShapeDtypeStruct((n_seq * seq_len, GROUP_WIDTH), BF16),
                   jax.ShapeDtypeStruct((n_seq,) + st_shape, F32)),
        grid=(n_seq,),
        in_specs=in_specs,
        out_specs=(pl.BlockSpec((seq_len, GROUP_WIDTH), lambda b: (b, 0)),
                   pl.BlockSpec((None,) + st_shape, lambda b: (b, 0, 0, 0, 0))),
        scratch_shapes=[pltpu.VMEM((seq_len, GROUP_WIDTH), F32), pltpu.VMEM(st_shape, F32),
                        pltpu.VMEM((DIRS, N_HEADS, MIX_CHUNK, MIX_CHUNK), F32),
                        pltpu.VMEM((DIRS, N_HEADS, MIX_CHUNK, HEAD_DIM), F32),
                        pltpu.VMEM((DIRS, N_HEADS, MIX_CHUNK, HEAD_DIM), F32),
                        pltpu.VMEM((DIRS, N_HEADS, MIX_CHUNK, HEAD_DIM), F32)],
        compiler_params=_mix_params(1),
        name="retention",
    )(*args)


def _rope_tables(l):
    cos, sin = _axial_rope(l)
    return jnp.concatenate([cos, cos], axis=-1), jnp.concatenate([-sin, sin], axis=-1)


S5_HALF_G = S5_GROUPS // 2
S5_HALF_U = S5_HALF_G * S5_CH
S5_HALF_X = S5_HALF_G * S5_N
S5_X = S5_GROUPS * S5_N
S5_TC = 512
S5_BLK = 8
S5_TABS = 4


def _s5_tables(a_re, a_im, log_step, b_re, b_im, c_re, c_im):
    dt = jnp.exp(log_step)[..., None]
    mag = jnp.exp(a_re * dt)
    ab_re = mag * jnp.cos(a_im * dt)
    ab_im = mag * jnp.sin(a_im * dt)
    den = a_re * a_re + a_im * a_im
    nr = ab_re - 1.0
    f_re = (nr * a_re + ab_im * a_im) / den
    f_im = (ab_im * a_re - nr * a_im) / den
    bb_re = f_re[..., None] * b_re - f_im[..., None] * b_im
    bb_im = f_re[..., None] * b_im + f_im[..., None] * b_re
    eye = jnp.eye(S5_HALF_G, dtype=F32)

    def in_mat(bb):
        bb = bb.reshape(DIRS, 2, S5_HALF_G, S5_N, S5_CH)
        return jnp.einsum('dhgnc,gk->dhgckn', bb, eye).reshape(DIRS, 2, S5_HALF_U, S5_HALF_X)

    def out_mat(cc):
        cc = cc.reshape(DIRS, 2, S5_HALF_G, S5_CH, S5_N)
        return jnp.einsum('dhgcn,gk->dhgnkc', cc, eye).reshape(DIRS, 2, S5_HALF_X, S5_HALF_U)

    bm = jnp.concatenate([in_mat(bb_re), in_mat(bb_im)], axis=-1).astype(BF16)
    cm = jnp.concatenate([out_mat(c_re), -out_mat(c_im)], axis=-2).astype(BF16)
    t = jnp.arange(S5_BLK, dtype=F32)
    order = jnp.stack([t, S5_BLK - 1.0 - t])
    shifts = 2.0 ** jnp.arange(S5_TABS - 1, dtype=F32)
    expo = jnp.concatenate([jnp.where(order[:, None, :] >= shifts[None, :, None], shifts[None, :, None], jnp.nan),
                            order[:, None, :] + 1.0], axis=1)
    live = ~jnp.isnan(expo)
    e = jnp.where(live, expo, 0.0)[..., None]
    adt_re = (a_re * dt).reshape(DIRS, 1, 1, S5_X)
    adt_im = (a_im * dt).reshape(DIRS, 1, 1, S5_X)
    pmag = jnp.where(live[..., None], jnp.exp(e * adt_re), 0.0)
    pw_re = pmag * jnp.cos(e * adt_im)
    pw_im = pmag * jnp.sin(e * adt_im)
    return bm, cm, pw_re, pw_im


def _gelu_tanh(x):
    return 0.5 * x * (1.0 + jnp.tanh(math.sqrt(2.0 / math.pi) * (x + 0.044715 * (x * x * x))))


def _s5_body(*refs, seq_len, has_s0):
    it = iter(refs)
    u_ref, bm_ref, cm_ref, pwr_ref, pwi_ref, d_ref, gw_ref, gb_ref = (next(it) for _ in range(8))
    x0r_ref, x0i_ref = (next(it), next(it)) if has_s0 else (None, None)
    o_ref, sfr_ref, sfi_ref, y_ref, xr_ref, xi_ref, xb_ref = (next(it) for _ in range(7))
    tc = min(seq_len, S5_TC)
    n_tiles = seq_len // tc
    n_pair = tc // (2 * S5_BLK)

    y_ref[...] = d_ref[...] * u_ref[...]
    unit = 0
    for d in range(DIRS):
        last = S5_BLK - 1 if d == 0 else 0
        for hf in range(2):
            xs = slice(hf * S5_HALF_X, (hf + 1) * S5_HALF_X)
            us = slice(hf * S5_HALF_U, (hf + 1) * S5_HALF_U)

            def scan_block(x_in, carry, d=d, xs=xs, last=last):
                xr, xi = x_in
                car_re, car_im = carry
                for k in range(S5_TABS - 1):
                    s = 1 << k
                    shift = s if d == 0 else S5_BLK - s
                    p_re, p_im = pwr_ref[d, k, :, xs], pwi_ref[d, k, :, xs]
                    sr, si = pltpu.roll(xr, shift, 0), pltpu.roll(xi, shift, 0)
                    xr, xi = xr + p_re * sr - p_im * si, xi + p_re * si + p_im * sr
                p_re, p_im = pwr_ref[d, S5_TABS - 1, :, xs], pwi_ref[d, S5_TABS - 1, :, xs]
                xr, xi = xr + p_re * car_re - p_im * car_im, xi + p_re * car_im + p_im * car_re
                return (xr, xi), (xr[last:last + 1, :], xi[last:last + 1, :])

            if has_s0:
                carry = (x0r_ref[d, :, xs], x0i_ref[d, :, xs])
            else:
                carry = (jnp.zeros((1, S5_HALF_X), F32), jnp.zeros((1, S5_HALF_X), F32))
            for i in range(n_tiles):
                slot = unit % 2
                unit += 1
                tile = i if d == 0 else n_tiles - 1 - i
                rows_t = slice(tile * tc, (tile + 1) * tc)
                bu = _dot(u_ref[rows_t, us], bm_ref[d, hf])
                xr_ref[slot] = bu[:, :S5_HALF_X]
                xi_ref[slot] = bu[:, S5_HALF_X:]
                for j in range(n_pair):
                    pair = j if d == 0 else n_pair - 1 - j
                    rows = slice(pair * 2 * S5_BLK, (pair + 1) * 2 * S5_BLK)
                    xr2, xi2 = xr_ref[slot, rows, :], xi_ref[slot, rows, :]
                    halves = [(xr2[:S5_BLK], xi2[:S5_BLK]), (xr2[S5_BLK:], xi2[S5_BLK:])]
                    out = [None, None]
                    for idx in ((0, 1) if d == 0 else (1, 0)):
                        out[idx], carry = scan_block(halves[idx], carry)
                    xb_ref[slot, rows, :S5_HALF_X] = jnp.concatenate([out[0][0], out[1][0]], axis=0).astype(BF16)
                    xb_ref[slot, rows, S5_HALF_X:] = jnp.concatenate([out[0][1], out[1][1]], axis=0).astype(BF16)
                y_ref[rows_t, us] += jnp.dot(xb_ref[slot], cm_ref[d, hf], preferred_element_type=F32)
            sfr_ref[d, :, xs] = carry[0]
            sfi_ref[d, :, xs] = carry[1]

    z = _gelu_tanh(y_ref[...])
    o_ref[...] = (z * jax.nn.sigmoid(_dot(z, gw_ref[...]) + gb_ref[...])).astype(BF16)


def _s5_pallas(pm, tables, s5_d, glu_w, glu_b, n_seq, seq_len, row_off, x0, layer):
    bm, cm, pw_re, pw_im = tables
    has_s0 = x0 is not None
    full = lambda shape: pl.BlockSpec(shape, lambda b: (0,) * len(shape))
    of_layer = lambda t: pl.BlockSpec((None,) + t.shape[1:], lambda b: (layer,) + (0,) * (t.ndim - 1))
    in_specs = [_seq_spec(seq_len, GROUP_WIDTH, row_off, 13),
                of_layer(bm), of_layer(cm), of_layer(pw_re), of_layer(pw_im),
                full((1, GROUP_WIDTH)), full((GROUP_WIDTH, GROUP_WIDTH)), full((1, GROUP_WIDTH))]
    args = [pm, bm, cm, pw_re, pw_im, s5_d.reshape(1, GROUP_WIDTH), glu_w.astype(BF16),
            glu_b.reshape(1, GROUP_WIDTH)]
    if has_s0:
        in_specs += [pl.BlockSpec((None, None, DIRS, 1, S5_X), lambda b: (b, layer, 0, 0, 0))] * 2
        args += list(x0)
    st = jax.ShapeDtypeStruct((n_seq, DIRS, 1, S5_X), F32)
    st_spec = pl.BlockSpec((None, DIRS, 1, S5_X), lambda b: (b, 0, 0, 0))
    tc = min(seq_len, S5_TC)
    return pl.pallas_call(
        functools.partial(_s5_body, seq_len=seq_len, has_s0=has_s0),
        out_shape=(jax.ShapeDtypeStruct((n_seq * seq_len, GROUP_WIDTH), BF16), st, st),
        grid=(n_seq,),
        in_specs=in_specs,
        out_specs=(pl.BlockSpec((seq_len, GROUP_WIDTH), lambda b: (b, 0)), st_spec, st_spec),
        scratch_shapes=[pltpu.VMEM((seq_len, GROUP_WIDTH), F32),
                        pltpu.VMEM((2, tc, S5_HALF_X), F32), pltpu.VMEM((2, tc, S5_HALF_X), F32),
                        pltpu.VMEM((2, tc, 2 * S5_HALF_X), BF16)],
        compiler_params=_mix_params(1),
        name="s5",
    )(*args)


GLA_LEVELS = 7


def _chunk_cumsum_rows(x, rowi):
    s = 1
    while s < MIX_CHUNK:
        x = x + jnp.where(rowi >= s, pltpu.roll(x, s, 0), 0.0)
        s *= 2
    return x


def _gla_body(*refs, n_chunks, has_s0):
    it = iter(refs)
    q_ref, f0_ref, f1_ref, i_ref, g_ref, lb_ref, nw_ref = (next(it) for _ in range(7))
    s0_ref = next(it) if has_s0 else None
    o_ref, sf_ref, acc_ref, s_ref, code_ref = (next(it) for _ in range(5))
    C, H = MIX_CHUNK, N_HEADS
    f_refs = (f0_ref, f1_ref)

    rowi = lax.broadcasted_iota(jnp.int32, (C, HEAD_DIM), 0)
    ri = lax.broadcasted_iota(jnp.int32, (C, C), 0)
    ci = lax.broadcasted_iota(jnp.int32, (C, C), 1)
    top_bit = 31 - lax.clz(ri ^ ci)
    code_ref[...] = jnp.where(ri > ci, top_bit, jnp.where(ri < ci, -1 - top_bit, GLA_LEVELS))
    if has_s0:
        s_ref[...] = s0_ref[...]
    else:
        s_ref[...] = jnp.zeros_like(s_ref)
    acc_ref[...] = jnp.zeros_like(acc_ref)

    def chunk_step(n):
        for d in range(DIRS):
            c = n if d == 0 else n_chunks - 1 - n
            rows = slice(c * C, (c + 1) * C)
            for h in range(H):
                cols = slice(h * HEAD_DIM, (h + 1) * HEAD_DIM)
                code = code_ref[...]
                q = _silu(q_ref[rows, cols]) * HEAD_DIM ** -0.5
                v = i_ref[rows, cols].astype(BF16)
                fx = f_refs[d][rows, cols]
                lb = lb_ref[d:d + 1, cols]
                sig = 1.0 / (1.0 + jnp.exp(-fx))
                logf = jnp.log2(jnp.maximum(lb, LB_FLOOR) + (1.0 - lb) * sig)
                k = (1.0 - lb) * (1.0 - sig)
                cum = _chunk_cumsum_rows(logf, rowi)
                own = cum
                attn = jnp.where(code == GLA_LEVELS, _dot_nt(q, k), 0.0)
                for lvl in range(GLA_LEVELS):
                    m = 1 << lvl
                    prev = pltpu.roll(own, m, 0)
                    pre = jnp.minimum(cum - prev, 0.0)
                    suf = own - cum
                    if d == 0:
                        sc = _dot_nt(q * jnp.exp2(pre), k * jnp.exp2(suf))
                        hit = code == lvl
                    else:
                        sc = _dot_nt(q * jnp.exp2(suf + logf), k * jnp.exp2(jnp.minimum(pre - logf, 0.0)))
                        hit = code == -1 - lvl
                    attn = jnp.where(hit, sc, attn)
                    own = jnp.where(((rowi >> lvl) & 1) == 0, pltpu.roll(own, C - m, 0), own)
                tot = own
                st = s_ref[d, h]
                if d == 0:
                    q_in, k_out = q * jnp.exp2(cum), k * jnp.exp2(tot - cum)
                else:
                    q_in, k_out = q * jnp.exp2(tot - cum + logf), k * jnp.exp2(cum - logf)
                acc_ref[rows, cols] += _dot(attn, v) + _dot_nt(q_in, st)
                s_ref[d, h] = jnp.exp2(tot[0:1, :]) * st + _dot_tn(v, k_out)

    for n in range(n_chunks):
        chunk_step(n)
    sf_ref[...] = s_ref[...]

    for n in range(n_chunks):
        rows = slice(n * C, (n + 1) * C)
        for h in range(H):
            cols = slice(h * HEAD_DIM, (h + 1) * HEAD_DIM)
            o = acc_ref[rows, cols]
            y = o * lax.rsqrt(jnp.mean(o * o, axis=-1, keepdims=True) + EPS) * nw_ref[...]
            o_ref[rows, cols] = (y * _silu(g_ref[rows, cols])).astype(BF16)


def _gla_pallas(pm, lower_bound, norm_w, n_seq, seq_len, row_off, s0_t, layer):
    has_s0 = s0_t is not None
    st_shape = (DIRS, N_HEADS, HEAD_DIM, HEAD_DIM)
    in_specs = [_seq_spec(seq_len, GROUP_WIDTH, row_off, cb) for cb in (8, 9, 10, 11, 12)]
    in_specs += [pl.BlockSpec((DIRS, GROUP_WIDTH), lambda b: (0, 0)), pl.BlockSpec((1, HEAD_DIM), lambda b: (0, 0))]
    args = [pm] * 5 + [lower_bound, norm_w.reshape(1, HEAD_DIM)]
    if has_s0:
        in_specs.append(pl.BlockSpec((None, None) + st_shape, lambda b: (b, layer, 0, 0, 0, 0)))
        args.append(s0_t)
    return pl.pallas_call(
        functools.partial(_gla_body, n_chunks=seq_len // MIX_CHUNK, has_s0=has_s0),
        out_shape=(jax.ShapeDtypeStruct((n_seq * seq_len, GROUP_WIDTH), BF16),
                   jax.ShapeDtypeStruct((n_seq,) + st_shape, F32)),
        grid=(n_seq,),
        in_specs=in_specs,
        out_specs=(pl.BlockSpec((seq_len, GROUP_WIDTH), lambda b: (b, 0)),
                   pl.BlockSpec((None,) + st_shape, lambda b: (b, 0, 0, 0, 0))),
        scratch_shapes=[pltpu.VMEM((seq_len, GROUP_WIDTH), F32), pltpu.VMEM(st_shape, F32),
                        pltpu.VMEM((MIX_CHUNK, MIX_CHUNK), jnp.int32)],
        compiler_params=_mix_params(1),
        name="hgrn2",
    )(*args)


GDN_BLOCK_BITS = 4
GDN_MERGES = 3
N_GATES = DIRS * N_HEADS


def _softplus(x):
    return jnp.maximum(x, 0.0) + jnp.log1p(jnp.exp(-jnp.abs(x)))


def _gdn_body(*refs, seq_len, n_sub, has_s0):
    it = iter(refs)
    (q_ref, k_ref, v_ref, g_ref, ps_ref, pst_ref, cw_ref, prow_ref, pcol_ref, nw_ref) = (next(it) for _ in range(10))
    s0_ref = next(it) if has_s0 else None
    (o_ref, sf_ref, acc_ref, s_ref, qn_ref, kn_ref, vn_ref, gate_ref, gatet_ref, blev_ref) = (
        next(it) for _ in range(10))
    C, H, L = MIX_CHUNK, N_HEADS, seq_len
    R = n_sub * L
    n_chunks = L // C
    w = GROUP_WIDTH

    rowi = lax.broadcasted_iota(jnp.int32, (C, HEAD_DIM), 0)
    lane = lax.broadcasted_iota(jnp.int32, (C, HEAD_DIM), 1)
    lane_t = lax.broadcasted_iota(jnp.int32, (2 * N_GATES, C), 1)
    row_t = lax.broadcasted_iota(jnp.int32, (2 * N_GATES, C), 0)

    def prepare(c):
        rows = slice(c * C, (c + 1) * C)
        first, final = c % n_chunks == 0, c % n_chunks == n_chunks - 1
        for part, (src, dst) in enumerate(((q_ref, qn_ref), (k_ref, kn_ref), (v_ref, vn_ref))):
            for h in range(H):
                cols = slice(h * HEAD_DIM, (h + 1) * HEAD_DIM)
                wc = slice(part * w + h * HEAD_DIM, part * w + (h + 1) * HEAD_DIM)
                x = src[rows, cols]
                before = jnp.zeros((1, HEAD_DIM), F32) if first else src[c * C - 1:c * C, cols]
                after = jnp.zeros((1, HEAD_DIM), F32) if final else src[(c + 1) * C:(c + 1) * C + 1, cols]
                x_prev = jnp.where(rowi == 0, before, pltpu.roll(x, 1, 0))
                x_next = jnp.where(rowi == C - 1, after, pltpu.roll(x, C - 1, 0))
                y = _silu(cw_ref[0:1, wc] * x_prev + cw_ref[1:2, wc] * x + cw_ref[2:3, wc] * x_next)
                if part < 2:
                    y = y * lax.rsqrt(jnp.sum(y * y, axis=-1, keepdims=True) + EPS)
                if part == 0:
                    y = y * HEAD_DIM ** -0.5
                dst[rows, cols] = y

        a = ps_ref[rows, :]
        la = -jnp.exp(prow_ref[0:1, :]) * _softplus(a + prow_ref[1:2, :])
        pre = _chunk_cumsum_rows(la, rowi)
        tot = jnp.broadcast_to(pre[C - 1:C, :], (C, HEAD_DIM))
        g = jnp.where(lane < N_HEADS, pre, tot - pre + la)
        gate_ref[rows, :] = jnp.where(lane < N_GATES, g,
                                      jnp.where(lane < 2 * N_GATES, jax.nn.sigmoid(a),
                                                pltpu.roll(tot, 2 * N_GATES, 1)))
        at = pst_ref[:, rows]
        lat = -jnp.exp(pcol_ref[:, 0:1]) * _softplus(at + pcol_ref[:, 1:2])
        pre_t, s = lat, 1
        while s < C:
            pre_t = pre_t + jnp.where(lane_t >= s, pltpu.roll(pre_t, s, 1), 0.0)
            s *= 2
        tot_t = jnp.broadcast_to(pre_t[:, C - 1:C], (2 * N_GATES, C))
        g_t = jnp.where(row_t < N_HEADS, pre_t, tot_t - pre_t + lat)
        gatet_ref[:, rows] = jnp.where(row_t < N_GATES, g_t, jax.nn.sigmoid(at))

    ri = lax.broadcasted_iota(jnp.int32, (C, C), 0)
    ci = lax.broadcasted_iota(jnp.int32, (C, C), 1)
    bx = (ri >> GDN_BLOCK_BITS) ^ (ci >> GDN_BLOCK_BITS)
    blev_ref[...] = jnp.where(bx == 0, 0, 32 - lax.clz(bx))
    if has_s0:
        for s in range(n_sub):
            s_ref[s * DIRS:(s + 1) * DIRS] = s0_ref[s]
    else:
        s_ref[...] = jnp.zeros_like(s_ref)
    acc_ref[...] = jnp.zeros_like(acc_ref)

    units = [(s, d, h) for s in range(n_sub) for d in range(DIRS) for h in range(H)]

    def rows_of(n, s, d):
        c = n if d == 0 else n_chunks - 1 - n
        return slice(s * L + c * C, s * L + (c + 1) * C)

    def state_free_phases(step_units):
        eye = jnp.where(ri == ci, 1.0, 0.0)
        blev = blev_ref[...]
        incl_d = [ri >= ci, ri <= ci]
        strict_d = [ri > ci, ri < ci]
        q_l, et_l, ek_l, rhs_l, a_l, attn_l = ([] for _ in range(6))
        for n, s, d, h in step_units:
            rows, cols, cg = rows_of(n, s, d), slice(h * HEAD_DIM, (h + 1) * HEAD_DIM), d * N_HEADS + h
            q, k, v = qn_ref[rows, cols], kn_ref[rows, cols], vn_ref[rows, cols]
            g_i = jnp.broadcast_to(gate_ref[rows, cg:cg + 1], (C, C))
            b_i = jnp.broadcast_to(gate_ref[rows, N_GATES + cg:N_GATES + cg + 1], (C, C))
            tot = jnp.broadcast_to(gate_ref[rows, 2 * N_GATES + cg:2 * N_GATES + cg + 1], (C, C))
            g_j = gatet_ref[cg:cg + 1, rows]
            decay = jnp.where(incl_d[d], jnp.exp(jnp.minimum(g_i - g_j, 0.0)), 0.0)
            e_g = jnp.exp(g_i)
            kb = k.astype(BF16)
            a_l.append(jnp.where(strict_d[d], _dot_nt(kb, kb) * b_i * decay, 0.0))
            attn_l.append(_dot_nt(q, kb) * decay)
            rhs_l.append(jnp.concatenate([v * b_i, k * (b_i * e_g)], axis=1).astype(BF16))
            q_l.append(q * e_g)
            ek_l.append(k * jnp.exp(tot - g_i))
            et_l.append(jnp.exp(tot))
        b_l = [-jnp.where(blev == 0, a, 0.0) for a in a_l]
        p_l = [eye + b for b in b_l]
        for _ in range(GDN_BLOCK_BITS - 1):
            b_l = [_dot(b, b) for b in b_l]
            p_l = [p + _dot(p, b) for p, b in zip(p_l, b_l)]
        for lvl in range(1, GDN_MERGES + 1):
            ep_l = [_dot(jnp.where(blev == lvl, a, 0.0), p) for a, p in zip(a_l, p_l)]
            p_l = [p - _dot(p, ep) for p, ep in zip(p_l, ep_l)]
        uw_l = [_dot(p, rhs) for p, rhs in zip(p_l, rhs_l)]
        lhs_l = [jnp.concatenate([q, attn], axis=1).astype(BF16) for q, attn in zip(q_l, attn_l)]
        return list(zip(uw_l, lhs_l, ek_l, et_l))

    def state_phase(n, vals):
        s_l = [s_ref[s * DIRS + d, h] for s, d, h in units]
        vn_l = [uw[:, :HEAD_DIM] - _dot(uw[:, HEAD_DIM:], st) for (uw, _, _, _), st in zip(vals, s_l)]
        for i, (s, d, h) in enumerate(units):
            cols = slice(h * HEAD_DIM, (h + 1) * HEAD_DIM)
            _, lhs, ek, et = vals[i]
            acc_ref[rows_of(n, s, d), cols] += _dot(lhs, jnp.concatenate([s_l[i], vn_l[i]], axis=0))
            s_ref[s * DIRS + d, h] = s_l[i] * et + _dot_tn(ek, vn_l[i])

    for c in range(R // C):
        prepare(c)
    group = max(1, GDN_GROUP_UNITS // len(units))
    for n0 in range(0, n_chunks, group):
        steps = range(n0, min(n0 + group, n_chunks))
        vals = state_free_phases([(n,) + u for n in steps for u in units])
        for j, n in enumerate(steps):
            state_phase(n, vals[j * len(units):(j + 1) * len(units)])
    sf_ref[...] = s_ref[...]

    for n in range(R // C):
        rows = slice(n * C, (n + 1) * C)
        for h in range(H):
            cols = slice(h * HEAD_DIM, (h + 1) * HEAD_DIM)
            o = acc_ref[rows, cols]
            y = o * lax.rsqrt(jnp.mean(o * o, axis=-1, keepdims=True) + EPS) * nw_ref[...]
            o_ref[rows, cols] = (y * _silu(g_ref[rows, cols])).astype(BF16)


GDN_GROUP_UNITS = 32
GDN_CTX_SUB = 2
GDN_LAT_SUB = 1
GDN_DOUBLE_BUFFER_BYTES = 2 * 1024 * 1024


def _gdn_pallas(pm, ps, ps_t, conv_w, a_log, dt_bias, norm_w, n_seq, seq_len, row_off, s0, layer):
    has_s0 = s0 is not None
    n_sub = GDN_LAT_SUB if has_s0 else GDN_CTX_SUB
    n_seq, seq_len, row_off = n_seq // n_sub, seq_len * n_sub, row_off // n_sub
    st_shape = (n_sub * DIRS, N_HEADS, HEAD_DIM, HEAD_DIM)
    par = jnp.stack([a_log.reshape(N_GATES), dt_bias.reshape(N_GATES)])
    par_row = jnp.pad(par, ((0, 0), (0, HEAD_DIM - N_GATES)))
    par_col = jnp.pad(par.T, ((0, N_GATES), (0, 0)))
    full = lambda shape: pl.BlockSpec(shape, lambda b: (0,) * len(shape))
    big_mode = pl.Buffered(1) if seq_len * GROUP_WIDTH * 4 > GDN_DOUBLE_BUFFER_BYTES else None
    in_specs = [_seq_spec(seq_len, GROUP_WIDTH, row_off, cb, big_mode) for cb in (4, 5, 6, 7)]
    in_specs += [_seq_spec(seq_len, PROJ_SMALL, row_off, 0),
                 pl.BlockSpec((2 * N_GATES, seq_len), lambda b: (0, row_off + b)),
                 full((GDN_CONV, 3 * GROUP_WIDTH)), full((2, HEAD_DIM)), full((2 * N_GATES, 2)), full((1, HEAD_DIM))]
    args = [pm] * 4 + [ps, ps_t, conv_w, par_row, par_col, norm_w.reshape(1, HEAD_DIM)]
    if has_s0:
        in_specs.append(pl.BlockSpec((n_sub, None, DIRS, N_HEADS, HEAD_DIM, HEAD_DIM),
                                     lambda b: (b, layer, 0, 0, 0, 0)))
        args.append(s0)
    seq_f32 = lambda width: pltpu.VMEM((seq_len, width), F32)
    out, s_fin = pl.pallas_call(
        functools.partial(_gdn_body, seq_len=seq_len // n_sub, n_sub=n_sub, has_s0=has_s0),
        out_shape=(jax.ShapeDtypeStruct((n_seq * seq_len, GROUP_WIDTH), BF16),
                   jax.ShapeDtypeStruct((n_seq,) + st_shape, F32)),
        grid=(n_seq,),
        in_specs=in_specs,
        out_specs=(pl.BlockSpec((seq_len, GROUP_WIDTH), lambda b: (b, 0)),
                   pl.BlockSpec((None,) + st_shape, lambda b: (b, 0, 0, 0, 0))),
        scratch_shapes=[seq_f32(GROUP_WIDTH), pltpu.VMEM(st_shape, F32),
                        seq_f32(GROUP_WIDTH), seq_f32(GROUP_WIDTH), seq_f32(GROUP_WIDTH),
                        seq_f32(HEAD_DIM), pltpu.VMEM((2 * N_GATES, seq_len), F32),
                        pltpu.VMEM((MIX_CHUNK, MIX_CHUNK), jnp.int32)],
        compiler_params=_mix_params(1),
        name="gated_delta",
    )(*args)
    return out, s_fin.reshape(n_seq * n_sub, DIRS, N_HEADS, HEAD_DIM, HEAD_DIM)


def kernel(x_prompt, x_sample, state_ret, state_gdn, state_hgrn, state_s5_re, state_s5_im, c, c_ctx, norm1_w, norm2_w, final_norm_w, ada_w, ada_b, in_proj, out_proj, ret_decay_logit, gdn_conv, gdn_a_log, gdn_dt_bias, gdn_norm_w, hg_lb_param, hg_norm_w, s5_a_re, s5_a_im, s5_b_re, s5_b_im, s5_c_re, s5_c_im, s5_log_step, s5_d, s5_glu_w, s5_glu_b, ffn_w1, ffn_w3, ffn_w2):
    lb_soft = jax.nn.softmax(hg_lb_param, axis=0)
    lower_bounds = jnp.cumsum(lb_soft, axis=0) - lb_soft[0]
    rope2 = _rope_tables(DEC_SEQ)

    cvec = jnp.zeros((N_SEQ_ROWS, D_MODEL), F32).at[0].set(c_ctx).at[1:1 + DEC_BATCH].set(c)
    mod_all = _ada(cvec, ada_w, ada_b).reshape(DEPTH, N_SEQ_ROWS, N_MOD, 1, D_MODEL)

    gate0 = 8 * GROUP_WIDTH
    w_in = in_proj.astype(BF16)
    w_main = jnp.concatenate([w_in[:, :, :gate0], w_in[:, :, gate0 + 2 * N_GATES:]], axis=-1)
    w_small = jnp.pad(w_in[:, :, gate0:gate0 + 2 * N_GATES], ((0, 0), (0, 0), (0, PROJ_SMALL - 2 * N_GATES)))
    w_out = out_proj.astype(BF16)
    w1, w3, w2 = ffn_w1.astype(BF16), ffn_w3.astype(BF16), ffn_w2.astype(BF16)

    x = jnp.concatenate([x_prompt.reshape(N_CTX_TOK, D_MODEL), x_sample.reshape(N_LAT_TOK, D_MODEL)], axis=0)
    hgrn_t = jnp.swapaxes(state_hgrn, -1, -2)
    s5_x0 = (state_s5_re.reshape(DEC_BATCH, DEPTH, DIRS, 1, S5_X), state_s5_im.reshape(DEC_BATCH, DEPTH, DIRS, 1, S5_X))
    lat_off = N_CTX_TOK // DEC_SEQ
    tables = jax.vmap(_s5_tables)(s5_a_re, s5_a_im, s5_log_step, s5_b_re, s5_b_im, s5_c_re, s5_c_im)
    ctx_states = []
    for i in range(DEPTH):
        mod = mod_all[i]
        pm, ps = _inproj(x, norm1_w[i][None], mod, w_main, w_small, i)
        ps_t = ps[:, :2 * N_GATES].T
        lg = jax.nn.log_sigmoid(ret_decay_logit[i])
        ctx = (BATCH, SEQ, 0)
        lat = (DEC_BATCH, DEC_SEQ, lat_off)

        ret_c, rs = _retention_pallas(pm, lg, *ctx, None, None, i)
        ret_l, _ = _retention_pallas(pm, lg, *lat, rope2, state_ret, i)
        gdn_args = (pm, ps, ps_t, gdn_conv[i], gdn_a_log[i], gdn_dt_bias[i], gdn_norm_w[i])
        gdn_c, gs = _gdn_pallas(*gdn_args, *ctx, None, i)
        gdn_l, _ = _gdn_pallas(*gdn_args, *lat, state_gdn, i)
        hg_c, hs = _gla_pallas(pm, lower_bounds[i], hg_norm_w[i], *ctx, None, i)
        hg_l, _ = _gla_pallas(pm, lower_bounds[i], hg_norm_w[i], *lat, hgrn_t, i)
        s5_args = (pm, tables, s5_d[i], s5_glu_w[i], s5_glu_b[i])
        s5_c, xr, xi = _s5_pallas(*s5_args, *ctx, None, i)
        s5_l, _, _ = _s5_pallas(*s5_args, *lat, s5_x0, i)
        ctx_states.append((rs, gs, jnp.swapaxes(hs, -1, -2), xr.reshape(BATCH, DIRS, S5_GROUPS, S5_N),
                           xi.reshape(BATCH, DIRS, S5_GROUPS, S5_N)))

        parts = ((ret_c, ret_l), (gdn_c, gdn_l), (hg_c, hg_l), (s5_c, s5_l))
        x = _outproj(parts, w_out, x, mod, i)
        x = _ffn(x, norm2_w[i][None], mod, w1, w3, w2, final_norm_w[None], i)

    y_prompt = x[0].reshape(BATCH, SEQ, D_MODEL)
    y_sample = x[1].reshape(DEC_BATCH, DEC_SEQ, D_MODEL)
    new_states = tuple(jnp.stack([s[j] for s in ctx_states], axis=1) for j in range(5))
    return (y_prompt, y_sample) + new_states
```

```python
import functools
import math

import jax
import jax.numpy as jnp
from jax import lax
from jax.experimental import pallas as pl
from jax.experimental.pallas import tpu as pltpu

F32 = jnp.float32
BF16 = jnp.bfloat16

D_MODEL = 2048
BATCH = 16
SEQ = 256
DEPTH = 2
DEC_BATCH = 8
DEC_SEQ = 1024
GRID_W = 64
HEAD_DIM = 128
GROUP_WIDTH = 512
N_HEADS = 4
S5_CH = 16
S5_GROUPS = 32
S5_N = 64
GDN_CONV = 3
ROPE_BASE = 10000.0
FFN_HIDDEN = 5632
N_MOD = 6
EPS = 1e-6
LB_FLOOR = 1e-30

N_CTX_TOK = BATCH * SEQ
N_LAT_TOK = DEC_BATCH * DEC_SEQ
N_TOK = N_CTX_TOK + N_LAT_TOK
N_SEQ_ROWS = 16
PROJ_MAIN = 14 * GROUP_WIDTH
PROJ_SMALL = 128
V7X_VMEM_BYTES = 64 * 1024 * 1024
VMEM_LIMIT = V7X_VMEM_BYTES - 8 * 1024 * 1024
ADA_TN = 1024
OUT_TN = 1024


def _seq_row(tile, tm):
    n_ctx = N_CTX_TOK // tm
    per_lat = DEC_SEQ // tm
    return jnp.where(tile < n_ctx, 0, 1 + (tile - n_ctx) // per_lat)


def _stream_specs(tm, width, n_col=1):
    n_ctx = N_CTX_TOK // tm

    def ctx_map(i, j):
        return jnp.minimum(i, n_ctx - 1), (jnp.where(i < n_ctx, j, n_col - 1) if n_col > 1 else 0)

    def lat_map(i, j):
        return jnp.maximum(i - n_ctx, 0), (jnp.where(i >= n_ctx, j, 0) if n_col > 1 else 0)

    return pl.BlockSpec((tm, width), ctx_map), pl.BlockSpec((tm, width), lat_map)


def _on_stream(tile, tm, fn):
    n_ctx = N_CTX_TOK // tm
    pl.when(tile < n_ctx)(functools.partial(fn, 0))
    pl.when(tile >= n_ctx)(functools.partial(fn, 1))


def _ada_body(c_ref, w_ref, b_ref, o_ref):
    cv = c_ref[...]
    s = cv * jax.nn.sigmoid(cv)
    o_ref[0] = jnp.dot(s.astype(BF16), w_ref[0].astype(BF16), preferred_element_type=F32) + b_ref[0]


def _ada(cvec, ada_w, ada_b):
    tn = ADA_TN
    n = N_MOD * D_MODEL
    return pl.pallas_call(
        _ada_body,
        out_shape=jax.ShapeDtypeStruct((DEPTH, N_SEQ_ROWS, n), F32),
        grid=(DEPTH, n // tn),
        in_specs=[
            pl.BlockSpec((N_SEQ_ROWS, D_MODEL), lambda l, j: (0, 0)),
            pl.BlockSpec((1, D_MODEL, tn), lambda l, j: (l, 0, j)),
            pl.BlockSpec((1, 1, tn), lambda l, j: (l, 0, j)),
        ],
        out_specs=pl.BlockSpec((1, N_SEQ_ROWS, tn), lambda l, j: (l, 0, j)),
        compiler_params=pltpu.CompilerParams(
            dimension_semantics=("parallel", "parallel"), vmem_limit_bytes=VMEM_LIMIT),
        name="ada_mod",
    )(cvec, ada_w, ada_b.reshape(DEPTH, 1, n))


def _norm_mod(x, nw, sc, sh):
    ms = jnp.mean(x * x, axis=-1, keepdims=True)
    y = x * lax.rsqrt(ms + EPS) * nw
    return y * (1.0 + sc) + sh


PROJ_TM = 1024
PROJ_TN = 1024


def _inproj_body(x_ref, nw_ref, sc_ref, sh_ref, w_ref, ws_ref, o_ref, os_ref, h_ref):
    @pl.when(pl.program_id(1) == 0)
    def _():
        hb = _norm_mod(x_ref[...], nw_ref[...], sc_ref[...], sh_ref[...]).astype(BF16)
        h_ref[...] = hb
        os_ref[...] = jnp.dot(hb, ws_ref[...], preferred_element_type=F32)

    o_ref[...] = jnp.dot(h_ref[...], w_ref[...], preferred_element_type=F32)


def _inproj(x, nw, mod, w_main, w_small, layer):
    tm, tn = PROJ_TM, PROJ_TN
    return pl.pallas_call(
        _inproj_body,
        out_shape=(jax.ShapeDtypeStruct((N_TOK, PROJ_MAIN), F32),
                   jax.ShapeDtypeStruct((N_TOK, PROJ_SMALL), F32)),
        grid=(N_TOK // tm, PROJ_MAIN // tn),
        in_specs=[
            pl.BlockSpec((tm, D_MODEL), lambda i, j: (i, 0)),
            pl.BlockSpec((1, D_MODEL), lambda i, j: (0, 0)),
            pl.BlockSpec((None, None, 1, D_MODEL), lambda i, j: (_seq_row(i, tm), 1, 0, 0)),
            pl.BlockSpec((None, None, 1, D_MODEL), lambda i, j: (_seq_row(i, tm), 0, 0, 0)),
            pl.BlockSpec((None, D_MODEL, tn), lambda i, j: (layer, 0, j)),
            pl.BlockSpec((None, D_MODEL, PROJ_SMALL), lambda i, j: (layer, 0, 0)),
        ],
        out_specs=(pl.BlockSpec((tm, tn), lambda i, j: (i, j)),
                   pl.BlockSpec((tm, PROJ_SMALL), lambda i, j: (i, 0))),
        scratch_shapes=[pltpu.VMEM((tm, D_MODEL), BF16)],
        compiler_params=pltpu.CompilerParams(
            dimension_semantics=("parallel", "arbitrary"), vmem_limit_bytes=VMEM_LIMIT),
        name="in_proj",
    )(x, nw, mod, mod, w_main, w_small)


def _outproj_body(*refs):
    n_mix = 4
    m_refs, (w_ref, x_ref, g_ref, o_ref) = refs[:2 * n_mix], refs[2 * n_mix:]

    def compute(stream):
        acc = None
        for p in range(n_mix):
            part = jnp.dot(m_refs[2 * p + stream][...], w_ref[p * GROUP_WIDTH:(p + 1) * GROUP_WIDTH, :],
                           preferred_element_type=F32)
            acc = part if acc is None else acc + part
        o_ref[...] = x_ref[...] + g_ref[...] * acc

    _on_stream(pl.program_id(0), PROJ_TM, compute)


def _outproj(parts, w, x, mod, layer):
    tm, tn = PROJ_TM, OUT_TN
    in_specs = []
    for _ in parts:
        in_specs += list(_stream_specs(tm, GROUP_WIDTH))
    in_specs += [pl.BlockSpec((None, D_MODEL, tn), lambda i, j: (layer, 0, j)),
                 pl.BlockSpec((tm, tn), lambda i, j: (i, j)),
                 pl.BlockSpec((None, None, 1, tn), lambda i, j: (_seq_row(i, tm), 2, 0, j))]
    return pl.pallas_call(
        _outproj_body,
        out_shape=jax.ShapeDtypeStruct((N_TOK, D_MODEL), F32),
        grid=(N_TOK // tm, D_MODEL // tn),
        in_specs=in_specs,
        out_specs=pl.BlockSpec((tm, tn), lambda i, j: (i, j)),
        compiler_params=pltpu.CompilerParams(
            dimension_semantics=("parallel", "arbitrary"), vmem_limit_bytes=VMEM_LIMIT),
        name="out_proj",
    )(*(a for pair in parts for a in pair), w, x, mod)


FFN_TM = 512
FFN_TH = 512
FFN_COL_SPLIT = 4


def _ffn_body(x_ref, nw_ref, sc_ref, sh_ref, g_ref, w1_ref, w3_ref, w2_ref, fw_ref, *rest, final_norm):
    o_refs, (h_ref, acc_ref) = rest[:-2], rest[-2:]
    i, k = pl.program_id(0), pl.program_id(1)

    @pl.when(k == 0)
    def _():
        h_ref[...] = _norm_mod(x_ref[...], nw_ref[...], sc_ref[...], sh_ref[...]).astype(BF16)
        acc_ref[...] = jnp.zeros_like(acc_ref)

    h = h_ref[...]
    a = jnp.dot(h, w1_ref[...], preferred_element_type=F32)
    b = jnp.dot(h, w3_ref[...], preferred_element_type=F32)
    g = (a * jax.nn.sigmoid(a) * b).astype(BF16)
    cs = D_MODEL // FFN_COL_SPLIT
    for c in range(FFN_COL_SPLIT):
        cols = slice(c * cs, (c + 1) * cs)
        acc_ref[:, cols] += jnp.dot(g, w2_ref[:, cols], preferred_element_type=F32)

    @pl.when(k == pl.num_programs(1) - 1)
    def _():
        y = x_ref[...] + g_ref[...] * acc_ref[...]
        if not final_norm:
            o_refs[0][...] = y
        else:
            ms = jnp.mean(y * y, axis=-1, keepdims=True)
            y = y * lax.rsqrt(ms + EPS) * fw_ref[...]

            def write(stream):
                o_refs[stream][...] = y

            _on_stream(i, FFN_TM, write)


def _ffn(x, nw, mod, w1, w3, w2, fw, layer):
    final_norm = layer == DEPTH - 1
    tm, th = FFN_TM, FFN_TH
    if final_norm:
        out_shape = (jax.ShapeDtypeStruct((N_CTX_TOK, D_MODEL), F32), jax.ShapeDtypeStruct((N_LAT_TOK, D_MODEL), F32))
        out_specs = _stream_specs(tm, D_MODEL)
    else:
        out_shape = jax.ShapeDtypeStruct((N_TOK, D_MODEL), F32)
        out_specs = pl.BlockSpec((tm, D_MODEL), lambda i, k: (i, 0))
    return pl.pallas_call(
        functools.partial(_ffn_body, final_norm=final_norm),
        out_shape=out_shape,
        grid=(N_TOK // tm, FFN_HIDDEN // th),
        in_specs=[
            pl.BlockSpec((tm, D_MODEL), lambda i, k: (i, 0)),
            pl.BlockSpec((1, D_MODEL), lambda i, k: (0, 0)),
            pl.BlockSpec((None, None, 1, D_MODEL), lambda i, k: (_seq_row(i, tm), 4, 0, 0)),
            pl.BlockSpec((None, None, 1, D_MODEL), lambda i, k: (_seq_row(i, tm), 3, 0, 0)),
            pl.BlockSpec((None, None, 1, D_MODEL), lambda i, k: (_seq_row(i, tm), 5, 0, 0)),
            pl.BlockSpec((None, D_MODEL, th), lambda i, k: (layer, 0, k)),
            pl.BlockSpec((None, D_MODEL, th), lambda i, k: (layer, 0, k)),
            pl.BlockSpec((None, th, D_MODEL), lambda i, k: (layer, k, 0)),
            pl.BlockSpec((1, D_MODEL), lambda i, k: (0, 0)),
        ],
        out_specs=out_specs,
        scratch_shapes=[pltpu.VMEM((tm, D_MODEL), BF16), pltpu.VMEM((tm, D_MODEL), F32)],
        compiler_params=pltpu.CompilerParams(
            dimension_semantics=("arbitrary", "arbitrary"), vmem_limit_bytes=VMEM_LIMIT),
        name="ffn",
    )(x, nw, mod, mod, mod, w1, w3, w2, fw)


def _axial_rope(l):
    n_rows = l // GRID_W
    t_row = jnp.repeat(jnp.arange(n_rows, dtype=F32), GRID_W)
    t_col = jnp.tile(jnp.arange(GRID_W, dtype=F32), n_rows)
    n_freq = HEAD_DIM // 4
    inv = ROPE_BASE ** (-jnp.arange(n_freq, dtype=F32) / n_freq)
    ang = jnp.concatenate([t_row[:, None] * inv, t_col[:, None] * inv], axis=-1)
    return jnp.cos(ang), jnp.sin(ang)


MIX_CHUNK = 128
DIRS = 2
NT_DIMS = (((1,), (1,)), ((), ()))
TN_DIMS = (((0,), (0,)), ((), ()))


def _dot(a, b):
    return jnp.dot(a.astype(BF16), b.astype(BF16), preferred_element_type=F32)


def _dot_nt(a, b):
    return lax.dot_general(a.astype(BF16), b.astype(BF16), NT_DIMS, preferred_element_type=F32)


def _dot_tn(a, b):
    return lax.dot_general(a.astype(BF16), b.astype(BF16), TN_DIMS, preferred_element_type=F32)


def _silu(x):
    return x * jax.nn.sigmoid(x)


def _seq_spec(seq_len, width, row_off, col_blk, pipeline_mode=None):
    return pl.BlockSpec((seq_len, width), lambda b: (row_off + b, col_blk), pipeline_mode=pipeline_mode)


def _mix_params(n_par):
    return pltpu.CompilerParams(dimension_semantics=("parallel",) * n_par, vmem_limit_bytes=VMEM_LIMIT)


def _ret_body(*refs, n_chunks, use_rope, has_s0):
    it = iter(refs)
    lg_ref, q_ref, k_ref, v_ref, g_ref = (next(it) for _ in range(5))
    cos_ref, sin_ref = (next(it), next(it)) if use_rope else (None, None)
    s0_ref = next(it) if has_s0 else None
    o_ref, sf_ref, acc_ref, s_ref, intra_ref, qd_ref, kd_ref, cd_ref = (next(it) for _ in range(8))
    C, H = MIX_CHUNK, N_HEADS

    row = lax.broadcasted_iota(jnp.int32, (C, C), 0)
    col = lax.broadcasted_iota(jnp.int32, (C, C), 1)
    rel = (row - col).astype(F32)
    pos = lax.broadcasted_iota(jnp.int32, (C, HEAD_DIM), 0).astype(F32)
    for d in range(DIRS):
        for h in range(H):
            lg = lg_ref[d, h]
            if d == 0:
                intra_ref[d, h] = jnp.where(rel >= 0, jnp.exp(jnp.maximum(rel, 0.0) * lg), 0.0)
                qd_ref[d, h] = jnp.exp((pos + 1.0) * lg)
                kd_ref[d, h] = jnp.exp((C - 1.0 - pos) * lg)
            else:
                intra_ref[d, h] = jnp.where(rel <= 0, jnp.exp(jnp.maximum(-rel, 0.0) * lg), 0.0)
                qd_ref[d, h] = jnp.exp((C - pos) * lg)
                kd_ref[d, h] = jnp.exp(pos * lg)
            cd_ref[d, h] = jnp.exp(jnp.full((C, HEAD_DIM), C, F32) * lg)
    if has_s0:
        s_ref[...] = s0_ref[...]
    else:
        s_ref[...] = jnp.zeros_like(s_ref)
    acc_ref[...] = jnp.zeros_like(acc_ref)

    units = [(d, h) for d in range(DIRS) for h in range(H)]

    def chunk_step(n):
        rows_d = [slice(c * C, (c + 1) * C) for c in (n, n_chunks - 1 - n)]
        q_l, k_l, v_l, sc_l = [], [], [], []
        for d, h in units:
            rows, cols = rows_d[d], slice(h * HEAD_DIM, (h + 1) * HEAD_DIM)
            q, k = q_ref[rows, cols], k_ref[rows, cols]
            if use_rope:
                cs, sn = cos_ref[rows, :], sin_ref[rows, :]
                q = q * cs + pltpu.roll(q, HEAD_DIM // 2, 1) * sn
                k = k * cs + pltpu.roll(k, HEAD_DIM // 2, 1) * sn
            k = k * HEAD_DIM ** -0.5
            q_l.append(q)
            k_l.append(k)
            v_l.append(v_ref[rows, cols].astype(BF16))
            sc_l.append(_dot_nt(q, k) * intra_ref[d, h])
        for i, (d, h) in enumerate(units):
            cols = slice(h * HEAD_DIM, (h + 1) * HEAD_DIM)
            s = s_ref[d, h]
            acc_ref[rows_d[d], cols] += _dot(jnp.concatenate([sc_l[i], q_l[i] * qd_ref[d, h]], axis=1),
                                             jnp.concatenate([v_l[i], s.astype(BF16)], axis=0))
            s_ref[d, h] = cd_ref[d, h] * s + _dot_tn(k_l[i] * kd_ref[d, h], v_l[i])

    for n in range(n_chunks):
        chunk_step(n)
    sf_ref[...] = s_ref[...]

    for n in range(n_chunks):
        rows = slice(n * C, (n + 1) * C)
        for h in range(H):
            cols = slice(h * HEAD_DIM, (h + 1) * HEAD_DIM)
            o = acc_ref[rows, cols]
            mu = jnp.mean(o, axis=-1, keepdims=True)
            oc = o - mu
            y = oc * lax.rsqrt(jnp.mean(oc * oc, axis=-1, keepdims=True) + EPS)
            o_ref[rows, cols] = (y * _silu(g_ref[rows, cols])).astype(BF16)


def _retention_pallas(pm, log_gamma, n_seq, seq_len, row_off, rope2, s0, layer):
    use_rope, has_s0 = rope2 is not None, s0 is not None
    st_shape = (DIRS, N_HEADS, HEAD_DIM, HEAD_DIM)
    in_specs = [pl.BlockSpec(memory_space=pltpu.SMEM)]
    in_specs += [_seq_spec(seq_len, GROUP_WIDTH, row_off, cb) for cb in range(4)]
    args = [log_gamma, pm, pm, pm, pm]
    if use_rope:
        in_specs += [pl.BlockSpec((seq_len, HEAD_DIM), lambda b: (0, 0))] * 2
        args += list(rope2)
    if has_s0:
        in_specs.append(pl.BlockSpec((None, None) + st_shape, lambda b: (b, layer, 0, 0, 0, 0)))
        args.append(s0)
    return pl.pallas_call(
        functools.partial(_ret_body, n_chunks=seq_len // MIX_CHUNK, use_rope=use_rope, has_s0=has_s0),
        out_shape=(jax.ShapeDtypeStruct((n_seq * seq_len, GROUP_WIDTH), BF16),
                   jax.ShapeDtypeStruct((n_seq,) + st_shape, F32)),
        grid=(n_seq,),
        in_specs=in_specs,
        out_specs=(pl.BlockSpec((seq_len, GROUP_WIDTH), lambda b: (b, 0)),
                   pl.BlockSpec((None,) + st_shape, lambda b: (b, 0, 0, 0, 0))),
        scratch_shapes=[pltpu.VMEM((seq_len, GROUP_WIDTH), F32), pltpu.VMEM(st_shape, F32),
                        pltpu.VMEM((DIRS, N_HEADS, MIX_CHUNK, MIX_CHUNK), F32),
                        pltpu.VMEM((DIRS, N_HEADS, MIX_CHUNK, HEAD_DIM), F32),
                        pltpu.VMEM((DIRS, N_HEADS, MIX_CHUNK, HEAD_DIM), F32),
                        pltpu.VMEM((DIRS, N_HEADS, MIX_CHUNK, HEAD_DIM), F32)],
        compiler_params=_mix_params(1),
        name="retention",
    )(*args)


def _rope_tables(l):
    cos, sin = _axial_rope(l)
    return jnp.concatenate([cos, cos], axis=-1), jnp.concatenate([-sin, sin], axis=-1)


S5_HALF_G = S5_GROUPS // 2
S5_HALF_U = S5_HALF_G * S5_CH
S5_HALF_X = S5_HALF_G * S5_N
S5_X = S5_GROUPS * S5_N
S5_TC = 512
S5_BLK = 8
S5_TABS = 4


def _s5_tables(a_re, a_im, log_step, b_re, b_im, c_re, c_im):
    dt = jnp.exp(log_step)[..., None]
    mag = jnp.exp(a_re * dt)
    ab_re = mag * jnp.cos(a_im * dt)
    ab_im = mag * jnp.sin(a_im * dt)
    den = a_re * a_re + a_im * a_im
    nr = ab_re - 1.0
    f_re = (nr * a_re + ab_im * a_im) / den
    f_im = (ab_im * a_re - nr * a_im) / den
    bb_re = f_re[..., None] * b_re - f_im[..., None] * b_im
    bb_im = f_re[..., None] * b_im + f_im[..., None] * b_re
    eye = jnp.eye(S5_HALF_G, dtype=F32)

    def in_mat(bb):
        bb = bb.reshape(DIRS, 2, S5_HALF_G, S5_N, S5_CH)
        return jnp.einsum('dhgnc,gk->dhgckn', bb, eye).reshape(DIRS, 2, S5_HALF_U, S5_HALF_X)

    def out_mat(cc):
        cc = cc.reshape(DIRS, 2, S5_HALF_G, S5_CH, S5_N)
        return jnp.einsum('dhgcn,gk->dhgnkc', cc, eye).reshape(DIRS, 2, S5_HALF_X, S5_HALF_U)

    bm = jnp.concatenate([in_mat(bb_re), in_mat(bb_im)], axis=-1).astype(BF16)
    cm = jnp.concatenate([out_mat(c_re), -out_mat(c_im)], axis=-2).astype(BF16)
    t = jnp.arange(S5_BLK, dtype=F32)
    order = jnp.stack([t, S5_BLK - 1.0 - t])
    shifts = 2.0 ** jnp.arange(S5_TABS - 1, dtype=F32)
    expo = jnp.concatenate([jnp.where(order[:, None, :] >= shifts[None, :, None], shifts[None, :, None], jnp.nan),
                            order[:, None, :] + 1.0], axis=1)
    live = ~jnp.isnan(expo)
    e = jnp.where(live, expo, 0.0)[..., None]
    adt_re = (a_re * dt).reshape(DIRS, 1, 1, S5_X)
    adt_im = (a_im * dt).reshape(DIRS, 1, 1, S5_X)
    pmag = jnp.where(live[..., None], jnp.exp(e * adt_re), 0.0)
    pw_re = pmag * jnp.cos(e * adt_im)
    pw_im = pmag * jnp.sin(e * adt_im)
    return bm, cm, pw_re, pw_im


def _gelu_tanh(x):
    return 0.5 * x * (1.0 + jnp.tanh(math.sqrt(2.0 / math.pi) * (x + 0.044715 * (x * x * x))))


def _s5_body(*refs, seq_len, has_s0):
    it = iter(refs)
    u_ref, bm_ref, cm_ref, pwr_ref, pwi_ref, d_ref, gw_ref, gb_ref = (next(it) for _ in range(8))
    x0r_ref, x0i_ref = (next(it), next(it)) if has_s0 else (None, None)
    o_ref, sfr_ref, sfi_ref, y_ref, xr_ref, xi_ref, xb_ref = (next(it) for _ in range(7))
    tc = min(seq_len, S5_TC)
    n_tiles = seq_len // tc
    n_pair = tc // (2 * S5_BLK)

    y_ref[...] = d_ref[...] * u_ref[...]
    unit = 0
    for d in range(DIRS):
        last = S5_BLK - 1 if d == 0 else 0
        for hf in range(2):
            xs = slice(hf * S5_HALF_X, (hf + 1) * S5_HALF_X)
            us = slice(hf * S5_HALF_U, (hf + 1) * S5_HALF_U)

            def scan_block(x_in, carry, d=d, xs=xs, last=last):
                xr, xi = x_in
                car_re, car_im = carry
                for k in range(S5_TABS - 1):
                    s = 1 << k
                    shift = s if d == 0 else S5_BLK - s
                    p_re, p_im = pwr_ref[d, k, :, xs], pwi_ref[d, k, :, xs]
                    sr, si = pltpu.roll(xr, shift, 0), pltpu.roll(xi, shift, 0)
                    xr, xi = xr + p_re * sr - p_im * si, xi + p_re * si + p_im * sr
                p_re, p_im = pwr_ref[d, S5_TABS - 1, :, xs], pwi_ref[d, S5_TABS - 1, :, xs]
                xr, xi = xr + p_re * car_re - p_im * car_im, xi + p_re * car_im + p_im * car_re
                return (xr, xi), (xr[last:last + 1, :], xi[last:last + 1, :])

            if has_s0:
                carry = (x0r_ref[d, :, xs], x0i_ref[d, :, xs])
            else:
                carry = (jnp.zeros((1, S5_HALF_X), F32), jnp.zeros((1, S5_HALF_X), F32))
            for i in range(n_tiles):
                slot = unit % 2
                unit += 1
                tile = i if d == 0 else n_tiles - 1 - i
                rows_t = slice(tile * tc, (tile + 1) * tc)
                bu = _dot(u_ref[rows_t, us], bm_ref[d, hf])
                xr_ref[slot] = bu[:, :S5_HALF_X]
                xi_ref[slot] = bu[:, S5_HALF_X:]
                for j in range(n_pair):
                    pair = j if d == 0 else n_pair - 1 - j
                    rows = slice(pair * 2 * S5_BLK, (pair + 1) * 2 * S5_BLK)
                    xr2, xi2 = xr_ref[slot, rows, :], xi_ref[slot, rows, :]
                    halves = [(xr2[:S5_BLK], xi2[:S5_BLK]), (xr2[S5_BLK:], xi2[S5_BLK:])]
                    out = [None, None]
                    for idx in ((0, 1) if d == 0 else (1, 0)):
                        out[idx], carry = scan_block(halves[idx], carry)
                    xb_ref[slot, rows, :S5_HALF_X] = jnp.concatenate([out[0][0], out[1][0]], axis=0).astype(BF16)
                    xb_ref[slot, rows, S5_HALF_X:] = jnp.concatenate([out[0][1], out[1][1]], axis=0).astype(BF16)
                y_ref[rows_t, us] += jnp.dot(xb_ref[slot], cm_ref[d, hf], preferred_element_type=F32)
            sfr_ref[d, :, xs] = carry[0]
            sfi_ref[d, :, xs] = carry[1]

    z = _gelu_tanh(y_ref[...])
    o_ref[...] = (z * jax.nn.sigmoid(_dot(z, gw_ref[...]) + gb_ref[...])).astype(BF16)


def _s5_pallas(pm, tables, s5_d, glu_w, glu_b, n_seq, seq_len, row_off, x0, layer):
    bm, cm, pw_re, pw_im = tables
    has_s0 = x0 is not None
    full = lambda shape: pl.BlockSpec(shape, lambda b: (0,) * len(shape))
    of_layer = lambda t: pl.BlockSpec((None,) + t.shape[1:], lambda b: (layer,) + (0,) * (t.ndim - 1))
    in_specs = [_seq_spec(seq_len, GROUP_WIDTH, row_off, 13),
                of_layer(bm), of_layer(cm), of_layer(pw_re), of_layer(pw_im),
                full((1, GROUP_WIDTH)), full((GROUP_WIDTH, GROUP_WIDTH)), full((1, GROUP_WIDTH))]
    args = [pm, bm, cm, pw_re, pw_im, s5_d.reshape(1, GROUP_WIDTH), glu_w.astype(BF16),
            glu_b.reshape(1, GROUP_WIDTH)]
    if has_s0:
        in_specs += [pl.BlockSpec((None, None, DIRS, 1, S5_X), lambda b: (b, layer, 0, 0, 0))] * 2
        args += list(x0)
    st = jax.ShapeDtypeStruct((n_seq, DIRS, 1, S5_X), F32)
    st_spec = pl.BlockSpec((None, DIRS, 1, S5_X), lambda b: (b, 0, 0, 0))
    tc = min(seq_len, S5_TC)
    return pl.pallas_call(
        functools.partial(_s5_body, seq_len=seq_len, has_s0=has_s0),
        out_shape=(jax.ShapeDtypeStruct((n_seq * seq_len, GROUP_WIDTH), BF16), st, st),
        grid=(n_seq,),
        in_specs=in_specs,
        out_specs=(pl.BlockSpec((seq_len, GROUP_WIDTH), lambda b: (b, 0)), st_spec, st_spec),
        scratch_shapes=[pltpu.VMEM((seq_len, GROUP_WIDTH), F32),
                        pltpu.VMEM((2, tc, S5_HALF_X), F32), pltpu.VMEM((2, tc, S5_HALF_X), F32),
                        pltpu.VMEM((2, tc, 2 * S5_HALF_X), BF16)],
        compiler_params=_mix_params(1),
        name="s5",
    )(*args)


GLA_LEVELS = 7


def _chunk_cumsum_rows(x, rowi):
    s = 1
    while s < MIX_CHUNK:
        x = x + jnp.where(rowi >= s, pltpu.roll(x, s, 0), 0.0)
        s *= 2
    return x


def _gla_body(*refs, n_chunks, has_s0):
    it = iter(refs)
    q_ref, f0_ref, f1_ref, i_ref, g_ref, lb_ref, nw_ref = (next(it) for _ in range(7))
    s0_ref = next(it) if has_s0 else None
    o_ref, sf_ref, acc_ref, s_ref, code_ref = (next(it) for _ in range(5))
    C, H = MIX_CHUNK, N_HEADS
    f_refs = (f0_ref, f1_ref)

    rowi = lax.broadcasted_iota(jnp.int32, (C, HEAD_DIM), 0)
    ri = lax.broadcasted_iota(jnp.int32, (C, C), 0)
    ci = lax.broadcasted_iota(jnp.int32, (C, C), 1)
    top_bit = 31 - lax.clz(ri ^ ci)
    code_ref[...] = jnp.where(ri > ci, top_bit, jnp.where(ri < ci, -1 - top_bit, GLA_LEVELS))
    if has_s0:
        s_ref[...] = s0_ref[...]
    else:
        s_ref[...] = jnp.zeros_like(s_ref)
    acc_ref[...] = jnp.zeros_like(acc_ref)

    def chunk_step(n):
        for d in range(DIRS):
            c = n if d == 0 else n_chunks - 1 - n
            rows = slice(c * C, (c + 1) * C)
            for h in range(H):
                cols = slice(h * HEAD_DIM, (h + 1) * HEAD_DIM)
                code = code_ref[...]
                q = _silu(q_ref[rows, cols]) * HEAD_DIM ** -0.5
                v = i_ref[rows, cols].astype(BF16)
                fx = f_refs[d][rows, cols]
                lb = lb_ref[d:d + 1, cols]
                sig = 1.0 / (1.0 + jnp.exp(-fx))
                logf = jnp.log2(jnp.maximum(lb, LB_FLOOR) + (1.0 - lb) * sig)
                k = (1.0 - lb) * (1.0 - sig)
                cum = _chunk_cumsum_rows(logf, rowi)
                own = cum
                attn = jnp.where(code == GLA_LEVELS, _dot_nt(q, k), 0.0)
                for lvl in range(GLA_LEVELS):
                    m = 1 << lvl
                    prev = pltpu.roll(own, m, 0)
                    pre = jnp.minimum(cum - prev, 0.0)
                    suf = own - cum
                    if d == 0:
                        sc = _dot_nt(q * jnp.exp2(pre), k * jnp.exp2(suf))
                        hit = code == lvl
                    else:
                        sc = _dot_nt(q * jnp.exp2(suf + logf), k * jnp.exp2(jnp.minimum(pre - logf, 0.0)))
                        hit = code == -1 - lvl
                    attn = jnp.where(hit, sc, attn)
                    own = jnp.where(((rowi >> lvl) & 1) == 0, pltpu.roll(own, C - m, 0), own)
                tot = own
                st = s_ref[d, h]
                if d == 0:
                    q_in, k_out = q * jnp.exp2(cum), k * jnp.exp2(tot - cum)
                else:
                    q_in, k_out = q * jnp.exp2(tot - cum + logf), k * jnp.exp2(cum - logf)
                acc_ref[rows, cols] += _dot(attn, v) + _dot_nt(q_in, st)
                s_ref[d, h] = jnp.exp2(tot[0:1, :]) * st + _dot_tn(v, k_out)

    for n in range(n_chunks):
        chunk_step(n)
    sf_ref[...] = s_ref[...]

    for n in range(n_chunks):
        rows = slice(n * C, (n + 1) * C)
        for h in range(H):
            cols = slice(h * HEAD_DIM, (h + 1) * HEAD_DIM)
            o = acc_ref[rows, cols]
            y = o * lax.rsqrt(jnp.mean(o * o, axis=-1, keepdims=True) + EPS) * nw_ref[...]
            o_ref[rows, cols] = (y * _silu(g_ref[rows, cols])).astype(BF16)


def _gla_pallas(pm, lower_bound, norm_w, n_seq, seq_len, row_off, s0_t, layer):
    has_s0 = s0_t is not None
    st_shape = (DIRS, N_HEADS, HEAD_DIM, HEAD_DIM)
    in_specs = [_seq_spec(seq_len, GROUP_WIDTH, row_off, cb) for cb in (8, 9, 10, 11, 12)]
    in_specs += [pl.BlockSpec((DIRS, GROUP_WIDTH), lambda b: (0, 0)), pl.BlockSpec((1, HEAD_DIM), lambda b: (0, 0))]
    args = [pm] * 5 + [lower_bound, norm_w.reshape(1, HEAD_DIM)]
    if has_s0:
        in_specs.append(pl.BlockSpec((None, None) + st_shape, lambda b: (b, layer, 0, 0, 0, 0)))
        args.append(s0_t)
    return pl.pallas_call(
        functools.partial(_gla_body, n_chunks=seq_len // MIX_CHUNK, has_s0=has_s0),
        out_shape=(jax.ShapeDtypeStruct((n_seq * seq_len, GROUP_WIDTH), BF16),
                   jax.ShapeDtypeStruct((n_seq,) + st_shape, F32)),
        grid=(n_seq,),
        in_specs=in_specs,
        out_specs=(pl.BlockSpec((seq_len, GROUP_WIDTH), lambda b: (b, 0)),
                   pl.BlockSpec((None,) + st_shape, lambda b: (b, 0, 0, 0, 0))),
        scratch_shapes=[pltpu.VMEM((seq_len, GROUP_WIDTH), F32), pltpu.VMEM(st_shape, F32),
                        pltpu.VMEM((MIX_CHUNK, MIX_CHUNK), jnp.int32)],
        compiler_params=_mix_params(1),
        name="hgrn2",
    )(*args)


GDN_BLOCK_BITS = 4
GDN_MERGES = 3
N_GATES = DIRS * N_HEADS


def _softplus(x):
    return jnp.maximum(x, 0.0) + jnp.log1p(jnp.exp(-jnp.abs(x)))


def _gdn_body(*refs, seq_len, n_sub, has_s0):
    it = iter(refs)
    (q_ref, k_ref, v_ref, g_ref, ps_ref, pst_ref, cw_ref, prow_ref, pcol_ref, nw_ref) = (next(it) for _ in range(10))
    s0_ref = next(it) if has_s0 else None
    (o_ref, sf_ref, acc_ref, s_ref, qn_ref, kn_ref, vn_ref, gate_ref, gatet_ref, blev_ref) = (
        next(it) for _ in range(10))
    C, H, L = MIX_CHUNK, N_HEADS, seq_len
    R = n_sub * L
    n_chunks = L // C
    w = GROUP_WIDTH

    rowi = lax.broadcasted_iota(jnp.int32, (C, HEAD_DIM), 0)
    lane = lax.broadcasted_iota(jnp.int32, (C, HEAD_DIM), 1)
    lane_t = lax.broadcasted_iota(jnp.int32, (2 * N_GATES, C), 1)
    row_t = lax.broadcasted_iota(jnp.int32, (2 * N_GATES, C), 0)

    def prepare(c):
        rows = slice(c * C, (c + 1) * C)
        first, final = c % n_chunks == 0, c % n_chunks == n_chunks - 1
        for part, (src, dst) in enumerate(((q_ref, qn_ref), (k_ref, kn_ref), (v_ref, vn_ref))):
            for h in range(H):
                cols = slice(h * HEAD_DIM, (h + 1) * HEAD_DIM)
                wc = slice(part * w + h * HEAD_DIM, part * w + (h + 1) * HEAD_DIM)
                x = src[rows, cols]
                before = jnp.zeros((1, HEAD_DIM), F32) if first else src[c * C - 1:c * C, cols]
                after = jnp.zeros((1, HEAD_DIM), F32) if final else src[(c + 1) * C:(c + 1) * C + 1, cols]
                x_prev = jnp.where(rowi == 0, before, pltpu.roll(x, 1, 0))
                x_next = jnp.where(rowi == C - 1, after, pltpu.roll(x, C - 1, 0))
                y = _silu(cw_ref[0:1, wc] * x_prev + cw_ref[1:2, wc] * x + cw_ref[2:3, wc] * x_next)
                if part < 2:
                    y = y * lax.rsqrt(jnp.sum(y * y, axis=-1, keepdims=True) + EPS)
                if part == 0:
                    y = y * HEAD_DIM ** -0.5
                dst[rows, cols] = y

        a = ps_ref[rows, :]
        la = -jnp.exp(prow_ref[0:1, :]) * _softplus(a + prow_ref[1:2, :])
        pre = _chunk_cumsum_rows(la, rowi)
        tot = jnp.broadcast_to(pre[C - 1:C, :], (C, HEAD_DIM))
        g = jnp.where(lane < N_HEADS, pre, tot - pre + la)
        gate_ref[rows, :] = jnp.where(lane < N_GATES, g,
                                      jnp.where(lane < 2 * N_GATES, jax.nn.sigmoid(a),
                                                pltpu.roll(tot, 2 * N_GATES, 1)))
        at = pst_ref[:, rows]
        lat = -jnp.exp(pcol_ref[:, 0:1]) * _softplus(at + pcol_ref[:, 1:2])
        pre_t, s = lat, 1
        while s < C:
            pre_t = pre_t + jnp.where(lane_t >= s, pltpu.roll(pre_t, s, 1), 0.0)
            s *= 2
        tot_t = jnp.broadcast_to(pre_t[:, C - 1:C], (2 * N_GATES, C))
        g_t = jnp.where(row_t < N_HEADS, pre_t, tot_t - pre_t + lat)
        gatet_ref[:, rows] = jnp.where(row_t < N_GATES, g_t, jax.nn.sigmoid(at))

    ri = lax.broadcasted_iota(jnp.int32, (C, C), 0)
    ci = lax.broadcasted_iota(jnp.int32, (C, C), 1)
    bx = (ri >> GDN_BLOCK_BITS) ^ (ci >> GDN_BLOCK_BITS)
    blev_ref[...] = jnp.where(bx == 0, 0, 32 - lax.clz(bx))
    if has_s0:
        for s in range(n_sub):
            s_ref[s * DIRS:(s + 1) * DIRS] = s0_ref[s]
    else:
        s_ref[...] = jnp.zeros_like(s_ref)
    acc_ref[...] = jnp.zeros_like(acc_ref)

    units = [(s, d, h) for s in range(n_sub) for d in range(DIRS) for h in range(H)]

    def rows_of(n, s, d):
        c = n if d == 0 else n_chunks - 1 - n
        return slice(s * L + c * C, s * L + (c + 1) * C)

    def state_free_phases(step_units):
        eye = jnp.where(ri == ci, 1.0, 0.0)
        blev = blev_ref[...]
        incl_d = [ri >= ci, ri <= ci]
        strict_d = [ri > ci, ri < ci]
        q_l, et_l, ek_l, rhs_l, a_l, attn_l = ([] for _ in range(6))
        for n, s, d, h in step_units:
            rows, cols, cg = rows_of(n, s, d), slice(h * HEAD_DIM, (h + 1) * HEAD_DIM), d * N_HEADS + h
            q, k, v = qn_ref[rows, cols], kn_ref[rows, cols], vn_ref[rows, cols]
            g_i = jnp.broadcast_to(gate_ref[rows, cg:cg + 1], (C, C))
            b_i = jnp.broadcast_to(gate_ref[rows, N_GATES + cg:N_GATES + cg + 1], (C, C))
            tot = jnp.broadcast_to(gate_ref[rows, 2 * N_GATES + cg:2 * N_GATES + cg + 1], (C, C))
            g_j = gatet_ref[cg:cg + 1, rows]
            decay = jnp.where(incl_d[d], jnp.exp(jnp.minimum(g_i - g_j, 0.0)), 0.0)
            e_g = jnp.exp(g_i)
            kb = k.astype(BF16)
            a_l.append(jnp.where(strict_d[d], _dot_nt(kb, kb) * b_i * decay, 0.0))
            attn_l.append(_dot_nt(q, kb) * decay)
            rhs_l.append(jnp.concatenate([v * b_i, k * (b_i * e_g)], axis=1).astype(BF16))
            q_l.append(q * e_g)
            ek_l.append(k * jnp.exp(tot - g_i))
            et_l.append(jnp.exp(tot))
        b_l = [-jnp.where(blev == 0, a, 0.0) for a in a_l]
        p_l = [eye + b for b in b_l]
        for _ in range(GDN_BLOCK_BITS - 1):
            b_l = [_dot(b, b) for b in b_l]
            p_l = [p + _dot(p, b) for p, b in zip(p_l, b_l)]
        for lvl in range(1, GDN_MERGES + 1):
            ep_l = [_dot(jnp.where(blev == lvl, a, 0.0), p) for a, p in zip(a_l, p_l)]
            p_l = [p - _dot(p, ep) for p, ep in zip(p_l, ep_l)]
        uw_l = [_dot(p, rhs) for p, rhs in zip(p_l, rhs_l)]
        lhs_l = [jnp.concatenate([q, attn], axis=1).astype(BF16) for q, attn in zip(q_l, attn_l)]
        return list(zip(uw_l, lhs_l, ek_l, et_l))

    def state_phase(n, vals):
        s_l = [s_ref[s * DIRS + d, h] for s, d, h in units]
        vn_l = [uw[:, :HEAD_DIM] - _dot(uw[:, HEAD_DIM:], st) for (uw, _, _, _), st in zip(vals, s_l)]
        for i, (s, d, h) in enumerate(units):
            cols = slice(h * HEAD_DIM, (h + 1) * HEAD_DIM)
            _, lhs, ek, et = vals[i]
            acc_ref[rows_of(n, s, d), cols] += _dot(lhs, jnp.concatenate([s_l[i], vn_l[i]], axis=0))
            s_ref[s * DIRS + d, h] = s_l[i] * et + _dot_tn(ek, vn_l[i])

    for c in range(R // C):
        prepare(c)
    group = max(1, GDN_GROUP_UNITS // len(units))
    for n0 in range(0, n_chunks, group):
        steps = range(n0, min(n0 + group, n_chunks))
        vals = state_free_phases([(n,) + u for n in steps for u in units])
        for j, n in enumerate(steps):
            state_phase(n, vals[j * len(units):(j + 1) * len(units)])
    sf_ref[...] = s_ref[...]

    for n in range(R // C):
        rows = slice(n * C, (n + 1) * C)
        for h in range(H):
            cols = slice(h * HEAD_DIM, (h + 1) * HEAD_DIM)
            o = acc_ref[rows, cols]
            y = o * lax.rsqrt(jnp.mean(o * o, axis=-1, keepdims=True) + EPS) * nw_ref[...]
            o_ref[rows, cols] = (y * _silu(g_ref[rows, cols])).astype(BF16)


GDN_GROUP_UNITS = 32
GDN_CTX_SUB = 2
GDN_LAT_SUB = 1
GDN_DOUBLE_BUFFER_BYTES = 2 * 1024 * 1024


def _gdn_pallas(pm, ps, ps_t, conv_w, a_log, dt_bias, norm_w, n_seq, seq_len, row_off, s0, layer):
    has_s0 = s0 is not None
    n_sub = GDN_LAT_SUB if has_s0 else GDN_CTX_SUB
    n_seq, seq_len, row_off = n_seq // n_sub, seq_len * n_sub, row_off // n_sub
    st_shape = (n_sub * DIRS, N_HEADS, HEAD_DIM, HEAD_DIM)
    par = jnp.stack([a_log.reshape(N_GATES), dt_bias.reshape(N_GATES)])
    par_row = jnp.pad(par, ((0, 0), (0, HEAD_DIM - N_GATES)))
    par_col = jnp.pad(par.T, ((0, N_GATES), (0, 0)))
    full = lambda shape: pl.BlockSpec(shape, lambda b: (0,) * len(shape))
    big_mode = pl.Buffered(1) if seq_len * GROUP_WIDTH * 4 > GDN_DOUBLE_BUFFER_BYTES else None
    in_specs = [_seq_spec(seq_len, GROUP_WIDTH, row_off, cb, big_mode) for cb in (4, 5, 6, 7)]
    in_specs += [_seq_spec(seq_len, PROJ_SMALL, row_off, 0),
                 pl.BlockSpec((2 * N_GATES, seq_len), lambda b: (0, row_off + b)),
                 full((GDN_CONV, 3 * GROUP_WIDTH)), full((2, HEAD_DIM)), full((2 * N_GATES, 2)), full((1, HEAD_DIM))]
    args = [pm] * 4 + [ps, ps_t, conv_w, par_row, par_col, norm_w.reshape(1, HEAD_DIM)]
    if has_s0:
        in_specs.append(pl.BlockSpec((n_sub, None, DIRS, N_HEADS, HEAD_DIM, HEAD_DIM),
                                     lambda b: (b, layer, 0, 0, 0, 0)))
        args.append(s0)
    seq_f32 = lambda width: pltpu.VMEM((seq_len, width), F32)
    out, s_fin = pl.pallas_call(
        functools.partial(_gdn_body, seq_len=seq_len // n_sub, n_sub=n_sub, has_s0=has_s0),
        out_shape=(jax.ShapeDtypeStruct((n_seq * seq_len, GROUP_WIDTH), BF16),
                   jax.ShapeDtypeStruct((n_seq,) + st_shape, F32)),
        grid=(n_seq,),
        in_specs=in_specs,
        out_specs=(pl.BlockSpec((seq_len, GROUP_WIDTH), lambda b: (b, 0)),
                   pl.BlockSpec((None,) + st_shape, lambda b: (b, 0, 0, 0, 0))),
        scratch_shapes=[seq_f32(GROUP_WIDTH), pltpu.VMEM(st_shape, F32),
                        seq_f32(GROUP_WIDTH), seq_f32(GROUP_WIDTH), seq_f32(GROUP_WIDTH),
                        seq_f32(HEAD_DIM), pltpu.VMEM((2 * N_GATES, seq_len), F32),
                        pltpu.VMEM((MIX_CHUNK, MIX_CHUNK), jnp.int32)],
        compiler_params=_mix_params(1),
        name="gated_delta",
    )(*args)
    return out, s_fin.reshape(n_seq * n_sub, DIRS, N_HEADS, HEAD_DIM, HEAD_DIM)


def kernel(x_prompt, x_sample, state_ret, state_gdn, state_hgrn, state_s5_re, state_s5_im, c, c_ctx, norm1_w, norm2_w, final_norm_w, ada_w, ada_b, in_proj, out_proj, ret_decay_logit, gdn_conv, gdn_a_log, gdn_dt_bias, gdn_norm_w, hg_lb_param, hg_norm_w, s5_a_re, s5_a_im, s5_b_re, s5_b_im, s5_c_re, s5_c_im, s5_log_step, s5_d, s5_glu_w, s5_glu_b, ffn_w1, ffn_w3, ffn_w2):
    lb_soft = jax.nn.softmax(hg_lb_param, axis=0)
    lower_bounds = jnp.cumsum(lb_soft, axis=0) - lb_soft[0]
    rope2 = _rope_tables(DEC_SEQ)

    cvec = jnp.zeros((N_SEQ_ROWS, D_MODEL), F32).at[0].set(c_ctx).at[1:1 + DEC_BATCH].set(c)
    mod_all = _ada(cvec, ada_w, ada_b).reshape(DEPTH, N_SEQ_ROWS, N_MOD, 1, D_MODEL)

    gate0 = 8 * GROUP_WIDTH
    w_in = in_proj.astype(BF16)
    w_main = jnp.concatenate([w_in[:, :, :gate0], w_in[:, :, gate0 + 2 * N_GATES:]], axis=-1)
    w_small = jnp.pad(w_in[:, :, gate0:gate0 + 2 * N_GATES], ((0, 0), (0, 0), (0, PROJ_SMALL - 2 * N_GATES)))
    w_out = out_proj.astype(BF16)
    w1, w3, w2 = ffn_w1.astype(BF16), ffn_w3.astype(BF16), ffn_w2.astype(BF16)

    x = jnp.concatenate([x_prompt.reshape(N_CTX_TOK, D_MODEL), x_sample.reshape(N_LAT_TOK, D_MODEL)], axis=0)
    hgrn_t = jnp.swapaxes(state_hgrn, -1, -2)
    s5_x0 = (state_s5_re.reshape(DEC_BATCH, DEPTH, DIRS, 1, S5_X), state_s5_im.reshape(DEC_BATCH, DEPTH, DIRS, 1, S5_X))
    lat_off = N_CTX_TOK // DEC_SEQ
    tables = jax.vmap(_s5_tables)(s5_a_re, s5_a_im, s5_log_step, s5_b_re, s5_b_im, s5_c_re, s5_c_im)
    ctx_states = []
    for i in range(DEPTH):
        mod = mod_all[i]
        pm, ps = _inproj(x, norm1_w[i][None], mod, w_main, w_small, i)
        ps_t = ps[:, :2 * N_GATES].T
        lg = jax.nn.log_sigmoid(ret_decay_logit[i])
        ctx = (BATCH, SEQ, 0)
        lat = (DEC_BATCH, DEC_SEQ, lat_off)

        ret_c, rs = _retention_pallas(pm, lg, *ctx, None, None, i)
        ret_l, _ = _retention_pallas(pm, lg, *lat, rope2, state_ret, i)
        gdn_args = (pm, ps, ps_t, gdn_conv[i], gdn_a_log[i], gdn_dt_bias[i], gdn_norm_w[i])
        gdn_c, gs = _gdn_pallas(*gdn_args, *ctx, None, i)
        gdn_l, _ = _gdn_pallas(*gdn_args, *lat, state_gdn, i)
        hg_c, hs = _gla_pallas(pm, lower_bounds[i], hg_norm_w[i], *ctx, None, i)
        hg_l, _ = _gla_pallas(pm, lower_bounds[i], hg_norm_w[i], *lat, hgrn_t, i)
        s5_args = (pm, tables, s5_d[i], s5_glu_w[i], s5_glu_b[i])
        s5_c, xr, xi = _s5_pallas(*s5_args, *ctx, None, i)
        s5_l, _, _ = _s5_pallas(*s5_args, *lat, s5_x0, i)
        ctx_states.append((rs, gs, jnp.swapaxes(hs, -1, -2), xr.reshape(BATCH, DIRS, S5_GROUPS, S5_N),
                           xi.reshape(BATCH, DIRS, S5_GROUPS, S5_N)))

        parts = ((ret_c, ret_l), (gdn_c, gdn_l), (hg_c, hg_l), (s5_c, s5_l))
        x = _outproj(parts, w_out, x, mod, i)
        x = _ffn(x, norm2_w[i][None], mod, w1, w3, w2, final_norm_w[None], i)

    y_prompt = x[0].reshape(BATCH, SEQ, D_MODEL)
    y_sample = x[1].reshape(DEC_BATCH, DEC_SEQ, D_MODEL)
    new_states = tuple(jnp.stack([s[j] for s in ctx_states], axis=1) for j in range(5))
    return (y_prompt, y_sample) + new_states
```
